```python
import jax, jax.numpy as jnp
from jax import lax
import numpy as np

D_MODEL = 1024
BATCH = 16
SEQ = 256
DEPTH = 2
DEC_BATCH = 8
DEC_SEQ = 1024
PAST_LEN = 256

GRID_W = 64
NA_HEADS = 8
NA_HEAD_DIM = 64
NA_WIDTH = NA_HEADS * NA_HEAD_DIM
WIN_H = 8
WIN_W = 16
Q_BLOCK = 16
SUP_W = 32
FOURIER_GROUPS = 4
FOURIER_GROUP_DIM = 128
FOURIER_WIDTH = FOURIER_GROUPS * FOURIER_GROUP_DIM
RET_HEADS = 4
RET_KEY_DIM = 128
RET_VAL_DIM = 128
RET_WIDTH = RET_HEADS * RET_KEY_DIM
RET_CHUNK = 128
ROPE_BASE = 10000.0
N_BRANCHES = 3
BRANCH_WIDTH = 512
IN_SIZES = (NA_WIDTH, NA_WIDTH, NA_WIDTH, FOURIER_WIDTH, RET_WIDTH, RET_WIDTH, RET_HEADS * RET_VAL_DIM, RET_HEADS * RET_VAL_DIM, D_MODEL, D_MODEL, D_MODEL)
IN_WIDTH = sum(IN_SIZES)
N_EXPERTS = 32
TOP_K = 4
D_FF = 1024
SWIGLU_LIMIT = 7.0
SWIGLU_ALPHA = 1.702
MOE_BLOCK = 128
EPS = 1e-6
NEG_INF = -1e30

kernel_name = "hybrid_diffusion_na_fnet_retnet_moe_step"


def _rmsnorm(x, g):
    x32 = x.astype(jnp.float32)
    y = x32 * lax.rsqrt(jnp.mean(x32 * x32, axis=-1, keepdims=True) + EPS) * g.astype(jnp.float32)
    return y.astype(x.dtype)


def _modulation(cvec, w_mod, b_mod):
    m = jax.nn.silu(cvec) @ w_mod + b_mod
    return m.reshape(cvec.shape[0], 1, 6, D_MODEL)


def _split_in(z):
    return jnp.split(z, list(np.cumsum(IN_SIZES)[:-1]), axis=-1)


def _heads(z, n_heads, d):
    return z.reshape(z.shape[0], z.shape[1], n_heads, d)


def _axial_rotary(x):
    t, d = x.shape[1], x.shape[-1]
    pos = jnp.arange(t)
    row = (pos // GRID_W).astype(jnp.float32)
    col = (pos % GRID_W).astype(jnp.float32)
    half = d // 2
    nf = half // 2
    inv_freq = ROPE_BASE ** (-jnp.arange(nf, dtype=jnp.float32) / nf)

    def rotate(xp, p):
        ang = p[:, None] * inv_freq[None]
        cos = jnp.cos(ang)[None, :, None, :]
        sin = jnp.sin(ang)[None, :, None, :]
        x1 = xp[..., :nf].astype(jnp.float32)
        x2 = xp[..., nf:].astype(jnp.float32)
        return jnp.concatenate([x1 * cos - x2 * sin, x1 * sin + x2 * cos], axis=-1)

    return jnp.concatenate([rotate(x[..., :half], row), rotate(x[..., half:], col)], axis=-1).astype(x.dtype)


def _neighbourhood_index(t):
    rows = t // GRID_W
    kh = min(WIN_H, rows)
    blocks_per_row = GRID_W // Q_BLOCK
    nb = t // Q_BLOCK
    blk = np.arange(nb)
    r = blk // blocks_per_row
    c0 = (blk % blocks_per_row) * Q_BLOCK
    rs = np.clip(r - kh // 2, 0, rows - kh)
    cs = np.clip(c0 - WIN_W // 2, 0, GRID_W - SUP_W)
    key_r = rs[:, None] + np.repeat(np.arange(kh), SUP_W)[None]
    key_c = cs[:, None] + np.tile(np.arange(SUP_W), kh)[None]
    key_idx = key_r * GRID_W + key_c
    q_c = c0[:, None] + np.arange(Q_BLOCK)[None]
    q_cs = np.clip(q_c - WIN_W // 2, 0, GRID_W - WIN_W)
    in_win = (key_c[:, None, :] >= q_cs[:, :, None]) & (key_c[:, None, :] < q_cs[:, :, None] + WIN_W)
    dr = key_r[:, None, :] - r[:, None, None]
    dc = key_c[:, None, :] - q_c[:, :, None]
    bias_idx = (dr + WIN_H - 1) * (2 * WIN_W - 1) + np.clip(dc + WIN_W - 1, 0, 2 * WIN_W - 2)
    bias_idx = np.broadcast_to(bias_idx, in_win.shape)
    return key_idx, in_win, bias_idx


def _context_attention(q, k, v):
    s = jnp.einsum("bqhd,bkhd->bhqk", q, k).astype(jnp.float32) * (NA_HEAD_DIM ** -0.5)
    p = jax.nn.softmax(s, axis=-1).astype(v.dtype)
    o = jnp.einsum("bhqk,bkhd->bqhd", p, v)
    return o.reshape(q.shape[0], q.shape[1], NA_WIDTH)


def _neighbourhood_attention(q, k, v, k_ctx, v_ctx, rpb):
    n, t, h, dh = q.shape
    key_idx, in_win, bias_idx = _neighbourhood_index(t)
    nb = t // Q_BLOCK
    n_lat = key_idx.shape[1]
    qb = q.reshape(n, nb, Q_BLOCK, h, dh)
    kg = k[:, key_idx]
    vg = v[:, key_idx]
    bias = rpb.reshape(h, -1)[:, bias_idx].astype(jnp.float32)
    scale = dh ** -0.5
    s_lat = jnp.einsum("bnqhd,bnkhd->bhnqk", qb, kg).astype(jnp.float32) * scale + bias[None]
    s_lat = jnp.where(in_win[None, None], s_lat, NEG_INF)
    s_ctx = jnp.einsum("bnqhd,blhd->bhnql", qb, k_ctx).astype(jnp.float32) * scale
    p = jax.nn.softmax(jnp.concatenate([s_lat, s_ctx], axis=-1), axis=-1).astype(v.dtype)
    o = (jnp.einsum("bhnqk,bnkhd->bnqhd", p[..., :n_lat], vg)
         + jnp.einsum("bhnql,blhd->bnqhd", p[..., n_lat:], v_ctx))
    return o.reshape(n, t, h * dh)


def _fourier_mix(u):
    n, t, _ = u.shape
    u32 = u.astype(jnp.float32).reshape(n, t, FOURIER_GROUPS, FOURIER_GROUP_DIM)
    f = jnp.fft.fftn(u32, axes=(1, 3), norm="ortho").real
    return f.reshape(n, t, FOURIER_WIDTH).astype(u.dtype)


def _retention_chunkwise(q, k, v, log_gamma, s0):
    n, t, h, dk = q.shape
    dv = v.shape[-1]
    c = RET_CHUNK
    nc = t // c
    qc = q.reshape(n, nc, c, h, dk)
    kc = (k * (dk ** -0.5)).reshape(n, nc, c, h, dk)
    vc = v.reshape(n, nc, c, h, dv)
    pos = jnp.arange(c, dtype=jnp.float32)
    diff = pos[:, None] - pos[None, :]
    inner_decay = jnp.where(diff[None] >= 0, jnp.exp(jnp.maximum(diff, 0.0)[None] * log_gamma[:, None, None]), 0.0)
    scores = jnp.einsum("bnqhd,bnkhd->bnhqk", qc, kc) * inner_decay[None, None]
    inner = jnp.einsum("bnhqk,bnkhe->bnqhe", scores, vc)
    k_dec = jnp.exp((c - 1 - pos)[:, None] * log_gamma[None])
    q_dec = jnp.exp((pos + 1)[:, None] * log_gamma[None])
    kv = jnp.einsum("bnkhd,bnkhe->nbhde", kc * k_dec[None, None, :, :, None], vc)
    chunk_decay = jnp.exp(c * log_gamma)[:, None, None]

    def step(s, kv_n):
        return chunk_decay * s + kv_n, s

    s_final, s_prev = lax.scan(step, s0, kv)
    cross = jnp.einsum("bnqhd,nbhde->bnqhe", qc * q_dec[None, None, :, :, None], s_prev)
    return (inner + cross).reshape(n, t, h, dv), s_final


def _retention_bidir(q, k, v, lg_f, lg_b, s_f0, s_b0):
    q32, k32, v32 = q.astype(jnp.float32), k.astype(jnp.float32), v.astype(jnp.float32)
    y_f, s_f = _retention_chunkwise(q32, k32, v32, lg_f, s_f0)
    y_b_rev, s_b = _retention_chunkwise(jnp.flip(q32, 1), jnp.flip(k32, 1), jnp.flip(v32, 1), lg_b, s_b0)
    return y_f + jnp.flip(y_b_rev, 1), s_f, s_b


def _retention_output(y, g):
    mean = jnp.mean(y, axis=-1, keepdims=True)
    var = jnp.mean(jnp.square(y - mean), axis=-1, keepdims=True)
    yn = ((y - mean) * lax.rsqrt(var + EPS)).reshape(g.shape)
    return jax.nn.silu(g) * yn.astype(g.dtype)


def _mixer_inputs(x, mod, g_pre, w_in):
    h = _rmsnorm(x, g_pre) * (1 + mod[..., 1, :]) + mod[..., 0, :]
    return _split_in(h @ w_in)


def _finish_mixer(x, mod, a_out, f_out, r_out, g_a, g_f, g_r, w_branch, w_out, g_post):
    merged = (jax.nn.sigmoid(g_a) * (a_out @ w_branch[0])
              + jax.nn.sigmoid(g_f) * (f_out @ w_branch[1])
              + jax.nn.sigmoid(g_r) * (r_out @ w_branch[2]))
    y = merged @ w_out
    return x + mod[..., 2, :] * _rmsnorm(y, g_post)


def _moe(h, w_router, b_router, w1, b1, w2, b2):
    n, d = h.shape
    logits = h.astype(jnp.float32) @ w_router.astype(jnp.float32) + b_router.astype(jnp.float32)
    top_val, top_idx = lax.top_k(logits, TOP_K)
    weights = jax.nn.softmax(top_val, axis=-1)
    nk = n * TOP_K
    flat_e = top_idx.reshape(nk)
    order = jnp.argsort(flat_e)
    sorted_e = flat_e[order]
    sorted_tok = order // TOP_K
    counts = jnp.bincount(flat_e, length=N_EXPERTS)
    padded = (counts + MOE_BLOCK - 1) // MOE_BLOCK * MOE_BLOCK
    start = jnp.cumsum(counts) - counts
    pend = jnp.cumsum(padded)
    pstart = pend - padded
    dest = pstart[sorted_e] + jnp.arange(nk) - start[sorted_e]
    n_rows = -(-(nk + N_EXPERTS * (MOE_BLOCK - 1)) // MOE_BLOCK) * MOE_BLOCK
    n_blocks = n_rows // MOE_BLOCK
    row_tok = jnp.full((n_rows,), n, jnp.int32).at[dest].set(sorted_tok.astype(jnp.int32))
    block_expert = jnp.minimum(jnp.searchsorted(pend, jnp.arange(n_blocks) * MOE_BLOCK, side="right"), N_EXPERTS - 1)
    h_pad = jnp.concatenate([h, jnp.zeros((1, d), h.dtype)], axis=0)
    xb = h_pad[row_tok].reshape(n_blocks, MOE_BLOCK, d)

    def expert_block(args):
        xblk, e = args
        z = xblk @ w1[e] + b1[e]
        glu = jnp.minimum(z[:, :D_FF], SWIGLU_LIMIT)
        lin = jnp.clip(z[:, D_FF:], -SWIGLU_LIMIT, SWIGLU_LIMIT)
        a = glu * jax.nn.sigmoid(SWIGLU_ALPHA * glu) * (lin + 1)
        return a @ w2[e] + b2[e]

    yb = lax.map(expert_block, (xb, block_expert)).reshape(n_rows, d)
    row_of_assign = jnp.zeros((nk,), jnp.int32).at[order].set(dest.astype(jnp.int32))
    y = yb[row_of_assign].reshape(n, TOP_K, d)
    return jnp.einsum("nkd,nk->nd", y, weights.astype(y.dtype))


def _ffn_sublayer(x, mod, lp):
    h = _rmsnorm(x, lp["g_pre_ffn"]) * (1 + mod[..., 4, :]) + mod[..., 3, :]
    n, t, d = h.shape
    y = _moe(h.reshape(n * t, d), lp["w_router"], lp["b_router"], lp["w1"], lp["b1"], lp["w2"], lp["b2"]).reshape(n, t, d)
    return x + mod[..., 5, :] * _rmsnorm(y, lp["g_post_ffn"])


def _context_layer(x, mod, lp):
    n = x.shape[0]
    qa, ka, va, uf, qr, kr, vr, gr, g_a, g_f, g_r = _mixer_inputs(x, mod, lp["g_pre_mix"], lp["w_in"])
    qa, ka, va = _heads(qa, NA_HEADS, NA_HEAD_DIM), _heads(ka, NA_HEADS, NA_HEAD_DIM), _heads(va, NA_HEADS, NA_HEAD_DIM)
    a_out = _context_attention(qa, ka, va)
    f_out = _fourier_mix(uf)
    zero = jnp.zeros((n, RET_HEADS, RET_KEY_DIM, RET_VAL_DIM), jnp.float32)
    y_r, s_f, s_b = _retention_bidir(_heads(qr, RET_HEADS, RET_KEY_DIM), _heads(kr, RET_HEADS, RET_KEY_DIM),
                                     _heads(vr, RET_HEADS, RET_VAL_DIM), lp["lg_f"], lp["lg_b"], zero, zero)
    r_out = _retention_output(y_r, gr)
    x = _finish_mixer(x, mod, a_out, f_out, r_out, g_a, g_f, g_r, lp["w_branch"], lp["w_out"], lp["g_post_mix"])
    x = _ffn_sublayer(x, mod, lp)
    return x, ka, va, s_f, s_b


def _latent_layer(x, mod, lp, k_ctx, v_ctx, s_f_ctx, s_b_ctx):
    qa, ka, va, uf, qr, kr, vr, gr, g_a, g_f, g_r = _mixer_inputs(x, mod, lp["g_pre_mix"], lp["w_in"])
    qa, ka, va = _heads(qa, NA_HEADS, NA_HEAD_DIM), _heads(ka, NA_HEADS, NA_HEAD_DIM), _heads(va, NA_HEADS, NA_HEAD_DIM)
    a_out = _neighbourhood_attention(qa, ka, va, k_ctx.astype(qa.dtype), v_ctx.astype(va.dtype), lp["rpb"])
    f_out = _fourier_mix(uf)
    qr = _axial_rotary(_heads(qr, RET_HEADS, RET_KEY_DIM))
    kr = _axial_rotary(_heads(kr, RET_HEADS, RET_KEY_DIM))
    y_r, _, _ = _retention_bidir(qr, kr, _heads(vr, RET_HEADS, RET_VAL_DIM), lp["lg_f"], lp["lg_b"],
                                 s_f_ctx.astype(jnp.float32), s_b_ctx.astype(jnp.float32))
    r_out = _retention_output(y_r, gr)
    x = _finish_mixer(x, mod, a_out, f_out, r_out, g_a, g_f, g_r, lp["w_branch"], lp["w_out"], lp["g_post_mix"])
    return _ffn_sublayer(x, mod, lp)


def setup_inputs(seed: int = 0) -> dict:
    key = jax.random.key(seed)
    ks = jax.random.split(key, 26)
    f32 = jnp.float32

    def nrm(k, shape, s):
        return s * jax.random.normal(k, shape, f32)

    decay_base = jnp.asarray(np.log(2.0 ** (5 + np.arange(RET_HEADS)) - 1.0), f32)
    return {
        "x_prompt": nrm(ks[0], (BATCH, SEQ, D_MODEL), 1.0),
        "x_sample": nrm(ks[1], (DEC_BATCH, DEC_SEQ, D_MODEL), 1.0),
        "cache_k": nrm(ks[2], (DEC_BATCH, DEPTH, PAST_LEN, NA_HEADS, NA_HEAD_DIM), 1.0),
        "cache_v": nrm(ks[3], (DEC_BATCH, DEPTH, PAST_LEN, NA_HEADS, NA_HEAD_DIM), 1.0),
        "state_ret_fwd": nrm(ks[4], (DEC_BATCH, DEPTH, RET_HEADS, RET_KEY_DIM, RET_VAL_DIM), 0.1),
        "state_ret_bwd": nrm(ks[5], (DEC_BATCH, DEPTH, RET_HEADS, RET_KEY_DIM, RET_VAL_DIM), 0.1),
        "c": nrm(ks[6], (DEC_BATCH, D_MODEL), 1.0),
        "c_ctx": nrm(ks[7], (D_MODEL,), 1.0),
        "w_mod": nrm(ks[8], (DEPTH, D_MODEL, 6 * D_MODEL), 0.5 * D_MODEL ** -0.5),
        "b_mod": nrm(ks[9], (DEPTH, 6 * D_MODEL), 0.01),
        "g_pre_mix": 1.0 + nrm(ks[10], (DEPTH, D_MODEL), 0.01),
        "g_post_mix": 1.0 + nrm(ks[11], (DEPTH, D_MODEL), 0.01),
        "g_pre_ffn": 1.0 + nrm(ks[12], (DEPTH, D_MODEL), 0.01),
        "g_post_ffn": 1.0 + nrm(ks[13], (DEPTH, D_MODEL), 0.01),
        "w_in": nrm(ks[14], (DEPTH, D_MODEL, IN_WIDTH), D_MODEL ** -0.5),
        "na_rel_bias": nrm(ks[15], (DEPTH, NA_HEADS, 2 * WIN_H - 1, 2 * WIN_W - 1), 0.02),
        "ret_decay_fwd": decay_base + nrm(ks[16], (DEPTH, RET_HEADS), 0.01),
        "ret_decay_bwd": decay_base + nrm(ks[17], (DEPTH, RET_HEADS), 0.01),
        "w_branch": nrm(ks[18], (DEPTH, N_BRANCHES, BRANCH_WIDTH, D_MODEL), BRANCH_WIDTH ** -0.5),
        "w_out": nrm(ks[19], (DEPTH, D_MODEL, D_MODEL), D_MODEL ** -0.5),
        "w_router": nrm(ks[20], (DEPTH, D_MODEL, N_EXPERTS), D_MODEL ** -0.5),
        "b_router": nrm(ks[21], (DEPTH, N_EXPERTS), 0.01),
        "w_exp_in": nrm(ks[22], (DEPTH, N_EXPERTS, D_MODEL, 2 * D_FF), D_MODEL ** -0.5),
        "b_exp_in": nrm(ks[23], (DEPTH, N_EXPERTS, 2 * D_FF), 0.01),
        "w_exp_out": nrm(ks[24], (DEPTH, N_EXPERTS, D_FF, D_MODEL), D_FF ** -0.5),
        "b_exp_out": nrm(ks[25], (DEPTH, N_EXPERTS, D_MODEL), 0.01),
    }


def reference(x_prompt, x_sample, cache_k, cache_v, state_ret_fwd, state_ret_bwd, c, c_ctx,
              w_mod, b_mod, g_pre_mix, g_post_mix, g_pre_ffn, g_post_ffn, w_in, na_rel_bias,
              ret_decay_fwd, ret_decay_bwd, w_branch, w_out, w_router, b_router,
              w_exp_in, b_exp_in, w_exp_out, b_exp_out):
    layers = []
    for l in range(DEPTH):
        layers.append({
            "g_pre_mix": g_pre_mix[l], "g_post_mix": g_post_mix[l],
            "g_pre_ffn": g_pre_ffn[l], "g_post_ffn": g_post_ffn[l],
            "w_in": w_in[l], "rpb": na_rel_bias[l],
            "lg_f": jax.nn.log_sigmoid(ret_decay_fwd[l].astype(jnp.float32)),
            "lg_b": jax.nn.log_sigmoid(ret_decay_bwd[l].astype(jnp.float32)),
            "w_branch": w_branch[l], "w_out": w_out[l],
            "w_router": w_router[l], "b_router": b_router[l],
            "w1": w_exp_in[l], "b1": b_exp_in[l], "w2": w_exp_out[l], "b2": b_exp_out[l],
        })

    h = x_prompt
    ks_, vs_, sfs_, sbs_ = [], [], [], []
    for l in range(DEPTH):
        mod_ctx = _modulation(c_ctx[None], w_mod[l], b_mod[l])
        h, k_l, v_l, sf_l, sb_l = _context_layer(h, mod_ctx, layers[l])
        ks_.append(k_l)
        vs_.append(v_l)
        sfs_.append(sf_l)
        sbs_.append(sb_l)
    y_prompt = h

    h = x_sample
    for l in range(DEPTH):
        mod = _modulation(c, w_mod[l], b_mod[l])
        h = _latent_layer(h, mod, layers[l], cache_k[:, l], cache_v[:, l], state_ret_fwd[:, l], state_ret_bwd[:, l])
    y_sample = h

    new_cache_k = jnp.stack(ks_, axis=1)
    new_cache_v = jnp.stack(vs_, axis=1)
    new_state_ret_fwd = jnp.stack(sfs_, axis=1)
    new_state_ret_bwd = jnp.stack(sbs_, axis=1)
    return (y_prompt, y_sample, new_cache_k, new_cache_v, new_state_ret_fwd, new_state_ret_bwd)
```

```python
import functools

import numpy as np
import jax
import jax.numpy as jnp
from jax import lax
from jax.experimental import pallas as pl
from jax.experimental.pallas import tpu as pltpu

F32 = jnp.float32
BF16 = jnp.bfloat16
I32 = jnp.int32

D_MODEL = 1024
GRID_W = 64
NA_HEADS = 8
NA_HEAD_DIM = 64
NA_WIDTH = NA_HEADS * NA_HEAD_DIM
WIN_H = 8
WIN_W = 16
FOURIER_GROUPS = 4
FOURIER_GROUP_DIM = 128
RET_HEADS = 4
RET_KEY_DIM = 128
ROPE_BASE = 10000.0
BRANCH_WIDTH = 512
N_EXPERTS = 32
TOP_K = 4
D_FF = 1024
SWIGLU_LIMIT = 7.0
SWIGLU_ALPHA = 1.702
EPS = 1e-6
NEG_INF = -1e30

QKV_W = 3 * NA_WIDTH
MIX_W = 5 * BRANCH_WIDTH
GATE_W = 3 * D_MODEL

LANES = 128
MOD_ROWS = 16
ROW_TILE = 1024
FIN_TILE = 512
MOE_BLOCK = 256
DISPATCH_TILE = 512
COMBINE_TILE = 256
VMEM_LIMIT = 56 * 1024 * 1024


def _params(n_axes, vmem=VMEM_LIMIT):
    return pltpu.CompilerParams(dimension_semantics=("arbitrary",) * n_axes, vmem_limit_bytes=vmem)


def _rms(x):
    return lax.rsqrt(jnp.mean(x * x, axis=-1, keepdims=True) + EPS)


def _mod_kernel(cv_ref, w_ref, b_ref, o_ref):
    cv = cv_ref[...]
    s = (cv * jax.nn.sigmoid(cv)).astype(BF16)
    o_ref[...] = jnp.dot(s, w_ref[...].astype(BF16), preferred_element_type=F32) + b_ref[...]


def _modulation(cv, w_mod, b_mod):
    depth, d, n = w_mod.shape
    tn = 1536
    return pl.pallas_call(
        _mod_kernel,
        grid=(depth, n // tn),
        in_specs=[pl.BlockSpec((MOD_ROWS, d), lambda l, j: (0, 0)),
                  pl.BlockSpec((None, d, tn), lambda l, j: (l, 0, j)),
                  pl.BlockSpec((None, 1, tn), lambda l, j: (l, 0, j))],
        out_specs=pl.BlockSpec((None, MOD_ROWS, tn), lambda l, j: (l, 0, j)),
        out_shape=jax.ShapeDtypeStruct((depth, MOD_ROWS, n), F32),
        compiler_params=_params(2),
        name="modulation",
    )(cv, w_mod, b_mod.reshape(depth, 1, n))


def _prenorm_kernel(tmod_ref, x_ref, g_ref, mod_ref, o_ref):
    del tmod_ref
    x = x_ref[...]
    h = x * _rms(x) * g_ref[...]
    o_ref[...] = (h * (1.0 + mod_ref[1:2, :]) + mod_ref[0:1, :]).astype(o_ref.dtype)


def _prenorm(x, g, mod, tile_mod, tm):
    n, d = x.shape
    return pl.pallas_call(
        _prenorm_kernel,
        grid_spec=pltpu.PrefetchScalarGridSpec(
            num_scalar_prefetch=1, grid=(n // tm,),
            in_specs=[pl.BlockSpec((tm, d), lambda i, t: (i, 0)),
                      pl.BlockSpec((1, d), lambda i, t: (0, 0)),
                      pl.BlockSpec((None, 6, d), lambda i, t: (t[i], 0, 0))],
            out_specs=pl.BlockSpec((tm, d), lambda i, t: (i, 0))),
        out_shape=jax.ShapeDtypeStruct((n, d), BF16),
        compiler_params=_params(1),
        name="prenorm",
    )(tile_mod, x, g.reshape(1, d), mod)


def _proj_kernel(h_ref, w_ref, o_ref, wb_ref):
    @pl.when(pl.program_id(1) == 0)
    def _():
        wb_ref[...] = w_ref[...].astype(BF16)

    o_ref[...] = jnp.dot(h_ref[...], wb_ref[...], preferred_element_type=F32).astype(o_ref.dtype)


def _project(h, w, tn, out_dtype):
    n, d = h.shape
    width = w.shape[1]
    tm = ROW_TILE
    return pl.pallas_call(
        _proj_kernel,
        grid=(width // tn, n // tm),
        in_specs=[pl.BlockSpec((tm, d), lambda j, i: (i, 0)),
                  pl.BlockSpec((d, tn), lambda j, i: (0, j))],
        out_specs=pl.BlockSpec((tm, tn), lambda j, i: (i, j)),
        out_shape=jax.ShapeDtypeStruct((n, width), out_dtype),
        scratch_shapes=[pltpu.VMEM((d, tn), BF16)],
        compiler_params=_params(2),
        name="in_proj",
    )(h, w)


def _head_pair_masks():
    lane = lax.broadcasted_iota(I32, (1, LANES), 1)
    first = lane < NA_HEAD_DIM
    return first, jnp.logical_not(first)


def _attn_ctx_kernel(qkv_ref, o_ref):
    masks = _head_pair_masks()
    scale = NA_HEAD_DIM ** -0.5
    for p in range(NA_WIDTH // LANES):
        cols = slice(LANES * p, LANES * (p + 1))
        q2 = qkv_ref[:, cols] * scale
        k2 = qkv_ref[:, NA_WIDTH + LANES * p:NA_WIDTH + LANES * (p + 1)].astype(BF16)
        v2 = qkv_ref[:, 2 * NA_WIDTH + LANES * p:2 * NA_WIDTH + LANES * (p + 1)].astype(BF16)
        outs = []
        for m in masks:
            qa = jnp.where(m, q2, 0.0).astype(BF16)
            s = lax.dot_general(qa, k2, (((1,), (1,)), ((), ())), preferred_element_type=F32)
            e = jnp.exp(s - jnp.max(s, axis=-1, keepdims=True))
            den = jnp.sum(e, axis=-1, keepdims=True)
            outs.append(jnp.dot(e.astype(BF16), v2, preferred_element_type=F32) / den)
        o_ref[:, cols] = jnp.where(masks[0], outs[0], outs[1]).astype(o_ref.dtype)


def _attention_ctx(z_qkv, n_seq, seq):
    return pl.pallas_call(
        _attn_ctx_kernel,
        grid=(n_seq,),
        in_specs=[pl.BlockSpec((seq, QKV_W), lambda b: (b, 0))],
        out_specs=pl.BlockSpec((seq, NA_WIDTH), lambda b: (b, 0)),
        out_shape=jax.ShapeDtypeStruct((n_seq * seq, NA_WIDTH), BF16),
        compiler_params=_params(1),
        name="attn_ctx",
    )(z_qkv)


def _attn_lat_kernel(q_ref, k_ref, v_ref, kc_ref, vc_ref, bias_ref, o_ref, *, tq):
    masks = _head_pair_masks()
    scale = NA_HEAD_DIM ** -0.5
    k2 = k_ref[...].astype(BF16)
    v2 = v_ref[...].astype(BF16)
    kc = kc_ref[...].astype(BF16)
    vc = vc_ref[...].astype(BF16)
    nt = (((1,), (1,)), ((), ()))

    def q_tile(qi, carry):
        rows = pl.ds(pl.multiple_of(qi * tq, tq), tq)
        q2 = q_ref[rows, :] * scale
        outs = []
        for hh, m in enumerate(masks):
            qa = jnp.where(m, q2, 0.0).astype(BF16)
            s_lat = lax.dot_general(qa, k2, nt, preferred_element_type=F32) + bias_ref[hh, rows, :]
            s_ctx = lax.dot_general(qa, kc, nt, preferred_element_type=F32)
            mx = jnp.maximum(jnp.max(s_lat, axis=-1, keepdims=True), jnp.max(s_ctx, axis=-1, keepdims=True))
            e_lat = jnp.exp(s_lat - mx)
            e_ctx = jnp.exp(s_ctx - mx)
            den = jnp.sum(e_lat, axis=-1, keepdims=True) + jnp.sum(e_ctx, axis=-1, keepdims=True)
            o = (jnp.dot(e_lat.astype(BF16), v2, preferred_element_type=F32)
                 + jnp.dot(e_ctx.astype(BF16), vc, preferred_element_type=F32))
            outs.append(o / den)
        o_ref[rows, :] = jnp.where(masks[0], outs[0], outs[1]).astype(o_ref.dtype)
        return carry

    lax.fori_loop(0, q_ref.shape[0] // tq, q_tile, 0)


def _attention_lat(z_qkv, cache_k, cache_v, bias, layer, row0, n_seq, seq):
    past = cache_k.shape[2]
    pairs = NA_WIDTH // LANES
    rb = row0 // seq
    kv_cols = NA_WIDTH // LANES
    return pl.pallas_call(
        functools.partial(_attn_lat_kernel, tq=256),
        grid=(pairs, n_seq),
        in_specs=[pl.BlockSpec((seq, LANES), lambda p, b: (rb + b, p)),
                  pl.BlockSpec((seq, LANES), lambda p, b: (rb + b, kv_cols + p)),
                  pl.BlockSpec((seq, LANES), lambda p, b: (rb + b, 2 * kv_cols + p)),
                  pl.BlockSpec((None, None, past, LANES), lambda p, b: (b, layer, 0, p)),
                  pl.BlockSpec((None, None, past, LANES), lambda p, b: (b, layer, 0, p)),
                  pl.BlockSpec((2, seq, seq), lambda p, b: (p, 0, 0))],
        out_specs=pl.BlockSpec((seq, LANES), lambda p, b: (b, p)),
        out_shape=jax.ShapeDtypeStruct((n_seq * seq, NA_WIDTH), BF16),
        compiler_params=_params(2),
        name="attn_lat",
    )(z_qkv, z_qkv, z_qkv, cache_k, cache_v, bias)


def _neighbourhood_bias(rpb, seq):
    rows = seq // GRID_W
    kh = min(WIN_H, rows)
    c = np.arange(GRID_W)
    q_cs = np.clip(c - WIN_W // 2, 0, GRID_W - WIN_W)
    col_ok = (c[None, :] >= q_cs[:, None]) & (c[None, :] < q_cs[:, None] + WIN_W)
    r = np.arange(rows)
    rs = np.clip(r - kh // 2, 0, rows - kh)
    row_ok = (r[None, :] >= rs[:, None]) & (r[None, :] < rs[:, None] + kh)
    pad_c = GRID_W - WIN_W
    pc = jnp.pad(rpb, ((0, 0), (0, 0), (pad_c, pad_c)), mode="edge")
    w = jnp.stack([pc[:, :, GRID_W - 1 - cq:2 * GRID_W - 1 - cq] for cq in range(GRID_W)], axis=2)
    pad_r = rows - WIN_H + 1 if rows >= WIN_H else 0
    pr = jnp.pad(w, ((0, 0), (pad_r, pad_r), (0, 0), (0, 0)), mode="edge")
    off = WIN_H - 1 + pad_r
    b = jnp.stack([pr[:, off - rq:off - rq + rows] for rq in range(rows)], axis=1)
    b = b.transpose(0, 1, 3, 2, 4)
    mask = row_ok[:, None, :, None] & col_ok[None, :, None, :]
    b = jnp.where(jnp.asarray(mask)[None], b, NEG_INF)
    return b.reshape(rpb.shape[0], seq, seq)


def _fourier_kernel(u_ref, ct2_ref, cc_ref, sc_ref, o_ref, pq_ref):
    t = u_ref.shape[0]
    for g in range(FOURIER_GROUPS):
        cols = slice(FOURIER_GROUP_DIM * g, FOURIER_GROUP_DIM * (g + 1))
        ug = u_ref[:, cols]
        pq_ref[0:t, cols] = jnp.dot(ug, cc_ref[...], preferred_element_type=F32).astype(BF16)
        pq_ref[t:2 * t, cols] = jnp.dot(ug, sc_ref[...], preferred_element_type=F32).astype(BF16)
    o_ref[...] = jnp.dot(ct2_ref[...], pq_ref[...], preferred_element_type=F32).astype(o_ref.dtype)


def _dft_tables(t):
    def cs(n):
        k = np.arange(n, dtype=np.int64)
        ang = 2.0 * np.pi * ((k[:, None] * k[None, :]) % n).astype(np.float64) / n
        return np.cos(ang) / np.sqrt(n), np.sin(ang) / np.sqrt(n)

    ct, st = cs(t)
    cc, sc = cs(FOURIER_GROUP_DIM)
    ct2 = np.concatenate([ct, -st], axis=1).astype(np.float32)
    return (jnp.asarray(ct2).astype(BF16), jnp.asarray(cc.astype(np.float32)).astype(BF16),
            jnp.asarray(sc.astype(np.float32)).astype(BF16))


def _fourier(z_mix, row0, n_seq, seq):
    ct2, cc, sc = _dft_tables(seq)
    width = FOURIER_GROUPS * FOURIER_GROUP_DIM
    rb = row0 // seq
    return pl.pallas_call(
        _fourier_kernel,
        grid=(n_seq,),
        in_specs=[pl.BlockSpec((seq, width), lambda b: (rb + b, 0)),
                  pl.BlockSpec((seq, 2 * seq), lambda b: (0, 0)),
                  pl.BlockSpec((FOURIER_GROUP_DIM, FOURIER_GROUP_DIM), lambda b: (0, 0)),
                  pl.BlockSpec((FOURIER_GROUP_DIM, FOURIER_GROUP_DIM), lambda b: (0, 0))],
        out_specs=pl.BlockSpec((seq, width), lambda b: (b, 0)),
        out_shape=jax.ShapeDtypeStruct((n_seq * seq, width), BF16),
        scratch_shapes=[pltpu.VMEM((2 * seq, width), BF16)],
        compiler_params=_params(1),
        name="fourier",
    )(z_mix, ct2, cc, sc)


def _rotary_tables(t):
    pos = np.arange(t)
    row = (pos // GRID_W).astype(np.float64)
    col = (pos % GRID_W).astype(np.float64)
    nf = RET_KEY_DIM // 4
    inv_freq = ROPE_BASE ** (-np.arange(nf, dtype=np.float64) / nf)
    ar = row[:, None] * inv_freq[None]
    ac = col[:, None] * inv_freq[None]
    cos = np.concatenate([np.cos(ar), np.cos(ar), np.cos(ac), np.cos(ac)], axis=1)
    sin = np.concatenate([-np.sin(ar), np.sin(ar), -np.sin(ac), np.sin(ac)], axis=1)
    return jnp.asarray(cos.astype(np.float32)), jnp.asarray(sin.astype(np.float32))


def _ret_kernel(lgf_ref, lgb_ref, *refs, t, tq, rotary, state_in, state_out):
    refs = list(refs)
    q_ref, k_ref, v_ref, g_ref = refs[:4]
    refs = refs[4:]
    if rotary:
        cos_ref, sin_ref = refs[:2]
        refs = refs[2:]
    if state_in:
        sf0_ref, sb0_ref = refs[:2]
        refs = refs[2:]
    o_ref = refs[0]
    refs = refs[1:]
    if state_out:
        sf_ref, sb_ref = refs[:2]
        refs = refs[2:]
    dec_ref, kb_ref = refs

    h = pl.program_id(0)
    lgf = lgf_ref[h]
    lgb = lgb_ref[h]
    scale = RET_KEY_DIM ** -0.5
    nq = t // tq

    @pl.when(pl.program_id(1) == 0)
    def _():
        def fill(ri, c):
            rows = pl.ds(pl.multiple_of(ri * tq, tq), tq)
            i = lax.broadcasted_iota(I32, (tq, t), 0) + ri * tq
            j = lax.broadcasted_iota(I32, (tq, t), 1)
            d = (i - j).astype(F32)
            m = jnp.exp(jnp.abs(d) * jnp.where(d > 0, lgf, lgb))
            dec_ref[rows, :] = jnp.where(d == 0, 2.0, m)
            return c

        lax.fori_loop(0, nq, fill, 0)

    if rotary:
        lane = lax.broadcasted_iota(I32, (1, LANES), 1)
        low = (lane % (RET_KEY_DIM // 2)) < (RET_KEY_DIM // 4)

        def rot(x, rows):
            swapped = jnp.where(low, pltpu.roll(x, LANES - RET_KEY_DIM // 4, 1), pltpu.roll(x, RET_KEY_DIM // 4, 1))
            return x * cos_ref[rows, :] + swapped * sin_ref[rows, :]
    else:
        def rot(x, rows):
            return x

    all_rows = slice(0, t)
    kr = rot(k_ref[...].astype(F32), all_rows)
    kb_ref[...] = kr.astype(BF16)
    vb = v_ref[...]

    if state_out:
        j = lax.broadcasted_iota(I32, (t, 1), 0).astype(F32)
        tn = (((0,), (0,)), ((), ()))
        kf = (kr * (scale * jnp.exp(lgf * (t - 1.0 - j)))).astype(BF16)
        kbw = (kr * (scale * jnp.exp(lgb * j))).astype(BF16)
        sf = lax.dot_general(kf, vb, tn, preferred_element_type=F32)
        sb = lax.dot_general(kbw, vb, tn, preferred_element_type=F32)
        if state_in:
            sf = sf + jnp.exp(lgf * t) * sf0_ref[...]
            sb = sb + jnp.exp(lgb * t) * sb0_ref[...]
        sf_ref[...] = sf
        sb_ref[...] = sb

    def q_tile(qi, carry):
        r0 = pl.multiple_of(qi * tq, tq)
        rows = pl.ds(r0, tq)
        qr = rot(q_ref[rows, :].astype(F32), rows)
        s = lax.dot_general((qr * scale).astype(BF16), kb_ref[...], (((1,), (1,)), ((), ())),
                            preferred_element_type=F32)
        y = jnp.dot((s * dec_ref[rows, :]).astype(BF16), vb, preferred_element_type=F32)
        if state_in:
            pos = (lax.broadcasted_iota(I32, (tq, 1), 0) + r0).astype(F32)
            qf = (qr * jnp.exp(lgf * (pos + 1.0))).astype(BF16)
            qb = (qr * jnp.exp(lgb * (t - pos))).astype(BF16)
            y = (y + jnp.dot(qf, sf0_ref[...].astype(BF16), preferred_element_type=F32)
                 + jnp.dot(qb, sb0_ref[...].astype(BF16), preferred_element_type=F32))
        mean = jnp.mean(y, axis=-1, keepdims=True)
        yc = y - mean
        yn = yc * lax.rsqrt(jnp.mean(yc * yc, axis=-1, keepdims=True) + EPS)
        g = g_ref[rows, :].astype(F32)
        o_ref[rows, :] = (g * jax.nn.sigmoid(g) * yn).astype(o_ref.dtype)
        return carry

    lax.fori_loop(0, nq, q_tile, 0)


def _retention(z_mix, lg_f, lg_b, row0, n_seq, seq, *, rotary, states=None, layer=0, state_out=False):
    rb = row0 // seq
    cb = BRANCH_WIDTH // LANES
    tq = min(seq, 256)
    state_in = states is not None
    in_specs = [pl.BlockSpec((seq, LANES), lambda h, b, *_: (rb + b, 1 * cb + h)),
                pl.BlockSpec((seq, LANES), lambda h, b, *_: (rb + b, 2 * cb + h)),
                pl.BlockSpec((seq, LANES), lambda h, b, *_: (rb + b, 3 * cb + h)),
                pl.BlockSpec((seq, LANES), lambda h, b, *_: (rb + b, 4 * cb + h))]
    args = [z_mix, z_mix, z_mix, z_mix]
    if rotary:
        cos, sin = _rotary_tables(seq)
        in_specs += [pl.BlockSpec((seq, LANES), lambda h, b, *_: (0, 0))] * 2
        args += [cos, sin]
    if state_in:
        st_spec = pl.BlockSpec((None, None, None, RET_KEY_DIM, RET_KEY_DIM), lambda h, b, *_: (b, layer, h, 0, 0))
        in_specs += [st_spec, st_spec]
        args += list(states)
    out_specs = [pl.BlockSpec((seq, LANES), lambda h, b, *_: (b, h))]
    out_shape = [jax.ShapeDtypeStruct((n_seq * seq, RET_HEADS * LANES), BF16)]
    if state_out:
        so = pl.BlockSpec((None, None, RET_KEY_DIM, RET_KEY_DIM), lambda h, b, *_: (b, h, 0, 0))
        out_specs += [so, so]
        out_shape += [jax.ShapeDtypeStruct((n_seq, RET_HEADS, RET_KEY_DIM, RET_KEY_DIM), F32)] * 2
    return pl.pallas_call(
        functools.partial(_ret_kernel, t=seq, tq=tq, rotary=rotary, state_in=state_in, state_out=state_out),
        grid_spec=pltpu.PrefetchScalarGridSpec(
            num_scalar_prefetch=2, grid=(RET_HEADS, n_seq),
            in_specs=in_specs, out_specs=out_specs,
            scratch_shapes=[pltpu.VMEM((seq, seq), F32), pltpu.VMEM((seq, LANES), BF16)]),
        out_shape=out_shape,
        compiler_params=_params(2),
        name="retention",
    )(lg_f, lg_b, *args)


def _split_dot_nt(w, x):
    nt = (((1,), (1,)), ((), ()))
    w_hi = w.astype(BF16)
    w_lo = (w - w_hi.astype(F32)).astype(BF16)
    x_hi = x.astype(BF16)
    x_lo = (x - x_hi.astype(F32)).astype(BF16)
    return (lax.dot_general(w_hi, x_hi, nt, preferred_element_type=F32)
            + lax.dot_general(w_hi, x_lo, nt, preferred_element_type=F32)
            + lax.dot_general(w_lo, x_hi, nt, preferred_element_type=F32))


def _finish_kernel(tmod_ref, a_ref, f_ref, r_ref, zg_ref, x_ref, mod_ref, gpost_ref, gpre_ref,
                   wb_ref, wo_ref, wrt_ref, br_ref, tri_ref,
                   x1_ref, h2_ref, idx_ref, wts_ref, rank_ref, cnt_ref, carry_ref):
    del tmod_ref

    @pl.when(pl.program_id(0) == 0)
    def _():
        carry_ref[...] = jnp.zeros_like(carry_ref)

    d = D_MODEL

    def gate(j):
        return jax.nn.sigmoid(zg_ref[:, d * j:d * (j + 1)].astype(F32))

    merged = (gate(0) * jnp.dot(a_ref[...], wb_ref[0], preferred_element_type=F32)
              + gate(1) * jnp.dot(f_ref[...], wb_ref[1], preferred_element_type=F32)
              + gate(2) * jnp.dot(r_ref[...], wb_ref[2], preferred_element_type=F32))
    y = jnp.dot(merged.astype(BF16), wo_ref[...], preferred_element_type=F32)
    x1 = x_ref[...] + mod_ref[2:3, :] * (y * _rms(y) * gpost_ref[...])
    x1_ref[...] = x1
    h2 = x1 * _rms(x1) * gpre_ref[...] * (1.0 + mod_ref[4:5, :]) + mod_ref[3:4, :]
    h2_ref[...] = h2

    logits = _split_dot_nt(wrt_ref[...], h2) + br_ref[:, 0:1]
    tm = logits.shape[1]
    eidx = lax.broadcasted_iota(I32, (N_EXPERTS, tm), 0)
    cur = logits
    vals, hots = [], []
    for k in range(TOP_K):
        m = jnp.max(cur, axis=0, keepdims=True)
        sel = jnp.min(jnp.where(cur == m, eidx, N_EXPERTS), axis=0, keepdims=True)
        hot = eidx == sel
        vals.append(m)
        hots.append(hot)
        idx_ref[k:k + 1, :] = sel
        cur = jnp.where(hot, -jnp.inf, cur)
    exps = [jnp.exp(v - vals[0]) for v in vals]
    den = exps[0] + exps[1] + exps[2] + exps[3]
    for k in range(TOP_K):
        wts_ref[k:k + 1, :] = exps[k] / den

    member = jnp.logical_or(jnp.logical_or(hots[0], hots[1]), jnp.logical_or(hots[2], hots[3]))
    member_f = member.astype(F32)
    before = jnp.dot(member_f.astype(BF16), tri_ref[...], preferred_element_type=F32) + carry_ref[:, 0:1]
    for k in range(TOP_K):
        rank_ref[k:k + 1, :] = jnp.sum(jnp.where(hots[k], before, 0.0), axis=0, keepdims=True).astype(I32)
    carry_ref[...] = carry_ref[...] + jnp.sum(member_f, axis=1, keepdims=True)
    cnt_ref[...] = carry_ref[...]


def _finish(a_out, f_out, r_out, z_gate, x, mod, g_post, g_pre_ffn, w_branch, w_out, w_router, b_router, tile_mod):
    n, d = x.shape
    tm = FIN_TILE
    tri = jnp.asarray(np.triu(np.ones((tm, tm), np.float32), k=1)).astype(BF16)
    row = lambda i, t: (i, 0)
    const2 = lambda i, t: (0, 0)
    col = lambda i, t: (0, i)
    outs = pl.pallas_call(
        _finish_kernel,
        grid_spec=pltpu.PrefetchScalarGridSpec(
            num_scalar_prefetch=1, grid=(n // tm,),
            in_specs=[pl.BlockSpec((tm, BRANCH_WIDTH), row),
                      pl.BlockSpec((tm, BRANCH_WIDTH), row),
                      pl.BlockSpec((tm, BRANCH_WIDTH), row),
                      pl.BlockSpec((tm, GATE_W), row),
                      pl.BlockSpec((tm, d), row),
                      pl.BlockSpec((None, 6, d), lambda i, t: (t[i], 0, 0)),
                      pl.BlockSpec((1, d), const2),
                      pl.BlockSpec((1, d), const2),
                      pl.BlockSpec((3, BRANCH_WIDTH, d), lambda i, t: (0, 0, 0)),
                      pl.BlockSpec((d, d), const2),
                      pl.BlockSpec((N_EXPERTS, d), const2),
                      pl.BlockSpec((N_EXPERTS, LANES), const2),
                      pl.BlockSpec((tm, tm), const2)],
            out_specs=[pl.BlockSpec((tm, d), row),
                       pl.BlockSpec((tm, d), row),
                       pl.BlockSpec((TOP_K, tm), col),
                       pl.BlockSpec((TOP_K, tm), col),
                       pl.BlockSpec((TOP_K, tm), col),
                       pl.BlockSpec((N_EXPERTS, LANES), const2)],
            scratch_shapes=[pltpu.VMEM((N_EXPERTS, LANES), F32)]),
        out_shape=[jax.ShapeDtypeStruct((n, d), F32),
                   jax.ShapeDtypeStruct((n, d), F32),
                   jax.ShapeDtypeStruct((TOP_K, n), I32),
                   jax.ShapeDtypeStruct((TOP_K, n), F32),
                   jax.ShapeDtypeStruct((TOP_K, n), I32),
                   jax.ShapeDtypeStruct((N_EXPERTS, LANES), F32)],
        compiler_params=_params(1),
        name="merge_router",
    )(tile_mod, a_out, f_out, r_out, z_gate, x, mod, g_post.reshape(1, d), g_pre_ffn.reshape(1, d),
      w_branch.astype(BF16), w_out.astype(BF16), w_router.T,
      jnp.broadcast_to(b_router[:, None], (N_EXPERTS, LANES)), tri)
    return outs


def _dispatch_kernel(dest_ref, h_hbm, xb_in_hbm, xb_hbm, sem, *, tm):
    del xb_in_hbm
    base = pl.program_id(0) * tm

    def row_copy(src_row, dst_row):
        return pltpu.make_async_copy(h_hbm.at[pl.ds(src_row, 1)], xb_hbm.at[pl.ds(dst_row, 1)], sem)

    def issue(t, c):
        for k in range(TOP_K):
            row_copy(base + t, dest_ref[k, t]).start()
        return c

    def drain(t, c):
        for k in range(TOP_K):
            row_copy(base + t, dest_ref[k, t]).wait()
        return c

    lax.fori_loop(0, tm, issue, 0)
    lax.fori_loop(0, tm, drain, 0)


def _dispatch(h2, dest_tiles, n_rows):
    n, d = h2.shape
    tm = dest_tiles.shape[2]
    return pl.pallas_call(
        functools.partial(_dispatch_kernel, tm=tm),
        grid=(n // tm,),
        in_specs=[pl.BlockSpec((None, TOP_K, tm), lambda i: (i, 0, 0), memory_space=pltpu.SMEM),
                  pl.BlockSpec(memory_space=pl.ANY),
                  pl.BlockSpec(memory_space=pl.ANY)],
        out_specs=pl.BlockSpec(memory_space=pl.ANY),
        out_shape=jax.ShapeDtypeStruct((n_rows, d), F32),
        scratch_shapes=[pltpu.SemaphoreType.DMA],
        input_output_aliases={2: 0},
        compiler_params=_params(1),
        name="moe_dispatch",
    )(dest_tiles, h2, jnp.zeros((n_rows, d), F32))


def _expert_kernel(be_ref, nu_ref, xb_ref, w1_ref, b1_ref, w2_ref, b2_ref, yb_ref, w1b_ref, w2b_ref):
    i = pl.program_id(0)
    used = i < nu_ref[0]
    fresh = jnp.logical_or(i == 0, be_ref[i] != be_ref[jnp.maximum(i - 1, 0)])

    @pl.when(jnp.logical_and(used, fresh))
    def _():
        w1b_ref[...] = w1_ref[...].astype(BF16)
        w2b_ref[...] = w2_ref[...].astype(BF16)

    @pl.when(used)
    def _():
        z = jnp.dot(xb_ref[...].astype(BF16), w1b_ref[...], preferred_element_type=F32) + b1_ref[...]
        glu = jnp.minimum(z[:, :D_FF], SWIGLU_LIMIT)
        lin = jnp.clip(z[:, D_FF:], -SWIGLU_LIMIT, SWIGLU_LIMIT)
        act = glu * jax.nn.sigmoid(SWIGLU_ALPHA * glu) * (lin + 1.0)
        yb_ref[...] = jnp.dot(act.astype(BF16), w2b_ref[...], preferred_element_type=F32) + b2_ref[...]

    @pl.when(jnp.logical_not(used))
    def _():
        yb_ref[...] = jnp.zeros_like(yb_ref)


def _experts(xb, block_expert, n_used, w1, b1, w2, b2):
    n_rows, d = xb.shape
    tm = MOE_BLOCK
    ne = w1.shape[0]
    return pl.pallas_call(
        _expert_kernel,
        grid_spec=pltpu.PrefetchScalarGridSpec(
            num_scalar_prefetch=2, grid=(n_rows // tm,),
            in_specs=[pl.BlockSpec((tm, d), lambda i, be, nu: (jnp.minimum(i, nu[0] - 1), 0)),
                      pl.BlockSpec((None, d, 2 * D_FF), lambda i, be, nu: (be[i], 0, 0)),
                      pl.BlockSpec((None, 1, 2 * D_FF), lambda i, be, nu: (be[i], 0, 0)),
                      pl.BlockSpec((None, D_FF, d), lambda i, be, nu: (be[i], 0, 0)),
                      pl.BlockSpec((None, 1, d), lambda i, be, nu: (be[i], 0, 0))],
            out_specs=pl.BlockSpec((tm, d), lambda i, be, nu: (i, 0)),
            scratch_shapes=[pltpu.VMEM((d, 2 * D_FF), BF16), pltpu.VMEM((D_FF, d), BF16)]),
        out_shape=jax.ShapeDtypeStruct((n_rows, d), F32),
        compiler_params=_params(1),
        name="moe_experts",
    )(block_expert, n_used, xb, w1, b1.reshape(ne, 1, 2 * D_FF), w2, b2.reshape(ne, 1, d))


def _combine_kernel(tmod_ref, dest_ref, yb_hbm, wts_ref, x1_ref, mod_ref, g_ref, o_ref, buf_ref, sem, *, tm):
    del tmod_ref

    def row_copy(k, t):
        return pltpu.make_async_copy(yb_hbm.at[pl.ds(dest_ref[k, t], 1)], buf_ref.at[k, pl.ds(t, 1)], sem)

    def issue(t, c):
        for k in range(TOP_K):
            row_copy(k, t).start()
        return c

    def drain(t, c):
        for k in range(TOP_K):
            row_copy(k, t).wait()
        return c

    lax.fori_loop(0, tm, issue, 0)
    lax.fori_loop(0, tm, drain, 0)
    y = wts_ref[:, 0:1] * buf_ref[0]
    for k in range(1, TOP_K):
        y = y + wts_ref[:, k:k + 1] * buf_ref[k]
    o_ref[...] = x1_ref[...] + mod_ref[5:6, :] * (y * _rms(y) * g_ref[...])


def _combine(yb, dest_tiles, wts, x1, mod, g_post, tile_mod):
    n, d = x1.shape
    tm = dest_tiles.shape[2]
    return pl.pallas_call(
        functools.partial(_combine_kernel, tm=tm),
        grid_spec=pltpu.PrefetchScalarGridSpec(
            num_scalar_prefetch=1, grid=(n // tm,),
            in_specs=[pl.BlockSpec((None, TOP_K, tm), lambda i, t: (i, 0, 0), memory_space=pltpu.SMEM),
                      pl.BlockSpec(memory_space=pl.ANY),
                      pl.BlockSpec((tm, TOP_K), lambda i, t: (i, 0)),
                      pl.BlockSpec((tm, d), lambda i, t: (i, 0)),
                      pl.BlockSpec((None, 6, d), lambda i, t: (t[i], 0, 0)),
                      pl.BlockSpec((1, d), lambda i, t: (0, 0))],
            out_specs=pl.BlockSpec((tm, d), lambda i, t: (i, 0)),
            scratch_shapes=[pltpu.VMEM((TOP_K, tm, d), F32), pltpu.SemaphoreType.DMA]),
        out_shape=jax.ShapeDtypeStruct((n, d), F32),
        compiler_params=_params(1),
        name="moe_combine",
    )(tile_mod, dest_tiles, yb, wts, x1, mod, g_post.reshape(1, d))


def _tile_dest(dest, tm):
    k, n = dest.shape
    return dest.reshape(k, n // tm, tm).transpose(1, 0, 2)


def _moe(h2, idx, wts, rank, counts, x1, mod, g_post, w1, b1, w2, b2, tile_mod_combine):
    n, d = h2.shape
    blk = MOE_BLOCK
    n_rows = -(-(n * TOP_K + N_EXPERTS * (blk - 1)) // blk) * blk
    n_blocks = n_rows // blk
    counts = counts[:, 0].astype(I32)
    padded = (counts + blk - 1) // blk * blk
    pend = jnp.cumsum(padded)
    pstart = pend - padded
    experts = jnp.arange(N_EXPERTS, dtype=I32)
    start_of = jnp.sum(jnp.where(idx[None] == experts[:, None, None], pstart[:, None, None], 0), axis=0)
    dest = (start_of + rank).astype(I32)
    blocks = jnp.arange(n_blocks, dtype=I32) * blk
    block_expert = jnp.minimum(jnp.sum(blocks[:, None] >= pend[None, :], axis=1), N_EXPERTS - 1).astype(I32)
    n_used = (pend[-1:] // blk).astype(I32)
    xb = _dispatch(h2, _tile_dest(dest, DISPATCH_TILE), n_rows)
    yb = _experts(xb, block_expert, n_used, w1, b1, w2, b2)
    return _combine(yb, _tile_dest(dest, COMBINE_TILE), wts.T, x1, mod, g_post, tile_mod_combine)


def _tile_mod_ids(n_ctx_rows, n_lat_rows, lat_seq, tm):
    ctx = np.zeros((n_ctx_rows // tm,), np.int32)
    lat = 1 + (np.arange(n_lat_rows // tm) * tm) // lat_seq
    return jnp.asarray(np.concatenate([ctx, lat.astype(np.int32)]))


def kernel(x_prompt, x_sample, cache_k, cache_v, state_ret_fwd, state_ret_bwd, c, c_ctx, w_mod, b_mod, g_pre_mix, g_post_mix, g_pre_ffn, g_post_ffn, w_in, na_rel_bias, ret_decay_fwd, ret_decay_bwd, w_branch, w_out, w_router, b_router, w_exp_in, b_exp_in, w_exp_out, b_exp_out):
    batch, seq, d = x_prompt.shape
    dec_batch, dec_seq, _ = x_sample.shape
    depth = w_in.shape[0]
    n_ctx = batch * seq
    n_lat = dec_batch * dec_seq
    assert 1 + dec_batch <= MOD_ROWS

    x = jnp.concatenate([x_prompt.reshape(n_ctx, d), x_sample.reshape(n_lat, d)], axis=0)
    cvec = jnp.concatenate([c_ctx[None], c, jnp.zeros((MOD_ROWS - 1 - dec_batch, d), F32)], axis=0)
    mod_all = _modulation(cvec, w_mod, b_mod).reshape(depth, MOD_ROWS, 6, d)
    tmod = {tm: _tile_mod_ids(n_ctx, n_lat, dec_seq, tm) for tm in (ROW_TILE, FIN_TILE, COMBINE_TILE)}
    past = cache_k.shape[2]
    ck = cache_k.reshape(dec_batch, depth, past, NA_WIDTH)
    cv = cache_v.reshape(dec_batch, depth, past, NA_WIDTH)
    lg_f = jax.nn.log_sigmoid(ret_decay_fwd.astype(F32))
    lg_b = jax.nn.log_sigmoid(ret_decay_bwd.astype(F32))

    ks, vs, sfs, sbs = [], [], [], []
    for l in range(depth):
        mod = mod_all[l]
        h = _prenorm(x, g_pre_mix[l], mod, tmod[ROW_TILE], ROW_TILE)
        w = w_in[l]
        z_qkv = _project(h, w[:, :QKV_W], QKV_W // 2, F32)
        z_mix = _project(h, w[:, QKV_W:QKV_W + MIX_W], MIX_W // 2, BF16)
        z_gate = _project(h, w[:, QKV_W + MIX_W:], GATE_W // 3, BF16)
        ks.append(z_qkv[:n_ctx, NA_WIDTH:2 * NA_WIDTH].reshape(batch, seq, NA_HEADS, NA_HEAD_DIM))
        vs.append(z_qkv[:n_ctx, 2 * NA_WIDTH:].reshape(batch, seq, NA_HEADS, NA_HEAD_DIM))

        bias = _neighbourhood_bias(na_rel_bias[l], dec_seq)
        a_out = jnp.concatenate([_attention_ctx(z_qkv, batch, seq),
                                 _attention_lat(z_qkv, ck, cv, bias, l, n_ctx, dec_batch, dec_seq)], axis=0)
        f_out = jnp.concatenate([_fourier(z_mix, 0, batch, seq),
                                 _fourier(z_mix, n_ctx, dec_batch, dec_seq)], axis=0)
        r_ctx, s_f, s_b = _retention(z_mix, lg_f[l], lg_b[l], 0, batch, seq, rotary=False, state_out=True)
        (r_lat,) = _retention(z_mix, lg_f[l], lg_b[l], n_ctx, dec_batch, dec_seq, rotary=True,
                              states=(state_ret_fwd, state_ret_bwd), layer=l)
        r_out = jnp.concatenate([r_ctx, r_lat], axis=0)
        sfs.append(s_f)
        sbs.append(s_b)

        x1, h2, idx, wts, rank, counts = _finish(
            a_out, f_out, r_out, z_gate, x, mod, g_post_mix[l], g_pre_ffn[l],
            w_branch[l], w_out[l], w_router[l], b_router[l], tmod[FIN_TILE])
        x = _moe(h2, idx, wts, rank, counts, x1, mod, g_post_ffn[l],
                 w_exp_in[l], b_exp_in[l], w_exp_out[l], b_exp_out[l], tmod[COMBINE_TILE])

    y_prompt = x[:n_ctx].reshape(batch, seq, d)
    y_sample = x[n_ctx:].reshape(dec_batch, dec_seq, d)
    return (y_prompt, y_sample, jnp.stack(ks, axis=1), jnp.stack(vs, axis=1),
            jnp.stack(sfs, axis=1), jnp.stack(sbs, axis=1))
```

```python
import functools

import numpy as np
import jax
import jax.numpy as jnp
from jax import lax
from jax.experimental import pallas as pl
from jax.experimental.pallas import tpu as pltpu

F32 = jnp.float32
BF16 = jnp.bfloat16
I32 = jnp.int32

D_MODEL = 1024
GRID_W = 64
NA_HEADS = 8
NA_HEAD_DIM = 64
NA_WIDTH = NA_HEADS * NA_HEAD_DIM
WIN_H = 8
WIN_W = 16
FOURIER_GROUPS = 4
FOURIER_GROUP_DIM = 128
RET_HEADS = 4
RET_KEY_DIM = 128
ROPE_BASE = 10000.0
BRANCH_WIDTH = 512
N_EXPERTS = 32
TOP_K = 4
D_FF = 1024
SWIGLU_LIMIT = 7.0
SWIGLU_ALPHA = 1.702
EPS = 1e-6
NEG_INF = -1e30

QKV_W = 3 * NA_WIDTH
MIX_W = 5 * BRANCH_WIDTH
GATE_W = 3 * D_MODEL

LANES = 128
MOD_ROWS = 16
ROW_TILE = 1024
FIN_TILE = 512
MOE_BLOCK = 256
DISPATCH_TILE = 512
COMBINE_TILE = 256
VMEM_LIMIT = 56 * 1024 * 1024


def _params(n_axes, vmem=VMEM_LIMIT):
    return pltpu.CompilerParams(dimension_semantics=("arbitrary",) * n_axes, vmem_limit_bytes=vmem)


def _rms(x):
    return lax.rsqrt(jnp.mean(x * x, axis=-1, keepdims=True) + EPS)


def _mod_kernel(cv_ref, w_ref, b_ref, o_ref):
    cv = cv_ref[...]
    s = (cv * jax.nn.sigmoid(cv)).astype(BF16)
    o_ref[...] = jnp.dot(s, w_ref[...].astype(BF16), preferred_element_type=F32) + b_ref[...]


def _modulation(cv, w_mod, b_mod):
    depth, d, n = w_mod.shape
    tn = 1536
    return pl.pallas_call(
        _mod_kernel,
        grid=(depth, n // tn),
        in_specs=[pl.BlockSpec((MOD_ROWS, d), lambda l, j: (0, 0)),
                  pl.BlockSpec((None, d, tn), lambda l, j: (l, 0, j)),
                  pl.BlockSpec((None, 1, tn), lambda l, j: (l, 0, j))],
        out_specs=pl.BlockSpec((None, MOD_ROWS, tn), lambda l, j: (l, 0, j)),
        out_shape=jax.ShapeDtypeStruct((depth, MOD_ROWS, n), F32),
        compiler_params=_params(2),
        name="modulation",
    )(cv, w_mod, b_mod.reshape(depth, 1, n))


def _prenorm_kernel(tmod_ref, x_ref, g_ref, mod_ref, o_ref):
    del tmod_ref
    x = x_ref[...]
    h = x * _rms(x) * g_ref[...]
    o_ref[...] = (h * (1.0 + mod_ref[1:2, :]) + mod_ref[0:1, :]).astype(o_ref.dtype)


def _prenorm(x, g, mod, tile_mod, tm):
    n, d = x.shape
    return pl.pallas_call(
        _prenorm_kernel,
        grid_spec=pltpu.PrefetchScalarGridSpec(
            num_scalar_prefetch=1, grid=(n // tm,),
            in_specs=[pl.BlockSpec((tm, d), lambda i, t: (i, 0)),
                      pl.BlockSpec((1, d), lambda i, t: (0, 0)),
                      pl.BlockSpec((None, 6, d), lambda i, t: (t[i], 0, 0))],
            out_specs=pl.BlockSpec((tm, d), lambda i, t: (i, 0))),
        out_shape=jax.ShapeDtypeStruct((n, d), BF16),
        compiler_params=_params(1),
        name="prenorm",
    )(tile_mod, x, g.reshape(1, d), mod)


def _proj_kernel(h_ref, w_ref, o_ref, wb_ref):
    @pl.when(pl.program_id(1) == 0)
    def _():
        wb_ref[...] = w_ref[...].astype(BF16)

    o_ref[...] = jnp.dot(h_ref[...], wb_ref[...], preferred_element_type=F32).astype(o_ref.dtype)


def _project(h, w, tn, out_dtype):
    n, d = h.shape
    width = w.shape[1]
    tm = ROW_TILE
    return pl.pallas_call(
        _proj_kernel,
        grid=(width // tn, n // tm),
        in_specs=[pl.BlockSpec((tm, d), lambda j, i: (i, 0)),
                  pl.BlockSpec((d, tn), lambda j, i: (0, j))],
        out_specs=pl.BlockSpec((tm, tn), lambda j, i: (i, j)),
        out_shape=jax.ShapeDtypeStruct((n, width), out_dtype),
        scratch_shapes=[pltpu.VMEM((d, tn), BF16)],
        compiler_params=_params(2),
        name="in_proj",
    )(h, w)


def _head_pair_masks():
    lane = lax.broadcasted_iota(I32, (1, LANES), 1)
    first = lane < NA_HEAD_DIM
    return first, jnp.logical_not(first)


def _attn_ctx_kernel(qkv_ref, o_ref):
    masks = _head_pair_masks()
    scale = NA_HEAD_DIM ** -0.5
    for p in range(NA_WIDTH // LANES):
        cols = slice(LANES * p, LANES * (p + 1))
        q2 = qkv_ref[:, cols] * scale
        k2 = qkv_ref[:, NA_WIDTH + LANES * p:NA_WIDTH + LANES * (p + 1)].astype(BF16)
        v2 = qkv_ref[:, 2 * NA_WIDTH + LANES * p:2 * NA_WIDTH + LANES * (p + 1)].astype(BF16)
        outs = []
        for m in masks:
            qa = jnp.where(m, q2, 0.0).astype(BF16)
            s = lax.dot_general(qa, k2, (((1,), (1,)), ((), ())), preferred_element_type=F32)
            e = jnp.exp(s - jnp.max(s, axis=-1, keepdims=True))
            den = jnp.sum(e, axis=-1, keepdims=True)
            outs.append(jnp.dot(e.astype(BF16), v2, preferred_element_type=F32) / den)
        o_ref[:, cols] = jnp.where(masks[0], outs[0], outs[1]).astype(o_ref.dtype)


def _attention_ctx(z_qkv, n_seq, seq):
    return pl.pallas_call(
        _attn_ctx_kernel,
        grid=(n_seq,),
        in_specs=[pl.BlockSpec((seq, QKV_W), lambda b: (b, 0))],
        out_specs=pl.BlockSpec((seq, NA_WIDTH), lambda b: (b, 0)),
        out_shape=jax.ShapeDtypeStruct((n_seq * seq, NA_WIDTH), BF16),
        compiler_params=_params(1),
        name="attn_ctx",
    )(z_qkv)


def _attn_lat_kernel(q_ref, k_ref, v_ref, kc_ref, vc_ref, bias_ref, o_ref, *, tq):
    masks = _head_pair_masks()
    scale = NA_HEAD_DIM ** -0.5
    k2 = k_ref[...].astype(BF16)
    v2 = v_ref[...].astype(BF16)
    kc = kc_ref[...].astype(BF16)
    vc = vc_ref[...].astype(BF16)
    nt = (((1,), (1,)), ((), ()))

    def q_tile(qi, carry):
        rows = pl.ds(pl.multiple_of(qi * tq, tq), tq)
        q2 = q_ref[rows, :] * scale
        outs = []
        for hh, m in enumerate(masks):
            qa = jnp.where(m, q2, 0.0).astype(BF16)
            s_lat = lax.dot_general(qa, k2, nt, preferred_element_type=F32) + bias_ref[hh, rows, :]
            s_ctx = lax.dot_general(qa, kc, nt, preferred_element_type=F32)
            mx = jnp.maximum(jnp.max(s_lat, axis=-1, keepdims=True), jnp.max(s_ctx, axis=-1, keepdims=True))
            e_lat = jnp.exp(s_lat - mx)
            e_ctx = jnp.exp(s_ctx - mx)
            den = jnp.sum(e_lat, axis=-1, keepdims=True) + jnp.sum(e_ctx, axis=-1, keepdims=True)
            o = (jnp.dot(e_lat.astype(BF16), v2, preferred_element_type=F32)
                 + jnp.dot(e_ctx.astype(BF16), vc, preferred_element_type=F32))
            outs.append(o / den)
        o_ref[rows, :] = jnp.where(masks[0], outs[0], outs[1]).astype(o_ref.dtype)
        return carry

    lax.fori_loop(0, q_ref.shape[0] // tq, q_tile, 0)


def _attention_lat(z_qkv, cache_k, cache_v, bias, layer, row0, n_seq, seq):
    past = cache_k.shape[2]
    pairs = NA_WIDTH // LANES
    rb = row0 // seq
    kv_cols = NA_WIDTH // LANES
    return pl.pallas_call(
        functools.partial(_attn_lat_kernel, tq=256),
        grid=(pairs, n_seq),
        in_specs=[pl.BlockSpec((seq, LANES), lambda p, b: (rb + b, p)),
                  pl.BlockSpec((seq, LANES), lambda p, b: (rb + b, kv_cols + p)),
                  pl.BlockSpec((seq, LANES), lambda p, b: (rb + b, 2 * kv_cols + p)),
                  pl.BlockSpec((None, None, past, LANES), lambda p, b: (b, layer, 0, p)),
                  pl.BlockSpec((None, None, past, LANES), lambda p, b: (b, layer, 0, p)),
                  pl.BlockSpec((2, seq, seq), lambda p, b: (p, 0, 0))],
        out_specs=pl.BlockSpec((seq, LANES), lambda p, b: (b, p)),
        out_shape=jax.ShapeDtypeStruct((n_seq * seq, NA_WIDTH), BF16),
        compiler_params=_params(2),
        name="attn_lat",
    )(z_qkv, z_qkv, z_qkv, cache_k, cache_v, bias)


def _neighbourhood_bias(rpb, seq):
    rows = seq // GRID_W
    kh = WIN_H
    assert rows >= WIN_H
    lead = rpb.shape[:-2]
    c = np.arange(GRID_W)
    q_cs = np.clip(c - WIN_W // 2, 0, GRID_W - WIN_W)
    col_ok = (c[None, :] >= q_cs[:, None]) & (c[None, :] < q_cs[:, None] + WIN_W)
    pad_c = GRID_W - WIN_W
    pc = jnp.pad(rpb, ((0, 0),) * (rpb.ndim - 1) + ((pad_c, pad_c),), mode="edge")
    w = jnp.stack([pc[..., GRID_W - 1 - cq:2 * GRID_W - 1 - cq] for cq in range(GRID_W)], axis=-3)
    w = jnp.where(jnp.asarray(col_ok)[:, None, :], w, NEG_INF)
    r = np.arange(rows)
    rs = np.clip(r - kh // 2, 0, rows - kh)
    blocks = []
    for rq in range(rows):
        lo = int(rs[rq]) - rq + WIN_H - 1
        slab = w[..., lo:lo + kh, :].reshape(lead + (GRID_W, kh * GRID_W))
        pad = ((0, 0),) * (len(lead) + 1) + ((int(rs[rq]) * GRID_W, (rows - kh - int(rs[rq])) * GRID_W),)
        blocks.append(jnp.pad(slab, pad, constant_values=NEG_INF))
    return jnp.stack(blocks, axis=-3).reshape(lead + (seq, seq))


def _fourier_kernel(u_ref, ct2_ref, cc_ref, sc_ref, o_ref, pq_ref):
    t = u_ref.shape[0]
    for g in range(FOURIER_GROUPS):
        cols = slice(FOURIER_GROUP_DIM * g, FOURIER_GROUP_DIM * (g + 1))
        ug = u_ref[:, cols]
        pq_ref[0:t, cols] = jnp.dot(ug, cc_ref[...], preferred_element_type=F32).astype(BF16)
        pq_ref[t:2 * t, cols] = jnp.dot(ug, sc_ref[...], preferred_element_type=F32).astype(BF16)
    o_ref[...] = jnp.dot(ct2_ref[...], pq_ref[...], preferred_element_type=F32).astype(o_ref.dtype)


def _dft_tables(t):
    def cs(n):
        k = np.arange(n, dtype=np.int64)
        ang = 2.0 * np.pi * ((k[:, None] * k[None, :]) % n).astype(np.float64) / n
        return np.cos(ang) / np.sqrt(n), np.sin(ang) / np.sqrt(n)

    ct, st = cs(t)
    cc, sc = cs(FOURIER_GROUP_DIM)
    ct2 = np.concatenate([ct, -st], axis=1).astype(np.float32)
    return (jnp.asarray(ct2).astype(BF16), jnp.asarray(cc.astype(np.float32)).astype(BF16),
            jnp.asarray(sc.astype(np.float32)).astype(BF16))


def _fourier(z_mix, row0, n_seq, seq):
    ct2, cc, sc = _dft_tables(seq)
    width = FOURIER_GROUPS * FOURIER_GROUP_DIM
    rb = row0 // seq
    return pl.pallas_call(
        _fourier_kernel,
        grid=(n_seq,),
        in_specs=[pl.BlockSpec((seq, width), lambda b: (rb + b, 0)),
                  pl.BlockSpec((seq, 2 * seq), lambda b: (0, 0)),
                  pl.BlockSpec((FOURIER_GROUP_DIM, FOURIER_GROUP_DIM), lambda b: (0, 0)),
                  pl.BlockSpec((FOURIER_GROUP_DIM, FOURIER_GROUP_DIM), lambda b: (0, 0))],
        out_specs=pl.BlockSpec((seq, width), lambda b: (b, 0)),
        out_shape=jax.ShapeDtypeStruct((n_seq * seq, width), BF16),
        scratch_shapes=[pltpu.VMEM((2 * seq, width), BF16)],
        compiler_params=_params(1),
        name="fourier",
    )(z_mix, ct2, cc, sc)


def _rotary_tables(t):
    pos = np.arange(t)
    row = (pos // GRID_W).astype(np.float64)
    col = (pos % GRID_W).astype(np.float64)
    nf = RET_KEY_DIM // 4
    inv_freq = ROPE_BASE ** (-np.arange(nf, dtype=np.float64) / nf)
    ar = row[:, None] * inv_freq[None]
    ac = col[:, None] * inv_freq[None]
    cos = np.concatenate([np.cos(ar), np.cos(ar), np.cos(ac), np.cos(ac)], axis=1)
    sin = np.concatenate([-np.sin(ar), np.sin(ar), -np.sin(ac), np.sin(ac)], axis=1)
    return jnp.asarray(cos.astype(np.float32)), jnp.asarray(sin.astype(np.float32))


def _ret_kernel(lgf_ref, lgb_ref, *refs, t, tq, rotary, state_in, state_out):
    refs = list(refs)
    q_ref, k_ref, v_ref, g_ref = refs[:4]
    refs = refs[4:]
    if rotary:
        cos_ref, sin_ref = refs[:2]
        refs = refs[2:]
    if state_in:
        sf0_ref, sb0_ref = refs[:2]
        refs = refs[2:]
    o_ref = refs[0]
    refs = refs[1:]
    if state_out:
        sf_ref, sb_ref = refs[:2]
        refs = refs[2:]
    dec_ref, kb_ref = refs

    h = pl.program_id(0)
    lgf = lgf_ref[h]
    lgb = lgb_ref[h]
    scale = RET_KEY_DIM ** -0.5
    nq = t // tq

    @pl.when(pl.program_id(1) == 0)
    def _():
        def fill(ri, c):
            rows = pl.ds(pl.multiple_of(ri * tq, tq), tq)
            i = lax.broadcasted_iota(I32, (tq, t), 0) + ri * tq
            j = lax.broadcasted_iota(I32, (tq, t), 1)
            d = (i - j).astype(F32)
            m = jnp.exp(jnp.abs(d) * jnp.where(d > 0, lgf, lgb))
            dec_ref[rows, :] = jnp.where(d == 0, 2.0, m)
            return c

        lax.fori_loop(0, nq, fill, 0)

    if rotary:
        lane = lax.broadcasted_iota(I32, (1, LANES), 1)
        low = (lane % (RET_KEY_DIM // 2)) < (RET_KEY_DIM // 4)

        def rot(x, rows):
            swapped = jnp.where(low, pltpu.roll(x, LANES - RET_KEY_DIM // 4, 1), pltpu.roll(x, RET_KEY_DIM // 4, 1))
            return x * cos_ref[rows, :] + swapped * sin_ref[rows, :]
    else:
        def rot(x, rows):
            return x

    all_rows = slice(0, t)
    kr = rot(k_ref[...].astype(F32), all_rows)
    kb_ref[...] = kr.astype(BF16)
    vb = v_ref[...]

    if state_out:
        j = lax.broadcasted_iota(I32, (t, 1), 0).astype(F32)
        tn = (((0,), (0,)), ((), ()))
        kf = (kr * (scale * jnp.exp(lgf * (t - 1.0 - j)))).astype(BF16)
        kbw = (kr * (scale * jnp.exp(lgb * j))).astype(BF16)
        sf = lax.dot_general(kf, vb, tn, preferred_element_type=F32)
        sb = lax.dot_general(kbw, vb, tn, preferred_element_type=F32)
        if state_in:
            sf = sf + jnp.exp(lgf * t) * sf0_ref[...]
            sb = sb + jnp.exp(lgb * t) * sb0_ref[...]
        sf_ref[...] = sf
        sb_ref[...] = sb

    def q_tile(qi, carry):
        r0 = pl.multiple_of(qi * tq, tq)
        rows = pl.ds(r0, tq)
        qr = rot(q_ref[rows, :].astype(F32), rows)
        s = lax.dot_general((qr * scale).astype(BF16), kb_ref[...], (((1,), (1,)), ((), ())),
                            preferred_element_type=F32)
        y = jnp.dot((s * dec_ref[rows, :]).astype(BF16), vb, preferred_element_type=F32)
        if state_in:
            pos = (lax.broadcasted_iota(I32, (tq, 1), 0) + r0).astype(F32)
            qf = (qr * jnp.exp(lgf * (pos + 1.0))).astype(BF16)
            qb = (qr * jnp.exp(lgb * (t - pos))).astype(BF16)
            y = (y + jnp.dot(qf, sf0_ref[...].astype(BF16), preferred_element_type=F32)
                 + jnp.dot(qb, sb0_ref[...].astype(BF16), preferred_element_type=F32))
        mean = jnp.mean(y, axis=-1, keepdims=True)
        yc = y - mean
        yn = yc * lax.rsqrt(jnp.mean(yc * yc, axis=-1, keepdims=True) + EPS)
        g = g_ref[rows, :].astype(F32)
        o_ref[rows, :] = (g * jax.nn.sigmoid(g) * yn).astype(o_ref.dtype)
        return carry

    lax.fori_loop(0, nq, q_tile, 0)


def _retention(z_mix, lg_f, lg_b, row0, n_seq, seq, *, rotary, states=None, layer=0, state_out=False):
    rb = row0 // seq
    cb = BRANCH_WIDTH // LANES
    tq = min(seq, 256)
    state_in = states is not None
    in_specs = [pl.BlockSpec((seq, LANES), lambda h, b, *_: (rb + b, 1 * cb + h)),
                pl.BlockSpec((seq, LANES), lambda h, b, *_: (rb + b, 2 * cb + h)),
                pl.BlockSpec((seq, LANES), lambda h, b, *_: (rb + b, 3 * cb + h)),
                pl.BlockSpec((seq, LANES), lambda h, b, *_: (rb + b, 4 * cb + h))]
    args = [z_mix, z_mix, z_mix, z_mix]
    if rotary:
        cos, sin = _rotary_tables(seq)
        in_specs += [pl.BlockSpec((seq, LANES), lambda h, b, *_: (0, 0))] * 2
        args += [cos, sin]
    if state_in:
        st_spec = pl.BlockSpec((None, None, None, RET_KEY_DIM, RET_KEY_DIM), lambda h, b, *_: (b, layer, h, 0, 0))
        in_specs += [st_spec, st_spec]
        args += list(states)
    out_specs = [pl.BlockSpec((seq, LANES), lambda h, b, *_: (b, h))]
    out_shape = [jax.ShapeDtypeStruct((n_seq * seq, RET_HEADS * LANES), BF16)]
    if state_out:
        so = pl.BlockSpec((None, None, RET_KEY_DIM, RET_KEY_DIM), lambda h, b, *_: (b, h, 0, 0))
        out_specs += [so, so]
        out_shape += [jax.ShapeDtypeStruct((n_seq, RET_HEADS, RET_KEY_DIM, RET_KEY_DIM), F32)] * 2
    return pl.pallas_call(
        functools.partial(_ret_kernel, t=seq, tq=tq, rotary=rotary, state_in=state_in, state_out=state_out),
        grid_spec=pltpu.PrefetchScalarGridSpec(
            num_scalar_prefetch=2, grid=(RET_HEADS, n_seq),
            in_specs=in_specs, out_specs=out_specs,
            scratch_shapes=[pltpu.VMEM((seq, seq), F32), pltpu.VMEM((seq, LANES), BF16)]),
        out_shape=out_shape,
        compiler_params=_params(2),
        name="retention",
    )(lg_f, lg_b, *args)


def _split_dot_nt(w, x):
    nt = (((1,), (1,)), ((), ()))
    w_hi = w.astype(BF16)
    w_lo = (w - w_hi.astype(F32)).astype(BF16)
    x_hi = x.astype(BF16)
    x_lo = (x - x_hi.astype(F32)).astype(BF16)
    return (lax.dot_general(w_hi, x_hi, nt, preferred_element_type=F32)
            + lax.dot_general(w_hi, x_lo, nt, preferred_element_type=F32)
            + lax.dot_general(w_lo, x_hi, nt, preferred_element_type=F32))


def _finish_kernel(tmod_ref, a_ref, f_ref, r_ref, zg_ref, x_ref, mod_ref, gpost_ref, gpre_ref,
                   wb_ref, wo_ref, wrt_ref, br_ref, tri_ref,
                   x1_ref, h2_ref, idx_ref, wts_ref, rank_ref, cnt_ref, carry_ref):
    del tmod_ref

    @pl.when(pl.program_id(0) == 0)
    def _():
        carry_ref[...] = jnp.zeros_like(carry_ref)

    d = D_MODEL

    def gate(j):
        return jax.nn.sigmoid(zg_ref[:, d * j:d * (j + 1)].astype(F32))

    merged = (gate(0) * jnp.dot(a_ref[...], wb_ref[0], preferred_element_type=F32)
              + gate(1) * jnp.dot(f_ref[...], wb_ref[1], preferred_element_type=F32)
              + gate(2) * jnp.dot(r_ref[...], wb_ref[2], preferred_element_type=F32))
    y = jnp.dot(merged.astype(BF16), wo_ref[...], preferred_element_type=F32)
    x1 = x_ref[...] + mod_ref[2:3, :] * (y * _rms(y) * gpost_ref[...])
    x1_ref[...] = x1
    h2 = x1 * _rms(x1) * gpre_ref[...] * (1.0 + mod_ref[4:5, :]) + mod_ref[3:4, :]
    h2_ref[...] = h2

    logits = _split_dot_nt(wrt_ref[...], h2) + br_ref[:, 0:1]
    tm = logits.shape[1]
    eidx = lax.broadcasted_iota(I32, (N_EXPERTS, tm), 0)
    cur = logits
    vals, hots = [], []
    for k in range(TOP_K):
        m = jnp.max(cur, axis=0, keepdims=True)
        sel = jnp.min(jnp.where(cur == m, eidx, N_EXPERTS), axis=0, keepdims=True)
        hot = eidx == sel
        vals.append(m)
        hots.append(hot)
        idx_ref[k:k + 1, :] = sel
        cur = jnp.where(hot, -jnp.inf, cur)
    exps = [jnp.exp(v - vals[0]) for v in vals]
    den = exps[0] + exps[1] + exps[2] + exps[3]
    for k in range(TOP_K):
        wts_ref[k:k + 1, :] = exps[k] / den

    member = jnp.logical_or(jnp.logical_or(hots[0], hots[1]), jnp.logical_or(hots[2], hots[3]))
    member_f = member.astype(F32)
    before = jnp.dot(member_f.astype(BF16), tri_ref[...], preferred_element_type=F32) + carry_ref[:, 0:1]
    for k in range(TOP_K):
        rank_ref[k:k + 1, :] = jnp.sum(jnp.where(hots[k], before, 0.0), axis=0, keepdims=True).astype(I32)
    carry_ref[...] = carry_ref[...] + jnp.sum(member_f, axis=1, keepdims=True)
    cnt_ref[...] = carry_ref[...]


def _finish(a_out, f_out, r_out, z_gate, x, mod, g_post, g_pre_ffn, w_branch, w_out, w_router, b_router, tile_mod):
    n, d = x.shape
    tm = FIN_TILE
    tri = jnp.asarray(np.triu(np.ones((tm, tm), np.float32), k=1)).astype(BF16)
    row = lambda i, t: (i, 0)
    const2 = lambda i, t: (0, 0)
    col = lambda i, t: (0, i)
    outs = pl.pallas_call(
        _finish_kernel,
        grid_spec=pltpu.PrefetchScalarGridSpec(
            num_scalar_prefetch=1, grid=(n // tm,),
            in_specs=[pl.BlockSpec((tm, BRANCH_WIDTH), row),
                      pl.BlockSpec((tm, BRANCH_WIDTH), row),
                      pl.BlockSpec((tm, BRANCH_WIDTH), row),
                      pl.BlockSpec((tm, GATE_W), row),
                      pl.BlockSpec((tm, d), row),
                      pl.BlockSpec((None, 6, d), lambda i, t: (t[i], 0, 0)),
                      pl.BlockSpec((1, d), const2),
                      pl.BlockSpec((1, d), const2),
                      pl.BlockSpec((3, BRANCH_WIDTH, d), lambda i, t: (0, 0, 0)),
                      pl.BlockSpec((d, d), const2),
                      pl.BlockSpec((N_EXPERTS, d), const2),
                      pl.BlockSpec((N_EXPERTS, LANES), const2),
                      pl.BlockSpec((tm, tm), const2)],
            out_specs=[pl.BlockSpec((tm, d), row),
                       pl.BlockSpec((tm, d), row),
                       pl.BlockSpec((TOP_K, tm), col),
                       pl.BlockSpec((TOP_K, tm), col),
                       pl.BlockSpec((TOP_K, tm), col),
                       pl.BlockSpec((N_EXPERTS, LANES), const2)],
            scratch_shapes=[pltpu.VMEM((N_EXPERTS, LANES), F32)]),
        out_shape=[jax.ShapeDtypeStruct((n, d), F32),
                   jax.ShapeDtypeStruct((n, d), F32),
                   jax.ShapeDtypeStruct((TOP_K, n), I32),
                   jax.ShapeDtypeStruct((TOP_K, n), F32),
                   jax.ShapeDtypeStruct((TOP_K, n), I32),
                   jax.ShapeDtypeStruct((N_EXPERTS, LANES), F32)],
        compiler_params=_params(1),
        name="merge_router",
    )(tile_mod, a_out, f_out, r_out, z_gate, x, mod, g_post.reshape(1, d), g_pre_ffn.reshape(1, d),
      w_branch.astype(BF16), w_out.astype(BF16), w_router.T,
      jnp.broadcast_to(b_router[:, None], (N_EXPERTS, LANES)), tri)
    return outs


def _dispatch_kernel(dest_ref, h_ref, xb_in_hbm, xb_hbm, sem, *, tm):
    del xb_in_hbm

    def row_copy(src_row, dst_row):
        return pltpu.make_async_copy(h_ref.at[pl.ds(src_row, 1)], xb_hbm.at[pl.ds(dst_row, 1)], sem)

    def issue(t, c):
        for k in range(TOP_K):
            row_copy(t, dest_ref[k, t]).start()
        return c

    def drain(t, c):
        for k in range(TOP_K):
            row_copy(t, dest_ref[k, t]).wait()
        return c

    lax.fori_loop(0, tm, issue, 0)
    lax.fori_loop(0, tm, drain, 0)


def _dispatch(h2, dest_tiles, n_rows):
    n, d = h2.shape
    tm = dest_tiles.shape[2]
    return pl.pallas_call(
        functools.partial(_dispatch_kernel, tm=tm),
        grid=(n // tm,),
        in_specs=[pl.BlockSpec((None, TOP_K, tm), lambda i: (i, 0, 0), memory_space=pltpu.SMEM),
                  pl.BlockSpec((tm, d), lambda i: (i, 0)),
                  pl.BlockSpec(memory_space=pl.ANY)],
        out_specs=pl.BlockSpec(memory_space=pl.ANY),
        out_shape=jax.ShapeDtypeStruct((n_rows, d), F32),
        scratch_shapes=[pltpu.SemaphoreType.DMA],
        input_output_aliases={2: 0},
        compiler_params=_params(1),
        name="moe_dispatch",
    )(dest_tiles, h2, jnp.zeros((n_rows, d), F32))


def _expert_kernel(be_ref, nu_ref, xb_ref, w1_ref, b1_ref, w2_ref, b2_ref, yb_ref, w1b_ref, w2b_ref):
    i = pl.program_id(0)
    used = i < nu_ref[0]
    fresh = jnp.logical_or(i == 0, be_ref[i] != be_ref[jnp.maximum(i - 1, 0)])

    @pl.when(jnp.logical_and(used, fresh))
    def _():
        w1b_ref[...] = w1_ref[...].astype(BF16)
        w2b_ref[...] = w2_ref[...].astype(BF16)

    @pl.when(used)
    def _():
        z = jnp.dot(xb_ref[...].astype(BF16), w1b_ref[...], preferred_element_type=F32) + b1_ref[...]
        glu = jnp.minimum(z[:, :D_FF], SWIGLU_LIMIT)
        lin = jnp.clip(z[:, D_FF:], -SWIGLU_LIMIT, SWIGLU_LIMIT)
        act = glu * jax.nn.sigmoid(SWIGLU_ALPHA * glu) * (lin + 1.0)
        yb_ref[...] = jnp.dot(act.astype(BF16), w2b_ref[...], preferred_element_type=F32) + b2_ref[...]

    @pl.when(jnp.logical_not(used))
    def _():
        yb_ref[...] = jnp.zeros_like(yb_ref)


def _experts(xb, block_expert, n_used, w1, b1, w2, b2):
    n_rows, d = xb.shape
    tm = MOE_BLOCK
    ne = w1.shape[0]
    return pl.pallas_call(
        _expert_kernel,
        grid_spec=pltpu.PrefetchScalarGridSpec(
            num_scalar_prefetch=2, grid=(n_rows // tm,),
            in_specs=[pl.BlockSpec((tm, d), lambda i, be, nu: (jnp.minimum(i, nu[0] - 1), 0)),
                      pl.BlockSpec((None, d, 2 * D_FF), lambda i, be, nu: (be[i], 0, 0)),
                      pl.BlockSpec((None, 1, 2 * D_FF), lambda i, be, nu: (be[i], 0, 0)),
                      pl.BlockSpec((None, D_FF, d), lambda i, be, nu: (be[i], 0, 0)),
                      pl.BlockSpec((None, 1, d), lambda i, be, nu: (be[i], 0, 0))],
            out_specs=pl.BlockSpec((tm, d), lambda i, be, nu: (i, 0)),
            scratch_shapes=[pltpu.VMEM((d, 2 * D_FF), BF16), pltpu.VMEM((D_FF, d), BF16)]),
        out_shape=jax.ShapeDtypeStruct((n_rows, d), F32),
        compiler_params=_params(1),
        name="moe_experts",
    )(block_expert, n_used, xb, w1, b1.reshape(ne, 1, 2 * D_FF), w2, b2.reshape(ne, 1, d))


def _combine_kernel(tmod_ref, dest_ref, yb_hbm, wts_ref, x1_ref, mod_ref, g_ref, o_ref, buf_ref, sem, *, tm):
    del tmod_ref

    def row_copy(k, t):
        return pltpu.make_async_copy(yb_hbm.at[pl.ds(dest_ref[k, t], 1)], buf_ref.at[k, pl.ds(t, 1)], sem)

    def issue(t, c):
        for k in range(TOP_K):
            row_copy(k, t).start()
        return c

    def drain(t, c):
        for k in range(TOP_K):
            row_copy(k, t).wait()
        return c

    lax.fori_loop(0, tm, issue, 0)
    lax.fori_loop(0, tm, drain, 0)
    y = wts_ref[:, 0:1] * buf_ref[0]
    for k in range(1, TOP_K):
        y = y + wts_ref[:, k:k + 1] * buf_ref[k]
    o_ref[...] = x1_ref[...] + mod_ref[5:6, :] * (y * _rms(y) * g_ref[...])


def _combine(yb, dest_tiles, wts, x1, mod, g_post, tile_mod):
    n, d = x1.shape
    tm = dest_tiles.shape[2]
    return pl.pallas_call(
        functools.partial(_combine_kernel, tm=tm),
        grid_spec=pltpu.PrefetchScalarGridSpec(
            num_scalar_prefetch=1, grid=(n // tm,),
            in_specs=[pl.BlockSpec((None, TOP_K, tm), lambda i, t: (i, 0, 0), memory_space=pltpu.SMEM),
                      pl.BlockSpec(memory_space=pl.ANY),
                      pl.BlockSpec((tm, TOP_K), lambda i, t: (i, 0)),
                      pl.BlockSpec((tm, d), lambda i, t: (i, 0)),
                      pl.BlockSpec((None, 6, d), lambda i, t: (t[i], 0, 0)),
                      pl.BlockSpec((1, d), lambda i, t: (0, 0))],
            out_specs=pl.BlockSpec((tm, d), lambda i, t: (i, 0)),
            scratch_shapes=[pltpu.VMEM((TOP_K, tm, d), F32), pltpu.SemaphoreType.DMA]),
        out_shape=jax.ShapeDtypeStruct((n, d), F32),
        compiler_params=_params(1),
        name="moe_combine",
    )(tile_mod, dest_tiles, yb, wts, x1, mod, g_post.reshape(1, d))


def _tile_dest(dest, tm):
    k, n = dest.shape
    return dest.reshape(k, n // tm, tm).transpose(1, 0, 2)


def _moe(h2, idx, wts, rank, counts, x1, mod, g_post, w1, b1, w2, b2, tile_mod_combine):
    n, d = h2.shape
    blk = MOE_BLOCK
    n_rows = -(-(n * TOP_K + N_EXPERTS * (blk - 1)) // blk) * blk
    n_blocks = n_rows // blk
    counts = counts[:, 0].astype(I32)
    padded = (counts + blk - 1) // blk * blk
    pend = jnp.cumsum(padded)
    pstart = pend - padded
    experts = jnp.arange(N_EXPERTS, dtype=I32)
    start_of = jnp.sum(jnp.where(idx[None] == experts[:, None, None], pstart[:, None, None], 0), axis=0)
    dest = (start_of + rank).astype(I32)
    blocks = jnp.arange(n_blocks, dtype=I32) * blk
    block_expert = jnp.minimum(jnp.sum(blocks[:, None] >= pend[None, :], axis=1), N_EXPERTS - 1).astype(I32)
    n_used = (pend[-1:] // blk).astype(I32)
    xb = _dispatch(h2, _tile_dest(dest, DISPATCH_TILE), n_rows)
    yb = _experts(xb, block_expert, n_used, w1, b1, w2, b2)
    return _combine(yb, _tile_dest(dest, COMBINE_TILE), wts.T, x1, mod, g_post, tile_mod_combine)


def _tile_mod_ids(n_ctx_rows, n_lat_rows, lat_seq, tm):
    ctx = np.zeros((n_ctx_rows // tm,), np.int32)
    lat = 1 + (np.arange(n_lat_rows // tm) * tm) // lat_seq
    return jnp.asarray(np.concatenate([ctx, lat.astype(np.int32)]))


def kernel(x_prompt, x_sample, cache_k, cache_v, state_ret_fwd, state_ret_bwd, c, c_ctx, w_mod, b_mod, g_pre_mix, g_post_mix, g_pre_ffn, g_post_ffn, w_in, na_rel_bias, ret_decay_fwd, ret_decay_bwd, w_branch, w_out, w_router, b_router, w_exp_in, b_exp_in, w_exp_out, b_exp_out):
    batch, seq, d = x_prompt.shape
    dec_batch, dec_seq, _ = x_sample.shape
    depth = w_in.shape[0]
    n_ctx = batch * seq
    n_lat = dec_batch * dec_seq
    assert 1 + dec_batch <= MOD_ROWS

    x = jnp.concatenate([x_prompt.reshape(n_ctx, d), x_sample.reshape(n_lat, d)], axis=0)
    cvec = jnp.concatenate([c_ctx[None], c, jnp.zeros((MOD_ROWS - 1 - dec_batch, d), F32)], axis=0)
    mod_all = _modulation(cvec, w_mod, b_mod).reshape(depth, MOD_ROWS, 6, d)
    tmod = {tm: _tile_mod_ids(n_ctx, n_lat, dec_seq, tm) for tm in (ROW_TILE, FIN_TILE, COMBINE_TILE)}
    past = cache_k.shape[2]
    ck = cache_k.reshape(dec_batch, depth, past, NA_WIDTH)
    cv = cache_v.reshape(dec_batch, depth, past, NA_WIDTH)
    lg_f = jax.nn.log_sigmoid(ret_decay_fwd.astype(F32))
    lg_b = jax.nn.log_sigmoid(ret_decay_bwd.astype(F32))
    bias_all = _neighbourhood_bias(na_rel_bias, dec_seq)

    ks, vs, sfs, sbs = [], [], [], []
    for l in range(depth):
        mod = mod_all[l]
        h = _prenorm(x, g_pre_mix[l], mod, tmod[ROW_TILE], ROW_TILE)
        w = w_in[l]
        z_qkv = _project(h, w[:, :QKV_W], QKV_W // 2, F32)
        z_mix = _project(h, w[:, QKV_W:QKV_W + MIX_W], MIX_W // 2, BF16)
        z_gate = _project(h, w[:, QKV_W + MIX_W:], GATE_W // 3, BF16)
        ks.append(z_qkv[:n_ctx, NA_WIDTH:2 * NA_WIDTH].reshape(batch, seq, NA_HEADS, NA_HEAD_DIM))
        vs.append(z_qkv[:n_ctx, 2 * NA_WIDTH:].reshape(batch, seq, NA_HEADS, NA_HEAD_DIM))

        a_out = jnp.concatenate([_attention_ctx(z_qkv, batch, seq),
                                 _attention_lat(z_qkv, ck, cv, bias_all[l], l, n_ctx, dec_batch, dec_seq)], axis=0)
        f_out = jnp.concatenate([_fourier(z_mix, 0, batch, seq),
                                 _fourier(z_mix, n_ctx, dec_batch, dec_seq)], axis=0)
        r_ctx, s_f, s_b = _retention(z_mix, lg_f[l], lg_b[l], 0, batch, seq, rotary=False, state_out=True)
        (r_lat,) = _retention(z_mix, lg_f[l], lg_b[l], n_ctx, dec_batch, dec_seq, rotary=True,
                              states=(state_ret_fwd, state_ret_bwd), layer=l)
        r_out = jnp.concatenate([r_ctx, r_lat], axis=0)
        sfs.append(s_f)
        sbs.append(s_b)

        x1, h2, idx, wts, rank, counts = _finish(
            a_out, f_out, r_out, z_gate, x, mod, g_post_mix[l], g_pre_ffn[l],
            w_branch[l], w_out[l], w_router[l], b_router[l], tmod[FIN_TILE])
        x = _moe(h2, idx, wts, rank, counts, x1, mod, g_post_ffn[l],
                 w_exp_in[l], b_exp_in[l], w_exp_out[l], b_exp_out[l], tmod[COMBINE_TILE])

    y_prompt = x[:n_ctx].reshape(batch, seq, d)
    y_sample = x[n_ctx:].reshape(dec_batch, dec_seq, d)
    return (y_prompt, y_sample, jnp.stack(ks, axis=1), jnp.stack(vs, axis=1),
            jnp.stack(sfs, axis=1), jnp.stack(sbs, axis=1))
```

```python
import functools

import numpy as np
import jax
import jax.numpy as jnp
from jax import lax
from jax.experimental import pallas as pl
from jax.experimental.pallas import tpu as pltpu

F32 = jnp.float32
BF16 = jnp.bfloat16
I32 = jnp.int32

D_MODEL = 1024
GRID_W = 64
NA_HEADS = 8
NA_HEAD_DIM = 64
NA_WIDTH = NA_HEADS * NA_HEAD_DIM
WIN_H = 8
WIN_W = 16
FOURIER_GROUPS = 4
FOURIER_GROUP_DIM = 128
RET_HEADS = 4
RET_KEY_DIM = 128
ROPE_BASE = 10000.0
BRANCH_WIDTH = 512
N_EXPERTS = 32
TOP_K = 4
D_FF = 1024
SWIGLU_LIMIT = 7.0
SWIGLU_ALPHA = 1.702
EPS = 1e-6
NEG_INF = -1e30

QKVU_W = 4 * NA_WIDTH
RET_W = 4 * BRANCH_WIDTH
GATE_W = 3 * D_MODEL
PROJ_TILE = 1024

LANES = 128
MOD_ROWS = 16
ROW_TILE = 1024
FIN_TILE = 512
MOE_BLOCK = 256
DISPATCH_TILE = 512
COMBINE_TILE = 256
VMEM_LIMIT = 56 * 1024 * 1024


def _params(n_axes, vmem=VMEM_LIMIT):
    return pltpu.CompilerParams(dimension_semantics=("arbitrary",) * n_axes, vmem_limit_bytes=vmem)


def _rms(x):
    return lax.rsqrt(jnp.mean(x * x, axis=-1, keepdims=True) + EPS)


def _mod_kernel(cv_ref, w_ref, b_ref, o_ref):
    cv = cv_ref[...]
    s = (cv * jax.nn.sigmoid(cv)).astype(BF16)
    o_ref[...] = jnp.dot(s, w_ref[...].astype(BF16), preferred_element_type=F32) + b_ref[...]


def _modulation(cv, w_mod, b_mod):
    depth, d, n = w_mod.shape
    tn = 1536
    return pl.pallas_call(
        _mod_kernel,
        grid=(depth, n // tn),
        in_specs=[pl.BlockSpec((MOD_ROWS, d), lambda l, j: (0, 0)),
                  pl.BlockSpec((None, d, tn), lambda l, j: (l, 0, j)),
                  pl.BlockSpec((None, 1, tn), lambda l, j: (l, 0, j))],
        out_specs=pl.BlockSpec((None, MOD_ROWS, tn), lambda l, j: (l, 0, j)),
        out_shape=jax.ShapeDtypeStruct((depth, MOD_ROWS, n), F32),
        compiler_params=_params(2),
        name="modulation",
    )(cv, w_mod, b_mod.reshape(depth, 1, n))


def _prenorm_kernel(tmod_ref, x_ref, g_ref, mod_ref, o_ref):
    del tmod_ref
    x = x_ref[...]
    h = x * _rms(x) * g_ref[...]
    o_ref[...] = (h * (1.0 + mod_ref[1:2, :]) + mod_ref[0:1, :]).astype(o_ref.dtype)


def _prenorm(x, g, mod, layer, tile_mod, tm):
    n, d = x.shape
    return pl.pallas_call(
        _prenorm_kernel,
        grid_spec=pltpu.PrefetchScalarGridSpec(
            num_scalar_prefetch=1, grid=(n // tm,),
            in_specs=[pl.BlockSpec((tm, d), lambda i, t: (i, 0)),
                      pl.BlockSpec((None, 1, d), lambda i, t: (layer, 0, 0)),
                      pl.BlockSpec((None, None, 6, d), lambda i, t: (layer, t[i], 0, 0))],
            out_specs=pl.BlockSpec((tm, d), lambda i, t: (i, 0))),
        out_shape=jax.ShapeDtypeStruct((n, d), BF16),
        compiler_params=_params(1),
        name="prenorm",
    )(tile_mod, x, g, mod)


def _proj_kernel(h_ref, w_ref, o_ref, wb_ref):
    @pl.when(pl.program_id(1) == 0)
    def _():
        wb_ref[...] = w_ref[...].astype(BF16)

    o_ref[...] = jnp.dot(h_ref[...], wb_ref[...], preferred_element_type=F32).astype(o_ref.dtype)


def _project(h, w, layer, col0, width, out_dtype):
    n, d = h.shape
    tm = ROW_TILE
    tn = PROJ_TILE
    cb = col0 // tn
    return pl.pallas_call(
        _proj_kernel,
        grid=(width // tn, n // tm),
        in_specs=[pl.BlockSpec((tm, d), lambda j, i: (i, 0)),
                  pl.BlockSpec((None, d, tn), lambda j, i: (layer, 0, cb + j))],
        out_specs=pl.BlockSpec((tm, tn), lambda j, i: (i, j)),
        out_shape=jax.ShapeDtypeStruct((n, width), out_dtype),
        scratch_shapes=[pltpu.VMEM((d, tn), BF16)],
        compiler_params=_params(2),
        name="in_proj",
    )(h, w)


def _head_pair_masks():
    lane = lax.broadcasted_iota(I32, (1, LANES), 1)
    first = lane < NA_HEAD_DIM
    return first, jnp.logical_not(first)


def _attn_ctx_kernel(q_ref, k_ref, v_ref, o_ref):
    masks = _head_pair_masks()
    scale = NA_HEAD_DIM ** -0.5
    for p in range(NA_WIDTH // LANES):
        cols = slice(LANES * p, LANES * (p + 1))
        q2 = q_ref[:, cols] * scale
        k2 = k_ref[:, cols].astype(BF16)
        v2 = v_ref[:, cols].astype(BF16)
        outs = []
        for m in masks:
            qa = jnp.where(m, q2, 0.0).astype(BF16)
            s = lax.dot_general(qa, k2, (((1,), (1,)), ((), ())), preferred_element_type=F32)
            e = jnp.exp(s - jnp.max(s, axis=-1, keepdims=True))
            den = jnp.sum(e, axis=-1, keepdims=True)
            outs.append(jnp.dot(e.astype(BF16), v2, preferred_element_type=F32) / den)
        o_ref[:, cols] = jnp.where(masks[0], outs[0], outs[1]).astype(o_ref.dtype)


def _attention_ctx(z_qkv, n_seq, seq):
    return pl.pallas_call(
        _attn_ctx_kernel,
        grid=(n_seq,),
        in_specs=[pl.BlockSpec((seq, NA_WIDTH), lambda b: (b, 0)),
                  pl.BlockSpec((seq, NA_WIDTH), lambda b: (b, 1)),
                  pl.BlockSpec((seq, NA_WIDTH), lambda b: (b, 2))],
        out_specs=pl.BlockSpec((seq, NA_WIDTH), lambda b: (b, 0)),
        out_shape=jax.ShapeDtypeStruct((n_seq * seq, NA_WIDTH), BF16),
        compiler_params=_params(1),
        name="attn_ctx",
    )(z_qkv, z_qkv, z_qkv)


def _attn_lat_kernel(q_ref, k_ref, v_ref, kc_ref, vc_ref, bias_ref, o_ref, *, tq):
    masks = _head_pair_masks()
    scale = NA_HEAD_DIM ** -0.5
    k2 = k_ref[...].astype(BF16)
    v2 = v_ref[...].astype(BF16)
    kc = kc_ref[...].astype(BF16)
    vc = vc_ref[...].astype(BF16)
    nt = (((1,), (1,)), ((), ()))

    def q_tile(qi, carry):
        rows = pl.ds(pl.multiple_of(qi * tq, tq), tq)
        q2 = q_ref[rows, :] * scale
        outs = []
        for hh, m in enumerate(masks):
            qa = jnp.where(m, q2, 0.0).astype(BF16)
            s_lat = lax.dot_general(qa, k2, nt, preferred_element_type=F32) + bias_ref[hh, rows, :]
            s_ctx = lax.dot_general(qa, kc, nt, preferred_element_type=F32)
            mx = jnp.maximum(jnp.max(s_lat, axis=-1, keepdims=True), jnp.max(s_ctx, axis=-1, keepdims=True))
            e_lat = jnp.exp(s_lat - mx)
            e_ctx = jnp.exp(s_ctx - mx)
            den = jnp.sum(e_lat, axis=-1, keepdims=True) + jnp.sum(e_ctx, axis=-1, keepdims=True)
            o = (jnp.dot(e_lat.astype(BF16), v2, preferred_element_type=F32)
                 + jnp.dot(e_ctx.astype(BF16), vc, preferred_element_type=F32))
            outs.append(o / den)
        o_ref[rows, :] = jnp.where(masks[0], outs[0], outs[1]).astype(o_ref.dtype)
        return carry

    lax.fori_loop(0, q_ref.shape[0] // tq, q_tile, 0)


def _attention_lat(z_qkv, cache_k, cache_v, bias, layer, row0, n_seq, seq):
    past = cache_k.shape[2]
    pairs = NA_WIDTH // LANES
    rb = row0 // seq
    kv_cols = NA_WIDTH // LANES
    return pl.pallas_call(
        functools.partial(_attn_lat_kernel, tq=256),
        grid=(pairs, n_seq),
        in_specs=[pl.BlockSpec((seq, LANES), lambda p, b: (rb + b, p)),
                  pl.BlockSpec((seq, LANES), lambda p, b: (rb + b, kv_cols + p)),
                  pl.BlockSpec((seq, LANES), lambda p, b: (rb + b, 2 * kv_cols + p)),
                  pl.BlockSpec((None, None, past, LANES), lambda p, b: (b, layer, 0, p)),
                  pl.BlockSpec((None, None, past, LANES), lambda p, b: (b, layer, 0, p)),
                  pl.BlockSpec((None, 2, seq, seq), lambda p, b: (layer, p, 0, 0))],
        out_specs=pl.BlockSpec((seq, LANES), lambda p, b: (b, p)),
        out_shape=jax.ShapeDtypeStruct((n_seq * seq, NA_WIDTH), BF16),
        compiler_params=_params(2),
        name="attn_lat",
    )(z_qkv, z_qkv, z_qkv, cache_k, cache_v, bias)


def _neighbourhood_bias(rpb, seq):
    rows = seq // GRID_W
    kh = WIN_H
    assert rows >= WIN_H
    lead = rpb.shape[:-2]
    c = np.arange(GRID_W)
    q_cs = np.clip(c - WIN_W // 2, 0, GRID_W - WIN_W)
    col_ok = (c[None, :] >= q_cs[:, None]) & (c[None, :] < q_cs[:, None] + WIN_W)
    pad_c = GRID_W - WIN_W
    pc = jnp.pad(rpb, ((0, 0),) * (rpb.ndim - 1) + ((pad_c, pad_c),), mode="edge")
    w = jnp.stack([pc[..., GRID_W - 1 - cq:2 * GRID_W - 1 - cq] for cq in range(GRID_W)], axis=-3)
    w = jnp.where(jnp.asarray(col_ok)[:, None, :], w, NEG_INF)
    r = np.arange(rows)
    rs = np.clip(r - kh // 2, 0, rows - kh)
    blocks = []
    for rq in range(rows):
        lo = int(rs[rq]) - rq + WIN_H - 1
        slab = w[..., lo:lo + kh, :].reshape(lead + (GRID_W, kh * GRID_W))
        pad = ((0, 0),) * (len(lead) + 1) + ((int(rs[rq]) * GRID_W, (rows - kh - int(rs[rq])) * GRID_W),)
        blocks.append(jnp.pad(slab, pad, constant_values=NEG_INF))
    return jnp.stack(blocks, axis=-3).reshape(lead + (seq, seq))


def _fourier_kernel(u_ref, ct2_ref, cc_ref, sc_ref, o_ref, pq_ref):
    t = u_ref.shape[0]
    for g in range(FOURIER_GROUPS):
        cols = slice(FOURIER_GROUP_DIM * g, FOURIER_GROUP_DIM * (g + 1))
        ug = u_ref[:, cols].astype(BF16)
        pq_ref[0:t, cols] = jnp.dot(ug, cc_ref[...], preferred_element_type=F32).astype(BF16)
        pq_ref[t:2 * t, cols] = jnp.dot(ug, sc_ref[...], preferred_element_type=F32).astype(BF16)
    o_ref[...] = jnp.dot(ct2_ref[...], pq_ref[...], preferred_element_type=F32).astype(o_ref.dtype)


def _dft_tables(t):
    def cs(n):
        k = np.arange(n, dtype=np.int64)
        ang = 2.0 * np.pi * ((k[:, None] * k[None, :]) % n).astype(np.float64) / n
        return np.cos(ang) / np.sqrt(n), np.sin(ang) / np.sqrt(n)

    ct, st = cs(t)
    cc, sc = cs(FOURIER_GROUP_DIM)
    ct2 = np.concatenate([ct, -st], axis=1).astype(np.float32)
    return (jnp.asarray(ct2).astype(BF16), jnp.asarray(cc.astype(np.float32)).astype(BF16),
            jnp.asarray(sc.astype(np.float32)).astype(BF16))


def _fourier(z_qkvu, row0, n_seq, seq):
    ct2, cc, sc = _dft_tables(seq)
    width = FOURIER_GROUPS * FOURIER_GROUP_DIM
    rb = row0 // seq
    ucol = 3 * NA_WIDTH // width
    return pl.pallas_call(
        _fourier_kernel,
        grid=(n_seq,),
        in_specs=[pl.BlockSpec((seq, width), lambda b: (rb + b, ucol)),
                  pl.BlockSpec((seq, 2 * seq), lambda b: (0, 0)),
                  pl.BlockSpec((FOURIER_GROUP_DIM, FOURIER_GROUP_DIM), lambda b: (0, 0)),
                  pl.BlockSpec((FOURIER_GROUP_DIM, FOURIER_GROUP_DIM), lambda b: (0, 0))],
        out_specs=pl.BlockSpec((seq, width), lambda b: (b, 0)),
        out_shape=jax.ShapeDtypeStruct((n_seq * seq, width), BF16),
        scratch_shapes=[pltpu.VMEM((2 * seq, width), BF16)],
        compiler_params=_params(1),
        name="fourier",
    )(z_qkvu, ct2, cc, sc)


def _rotary_tables(t):
    pos = np.arange(t)
    row = (pos // GRID_W).astype(np.float64)
    col = (pos % GRID_W).astype(np.float64)
    nf = RET_KEY_DIM // 4
    inv_freq = ROPE_BASE ** (-np.arange(nf, dtype=np.float64) / nf)
    ar = row[:, None] * inv_freq[None]
    ac = col[:, None] * inv_freq[None]
    cos = np.concatenate([np.cos(ar), np.cos(ar), np.cos(ac), np.cos(ac)], axis=1)
    sin = np.concatenate([-np.sin(ar), np.sin(ar), -np.sin(ac), np.sin(ac)], axis=1)
    return jnp.asarray(cos.astype(np.float32)), jnp.asarray(sin.astype(np.float32))


def _ret_kernel(lgf_ref, lgb_ref, *refs, t, tq, layer, rotary, state_in, state_out):
    refs = list(refs)
    q_ref, k_ref, v_ref, g_ref = refs[:4]
    refs = refs[4:]
    if rotary:
        cos_ref, sin_ref = refs[:2]
        refs = refs[2:]
    if state_in:
        sf0_ref, sb0_ref = refs[:2]
        refs = refs[2:]
    o_ref = refs[0]
    refs = refs[1:]
    if state_out:
        sf_ref, sb_ref = refs[:2]
        refs = refs[2:]
    dec_ref, kb_ref = refs

    h = pl.program_id(0)
    lgf = lgf_ref[layer * RET_HEADS + h]
    lgb = lgb_ref[layer * RET_HEADS + h]
    scale = RET_KEY_DIM ** -0.5
    nq = t // tq

    @pl.when(pl.program_id(1) == 0)
    def _():
        def fill(ri, c):
            rows = pl.ds(pl.multiple_of(ri * tq, tq), tq)
            i = lax.broadcasted_iota(I32, (tq, t), 0) + ri * tq
            j = lax.broadcasted_iota(I32, (tq, t), 1)
            d = (i - j).astype(F32)
            m = jnp.exp(jnp.abs(d) * jnp.where(d > 0, lgf, lgb))
            dec_ref[rows, :] = jnp.where(d == 0, 2.0, m)
            return c

        lax.fori_loop(0, nq, fill, 0)

    if rotary:
        lane = lax.broadcasted_iota(I32, (1, LANES), 1)
        low = (lane % (RET_KEY_DIM // 2)) < (RET_KEY_DIM // 4)

        def rot(x, rows):
            swapped = jnp.where(low, pltpu.roll(x, LANES - RET_KEY_DIM // 4, 1), pltpu.roll(x, RET_KEY_DIM // 4, 1))
            return x * cos_ref[rows, :] + swapped * sin_ref[rows, :]
    else:
        def rot(x, rows):
            return x

    all_rows = slice(0, t)
    kr = rot(k_ref[...].astype(F32), all_rows)
    kb_ref[...] = kr.astype(BF16)
    vb = v_ref[...]

    if state_out:
        j = lax.broadcasted_iota(I32, (t, 1), 0).astype(F32)
        tn = (((0,), (0,)), ((), ()))
        kf = (kr * (scale * jnp.exp(lgf * (t - 1.0 - j)))).astype(BF16)
        kbw = (kr * (scale * jnp.exp(lgb * j))).astype(BF16)
        sf = lax.dot_general(kf, vb, tn, preferred_element_type=F32)
        sb = lax.dot_general(kbw, vb, tn, preferred_element_type=F32)
        if state_in:
            sf = sf + jnp.exp(lgf * t) * sf0_ref[...]
            sb = sb + jnp.exp(lgb * t) * sb0_ref[...]
        sf_ref[...] = sf
        sb_ref[...] = sb

    def q_tile(qi, carry):
        r0 = pl.multiple_of(qi * tq, tq)
        rows = pl.ds(r0, tq)
        qr = rot(q_ref[rows, :].astype(F32), rows)
        s = lax.dot_general((qr * scale).astype(BF16), kb_ref[...], (((1,), (1,)), ((), ())),
                            preferred_element_type=F32)
        y = jnp.dot((s * dec_ref[rows, :]).astype(BF16), vb, preferred_element_type=F32)
        if state_in:
            pos = (lax.broadcasted_iota(I32, (tq, 1), 0) + r0).astype(F32)
            qf = (qr * jnp.exp(lgf * (pos + 1.0))).astype(BF16)
            qb = (qr * jnp.exp(lgb * (t - pos))).astype(BF16)
            y = (y + jnp.dot(qf, sf0_ref[...].astype(BF16), preferred_element_type=F32)
                 + jnp.dot(qb, sb0_ref[...].astype(BF16), preferred_element_type=F32))
        mean = jnp.mean(y, axis=-1, keepdims=True)
        yc = y - mean
        yn = yc * lax.rsqrt(jnp.mean(yc * yc, axis=-1, keepdims=True) + EPS)
        g = g_ref[rows, :].astype(F32)
        o_ref[rows, :] = (g * jax.nn.sigmoid(g) * yn).astype(o_ref.dtype)
        return carry

    lax.fori_loop(0, nq, q_tile, 0)


def _retention(z_ret, lg_f, lg_b, layer, row0, n_seq, seq, *, rotary, states=None, state_out=False):
    rb = row0 // seq
    cb = BRANCH_WIDTH // LANES
    tq = min(seq, 256)
    state_in = states is not None
    in_specs = [pl.BlockSpec((seq, LANES), lambda h, b, *_: (rb + b, 0 * cb + h)),
                pl.BlockSpec((seq, LANES), lambda h, b, *_: (rb + b, 1 * cb + h)),
                pl.BlockSpec((seq, LANES), lambda h, b, *_: (rb + b, 2 * cb + h)),
                pl.BlockSpec((seq, LANES), lambda h, b, *_: (rb + b, 3 * cb + h))]
    args = [z_ret, z_ret, z_ret, z_ret]
    if rotary:
        cos, sin = _rotary_tables(seq)
        in_specs += [pl.BlockSpec((seq, LANES), lambda h, b, *_: (0, 0))] * 2
        args += [cos, sin]
    if state_in:
        st_spec = pl.BlockSpec((None, None, None, RET_KEY_DIM, RET_KEY_DIM), lambda h, b, *_: (b, layer, h, 0, 0))
        in_specs += [st_spec, st_spec]
        args += list(states)
    out_specs = [pl.BlockSpec((seq, LANES), lambda h, b, *_: (b, h))]
    out_shape = [jax.ShapeDtypeStruct((n_seq * seq, RET_HEADS * LANES), BF16)]
    if state_out:
        so = pl.BlockSpec((None, None, RET_KEY_DIM, RET_KEY_DIM), lambda h, b, *_: (b, h, 0, 0))
        out_specs += [so, so]
        out_shape += [jax.ShapeDtypeStruct((n_seq, RET_HEADS, RET_KEY_DIM, RET_KEY_DIM), F32)] * 2
    return pl.pallas_call(
        functools.partial(_ret_kernel, t=seq, tq=tq, layer=layer, rotary=rotary, state_in=state_in,
                          state_out=state_out),
        grid_spec=pltpu.PrefetchScalarGridSpec(
            num_scalar_prefetch=2, grid=(RET_HEADS, n_seq),
            in_specs=in_specs, out_specs=out_specs,
            scratch_shapes=[pltpu.VMEM((seq, seq), F32), pltpu.VMEM((seq, LANES), BF16)]),
        out_shape=out_shape,
        compiler_params=_params(2),
        name="retention",
    )(lg_f, lg_b, *args)


def _split_dot_nt(w, x):
    nt = (((1,), (1,)), ((), ()))
    w_hi = w.astype(BF16)
    w_lo = (w - w_hi.astype(F32)).astype(BF16)
    x_hi = x.astype(BF16)
    x_lo = (x - x_hi.astype(F32)).astype(BF16)
    return (lax.dot_general(w_hi, x_hi, nt, preferred_element_type=F32)
            + lax.dot_general(w_hi, x_lo, nt, preferred_element_type=F32)
            + lax.dot_general(w_lo, x_hi, nt, preferred_element_type=F32))


def _finish_kernel(tmod_ref, ac_ref, al_ref, fc_ref, fl_ref, rc_ref, rl_ref, zg_ref, x_ref, mod_ref,
                   gpost_ref, gpre_ref, wb_ref, wo_ref, wrt_ref, br_ref, tri_ref,
                   x1_ref, h2_ref, idx_ref, wts_ref, rank_ref, cnt_ref, carry_ref, *, ctx_tiles):
    del tmod_ref

    @pl.when(pl.program_id(0) == 0)
    def _():
        carry_ref[...] = jnp.zeros_like(carry_ref)

    d = D_MODEL
    is_ctx = pl.program_id(0) < ctx_tiles

    def branch(c_ref, l_ref):
        return jnp.where(is_ctx, c_ref[...], l_ref[...])

    def gate(j):
        return jax.nn.sigmoid(zg_ref[:, d * j:d * (j + 1)].astype(F32))

    merged = (gate(0) * jnp.dot(branch(ac_ref, al_ref), wb_ref[0], preferred_element_type=F32)
              + gate(1) * jnp.dot(branch(fc_ref, fl_ref), wb_ref[1], preferred_element_type=F32)
              + gate(2) * jnp.dot(branch(rc_ref, rl_ref), wb_ref[2], preferred_element_type=F32))
    y = jnp.dot(merged.astype(BF16), wo_ref[...], preferred_element_type=F32)
    x1 = x_ref[...] + mod_ref[2:3, :] * (y * _rms(y) * gpost_ref[...])
    x1_ref[...] = x1
    h2 = x1 * _rms(x1) * gpre_ref[...] * (1.0 + mod_ref[4:5, :]) + mod_ref[3:4, :]
    h2_ref[...] = h2

    logits = _split_dot_nt(wrt_ref[...], h2) + br_ref[:, 0:1]
    tm = logits.shape[1]
    eidx = lax.broadcasted_iota(I32, (N_EXPERTS, tm), 0)
    cur = logits
    vals, hots = [], []
    for k in range(TOP_K):
        m = jnp.max(cur, axis=0, keepdims=True)
        sel = jnp.min(jnp.where(cur == m, eidx, N_EXPERTS), axis=0, keepdims=True)
        hot = eidx == sel
        vals.append(m)
        hots.append(hot)
        idx_ref[k:k + 1, :] = sel
        cur = jnp.where(hot, -jnp.inf, cur)
    exps = [jnp.exp(v - vals[0]) for v in vals]
    den = exps[0] + exps[1] + exps[2] + exps[3]
    for k in range(TOP_K):
        wts_ref[k:k + 1, :] = exps[k] / den

    member = jnp.logical_or(jnp.logical_or(hots[0], hots[1]), jnp.logical_or(hots[2], hots[3]))
    member_f = member.astype(F32)
    before = jnp.dot(member_f.astype(BF16), tri_ref[...], preferred_element_type=F32) + carry_ref[:, 0:1]
    for k in range(TOP_K):
        rank_ref[k:k + 1, :] = jnp.sum(jnp.where(hots[k], before, 0.0), axis=0, keepdims=True).astype(I32)
    carry_ref[...] = carry_ref[...] + jnp.sum(member_f, axis=1, keepdims=True)
    cnt_ref[...] = carry_ref[...]


def _finish(branches, z_gate, x, mod, g_post, g_pre_ffn, w_branch, w_out, w_router_t, b_router, layer, tile_mod):
    n, d = x.shape
    tm = FIN_TILE
    ctx_tiles = branches[0][0].shape[0] // tm
    tri = jnp.asarray(np.triu(np.ones((tm, tm), np.float32), k=1)).astype(BF16)
    row = lambda i, t: (i, 0)
    ctx_row = lambda i, t: (jnp.minimum(i, ctx_tiles - 1), 0)
    lat_row = lambda i, t: (jnp.maximum(i - ctx_tiles, 0), 0)
    const2 = lambda i, t: (0, 0)
    lay3 = lambda i, t: (layer, 0, 0)
    col = lambda i, t: (0, i)
    branch_specs, branch_args = [], []
    for c_arr, l_arr in branches:
        branch_specs += [pl.BlockSpec((tm, BRANCH_WIDTH), ctx_row), pl.BlockSpec((tm, BRANCH_WIDTH), lat_row)]
        branch_args += [c_arr, l_arr]
    outs = pl.pallas_call(
        functools.partial(_finish_kernel, ctx_tiles=ctx_tiles),
        grid_spec=pltpu.PrefetchScalarGridSpec(
            num_scalar_prefetch=1, grid=(n // tm,),
            in_specs=branch_specs + [
                      pl.BlockSpec((tm, GATE_W), row),
                      pl.BlockSpec((tm, d), row),
                      pl.BlockSpec((None, None, 6, d), lambda i, t: (layer, t[i], 0, 0)),
                      pl.BlockSpec((None, 1, d), lay3),
                      pl.BlockSpec((None, 1, d), lay3),
                      pl.BlockSpec((None, 3, BRANCH_WIDTH, d), lambda i, t: (layer, 0, 0, 0)),
                      pl.BlockSpec((None, d, d), lay3),
                      pl.BlockSpec((None, N_EXPERTS, d), lay3),
                      pl.BlockSpec((None, N_EXPERTS, LANES), lay3),
                      pl.BlockSpec((tm, tm), const2)],
            out_specs=[pl.BlockSpec((tm, d), row),
                       pl.BlockSpec((tm, d), row),
                       pl.BlockSpec((TOP_K, tm), col),
                       pl.BlockSpec((TOP_K, tm), col),
                       pl.BlockSpec((TOP_K, tm), col),
                       pl.BlockSpec((N_EXPERTS, LANES), const2)],
            scratch_shapes=[pltpu.VMEM((N_EXPERTS, LANES), F32)]),
        out_shape=[jax.ShapeDtypeStruct((n, d), F32),
                   jax.ShapeDtypeStruct((n, d), F32),
                   jax.ShapeDtypeStruct((TOP_K, n), I32),
                   jax.ShapeDtypeStruct((TOP_K, n), F32),
                   jax.ShapeDtypeStruct((TOP_K, n), I32),
                   jax.ShapeDtypeStruct((N_EXPERTS, LANES), F32)],
        compiler_params=_params(1),
        name="merge_router",
    )(tile_mod, *branch_args, z_gate, x, mod, g_post, g_pre_ffn, w_branch, w_out, w_router_t, b_router, tri)
    return outs


def _dispatch_kernel(dest_ref, h_ref, xb_in_hbm, xb_hbm, sem, *, tm):
    del xb_in_hbm

    def row_copy(src_row, dst_row):
        return pltpu.make_async_copy(h_ref.at[pl.ds(src_row, 1)], xb_hbm.at[pl.ds(dst_row, 1)], sem)

    def issue(t, c):
        for k in range(TOP_K):
            row_copy(t, dest_ref[k, t]).start()
        return c

    def drain(t, c):
        for k in range(TOP_K):
            row_copy(t, dest_ref[k, t]).wait()
        return c

    lax.fori_loop(0, tm, issue, 0)
    lax.fori_loop(0, tm, drain, 0)


def _dispatch(h2, dest_tiles, n_rows):
    n, d = h2.shape
    tm = dest_tiles.shape[2]
    return pl.pallas_call(
        functools.partial(_dispatch_kernel, tm=tm),
        grid=(n // tm,),
        in_specs=[pl.BlockSpec((None, TOP_K, tm), lambda i: (i, 0, 0), memory_space=pltpu.SMEM),
                  pl.BlockSpec((tm, d), lambda i: (i, 0)),
                  pl.BlockSpec(memory_space=pl.ANY)],
        out_specs=pl.BlockSpec(memory_space=pl.ANY),
        out_shape=jax.ShapeDtypeStruct((n_rows, d), F32),
        scratch_shapes=[pltpu.SemaphoreType.DMA],
        input_output_aliases={2: 0},
        compiler_params=_params(1),
        name="moe_dispatch",
    )(dest_tiles, h2, jnp.zeros((n_rows, d), F32))


def _expert_kernel(be_ref, nxt_ref, nu_ref, xb_ref, w1_hbm, b1_ref, w2_hbm, b2_ref, yb_ref,
                   w1s_ref, w2s_ref, w1b_ref, w2b_ref, sem, *, layer):
    i = pl.program_id(0)
    used = i < nu_ref[0]
    fresh = jnp.logical_or(i == 0, be_ref[i] != be_ref[jnp.maximum(i - 1, 0)])

    def fetch(e):
        return (pltpu.make_async_copy(w1_hbm.at[layer, e], w1s_ref, sem.at[0]),
                pltpu.make_async_copy(w2_hbm.at[layer, e], w2s_ref, sem.at[1]))

    @pl.when(i == 0)
    def _():
        for cp in fetch(be_ref[0]):
            cp.start()

    @pl.when(jnp.logical_and(used, fresh))
    def _():
        for cp in fetch(be_ref[i]):
            cp.wait()
        w1b_ref[...] = w1s_ref[...].astype(BF16)
        w2b_ref[...] = w2s_ref[...].astype(BF16)

        @pl.when(nxt_ref[i] >= 0)
        def _():
            for cp in fetch(nxt_ref[i]):
                cp.start()

    @pl.when(used)
    def _():
        z = jnp.dot(xb_ref[...].astype(BF16), w1b_ref[...], preferred_element_type=F32) + b1_ref[...]
        glu = jnp.minimum(z[:, :D_FF], SWIGLU_LIMIT)
        lin = jnp.clip(z[:, D_FF:], -SWIGLU_LIMIT, SWIGLU_LIMIT)
        act = glu * jax.nn.sigmoid(SWIGLU_ALPHA * glu) * (lin + 1.0)
        yb_ref[...] = jnp.dot(act.astype(BF16), w2b_ref[...], preferred_element_type=F32) + b2_ref[...]

    @pl.when(jnp.logical_not(used))
    def _():
        yb_ref[...] = jnp.zeros_like(yb_ref)


def _experts(xb, block_expert, next_expert, n_used, w1, b1, w2, b2, layer):
    n_rows, d = xb.shape
    tm = MOE_BLOCK
    return pl.pallas_call(
        functools.partial(_expert_kernel, layer=layer),
        grid_spec=pltpu.PrefetchScalarGridSpec(
            num_scalar_prefetch=3, grid=(n_rows // tm,),
            in_specs=[pl.BlockSpec((tm, d), lambda i, be, nx, nu: (jnp.minimum(i, nu[0] - 1), 0)),
                      pl.BlockSpec(memory_space=pl.ANY),
                      pl.BlockSpec((None, None, 1, 2 * D_FF), lambda i, be, nx, nu: (layer, be[i], 0, 0)),
                      pl.BlockSpec(memory_space=pl.ANY),
                      pl.BlockSpec((None, None, 1, d), lambda i, be, nx, nu: (layer, be[i], 0, 0))],
            out_specs=pl.BlockSpec((tm, d), lambda i, be, nx, nu: (i, 0)),
            scratch_shapes=[pltpu.VMEM((d, 2 * D_FF), F32), pltpu.VMEM((D_FF, d), F32),
                            pltpu.VMEM((d, 2 * D_FF), BF16), pltpu.VMEM((D_FF, d), BF16),
                            pltpu.SemaphoreType.DMA((2,))]),
        out_shape=jax.ShapeDtypeStruct((n_rows, d), F32),
        compiler_params=_params(1),
        name="moe_experts",
    )(block_expert, next_expert, n_used, xb, w1, b1, w2, b2)


def _combine_kernel(tmod_ref, dest_ref, yb_hbm, wts_ref, x1_ref, mod_ref, g_ref, o_ref, buf_ref, sem, *, tm):
    del tmod_ref

    def row_copy(k, t):
        return pltpu.make_async_copy(yb_hbm.at[pl.ds(dest_ref[k, t], 1)], buf_ref.at[k, pl.ds(t, 1)], sem)

    def issue(t, c):
        for k in range(TOP_K):
            row_copy(k, t).start()
        return c

    def drain(t, c):
        for k in range(TOP_K):
            row_copy(k, t).wait()
        return c

    lax.fori_loop(0, tm, issue, 0)
    lax.fori_loop(0, tm, drain, 0)
    y = wts_ref[:, 0:1] * buf_ref[0]
    for k in range(1, TOP_K):
        y = y + wts_ref[:, k:k + 1] * buf_ref[k]
    o_ref[...] = x1_ref[...] + mod_ref[5:6, :] * (y * _rms(y) * g_ref[...])


def _combine(yb, dest_tiles, wts, x1, mod, g_post, layer, tile_mod):
    n, d = x1.shape
    tm = dest_tiles.shape[2]
    return pl.pallas_call(
        functools.partial(_combine_kernel, tm=tm),
        grid_spec=pltpu.PrefetchScalarGridSpec(
            num_scalar_prefetch=1, grid=(n // tm,),
            in_specs=[pl.BlockSpec((None, TOP_K, tm), lambda i, t: (i, 0, 0), memory_space=pltpu.SMEM),
                      pl.BlockSpec(memory_space=pl.ANY),
                      pl.BlockSpec((tm, TOP_K), lambda i, t: (i, 0)),
                      pl.BlockSpec((tm, d), lambda i, t: (i, 0)),
                      pl.BlockSpec((None, None, 6, d), lambda i, t: (layer, t[i], 0, 0)),
                      pl.BlockSpec((None, 1, d), lambda i, t: (layer, 0, 0))],
            out_specs=pl.BlockSpec((tm, d), lambda i, t: (i, 0)),
            scratch_shapes=[pltpu.VMEM((TOP_K, tm, d), F32), pltpu.SemaphoreType.DMA]),
        out_shape=jax.ShapeDtypeStruct((n, d), F32),
        compiler_params=_params(1),
        name="moe_combine",
    )(tile_mod, dest_tiles, yb, wts, x1, mod, g_post)


def _tile_dest(dest, tm):
    k, n = dest.shape
    return dest.reshape(k, n // tm, tm).transpose(1, 0, 2)


def _moe(h2, idx, wts, rank, counts, x1, mod, g_post, w1, b1, w2, b2, layer, tile_mod_combine):
    n, d = h2.shape
    blk = MOE_BLOCK
    n_rows = -(-(n * TOP_K + N_EXPERTS * (blk - 1)) // blk) * blk
    n_blocks = n_rows // blk
    counts = counts[:, 0].astype(I32)
    padded = (counts + blk - 1) // blk * blk
    pend = jnp.cumsum(padded)
    pstart = pend - padded
    experts = jnp.arange(N_EXPERTS, dtype=I32)
    start_of = jnp.sum(jnp.where(idx[None] == experts[:, None, None], pstart[:, None, None], 0), axis=0)
    dest = (start_of + rank).astype(I32)
    blocks = jnp.arange(n_blocks, dtype=I32) * blk
    block_expert = jnp.minimum(jnp.sum(blocks[:, None] >= pend[None, :], axis=1), N_EXPERTS - 1).astype(I32)
    n_used = (pend[-1:] // blk).astype(I32)
    ids = jnp.arange(n_blocks, dtype=I32)
    run_start = jnp.logical_and(jnp.concatenate([jnp.ones((1,), bool), block_expert[1:] != block_expert[:-1]]),
                                ids < n_used[0])
    first_after = lax.cummin(jnp.where(run_start, ids, n_blocks)[::-1])[::-1]
    first_after = jnp.concatenate([first_after[1:], jnp.full((1,), n_blocks, I32)])
    next_expert = jnp.where(first_after < n_blocks, block_expert[jnp.minimum(first_after, n_blocks - 1)], -1)
    xb = _dispatch(h2, _tile_dest(dest, DISPATCH_TILE), n_rows)
    yb = _experts(xb, block_expert, next_expert.astype(I32), n_used, w1, b1, w2, b2, layer)
    return _combine(yb, _tile_dest(dest, COMBINE_TILE), wts.T, x1, mod, g_post, layer, tile_mod_combine)


def _tile_mod_ids(n_ctx_rows, n_lat_rows, lat_seq, tm):
    ctx = np.zeros((n_ctx_rows // tm,), np.int32)
    lat = 1 + (np.arange(n_lat_rows // tm) * tm) // lat_seq
    return jnp.asarray(np.concatenate([ctx, lat.astype(np.int32)]))


def kernel(x_prompt, x_sample, cache_k, cache_v, state_ret_fwd, state_ret_bwd, c, c_ctx, w_mod, b_mod, g_pre_mix, g_post_mix, g_pre_ffn, g_post_ffn, w_in, na_rel_bias, ret_decay_fwd, ret_decay_bwd, w_branch, w_out, w_router, b_router, w_exp_in, b_exp_in, w_exp_out, b_exp_out):
    batch, seq, d = x_prompt.shape
    dec_batch, dec_seq, _ = x_sample.shape
    depth = w_in.shape[0]
    n_ctx = batch * seq
    n_lat = dec_batch * dec_seq
    assert 1 + dec_batch <= MOD_ROWS

    x = jnp.concatenate([x_prompt.reshape(n_ctx, d), x_sample.reshape(n_lat, d)], axis=0)
    cvec = jnp.concatenate([c_ctx[None], c, jnp.zeros((MOD_ROWS - 1 - dec_batch, d), F32)], axis=0)
    mod_all = _modulation(cvec, w_mod, b_mod).reshape(depth, MOD_ROWS, 6, d)
    tmod = {tm: _tile_mod_ids(n_ctx, n_lat, dec_seq, tm) for tm in (ROW_TILE, FIN_TILE, COMBINE_TILE)}
    past = cache_k.shape[2]
    ck = cache_k.reshape(dec_batch, depth, past, NA_WIDTH)
    cv = cache_v.reshape(dec_batch, depth, past, NA_WIDTH)
    lg_f = jax.nn.log_sigmoid(ret_decay_fwd.astype(F32)).reshape(-1)
    lg_b = jax.nn.log_sigmoid(ret_decay_bwd.astype(F32)).reshape(-1)
    bias_all = _neighbourhood_bias(na_rel_bias, dec_seq)
    vec = lambda g: g.reshape(depth, 1, d)
    g_pre_mix, g_post_mix, g_pre_ffn, g_post_ffn = vec(g_pre_mix), vec(g_post_mix), vec(g_pre_ffn), vec(g_post_ffn)
    w_branch_b = w_branch.astype(BF16)
    w_out_b = w_out.astype(BF16)
    w_router_t = jnp.swapaxes(w_router, 1, 2)
    b_router_l = jnp.broadcast_to(b_router[:, :, None], (depth, N_EXPERTS, LANES))
    b_exp_in = b_exp_in.reshape(depth, N_EXPERTS, 1, 2 * D_FF)
    b_exp_out = b_exp_out.reshape(depth, N_EXPERTS, 1, d)

    ks, vs, sfs, sbs = [], [], [], []
    for l in range(depth):
        h = _prenorm(x, g_pre_mix, mod_all, l, tmod[ROW_TILE], ROW_TILE)
        z_qkvu = _project(h, w_in, l, 0, QKVU_W, F32)
        z_ret = _project(h, w_in, l, QKVU_W, RET_W, BF16)
        z_gate = _project(h, w_in, l, QKVU_W + RET_W, GATE_W, BF16)
        ks.append(z_qkvu[:n_ctx, NA_WIDTH:2 * NA_WIDTH].reshape(batch, seq, NA_HEADS, NA_HEAD_DIM))
        vs.append(z_qkvu[:n_ctx, 2 * NA_WIDTH:3 * NA_WIDTH].reshape(batch, seq, NA_HEADS, NA_HEAD_DIM))

        a_pair = (_attention_ctx(z_qkvu, batch, seq),
                  _attention_lat(z_qkvu, ck, cv, bias_all, l, n_ctx, dec_batch, dec_seq))
        f_pair = (_fourier(z_qkvu, 0, batch, seq), _fourier(z_qkvu, n_ctx, dec_batch, dec_seq))
        r_ctx, s_f, s_b = _retention(z_ret, lg_f, lg_b, l, 0, batch, seq, rotary=False, state_out=True)
        (r_lat,) = _retention(z_ret, lg_f, lg_b, l, n_ctx, dec_batch, dec_seq, rotary=True,
                              states=(state_ret_fwd, state_ret_bwd))
        sfs.append(s_f)
        sbs.append(s_b)

        x1, h2, idx, wts, rank, counts = _finish(
            (a_pair, f_pair, (r_ctx, r_lat)), z_gate, x, mod_all, g_post_mix, g_pre_ffn,
            w_branch_b, w_out_b, w_router_t, b_router_l, l, tmod[FIN_TILE])
        x = _moe(h2, idx, wts, rank, counts, x1, mod_all, g_post_ffn,
                 w_exp_in, b_exp_in, w_exp_out, b_exp_out, l, tmod[COMBINE_TILE])

    y_prompt = x[:n_ctx].reshape(batch, seq, d)
    y_sample = x[n_ctx:].reshape(dec_batch, dec_seq, d)
    return (y_prompt, y_sample, jnp.stack(ks, axis=1), jnp.stack(vs, axis=1),
            jnp.stack(sfs, axis=1), jnp.stack(sbs, axis=1))
```

```python
import functools

import numpy as np
import jax
import jax.numpy as jnp
from jax import lax
from jax.experimental import pallas as pl
from jax.experimental.pallas import tpu as pltpu

F32 = jnp.float32
BF16 = jnp.bfloat16
I32 = jnp.int32

D_MODEL = 1024
GRID_W = 64
NA_HEADS = 8
NA_HEAD_DIM = 64
NA_WIDTH = NA_HEADS * NA_HEAD_DIM
WIN_H = 8
WIN_W = 16
FOURIER_GROUPS = 4
FOURIER_GROUP_DIM = 128
RET_HEADS = 4
RET_KEY_DIM = 128
ROPE_BASE = 10000.0
BRANCH_WIDTH = 512
N_EXPERTS = 32
TOP_K = 4
D_FF = 1024
SWIGLU_LIMIT = 7.0
SWIGLU_ALPHA = 1.702
EPS = 1e-6
NEG_INF = -1e30

QKVU_W = 4 * NA_WIDTH
RET_W = 4 * BRANCH_WIDTH
GATE_W = 3 * D_MODEL
PROJ_TILE = 1024

LANES = 128
MOD_ROWS = 16
ROW_TILE = 1024
FIN_TILE = 512
MOE_BLOCK = 256
SEG_ALIGN = 8
SORT_CHUNK = 256
SORT_ROWS = -(-(FIN_TILE * TOP_K + N_EXPERTS * (SEG_ALIGN - 1)) // SORT_CHUNK) * SORT_CHUNK
VMEM_LIMIT = 56 * 1024 * 1024


def _params(n_axes, vmem=VMEM_LIMIT):
    return pltpu.CompilerParams(dimension_semantics=("arbitrary",) * n_axes, vmem_limit_bytes=vmem)


def _rms(x):
    return lax.rsqrt(jnp.mean(x * x, axis=-1, keepdims=True) + EPS)


def _mod_kernel(cv_ref, w_ref, b_ref, o_ref):
    cv = cv_ref[...]
    s = (cv * jax.nn.sigmoid(cv)).astype(BF16)
    o_ref[...] = jnp.dot(s, w_ref[...].astype(BF16), preferred_element_type=F32) + b_ref[...]


def _modulation(cv, w_mod, b_mod):
    depth, d, n = w_mod.shape
    tn = 1536
    return pl.pallas_call(
        _mod_kernel,
        grid=(depth, n // tn),
        in_specs=[pl.BlockSpec((MOD_ROWS, d), lambda l, j: (0, 0)),
                  pl.BlockSpec((None, d, tn), lambda l, j: (l, 0, j)),
                  pl.BlockSpec((None, 1, tn), lambda l, j: (l, 0, j))],
        out_specs=pl.BlockSpec((None, MOD_ROWS, tn), lambda l, j: (l, 0, j)),
        out_shape=jax.ShapeDtypeStruct((depth, MOD_ROWS, n), F32),
        compiler_params=_params(2),
        name="modulation",
    )(cv, w_mod, b_mod.reshape(depth, 1, n))


def _prenorm_kernel(tmod_ref, x_ref, g_ref, mod_ref, o_ref):
    del tmod_ref
    x = x_ref[...]
    h = x * _rms(x) * g_ref[...]
    o_ref[...] = (h * (1.0 + mod_ref[1:2, :]) + mod_ref[0:1, :]).astype(o_ref.dtype)


def _prenorm(x, g, mod, layer, tile_mod, tm):
    n, d = x.shape
    return pl.pallas_call(
        _prenorm_kernel,
        grid_spec=pltpu.PrefetchScalarGridSpec(
            num_scalar_prefetch=1, grid=(n // tm,),
            in_specs=[pl.BlockSpec((tm, d), lambda i, t: (i, 0)),
                      pl.BlockSpec((None, 1, d), lambda i, t: (layer, 0, 0)),
                      pl.BlockSpec((None, None, 6, d), lambda i, t: (layer, t[i], 0, 0))],
            out_specs=pl.BlockSpec((tm, d), lambda i, t: (i, 0))),
        out_shape=jax.ShapeDtypeStruct((n, d), BF16),
        compiler_params=_params(1),
        name="prenorm",
    )(tile_mod, x, g, mod)


def _proj_kernel(h_ref, w_ref, o_ref, wb_ref):
    @pl.when(pl.program_id(1) == 0)
    def _():
        wb_ref[...] = w_ref[...].astype(BF16)

    o_ref[...] = jnp.dot(h_ref[...], wb_ref[...], preferred_element_type=F32).astype(o_ref.dtype)


def _project(h, w, layer, col0, width, out_dtype):
    n, d = h.shape
    tm = ROW_TILE
    tn = PROJ_TILE
    cb = col0 // tn
    return pl.pallas_call(
        _proj_kernel,
        grid=(width // tn, n // tm),
        in_specs=[pl.BlockSpec((tm, d), lambda j, i: (i, 0)),
                  pl.BlockSpec((None, d, tn), lambda j, i: (layer, 0, cb + j))],
        out_specs=pl.BlockSpec((tm, tn), lambda j, i: (i, j)),
        out_shape=jax.ShapeDtypeStruct((n, width), out_dtype),
        scratch_shapes=[pltpu.VMEM((d, tn), BF16)],
        compiler_params=_params(2),
        name="in_proj",
    )(h, w)


def _head_pair_masks():
    lane = lax.broadcasted_iota(I32, (1, LANES), 1)
    first = lane < NA_HEAD_DIM
    return first, jnp.logical_not(first)


def _attn_ctx_kernel(q_ref, k_ref, v_ref, o_ref):
    masks = _head_pair_masks()
    scale = NA_HEAD_DIM ** -0.5
    for p in range(NA_WIDTH // LANES):
        cols = slice(LANES * p, LANES * (p + 1))
        q2 = q_ref[:, cols] * scale
        k2 = k_ref[:, cols].astype(BF16)
        v2 = v_ref[:, cols].astype(BF16)
        outs = []
        for m in masks:
            qa = jnp.where(m, q2, 0.0).astype(BF16)
            s = lax.dot_general(qa, k2, (((1,), (1,)), ((), ())), preferred_element_type=F32)
            e = jnp.exp(s - jnp.max(s, axis=-1, keepdims=True))
            den = jnp.sum(e, axis=-1, keepdims=True)
            outs.append(jnp.dot(e.astype(BF16), v2, preferred_element_type=F32) / den)
        o_ref[:, cols] = jnp.where(masks[0], outs[0], outs[1]).astype(o_ref.dtype)


def _attention_ctx(z_qkv, n_seq, seq):
    return pl.pallas_call(
        _attn_ctx_kernel,
        grid=(n_seq,),
        in_specs=[pl.BlockSpec((seq, NA_WIDTH), lambda b: (b, 0)),
                  pl.BlockSpec((seq, NA_WIDTH), lambda b: (b, 1)),
                  pl.BlockSpec((seq, NA_WIDTH), lambda b: (b, 2))],
        out_specs=pl.BlockSpec((seq, NA_WIDTH), lambda b: (b, 0)),
        out_shape=jax.ShapeDtypeStruct((n_seq * seq, NA_WIDTH), BF16),
        compiler_params=_params(1),
        name="attn_ctx",
    )(z_qkv, z_qkv, z_qkv)


def _attn_lat_kernel(q_ref, k_ref, v_ref, kc_ref, vc_ref, bias_ref, o_ref, *, tq):
    masks = _head_pair_masks()
    scale = NA_HEAD_DIM ** -0.5
    k2 = k_ref[...].astype(BF16)
    v2 = v_ref[...].astype(BF16)
    kc = kc_ref[...].astype(BF16)
    vc = vc_ref[...].astype(BF16)
    nt = (((1,), (1,)), ((), ()))

    def q_tile(qi, carry):
        rows = pl.ds(pl.multiple_of(qi * tq, tq), tq)
        q2 = q_ref[rows, :] * scale
        outs = []
        for hh, m in enumerate(masks):
            qa = jnp.where(m, q2, 0.0).astype(BF16)
            s_lat = lax.dot_general(qa, k2, nt, preferred_element_type=F32) + bias_ref[hh, rows, :]
            s_ctx = lax.dot_general(qa, kc, nt, preferred_element_type=F32)
            mx = jnp.maximum(jnp.max(s_lat, axis=-1, keepdims=True), jnp.max(s_ctx, axis=-1, keepdims=True))
            e_lat = jnp.exp(s_lat - mx)
            e_ctx = jnp.exp(s_ctx - mx)
            den = jnp.sum(e_lat, axis=-1, keepdims=True) + jnp.sum(e_ctx, axis=-1, keepdims=True)
            o = (jnp.dot(e_lat.astype(BF16), v2, preferred_element_type=F32)
                 + jnp.dot(e_ctx.astype(BF16), vc, preferred_element_type=F32))
            outs.append(o / den)
        o_ref[rows, :] = jnp.where(masks[0], outs[0], outs[1]).astype(o_ref.dtype)
        return carry

    lax.fori_loop(0, q_ref.shape[0] // tq, q_tile, 0)


def _attention_lat(z_qkv, cache_k, cache_v, bias, layer, row0, n_seq, seq):
    past = cache_k.shape[2]
    pairs = NA_WIDTH // LANES
    rb = row0 // seq
    kv_cols = NA_WIDTH // LANES
    return pl.pallas_call(
        functools.partial(_attn_lat_kernel, tq=256),
        grid=(pairs, n_seq),
        in_specs=[pl.BlockSpec((seq, LANES), lambda p, b: (rb + b, p)),
                  pl.BlockSpec((seq, LANES), lambda p, b: (rb + b, kv_cols + p)),
                  pl.BlockSpec((seq, LANES), lambda p, b: (rb + b, 2 * kv_cols + p)),
                  pl.BlockSpec((None, None, past, LANES), lambda p, b: (b, layer, 0, p)),
                  pl.BlockSpec((None, None, past, LANES), lambda p, b: (b, layer, 0, p)),
                  pl.BlockSpec((None, 2, seq, seq), lambda p, b: (layer, p, 0, 0))],
        out_specs=pl.BlockSpec((seq, LANES), lambda p, b: (b, p)),
        out_shape=jax.ShapeDtypeStruct((n_seq * seq, NA_WIDTH), BF16),
        compiler_params=_params(2),
        name="attn_lat",
    )(z_qkv, z_qkv, z_qkv, cache_k, cache_v, bias)


def _neighbourhood_bias(rpb, seq):
    rows = seq // GRID_W
    kh = WIN_H
    assert rows >= WIN_H
    lead = rpb.shape[:-2]
    c = np.arange(GRID_W)
    q_cs = np.clip(c - WIN_W // 2, 0, GRID_W - WIN_W)
    col_ok = (c[None, :] >= q_cs[:, None]) & (c[None, :] < q_cs[:, None] + WIN_W)
    pad_c = GRID_W - WIN_W
    pc = jnp.pad(rpb, ((0, 0),) * (rpb.ndim - 1) + ((pad_c, pad_c),), mode="edge")
    w = jnp.stack([pc[..., GRID_W - 1 - cq:2 * GRID_W - 1 - cq] for cq in range(GRID_W)], axis=-3)
    w = jnp.where(jnp.asarray(col_ok)[:, None, :], w, NEG_INF)
    r = np.arange(rows)
    rs = np.clip(r - kh // 2, 0, rows - kh)
    blocks = []
    for rq in range(rows):
        lo = int(rs[rq]) - rq + WIN_H - 1
        slab = w[..., lo:lo + kh, :].reshape(lead + (GRID_W, kh * GRID_W))
        pad = ((0, 0),) * (len(lead) + 1) + ((int(rs[rq]) * GRID_W, (rows - kh - int(rs[rq])) * GRID_W),)
        blocks.append(jnp.pad(slab, pad, constant_values=NEG_INF))
    return jnp.stack(blocks, axis=-3).reshape(lead + (seq, seq))


def _fourier_kernel(u_ref, ct2_ref, cc_ref, sc_ref, o_ref, pq_ref):
    t = u_ref.shape[0]
    for g in range(FOURIER_GROUPS):
        cols = slice(FOURIER_GROUP_DIM * g, FOURIER_GROUP_DIM * (g + 1))
        ug = u_ref[:, cols].astype(BF16)
        pq_ref[0:t, cols] = jnp.dot(ug, cc_ref[...], preferred_element_type=F32).astype(BF16)
        pq_ref[t:2 * t, cols] = jnp.dot(ug, sc_ref[...], preferred_element_type=F32).astype(BF16)
    o_ref[...] = jnp.dot(ct2_ref[...], pq_ref[...], preferred_element_type=F32).astype(o_ref.dtype)


def _dft_tables(t):
    def cs(n):
        k = np.arange(n, dtype=np.int64)
        ang = 2.0 * np.pi * ((k[:, None] * k[None, :]) % n).astype(np.float64) / n
        return np.cos(ang) / np.sqrt(n), np.sin(ang) / np.sqrt(n)

    ct, st = cs(t)
    cc, sc = cs(FOURIER_GROUP_DIM)
    ct2 = np.concatenate([ct, -st], axis=1).astype(np.float32)
    return (jnp.asarray(ct2).astype(BF16), jnp.asarray(cc.astype(np.float32)).astype(BF16),
            jnp.asarray(sc.astype(np.float32)).astype(BF16))


def _fourier(z_qkvu, row0, n_seq, seq):
    ct2, cc, sc = _dft_tables(seq)
    width = FOURIER_GROUPS * FOURIER_GROUP_DIM
    rb = row0 // seq
    ucol = 3 * NA_WIDTH // width
    return pl.pallas_call(
        _fourier_kernel,
        grid=(n_seq,),
        in_specs=[pl.BlockSpec((seq, width), lambda b: (rb + b, ucol)),
                  pl.BlockSpec((seq, 2 * seq), lambda b: (0, 0)),
                  pl.BlockSpec((FOURIER_GROUP_DIM, FOURIER_GROUP_DIM), lambda b: (0, 0)),
                  pl.BlockSpec((FOURIER_GROUP_DIM, FOURIER_GROUP_DIM), lambda b: (0, 0))],
        out_specs=pl.BlockSpec((seq, width), lambda b: (b, 0)),
        out_shape=jax.ShapeDtypeStruct((n_seq * seq, width), BF16),
        scratch_shapes=[pltpu.VMEM((2 * seq, width), BF16)],
        compiler_params=_params(1),
        name="fourier",
    )(z_qkvu, ct2, cc, sc)


def _rotary_tables(t):
    pos = np.arange(t)
    row = (pos // GRID_W).astype(np.float64)
    col = (pos % GRID_W).astype(np.float64)
    nf = RET_KEY_DIM // 4
    inv_freq = ROPE_BASE ** (-np.arange(nf, dtype=np.float64) / nf)
    ar = row[:, None] * inv_freq[None]
    ac = col[:, None] * inv_freq[None]
    cos = np.concatenate([np.cos(ar), np.cos(ar), np.cos(ac), np.cos(ac)], axis=1)
    sin = np.concatenate([-np.sin(ar), np.sin(ar), -np.sin(ac), np.sin(ac)], axis=1)
    return jnp.asarray(cos.astype(np.float32)), jnp.asarray(sin.astype(np.float32))


def _ret_kernel(lgf_ref, lgb_ref, *refs, t, tq, layer, rotary, state_in, state_out):
    refs = list(refs)
    q_ref, k_ref, v_ref, g_ref = refs[:4]
    refs = refs[4:]
    if rotary:
        cos_ref, sin_ref = refs[:2]
        refs = refs[2:]
    if state_in:
        sf0_ref, sb0_ref = refs[:2]
        refs = refs[2:]
    o_ref = refs[0]
    refs = refs[1:]
    if state_out:
        sf_ref, sb_ref = refs[:2]
        refs = refs[2:]
    dec_ref, kb_ref = refs

    h = pl.program_id(0)
    lgf = lgf_ref[layer * RET_HEADS + h]
    lgb = lgb_ref[layer * RET_HEADS + h]
    scale = RET_KEY_DIM ** -0.5
    nq = t // tq

    @pl.when(pl.program_id(1) == 0)
    def _():
        def fill(ri, c):
            rows = pl.ds(pl.multiple_of(ri * tq, tq), tq)
            i = lax.broadcasted_iota(I32, (tq, t), 0) + ri * tq
            j = lax.broadcasted_iota(I32, (tq, t), 1)
            d = (i - j).astype(F32)
            m = jnp.exp(jnp.abs(d) * jnp.where(d > 0, lgf, lgb))
            dec_ref[rows, :] = jnp.where(d == 0, 2.0, m)
            return c

        lax.fori_loop(0, nq, fill, 0)

    if rotary:
        lane = lax.broadcasted_iota(I32, (1, LANES), 1)
        low = (lane % (RET_KEY_DIM // 2)) < (RET_KEY_DIM // 4)

        def rot(x, rows):
            swapped = jnp.where(low, pltpu.roll(x, LANES - RET_KEY_DIM // 4, 1), pltpu.roll(x, RET_KEY_DIM // 4, 1))
            return x * cos_ref[rows, :] + swapped * sin_ref[rows, :]
    else:
        def rot(x, rows):
            return x

    all_rows = slice(0, t)
    kr = rot(k_ref[...].astype(F32), all_rows)
    kb_ref[...] = kr.astype(BF16)
    vb = v_ref[...]

    if state_out:
        j = lax.broadcasted_iota(I32, (t, 1), 0).astype(F32)
        tn = (((0,), (0,)), ((), ()))
        kf = (kr * (scale * jnp.exp(lgf * (t - 1.0 - j)))).astype(BF16)
        kbw = (kr * (scale * jnp.exp(lgb * j))).astype(BF16)
        sf = lax.dot_general(kf, vb, tn, preferred_element_type=F32)
        sb = lax.dot_general(kbw, vb, tn, preferred_element_type=F32)
        if state_in:
            sf = sf + jnp.exp(lgf * t) * sf0_ref[...]
            sb = sb + jnp.exp(lgb * t) * sb0_ref[...]
        sf_ref[...] = sf
        sb_ref[...] = sb

    def q_tile(qi, carry):
        r0 = pl.multiple_of(qi * tq, tq)
        rows = pl.ds(r0, tq)
        qr = rot(q_ref[rows, :].astype(F32), rows)
        s = lax.dot_general((qr * scale).astype(BF16), kb_ref[...], (((1,), (1,)), ((), ())),
                            preferred_element_type=F32)
        y = jnp.dot((s * dec_ref[rows, :]).astype(BF16), vb, preferred_element_type=F32)
        if state_in:
            pos = (lax.broadcasted_iota(I32, (tq, 1), 0) + r0).astype(F32)
            qf = (qr * jnp.exp(lgf * (pos + 1.0))).astype(BF16)
            qb = (qr * jnp.exp(lgb * (t - pos))).astype(BF16)
            y = (y + jnp.dot(qf, sf0_ref[...].astype(BF16), preferred_element_type=F32)
                 + jnp.dot(qb, sb0_ref[...].astype(BF16), preferred_element_type=F32))
        mean = jnp.mean(y, axis=-1, keepdims=True)
        yc = y - mean
        yn = yc * lax.rsqrt(jnp.mean(yc * yc, axis=-1, keepdims=True) + EPS)
        g = g_ref[rows, :].astype(F32)
        o_ref[rows, :] = (g * jax.nn.sigmoid(g) * yn).astype(o_ref.dtype)
        return carry

    lax.fori_loop(0, nq, q_tile, 0)


def _retention(z_ret, lg_f, lg_b, layer, row0, n_seq, seq, *, rotary, states=None, state_out=False):
    rb = row0 // seq
    cb = BRANCH_WIDTH // LANES
    tq = min(seq, 256)
    state_in = states is not None
    in_specs = [pl.BlockSpec((seq, LANES), lambda h, b, *_: (rb + b, 0 * cb + h)),
                pl.BlockSpec((seq, LANES), lambda h, b, *_: (rb + b, 1 * cb + h)),
                pl.BlockSpec((seq, LANES), lambda h, b, *_: (rb + b, 2 * cb + h)),
                pl.BlockSpec((seq, LANES), lambda h, b, *_: (rb + b, 3 * cb + h))]
    args = [z_ret, z_ret, z_ret, z_ret]
    if rotary:
        cos, sin = _rotary_tables(seq)
        in_specs += [pl.BlockSpec((seq, LANES), lambda h, b, *_: (0, 0))] * 2
        args += [cos, sin]
    if state_in:
        st_spec = pl.BlockSpec((None, None, None, RET_KEY_DIM, RET_KEY_DIM), lambda h, b, *_: (b, layer, h, 0, 0))
        in_specs += [st_spec, st_spec]
        args += list(states)
    out_specs = [pl.BlockSpec((seq, LANES), lambda h, b, *_: (b, h))]
    out_shape = [jax.ShapeDtypeStruct((n_seq * seq, RET_HEADS * LANES), BF16)]
    if state_out:
        so = pl.BlockSpec((None, None, RET_KEY_DIM, RET_KEY_DIM), lambda h, b, *_: (b, h, 0, 0))
        out_specs += [so, so]
        out_shape += [jax.ShapeDtypeStruct((n_seq, RET_HEADS, RET_KEY_DIM, RET_KEY_DIM), F32)] * 2
    return pl.pallas_call(
        functools.partial(_ret_kernel, t=seq, tq=tq, layer=layer, rotary=rotary, state_in=state_in,
                          state_out=state_out),
        grid_spec=pltpu.PrefetchScalarGridSpec(
            num_scalar_prefetch=2, grid=(RET_HEADS, n_seq),
            in_specs=in_specs, out_specs=out_specs,
            scratch_shapes=[pltpu.VMEM((seq, seq), F32), pltpu.VMEM((seq, LANES), BF16)]),
        out_shape=out_shape,
        compiler_params=_params(2),
        name="retention",
    )(lg_f, lg_b, *args)


def _split_dot_nt(w, x):
    nt = (((1,), (1,)), ((), ()))
    w_hi = w.astype(BF16)
    w_lo = (w - w_hi.astype(F32)).astype(BF16)
    x_hi = x.astype(BF16)
    x_lo = (x - x_hi.astype(F32)).astype(BF16)
    return (lax.dot_general(w_hi, x_hi, nt, preferred_element_type=F32)
            + lax.dot_general(w_hi, x_lo, nt, preferred_element_type=F32)
            + lax.dot_general(w_lo, x_hi, nt, preferred_element_type=F32))


def _finish_kernel(tmod_ref, ac_ref, al_ref, fc_ref, fl_ref, rc_ref, rl_ref, zg_ref, x_ref, mod_ref,
                   gpost_ref, gpre_ref, wb_ref, wo_ref, wrt_ref, br_ref, tri_ref,
                   x1_ref, h2_ref, idx_ref, wts_ref, rank_ref, cnt_ref, *, ctx_tiles):
    del tmod_ref
    d = D_MODEL
    is_ctx = pl.program_id(0) < ctx_tiles

    def branch(c_ref, l_ref):
        return jnp.where(is_ctx, c_ref[...], l_ref[...])

    def gate(j):
        return jax.nn.sigmoid(zg_ref[:, d * j:d * (j + 1)].astype(F32))

    merged = (gate(0) * jnp.dot(branch(ac_ref, al_ref), wb_ref[0], preferred_element_type=F32)
              + gate(1) * jnp.dot(branch(fc_ref, fl_ref), wb_ref[1], preferred_element_type=F32)
              + gate(2) * jnp.dot(branch(rc_ref, rl_ref), wb_ref[2], preferred_element_type=F32))
    y = jnp.dot(merged.astype(BF16), wo_ref[...], preferred_element_type=F32)
    x1 = x_ref[...] + mod_ref[2:3, :] * (y * _rms(y) * gpost_ref[...])
    x1_ref[...] = x1
    h2 = x1 * _rms(x1) * gpre_ref[...] * (1.0 + mod_ref[4:5, :]) + mod_ref[3:4, :]
    h2_ref[...] = h2

    logits = _split_dot_nt(wrt_ref[...], h2) + br_ref[:, 0:1]
    tm = logits.shape[1]
    eidx = lax.broadcasted_iota(I32, (N_EXPERTS, tm), 0)
    cur = logits
    vals, hots = [], []
    for k in range(TOP_K):
        m = jnp.max(cur, axis=0, keepdims=True)
        sel = jnp.min(jnp.where(cur == m, eidx, N_EXPERTS), axis=0, keepdims=True)
        hot = eidx == sel
        vals.append(m)
        hots.append(hot)
        idx_ref[k:k + 1, :] = sel
        cur = jnp.where(hot, -jnp.inf, cur)
    exps = [jnp.exp(v - vals[0]) for v in vals]
    den = exps[0] + exps[1] + exps[2] + exps[3]
    for k in range(TOP_K):
        wts_ref[k:k + 1, :] = exps[k] / den

    member = jnp.logical_or(jnp.logical_or(hots[0], hots[1]), jnp.logical_or(hots[2], hots[3]))
    member_f = member.astype(F32)
    before = jnp.dot(member_f.astype(BF16), tri_ref[...], preferred_element_type=F32)
    for k in range(TOP_K):
        rank_ref[k:k + 1, :] = jnp.sum(jnp.where(hots[k], before, 0.0), axis=0, keepdims=True).astype(I32)
    cnt_ref[...] = jnp.broadcast_to(jnp.sum(member_f, axis=1, keepdims=True), cnt_ref.shape)


def _finish(branches, z_gate, x, mod, g_post, g_pre_ffn, w_branch, w_out, w_router_t, b_router, layer, tile_mod):
    n, d = x.shape
    tm = FIN_TILE
    ctx_tiles = branches[0][0].shape[0] // tm
    tri = jnp.asarray(np.triu(np.ones((tm, tm), np.float32), k=1)).astype(BF16)
    row = lambda i, t: (i, 0)
    ctx_row = lambda i, t: (jnp.minimum(i, ctx_tiles - 1), 0)
    lat_row = lambda i, t: (jnp.maximum(i - ctx_tiles, 0), 0)
    const2 = lambda i, t: (0, 0)
    lay3 = lambda i, t: (layer, 0, 0)
    col = lambda i, t: (0, i)
    branch_specs, branch_args = [], []
    for c_arr, l_arr in branches:
        branch_specs += [pl.BlockSpec((tm, BRANCH_WIDTH), ctx_row), pl.BlockSpec((tm, BRANCH_WIDTH), lat_row)]
        branch_args += [c_arr, l_arr]
    outs = pl.pallas_call(
        functools.partial(_finish_kernel, ctx_tiles=ctx_tiles),
        grid_spec=pltpu.PrefetchScalarGridSpec(
            num_scalar_prefetch=1, grid=(n // tm,),
            in_specs=branch_specs + [
                      pl.BlockSpec((tm, GATE_W), row),
                      pl.BlockSpec((tm, d), row),
                      pl.BlockSpec((None, None, 6, d), lambda i, t: (layer, t[i], 0, 0)),
                      pl.BlockSpec((None, 1, d), lay3),
                      pl.BlockSpec((None, 1, d), lay3),
                      pl.BlockSpec((None, 3, BRANCH_WIDTH, d), lambda i, t: (layer, 0, 0, 0)),
                      pl.BlockSpec((None, d, d), lay3),
                      pl.BlockSpec((None, N_EXPERTS, d), lay3),
                      pl.BlockSpec((None, N_EXPERTS, LANES), lay3),
                      pl.BlockSpec((tm, tm), const2)],
            out_specs=[pl.BlockSpec((tm, d), row),
                       pl.BlockSpec((tm, d), row),
                       pl.BlockSpec((TOP_K, tm), col),
                       pl.BlockSpec((TOP_K, tm), col),
                       pl.BlockSpec((TOP_K, tm), col),
                       pl.BlockSpec((None, N_EXPERTS, LANES), lambda i, t: (i, 0, 0))]),
        out_shape=[jax.ShapeDtypeStruct((n, d), F32),
                   jax.ShapeDtypeStruct((n, d), F32),
                   jax.ShapeDtypeStruct((TOP_K, n), I32),
                   jax.ShapeDtypeStruct((TOP_K, n), F32),
                   jax.ShapeDtypeStruct((TOP_K, n), I32),
                   jax.ShapeDtypeStruct((n // tm, N_EXPERTS, LANES), F32)],
        compiler_params=_params(1),
        name="merge_router",
    )(tile_mod, *branch_args, z_gate, x, mod, g_post, g_pre_ffn, w_branch, w_out, w_router_t, b_router, tri)
    return outs


def _segment_chunks(length, src_ref, src0, dst_ref, dst0, sem, max_chunk, fixed_src=False):
    out = []
    chunk = max_chunk
    while chunk >= SEG_ALIGN:
        done = jnp.bitwise_and(length, ~(2 * chunk - 1))
        present = jnp.bitwise_and(length, chunk) != 0
        s = 0 if fixed_src else pl.multiple_of(src0 + done, SEG_ALIGN)
        dd = pl.multiple_of(dst0 + done, SEG_ALIGN)
        out.append((present, pltpu.make_async_copy(src_ref.at[pl.ds(s, chunk)], dst_ref.at[pl.ds(dd, chunk)], sem)))
        chunk //= 2
    return out


def _for_each_chunk(n_segments, chunks_of, action):
    def body(e, c):
        for present, cp in chunks_of(e):
            pl.when(present)(functools.partial(action, cp))
        return c

    lax.fori_loop(0, n_segments, body, 0)


def _start(cp):
    cp.start()


def _wait(cp):
    cp.wait()


def _scatter_kernel(seg_ref, off_ref, pos_ref, tpos_ref, tlen_ref, h_ref, lpos_ref, xe_hbm,
                    buf_ref, zero_ref, sem, *, tm):
    t = pl.program_id(0)
    hb = h_ref[...].astype(BF16)
    lp = [lpos_ref[k:k + 1, :] for k in range(TOP_K)]
    rows = buf_ref.shape[0]
    for c0 in range(0, rows, SORT_CHUNK):
        r = lax.broadcasted_iota(I32, (SORT_CHUNK, tm), 0) + c0
        hit = jnp.logical_or(jnp.logical_or(r == lp[0], r == lp[1]), jnp.logical_or(r == lp[2], r == lp[3]))
        onehot = jnp.where(hit, 1.0, 0.0).astype(BF16)
        buf_ref[c0:c0 + SORT_CHUNK, :] = jnp.dot(onehot, hb, preferred_element_type=F32)

    def segment(e):
        j = t * N_EXPERTS + e
        return _segment_chunks(seg_ref[j], buf_ref, off_ref[j], xe_hbm, pos_ref[j], sem, tm)

    _for_each_chunk(N_EXPERTS, segment, _start)
    _for_each_chunk(N_EXPERTS, segment, _wait)

    @pl.when(t == pl.num_programs(0) - 1)
    def _():
        zero_ref[...] = jnp.zeros_like(zero_ref)

        def tail(e):
            return _segment_chunks(tlen_ref[e], zero_ref, 0, xe_hbm, tpos_ref[e], sem, zero_ref.shape[0],
                                   fixed_src=True)

        _for_each_chunk(N_EXPERTS, tail, _start)
        _for_each_chunk(N_EXPERTS, tail, _wait)


def _scatter_rows(h2, lpos, plan, n_rows):
    n, d = h2.shape
    tm = FIN_TILE
    return pl.pallas_call(
        functools.partial(_scatter_kernel, tm=tm),
        grid_spec=pltpu.PrefetchScalarGridSpec(
            num_scalar_prefetch=5, grid=(n // tm,),
            in_specs=[pl.BlockSpec((tm, d), lambda i, *_: (i, 0)),
                      pl.BlockSpec((TOP_K, tm), lambda i, *_: (0, i))],
            out_specs=pl.BlockSpec(memory_space=pl.ANY),
            scratch_shapes=[pltpu.VMEM((SORT_ROWS, d), F32), pltpu.VMEM((MOE_BLOCK // 2, d), F32),
                            pltpu.SemaphoreType.DMA]),
        out_shape=jax.ShapeDtypeStruct((n_rows, d), F32),
        compiler_params=_params(1),
        name="moe_scatter",
    )(plan["seg"], plan["off"], plan["pos"], plan["tail_pos"], plan["tail_len"], h2, lpos)


def _expert_kernel(be_ref, nxt_ref, nu_ref, xb_ref, w1_hbm, b1_ref, w2_hbm, b2_ref, yb_ref,
                   w1s_ref, w2s_ref, w1b_ref, w2b_ref, sem, *, layer):
    i = pl.program_id(0)
    used = i < nu_ref[0]
    fresh = jnp.logical_or(i == 0, be_ref[i] != be_ref[jnp.maximum(i - 1, 0)])

    def fetch(e):
        return (pltpu.make_async_copy(w1_hbm.at[layer, e], w1s_ref, sem.at[0]),
                pltpu.make_async_copy(w2_hbm.at[layer, e], w2s_ref, sem.at[1]))

    @pl.when(i == 0)
    def _():
        for cp in fetch(be_ref[0]):
            cp.start()

    @pl.when(jnp.logical_and(used, fresh))
    def _():
        for cp in fetch(be_ref[i]):
            cp.wait()
        w1b_ref[...] = w1s_ref[...].astype(BF16)
        w2b_ref[...] = w2s_ref[...].astype(BF16)

        @pl.when(nxt_ref[i] >= 0)
        def _():
            for cp in fetch(nxt_ref[i]):
                cp.start()

    @pl.when(used)
    def _():
        z = jnp.dot(xb_ref[...].astype(BF16), w1b_ref[...], preferred_element_type=F32) + b1_ref[...]
        glu = jnp.minimum(z[:, :D_FF], SWIGLU_LIMIT)
        lin = jnp.clip(z[:, D_FF:], -SWIGLU_LIMIT, SWIGLU_LIMIT)
        act = glu * jax.nn.sigmoid(SWIGLU_ALPHA * glu) * (lin + 1.0)
        yb_ref[...] = jnp.dot(act.astype(BF16), w2b_ref[...], preferred_element_type=F32) + b2_ref[...]

    @pl.when(jnp.logical_not(used))
    def _():
        yb_ref[...] = jnp.zeros_like(yb_ref)


def _experts(xb, block_expert, next_expert, n_used, w1, b1, w2, b2, layer):
    n_rows, d = xb.shape
    tm = MOE_BLOCK
    return pl.pallas_call(
        functools.partial(_expert_kernel, layer=layer),
        grid_spec=pltpu.PrefetchScalarGridSpec(
            num_scalar_prefetch=3, grid=(n_rows // tm,),
            in_specs=[pl.BlockSpec((tm, d), lambda i, be, nx, nu: (jnp.minimum(i, nu[0] - 1), 0)),
                      pl.BlockSpec(memory_space=pl.ANY),
                      pl.BlockSpec((None, None, 1, 2 * D_FF), lambda i, be, nx, nu: (layer, be[i], 0, 0)),
                      pl.BlockSpec(memory_space=pl.ANY),
                      pl.BlockSpec((None, None, 1, d), lambda i, be, nx, nu: (layer, be[i], 0, 0))],
            out_specs=pl.BlockSpec((tm, d), lambda i, be, nx, nu: (i, 0)),
            scratch_shapes=[pltpu.VMEM((d, 2 * D_FF), F32), pltpu.VMEM((D_FF, d), F32),
                            pltpu.VMEM((d, 2 * D_FF), BF16), pltpu.VMEM((D_FF, d), BF16),
                            pltpu.SemaphoreType.DMA((2,))]),
        out_shape=jax.ShapeDtypeStruct((n_rows, d), F32),
        compiler_params=_params(1),
        name="moe_experts",
    )(block_expert, next_expert, n_used, xb, w1, b1, w2, b2)


def _gather_kernel(tmod_ref, seg_ref, off_ref, pos_ref, ye_hbm, lpos_ref, wts_ref, x1_ref, mod_ref, g_ref, o_ref,
                   buf_ref, sem, *, tm):
    del tmod_ref
    t = pl.program_id(0)

    @pl.when(t == 0)
    def _():
        buf_ref[...] = jnp.zeros_like(buf_ref)

    def segment(e):
        j = t * N_EXPERTS + e
        return _segment_chunks(seg_ref[j], ye_hbm, pos_ref[j], buf_ref, off_ref[j], sem, tm)

    _for_each_chunk(N_EXPERTS, segment, _start)
    _for_each_chunk(N_EXPERTS, segment, _wait)

    lp = [lpos_ref[:, k:k + 1] for k in range(TOP_K)]
    wt = [wts_ref[:, k:k + 1] for k in range(TOP_K)]
    y = jnp.zeros(o_ref.shape, F32)
    for c0 in range(0, buf_ref.shape[0], SORT_CHUNK):
        r = lax.broadcasted_iota(I32, (tm, SORT_CHUNK), 1) + c0
        wm = jnp.where(r == lp[0], wt[0], 0.0)
        for k in range(1, TOP_K):
            wm = jnp.where(r == lp[k], wt[k], wm)
        y = y + jnp.dot(wm.astype(BF16), buf_ref[c0:c0 + SORT_CHUNK, :].astype(BF16), preferred_element_type=F32)
    o_ref[...] = x1_ref[...] + mod_ref[5:6, :] * (y * _rms(y) * g_ref[...])


def _gather_combine(ye, lpos_t, wts_t, plan, x1, mod, g_post, layer, tile_mod):
    n, d = x1.shape
    tm = FIN_TILE
    row = lambda i, *_: (i, 0)
    return pl.pallas_call(
        functools.partial(_gather_kernel, tm=tm),
        grid_spec=pltpu.PrefetchScalarGridSpec(
            num_scalar_prefetch=4, grid=(n // tm,),
            in_specs=[pl.BlockSpec(memory_space=pl.ANY),
                      pl.BlockSpec((tm, TOP_K), row),
                      pl.BlockSpec((tm, TOP_K), row),
                      pl.BlockSpec((tm, d), row),
                      pl.BlockSpec((None, None, 6, d), lambda i, t, *_: (layer, t[i], 0, 0)),
                      pl.BlockSpec((None, 1, d), lambda i, *_: (layer, 0, 0))],
            out_specs=pl.BlockSpec((tm, d), row),
            scratch_shapes=[pltpu.VMEM((SORT_ROWS, d), F32), pltpu.SemaphoreType.DMA]),
        out_shape=jax.ShapeDtypeStruct((n, d), F32),
        compiler_params=_params(1),
        name="moe_gather",
    )(tile_mod, plan["seg"], plan["off"], plan["pos"], ye, lpos_t, wts_t, x1, mod, g_post)


def _moe(h2, idx, wts, rank, counts, x1, mod, g_post, w1, b1, w2, b2, layer, tile_mod):
    n, d = h2.shape
    blk = MOE_BLOCK
    tm = FIN_TILE
    tiles = n // tm
    n_rows = -(-(n * TOP_K + tiles * N_EXPERTS * (SEG_ALIGN - 1) + N_EXPERTS * (blk - 1)) // blk) * blk
    n_blocks = n_rows // blk
    cnt = counts[:, :, 0].astype(I32)
    seg = (cnt + SEG_ALIGN - 1) // SEG_ALIGN * SEG_ALIGN
    off = jnp.cumsum(seg, axis=1) - seg
    rows_e = jnp.sum(seg, axis=0)
    region = (rows_e + blk - 1) // blk * blk
    pend = jnp.cumsum(region)
    pstart = pend - region
    pos = pstart[None, :] + jnp.cumsum(seg, axis=0) - seg
    experts = jnp.arange(N_EXPERTS, dtype=I32)
    idx3 = idx.reshape(TOP_K, tiles, tm)
    off_sel = jnp.sum(jnp.where(idx3[None] == experts[:, None, None, None], off.T[:, None, :, None], 0), axis=0)
    lpos = (off_sel + rank.reshape(TOP_K, tiles, tm)).reshape(TOP_K, n).astype(I32)
    plan = {"seg": seg.reshape(-1), "off": off.reshape(-1).astype(I32), "pos": pos.reshape(-1).astype(I32),
            "tail_pos": (pstart + rows_e).astype(I32), "tail_len": (region - rows_e).astype(I32)}
    blocks = jnp.arange(n_blocks, dtype=I32) * blk
    block_expert = jnp.minimum(jnp.sum(blocks[:, None] >= pend[None, :], axis=1), N_EXPERTS - 1).astype(I32)
    n_used = (pend[-1:] // blk).astype(I32)
    ids = jnp.arange(n_blocks, dtype=I32)
    run_start = jnp.logical_and(jnp.concatenate([jnp.ones((1,), bool), block_expert[1:] != block_expert[:-1]]),
                                ids < n_used[0])
    first_after = lax.cummin(jnp.where(run_start, ids, n_blocks)[::-1])[::-1]
    first_after = jnp.concatenate([first_after[1:], jnp.full((1,), n_blocks, I32)])
    next_expert = jnp.where(first_after < n_blocks, block_expert[jnp.minimum(first_after, n_blocks - 1)], -1)
    xe = _scatter_rows(h2, lpos, plan, n_rows)
    ye = _experts(xe, block_expert, next_expert.astype(I32), n_used, w1, b1, w2, b2, layer)
    return _gather_combine(ye, lpos.T, wts.T, plan, x1, mod, g_post, layer, tile_mod)


def _tile_mod_ids(n_ctx_rows, n_lat_rows, lat_seq, tm):
    ctx = np.zeros((n_ctx_rows // tm,), np.int32)
    lat = 1 + (np.arange(n_lat_rows // tm) * tm) // lat_seq
    return jnp.asarray(np.concatenate([ctx, lat.astype(np.int32)]))


def kernel(x_prompt, x_sample, cache_k, cache_v, state_ret_fwd, state_ret_bwd, c, c_ctx, w_mod, b_mod, g_pre_mix, g_post_mix, g_pre_ffn, g_post_ffn, w_in, na_rel_bias, ret_decay_fwd, ret_decay_bwd, w_branch, w_out, w_router, b_router, w_exp_in, b_exp_in, w_exp_out, b_exp_out):
    batch, seq, d = x_prompt.shape
    dec_batch, dec_seq, _ = x_sample.shape
    depth = w_in.shape[0]
    n_ctx = batch * seq
    n_lat = dec_batch * dec_seq
    assert 1 + dec_batch <= MOD_ROWS

    x = jnp.concatenate([x_prompt.reshape(n_ctx, d), x_sample.reshape(n_lat, d)], axis=0)
    cvec = jnp.concatenate([c_ctx[None], c, jnp.zeros((MOD_ROWS - 1 - dec_batch, d), F32)], axis=0)
    mod_all = _modulation(cvec, w_mod, b_mod).reshape(depth, MOD_ROWS, 6, d)
    tmod = {tm: _tile_mod_ids(n_ctx, n_lat, dec_seq, tm) for tm in (ROW_TILE, FIN_TILE)}
    past = cache_k.shape[2]
    ck = cache_k.reshape(dec_batch, depth, past, NA_WIDTH)
    cv = cache_v.reshape(dec_batch, depth, past, NA_WIDTH)
    lg_f = jax.nn.log_sigmoid(ret_decay_fwd.astype(F32)).reshape(-1)
    lg_b = jax.nn.log_sigmoid(ret_decay_bwd.astype(F32)).reshape(-1)
    bias_all = _neighbourhood_bias(na_rel_bias, dec_seq)
    vec = lambda g: g.reshape(depth, 1, d)
    g_pre_mix, g_post_mix, g_pre_ffn, g_post_ffn = vec(g_pre_mix), vec(g_post_mix), vec(g_pre_ffn), vec(g_post_ffn)
    w_branch_b = w_branch.astype(BF16)
    w_out_b = w_out.astype(BF16)
    w_router_t = jnp.swapaxes(w_router, 1, 2)
    b_router_l = jnp.broadcast_to(b_router[:, :, None], (depth, N_EXPERTS, LANES))
    b_exp_in = b_exp_in.reshape(depth, N_EXPERTS, 1, 2 * D_FF)
    b_exp_out = b_exp_out.reshape(depth, N_EXPERTS, 1, d)

    ks, vs, sfs, sbs = [], [], [], []
    for l in range(depth):
        h = _prenorm(x, g_pre_mix, mod_all, l, tmod[ROW_TILE], ROW_TILE)
        z_qkvu = _project(h, w_in, l, 0, QKVU_W, F32)
        z_ret = _project(h, w_in, l, QKVU_W, RET_W, BF16)
        z_gate = _project(h, w_in, l, QKVU_W + RET_W, GATE_W, BF16)
        ks.append(z_qkvu[:n_ctx, NA_WIDTH:2 * NA_WIDTH].reshape(batch, seq, NA_HEADS, NA_HEAD_DIM))
        vs.append(z_qkvu[:n_ctx, 2 * NA_WIDTH:3 * NA_WIDTH].reshape(batch, seq, NA_HEADS, NA_HEAD_DIM))

        a_pair = (_attention_ctx(z_qkvu, batch, seq),
                  _attention_lat(z_qkvu, ck, cv, bias_all, l, n_ctx, dec_batch, dec_seq))
        f_pair = (_fourier(z_qkvu, 0, batch, seq), _fourier(z_qkvu, n_ctx, dec_batch, dec_seq))
        r_ctx, s_f, s_b = _retention(z_ret, lg_f, lg_b, l, 0, batch, seq, rotary=False, state_out=True)
        (r_lat,) = _retention(z_ret, lg_f, lg_b, l, n_ctx, dec_batch, dec_seq, rotary=True,
                              states=(state_ret_fwd, state_ret_bwd))
        sfs.append(s_f)
        sbs.append(s_b)

        x1, h2, idx, wts, rank, counts = _finish(
            (a_pair, f_pair, (r_ctx, r_lat)), z_gate, x, mod_all, g_post_mix, g_pre_ffn,
            w_branch_b, w_out_b, w_router_t, b_router_l, l, tmod[FIN_TILE])
        x = _moe(h2, idx, wts, rank, counts, x1, mod_all, g_post_ffn,
                 w_exp_in, b_exp_in, w_exp_out, b_exp_out, l, tmod[FIN_TILE])

    y_prompt = x[:n_ctx].reshape(batch, seq, d)
    y_sample = x[n_ctx:].reshape(dec_batch, dec_seq, d)
    return (y_prompt, y_sample, jnp.stack(ks, axis=1), jnp.stack(vs, axis=1),
            jnp.stack(sfs, axis=1), jnp.stack(sbs, axis=1))
```

```python
import functools

import numpy as np
import jax
import jax.numpy as jnp
from jax import lax
from jax.experimental import pallas as pl
from jax.experimental.pallas import tpu as pltpu

F32 = jnp.float32
BF16 = jnp.bfloat16
I32 = jnp.int32

D_MODEL = 1024
GRID_W = 64
NA_HEADS = 8
NA_HEAD_DIM = 64
NA_WIDTH = NA_HEADS * NA_HEAD_DIM
WIN_H = 8
WIN_W = 16
KEY_SLAB_ROWS = 12
FOURIER_GROUPS = 4
FOURIER_GROUP_DIM = 128
RET_HEADS = 4
RET_KEY_DIM = 128
ROPE_BASE = 10000.0
BRANCH_WIDTH = 512
N_EXPERTS = 32
TOP_K = 4
D_FF = 1024
SWIGLU_LIMIT = 7.0
SWIGLU_ALPHA = 1.702
EPS = 1e-6
NEG_INF = -1e30

QKVU_W = 4 * NA_WIDTH
RET_W = 4 * BRANCH_WIDTH
GATE_W = 3 * D_MODEL
PROJ_TILE = 1024

LANES = 128
MOD_ROWS = 16
ROW_TILE = 1024
FIN_TILE = 512
MOE_BLOCK = 256
SEG_ALIGN = 8
SORT_CHUNK = 256
SORT_ROWS = -(-(FIN_TILE * TOP_K + N_EXPERTS * (SEG_ALIGN - 1)) // SORT_CHUNK) * SORT_CHUNK
VMEM_LIMIT = 56 * 1024 * 1024


def _params(n_axes, vmem=VMEM_LIMIT):
    return pltpu.CompilerParams(dimension_semantics=("arbitrary",) * n_axes, vmem_limit_bytes=vmem)


def _rms(x):
    return lax.rsqrt(jnp.mean(x * x, axis=-1, keepdims=True) + EPS)


def _mod_kernel(cv_ref, w_ref, b_ref, o_ref):
    cv = cv_ref[...]
    s = (cv * jax.nn.sigmoid(cv)).astype(BF16)
    o_ref[...] = jnp.dot(s, w_ref[...].astype(BF16), preferred_element_type=F32) + b_ref[...]


def _modulation(cv, w_mod, b_mod):
    depth, d, n = w_mod.shape
    tn = 1536
    return pl.pallas_call(
        _mod_kernel,
        grid=(depth, n // tn),
        in_specs=[pl.BlockSpec((MOD_ROWS, d), lambda l, j: (0, 0)),
                  pl.BlockSpec((None, d, tn), lambda l, j: (l, 0, j)),
                  pl.BlockSpec((None, 1, tn), lambda l, j: (l, 0, j))],
        out_specs=pl.BlockSpec((None, MOD_ROWS, tn), lambda l, j: (l, 0, j)),
        out_shape=jax.ShapeDtypeStruct((depth, MOD_ROWS, n), F32),
        compiler_params=_params(2),
        name="modulation",
    )(cv, w_mod, b_mod.reshape(depth, 1, n))


def _prenorm_kernel(tmod_ref, x_ref, g_ref, mod_ref, o_ref):
    del tmod_ref
    x = x_ref[...]
    h = x * _rms(x) * g_ref[...]
    o_ref[...] = (h * (1.0 + mod_ref[1:2, :]) + mod_ref[0:1, :]).astype(o_ref.dtype)


def _prenorm(x, g, mod, layer, tile_mod, tm):
    n, d = x.shape
    return pl.pallas_call(
        _prenorm_kernel,
        grid_spec=pltpu.PrefetchScalarGridSpec(
            num_scalar_prefetch=1, grid=(n // tm,),
            in_specs=[pl.BlockSpec((tm, d), lambda i, t: (i, 0)),
                      pl.BlockSpec((None, 1, d), lambda i, t: (layer, 0, 0)),
                      pl.BlockSpec((None, None, 6, d), lambda i, t: (layer, t[i], 0, 0))],
            out_specs=pl.BlockSpec((tm, d), lambda i, t: (i, 0))),
        out_shape=jax.ShapeDtypeStruct((n, d), BF16),
        compiler_params=_params(1),
        name="prenorm",
    )(tile_mod, x, g, mod)


def _proj_kernel(h_ref, w_ref, o_ref, wb_ref):
    @pl.when(pl.program_id(1) == 0)
    def _():
        wb_ref[...] = w_ref[...].astype(BF16)

    o_ref[...] = jnp.dot(h_ref[...], wb_ref[...], preferred_element_type=F32).astype(o_ref.dtype)


def _project(h, w, layer, col0, width, out_dtype):
    n, d = h.shape
    tm = ROW_TILE
    tn = PROJ_TILE
    cb = col0 // tn
    return pl.pallas_call(
        _proj_kernel,
        grid=(width // tn, n // tm),
        in_specs=[pl.BlockSpec((tm, d), lambda j, i: (i, 0)),
                  pl.BlockSpec((None, d, tn), lambda j, i: (layer, 0, cb + j))],
        out_specs=pl.BlockSpec((tm, tn), lambda j, i: (i, j)),
        out_shape=jax.ShapeDtypeStruct((n, width), out_dtype),
        scratch_shapes=[pltpu.VMEM((d, tn), BF16)],
        compiler_params=_params(2),
        name="in_proj",
    )(h, w)


def _head_pair_masks():
    lane = lax.broadcasted_iota(I32, (1, LANES), 1)
    first = lane < NA_HEAD_DIM
    return first, jnp.logical_not(first)


def _attn_ctx_kernel(q_ref, k_ref, v_ref, o_ref):
    masks = _head_pair_masks()
    scale = NA_HEAD_DIM ** -0.5
    for p in range(NA_WIDTH // LANES):
        cols = slice(LANES * p, LANES * (p + 1))
        q2 = q_ref[:, cols] * scale
        k2 = k_ref[:, cols].astype(BF16)
        v2 = v_ref[:, cols].astype(BF16)
        outs = []
        for m in masks:
            qa = jnp.where(m, q2, 0.0).astype(BF16)
            s = lax.dot_general(qa, k2, (((1,), (1,)), ((), ())), preferred_element_type=F32)
            e = jnp.exp(s - jnp.max(s, axis=-1, keepdims=True))
            den = jnp.sum(e, axis=-1, keepdims=True)
            outs.append(jnp.dot(e.astype(BF16), v2, preferred_element_type=F32) / den)
        o_ref[:, cols] = jnp.where(masks[0], outs[0], outs[1]).astype(o_ref.dtype)


def _attention_ctx(z_qkv, n_seq, seq):
    return pl.pallas_call(
        _attn_ctx_kernel,
        grid=(n_seq,),
        in_specs=[pl.BlockSpec((seq, NA_WIDTH), lambda b: (b, 0)),
                  pl.BlockSpec((seq, NA_WIDTH), lambda b: (b, 1)),
                  pl.BlockSpec((seq, NA_WIDTH), lambda b: (b, 2))],
        out_specs=pl.BlockSpec((seq, NA_WIDTH), lambda b: (b, 0)),
        out_shape=jax.ShapeDtypeStruct((n_seq * seq, NA_WIDTH), BF16),
        compiler_params=_params(1),
        name="attn_ctx",
    )(z_qkv, z_qkv, z_qkv)


def _attn_lat_kernel(q_ref, k_ref, v_ref, kc_ref, vc_ref, bias_ref, o_ref, kb_ref, vb_ref, *, tq):
    masks = _head_pair_masks()
    scale = NA_HEAD_DIM ** -0.5
    seq = q_ref.shape[0]
    slab = bias_ref.shape[3]
    per_half = seq // 2 // tq
    kb_ref[...] = k_ref[...].astype(BF16)
    vb_ref[...] = v_ref[...].astype(BF16)
    kc = kc_ref[...].astype(BF16)
    vc = vc_ref[...].astype(BF16)
    nt = (((1,), (1,)), ((), ()))

    def q_tile(qi, carry):
        rows = pl.ds(pl.multiple_of(qi * tq, tq), tq)
        half = qi // per_half
        half_rows = pl.ds(pl.multiple_of((qi % per_half) * tq, tq), tq)
        keys = pl.ds(pl.multiple_of(half * (seq - slab), tq), slab)
        k2 = kb_ref[keys, :]
        v2 = vb_ref[keys, :]
        q2 = q_ref[rows, :] * scale
        outs = []
        for hh, m in enumerate(masks):
            qa = jnp.where(m, q2, 0.0).astype(BF16)
            s_lat = lax.dot_general(qa, k2, nt, preferred_element_type=F32) + bias_ref[hh, half, half_rows, :]
            s_ctx = lax.dot_general(qa, kc, nt, preferred_element_type=F32)
            mx = jnp.maximum(jnp.max(s_lat, axis=-1, keepdims=True), jnp.max(s_ctx, axis=-1, keepdims=True))
            e_lat = jnp.exp(s_lat - mx)
            e_ctx = jnp.exp(s_ctx - mx)
            den = jnp.sum(e_lat, axis=-1, keepdims=True) + jnp.sum(e_ctx, axis=-1, keepdims=True)
            o = (jnp.dot(e_lat.astype(BF16), v2, preferred_element_type=F32)
                 + jnp.dot(e_ctx.astype(BF16), vc, preferred_element_type=F32))
            outs.append(o / den)
        o_ref[rows, :] = jnp.where(masks[0], outs[0], outs[1]).astype(o_ref.dtype)
        return carry

    lax.fori_loop(0, q_ref.shape[0] // tq, q_tile, 0)


def _attention_lat(z_qkv, cache_k, cache_v, bias, layer, row0, n_seq, seq):
    past = cache_k.shape[2]
    pairs = NA_WIDTH // LANES
    rb = row0 // seq
    kv_cols = NA_WIDTH // LANES
    slab = bias.shape[-1]
    return pl.pallas_call(
        functools.partial(_attn_lat_kernel, tq=256),
        grid=(pairs, n_seq),
        in_specs=[pl.BlockSpec((seq, LANES), lambda p, b: (rb + b, p)),
                  pl.BlockSpec((seq, LANES), lambda p, b: (rb + b, kv_cols + p)),
                  pl.BlockSpec((seq, LANES), lambda p, b: (rb + b, 2 * kv_cols + p)),
                  pl.BlockSpec((None, None, past, LANES), lambda p, b: (b, layer, 0, p)),
                  pl.BlockSpec((None, None, past, LANES), lambda p, b: (b, layer, 0, p)),
                  pl.BlockSpec((None, 2, 2, seq // 2, slab), lambda p, b: (layer, p, 0, 0, 0))],
        out_specs=pl.BlockSpec((seq, LANES), lambda p, b: (b, p)),
        out_shape=jax.ShapeDtypeStruct((n_seq * seq, NA_WIDTH), BF16),
        scratch_shapes=[pltpu.VMEM((seq, LANES), BF16), pltpu.VMEM((seq, LANES), BF16)],
        compiler_params=_params(2),
        name="attn_lat",
    )(z_qkv, z_qkv, z_qkv, cache_k, cache_v, bias)


def _neighbourhood_bias(rpb, seq):
    rows = seq // GRID_W
    kh = WIN_H
    assert rows >= WIN_H
    lead = rpb.shape[:-2]
    c = np.arange(GRID_W)
    q_cs = np.clip(c - WIN_W // 2, 0, GRID_W - WIN_W)
    col_ok = (c[None, :] >= q_cs[:, None]) & (c[None, :] < q_cs[:, None] + WIN_W)
    pad_c = GRID_W - WIN_W
    pc = jnp.pad(rpb, ((0, 0),) * (rpb.ndim - 1) + ((pad_c, pad_c),), mode="edge")
    w = jnp.stack([pc[..., GRID_W - 1 - cq:2 * GRID_W - 1 - cq] for cq in range(GRID_W)], axis=-3)
    w = jnp.where(jnp.asarray(col_ok)[:, None, :], w, NEG_INF)
    r = np.arange(rows)
    rs = np.clip(r - kh // 2, 0, rows - kh)
    blocks = []
    for rq in range(rows):
        base = _key_slab_start(rq // (rows // 2), rows)
        lo = int(rs[rq]) - rq + WIN_H - 1
        slab = w[..., lo:lo + kh, :].reshape(lead + (GRID_W, kh * GRID_W))
        left = int(rs[rq]) - base
        assert 0 <= left <= KEY_SLAB_ROWS - kh
        pad = ((0, 0),) * (len(lead) + 1) + ((left * GRID_W, (KEY_SLAB_ROWS - kh - left) * GRID_W),)
        blocks.append(jnp.pad(slab, pad, constant_values=NEG_INF))
    return jnp.stack(blocks, axis=-3).reshape(lead + (2, seq // 2, KEY_SLAB_ROWS * GRID_W))


def _key_slab_start(half, rows):
    return 0 if half == 0 else rows - KEY_SLAB_ROWS


def _fourier_kernel(u_ref, ct2_ref, cc_ref, sc_ref, o_ref, pq_ref):
    t = u_ref.shape[0]
    for g in range(FOURIER_GROUPS):
        cols = slice(FOURIER_GROUP_DIM * g, FOURIER_GROUP_DIM * (g + 1))
        ug = u_ref[:, cols].astype(BF16)
        pq_ref[0:t, cols] = jnp.dot(ug, cc_ref[...], preferred_element_type=F32).astype(BF16)
        pq_ref[t:2 * t, cols] = jnp.dot(ug, sc_ref[...], preferred_element_type=F32).astype(BF16)
    o_ref[...] = jnp.dot(ct2_ref[...], pq_ref[...], preferred_element_type=F32).astype(o_ref.dtype)


def _dft_tables(t):
    def cs(n):
        k = np.arange(n, dtype=np.int64)
        ang = 2.0 * np.pi * ((k[:, None] * k[None, :]) % n).astype(np.float64) / n
        return np.cos(ang) / np.sqrt(n), np.sin(ang) / np.sqrt(n)

    ct, st = cs(t)
    cc, sc = cs(FOURIER_GROUP_DIM)
    ct2 = np.concatenate([ct, -st], axis=1).astype(np.float32)
    return (jnp.asarray(ct2).astype(BF16), jnp.asarray(cc.astype(np.float32)).astype(BF16),
            jnp.asarray(sc.astype(np.float32)).astype(BF16))


def _fourier(z_qkvu, row0, n_seq, seq):
    ct2, cc, sc = _dft_tables(seq)
    width = FOURIER_GROUPS * FOURIER_GROUP_DIM
    rb = row0 // seq
    ucol = 3 * NA_WIDTH // width
    return pl.pallas_call(
        _fourier_kernel,
        grid=(n_seq,),
        in_specs=[pl.BlockSpec((seq, width), lambda b: (rb + b, ucol)),
                  pl.BlockSpec((seq, 2 * seq), lambda b: (0, 0)),
                  pl.BlockSpec((FOURIER_GROUP_DIM, FOURIER_GROUP_DIM), lambda b: (0, 0)),
                  pl.BlockSpec((FOURIER_GROUP_DIM, FOURIER_GROUP_DIM), lambda b: (0, 0))],
        out_specs=pl.BlockSpec((seq, width), lambda b: (b, 0)),
        out_shape=jax.ShapeDtypeStruct((n_seq * seq, width), BF16),
        scratch_shapes=[pltpu.VMEM((2 * seq, width), BF16)],
        compiler_params=_params(1),
        name="fourier",
    )(z_qkvu, ct2, cc, sc)


def _rotary_tables(t):
    pos = np.arange(t)
    row = (pos // GRID_W).astype(np.float64)
    col = (pos % GRID_W).astype(np.float64)
    nf = RET_KEY_DIM // 4
    inv_freq = ROPE_BASE ** (-np.arange(nf, dtype=np.float64) / nf)
    ar = row[:, None] * inv_freq[None]
    ac = col[:, None] * inv_freq[None]
    cos = np.concatenate([np.cos(ar), np.cos(ar), np.cos(ac), np.cos(ac)], axis=1)
    sin = np.concatenate([-np.sin(ar), np.sin(ar), -np.sin(ac), np.sin(ac)], axis=1)
    return jnp.asarray(cos.astype(np.float32)), jnp.asarray(sin.astype(np.float32))


def _ret_kernel(lgf_ref, lgb_ref, *refs, t, tq, layer, rotary, state_in, state_out):
    refs = list(refs)
    q_ref, k_ref, v_ref, g_ref = refs[:4]
    refs = refs[4:]
    if rotary:
        cos_ref, sin_ref = refs[:2]
        refs = refs[2:]
    if state_in:
        sf0_ref, sb0_ref = refs[:2]
        refs = refs[2:]
    o_ref = refs[0]
    refs = refs[1:]
    if state_out:
        sf_ref, sb_ref = refs[:2]
        refs = refs[2:]
    dec_ref, kb_ref = refs

    h = pl.program_id(0)
    lgf = lgf_ref[layer * RET_HEADS + h]
    lgb = lgb_ref[layer * RET_HEADS + h]
    scale = RET_KEY_DIM ** -0.5
    nq = t // tq

    @pl.when(pl.program_id(1) == 0)
    def _():
        def fill(ri, c):
            rows = pl.ds(pl.multiple_of(ri * tq, tq), tq)
            i = lax.broadcasted_iota(I32, (tq, t), 0) + ri * tq
            j = lax.broadcasted_iota(I32, (tq, t), 1)
            d = (i - j).astype(F32)
            m = jnp.exp(jnp.abs(d) * jnp.where(d > 0, lgf, lgb))
            dec_ref[rows, :] = jnp.where(d == 0, 2.0, m)
            return c

        lax.fori_loop(0, nq, fill, 0)

    if rotary:
        lane = lax.broadcasted_iota(I32, (1, LANES), 1)
        low = (lane % (RET_KEY_DIM // 2)) < (RET_KEY_DIM // 4)

        def rot(x, rows):
            swapped = jnp.where(low, pltpu.roll(x, LANES - RET_KEY_DIM // 4, 1), pltpu.roll(x, RET_KEY_DIM // 4, 1))
            return x * cos_ref[rows, :] + swapped * sin_ref[rows, :]
    else:
        def rot(x, rows):
            return x

    all_rows = slice(0, t)
    kr = rot(k_ref[...].astype(F32), all_rows)
    kb_ref[...] = kr.astype(BF16)
    vb = v_ref[...]

    if state_out:
        j = lax.broadcasted_iota(I32, (t, 1), 0).astype(F32)
        tn = (((0,), (0,)), ((), ()))
        kf = (kr * (scale * jnp.exp(lgf * (t - 1.0 - j)))).astype(BF16)
        kbw = (kr * (scale * jnp.exp(lgb * j))).astype(BF16)
        sf = lax.dot_general(kf, vb, tn, preferred_element_type=F32)
        sb = lax.dot_general(kbw, vb, tn, preferred_element_type=F32)
        if state_in:
            sf = sf + jnp.exp(lgf * t) * sf0_ref[...]
            sb = sb + jnp.exp(lgb * t) * sb0_ref[...]
        sf_ref[...] = sf
        sb_ref[...] = sb

    def q_tile(qi, carry):
        r0 = pl.multiple_of(qi * tq, tq)
        rows = pl.ds(r0, tq)
        qr = rot(q_ref[rows, :].astype(F32), rows)
        s = lax.dot_general((qr * scale).astype(BF16), kb_ref[...], (((1,), (1,)), ((), ())),
                            preferred_element_type=F32)
        y = jnp.dot((s * dec_ref[rows, :]).astype(BF16), vb, preferred_element_type=F32)
        if state_in:
            pos = (lax.broadcasted_iota(I32, (tq, 1), 0) + r0).astype(F32)
            qf = (qr * jnp.exp(lgf * (pos + 1.0))).astype(BF16)
            qb = (qr * jnp.exp(lgb * (t - pos))).astype(BF16)
            y = (y + jnp.dot(qf, sf0_ref[...].astype(BF16), preferred_element_type=F32)
                 + jnp.dot(qb, sb0_ref[...].astype(BF16), preferred_element_type=F32))
        mean = jnp.mean(y, axis=-1, keepdims=True)
        yc = y - mean
        yn = yc * lax.rsqrt(jnp.mean(yc * yc, axis=-1, keepdims=True) + EPS)
        g = g_ref[rows, :].astype(F32)
        o_ref[rows, :] = (g * jax.nn.sigmoid(g) * yn).astype(o_ref.dtype)
        return carry

    lax.fori_loop(0, nq, q_tile, 0)


def _retention(z_ret, lg_f, lg_b, layer, row0, n_seq, seq, *, rotary, states=None, state_out=False):
    rb = row0 // seq
    cb = BRANCH_WIDTH // LANES
    tq = min(seq, 256)
    state_in = states is not None
    in_specs = [pl.BlockSpec((seq, LANES), lambda h, b, *_: (rb + b, 0 * cb + h)),
                pl.BlockSpec((seq, LANES), lambda h, b, *_: (rb + b, 1 * cb + h)),
                pl.BlockSpec((seq, LANES), lambda h, b, *_: (rb + b, 2 * cb + h)),
                pl.BlockSpec((seq, LANES), lambda h, b, *_: (rb + b, 3 * cb + h))]
    args = [z_ret, z_ret, z_ret, z_ret]
    if rotary:
        cos, sin = _rotary_tables(seq)
        in_specs += [pl.BlockSpec((seq, LANES), lambda h, b, *_: (0, 0))] * 2
        args += [cos, sin]
    if state_in:
        st_spec = pl.BlockSpec((None, None, None, RET_KEY_DIM, RET_KEY_DIM), lambda h, b, *_: (b, layer, h, 0, 0))
        in_specs += [st_spec, st_spec]
        args += list(states)
    out_specs = [pl.BlockSpec((seq, LANES), lambda h, b, *_: (b, h))]
    out_shape = [jax.ShapeDtypeStruct((n_seq * seq, RET_HEADS * LANES), BF16)]
    if state_out:
        so = pl.BlockSpec((None, None, RET_KEY_DIM, RET_KEY_DIM), lambda h, b, *_: (b, h, 0, 0))
        out_specs += [so, so]
        out_shape += [jax.ShapeDtypeStruct((n_seq, RET_HEADS, RET_KEY_DIM, RET_KEY_DIM), F32)] * 2
    return pl.pallas_call(
        functools.partial(_ret_kernel, t=seq, tq=tq, layer=layer, rotary=rotary, state_in=state_in,
                          state_out=state_out),
        grid_spec=pltpu.PrefetchScalarGridSpec(
            num_scalar_prefetch=2, grid=(RET_HEADS, n_seq),
            in_specs=in_specs, out_specs=out_specs,
            scratch_shapes=[pltpu.VMEM((seq, seq), F32), pltpu.VMEM((seq, LANES), BF16)]),
        out_shape=out_shape,
        compiler_params=_params(2),
        name="retention",
    )(lg_f, lg_b, *args)


def _split_dot_nt(w, x):
    nt = (((1,), (1,)), ((), ()))
    w_hi = w.astype(BF16)
    w_lo = (w - w_hi.astype(F32)).astype(BF16)
    x_hi = x.astype(BF16)
    x_lo = (x - x_hi.astype(F32)).astype(BF16)
    return (lax.dot_general(w_hi, x_hi, nt, preferred_element_type=F32)
            + lax.dot_general(w_hi, x_lo, nt, preferred_element_type=F32)
            + lax.dot_general(w_lo, x_hi, nt, preferred_element_type=F32))


def _finish_kernel(tmod_ref, ac_ref, al_ref, fc_ref, fl_ref, rc_ref, rl_ref, zg_ref, x_ref, mod_ref,
                   gpost_ref, gpre_ref, wb_ref, wo_ref, wrt_ref, br_ref, tri_ref,
                   x1_ref, h2_ref, idx_ref, wts_ref, rank_ref, cnt_ref, *, ctx_tiles):
    del tmod_ref
    d = D_MODEL
    is_ctx = pl.program_id(0) < ctx_tiles

    def branch(c_ref, l_ref):
        return jnp.where(is_ctx, c_ref[...], l_ref[...])

    def gate(j):
        return jax.nn.sigmoid(zg_ref[:, d * j:d * (j + 1)].astype(F32))

    merged = (gate(0) * jnp.dot(branch(ac_ref, al_ref), wb_ref[0], preferred_element_type=F32)
              + gate(1) * jnp.dot(branch(fc_ref, fl_ref), wb_ref[1], preferred_element_type=F32)
              + gate(2) * jnp.dot(branch(rc_ref, rl_ref), wb_ref[2], preferred_element_type=F32))
    y = jnp.dot(merged.astype(BF16), wo_ref[...], preferred_element_type=F32)
    x1 = x_ref[...] + mod_ref[2:3, :] * (y * _rms(y) * gpost_ref[...])
    x1_ref[...] = x1
    h2 = x1 * _rms(x1) * gpre_ref[...] * (1.0 + mod_ref[4:5, :]) + mod_ref[3:4, :]
    h2_ref[...] = h2

    logits = _split_dot_nt(wrt_ref[...], h2) + br_ref[:, 0:1]
    tm = logits.shape[1]
    eidx = lax.broadcasted_iota(I32, (N_EXPERTS, tm), 0)
    cur = logits
    vals, hots = [], []
    for k in range(TOP_K):
        m = jnp.max(cur, axis=0, keepdims=True)
        sel = jnp.min(jnp.where(cur == m, eidx, N_EXPERTS), axis=0, keepdims=True)
        hot = eidx == sel
        vals.append(m)
        hots.append(hot)
        idx_ref[k:k + 1, :] = sel
        cur = jnp.where(hot, -jnp.inf, cur)
    exps = [jnp.exp(v - vals[0]) for v in vals]
    den = exps[0] + exps[1] + exps[2] + exps[3]
    for k in range(TOP_K):
        wts_ref[k:k + 1, :] = exps[k] / den

    member = jnp.logical_or(jnp.logical_or(hots[0], hots[1]), jnp.logical_or(hots[2], hots[3]))
    member_f = member.astype(F32)
    before = jnp.dot(member_f.astype(BF16), tri_ref[...], preferred_element_type=F32)
    for k in range(TOP_K):
        rank_ref[k:k + 1, :] = jnp.sum(jnp.where(hots[k], before, 0.0), axis=0, keepdims=True).astype(I32)
    cnt_ref[...] = jnp.broadcast_to(jnp.sum(member_f, axis=1, keepdims=True), cnt_ref.shape)


def _finish(branches, z_gate, x, mod, g_post, g_pre_ffn, w_branch, w_out, w_router_t, b_router, layer, tile_mod):
    n, d = x.shape
    tm = FIN_TILE
    ctx_tiles = branches[0][0].shape[0] // tm
    tri = jnp.asarray(np.triu(np.ones((tm, tm), np.float32), k=1)).astype(BF16)
    row = lambda i, t: (i, 0)
    ctx_row = lambda i, t: (jnp.minimum(i, ctx_tiles - 1), 0)
    lat_row = lambda i, t: (jnp.maximum(i - ctx_tiles, 0), 0)
    const2 = lambda i, t: (0, 0)
    lay3 = lambda i, t: (layer, 0, 0)
    col = lambda i, t: (0, i)
    branch_specs, branch_args = [], []
    for c_arr, l_arr in branches:
        branch_specs += [pl.BlockSpec((tm, BRANCH_WIDTH), ctx_row), pl.BlockSpec((tm, BRANCH_WIDTH), lat_row)]
        branch_args += [c_arr, l_arr]
    outs = pl.pallas_call(
        functools.partial(_finish_kernel, ctx_tiles=ctx_tiles),
        grid_spec=pltpu.PrefetchScalarGridSpec(
            num_scalar_prefetch=1, grid=(n // tm,),
            in_specs=branch_specs + [
                      pl.BlockSpec((tm, GATE_W), row),
                      pl.BlockSpec((tm, d), row),
                      pl.BlockSpec((None, None, 6, d), lambda i, t: (layer, t[i], 0, 0)),
                      pl.BlockSpec((None, 1, d), lay3),
                      pl.BlockSpec((None, 1, d), lay3),
                      pl.BlockSpec((None, 3, BRANCH_WIDTH, d), lambda i, t: (layer, 0, 0, 0)),
                      pl.BlockSpec((None, d, d), lay3),
                      pl.BlockSpec((None, N_EXPERTS, d), lay3),
                      pl.BlockSpec((None, N_EXPERTS, LANES), lay3),
                      pl.BlockSpec((tm, tm), const2)],
            out_specs=[pl.BlockSpec((tm, d), row),
                       pl.BlockSpec((tm, d), row),
                       pl.BlockSpec((TOP_K, tm), col),
                       pl.BlockSpec((TOP_K, tm), col),
                       pl.BlockSpec((TOP_K, tm), col),
                       pl.BlockSpec((None, N_EXPERTS, LANES), lambda i, t: (i, 0, 0))]),
        out_shape=[jax.ShapeDtypeStruct((n, d), F32),
                   jax.ShapeDtypeStruct((n, d), F32),
                   jax.ShapeDtypeStruct((TOP_K, n), I32),
                   jax.ShapeDtypeStruct((TOP_K, n), F32),
                   jax.ShapeDtypeStruct((TOP_K, n), I32),
                   jax.ShapeDtypeStruct((n // tm, N_EXPERTS, LANES), F32)],
        compiler_params=_params(1),
        name="merge_router",
    )(tile_mod, *branch_args, z_gate, x, mod, g_post, g_pre_ffn, w_branch, w_out, w_router_t, b_router, tri)
    return outs


def _segment_chunks(length, src_ref, src0, dst_ref, dst0, sem, max_chunk, fixed_src=False):
    out = []
    chunk = max_chunk
    while chunk >= SEG_ALIGN:
        done = jnp.bitwise_and(length, ~(2 * chunk - 1))
        present = jnp.bitwise_and(length, chunk) != 0
        s = 0 if fixed_src else pl.multiple_of(src0 + done, SEG_ALIGN)
        dd = pl.multiple_of(dst0 + done, SEG_ALIGN)
        out.append((present, pltpu.make_async_copy(src_ref.at[pl.ds(s, chunk)], dst_ref.at[pl.ds(dd, chunk)], sem)))
        chunk //= 2
    return out


def _for_each_chunk(n_segments, chunks_of, action):
    def body(e, c):
        for present, cp in chunks_of(e):
            pl.when(present)(functools.partial(action, cp))
        return c

    lax.fori_loop(0, n_segments, body, 0)


def _start(cp):
    cp.start()


def _wait(cp):
    cp.wait()


def _scatter_kernel(seg_ref, off_ref, pos_ref, tpos_ref, tlen_ref, h_ref, lpos_ref, xe_hbm,
                    buf_ref, zero_ref, sem, *, tm):
    t = pl.program_id(0)
    slot = t % 2
    hb = h_ref[...].astype(BF16)
    lp = [lpos_ref[k:k + 1, :] for k in range(TOP_K)]
    rows = buf_ref.shape[1]
    for c0 in range(0, rows, SORT_CHUNK):
        r = lax.broadcasted_iota(I32, (SORT_CHUNK, tm), 0) + c0
        hit = jnp.logical_or(jnp.logical_or(r == lp[0], r == lp[1]), jnp.logical_or(r == lp[2], r == lp[3]))
        onehot = jnp.where(hit, 1.0, 0.0).astype(BF16)
        buf_ref[slot, c0:c0 + SORT_CHUNK, :] = jnp.dot(onehot, hb, preferred_element_type=F32)

    def segments_of(tile):
        def segment(e):
            j = tile * N_EXPERTS + e
            return _segment_chunks(seg_ref[j], buf_ref.at[tile % 2], off_ref[j], xe_hbm, pos_ref[j],
                                   sem.at[tile % 2], tm)
        return segment

    @pl.when(t > 0)
    def _():
        _for_each_chunk(N_EXPERTS, segments_of(t - 1), _wait)

    _for_each_chunk(N_EXPERTS, segments_of(t), _start)

    @pl.when(t == pl.num_programs(0) - 1)
    def _():
        _for_each_chunk(N_EXPERTS, segments_of(t), _wait)
        zero_ref[...] = jnp.zeros_like(zero_ref)

        def tail(e):
            return _segment_chunks(tlen_ref[e], zero_ref, 0, xe_hbm, tpos_ref[e], sem.at[0], zero_ref.shape[0],
                                   fixed_src=True)

        _for_each_chunk(N_EXPERTS, tail, _start)
        _for_each_chunk(N_EXPERTS, tail, _wait)


def _scatter_rows(h2, lpos, plan, n_rows):
    n, d = h2.shape
    tm = FIN_TILE
    return pl.pallas_call(
        functools.partial(_scatter_kernel, tm=tm),
        grid_spec=pltpu.PrefetchScalarGridSpec(
            num_scalar_prefetch=5, grid=(n // tm,),
            in_specs=[pl.BlockSpec((tm, d), lambda i, *_: (i, 0)),
                      pl.BlockSpec((TOP_K, tm), lambda i, *_: (0, i))],
            out_specs=pl.BlockSpec(memory_space=pl.ANY),
            scratch_shapes=[pltpu.VMEM((2, SORT_ROWS, d), F32), pltpu.VMEM((MOE_BLOCK // 2, d), F32),
                            pltpu.SemaphoreType.DMA((2,))]),
        out_shape=jax.ShapeDtypeStruct((n_rows, d), F32),
        compiler_params=_params(1),
        name="moe_scatter",
    )(plan["seg"], plan["off"], plan["pos"], plan["tail_pos"], plan["tail_len"], h2, lpos)


def _expert_kernel(be_ref, nxt_ref, nu_ref, xb_ref, w1_hbm, b1_ref, w2_hbm, b2_ref, yb_ref,
                   w1s_ref, w2s_ref, w1b_ref, w2b_ref, sem, *, layer):
    i = pl.program_id(0)
    used = i < nu_ref[0]
    fresh = jnp.logical_or(i == 0, be_ref[i] != be_ref[jnp.maximum(i - 1, 0)])

    def fetch(e):
        return (pltpu.make_async_copy(w1_hbm.at[layer, e], w1s_ref, sem.at[0]),
                pltpu.make_async_copy(w2_hbm.at[layer, e], w2s_ref, sem.at[1]))

    @pl.when(i == 0)
    def _():
        for cp in fetch(be_ref[0]):
            cp.start()

    @pl.when(jnp.logical_and(used, fresh))
    def _():
        for cp in fetch(be_ref[i]):
            cp.wait()
        w1b_ref[...] = w1s_ref[...].astype(BF16)
        w2b_ref[...] = w2s_ref[...].astype(BF16)

        @pl.when(nxt_ref[i] >= 0)
        def _():
            for cp in fetch(nxt_ref[i]):
                cp.start()

    @pl.when(used)
    def _():
        z = jnp.dot(xb_ref[...].astype(BF16), w1b_ref[...], preferred_element_type=F32) + b1_ref[...]
        glu = jnp.minimum(z[:, :D_FF], SWIGLU_LIMIT)
        lin = jnp.clip(z[:, D_FF:], -SWIGLU_LIMIT, SWIGLU_LIMIT)
        act = glu * jax.nn.sigmoid(SWIGLU_ALPHA * glu) * (lin + 1.0)
        yb_ref[...] = jnp.dot(act.astype(BF16), w2b_ref[...], preferred_element_type=F32) + b2_ref[...]

    @pl.when(jnp.logical_not(used))
    def _():
        yb_ref[...] = jnp.zeros_like(yb_ref)


def _experts(xb, block_expert, next_expert, n_used, w1, b1, w2, b2, layer):
    n_rows, d = xb.shape
    tm = MOE_BLOCK
    return pl.pallas_call(
        functools.partial(_expert_kernel, layer=layer),
        grid_spec=pltpu.PrefetchScalarGridSpec(
            num_scalar_prefetch=3, grid=(n_rows // tm,),
            in_specs=[pl.BlockSpec((tm, d), lambda i, be, nx, nu: (jnp.minimum(i, nu[0] - 1), 0)),
                      pl.BlockSpec(memory_space=pl.ANY),
                      pl.BlockSpec((None, None, 1, 2 * D_FF), lambda i, be, nx, nu: (layer, be[i], 0, 0)),
                      pl.BlockSpec(memory_space=pl.ANY),
                      pl.BlockSpec((None, None, 1, d), lambda i, be, nx, nu: (layer, be[i], 0, 0))],
            out_specs=pl.BlockSpec((tm, d), lambda i, be, nx, nu: (i, 0)),
            scratch_shapes=[pltpu.VMEM((d, 2 * D_FF), F32), pltpu.VMEM((D_FF, d), F32),
                            pltpu.VMEM((d, 2 * D_FF), BF16), pltpu.VMEM((D_FF, d), BF16),
                            pltpu.SemaphoreType.DMA((2,))]),
        out_shape=jax.ShapeDtypeStruct((n_rows, d), F32),
        compiler_params=_params(1),
        name="moe_experts",
    )(block_expert, next_expert, n_used, xb, w1, b1, w2, b2)


def _gather_kernel(tmod_ref, seg_ref, off_ref, pos_ref, ye_hbm, lpos_ref, wts_ref, x1_ref, mod_ref, g_ref, o_ref,
                   buf_ref, sem, *, tm):
    del tmod_ref
    t = pl.program_id(0)
    slot = t % 2

    def segments_of(tile):
        def segment(e):
            j = tile * N_EXPERTS + e
            return _segment_chunks(seg_ref[j], ye_hbm, pos_ref[j], buf_ref.at[tile % 2], off_ref[j],
                                   sem.at[tile % 2], tm)
        return segment

    @pl.when(t == 0)
    def _():
        buf_ref[...] = jnp.zeros_like(buf_ref)
        _for_each_chunk(N_EXPERTS, segments_of(t), _start)

    @pl.when(t + 1 < pl.num_programs(0))
    def _():
        _for_each_chunk(N_EXPERTS, segments_of(t + 1), _start)

    _for_each_chunk(N_EXPERTS, segments_of(t), _wait)

    lp = [lpos_ref[:, k:k + 1] for k in range(TOP_K)]
    wt = [wts_ref[:, k:k + 1] for k in range(TOP_K)]
    y = jnp.zeros(o_ref.shape, F32)
    for c0 in range(0, buf_ref.shape[1], SORT_CHUNK):
        r = lax.broadcasted_iota(I32, (tm, SORT_CHUNK), 1) + c0
        wm = jnp.where(r == lp[0], wt[0], 0.0)
        for k in range(1, TOP_K):
            wm = jnp.where(r == lp[k], wt[k], wm)
        y = y + jnp.dot(wm.astype(BF16), buf_ref[slot, c0:c0 + SORT_CHUNK, :].astype(BF16),
                        preferred_element_type=F32)
    o_ref[...] = x1_ref[...] + mod_ref[5:6, :] * (y * _rms(y) * g_ref[...])


def _gather_combine(ye, lpos_t, wts_t, plan, x1, mod, g_post, layer, tile_mod):
    n, d = x1.shape
    tm = FIN_TILE
    row = lambda i, *_: (i, 0)
    return pl.pallas_call(
        functools.partial(_gather_kernel, tm=tm),
        grid_spec=pltpu.PrefetchScalarGridSpec(
            num_scalar_prefetch=4, grid=(n // tm,),
            in_specs=[pl.BlockSpec(memory_space=pl.ANY),
                      pl.BlockSpec((tm, TOP_K), row),
                      pl.BlockSpec((tm, TOP_K), row),
                      pl.BlockSpec((tm, d), row),
                      pl.BlockSpec((None, None, 6, d), lambda i, t, *_: (layer, t[i], 0, 0)),
                      pl.BlockSpec((None, 1, d), lambda i, *_: (layer, 0, 0))],
            out_specs=pl.BlockSpec((tm, d), row),
            scratch_shapes=[pltpu.VMEM((2, SORT_ROWS, d), F32), pltpu.SemaphoreType.DMA((2,))]),
        out_shape=jax.ShapeDtypeStruct((n, d), F32),
        compiler_params=_params(1),
        name="moe_gather",
    )(tile_mod, plan["seg"], plan["off"], plan["pos"], ye, lpos_t, wts_t, x1, mod, g_post)


def _moe(h2, idx, wts, rank, counts, x1, mod, g_post, w1, b1, w2, b2, layer, tile_mod):
    n, d = h2.shape
    blk = MOE_BLOCK
    tm = FIN_TILE
    tiles = n // tm
    n_rows = -(-(n * TOP_K + tiles * N_EXPERTS * (SEG_ALIGN - 1) + N_EXPERTS * (blk - 1)) // blk) * blk
    n_blocks = n_rows // blk
    cnt = counts[:, :, 0].astype(I32)
    seg = (cnt + SEG_ALIGN - 1) // SEG_ALIGN * SEG_ALIGN
    off = jnp.cumsum(seg, axis=1) - seg
    rows_e = jnp.sum(seg, axis=0)
    region = (rows_e + blk - 1) // blk * blk
    pend = jnp.cumsum(region)
    pstart = pend - region
    pos = pstart[None, :] + jnp.cumsum(seg, axis=0) - seg
    experts = jnp.arange(N_EXPERTS, dtype=I32)
    idx3 = idx.reshape(TOP_K, tiles, tm)
    off_sel = jnp.sum(jnp.where(idx3[None] == experts[:, None, None, None], off.T[:, None, :, None], 0), axis=0)
    lpos = (off_sel + rank.reshape(TOP_K, tiles, tm)).reshape(TOP_K, n).astype(I32)
    plan = {"seg": seg.reshape(-1), "off": off.reshape(-1).astype(I32), "pos": pos.reshape(-1).astype(I32),
            "tail_pos": (pstart + rows_e).astype(I32), "tail_len": (region - rows_e).astype(I32)}
    blocks = jnp.arange(n_blocks, dtype=I32) * blk
    block_expert = jnp.minimum(jnp.sum(blocks[:, None] >= pend[None, :], axis=1), N_EXPERTS - 1).astype(I32)
    n_used = (pend[-1:] // blk).astype(I32)
    ids = jnp.arange(n_blocks, dtype=I32)
    run_start = jnp.logical_and(jnp.concatenate([jnp.ones((1,), bool), block_expert[1:] != block_expert[:-1]]),
                                ids < n_used[0])
    first_after = lax.cummin(jnp.where(run_start, ids, n_blocks)[::-1])[::-1]
    first_after = jnp.concatenate([first_after[1:], jnp.full((1,), n_blocks, I32)])
    next_expert = jnp.where(first_after < n_blocks, block_expert[jnp.minimum(first_after, n_blocks - 1)], -1)
    xe = _scatter_rows(h2, lpos, plan, n_rows)
    ye = _experts(xe, block_expert, next_expert.astype(I32), n_used, w1, b1, w2, b2, layer)
    return _gather_combine(ye, lpos.T, wts.T, plan, x1, mod, g_post, layer, tile_mod)


def _tile_mod_ids(n_ctx_rows, n_lat_rows, lat_seq, tm):
    ctx = np.zeros((n_ctx_rows // tm,), np.int32)
    lat = 1 + (np.arange(n_lat_rows // tm) * tm) // lat_seq
    return jnp.asarray(np.concatenate([ctx, lat.astype(np.int32)]))


def kernel(x_prompt, x_sample, cache_k, cache_v, state_ret_fwd, state_ret_bwd, c, c_ctx, w_mod, b_mod, g_pre_mix, g_post_mix, g_pre_ffn, g_post_ffn, w_in, na_rel_bias, ret_decay_fwd, ret_decay_bwd, w_branch, w_out, w_router, b_router, w_exp_in, b_exp_in, w_exp_out, b_exp_out):
    batch, seq, d = x_prompt.shape
    dec_batch, dec_seq, _ = x_sample.shape
    depth = w_in.shape[0]
    n_ctx = batch * seq
    n_lat = dec_batch * dec_seq
    assert 1 + dec_batch <= MOD_ROWS

    x = jnp.concatenate([x_prompt.reshape(n_ctx, d), x_sample.reshape(n_lat, d)], axis=0)
    cvec = jnp.concatenate([c_ctx[None], c, jnp.zeros((MOD_ROWS - 1 - dec_batch, d), F32)], axis=0)
    mod_all = _modulation(cvec, w_mod, b_mod).reshape(depth, MOD_ROWS, 6, d)
    tmod = {tm: _tile_mod_ids(n_ctx, n_lat, dec_seq, tm) for tm in (ROW_TILE, FIN_TILE)}
    past = cache_k.shape[2]
    ck = cache_k.reshape(dec_batch, depth, past, NA_WIDTH)
    cv = cache_v.reshape(dec_batch, depth, past, NA_WIDTH)
    lg_f = jax.nn.log_sigmoid(ret_decay_fwd.astype(F32)).reshape(-1)
    lg_b = jax.nn.log_sigmoid(ret_decay_bwd.astype(F32)).reshape(-1)
    bias_all = _neighbourhood_bias(na_rel_bias, dec_seq)
    vec = lambda g: g.reshape(depth, 1, d)
    g_pre_mix, g_post_mix, g_pre_ffn, g_post_ffn = vec(g_pre_mix), vec(g_post_mix), vec(g_pre_ffn), vec(g_post_ffn)
    w_branch_b = w_branch.astype(BF16)
    w_out_b = w_out.astype(BF16)
    w_router_t = jnp.swapaxes(w_router, 1, 2)
    b_router_l = jnp.broadcast_to(b_router[:, :, None], (depth, N_EXPERTS, LANES))
    b_exp_in = b_exp_in.reshape(depth, N_EXPERTS, 1, 2 * D_FF)
    b_exp_out = b_exp_out.reshape(depth, N_EXPERTS, 1, d)

    ks, vs, sfs, sbs = [], [], [], []
    for l in range(depth):
        h = _prenorm(x, g_pre_mix, mod_all, l, tmod[ROW_TILE], ROW_TILE)
        z_qkvu = _project(h, w_in, l, 0, QKVU_W, F32)
        z_ret = _project(h, w_in, l, QKVU_W, RET_W, BF16)
        z_gate = _project(h, w_in, l, QKVU_W + RET_W, GATE_W, BF16)
        ks.append(z_qkvu[:n_ctx, NA_WIDTH:2 * NA_WIDTH].reshape(batch, seq, NA_HEADS, NA_HEAD_DIM))
        vs.append(z_qkvu[:n_ctx, 2 * NA_WIDTH:3 * NA_WIDTH].reshape(batch, seq, NA_HEADS, NA_HEAD_DIM))

        a_pair = (_attention_ctx(z_qkvu, batch, seq),
                  _attention_lat(z_qkvu, ck, cv, bias_all, l, n_ctx, dec_batch, dec_seq))
        f_pair = (_fourier(z_qkvu, 0, batch, seq), _fourier(z_qkvu, n_ctx, dec_batch, dec_seq))
        r_ctx, s_f, s_b = _retention(z_ret, lg_f, lg_b, l, 0, batch, seq, rotary=False, state_out=True)
        (r_lat,) = _retention(z_ret, lg_f, lg_b, l, n_ctx, dec_batch, dec_seq, rotary=True,
                              states=(state_ret_fwd, state_ret_bwd))
        sfs.append(s_f)
        sbs.append(s_b)

        x1, h2, idx, wts, rank, counts = _finish(
            (a_pair, f_pair, (r_ctx, r_lat)), z_gate, x, mod_all, g_post_mix, g_pre_ffn,
            w_branch_b, w_out_b, w_router_t, b_router_l, l, tmod[FIN_TILE])
        x = _moe(h2, idx, wts, rank, counts, x1, mod_all, g_post_ffn,
                 w_exp_in, b_exp_in, w_exp_out, b_exp_out, l, tmod[FIN_TILE])

    y_prompt = x[:n_ctx].reshape(batch, seq, d)
    y_sample = x[n_ctx:].reshape(dec_batch, dec_seq, d)
    return (y_prompt, y_sample, jnp.stack(ks, axis=1), jnp.stack(vs, axis=1),
            jnp.stack(sfs, axis=1), jnp.stack(sbs, axis=1))
```

```python
import functools

import numpy as np
import jax
import jax.numpy as jnp
from jax import lax
from jax.experimental import pallas as pl
from jax.experimental.pallas import tpu as pltpu

F32 = jnp.float32
BF16 = jnp.bfloat16
I32 = jnp.int32

D_MODEL = 1024
GRID_W = 64
NA_HEADS = 8
NA_HEAD_DIM = 64
NA_WIDTH = NA_HEADS * NA_HEAD_DIM
WIN_H = 8
WIN_W = 16
KEY_SLAB_ROWS = 12
FOURIER_GROUPS = 4
FOURIER_GROUP_DIM = 128
RET_HEADS = 4
RET_KEY_DIM = 128
ROPE_BASE = 10000.0
BRANCH_WIDTH = 512
N_EXPERTS = 32
TOP_K = 4
D_FF = 1024
SWIGLU_LIMIT = 7.0
SWIGLU_ALPHA = 1.702
EPS = 1e-6
NEG_INF = -1e30

QKVU_W = 4 * NA_WIDTH
RET_W = 4 * BRANCH_WIDTH
GATE_W = 3 * D_MODEL
PROJ_TILE = 1024
PROJ_ROWS = 2048
ATTN_Q_TILE = 512

LANES = 128
MOD_ROWS = 16
ROW_TILE = 1024
FIN_TILE = 512
MOE_BLOCK = 256
SEG_ALIGN = 8
SORT_CHUNK = 256
SORT_ROWS = -(-(FIN_TILE * TOP_K + N_EXPERTS * (SEG_ALIGN - 1)) // SORT_CHUNK) * SORT_CHUNK
VMEM_LIMIT = 56 * 1024 * 1024


def _params(n_axes, vmem=VMEM_LIMIT):
    return pltpu.CompilerParams(dimension_semantics=("arbitrary",) * n_axes, vmem_limit_bytes=vmem)


def _rms(x):
    return lax.rsqrt(jnp.mean(x * x, axis=-1, keepdims=True) + EPS)


def _mod_kernel(cv_ref, w_ref, b_ref, o_ref):
    cv = cv_ref[...]
    s = (cv * jax.nn.sigmoid(cv)).astype(BF16)
    o_ref[...] = jnp.dot(s, w_ref[...].astype(BF16), preferred_element_type=F32) + b_ref[...]


def _modulation(cv, w_mod, b_mod):
    depth, d, n = w_mod.shape
    tn = 1536
    return pl.pallas_call(
        _mod_kernel,
        grid=(depth, n // tn),
        in_specs=[pl.BlockSpec((MOD_ROWS, d), lambda l, j: (0, 0)),
                  pl.BlockSpec((None, d, tn), lambda l, j: (l, 0, j)),
                  pl.BlockSpec((None, 1, tn), lambda l, j: (l, 0, j))],
        out_specs=pl.BlockSpec((None, MOD_ROWS, tn), lambda l, j: (l, 0, j)),
        out_shape=jax.ShapeDtypeStruct((depth, MOD_ROWS, n), F32),
        compiler_params=_params(2),
        name="modulation",
    )(cv, w_mod, b_mod.reshape(depth, 1, n))


def _prenorm_kernel(tmod_ref, x_ref, g_ref, mod_ref, o_ref):
    del tmod_ref
    x = x_ref[...]
    h = x * _rms(x) * g_ref[...]
    o_ref[...] = (h * (1.0 + mod_ref[1:2, :]) + mod_ref[0:1, :]).astype(o_ref.dtype)


def _prenorm(x, g, mod, layer, tile_mod, tm):
    n, d = x.shape
    return pl.pallas_call(
        _prenorm_kernel,
        grid_spec=pltpu.PrefetchScalarGridSpec(
            num_scalar_prefetch=1, grid=(n // tm,),
            in_specs=[pl.BlockSpec((tm, d), lambda i, t: (i, 0)),
                      pl.BlockSpec((None, 1, d), lambda i, t: (layer, 0, 0)),
                      pl.BlockSpec((None, None, 6, d), lambda i, t: (layer, t[i], 0, 0))],
            out_specs=pl.BlockSpec((tm, d), lambda i, t: (i, 0))),
        out_shape=jax.ShapeDtypeStruct((n, d), BF16),
        compiler_params=_params(1),
        name="prenorm",
    )(tile_mod, x, g, mod)


def _proj_kernel(h_ref, w_ref, o_ref, wb_ref):
    @pl.when(pl.program_id(1) == 0)
    def _():
        wb_ref[...] = w_ref[...].astype(BF16)

    o_ref[...] = jnp.dot(h_ref[...], wb_ref[...], preferred_element_type=F32).astype(o_ref.dtype)


def _project(h, w, layer, col0, width, out_dtype):
    n, d = h.shape
    tm = PROJ_ROWS
    tn = PROJ_TILE
    cb = col0 // tn
    return pl.pallas_call(
        _proj_kernel,
        grid=(width // tn, n // tm),
        in_specs=[pl.BlockSpec((tm, d), lambda j, i: (i, 0)),
                  pl.BlockSpec((None, d, tn), lambda j, i: (layer, 0, cb + j))],
        out_specs=pl.BlockSpec((tm, tn), lambda j, i: (i, j)),
        out_shape=jax.ShapeDtypeStruct((n, width), out_dtype),
        scratch_shapes=[pltpu.VMEM((d, tn), BF16)],
        compiler_params=_params(2),
        name="in_proj",
    )(h, w)


def _head_pair_masks():
    lane = lax.broadcasted_iota(I32, (1, LANES), 1)
    first = lane < NA_HEAD_DIM
    return first, jnp.logical_not(first)


def _attn_ctx_kernel(q_ref, k_ref, v_ref, o_ref):
    masks = _head_pair_masks()
    scale = NA_HEAD_DIM ** -0.5
    for p in range(NA_WIDTH // LANES):
        cols = slice(LANES * p, LANES * (p + 1))
        q2 = q_ref[:, cols] * scale
        k2 = k_ref[:, cols].astype(BF16)
        v2 = v_ref[:, cols].astype(BF16)
        outs = []
        for m in masks:
            qa = jnp.where(m, q2, 0.0).astype(BF16)
            s = lax.dot_general(qa, k2, (((1,), (1,)), ((), ())), preferred_element_type=F32)
            e = jnp.exp(s - jnp.max(s, axis=-1, keepdims=True))
            den = jnp.sum(e, axis=-1, keepdims=True)
            outs.append(jnp.dot(e.astype(BF16), v2, preferred_element_type=F32) / den)
        o_ref[:, cols] = jnp.where(masks[0], outs[0], outs[1]).astype(o_ref.dtype)


def _attention_ctx(z_qkv, n_seq, seq):
    return pl.pallas_call(
        _attn_ctx_kernel,
        grid=(n_seq,),
        in_specs=[pl.BlockSpec((seq, NA_WIDTH), lambda b: (b, 0)),
                  pl.BlockSpec((seq, NA_WIDTH), lambda b: (b, 1)),
                  pl.BlockSpec((seq, NA_WIDTH), lambda b: (b, 2))],
        out_specs=pl.BlockSpec((seq, NA_WIDTH), lambda b: (b, 0)),
        out_shape=jax.ShapeDtypeStruct((n_seq * seq, NA_WIDTH), BF16),
        compiler_params=_params(1),
        name="attn_ctx",
    )(z_qkv, z_qkv, z_qkv)


def _attn_lat_kernel(q_ref, k_ref, v_ref, kc_ref, vc_ref, bias_ref, o_ref, kb_ref, vb_ref, *, tq):
    masks = _head_pair_masks()
    scale = NA_HEAD_DIM ** -0.5
    seq = q_ref.shape[0]
    slab = bias_ref.shape[3]
    per_half = seq // 2 // tq
    kb_ref[...] = k_ref[...].astype(BF16)
    vb_ref[...] = v_ref[...].astype(BF16)
    kc = kc_ref[...].astype(BF16)
    vc = vc_ref[...].astype(BF16)
    nt = (((1,), (1,)), ((), ()))

    def q_tile(qi, carry):
        rows = pl.ds(pl.multiple_of(qi * tq, tq), tq)
        half = qi // per_half
        half_rows = pl.ds(pl.multiple_of((qi % per_half) * tq, tq), tq)
        keys = pl.ds(pl.multiple_of(half * (seq - slab), seq - slab), slab)
        k2 = kb_ref[keys, :]
        v2 = vb_ref[keys, :]
        q2 = q_ref[rows, :] * scale
        outs = []
        for hh, m in enumerate(masks):
            qa = jnp.where(m, q2, 0.0).astype(BF16)
            s_lat = lax.dot_general(qa, k2, nt, preferred_element_type=F32) + bias_ref[hh, half, half_rows, :]
            s_ctx = lax.dot_general(qa, kc, nt, preferred_element_type=F32)
            mx = jnp.maximum(jnp.max(s_lat, axis=-1, keepdims=True), jnp.max(s_ctx, axis=-1, keepdims=True))
            e_lat = jnp.exp(s_lat - mx)
            e_ctx = jnp.exp(s_ctx - mx)
            den = jnp.sum(e_lat, axis=-1, keepdims=True) + jnp.sum(e_ctx, axis=-1, keepdims=True)
            o = (jnp.dot(e_lat.astype(BF16), v2, preferred_element_type=F32)
                 + jnp.dot(e_ctx.astype(BF16), vc, preferred_element_type=F32))
            outs.append(o / den)
        o_ref[rows, :] = jnp.where(masks[0], outs[0], outs[1]).astype(o_ref.dtype)
        return carry

    lax.fori_loop(0, q_ref.shape[0] // tq, q_tile, 0)


def _attention_lat(z_qkv, cache_k, cache_v, bias, layer, row0, n_seq, seq):
    past = cache_k.shape[2]
    pairs = NA_WIDTH // LANES
    rb = row0 // seq
    kv_cols = NA_WIDTH // LANES
    slab = bias.shape[-1]
    return pl.pallas_call(
        functools.partial(_attn_lat_kernel, tq=min(seq // 2, ATTN_Q_TILE)),
        grid=(pairs, n_seq),
        in_specs=[pl.BlockSpec((seq, LANES), lambda p, b: (rb + b, p)),
                  pl.BlockSpec((seq, LANES), lambda p, b: (rb + b, kv_cols + p)),
                  pl.BlockSpec((seq, LANES), lambda p, b: (rb + b, 2 * kv_cols + p)),
                  pl.BlockSpec((None, None, past, LANES), lambda p, b: (b, layer, 0, p)),
                  pl.BlockSpec((None, None, past, LANES), lambda p, b: (b, layer, 0, p)),
                  pl.BlockSpec((None, 2, 2, seq // 2, slab), lambda p, b: (layer, p, 0, 0, 0))],
        out_specs=pl.BlockSpec((seq, LANES), lambda p, b: (b, p)),
        out_shape=jax.ShapeDtypeStruct((n_seq * seq, NA_WIDTH), BF16),
        scratch_shapes=[pltpu.VMEM((seq, LANES), BF16), pltpu.VMEM((seq, LANES), BF16)],
        compiler_params=_params(2),
        name="attn_lat",
    )(z_qkv, z_qkv, z_qkv, cache_k, cache_v, bias)


def _neighbourhood_bias(rpb, seq):
    rows = seq // GRID_W
    kh = WIN_H
    assert rows >= WIN_H
    lead = rpb.shape[:-2]
    c = np.arange(GRID_W)
    q_cs = np.clip(c - WIN_W // 2, 0, GRID_W - WIN_W)
    col_ok = (c[None, :] >= q_cs[:, None]) & (c[None, :] < q_cs[:, None] + WIN_W)
    pad_c = GRID_W - WIN_W
    pc = jnp.pad(rpb, ((0, 0),) * (rpb.ndim - 1) + ((pad_c, pad_c),), mode="edge")
    w = jnp.stack([pc[..., GRID_W - 1 - cq:2 * GRID_W - 1 - cq] for cq in range(GRID_W)], axis=-3)
    w = jnp.where(jnp.asarray(col_ok)[:, None, :], w, NEG_INF)
    r = np.arange(rows)
    rs = np.clip(r - kh // 2, 0, rows - kh)
    blocks = []
    for rq in range(rows):
        base = _key_slab_start(rq // (rows // 2), rows)
        lo = int(rs[rq]) - rq + WIN_H - 1
        slab = w[..., lo:lo + kh, :].reshape(lead + (GRID_W, kh * GRID_W))
        left = int(rs[rq]) - base
        assert 0 <= left <= KEY_SLAB_ROWS - kh
        pad = ((0, 0),) * (len(lead) + 1) + ((left * GRID_W, (KEY_SLAB_ROWS - kh - left) * GRID_W),)
        blocks.append(jnp.pad(slab, pad, constant_values=NEG_INF))
    return jnp.stack(blocks, axis=-3).reshape(lead + (2, seq // 2, KEY_SLAB_ROWS * GRID_W))


def _key_slab_start(half, rows):
    return 0 if half == 0 else rows - KEY_SLAB_ROWS


def _fourier_kernel(u_ref, ct2_ref, cc_ref, sc_ref, o_ref, pq_ref):
    t = u_ref.shape[0]
    for g in range(FOURIER_GROUPS):
        cols = slice(FOURIER_GROUP_DIM * g, FOURIER_GROUP_DIM * (g + 1))
        ug = u_ref[:, cols].astype(BF16)
        pq_ref[0:t, cols] = jnp.dot(ug, cc_ref[...], preferred_element_type=F32).astype(BF16)
        pq_ref[t:2 * t, cols] = jnp.dot(ug, sc_ref[...], preferred_element_type=F32).astype(BF16)
    o_ref[...] = jnp.dot(ct2_ref[...], pq_ref[...], preferred_element_type=F32).astype(o_ref.dtype)


def _dft_tables(t):
    def cs(n):
        k = np.arange(n, dtype=np.int64)
        ang = 2.0 * np.pi * ((k[:, None] * k[None, :]) % n).astype(np.float64) / n
        return np.cos(ang) / np.sqrt(n), np.sin(ang) / np.sqrt(n)

    ct, st = cs(t)
    cc, sc = cs(FOURIER_GROUP_DIM)
    ct2 = np.concatenate([ct, -st], axis=1).astype(np.float32)
    return (jnp.asarray(ct2).astype(BF16), jnp.asarray(cc.astype(np.float32)).astype(BF16),
            jnp.asarray(sc.astype(np.float32)).astype(BF16))


def _fourier(z_qkvu, row0, n_seq, seq):
    ct2, cc, sc = _dft_tables(seq)
    width = FOURIER_GROUPS * FOURIER_GROUP_DIM
    rb = row0 // seq
    ucol = 3 * NA_WIDTH // width
    return pl.pallas_call(
        _fourier_kernel,
        grid=(n_seq,),
        in_specs=[pl.BlockSpec((seq, width), lambda b: (rb + b, ucol)),
                  pl.BlockSpec((seq, 2 * seq), lambda b: (0, 0)),
                  pl.BlockSpec((FOURIER_GROUP_DIM, FOURIER_GROUP_DIM), lambda b: (0, 0)),
                  pl.BlockSpec((FOURIER_GROUP_DIM, FOURIER_GROUP_DIM), lambda b: (0, 0))],
        out_specs=pl.BlockSpec((seq, width), lambda b: (b, 0)),
        out_shape=jax.ShapeDtypeStruct((n_seq * seq, width), BF16),
        scratch_shapes=[pltpu.VMEM((2 * seq, width), BF16)],
        compiler_params=_params(1),
        name="fourier",
    )(z_qkvu, ct2, cc, sc)


def _rotary_tables(t):
    pos = np.arange(t)
    row = (pos // GRID_W).astype(np.float64)
    col = (pos % GRID_W).astype(np.float64)
    nf = RET_KEY_DIM // 4
    inv_freq = ROPE_BASE ** (-np.arange(nf, dtype=np.float64) / nf)
    ar = row[:, None] * inv_freq[None]
    ac = col[:, None] * inv_freq[None]
    cos = np.concatenate([np.cos(ar), np.cos(ar), np.cos(ac), np.cos(ac)], axis=1)
    sin = np.concatenate([-np.sin(ar), np.sin(ar), -np.sin(ac), np.sin(ac)], axis=1)
    return jnp.asarray(cos.astype(np.float32)), jnp.asarray(sin.astype(np.float32))


def _ret_kernel(lgf_ref, lgb_ref, *refs, t, tq, layer, rotary, state_in, state_out):
    refs = list(refs)
    q_ref, k_ref, v_ref, g_ref = refs[:4]
    refs = refs[4:]
    if rotary:
        cos_ref, sin_ref = refs[:2]
        refs = refs[2:]
    if state_in:
        sf0_ref, sb0_ref = refs[:2]
        refs = refs[2:]
    o_ref = refs[0]
    refs = refs[1:]
    if state_out:
        sf_ref, sb_ref = refs[:2]
        refs = refs[2:]
    dec_ref, kb_ref = refs

    h = pl.program_id(0)
    lgf = lgf_ref[layer * RET_HEADS + h]
    lgb = lgb_ref[layer * RET_HEADS + h]
    scale = RET_KEY_DIM ** -0.5
    nq = t // tq

    @pl.when(pl.program_id(1) == 0)
    def _():
        def fill(ri, c):
            rows = pl.ds(pl.multiple_of(ri * tq, tq), tq)
            i = lax.broadcasted_iota(I32, (tq, t), 0) + ri * tq
            j = lax.broadcasted_iota(I32, (tq, t), 1)
            d = (i - j).astype(F32)
            m = jnp.exp(jnp.abs(d) * jnp.where(d > 0, lgf, lgb))
            dec_ref[rows, :] = jnp.where(d == 0, 2.0, m)
            return c

        lax.fori_loop(0, nq, fill, 0)

    if rotary:
        lane = lax.broadcasted_iota(I32, (1, LANES), 1)
        low = (lane % (RET_KEY_DIM // 2)) < (RET_KEY_DIM // 4)

        def rot(x, rows):
            swapped = jnp.where(low, pltpu.roll(x, LANES - RET_KEY_DIM // 4, 1), pltpu.roll(x, RET_KEY_DIM // 4, 1))
            return x * cos_ref[rows, :] + swapped * sin_ref[rows, :]
    else:
        def rot(x, rows):
            return x

    all_rows = slice(0, t)
    kr = rot(k_ref[...].astype(F32), all_rows)
    kb_ref[...] = kr.astype(BF16)
    vb = v_ref[...]

    if state_out:
        j = lax.broadcasted_iota(I32, (t, 1), 0).astype(F32)
        tn = (((0,), (0,)), ((), ()))
        kf = (kr * (scale * jnp.exp(lgf * (t - 1.0 - j)))).astype(BF16)
        kbw = (kr * (scale * jnp.exp(lgb * j))).astype(BF16)
        sf = lax.dot_general(kf, vb, tn, preferred_element_type=F32)
        sb = lax.dot_general(kbw, vb, tn, preferred_element_type=F32)
        if state_in:
            sf = sf + jnp.exp(lgf * t) * sf0_ref[...]
            sb = sb + jnp.exp(lgb * t) * sb0_ref[...]
        sf_ref[...] = sf
        sb_ref[...] = sb

    def q_tile(qi, carry):
        r0 = pl.multiple_of(qi * tq, tq)
        rows = pl.ds(r0, tq)
        qr = rot(q_ref[rows, :].astype(F32), rows)
        s = lax.dot_general((qr * scale).astype(BF16), kb_ref[...], (((1,), (1,)), ((), ())),
                            preferred_element_type=F32)
        y = jnp.dot((s * dec_ref[rows, :]).astype(BF16), vb, preferred_element_type=F32)
        if state_in:
            pos = (lax.broadcasted_iota(I32, (tq, 1), 0) + r0).astype(F32)
            qf = (qr * jnp.exp(lgf * (pos + 1.0))).astype(BF16)
            qb = (qr * jnp.exp(lgb * (t - pos))).astype(BF16)
            y = (y + jnp.dot(qf, sf0_ref[...].astype(BF16), preferred_element_type=F32)
                 + jnp.dot(qb, sb0_ref[...].astype(BF16), preferred_element_type=F32))
        mean = jnp.mean(y, axis=-1, keepdims=True)
        yc = y - mean
        yn = yc * lax.rsqrt(jnp.mean(yc * yc, axis=-1, keepdims=True) + EPS)
        g = g_ref[rows, :].astype(F32)
        o_ref[rows, :] = (g * jax.nn.sigmoid(g) * yn).astype(o_ref.dtype)
        return carry

    lax.fori_loop(0, nq, q_tile, 0)


def _retention(z_ret, lg_f, lg_b, layer, row0, n_seq, seq, *, rotary, states=None, state_out=False):
    rb = row0 // seq
    cb = BRANCH_WIDTH // LANES
    tq = min(seq, ATTN_Q_TILE)
    state_in = states is not None
    in_specs = [pl.BlockSpec((seq, LANES), lambda h, b, *_: (rb + b, 0 * cb + h)),
                pl.BlockSpec((seq, LANES), lambda h, b, *_: (rb + b, 1 * cb + h)),
                pl.BlockSpec((seq, LANES), lambda h, b, *_: (rb + b, 2 * cb + h)),
                pl.BlockSpec((seq, LANES), lambda h, b, *_: (rb + b, 3 * cb + h))]
    args = [z_ret, z_ret, z_ret, z_ret]
    if rotary:
        cos, sin = _rotary_tables(seq)
        in_specs += [pl.BlockSpec((seq, LANES), lambda h, b, *_: (0, 0))] * 2
        args += [cos, sin]
    if state_in:
        st_spec = pl.BlockSpec((None, None, None, RET_KEY_DIM, RET_KEY_DIM), lambda h, b, *_: (b, layer, h, 0, 0))
        in_specs += [st_spec, st_spec]
        args += list(states)
    out_specs = [pl.BlockSpec((seq, LANES), lambda h, b, *_: (b, h))]
    out_shape = [jax.ShapeDtypeStruct((n_seq * seq, RET_HEADS * LANES), BF16)]
    if state_out:
        so = pl.BlockSpec((None, None, RET_KEY_DIM, RET_KEY_DIM), lambda h, b, *_: (b, h, 0, 0))
        out_specs += [so, so]
        out_shape += [jax.ShapeDtypeStruct((n_seq, RET_HEADS, RET_KEY_DIM, RET_KEY_DIM), F32)] * 2
    return pl.pallas_call(
        functools.partial(_ret_kernel, t=seq, tq=tq, layer=layer, rotary=rotary, state_in=state_in,
                          state_out=state_out),
        grid_spec=pltpu.PrefetchScalarGridSpec(
            num_scalar_prefetch=2, grid=(RET_HEADS, n_seq),
            in_specs=in_specs, out_specs=out_specs,
            scratch_shapes=[pltpu.VMEM((seq, seq), F32), pltpu.VMEM((seq, LANES), BF16)]),
        out_shape=out_shape,
        compiler_params=_params(2),
        name="retention",
    )(lg_f, lg_b, *args)


def _split_dot_nt(w, x):
    nt = (((1,), (1,)), ((), ()))
    w_hi = w.astype(BF16)
    w_lo = (w - w_hi.astype(F32)).astype(BF16)
    x_hi = x.astype(BF16)
    x_lo = (x - x_hi.astype(F32)).astype(BF16)
    return (lax.dot_general(w_hi, x_hi, nt, preferred_element_type=F32)
            + lax.dot_general(w_hi, x_lo, nt, preferred_element_type=F32)
            + lax.dot_general(w_lo, x_hi, nt, preferred_element_type=F32))


def _finish_kernel(tmod_ref, ac_ref, al_ref, fc_ref, fl_ref, rc_ref, rl_ref, zg_ref, x_ref, mod_ref,
                   gpost_ref, gpre_ref, wb_ref, wo_ref, wrt_ref, br_ref, tri_ref, ltri_ref,
                   x1_ref, h2_ref, wts_ref, lpos_ref, seg_ref, off_ref, *, ctx_tiles):
    del tmod_ref
    d = D_MODEL
    is_ctx = pl.program_id(0) < ctx_tiles

    def branch(c_ref, l_ref):
        return jnp.where(is_ctx, c_ref[...], l_ref[...])

    def gate(j):
        return jax.nn.sigmoid(zg_ref[:, d * j:d * (j + 1)].astype(F32))

    merged = (gate(0) * jnp.dot(branch(ac_ref, al_ref), wb_ref[0], preferred_element_type=F32)
              + gate(1) * jnp.dot(branch(fc_ref, fl_ref), wb_ref[1], preferred_element_type=F32)
              + gate(2) * jnp.dot(branch(rc_ref, rl_ref), wb_ref[2], preferred_element_type=F32))
    y = jnp.dot(merged.astype(BF16), wo_ref[...], preferred_element_type=F32)
    x1 = x_ref[...] + mod_ref[2:3, :] * (y * _rms(y) * gpost_ref[...])
    x1_ref[...] = x1
    h2 = x1 * _rms(x1) * gpre_ref[...] * (1.0 + mod_ref[4:5, :]) + mod_ref[3:4, :]
    h2_ref[...] = h2

    logits = _split_dot_nt(wrt_ref[...], h2) + br_ref[:, 0:1]
    tm = logits.shape[1]
    eidx = lax.broadcasted_iota(I32, (N_EXPERTS, tm), 0)
    cur = logits
    vals, hots = [], []
    for k in range(TOP_K):
        m = jnp.max(cur, axis=0, keepdims=True)
        sel = jnp.min(jnp.where(cur == m, eidx, N_EXPERTS), axis=0, keepdims=True)
        hot = eidx == sel
        vals.append(m)
        hots.append(hot)
        cur = jnp.where(hot, -jnp.inf, cur)
    exps = [jnp.exp(v - vals[0]) for v in vals]
    den = exps[0] + exps[1] + exps[2] + exps[3]
    for k in range(TOP_K):
        wts_ref[k:k + 1, :] = exps[k] / den

    member = jnp.logical_or(jnp.logical_or(hots[0], hots[1]), jnp.logical_or(hots[2], hots[3]))
    member_f = member.astype(F32)
    before = jnp.dot(member_f.astype(BF16), tri_ref[...], preferred_element_type=F32)
    units = jnp.ceil(jnp.sum(member_f, axis=1, keepdims=True) * (1.0 / SEG_ALIGN))
    units = jnp.broadcast_to(units, seg_ref.shape)
    off = jnp.dot(ltri_ref[...], units.astype(BF16), preferred_element_type=F32) * SEG_ALIGN
    seg_ref[...] = units * SEG_ALIGN
    off_ref[...] = off
    place = before + off[:, 0:1]
    for k in range(TOP_K):
        lpos_ref[k:k + 1, :] = jnp.sum(jnp.where(hots[k], place, 0.0), axis=0, keepdims=True).astype(I32)


def _finish(branches, z_gate, x, mod, g_post, g_pre_ffn, w_branch, w_out, w_router_t, b_router, layer, tile_mod):
    n, d = x.shape
    tm = FIN_TILE
    ctx_tiles = branches[0][0].shape[0] // tm
    tri = jnp.asarray(np.triu(np.ones((tm, tm), np.float32), k=1)).astype(BF16)
    ltri = jnp.asarray(np.tril(np.ones((N_EXPERTS, N_EXPERTS), np.float32), k=-1)).astype(BF16)
    row = lambda i, t: (i, 0)
    ctx_row = lambda i, t: (jnp.minimum(i, ctx_tiles - 1), 0)
    lat_row = lambda i, t: (jnp.maximum(i - ctx_tiles, 0), 0)
    const2 = lambda i, t: (0, 0)
    lay3 = lambda i, t: (layer, 0, 0)
    col = lambda i, t: (0, i)
    branch_specs, branch_args = [], []
    for c_arr, l_arr in branches:
        branch_specs += [pl.BlockSpec((tm, BRANCH_WIDTH), ctx_row), pl.BlockSpec((tm, BRANCH_WIDTH), lat_row)]
        branch_args += [c_arr, l_arr]
    outs = pl.pallas_call(
        functools.partial(_finish_kernel, ctx_tiles=ctx_tiles),
        grid_spec=pltpu.PrefetchScalarGridSpec(
            num_scalar_prefetch=1, grid=(n // tm,),
            in_specs=branch_specs + [
                      pl.BlockSpec((tm, GATE_W), row),
                      pl.BlockSpec((tm, d), row),
                      pl.BlockSpec((None, None, 6, d), lambda i, t: (layer, t[i], 0, 0)),
                      pl.BlockSpec((None, 1, d), lay3),
                      pl.BlockSpec((None, 1, d), lay3),
                      pl.BlockSpec((None, 3, BRANCH_WIDTH, d), lambda i, t: (layer, 0, 0, 0)),
                      pl.BlockSpec((None, d, d), lay3),
                      pl.BlockSpec((None, N_EXPERTS, d), lay3),
                      pl.BlockSpec((None, N_EXPERTS, LANES), lay3),
                      pl.BlockSpec((tm, tm), const2),
                      pl.BlockSpec((N_EXPERTS, N_EXPERTS), const2)],
            out_specs=[pl.BlockSpec((tm, d), row),
                       pl.BlockSpec((tm, d), row),
                       pl.BlockSpec((TOP_K, tm), col),
                       pl.BlockSpec((TOP_K, tm), col),
                       pl.BlockSpec((None, N_EXPERTS, LANES), lambda i, t: (i, 0, 0)),
                       pl.BlockSpec((None, N_EXPERTS, LANES), lambda i, t: (i, 0, 0))]),
        out_shape=[jax.ShapeDtypeStruct((n, d), F32),
                   jax.ShapeDtypeStruct((n, d), F32),
                   jax.ShapeDtypeStruct((TOP_K, n), F32),
                   jax.ShapeDtypeStruct((TOP_K, n), I32),
                   jax.ShapeDtypeStruct((n // tm, N_EXPERTS, LANES), F32),
                   jax.ShapeDtypeStruct((n // tm, N_EXPERTS, LANES), F32)],
        compiler_params=_params(1),
        name="merge_router",
    )(tile_mod, *branch_args, z_gate, x, mod, g_post, g_pre_ffn, w_branch, w_out, w_router_t, b_router, tri, ltri)
    return outs


def _segment_chunks(length, src_ref, src0, dst_ref, dst0, sem, max_chunk, fixed_src=False):
    out = []
    chunk = max_chunk
    while chunk >= SEG_ALIGN:
        done = jnp.bitwise_and(length, ~(2 * chunk - 1))
        present = jnp.bitwise_and(length, chunk) != 0
        s = 0 if fixed_src else pl.multiple_of(src0 + done, SEG_ALIGN)
        dd = pl.multiple_of(dst0 + done, SEG_ALIGN)
        out.append((present, pltpu.make_async_copy(src_ref.at[pl.ds(s, chunk)], dst_ref.at[pl.ds(dd, chunk)], sem)))
        chunk //= 2
    return out


def _for_each_chunk(n_segments, chunks_of, action):
    def body(e, c):
        for present, cp in chunks_of(e):
            pl.when(present)(functools.partial(action, cp))
        return c

    lax.fori_loop(0, n_segments, body, 0)


def _start(cp):
    cp.start()


def _wait(cp):
    cp.wait()


def _wait_rows(total, src_ref, dst_ref, sem):
    chunk = pl.next_power_of_2(SORT_ROWS) // 2
    while chunk >= SEG_ALIGN:
        @pl.when(jnp.bitwise_and(total, chunk) != 0)
        def _(chunk=chunk):
            pltpu.make_async_copy(src_ref.at[pl.ds(0, chunk)], dst_ref.at[pl.ds(0, chunk)], sem).wait()
        chunk //= 2


def _scatter_kernel(seg_ref, off_ref, pos_ref, used_ref, tpos_ref, tlen_ref, h_ref, lpos_ref, xe_hbm,
                    buf_ref, zero_ref, sem, *, tm):
    t = pl.program_id(0)
    slot = t % 2
    hb = h_ref[...].astype(BF16)
    lp = [lpos_ref[k:k + 1, :] for k in range(TOP_K)]
    rows = buf_ref.shape[1]
    for c0 in range(0, rows, SORT_CHUNK):
        r = lax.broadcasted_iota(I32, (SORT_CHUNK, tm), 0) + c0
        hit = jnp.logical_or(jnp.logical_or(r == lp[0], r == lp[1]), jnp.logical_or(r == lp[2], r == lp[3]))
        onehot = jnp.where(hit, 1.0, 0.0).astype(BF16)
        buf_ref[slot, c0:c0 + SORT_CHUNK, :] = jnp.dot(onehot, hb, preferred_element_type=F32)

    def segments_of(tile):
        def segment(e):
            j = tile * N_EXPERTS + e
            return _segment_chunks(seg_ref[j], buf_ref.at[tile % 2], off_ref[j], xe_hbm, pos_ref[j],
                                   sem.at[tile % 2], tm)
        return segment

    @pl.when(t > 0)
    def _():
        _wait_rows(used_ref[t - 1], buf_ref.at[1 - slot], xe_hbm, sem.at[1 - slot])

    _for_each_chunk(N_EXPERTS, segments_of(t), _start)

    @pl.when(t == pl.num_programs(0) - 1)
    def _():
        _wait_rows(used_ref[t], buf_ref.at[slot], xe_hbm, sem.at[slot])
        zero_ref[...] = jnp.zeros_like(zero_ref)

        def tail(e):
            return _segment_chunks(tlen_ref[e], zero_ref, 0, xe_hbm, tpos_ref[e], sem.at[0], zero_ref.shape[0],
                                   fixed_src=True)

        _for_each_chunk(N_EXPERTS, tail, _start)
        _for_each_chunk(N_EXPERTS, tail, _wait)


def _scatter_rows(h2, lpos, plan, n_rows):
    n, d = h2.shape
    tm = FIN_TILE
    return pl.pallas_call(
        functools.partial(_scatter_kernel, tm=tm),
        grid_spec=pltpu.PrefetchScalarGridSpec(
            num_scalar_prefetch=6, grid=(n // tm,),
            in_specs=[pl.BlockSpec((tm, d), lambda i, *_: (i, 0)),
                      pl.BlockSpec((TOP_K, tm), lambda i, *_: (0, i))],
            out_specs=pl.BlockSpec(memory_space=pl.ANY),
            scratch_shapes=[pltpu.VMEM((2, SORT_ROWS, d), F32), pltpu.VMEM((MOE_BLOCK // 2, d), F32),
                            pltpu.SemaphoreType.DMA((2,))]),
        out_shape=jax.ShapeDtypeStruct((n_rows, d), F32),
        compiler_params=_params(1),
        name="moe_scatter",
    )(plan["seg"], plan["off"], plan["pos"], plan["used"], plan["tail_pos"], plan["tail_len"], h2, lpos)


def _expert_kernel(be_ref, nxt_ref, nu_ref, xb_ref, w1_hbm, b1_ref, w2_hbm, b2_ref, yb_ref,
                   w1s_ref, w2s_ref, w1b_ref, w2b_ref, sem, *, layer):
    i = pl.program_id(0)
    used = i < nu_ref[0]
    fresh = jnp.logical_or(i == 0, be_ref[i] != be_ref[jnp.maximum(i - 1, 0)])

    def fetch(e):
        return (pltpu.make_async_copy(w1_hbm.at[layer, e], w1s_ref, sem.at[0]),
                pltpu.make_async_copy(w2_hbm.at[layer, e], w2s_ref, sem.at[1]))

    @pl.when(i == 0)
    def _():
        for cp in fetch(be_ref[0]):
            cp.start()

    @pl.when(jnp.logical_and(used, fresh))
    def _():
        for cp in fetch(be_ref[i]):
            cp.wait()
        w1b_ref[...] = w1s_ref[...].astype(BF16)
        w2b_ref[...] = w2s_ref[...].astype(BF16)

        @pl.when(nxt_ref[i] >= 0)
        def _():
            for cp in fetch(nxt_ref[i]):
                cp.start()

    @pl.when(used)
    def _():
        z = jnp.dot(xb_ref[...].astype(BF16), w1b_ref[...], preferred_element_type=F32) + b1_ref[...]
        glu = jnp.minimum(z[:, :D_FF], SWIGLU_LIMIT)
        lin = jnp.clip(z[:, D_FF:], -SWIGLU_LIMIT, SWIGLU_LIMIT)
        act = glu * jax.nn.sigmoid(SWIGLU_ALPHA * glu) * (lin + 1.0)
        yb_ref[...] = jnp.dot(act.astype(BF16), w2b_ref[...], preferred_element_type=F32) + b2_ref[...]

    @pl.when(jnp.logical_not(used))
    def _():
        yb_ref[...] = jnp.zeros_like(yb_ref)


def _experts(xb, block_expert, next_expert, n_used, w1, b1, w2, b2, layer):
    n_rows, d = xb.shape
    tm = MOE_BLOCK
    return pl.pallas_call(
        functools.partial(_expert_kernel, layer=layer),
        grid_spec=pltpu.PrefetchScalarGridSpec(
            num_scalar_prefetch=3, grid=(n_rows // tm,),
            in_specs=[pl.BlockSpec((tm, d), lambda i, be, nx, nu: (jnp.minimum(i, nu[0] - 1), 0)),
                      pl.BlockSpec(memory_space=pl.ANY),
                      pl.BlockSpec((None, None, 1, 2 * D_FF), lambda i, be, nx, nu: (layer, be[i], 0, 0)),
                      pl.BlockSpec(memory_space=pl.ANY),
                      pl.BlockSpec((None, None, 1, d), lambda i, be, nx, nu: (layer, be[i], 0, 0))],
            out_specs=pl.BlockSpec((tm, d), lambda i, be, nx, nu: (i, 0)),
            scratch_shapes=[pltpu.VMEM((d, 2 * D_FF), F32), pltpu.VMEM((D_FF, d), F32),
                            pltpu.VMEM((d, 2 * D_FF), BF16), pltpu.VMEM((D_FF, d), BF16),
                            pltpu.SemaphoreType.DMA((2,))]),
        out_shape=jax.ShapeDtypeStruct((n_rows, d), F32),
        compiler_params=_params(1),
        name="moe_experts",
    )(block_expert, next_expert, n_used, xb, w1, b1, w2, b2)


def _gather_kernel(tmod_ref, seg_ref, off_ref, pos_ref, used_ref, ye_hbm, lpos_ref, wts_ref, x1_ref, mod_ref, g_ref,
                   o_ref, buf_ref, sem, *, tm):
    del tmod_ref
    t = pl.program_id(0)
    slot = t % 2

    def segments_of(tile):
        def segment(e):
            j = tile * N_EXPERTS + e
            return _segment_chunks(seg_ref[j], ye_hbm, pos_ref[j], buf_ref.at[tile % 2], off_ref[j],
                                   sem.at[tile % 2], tm)
        return segment

    @pl.when(t == 0)
    def _():
        buf_ref[...] = jnp.zeros_like(buf_ref)
        _for_each_chunk(N_EXPERTS, segments_of(t), _start)

    @pl.when(t + 1 < pl.num_programs(0))
    def _():
        _for_each_chunk(N_EXPERTS, segments_of(t + 1), _start)

    _wait_rows(used_ref[t], ye_hbm, buf_ref.at[slot], sem.at[slot])

    lp = [lpos_ref[:, k:k + 1] for k in range(TOP_K)]
    wt = [wts_ref[:, k:k + 1] for k in range(TOP_K)]
    y = jnp.zeros(o_ref.shape, F32)
    for c0 in range(0, buf_ref.shape[1], SORT_CHUNK):
        r = lax.broadcasted_iota(I32, (tm, SORT_CHUNK), 1) + c0
        wm = jnp.where(r == lp[0], wt[0], 0.0)
        for k in range(1, TOP_K):
            wm = jnp.where(r == lp[k], wt[k], wm)
        y = y + jnp.dot(wm.astype(BF16), buf_ref[slot, c0:c0 + SORT_CHUNK, :].astype(BF16),
                        preferred_element_type=F32)
    o_ref[...] = x1_ref[...] + mod_ref[5:6, :] * (y * _rms(y) * g_ref[...])


def _gather_combine(ye, lpos_t, wts_t, plan, x1, mod, g_post, layer, tile_mod):
    n, d = x1.shape
    tm = FIN_TILE
    row = lambda i, *_: (i, 0)
    return pl.pallas_call(
        functools.partial(_gather_kernel, tm=tm),
        grid_spec=pltpu.PrefetchScalarGridSpec(
            num_scalar_prefetch=5, grid=(n // tm,),
            in_specs=[pl.BlockSpec(memory_space=pl.ANY),
                      pl.BlockSpec((tm, TOP_K), row),
                      pl.BlockSpec((tm, TOP_K), row),
                      pl.BlockSpec((tm, d), row),
                      pl.BlockSpec((None, None, 6, d), lambda i, t, *_: (layer, t[i], 0, 0)),
                      pl.BlockSpec((None, 1, d), lambda i, *_: (layer, 0, 0))],
            out_specs=pl.BlockSpec((tm, d), row),
            scratch_shapes=[pltpu.VMEM((2, SORT_ROWS, d), F32), pltpu.SemaphoreType.DMA((2,))]),
        out_shape=jax.ShapeDtypeStruct((n, d), F32),
        compiler_params=_params(1),
        name="moe_gather",
    )(tile_mod, plan["seg"], plan["off"], plan["pos"], plan["used"], ye, lpos_t, wts_t, x1, mod, g_post)


def _moe(h2, wts, lpos, seg, off, x1, mod, g_post, w1, b1, w2, b2, layer, tile_mod):
    n, d = h2.shape
    blk = MOE_BLOCK
    tm = FIN_TILE
    tiles = n // tm
    n_rows = -(-(n * TOP_K + tiles * N_EXPERTS * (SEG_ALIGN - 1) + N_EXPERTS * (blk - 1)) // blk) * blk
    n_blocks = n_rows // blk
    seg = seg[:, :, 0].astype(I32)
    off = off[:, :, 0].astype(I32)
    rows_e = jnp.sum(seg, axis=0)
    region = (rows_e + blk - 1) // blk * blk
    pend = jnp.cumsum(region)
    pstart = pend - region
    pos = pstart[None, :] + jnp.cumsum(seg, axis=0) - seg
    plan = {"seg": seg.reshape(-1), "off": off.reshape(-1), "pos": pos.reshape(-1).astype(I32),
            "used": jnp.sum(seg, axis=1).astype(I32),
            "tail_pos": (pstart + rows_e).astype(I32), "tail_len": (region - rows_e).astype(I32)}
    blocks = jnp.arange(n_blocks, dtype=I32) * blk
    block_expert = jnp.minimum(jnp.sum(blocks[:, None] >= pend[None, :], axis=1), N_EXPERTS - 1).astype(I32)
    n_used = (pend[-1:] // blk).astype(I32)
    ids = jnp.arange(n_blocks, dtype=I32)
    run_start = jnp.logical_and(jnp.concatenate([jnp.ones((1,), bool), block_expert[1:] != block_expert[:-1]]),
                                ids < n_used[0])
    first_after = lax.cummin(jnp.where(run_start, ids, n_blocks)[::-1])[::-1]
    first_after = jnp.concatenate([first_after[1:], jnp.full((1,), n_blocks, I32)])
    next_expert = jnp.where(first_after < n_blocks, block_expert[jnp.minimum(first_after, n_blocks - 1)], -1)
    xe = _scatter_rows(h2, lpos, plan, n_rows)
    ye = _experts(xe, block_expert, next_expert.astype(I32), n_used, w1, b1, w2, b2, layer)
    return _gather_combine(ye, lpos.T, wts.T, plan, x1, mod, g_post, layer, tile_mod)


def _tile_mod_ids(n_ctx_rows, n_lat_rows, lat_seq, tm):
    ctx = np.zeros((n_ctx_rows // tm,), np.int32)
    lat = 1 + (np.arange(n_lat_rows // tm) * tm) // lat_seq
    return jnp.asarray(np.concatenate([ctx, lat.astype(np.int32)]))


def kernel(x_prompt, x_sample, cache_k, cache_v, state_ret_fwd, state_ret_bwd, c, c_ctx, w_mod, b_mod, g_pre_mix, g_post_mix, g_pre_ffn, g_post_ffn, w_in, na_rel_bias, ret_decay_fwd, ret_decay_bwd, w_branch, w_out, w_router, b_router, w_exp_in, b_exp_in, w_exp_out, b_exp_out):
    batch, seq, d = x_prompt.shape
    dec_batch, dec_seq, _ = x_sample.shape
    depth = w_in.shape[0]
    n_ctx = batch * seq
    n_lat = dec_batch * dec_seq
    assert 1 + dec_batch <= MOD_ROWS

    x = jnp.concatenate([x_prompt.reshape(n_ctx, d), x_sample.reshape(n_lat, d)], axis=0)
    cvec = jnp.concatenate([c_ctx[None], c, jnp.zeros((MOD_ROWS - 1 - dec_batch, d), F32)], axis=0)
    mod_all = _modulation(cvec, w_mod, b_mod).reshape(depth, MOD_ROWS, 6, d)
    tmod = {tm: _tile_mod_ids(n_ctx, n_lat, dec_seq, tm) for tm in (ROW_TILE, FIN_TILE)}
    past = cache_k.shape[2]
    ck = cache_k.reshape(dec_batch, depth, past, NA_WIDTH)
    cv = cache_v.reshape(dec_batch, depth, past, NA_WIDTH)
    lg_f = jax.nn.log_sigmoid(ret_decay_fwd.astype(F32)).reshape(-1)
    lg_b = jax.nn.log_sigmoid(ret_decay_bwd.astype(F32)).reshape(-1)
    bias_all = _neighbourhood_bias(na_rel_bias, dec_seq)
    vec = lambda g: g.reshape(depth, 1, d)
    g_pre_mix, g_post_mix, g_pre_ffn, g_post_ffn = vec(g_pre_mix), vec(g_post_mix), vec(g_pre_ffn), vec(g_post_ffn)
    w_branch_b = w_branch.astype(BF16)
    w_out_b = w_out.astype(BF16)
    w_router_t = jnp.swapaxes(w_router, 1, 2)
    b_router_l = jnp.broadcast_to(b_router[:, :, None], (depth, N_EXPERTS, LANES))
    b_exp_in = b_exp_in.reshape(depth, N_EXPERTS, 1, 2 * D_FF)
    b_exp_out = b_exp_out.reshape(depth, N_EXPERTS, 1, d)

    ks, vs, sfs, sbs = [], [], [], []
    for l in range(depth):
        h = _prenorm(x, g_pre_mix, mod_all, l, tmod[ROW_TILE], ROW_TILE)
        z_qkvu = _project(h, w_in, l, 0, QKVU_W, F32)
        z_ret = _project(h, w_in, l, QKVU_W, RET_W, BF16)
        z_gate = _project(h, w_in, l, QKVU_W + RET_W, GATE_W, BF16)
        ks.append(z_qkvu[:n_ctx, NA_WIDTH:2 * NA_WIDTH].reshape(batch, seq, NA_HEADS, NA_HEAD_DIM))
        vs.append(z_qkvu[:n_ctx, 2 * NA_WIDTH:3 * NA_WIDTH].reshape(batch, seq, NA_HEADS, NA_HEAD_DIM))

        a_pair = (_attention_ctx(z_qkvu, batch, seq),
                  _attention_lat(z_qkvu, ck, cv, bias_all, l, n_ctx, dec_batch, dec_seq))
        f_pair = (_fourier(z_qkvu, 0, batch, seq), _fourier(z_qkvu, n_ctx, dec_batch, dec_seq))
        r_ctx, s_f, s_b = _retention(z_ret, lg_f, lg_b, l, 0, batch, seq, rotary=False, state_out=True)
        (r_lat,) = _retention(z_ret, lg_f, lg_b, l, n_ctx, dec_batch, dec_seq, rotary=True,
                              states=(state_ret_fwd, state_ret_bwd))
        sfs.append(s_f)
        sbs.append(s_b)

        x1, h2, wts, lpos, seg, off = _finish(
            (a_pair, f_pair, (r_ctx, r_lat)), z_gate, x, mod_all, g_post_mix, g_pre_ffn,
            w_branch_b, w_out_b, w_router_t, b_router_l, l, tmod[FIN_TILE])
        x = _moe(h2, wts, lpos, seg, off, x1, mod_all, g_post_ffn,
                 w_exp_in, b_exp_in, w_exp_out, b_exp_out, l, tmod[FIN_TILE])

    y_prompt = x[:n_ctx].reshape(batch, seq, d)
    y_sample = x[n_ctx:].reshape(dec_batch, dec_seq, d)
    return (y_prompt, y_sample, jnp.stack(ks, axis=1), jnp.stack(vs, axis=1),
            jnp.stack(sfs, axis=1), jnp.stack(sbs, axis=1))
```

```python
import functools

import numpy as np
import jax
import jax.numpy as jnp
from jax import lax
from jax.experimental import pallas as pl
from jax.experimental.pallas import tpu as pltpu

F32 = jnp.float32
BF16 = jnp.bfloat16
I32 = jnp.int32

D_MODEL = 1024
GRID_W = 64
NA_HEADS = 8
NA_HEAD_DIM = 64
NA_WIDTH = NA_HEADS * NA_HEAD_DIM
WIN_H = 8
WIN_W = 16
KEY_SLAB_ROWS = 12
FOURIER_GROUPS = 4
FOURIER_GROUP_DIM = 128
RET_HEADS = 4
RET_KEY_DIM = 128
ROPE_BASE = 10000.0
BRANCH_WIDTH = 512
N_EXPERTS = 32
TOP_K = 4
D_FF = 1024
SWIGLU_LIMIT = 7.0
SWIGLU_ALPHA = 1.702
EPS = 1e-6
NEG_INF = -1e30

QKVU_W = 4 * NA_WIDTH
RET_W = 4 * BRANCH_WIDTH
GATE_W = 3 * D_MODEL
PROJ_TILE = 1024
PROJ_ROWS = 2048
ATTN_Q_TILE = 512

LANES = 128
MOD_ROWS = 16
ROW_TILE = 1024
FIN_TILE = 512
MOE_BLOCK = 512
EXPERT_ROW_STEP = 128
SEG_ALIGN = 8
SORT_CHUNK = 256
SORT_ROWS = -(-(FIN_TILE * TOP_K + N_EXPERTS * (SEG_ALIGN - 1)) // SORT_CHUNK) * SORT_CHUNK
VMEM_LIMIT = 56 * 1024 * 1024


def _params(n_axes, vmem=VMEM_LIMIT):
    return pltpu.CompilerParams(dimension_semantics=("arbitrary",) * n_axes, vmem_limit_bytes=vmem)


def _rms(x):
    return lax.rsqrt(jnp.mean(x * x, axis=-1, keepdims=True) + EPS)


def _mod_kernel(cv_ref, w_ref, b_ref, o_ref):
    cv = cv_ref[...]
    s = (cv * jax.nn.sigmoid(cv)).astype(BF16)
    o_ref[...] = jnp.dot(s, w_ref[...].astype(BF16), preferred_element_type=F32) + b_ref[...]


def _modulation(cv, w_mod, b_mod):
    depth, d, n = w_mod.shape
    tn = 1536
    return pl.pallas_call(
        _mod_kernel,
        grid=(depth, n // tn),
        in_specs=[pl.BlockSpec((MOD_ROWS, d), lambda l, j: (0, 0)),
                  pl.BlockSpec((None, d, tn), lambda l, j: (l, 0, j)),
                  pl.BlockSpec((None, 1, tn), lambda l, j: (l, 0, j))],
        out_specs=pl.BlockSpec((None, MOD_ROWS, tn), lambda l, j: (l, 0, j)),
        out_shape=jax.ShapeDtypeStruct((depth, MOD_ROWS, n), F32),
        compiler_params=_params(2),
        name="modulation",
    )(cv, w_mod, b_mod.reshape(depth, 1, n))


def _prenorm_kernel(tmod_ref, x_ref, g_ref, mod_ref, o_ref):
    del tmod_ref
    x = x_ref[...]
    h = x * _rms(x) * g_ref[...]
    o_ref[...] = (h * (1.0 + mod_ref[1:2, :]) + mod_ref[0:1, :]).astype(o_ref.dtype)


def _prenorm(x, g, mod, layer, tile_mod, tm):
    n, d = x.shape
    return pl.pallas_call(
        _prenorm_kernel,
        grid_spec=pltpu.PrefetchScalarGridSpec(
            num_scalar_prefetch=1, grid=(n // tm,),
            in_specs=[pl.BlockSpec((tm, d), lambda i, t: (i, 0)),
                      pl.BlockSpec((None, 1, d), lambda i, t: (layer, 0, 0)),
                      pl.BlockSpec((None, None, 6, d), lambda i, t: (layer, t[i], 0, 0))],
            out_specs=pl.BlockSpec((tm, d), lambda i, t: (i, 0))),
        out_shape=jax.ShapeDtypeStruct((n, d), BF16),
        compiler_params=_params(1),
        name="prenorm",
    )(tile_mod, x, g, mod)


def _proj_kernel(h_ref, w_ref, o_ref, wb_ref):
    @pl.when(pl.program_id(1) == 0)
    def _():
        wb_ref[...] = w_ref[...].astype(BF16)

    o_ref[...] = jnp.dot(h_ref[...], wb_ref[...], preferred_element_type=F32).astype(o_ref.dtype)


def _project(h, w, layer, col0, width, out_dtype):
    n, d = h.shape
    tm = PROJ_ROWS
    tn = PROJ_TILE
    cb = col0 // tn
    return pl.pallas_call(
        _proj_kernel,
        grid=(width // tn, n // tm),
        in_specs=[pl.BlockSpec((tm, d), lambda j, i: (i, 0)),
                  pl.BlockSpec((None, d, tn), lambda j, i: (layer, 0, cb + j))],
        out_specs=pl.BlockSpec((tm, tn), lambda j, i: (i, j)),
        out_shape=jax.ShapeDtypeStruct((n, width), out_dtype),
        scratch_shapes=[pltpu.VMEM((d, tn), BF16)],
        compiler_params=_params(2),
        name="in_proj",
    )(h, w)


def _head_pair_masks():
    lane = lax.broadcasted_iota(I32, (1, LANES), 1)
    first = lane < NA_HEAD_DIM
    return first, jnp.logical_not(first)


def _attn_ctx_kernel(q_ref, k_ref, v_ref, o_ref):
    masks = _head_pair_masks()
    scale = NA_HEAD_DIM ** -0.5
    for p in range(NA_WIDTH // LANES):
        cols = slice(LANES * p, LANES * (p + 1))
        q2 = q_ref[:, cols] * scale
        k2 = k_ref[:, cols].astype(BF16)
        v2 = v_ref[:, cols].astype(BF16)
        outs = []
        for m in masks:
            qa = jnp.where(m, q2, 0.0).astype(BF16)
            s = lax.dot_general(qa, k2, (((1,), (1,)), ((), ())), preferred_element_type=F32)
            e = jnp.exp(s - jnp.max(s, axis=-1, keepdims=True))
            den = jnp.sum(e, axis=-1, keepdims=True)
            outs.append(jnp.dot(e.astype(BF16), v2, preferred_element_type=F32) / den)
        o_ref[:, cols] = jnp.where(masks[0], outs[0], outs[1]).astype(o_ref.dtype)


def _attention_ctx(z_qkv, n_seq, seq):
    return pl.pallas_call(
        _attn_ctx_kernel,
        grid=(n_seq,),
        in_specs=[pl.BlockSpec((seq, NA_WIDTH), lambda b: (b, 0)),
                  pl.BlockSpec((seq, NA_WIDTH), lambda b: (b, 1)),
                  pl.BlockSpec((seq, NA_WIDTH), lambda b: (b, 2))],
        out_specs=pl.BlockSpec((seq, NA_WIDTH), lambda b: (b, 0)),
        out_shape=jax.ShapeDtypeStruct((n_seq * seq, NA_WIDTH), BF16),
        compiler_params=_params(1),
        name="attn_ctx",
    )(z_qkv, z_qkv, z_qkv)


def _attn_lat_kernel(q_ref, k_ref, v_ref, kc_ref, vc_ref, bias_ref, o_ref, kb_ref, vb_ref, *, tq):
    masks = _head_pair_masks()
    scale = NA_HEAD_DIM ** -0.5
    seq = q_ref.shape[0]
    slab = bias_ref.shape[3]
    per_half = seq // 2 // tq
    kb_ref[...] = k_ref[...].astype(BF16)
    vb_ref[...] = v_ref[...].astype(BF16)
    kc = kc_ref[...].astype(BF16)
    vc = vc_ref[...].astype(BF16)
    nt = (((1,), (1,)), ((), ()))

    def q_tile(qi, carry):
        rows = pl.ds(pl.multiple_of(qi * tq, tq), tq)
        half = qi // per_half
        half_rows = pl.ds(pl.multiple_of((qi % per_half) * tq, tq), tq)
        keys = pl.ds(pl.multiple_of(half * (seq - slab), seq - slab), slab)
        k2 = kb_ref[keys, :]
        v2 = vb_ref[keys, :]
        q2 = q_ref[rows, :] * scale
        outs = []
        for hh, m in enumerate(masks):
            qa = jnp.where(m, q2, 0.0).astype(BF16)
            s_lat = lax.dot_general(qa, k2, nt, preferred_element_type=F32) + bias_ref[hh, half, half_rows, :]
            s_ctx = lax.dot_general(qa, kc, nt, preferred_element_type=F32)
            mx = jnp.maximum(jnp.max(s_lat, axis=-1, keepdims=True), jnp.max(s_ctx, axis=-1, keepdims=True))
            e_lat = jnp.exp(s_lat - mx)
            e_ctx = jnp.exp(s_ctx - mx)
            den = jnp.sum(e_lat, axis=-1, keepdims=True) + jnp.sum(e_ctx, axis=-1, keepdims=True)
            o = (jnp.dot(e_lat.astype(BF16), v2, preferred_element_type=F32)
                 + jnp.dot(e_ctx.astype(BF16), vc, preferred_element_type=F32))
            outs.append(o / den)
        o_ref[rows, :] = jnp.where(masks[0], outs[0], outs[1]).astype(o_ref.dtype)
        return carry

    lax.fori_loop(0, q_ref.shape[0] // tq, q_tile, 0)


def _attention_lat(z_qkv, cache_k, cache_v, bias, layer, row0, n_seq, seq):
    past = cache_k.shape[2]
    pairs = NA_WIDTH // LANES
    rb = row0 // seq
    kv_cols = NA_WIDTH // LANES
    slab = bias.shape[-1]
    return pl.pallas_call(
        functools.partial(_attn_lat_kernel, tq=min(seq // 2, ATTN_Q_TILE)),
        grid=(pairs, n_seq),
        in_specs=[pl.BlockSpec((seq, LANES), lambda p, b: (rb + b, p)),
                  pl.BlockSpec((seq, LANES), lambda p, b: (rb + b, kv_cols + p)),
                  pl.BlockSpec((seq, LANES), lambda p, b: (rb + b, 2 * kv_cols + p)),
                  pl.BlockSpec((None, None, past, LANES), lambda p, b: (b, layer, 0, p)),
                  pl.BlockSpec((None, None, past, LANES), lambda p, b: (b, layer, 0, p)),
                  pl.BlockSpec((None, 2, 2, seq // 2, slab), lambda p, b: (layer, p, 0, 0, 0))],
        out_specs=pl.BlockSpec((seq, LANES), lambda p, b: (b, p)),
        out_shape=jax.ShapeDtypeStruct((n_seq * seq, NA_WIDTH), BF16),
        scratch_shapes=[pltpu.VMEM((seq, LANES), BF16), pltpu.VMEM((seq, LANES), BF16)],
        compiler_params=_params(2),
        name="attn_lat",
    )(z_qkv, z_qkv, z_qkv, cache_k, cache_v, bias)


def _neighbourhood_bias(rpb, seq):
    rows = seq // GRID_W
    kh = WIN_H
    assert rows >= WIN_H
    lead = rpb.shape[:-2]
    c = np.arange(GRID_W)
    q_cs = np.clip(c - WIN_W // 2, 0, GRID_W - WIN_W)
    col_ok = (c[None, :] >= q_cs[:, None]) & (c[None, :] < q_cs[:, None] + WIN_W)
    pad_c = GRID_W - WIN_W
    pc = jnp.pad(rpb, ((0, 0),) * (rpb.ndim - 1) + ((pad_c, pad_c),), mode="edge")
    w = jnp.stack([pc[..., GRID_W - 1 - cq:2 * GRID_W - 1 - cq] for cq in range(GRID_W)], axis=-3)
    w = jnp.where(jnp.asarray(col_ok)[:, None, :], w, NEG_INF)
    r = np.arange(rows)
    rs = np.clip(r - kh // 2, 0, rows - kh)
    blocks = []
    for rq in range(rows):
        base = _key_slab_start(rq // (rows // 2), rows)
        lo = int(rs[rq]) - rq + WIN_H - 1
        slab = w[..., lo:lo + kh, :].reshape(lead + (GRID_W, kh * GRID_W))
        left = int(rs[rq]) - base
        assert 0 <= left <= KEY_SLAB_ROWS - kh
        pad = ((0, 0),) * (len(lead) + 1) + ((left * GRID_W, (KEY_SLAB_ROWS - kh - left) * GRID_W),)
        blocks.append(jnp.pad(slab, pad, constant_values=NEG_INF))
    return jnp.stack(blocks, axis=-3).reshape(lead + (2, seq // 2, KEY_SLAB_ROWS * GRID_W))


def _key_slab_start(half, rows):
    return 0 if half == 0 else rows - KEY_SLAB_ROWS


def _fourier_kernel(u_ref, ct2_ref, cc_ref, sc_ref, o_ref, pq_ref):
    t = u_ref.shape[0]
    for g in range(FOURIER_GROUPS):
        cols = slice(FOURIER_GROUP_DIM * g, FOURIER_GROUP_DIM * (g + 1))
        ug = u_ref[:, cols].astype(BF16)
        pq_ref[0:t, cols] = jnp.dot(ug, cc_ref[...], preferred_element_type=F32).astype(BF16)
        pq_ref[t:2 * t, cols] = jnp.dot(ug, sc_ref[...], preferred_element_type=F32).astype(BF16)
    o_ref[...] = jnp.dot(ct2_ref[...], pq_ref[...], preferred_element_type=F32).astype(o_ref.dtype)


def _dft_tables(t):
    def cs(n):
        k = np.arange(n, dtype=np.int64)
        ang = 2.0 * np.pi * ((k[:, None] * k[None, :]) % n).astype(np.float64) / n
        return np.cos(ang) / np.sqrt(n), np.sin(ang) / np.sqrt(n)

    ct, st = cs(t)
    cc, sc = cs(FOURIER_GROUP_DIM)
    ct2 = np.concatenate([ct, -st], axis=1).astype(np.float32)
    return (jnp.asarray(ct2).astype(BF16), jnp.asarray(cc.astype(np.float32)).astype(BF16),
            jnp.asarray(sc.astype(np.float32)).astype(BF16))


def _fourier(z_qkvu, row0, n_seq, seq):
    ct2, cc, sc = _dft_tables(seq)
    width = FOURIER_GROUPS * FOURIER_GROUP_DIM
    rb = row0 // seq
    ucol = 3 * NA_WIDTH // width
    return pl.pallas_call(
        _fourier_kernel,
        grid=(n_seq,),
        in_specs=[pl.BlockSpec((seq, width), lambda b: (rb + b, ucol)),
                  pl.BlockSpec((seq, 2 * seq), lambda b: (0, 0)),
                  pl.BlockSpec((FOURIER_GROUP_DIM, FOURIER_GROUP_DIM), lambda b: (0, 0)),
                  pl.BlockSpec((FOURIER_GROUP_DIM, FOURIER_GROUP_DIM), lambda b: (0, 0))],
        out_specs=pl.BlockSpec((seq, width), lambda b: (b, 0)),
        out_shape=jax.ShapeDtypeStruct((n_seq * seq, width), BF16),
        scratch_shapes=[pltpu.VMEM((2 * seq, width), BF16)],
        compiler_params=_params(1),
        name="fourier",
    )(z_qkvu, ct2, cc, sc)


def _rotary_tables(t):
    pos = np.arange(t)
    row = (pos // GRID_W).astype(np.float64)
    col = (pos % GRID_W).astype(np.float64)
    nf = RET_KEY_DIM // 4
    inv_freq = ROPE_BASE ** (-np.arange(nf, dtype=np.float64) / nf)
    ar = row[:, None] * inv_freq[None]
    ac = col[:, None] * inv_freq[None]
    cos = np.concatenate([np.cos(ar), np.cos(ar), np.cos(ac), np.cos(ac)], axis=1)
    sin = np.concatenate([-np.sin(ar), np.sin(ar), -np.sin(ac), np.sin(ac)], axis=1)
    return jnp.asarray(cos.astype(np.float32)), jnp.asarray(sin.astype(np.float32))


def _ret_kernel(lgf_ref, lgb_ref, *refs, t, tq, layer, rotary, state_in, state_out):
    refs = list(refs)
    q_ref, k_ref, v_ref, g_ref = refs[:4]
    refs = refs[4:]
    if rotary:
        cos_ref, sin_ref = refs[:2]
        refs = refs[2:]
    if state_in:
        sf0_ref, sb0_ref = refs[:2]
        refs = refs[2:]
    o_ref = refs[0]
    refs = refs[1:]
    if state_out:
        sf_ref, sb_ref = refs[:2]
        refs = refs[2:]
    dec_ref, kb_ref = refs

    h = pl.program_id(0)
    lgf = lgf_ref[layer * RET_HEADS + h]
    lgb = lgb_ref[layer * RET_HEADS + h]
    scale = RET_KEY_DIM ** -0.5
    nq = t // tq

    @pl.when(pl.program_id(1) == 0)
    def _():
        def fill(ri, c):
            rows = pl.ds(pl.multiple_of(ri * tq, tq), tq)
            i = lax.broadcasted_iota(I32, (tq, t), 0) + ri * tq
            j = lax.broadcasted_iota(I32, (tq, t), 1)
            d = (i - j).astype(F32)
            m = jnp.exp(jnp.abs(d) * jnp.where(d > 0, lgf, lgb))
            dec_ref[rows, :] = jnp.where(d == 0, 2.0, m)
            return c

        lax.fori_loop(0, nq, fill, 0)

    if rotary:
        lane = lax.broadcasted_iota(I32, (1, LANES), 1)
        low = (lane % (RET_KEY_DIM // 2)) < (RET_KEY_DIM // 4)

        def rot(x, rows):
            swapped = jnp.where(low, pltpu.roll(x, LANES - RET_KEY_DIM // 4, 1), pltpu.roll(x, RET_KEY_DIM // 4, 1))
            return x * cos_ref[rows, :] + swapped * sin_ref[rows, :]
    else:
        def rot(x, rows):
            return x

    all_rows = slice(0, t)
    kr = rot(k_ref[...].astype(F32), all_rows)
    kb_ref[...] = kr.astype(BF16)
    vb = v_ref[...]

    if state_out:
        j = lax.broadcasted_iota(I32, (t, 1), 0).astype(F32)
        tn = (((0,), (0,)), ((), ()))
        kf = (kr * (scale * jnp.exp(lgf * (t - 1.0 - j)))).astype(BF16)
        kbw = (kr * (scale * jnp.exp(lgb * j))).astype(BF16)
        sf = lax.dot_general(kf, vb, tn, preferred_element_type=F32)
        sb = lax.dot_general(kbw, vb, tn, preferred_element_type=F32)
        if state_in:
            sf = sf + jnp.exp(lgf * t) * sf0_ref[...]
            sb = sb + jnp.exp(lgb * t) * sb0_ref[...]
        sf_ref[...] = sf
        sb_ref[...] = sb

    def q_tile(qi, carry):
        r0 = pl.multiple_of(qi * tq, tq)
        rows = pl.ds(r0, tq)
        qr = rot(q_ref[rows, :].astype(F32), rows)
        s = lax.dot_general((qr * scale).astype(BF16), kb_ref[...], (((1,), (1,)), ((), ())),
                            preferred_element_type=F32)
        y = jnp.dot((s * dec_ref[rows, :]).astype(BF16), vb, preferred_element_type=F32)
        if state_in:
            pos = (lax.broadcasted_iota(I32, (tq, 1), 0) + r0).astype(F32)
            qf = (qr * jnp.exp(lgf * (pos + 1.0))).astype(BF16)
            qb = (qr * jnp.exp(lgb * (t - pos))).astype(BF16)
            y = (y + jnp.dot(qf, sf0_ref[...].astype(BF16), preferred_element_type=F32)
                 + jnp.dot(qb, sb0_ref[...].astype(BF16), preferred_element_type=F32))
        mean = jnp.mean(y, axis=-1, keepdims=True)
        yc = y - mean
        yn = yc * lax.rsqrt(jnp.mean(yc * yc, axis=-1, keepdims=True) + EPS)
        g = g_ref[rows, :].astype(F32)
        o_ref[rows, :] = (g * jax.nn.sigmoid(g) * yn).astype(o_ref.dtype)
        return carry

    lax.fori_loop(0, nq, q_tile, 0)


def _retention(z_ret, lg_f, lg_b, layer, row0, n_seq, seq, *, rotary, states=None, state_out=False):
    rb = row0 // seq
    cb = BRANCH_WIDTH // LANES
    tq = min(seq, ATTN_Q_TILE)
    state_in = states is not None
    in_specs = [pl.BlockSpec((seq, LANES), lambda h, b, *_: (rb + b, 0 * cb + h)),
                pl.BlockSpec((seq, LANES), lambda h, b, *_: (rb + b, 1 * cb + h)),
                pl.BlockSpec((seq, LANES), lambda h, b, *_: (rb + b, 2 * cb + h)),
                pl.BlockSpec((seq, LANES), lambda h, b, *_: (rb + b, 3 * cb + h))]
    args = [z_ret, z_ret, z_ret, z_ret]
    if rotary:
        cos, sin = _rotary_tables(seq)
        in_specs += [pl.BlockSpec((seq, LANES), lambda h, b, *_: (0, 0))] * 2
        args += [cos, sin]
    if state_in:
        st_spec = pl.BlockSpec((None, None, None, RET_KEY_DIM, RET_KEY_DIM), lambda h, b, *_: (b, layer, h, 0, 0))
        in_specs += [st_spec, st_spec]
        args += list(states)
    out_specs = [pl.BlockSpec((seq, LANES), lambda h, b, *_: (b, h))]
    out_shape = [jax.ShapeDtypeStruct((n_seq * seq, RET_HEADS * LANES), BF16)]
    if state_out:
        so = pl.BlockSpec((None, None, RET_KEY_DIM, RET_KEY_DIM), lambda h, b, *_: (b, h, 0, 0))
        out_specs += [so, so]
        out_shape += [jax.ShapeDtypeStruct((n_seq, RET_HEADS, RET_KEY_DIM, RET_KEY_DIM), F32)] * 2
    return pl.pallas_call(
        functools.partial(_ret_kernel, t=seq, tq=tq, layer=layer, rotary=rotary, state_in=state_in,
                          state_out=state_out),
        grid_spec=pltpu.PrefetchScalarGridSpec(
            num_scalar_prefetch=2, grid=(RET_HEADS, n_seq),
            in_specs=in_specs, out_specs=out_specs,
            scratch_shapes=[pltpu.VMEM((seq, seq), F32), pltpu.VMEM((seq, LANES), BF16)]),
        out_shape=out_shape,
        compiler_params=_params(2),
        name="retention",
    )(lg_f, lg_b, *args)


def _split_dot_nt(w, x):
    nt = (((1,), (1,)), ((), ()))
    w_hi = w.astype(BF16)
    w_lo = (w - w_hi.astype(F32)).astype(BF16)
    x_hi = x.astype(BF16)
    x_lo = (x - x_hi.astype(F32)).astype(BF16)
    return (lax.dot_general(w_hi, x_hi, nt, preferred_element_type=F32)
            + lax.dot_general(w_hi, x_lo, nt, preferred_element_type=F32)
            + lax.dot_general(w_lo, x_hi, nt, preferred_element_type=F32))


def _finish_kernel(tmod_ref, ac_ref, al_ref, fc_ref, fl_ref, rc_ref, rl_ref, zg_ref, x_ref, mod_ref,
                   gpost_ref, gpre_ref, wb_ref, wo_ref, wrt_ref, br_ref, tri_ref, ltri_ref,
                   x1_ref, h2_ref, wts_ref, lpos_ref, seg_ref, off_ref, *, ctx_tiles):
    del tmod_ref
    d = D_MODEL
    is_ctx = pl.program_id(0) < ctx_tiles

    def branch(c_ref, l_ref):
        return jnp.where(is_ctx, c_ref[...], l_ref[...])

    def gate(j):
        return jax.nn.sigmoid(zg_ref[:, d * j:d * (j + 1)].astype(F32))

    merged = (gate(0) * jnp.dot(branch(ac_ref, al_ref), wb_ref[0], preferred_element_type=F32)
              + gate(1) * jnp.dot(branch(fc_ref, fl_ref), wb_ref[1], preferred_element_type=F32)
              + gate(2) * jnp.dot(branch(rc_ref, rl_ref), wb_ref[2], preferred_element_type=F32))
    y = jnp.dot(merged.astype(BF16), wo_ref[...], preferred_element_type=F32)
    x1 = x_ref[...] + mod_ref[2:3, :] * (y * _rms(y) * gpost_ref[...])
    x1_ref[...] = x1
    h2 = x1 * _rms(x1) * gpre_ref[...] * (1.0 + mod_ref[4:5, :]) + mod_ref[3:4, :]
    h2_ref[...] = h2

    logits = _split_dot_nt(wrt_ref[...], h2) + br_ref[:, 0:1]
    tm = logits.shape[1]
    eidx = lax.broadcasted_iota(I32, (N_EXPERTS, tm), 0)
    cur = logits
    vals, hots = [], []
    for k in range(TOP_K):
        m = jnp.max(cur, axis=0, keepdims=True)
        sel = jnp.min(jnp.where(cur == m, eidx, N_EXPERTS), axis=0, keepdims=True)
        hot = eidx == sel
        vals.append(m)
        hots.append(hot)
        cur = jnp.where(hot, -jnp.inf, cur)
    exps = [jnp.exp(v - vals[0]) for v in vals]
    den = exps[0] + exps[1] + exps[2] + exps[3]
    for k in range(TOP_K):
        wts_ref[k:k + 1, :] = exps[k] / den

    member = jnp.logical_or(jnp.logical_or(hots[0], hots[1]), jnp.logical_or(hots[2], hots[3]))
    member_f = member.astype(F32)
    before = jnp.dot(member_f.astype(BF16), tri_ref[...], preferred_element_type=F32)
    units = jnp.ceil(jnp.sum(member_f, axis=1, keepdims=True) * (1.0 / SEG_ALIGN))
    units = jnp.broadcast_to(units, seg_ref.shape)
    off = jnp.dot(ltri_ref[...], units.astype(BF16), preferred_element_type=F32) * SEG_ALIGN
    seg_ref[...] = units * SEG_ALIGN
    off_ref[...] = off
    place = before + off[:, 0:1]
    for k in range(TOP_K):
        lpos_ref[k:k + 1, :] = jnp.sum(jnp.where(hots[k], place, 0.0), axis=0, keepdims=True).astype(I32)


def _finish(branches, z_gate, x, mod, g_post, g_pre_ffn, w_branch, w_out, w_router_t, b_router, layer, tile_mod):
    n, d = x.shape
    tm = FIN_TILE
    ctx_tiles = branches[0][0].shape[0] // tm
    tri = jnp.asarray(np.triu(np.ones((tm, tm), np.float32), k=1)).astype(BF16)
    ltri = jnp.asarray(np.tril(np.ones((N_EXPERTS, N_EXPERTS), np.float32), k=-1)).astype(BF16)
    row = lambda i, t: (i, 0)
    ctx_row = lambda i, t: (jnp.minimum(i, ctx_tiles - 1), 0)
    lat_row = lambda i, t: (jnp.maximum(i - ctx_tiles, 0), 0)
    const2 = lambda i, t: (0, 0)
    lay3 = lambda i, t: (layer, 0, 0)
    col = lambda i, t: (0, i)
    branch_specs, branch_args = [], []
    for c_arr, l_arr in branches:
        branch_specs += [pl.BlockSpec((tm, BRANCH_WIDTH), ctx_row), pl.BlockSpec((tm, BRANCH_WIDTH), lat_row)]
        branch_args += [c_arr, l_arr]
    outs = pl.pallas_call(
        functools.partial(_finish_kernel, ctx_tiles=ctx_tiles),
        grid_spec=pltpu.PrefetchScalarGridSpec(
            num_scalar_prefetch=1, grid=(n // tm,),
            in_specs=branch_specs + [
                      pl.BlockSpec((tm, GATE_W), row),
                      pl.BlockSpec((tm, d), row),
                      pl.BlockSpec((None, None, 6, d), lambda i, t: (layer, t[i], 0, 0)),
                      pl.BlockSpec((None, 1, d), lay3),
                      pl.BlockSpec((None, 1, d), lay3),
                      pl.BlockSpec((None, 3, BRANCH_WIDTH, d), lambda i, t: (layer, 0, 0, 0)),
                      pl.BlockSpec((None, d, d), lay3),
                      pl.BlockSpec((None, N_EXPERTS, d), lay3),
                      pl.BlockSpec((None, N_EXPERTS, LANES), lay3),
                      pl.BlockSpec((tm, tm), const2),
                      pl.BlockSpec((N_EXPERTS, N_EXPERTS), const2)],
            out_specs=[pl.BlockSpec((tm, d), row),
                       pl.BlockSpec((tm, d), row),
                       pl.BlockSpec((TOP_K, tm), col),
                       pl.BlockSpec((TOP_K, tm), col),
                       pl.BlockSpec((None, N_EXPERTS, LANES), lambda i, t: (i, 0, 0)),
                       pl.BlockSpec((None, N_EXPERTS, LANES), lambda i, t: (i, 0, 0))]),
        out_shape=[jax.ShapeDtypeStruct((n, d), F32),
                   jax.ShapeDtypeStruct((n, d), F32),
                   jax.ShapeDtypeStruct((TOP_K, n), F32),
                   jax.ShapeDtypeStruct((TOP_K, n), I32),
                   jax.ShapeDtypeStruct((n // tm, N_EXPERTS, LANES), F32),
                   jax.ShapeDtypeStruct((n // tm, N_EXPERTS, LANES), F32)],
        compiler_params=_params(1),
        name="merge_router",
    )(tile_mod, *branch_args, z_gate, x, mod, g_post, g_pre_ffn, w_branch, w_out, w_router_t, b_router, tri, ltri)
    return outs


def _segment_chunks(length, src_ref, src0, dst_ref, dst0, sem, max_chunk, fixed_src=False):
    out = []
    chunk = max_chunk
    while chunk >= SEG_ALIGN:
        done = jnp.bitwise_and(length, ~(2 * chunk - 1))
        present = jnp.bitwise_and(length, chunk) != 0
        s = 0 if fixed_src else pl.multiple_of(src0 + done, SEG_ALIGN)
        dd = pl.multiple_of(dst0 + done, SEG_ALIGN)
        out.append((present, pltpu.make_async_copy(src_ref.at[pl.ds(s, chunk)], dst_ref.at[pl.ds(dd, chunk)], sem)))
        chunk //= 2
    return out


def _for_each_chunk(n_segments, chunks_of, action):
    def body(e, c):
        for present, cp in chunks_of(e):
            pl.when(present)(functools.partial(action, cp))
        return c

    lax.fori_loop(0, n_segments, body, 0)


def _start(cp):
    cp.start()


def _wait(cp):
    cp.wait()


def _wait_rows(total, src_ref, dst_ref, sem):
    chunk = pl.next_power_of_2(SORT_ROWS) // 2
    while chunk >= SEG_ALIGN:
        @pl.when(jnp.bitwise_and(total, chunk) != 0)
        def _(chunk=chunk):
            pltpu.make_async_copy(src_ref.at[pl.ds(0, chunk)], dst_ref.at[pl.ds(0, chunk)], sem).wait()
        chunk //= 2


def _scatter_kernel(seg_ref, off_ref, pos_ref, used_ref, tpos_ref, tlen_ref, h_ref, lpos_ref, xe_hbm,
                    buf_ref, zero_ref, sem, *, tm):
    t = pl.program_id(0)
    slot = t % 2
    hb = h_ref[...].astype(BF16)
    lp = [lpos_ref[k:k + 1, :] for k in range(TOP_K)]
    rows = buf_ref.shape[1]
    for c0 in range(0, rows, SORT_CHUNK):
        r = lax.broadcasted_iota(I32, (SORT_CHUNK, tm), 0) + c0
        hit = jnp.logical_or(jnp.logical_or(r == lp[0], r == lp[1]), jnp.logical_or(r == lp[2], r == lp[3]))
        onehot = jnp.where(hit, 1.0, 0.0).astype(BF16)
        buf_ref[slot, c0:c0 + SORT_CHUNK, :] = jnp.dot(onehot, hb, preferred_element_type=F32)

    def segments_of(tile):
        def segment(e):
            j = tile * N_EXPERTS + e
            return _segment_chunks(seg_ref[j], buf_ref.at[tile % 2], off_ref[j], xe_hbm, pos_ref[j],
                                   sem.at[tile % 2], tm)
        return segment

    @pl.when(t > 0)
    def _():
        _wait_rows(used_ref[t - 1], buf_ref.at[1 - slot], xe_hbm, sem.at[1 - slot])

    _for_each_chunk(N_EXPERTS, segments_of(t), _start)

    @pl.when(t == pl.num_programs(0) - 1)
    def _():
        _wait_rows(used_ref[t], buf_ref.at[slot], xe_hbm, sem.at[slot])
        zero_ref[...] = jnp.zeros_like(zero_ref)

        def tail(e):
            return _segment_chunks(tlen_ref[e], zero_ref, 0, xe_hbm, tpos_ref[e], sem.at[0], zero_ref.shape[0],
                                   fixed_src=True)

        _for_each_chunk(N_EXPERTS, tail, _start)
        _for_each_chunk(N_EXPERTS, tail, _wait)


def _scatter_rows(h2, lpos, plan, n_rows):
    n, d = h2.shape
    tm = FIN_TILE
    return pl.pallas_call(
        functools.partial(_scatter_kernel, tm=tm),
        grid_spec=pltpu.PrefetchScalarGridSpec(
            num_scalar_prefetch=6, grid=(n // tm,),
            in_specs=[pl.BlockSpec((tm, d), lambda i, *_: (i, 0)),
                      pl.BlockSpec((TOP_K, tm), lambda i, *_: (0, i))],
            out_specs=pl.BlockSpec(memory_space=pl.ANY),
            scratch_shapes=[pltpu.VMEM((2, SORT_ROWS, d), F32), pltpu.VMEM((MOE_BLOCK // 2, d), F32),
                            pltpu.SemaphoreType.DMA((2,))]),
        out_shape=jax.ShapeDtypeStruct((n_rows, d), F32),
        compiler_params=_params(1),
        name="moe_scatter",
    )(plan["seg"], plan["off"], plan["pos"], plan["used"], plan["tail_pos"], plan["tail_len"], h2, lpos)


def _expert_kernel(be_ref, nxt_ref, valid_ref, nu_ref, xb_ref, w1_hbm, b1_ref, w2_hbm, b2_ref, yb_ref,
                   w1s_ref, w2s_ref, w1b_ref, w2b_ref, sem, *, layer):
    i = pl.program_id(0)
    used = i < nu_ref[0]
    fresh = jnp.logical_or(i == 0, be_ref[i] != be_ref[jnp.maximum(i - 1, 0)])

    def fetch(e):
        return (pltpu.make_async_copy(w1_hbm.at[layer, e], w1s_ref, sem.at[0]),
                pltpu.make_async_copy(w2_hbm.at[layer, e], w2s_ref, sem.at[1]))

    @pl.when(i == 0)
    def _():
        for cp in fetch(be_ref[0]):
            cp.start()

    @pl.when(jnp.logical_and(used, fresh))
    def _():
        for cp in fetch(be_ref[i]):
            cp.wait()
        w1b_ref[...] = w1s_ref[...].astype(BF16)
        w2b_ref[...] = w2s_ref[...].astype(BF16)

        @pl.when(nxt_ref[i] >= 0)
        def _():
            for cp in fetch(nxt_ref[i]):
                cp.start()

    def ffn(m):
        z = jnp.dot(xb_ref[0:m, :].astype(BF16), w1b_ref[...], preferred_element_type=F32) + b1_ref[...]
        glu = jnp.minimum(z[:, :D_FF], SWIGLU_LIMIT)
        lin = jnp.clip(z[:, D_FF:], -SWIGLU_LIMIT, SWIGLU_LIMIT)
        act = glu * jax.nn.sigmoid(SWIGLU_ALPHA * glu) * (lin + 1.0)
        yb_ref[0:m, :] = jnp.dot(act.astype(BF16), w2b_ref[...], preferred_element_type=F32) + b2_ref[...]
        if m < yb_ref.shape[0]:
            yb_ref[m:, :] = jnp.zeros((yb_ref.shape[0] - m, yb_ref.shape[1]), F32)

    valid = valid_ref[i]
    for m in range(EXPERT_ROW_STEP, yb_ref.shape[0] + 1, EXPERT_ROW_STEP):
        pl.when(jnp.logical_and(valid > m - EXPERT_ROW_STEP, valid <= m))(functools.partial(ffn, m))

    @pl.when(valid == 0)
    def _():
        yb_ref[...] = jnp.zeros_like(yb_ref)


def _experts(xb, block_expert, next_expert, valid_rows, n_used, w1, b1, w2, b2, layer):
    n_rows, d = xb.shape
    tm = MOE_BLOCK
    return pl.pallas_call(
        functools.partial(_expert_kernel, layer=layer),
        grid_spec=pltpu.PrefetchScalarGridSpec(
            num_scalar_prefetch=4, grid=(n_rows // tm,),
            in_specs=[pl.BlockSpec((tm, d), lambda i, be, nx, vr, nu: (jnp.minimum(i, nu[0] - 1), 0)),
                      pl.BlockSpec(memory_space=pl.ANY),
                      pl.BlockSpec((None, None, 1, 2 * D_FF), lambda i, be, nx, vr, nu: (layer, be[i], 0, 0)),
                      pl.BlockSpec(memory_space=pl.ANY),
                      pl.BlockSpec((None, None, 1, d), lambda i, be, nx, vr, nu: (layer, be[i], 0, 0))],
            out_specs=pl.BlockSpec((tm, d), lambda i, be, nx, vr, nu: (i, 0)),
            scratch_shapes=[pltpu.VMEM((d, 2 * D_FF), F32), pltpu.VMEM((D_FF, d), F32),
                            pltpu.VMEM((d, 2 * D_FF), BF16), pltpu.VMEM((D_FF, d), BF16),
                            pltpu.SemaphoreType.DMA((2,))]),
        out_shape=jax.ShapeDtypeStruct((n_rows, d), F32),
        compiler_params=_params(1),
        name="moe_experts",
    )(block_expert, next_expert, valid_rows, n_used, xb, w1, b1, w2, b2)


def _gather_kernel(tmod_ref, seg_ref, off_ref, pos_ref, used_ref, ye_hbm, lpos_ref, wts_ref, x1_ref, mod_ref, g_ref,
                   o_ref, buf_ref, sem, *, tm):
    del tmod_ref
    t = pl.program_id(0)
    slot = t % 2

    def segments_of(tile):
        def segment(e):
            j = tile * N_EXPERTS + e
            return _segment_chunks(seg_ref[j], ye_hbm, pos_ref[j], buf_ref.at[tile % 2], off_ref[j],
                                   sem.at[tile % 2], tm)
        return segment

    @pl.when(t == 0)
    def _():
        buf_ref[...] = jnp.zeros_like(buf_ref)
        _for_each_chunk(N_EXPERTS, segments_of(t), _start)

    @pl.when(t + 1 < pl.num_programs(0))
    def _():
        _for_each_chunk(N_EXPERTS, segments_of(t + 1), _start)

    _wait_rows(used_ref[t], ye_hbm, buf_ref.at[slot], sem.at[slot])

    lp = [lpos_ref[:, k:k + 1] for k in range(TOP_K)]
    wt = [wts_ref[:, k:k + 1] for k in range(TOP_K)]
    y = jnp.zeros(o_ref.shape, F32)
    for c0 in range(0, buf_ref.shape[1], SORT_CHUNK):
        r = lax.broadcasted_iota(I32, (tm, SORT_CHUNK), 1) + c0
        wm = jnp.where(r == lp[0], wt[0], 0.0)
        for k in range(1, TOP_K):
            wm = jnp.where(r == lp[k], wt[k], wm)
        y = y + jnp.dot(wm.astype(BF16), buf_ref[slot, c0:c0 + SORT_CHUNK, :].astype(BF16),
                        preferred_element_type=F32)
    o_ref[...] = x1_ref[...] + mod_ref[5:6, :] * (y * _rms(y) * g_ref[...])


def _gather_combine(ye, lpos_t, wts_t, plan, x1, mod, g_post, layer, tile_mod):
    n, d = x1.shape
    tm = FIN_TILE
    row = lambda i, *_: (i, 0)
    return pl.pallas_call(
        functools.partial(_gather_kernel, tm=tm),
        grid_spec=pltpu.PrefetchScalarGridSpec(
            num_scalar_prefetch=5, grid=(n // tm,),
            in_specs=[pl.BlockSpec(memory_space=pl.ANY),
                      pl.BlockSpec((tm, TOP_K), row),
                      pl.BlockSpec((tm, TOP_K), row),
                      pl.BlockSpec((tm, d), row),
                      pl.BlockSpec((None, None, 6, d), lambda i, t, *_: (layer, t[i], 0, 0)),
                      pl.BlockSpec((None, 1, d), lambda i, *_: (layer, 0, 0))],
            out_specs=pl.BlockSpec((tm, d), row),
            scratch_shapes=[pltpu.VMEM((2, SORT_ROWS, d), F32), pltpu.SemaphoreType.DMA((2,))]),
        out_shape=jax.ShapeDtypeStruct((n, d), F32),
        compiler_params=_params(1),
        name="moe_gather",
    )(tile_mod, plan["seg"], plan["off"], plan["pos"], plan["used"], ye, lpos_t, wts_t, x1, mod, g_post)


def _moe(h2, wts, lpos, seg, off, x1, mod, g_post, w1, b1, w2, b2, layer, tile_mod):
    n, d = h2.shape
    blk = MOE_BLOCK
    tm = FIN_TILE
    tiles = n // tm
    n_rows = -(-(n * TOP_K + tiles * N_EXPERTS * (SEG_ALIGN - 1) + N_EXPERTS * (blk - 1)) // blk) * blk
    n_blocks = n_rows // blk
    seg = seg[:, :, 0].astype(I32)
    off = off[:, :, 0].astype(I32)
    rows_e = jnp.sum(seg, axis=0)
    region = (rows_e + blk - 1) // blk * blk
    pend = jnp.cumsum(region)
    pstart = pend - region
    pos = pstart[None, :] + jnp.cumsum(seg, axis=0) - seg
    plan = {"seg": seg.reshape(-1), "off": off.reshape(-1), "pos": pos.reshape(-1).astype(I32),
            "used": jnp.sum(seg, axis=1).astype(I32),
            "tail_pos": (pstart + rows_e).astype(I32), "tail_len": (region - rows_e).astype(I32)}
    blocks = jnp.arange(n_blocks, dtype=I32) * blk
    block_expert = jnp.minimum(jnp.sum(blocks[:, None] >= pend[None, :], axis=1), N_EXPERTS - 1).astype(I32)
    n_used = (pend[-1:] // blk).astype(I32)
    valid_rows = jnp.clip((pstart + rows_e)[block_expert] - blocks, 0, blk).astype(I32)
    ids = jnp.arange(n_blocks, dtype=I32)
    run_start = jnp.logical_and(jnp.concatenate([jnp.ones((1,), bool), block_expert[1:] != block_expert[:-1]]),
                                ids < n_used[0])
    first_after = lax.cummin(jnp.where(run_start, ids, n_blocks)[::-1])[::-1]
    first_after = jnp.concatenate([first_after[1:], jnp.full((1,), n_blocks, I32)])
    next_expert = jnp.where(first_after < n_blocks, block_expert[jnp.minimum(first_after, n_blocks - 1)], -1)
    xe = _scatter_rows(h2, lpos, plan, n_rows)
    ye = _experts(xe, block_expert, next_expert.astype(I32), valid_rows, n_used, w1, b1, w2, b2, layer)
    return _gather_combine(ye, lpos.T, wts.T, plan, x1, mod, g_post, layer, tile_mod)


def _tile_mod_ids(n_ctx_rows, n_lat_rows, lat_seq, tm):
    ctx = np.zeros((n_ctx_rows // tm,), np.int32)
    lat = 1 + (np.arange(n_lat_rows // tm) * tm) // lat_seq
    return jnp.asarray(np.concatenate([ctx, lat.astype(np.int32)]))


def kernel(x_prompt, x_sample, cache_k, cache_v, state_ret_fwd, state_ret_bwd, c, c_ctx, w_mod, b_mod, g_pre_mix, g_post_mix, g_pre_ffn, g_post_ffn, w_in, na_rel_bias, ret_decay_fwd, ret_decay_bwd, w_branch, w_out, w_router, b_router, w_exp_in, b_exp_in, w_exp_out, b_exp_out):
    batch, seq, d = x_prompt.shape
    dec_batch, dec_seq, _ = x_sample.shape
    depth = w_in.shape[0]
    n_ctx = batch * seq
    n_lat = dec_batch * dec_seq
    assert 1 + dec_batch <= MOD_ROWS

    x = jnp.concatenate([x_prompt.reshape(n_ctx, d), x_sample.reshape(n_lat, d)], axis=0)
    cvec = jnp.concatenate([c_ctx[None], c, jnp.zeros((MOD_ROWS - 1 - dec_batch, d), F32)], axis=0)
    mod_all = _modulation(cvec, w_mod, b_mod).reshape(depth, MOD_ROWS, 6, d)
    tmod = {tm: _tile_mod_ids(n_ctx, n_lat, dec_seq, tm) for tm in (ROW_TILE, FIN_TILE)}
    past = cache_k.shape[2]
    ck = cache_k.reshape(dec_batch, depth, past, NA_WIDTH)
    cv = cache_v.reshape(dec_batch, depth, past, NA_WIDTH)
    lg_f = jax.nn.log_sigmoid(ret_decay_fwd.astype(F32)).reshape(-1)
    lg_b = jax.nn.log_sigmoid(ret_decay_bwd.astype(F32)).reshape(-1)
    bias_all = _neighbourhood_bias(na_rel_bias, dec_seq)
    vec = lambda g: g.reshape(depth, 1, d)
    g_pre_mix, g_post_mix, g_pre_ffn, g_post_ffn = vec(g_pre_mix), vec(g_post_mix), vec(g_pre_ffn), vec(g_post_ffn)
    w_branch_b = w_branch.astype(BF16)
    w_out_b = w_out.astype(BF16)
    w_router_t = jnp.swapaxes(w_router, 1, 2)
    b_router_l = jnp.broadcast_to(b_router[:, :, None], (depth, N_EXPERTS, LANES))
    b_exp_in = b_exp_in.reshape(depth, N_EXPERTS, 1, 2 * D_FF)
    b_exp_out = b_exp_out.reshape(depth, N_EXPERTS, 1, d)

    ks, vs, sfs, sbs = [], [], [], []
    for l in range(depth):
        h = _prenorm(x, g_pre_mix, mod_all, l, tmod[ROW_TILE], ROW_TILE)
        z_qkvu = _project(h, w_in, l, 0, QKVU_W, F32)
        z_ret = _project(h, w_in, l, QKVU_W, RET_W, BF16)
        z_gate = _project(h, w_in, l, QKVU_W + RET_W, GATE_W, BF16)
        ks.append(z_qkvu[:n_ctx, NA_WIDTH:2 * NA_WIDTH].reshape(batch, seq, NA_HEADS, NA_HEAD_DIM))
        vs.append(z_qkvu[:n_ctx, 2 * NA_WIDTH:3 * NA_WIDTH].reshape(batch, seq, NA_HEADS, NA_HEAD_DIM))

        a_pair = (_attention_ctx(z_qkvu, batch, seq),
                  _attention_lat(z_qkvu, ck, cv, bias_all, l, n_ctx, dec_batch, dec_seq))
        f_pair = (_fourier(z_qkvu, 0, batch, seq), _fourier(z_qkvu, n_ctx, dec_batch, dec_seq))
        r_ctx, s_f, s_b = _retention(z_ret, lg_f, lg_b, l, 0, batch, seq, rotary=False, state_out=True)
        (r_lat,) = _retention(z_ret, lg_f, lg_b, l, n_ctx, dec_batch, dec_seq, rotary=True,
                              states=(state_ret_fwd, state_ret_bwd))
        sfs.append(s_f)
        sbs.append(s_b)

        x1, h2, wts, lpos, seg, off = _finish(
            (a_pair, f_pair, (r_ctx, r_lat)), z_gate, x, mod_all, g_post_mix, g_pre_ffn,
            w_branch_b, w_out_b, w_router_t, b_router_l, l, tmod[FIN_TILE])
        x = _moe(h2, wts, lpos, seg, off, x1, mod_all, g_post_ffn,
                 w_exp_in, b_exp_in, w_exp_out, b_exp_out, l, tmod[FIN_TILE])

    y_prompt = x[:n_ctx].reshape(batch, seq, d)
    y_sample = x[n_ctx:].reshape(dec_batch, dec_seq, d)
    return (y_prompt, y_sample, jnp.stack(ks, axis=1), jnp.stack(vs, axis=1),
            jnp.stack(sfs, axis=1), jnp.stack(sbs, axis=1))
```

```python
import functools

import numpy as np
import jax
import jax.numpy as jnp
from jax import lax
from jax.experimental import pallas as pl
from jax.experimental.pallas import tpu as pltpu

F32 = jnp.float32
BF16 = jnp.bfloat16
I32 = jnp.int32

D_MODEL = 1024
GRID_W = 64
NA_HEADS = 8
NA_HEAD_DIM = 64
NA_WIDTH = NA_HEADS * NA_HEAD_DIM
WIN_H = 8
WIN_W = 16
KEY_SLAB_ROWS = 12
FOURIER_GROUPS = 4
FOURIER_GROUP_DIM = 128
RET_HEADS = 4
RET_KEY_DIM = 128
ROPE_BASE = 10000.0
BRANCH_WIDTH = 512
N_EXPERTS = 32
TOP_K = 4
D_FF = 1024
SWIGLU_LIMIT = 7.0
SWIGLU_ALPHA = 1.702
EPS = 1e-6
NEG_INF = -1e30

QKVU_W = 4 * NA_WIDTH
RET_W = 4 * BRANCH_WIDTH
GATE_W = 3 * D_MODEL
PROJ_TILE = 1024
PROJ_ROWS = 2048
ATTN_Q_TILE = 512
RET_DECAY_BYTES = 4 * 1024 * 1024

LANES = 128
MOD_ROWS = 16
ROW_TILE = 1024
FIN_TILE = 512
MOE_BLOCK = 512
EXPERT_ROW_STEP = 128
SEG_ALIGN = 8
SORT_CHUNK = 256
SORT_ROWS = -(-(FIN_TILE * TOP_K + N_EXPERTS * (SEG_ALIGN - 1)) // SORT_CHUNK) * SORT_CHUNK
VMEM_LIMIT = 56 * 1024 * 1024


def _params(n_axes, vmem=VMEM_LIMIT):
    return pltpu.CompilerParams(dimension_semantics=("arbitrary",) * n_axes, vmem_limit_bytes=vmem)


def _rms(x):
    return lax.rsqrt(jnp.mean(x * x, axis=-1, keepdims=True) + EPS)


def _mod_kernel(cv_ref, w_ref, b_ref, o_ref):
    cv = cv_ref[...]
    s = (cv * jax.nn.sigmoid(cv)).astype(BF16)
    o_ref[...] = jnp.dot(s, w_ref[...].astype(BF16), preferred_element_type=F32) + b_ref[...]


def _modulation(cv, w_mod, b_mod):
    depth, d, n = w_mod.shape
    tn = 1536
    return pl.pallas_call(
        _mod_kernel,
        grid=(depth, n // tn),
        in_specs=[pl.BlockSpec((MOD_ROWS, d), lambda l, j: (0, 0)),
                  pl.BlockSpec((None, d, tn), lambda l, j: (l, 0, j)),
                  pl.BlockSpec((None, 1, tn), lambda l, j: (l, 0, j))],
        out_specs=pl.BlockSpec((None, MOD_ROWS, tn), lambda l, j: (l, 0, j)),
        out_shape=jax.ShapeDtypeStruct((depth, MOD_ROWS, n), F32),
        compiler_params=_params(2),
        name="modulation",
    )(cv, w_mod, b_mod.reshape(depth, 1, n))


def _token_rows(x, tm):
    if isinstance(x, tuple):
        d = x[0].shape[1]
        ct = x[0].shape[0] // tm
        specs = [pl.BlockSpec((tm, d), lambda i, *_: (jnp.minimum(i, ct - 1), 0)),
                 pl.BlockSpec((tm, d), lambda i, *_: (jnp.maximum(i - ct, 0), 0))]
        return specs, list(x), ct, x[0].shape[0] + x[1].shape[0]
    return [pl.BlockSpec((tm, x.shape[1]), lambda i, *_: (i, 0))], [x], None, x.shape[0]


def _load_rows(refs, ctx_tiles):
    if len(refs) == 1:
        return refs[0][...]
    return jnp.where(pl.program_id(0) < ctx_tiles, refs[0][...], refs[1][...])


def _prenorm_kernel(tmod_ref, *refs, ctx_tiles):
    del tmod_ref
    g_ref, mod_ref, o_ref = refs[-3:]
    x = _load_rows(refs[:-3], ctx_tiles)
    h = x * _rms(x) * g_ref[...]
    o_ref[...] = (h * (1.0 + mod_ref[1:2, :]) + mod_ref[0:1, :]).astype(o_ref.dtype)


def _prenorm(x, g, mod, layer, tile_mod, tm):
    x_specs, x_args, ctx_tiles, n = _token_rows(x, tm)
    d = x_args[0].shape[1]
    return pl.pallas_call(
        functools.partial(_prenorm_kernel, ctx_tiles=ctx_tiles),
        grid_spec=pltpu.PrefetchScalarGridSpec(
            num_scalar_prefetch=1, grid=(n // tm,),
            in_specs=x_specs + [pl.BlockSpec((None, 1, d), lambda i, t: (layer, 0, 0)),
                                pl.BlockSpec((None, None, 6, d), lambda i, t: (layer, t[i], 0, 0))],
            out_specs=pl.BlockSpec((tm, d), lambda i, t: (i, 0))),
        out_shape=jax.ShapeDtypeStruct((n, d), BF16),
        compiler_params=_params(1),
        name="prenorm",
    )(tile_mod, *x_args, g, mod)


def _proj_kernel(h_ref, w_ref, o_ref, wb_ref):
    @pl.when(pl.program_id(1) == 0)
    def _():
        wb_ref[...] = w_ref[...].astype(BF16)

    o_ref[...] = jnp.dot(h_ref[...], wb_ref[...], preferred_element_type=F32).astype(o_ref.dtype)


def _project(h, w, layer, col0, width, out_dtype):
    n, d = h.shape
    tm = PROJ_ROWS
    tn = PROJ_TILE
    cb = col0 // tn
    return pl.pallas_call(
        _proj_kernel,
        grid=(width // tn, n // tm),
        in_specs=[pl.BlockSpec((tm, d), lambda j, i: (i, 0)),
                  pl.BlockSpec((None, d, tn), lambda j, i: (layer, 0, cb + j))],
        out_specs=pl.BlockSpec((tm, tn), lambda j, i: (i, j)),
        out_shape=jax.ShapeDtypeStruct((n, width), out_dtype),
        scratch_shapes=[pltpu.VMEM((d, tn), BF16)],
        compiler_params=_params(2),
        name="in_proj",
    )(h, w)


def _head_pair_masks():
    lane = lax.broadcasted_iota(I32, (1, LANES), 1)
    first = lane < NA_HEAD_DIM
    return first, jnp.logical_not(first)


def _attn_ctx_kernel(q_ref, k_ref, v_ref, o_ref):
    masks = _head_pair_masks()
    scale = NA_HEAD_DIM ** -0.5
    for p in range(NA_WIDTH // LANES):
        cols = slice(LANES * p, LANES * (p + 1))
        q2 = q_ref[:, cols] * scale
        k2 = k_ref[:, cols].astype(BF16)
        v2 = v_ref[:, cols].astype(BF16)
        outs = []
        for m in masks:
            qa = jnp.where(m, q2, 0.0).astype(BF16)
            s = lax.dot_general(qa, k2, (((1,), (1,)), ((), ())), preferred_element_type=F32)
            e = jnp.exp(s - jnp.max(s, axis=-1, keepdims=True))
            den = jnp.sum(e, axis=-1, keepdims=True)
            outs.append(jnp.dot(e.astype(BF16), v2, preferred_element_type=F32) / den)
        o_ref[:, cols] = jnp.where(masks[0], outs[0], outs[1]).astype(o_ref.dtype)


def _attention_ctx(z_qkv, n_seq, seq):
    return pl.pallas_call(
        _attn_ctx_kernel,
        grid=(n_seq,),
        in_specs=[pl.BlockSpec((seq, NA_WIDTH), lambda b: (b, 0)),
                  pl.BlockSpec((seq, NA_WIDTH), lambda b: (b, 1)),
                  pl.BlockSpec((seq, NA_WIDTH), lambda b: (b, 2))],
        out_specs=pl.BlockSpec((seq, NA_WIDTH), lambda b: (b, 0)),
        out_shape=jax.ShapeDtypeStruct((n_seq * seq, NA_WIDTH), BF16),
        compiler_params=_params(1),
        name="attn_ctx",
    )(z_qkv, z_qkv, z_qkv)


def _attn_lat_kernel(q_ref, k_ref, v_ref, kc_ref, vc_ref, bias_ref, o_ref, kb_ref, vb_ref, *, tq):
    masks = _head_pair_masks()
    scale = NA_HEAD_DIM ** -0.5
    seq = q_ref.shape[0]
    slab = bias_ref.shape[3]
    per_half = seq // 2 // tq
    kb_ref[...] = k_ref[...].astype(BF16)
    vb_ref[...] = v_ref[...].astype(BF16)
    kc = kc_ref[...].astype(BF16)
    vc = vc_ref[...].astype(BF16)
    nt = (((1,), (1,)), ((), ()))

    def q_tile(qi, carry):
        rows = pl.ds(pl.multiple_of(qi * tq, tq), tq)
        half = qi // per_half
        half_rows = pl.ds(pl.multiple_of((qi % per_half) * tq, tq), tq)
        keys = pl.ds(pl.multiple_of(half * (seq - slab), seq - slab), slab)
        k2 = kb_ref[keys, :]
        v2 = vb_ref[keys, :]
        q2 = q_ref[rows, :] * scale
        outs = []
        for hh, m in enumerate(masks):
            qa = jnp.where(m, q2, 0.0).astype(BF16)
            s_lat = lax.dot_general(qa, k2, nt, preferred_element_type=F32) + bias_ref[hh, half, half_rows, :]
            s_ctx = lax.dot_general(qa, kc, nt, preferred_element_type=F32)
            mx = jnp.maximum(jnp.max(s_lat, axis=-1, keepdims=True), jnp.max(s_ctx, axis=-1, keepdims=True))
            e_lat = jnp.exp(s_lat - mx)
            e_ctx = jnp.exp(s_ctx - mx)
            den = jnp.sum(e_lat, axis=-1, keepdims=True) + jnp.sum(e_ctx, axis=-1, keepdims=True)
            o = (jnp.dot(e_lat.astype(BF16), v2, preferred_element_type=F32)
                 + jnp.dot(e_ctx.astype(BF16), vc, preferred_element_type=F32))
            outs.append(o / den)
        o_ref[rows, :] = jnp.where(masks[0], outs[0], outs[1]).astype(o_ref.dtype)
        return carry

    lax.fori_loop(0, q_ref.shape[0] // tq, q_tile, 0)


def _attention_lat(z_qkv, cache_k, cache_v, bias, layer, row0, n_seq, seq):
    past = cache_k.shape[2]
    pairs = NA_WIDTH // LANES
    rb = row0 // seq
    kv_cols = NA_WIDTH // LANES
    slab = bias.shape[-1]
    return pl.pallas_call(
        functools.partial(_attn_lat_kernel, tq=min(seq // 2, ATTN_Q_TILE)),
        grid=(pairs, n_seq),
        in_specs=[pl.BlockSpec((seq, LANES), lambda p, b: (rb + b, p)),
                  pl.BlockSpec((seq, LANES), lambda p, b: (rb + b, kv_cols + p)),
                  pl.BlockSpec((seq, LANES), lambda p, b: (rb + b, 2 * kv_cols + p)),
                  pl.BlockSpec((None, None, past, LANES), lambda p, b: (b, layer, 0, p)),
                  pl.BlockSpec((None, None, past, LANES), lambda p, b: (b, layer, 0, p)),
                  pl.BlockSpec((None, 2, 2, seq // 2, slab), lambda p, b: (layer, p, 0, 0, 0))],
        out_specs=pl.BlockSpec((seq, LANES), lambda p, b: (b, p)),
        out_shape=jax.ShapeDtypeStruct((n_seq * seq, NA_WIDTH), BF16),
        scratch_shapes=[pltpu.VMEM((seq, LANES), BF16), pltpu.VMEM((seq, LANES), BF16)],
        compiler_params=_params(2),
        name="attn_lat",
    )(z_qkv, z_qkv, z_qkv, cache_k, cache_v, bias)


def _neighbourhood_bias(rpb, seq):
    rows = seq // GRID_W
    kh = WIN_H
    assert rows >= WIN_H
    lead = rpb.shape[:-2]
    c = np.arange(GRID_W)
    q_cs = np.clip(c - WIN_W // 2, 0, GRID_W - WIN_W)
    col_ok = (c[None, :] >= q_cs[:, None]) & (c[None, :] < q_cs[:, None] + WIN_W)
    pad_c = GRID_W - WIN_W
    pc = jnp.pad(rpb, ((0, 0),) * (rpb.ndim - 1) + ((pad_c, pad_c),), mode="edge")
    w = jnp.stack([pc[..., GRID_W - 1 - cq:2 * GRID_W - 1 - cq] for cq in range(GRID_W)], axis=-3)
    w = jnp.where(jnp.asarray(col_ok)[:, None, :], w, NEG_INF)
    r = np.arange(rows)
    rs = np.clip(r - kh // 2, 0, rows - kh)
    blocks = []
    for rq in range(rows):
        base = _key_slab_start(rq // (rows // 2), rows)
        lo = int(rs[rq]) - rq + WIN_H - 1
        slab = w[..., lo:lo + kh, :].reshape(lead + (GRID_W, kh * GRID_W))
        left = int(rs[rq]) - base
        assert 0 <= left <= KEY_SLAB_ROWS - kh
        pad = ((0, 0),) * (len(lead) + 1) + ((left * GRID_W, (KEY_SLAB_ROWS - kh - left) * GRID_W),)
        blocks.append(jnp.pad(slab, pad, constant_values=NEG_INF))
    return jnp.stack(blocks, axis=-3).reshape(lead + (2, seq // 2, KEY_SLAB_ROWS * GRID_W))


def _key_slab_start(half, rows):
    return 0 if half == 0 else rows - KEY_SLAB_ROWS


def _fourier_kernel(u_ref, ct2_ref, cc_ref, sc_ref, o_ref, pq_ref):
    t = u_ref.shape[0]
    for g in range(FOURIER_GROUPS):
        cols = slice(FOURIER_GROUP_DIM * g, FOURIER_GROUP_DIM * (g + 1))
        ug = u_ref[:, cols].astype(BF16)
        pq_ref[0:t, cols] = jnp.dot(ug, cc_ref[...], preferred_element_type=F32).astype(BF16)
        pq_ref[t:2 * t, cols] = jnp.dot(ug, sc_ref[...], preferred_element_type=F32).astype(BF16)
    o_ref[...] = jnp.dot(ct2_ref[...], pq_ref[...], preferred_element_type=F32).astype(o_ref.dtype)


def _dft_tables(t):
    def cs(n):
        k = np.arange(n, dtype=np.int64)
        ang = 2.0 * np.pi * ((k[:, None] * k[None, :]) % n).astype(np.float64) / n
        return np.cos(ang) / np.sqrt(n), np.sin(ang) / np.sqrt(n)

    ct, st = cs(t)
    cc, sc = cs(FOURIER_GROUP_DIM)
    ct2 = np.concatenate([ct, -st], axis=1).astype(np.float32)
    return (jnp.asarray(ct2).astype(BF16), jnp.asarray(cc.astype(np.float32)).astype(BF16),
            jnp.asarray(sc.astype(np.float32)).astype(BF16))


def _fourier(z_qkvu, row0, n_seq, seq):
    ct2, cc, sc = _dft_tables(seq)
    width = FOURIER_GROUPS * FOURIER_GROUP_DIM
    rb = row0 // seq
    ucol = 3 * NA_WIDTH // width
    return pl.pallas_call(
        _fourier_kernel,
        grid=(n_seq,),
        in_specs=[pl.BlockSpec((seq, width), lambda b: (rb + b, ucol)),
                  pl.BlockSpec((seq, 2 * seq), lambda b: (0, 0)),
                  pl.BlockSpec((FOURIER_GROUP_DIM, FOURIER_GROUP_DIM), lambda b: (0, 0)),
                  pl.BlockSpec((FOURIER_GROUP_DIM, FOURIER_GROUP_DIM), lambda b: (0, 0))],
        out_specs=pl.BlockSpec((seq, width), lambda b: (b, 0)),
        out_shape=jax.ShapeDtypeStruct((n_seq * seq, width), BF16),
        scratch_shapes=[pltpu.VMEM((2 * seq, width), BF16)],
        compiler_params=_params(1),
        name="fourier",
    )(z_qkvu, ct2, cc, sc)


def _rotary_tables(t):
    pos = np.arange(t)
    row = (pos // GRID_W).astype(np.float64)
    col = (pos % GRID_W).astype(np.float64)
    nf = RET_KEY_DIM // 4
    inv_freq = ROPE_BASE ** (-np.arange(nf, dtype=np.float64) / nf)
    ar = row[:, None] * inv_freq[None]
    ac = col[:, None] * inv_freq[None]
    cos = np.concatenate([np.cos(ar), np.cos(ar), np.cos(ac), np.cos(ac)], axis=1)
    sin = np.concatenate([-np.sin(ar), np.sin(ar), -np.sin(ac), np.sin(ac)], axis=1)
    return jnp.asarray(cos.astype(np.float32)), jnp.asarray(sin.astype(np.float32))


def _ret_kernel(lgf_ref, lgb_ref, *refs, t, tq, layer, rotary, state_in, state_out):
    refs = list(refs)
    q_ref, k_ref, v_ref, g_ref = refs[:4]
    refs = refs[4:]
    if rotary:
        cos_ref, sin_ref = refs[:2]
        refs = refs[2:]
    if state_in:
        sf0_ref, sb0_ref = refs[:2]
        refs = refs[2:]
    o_ref = refs[0]
    refs = refs[1:]
    if state_out:
        sf_ref, sb_ref = refs[:2]
        refs = refs[2:]
    dec_ref, kb_ref = refs

    scale = RET_KEY_DIM ** -0.5
    nq = t // tq
    heads_here = dec_ref.shape[0]

    if rotary:
        lane = lax.broadcasted_iota(I32, (1, LANES), 1)
        low = (lane % (RET_KEY_DIM // 2)) < (RET_KEY_DIM // 4)

        def rot(x, rows):
            swapped = jnp.where(low, pltpu.roll(x, LANES - RET_KEY_DIM // 4, 1), pltpu.roll(x, RET_KEY_DIM // 4, 1))
            return x * cos_ref[rows, :] + swapped * sin_ref[rows, :]
    else:
        def rot(x, rows):
            return x

    def one_head(hh):
        h = pl.program_id(0) * heads_here + hh
        cols = slice(LANES * hh, LANES * (hh + 1))
        lgf = lgf_ref[layer * RET_HEADS + h]
        lgb = lgb_ref[layer * RET_HEADS + h]

        @pl.when(pl.program_id(1) == 0)
        def _():
            def fill(ri, c):
                rows = pl.ds(pl.multiple_of(ri * tq, tq), tq)
                i = lax.broadcasted_iota(I32, (tq, t), 0) + ri * tq
                j = lax.broadcasted_iota(I32, (tq, t), 1)
                d = (i - j).astype(F32)
                m = jnp.exp(jnp.abs(d) * jnp.where(d > 0, lgf, lgb))
                dec_ref[hh, rows, :] = jnp.where(d == 0, 2.0, m)
                return c

            lax.fori_loop(0, nq, fill, 0)

        kr = rot(k_ref[:, cols].astype(F32), slice(0, t))
        kb_ref[hh] = kr.astype(BF16)
        vb = v_ref[:, cols]

        if state_out:
            j = lax.broadcasted_iota(I32, (t, 1), 0).astype(F32)
            tn = (((0,), (0,)), ((), ()))
            kf = (kr * (scale * jnp.exp(lgf * (t - 1.0 - j)))).astype(BF16)
            kbw = (kr * (scale * jnp.exp(lgb * j))).astype(BF16)
            sf = lax.dot_general(kf, vb, tn, preferred_element_type=F32)
            sb = lax.dot_general(kbw, vb, tn, preferred_element_type=F32)
            if state_in:
                sf = sf + jnp.exp(lgf * t) * sf0_ref[hh]
                sb = sb + jnp.exp(lgb * t) * sb0_ref[hh]
            sf_ref[hh] = sf
            sb_ref[hh] = sb

        def q_tile(qi, carry):
            r0 = pl.multiple_of(qi * tq, tq)
            rows = pl.ds(r0, tq)
            qr = rot(q_ref[rows, cols].astype(F32), rows)
            s = lax.dot_general((qr * scale).astype(BF16), kb_ref[hh], (((1,), (1,)), ((), ())),
                                preferred_element_type=F32)
            y = jnp.dot((s * dec_ref[hh, rows, :]).astype(BF16), vb, preferred_element_type=F32)
            if state_in:
                pos = (lax.broadcasted_iota(I32, (tq, 1), 0) + r0).astype(F32)
                qf = (qr * jnp.exp(lgf * (pos + 1.0))).astype(BF16)
                qb = (qr * jnp.exp(lgb * (t - pos))).astype(BF16)
                y = (y + jnp.dot(qf, sf0_ref[hh].astype(BF16), preferred_element_type=F32)
                     + jnp.dot(qb, sb0_ref[hh].astype(BF16), preferred_element_type=F32))
            mean = jnp.mean(y, axis=-1, keepdims=True)
            yc = y - mean
            yn = yc * lax.rsqrt(jnp.mean(yc * yc, axis=-1, keepdims=True) + EPS)
            g = g_ref[rows, cols].astype(F32)
            o_ref[rows, cols] = (g * jax.nn.sigmoid(g) * yn).astype(o_ref.dtype)
            return carry

        lax.fori_loop(0, nq, q_tile, 0)

    for hh in range(heads_here):
        one_head(hh)


def _retention(z_ret, lg_f, lg_b, layer, row0, n_seq, seq, *, rotary, states=None, state_out=False):
    rb = row0 // seq
    tq = min(seq, ATTN_Q_TILE)
    state_in = states is not None
    hps = RET_HEADS if seq * seq * RET_HEADS * 4 <= RET_DECAY_BYTES else 1
    cb = BRANCH_WIDTH // (hps * LANES)
    width = hps * LANES
    in_specs = [pl.BlockSpec((seq, width), lambda h, b, *_: (rb + b, 0 * cb + h)),
                pl.BlockSpec((seq, width), lambda h, b, *_: (rb + b, 1 * cb + h)),
                pl.BlockSpec((seq, width), lambda h, b, *_: (rb + b, 2 * cb + h)),
                pl.BlockSpec((seq, width), lambda h, b, *_: (rb + b, 3 * cb + h))]
    args = [z_ret, z_ret, z_ret, z_ret]
    if rotary:
        cos, sin = _rotary_tables(seq)
        in_specs += [pl.BlockSpec((seq, LANES), lambda h, b, *_: (0, 0))] * 2
        args += [cos, sin]
    if state_in:
        st_spec = pl.BlockSpec((None, None, hps, RET_KEY_DIM, RET_KEY_DIM), lambda h, b, *_: (b, layer, h, 0, 0))
        in_specs += [st_spec, st_spec]
        args += list(states)
    out_specs = [pl.BlockSpec((seq, width), lambda h, b, *_: (b, h))]
    out_shape = [jax.ShapeDtypeStruct((n_seq * seq, RET_HEADS * LANES), BF16)]
    if state_out:
        so = pl.BlockSpec((None, hps, RET_KEY_DIM, RET_KEY_DIM), lambda h, b, *_: (b, h, 0, 0))
        out_specs += [so, so]
        out_shape += [jax.ShapeDtypeStruct((n_seq, RET_HEADS, RET_KEY_DIM, RET_KEY_DIM), F32)] * 2
    return pl.pallas_call(
        functools.partial(_ret_kernel, t=seq, tq=tq, layer=layer, rotary=rotary, state_in=state_in,
                          state_out=state_out),
        grid_spec=pltpu.PrefetchScalarGridSpec(
            num_scalar_prefetch=2, grid=(RET_HEADS // hps, n_seq),
            in_specs=in_specs, out_specs=out_specs,
            scratch_shapes=[pltpu.VMEM((hps, seq, seq), F32), pltpu.VMEM((hps, seq, LANES), BF16)]),
        out_shape=out_shape,
        compiler_params=_params(2),
        name="retention",
    )(lg_f, lg_b, *args)


def _split_dot_nt(w, x):
    nt = (((1,), (1,)), ((), ()))
    w_hi = w.astype(BF16)
    w_lo = (w - w_hi.astype(F32)).astype(BF16)
    x_hi = x.astype(BF16)
    x_lo = (x - x_hi.astype(F32)).astype(BF16)
    return (lax.dot_general(w_hi, x_hi, nt, preferred_element_type=F32)
            + lax.dot_general(w_hi, x_lo, nt, preferred_element_type=F32)
            + lax.dot_general(w_lo, x_hi, nt, preferred_element_type=F32))


def _finish_kernel(tmod_ref, *refs, ctx_tiles, n_x):
    del tmod_ref
    branch_refs, refs = refs[:6], refs[6:]
    zg_ref, refs = refs[0], refs[1:]
    x_refs, refs = refs[:n_x], refs[n_x:]
    (mod_ref, gpost_ref, gpre_ref, wb_ref, wo_ref, wrt_ref, br_ref, tri_ref, ltri_ref,
     x1_ref, h2_ref, wts_ref, lpos_ref, seg_ref, off_ref) = refs
    d = D_MODEL

    def branch(j):
        return _load_rows(branch_refs[2 * j:2 * j + 2], ctx_tiles)

    def gate(j):
        return jax.nn.sigmoid(zg_ref[:, d * j:d * (j + 1)].astype(F32))

    merged = (gate(0) * jnp.dot(branch(0), wb_ref[0], preferred_element_type=F32)
              + gate(1) * jnp.dot(branch(1), wb_ref[1], preferred_element_type=F32)
              + gate(2) * jnp.dot(branch(2), wb_ref[2], preferred_element_type=F32))
    y = jnp.dot(merged.astype(BF16), wo_ref[...], preferred_element_type=F32)
    x1 = _load_rows(x_refs, ctx_tiles) + mod_ref[2:3, :] * (y * _rms(y) * gpost_ref[...])
    x1_ref[...] = x1
    h2 = x1 * _rms(x1) * gpre_ref[...] * (1.0 + mod_ref[4:5, :]) + mod_ref[3:4, :]
    h2_ref[...] = h2

    logits = _split_dot_nt(wrt_ref[...], h2) + br_ref[:, 0:1]
    tm = logits.shape[1]
    eidx = lax.broadcasted_iota(I32, (N_EXPERTS, tm), 0)
    cur = logits
    vals, hots = [], []
    for k in range(TOP_K):
        m = jnp.max(cur, axis=0, keepdims=True)
        sel = jnp.min(jnp.where(cur == m, eidx, N_EXPERTS), axis=0, keepdims=True)
        hot = eidx == sel
        vals.append(m)
        hots.append(hot)
        cur = jnp.where(hot, -jnp.inf, cur)
    exps = [jnp.exp(v - vals[0]) for v in vals]
    den = exps[0] + exps[1] + exps[2] + exps[3]
    for k in range(TOP_K):
        wts_ref[k:k + 1, :] = exps[k] / den

    member = jnp.logical_or(jnp.logical_or(hots[0], hots[1]), jnp.logical_or(hots[2], hots[3]))
    member_f = member.astype(F32)
    before = jnp.dot(member_f.astype(BF16), tri_ref[...], preferred_element_type=F32)
    units = jnp.ceil(jnp.sum(member_f, axis=1, keepdims=True) * (1.0 / SEG_ALIGN))
    units = jnp.broadcast_to(units, seg_ref.shape)
    off = jnp.dot(ltri_ref[...], units.astype(BF16), preferred_element_type=F32) * SEG_ALIGN
    seg_ref[...] = units * SEG_ALIGN
    off_ref[...] = off
    place = before + off[:, 0:1]
    for k in range(TOP_K):
        lpos_ref[k:k + 1, :] = jnp.sum(jnp.where(hots[k], place, 0.0), axis=0, keepdims=True).astype(I32)


def _finish(branches, z_gate, x, mod, g_post, g_pre_ffn, w_branch, w_out, w_router_t, b_router, layer, tile_mod):
    tm = FIN_TILE
    x_specs, x_args, _, n = _token_rows(x, tm)
    d = x_args[0].shape[1]
    tri = jnp.asarray(np.triu(np.ones((tm, tm), np.float32), k=1)).astype(BF16)
    ltri = jnp.asarray(np.tril(np.ones((N_EXPERTS, N_EXPERTS), np.float32), k=-1)).astype(BF16)
    row = lambda i, t: (i, 0)
    const2 = lambda i, t: (0, 0)
    lay3 = lambda i, t: (layer, 0, 0)
    col = lambda i, t: (0, i)
    branch_specs, branch_args = [], []
    for pair in branches:
        specs, args, ctx_tiles, _ = _token_rows(pair, tm)
        branch_specs += specs
        branch_args += args
    outs = pl.pallas_call(
        functools.partial(_finish_kernel, ctx_tiles=ctx_tiles, n_x=len(x_args)),
        grid_spec=pltpu.PrefetchScalarGridSpec(
            num_scalar_prefetch=1, grid=(n // tm,),
            in_specs=branch_specs + [pl.BlockSpec((tm, GATE_W), row)] + x_specs + [
                      pl.BlockSpec((None, None, 6, d), lambda i, t: (layer, t[i], 0, 0)),
                      pl.BlockSpec((None, 1, d), lay3),
                      pl.BlockSpec((None, 1, d), lay3),
                      pl.BlockSpec((None, 3, BRANCH_WIDTH, d), lambda i, t: (layer, 0, 0, 0)),
                      pl.BlockSpec((None, d, d), lay3),
                      pl.BlockSpec((None, N_EXPERTS, d), lay3),
                      pl.BlockSpec((None, N_EXPERTS, LANES), lay3),
                      pl.BlockSpec((tm, tm), const2),
                      pl.BlockSpec((N_EXPERTS, N_EXPERTS), const2)],
            out_specs=[pl.BlockSpec((tm, d), row),
                       pl.BlockSpec((tm, d), row),
                       pl.BlockSpec((TOP_K, tm), col),
                       pl.BlockSpec((TOP_K, tm), col),
                       pl.BlockSpec((None, N_EXPERTS, LANES), lambda i, t: (i, 0, 0)),
                       pl.BlockSpec((None, N_EXPERTS, LANES), lambda i, t: (i, 0, 0))]),
        out_shape=[jax.ShapeDtypeStruct((n, d), F32),
                   jax.ShapeDtypeStruct((n, d), F32),
                   jax.ShapeDtypeStruct((TOP_K, n), F32),
                   jax.ShapeDtypeStruct((TOP_K, n), I32),
                   jax.ShapeDtypeStruct((n // tm, N_EXPERTS, LANES), F32),
                   jax.ShapeDtypeStruct((n // tm, N_EXPERTS, LANES), F32)],
        compiler_params=_params(1),
        name="merge_router",
    )(tile_mod, *branch_args, z_gate, *x_args, mod, g_post, g_pre_ffn, w_branch, w_out, w_router_t, b_router,
      tri, ltri)
    return outs


def _segment_chunks(length, src_ref, src0, dst_ref, dst0, sem, max_chunk, fixed_src=False):
    out = []
    chunk = max_chunk
    while chunk >= SEG_ALIGN:
        done = jnp.bitwise_and(length, ~(2 * chunk - 1))
        present = jnp.bitwise_and(length, chunk) != 0
        s = 0 if fixed_src else pl.multiple_of(src0 + done, SEG_ALIGN)
        dd = pl.multiple_of(dst0 + done, SEG_ALIGN)
        out.append((present, pltpu.make_async_copy(src_ref.at[pl.ds(s, chunk)], dst_ref.at[pl.ds(dd, chunk)], sem)))
        chunk //= 2
    return out


def _for_each_chunk(n_segments, chunks_of, action):
    def body(e, c):
        for present, cp in chunks_of(e):
            pl.when(present)(functools.partial(action, cp))
        return c

    lax.fori_loop(0, n_segments, body, 0)


def _start(cp):
    cp.start()


def _wait(cp):
    cp.wait()


def _wait_rows(total, src_ref, dst_ref, sem):
    chunk = pl.next_power_of_2(SORT_ROWS) // 2
    while chunk >= SEG_ALIGN:
        @pl.when(jnp.bitwise_and(total, chunk) != 0)
        def _(chunk=chunk):
            pltpu.make_async_copy(src_ref.at[pl.ds(0, chunk)], dst_ref.at[pl.ds(0, chunk)], sem).wait()
        chunk //= 2


def _scatter_kernel(seg_ref, off_ref, pos_ref, used_ref, tpos_ref, tlen_ref, h_ref, lpos_ref, xe_hbm,
                    buf_ref, zero_ref, sem, *, tm):
    t = pl.program_id(0)
    slot = t % 2
    hb = h_ref[...].astype(BF16)
    lp = [lpos_ref[k:k + 1, :] for k in range(TOP_K)]
    rows = buf_ref.shape[1]
    for c0 in range(0, rows, SORT_CHUNK):
        r = lax.broadcasted_iota(I32, (SORT_CHUNK, tm), 0) + c0
        hit = jnp.logical_or(jnp.logical_or(r == lp[0], r == lp[1]), jnp.logical_or(r == lp[2], r == lp[3]))
        onehot = jnp.where(hit, 1.0, 0.0).astype(BF16)
        buf_ref[slot, c0:c0 + SORT_CHUNK, :] = jnp.dot(onehot, hb, preferred_element_type=F32)

    def segments_of(tile):
        def segment(e):
            j = tile * N_EXPERTS + e
            return _segment_chunks(seg_ref[j], buf_ref.at[tile % 2], off_ref[j], xe_hbm, pos_ref[j],
                                   sem.at[tile % 2], tm)
        return segment

    @pl.when(t > 0)
    def _():
        _wait_rows(used_ref[t - 1], buf_ref.at[1 - slot], xe_hbm, sem.at[1 - slot])

    _for_each_chunk(N_EXPERTS, segments_of(t), _start)

    @pl.when(t == pl.num_programs(0) - 1)
    def _():
        _wait_rows(used_ref[t], buf_ref.at[slot], xe_hbm, sem.at[slot])
        zero_ref[...] = jnp.zeros_like(zero_ref)

        def tail(e):
            return _segment_chunks(tlen_ref[e], zero_ref, 0, xe_hbm, tpos_ref[e], sem.at[0], zero_ref.shape[0],
                                   fixed_src=True)

        _for_each_chunk(N_EXPERTS, tail, _start)
        _for_each_chunk(N_EXPERTS, tail, _wait)


def _scatter_rows(h2, lpos, plan, n_rows):
    n, d = h2.shape
    tm = FIN_TILE
    return pl.pallas_call(
        functools.partial(_scatter_kernel, tm=tm),
        grid_spec=pltpu.PrefetchScalarGridSpec(
            num_scalar_prefetch=6, grid=(n // tm,),
            in_specs=[pl.BlockSpec((tm, d), lambda i, *_: (i, 0)),
                      pl.BlockSpec((TOP_K, tm), lambda i, *_: (0, i))],
            out_specs=pl.BlockSpec(memory_space=pl.ANY),
            scratch_shapes=[pltpu.VMEM((2, SORT_ROWS, d), F32), pltpu.VMEM((MOE_BLOCK // 2, d), F32),
                            pltpu.SemaphoreType.DMA((2,))]),
        out_shape=jax.ShapeDtypeStruct((n_rows, d), F32),
        compiler_params=_params(1),
        name="moe_scatter",
    )(plan["seg"], plan["off"], plan["pos"], plan["used"], plan["tail_pos"], plan["tail_len"], h2, lpos)


def _expert_kernel(be_ref, nxt_ref, valid_ref, nu_ref, xb_ref, w1_hbm, b1_ref, w2_hbm, b2_ref, yb_ref,
                   w1s_ref, w2s_ref, w1b_ref, w2b_ref, sem, *, layer):
    i = pl.program_id(0)
    used = i < nu_ref[0]
    fresh = jnp.logical_or(i == 0, be_ref[i] != be_ref[jnp.maximum(i - 1, 0)])

    def fetch(e):
        return (pltpu.make_async_copy(w1_hbm.at[layer, e], w1s_ref, sem.at[0]),
                pltpu.make_async_copy(w2_hbm.at[layer, e], w2s_ref, sem.at[1]))

    @pl.when(i == 0)
    def _():
        for cp in fetch(be_ref[0]):
            cp.start()

    @pl.when(jnp.logical_and(used, fresh))
    def _():
        for cp in fetch(be_ref[i]):
            cp.wait()
        w1b_ref[...] = w1s_ref[...].astype(BF16)
        w2b_ref[...] = w2s_ref[...].astype(BF16)

        @pl.when(nxt_ref[i] >= 0)
        def _():
            for cp in fetch(nxt_ref[i]):
                cp.start()

    def ffn(m):
        z = jnp.dot(xb_ref[0:m, :].astype(BF16), w1b_ref[...], preferred_element_type=F32) + b1_ref[...]
        glu = jnp.minimum(z[:, :D_FF], SWIGLU_LIMIT)
        lin = jnp.clip(z[:, D_FF:], -SWIGLU_LIMIT, SWIGLU_LIMIT)
        act = glu * jax.nn.sigmoid(SWIGLU_ALPHA * glu) * (lin + 1.0)
        yb_ref[0:m, :] = jnp.dot(act.astype(BF16), w2b_ref[...], preferred_element_type=F32) + b2_ref[...]
        if m < yb_ref.shape[0]:
            yb_ref[m:, :] = jnp.zeros((yb_ref.shape[0] - m, yb_ref.shape[1]), F32)

    valid = valid_ref[i]
    for m in range(EXPERT_ROW_STEP, yb_ref.shape[0] + 1, EXPERT_ROW_STEP):
        pl.when(jnp.logical_and(valid > m - EXPERT_ROW_STEP, valid <= m))(functools.partial(ffn, m))

    @pl.when(valid == 0)
    def _():
        yb_ref[...] = jnp.zeros_like(yb_ref)


def _experts(xb, block_expert, next_expert, valid_rows, n_used, w1, b1, w2, b2, layer):
    n_rows, d = xb.shape
    tm = MOE_BLOCK
    return pl.pallas_call(
        functools.partial(_expert_kernel, layer=layer),
        grid_spec=pltpu.PrefetchScalarGridSpec(
            num_scalar_prefetch=4, grid=(n_rows // tm,),
            in_specs=[pl.BlockSpec((tm, d), lambda i, be, nx, vr, nu: (jnp.minimum(i, nu[0] - 1), 0)),
                      pl.BlockSpec(memory_space=pl.ANY),
                      pl.BlockSpec((None, None, 1, 2 * D_FF), lambda i, be, nx, vr, nu: (layer, be[i], 0, 0)),
                      pl.BlockSpec(memory_space=pl.ANY),
                      pl.BlockSpec((None, None, 1, d), lambda i, be, nx, vr, nu: (layer, be[i], 0, 0))],
            out_specs=pl.BlockSpec((tm, d), lambda i, be, nx, vr, nu: (i, 0)),
            scratch_shapes=[pltpu.VMEM((d, 2 * D_FF), F32), pltpu.VMEM((D_FF, d), F32),
                            pltpu.VMEM((d, 2 * D_FF), BF16), pltpu.VMEM((D_FF, d), BF16),
                            pltpu.SemaphoreType.DMA((2,))]),
        out_shape=jax.ShapeDtypeStruct((n_rows, d), F32),
        compiler_params=_params(1),
        name="moe_experts",
    )(block_expert, next_expert, valid_rows, n_used, xb, w1, b1, w2, b2)


def _gather_kernel(tmod_ref, seg_ref, off_ref, pos_ref, used_ref, ye_hbm, lpos_ref, wts_ref, x1_ref, mod_ref, g_ref,
                   o_ref, buf_ref, sem, *, tm, tile0):
    del tmod_ref
    step = pl.program_id(0)
    t = step + tile0
    slot = step % 2

    def segments_of(tile):
        def segment(e):
            j = tile * N_EXPERTS + e
            half = (tile - tile0) % 2
            return _segment_chunks(seg_ref[j], ye_hbm, pos_ref[j], buf_ref.at[half], off_ref[j], sem.at[half], tm)
        return segment

    @pl.when(step == 0)
    def _():
        buf_ref[...] = jnp.zeros_like(buf_ref)
        _for_each_chunk(N_EXPERTS, segments_of(t), _start)

    @pl.when(step + 1 < pl.num_programs(0))
    def _():
        _for_each_chunk(N_EXPERTS, segments_of(t + 1), _start)

    _wait_rows(used_ref[t], ye_hbm, buf_ref.at[slot], sem.at[slot])

    lp = [lpos_ref[:, k:k + 1] for k in range(TOP_K)]
    wt = [wts_ref[:, k:k + 1] for k in range(TOP_K)]
    y = jnp.zeros(o_ref.shape, F32)
    for c0 in range(0, buf_ref.shape[1], SORT_CHUNK):
        r = lax.broadcasted_iota(I32, (tm, SORT_CHUNK), 1) + c0
        wm = jnp.where(r == lp[0], wt[0], 0.0)
        for k in range(1, TOP_K):
            wm = jnp.where(r == lp[k], wt[k], wm)
        y = y + jnp.dot(wm.astype(BF16), buf_ref[slot, c0:c0 + SORT_CHUNK, :].astype(BF16),
                        preferred_element_type=F32)
    o_ref[...] = x1_ref[...] + mod_ref[5:6, :] * (y * _rms(y) * g_ref[...])


def _gather_combine(ye, lpos_t, wts_t, plan, x1, mod, g_post, layer, tile_mod, row0=0, rows=None):
    n, d = x1.shape
    tm = FIN_TILE
    rows = n if rows is None else rows
    tile0 = row0 // tm
    row = lambda i, *_: (tile0 + i, 0)
    return pl.pallas_call(
        functools.partial(_gather_kernel, tm=tm, tile0=tile0),
        grid_spec=pltpu.PrefetchScalarGridSpec(
            num_scalar_prefetch=5, grid=(rows // tm,),
            in_specs=[pl.BlockSpec(memory_space=pl.ANY),
                      pl.BlockSpec((tm, TOP_K), row),
                      pl.BlockSpec((tm, TOP_K), row),
                      pl.BlockSpec((tm, d), row),
                      pl.BlockSpec((None, None, 6, d), lambda i, t, *_: (layer, t[tile0 + i], 0, 0)),
                      pl.BlockSpec((None, 1, d), lambda i, *_: (layer, 0, 0))],
            out_specs=pl.BlockSpec((tm, d), lambda i, *_: (i, 0)),
            scratch_shapes=[pltpu.VMEM((2, SORT_ROWS, d), F32), pltpu.SemaphoreType.DMA((2,))]),
        out_shape=jax.ShapeDtypeStruct((rows, d), F32),
        compiler_params=_params(1),
        name="moe_gather",
    )(tile_mod, plan["seg"], plan["off"], plan["pos"], plan["used"], ye, lpos_t, wts_t, x1, mod, g_post)


def _moe(h2, wts, lpos, seg, off, x1, mod, g_post, w1, b1, w2, b2, layer, tile_mod, split_rows=None):
    n, d = h2.shape
    blk = MOE_BLOCK
    tm = FIN_TILE
    tiles = n // tm
    n_rows = -(-(n * TOP_K + tiles * N_EXPERTS * (SEG_ALIGN - 1) + N_EXPERTS * (blk - 1)) // blk) * blk
    n_blocks = n_rows // blk
    seg = seg[:, :, 0].astype(I32)
    off = off[:, :, 0].astype(I32)
    rows_e = jnp.sum(seg, axis=0)
    region = (rows_e + blk - 1) // blk * blk
    pend = jnp.cumsum(region)
    pstart = pend - region
    pos = pstart[None, :] + jnp.cumsum(seg, axis=0) - seg
    plan = {"seg": seg.reshape(-1), "off": off.reshape(-1), "pos": pos.reshape(-1).astype(I32),
            "used": jnp.sum(seg, axis=1).astype(I32),
            "tail_pos": (pstart + rows_e).astype(I32), "tail_len": (region - rows_e).astype(I32)}
    blocks = jnp.arange(n_blocks, dtype=I32) * blk
    block_expert = jnp.minimum(jnp.sum(blocks[:, None] >= pend[None, :], axis=1), N_EXPERTS - 1).astype(I32)
    n_used = (pend[-1:] // blk).astype(I32)
    valid_rows = jnp.clip((pstart + rows_e)[block_expert] - blocks, 0, blk).astype(I32)
    ids = jnp.arange(n_blocks, dtype=I32)
    run_start = jnp.logical_and(jnp.concatenate([jnp.ones((1,), bool), block_expert[1:] != block_expert[:-1]]),
                                ids < n_used[0])
    first_after = lax.cummin(jnp.where(run_start, ids, n_blocks)[::-1])[::-1]
    first_after = jnp.concatenate([first_after[1:], jnp.full((1,), n_blocks, I32)])
    next_expert = jnp.where(first_after < n_blocks, block_expert[jnp.minimum(first_after, n_blocks - 1)], -1)
    xe = _scatter_rows(h2, lpos, plan, n_rows)
    ye = _experts(xe, block_expert, next_expert.astype(I32), valid_rows, n_used, w1, b1, w2, b2, layer)
    combine = functools.partial(_gather_combine, ye, lpos.T, wts.T, plan, x1, mod, g_post, layer, tile_mod)
    if split_rows is None:
        return combine()
    return combine(row0=0, rows=split_rows), combine(row0=split_rows, rows=n - split_rows)


def _tile_mod_ids(n_ctx_rows, n_lat_rows, lat_seq, tm):
    ctx = np.zeros((n_ctx_rows // tm,), np.int32)
    lat = 1 + (np.arange(n_lat_rows // tm) * tm) // lat_seq
    return jnp.asarray(np.concatenate([ctx, lat.astype(np.int32)]))


def kernel(x_prompt, x_sample, cache_k, cache_v, state_ret_fwd, state_ret_bwd, c, c_ctx, w_mod, b_mod, g_pre_mix, g_post_mix, g_pre_ffn, g_post_ffn, w_in, na_rel_bias, ret_decay_fwd, ret_decay_bwd, w_branch, w_out, w_router, b_router, w_exp_in, b_exp_in, w_exp_out, b_exp_out):
    batch, seq, d = x_prompt.shape
    dec_batch, dec_seq, _ = x_sample.shape
    depth = w_in.shape[0]
    n_ctx = batch * seq
    n_lat = dec_batch * dec_seq
    assert 1 + dec_batch <= MOD_ROWS

    x = (x_prompt.reshape(n_ctx, d), x_sample.reshape(n_lat, d))
    cvec =jnp.concatenate([c_ctx[None], c, jnp.zeros((MOD_ROWS - 1 - dec_batch, d), F32)], axis=0)
    mod_all = _modulation(cvec, w_mod, b_mod).reshape(depth, MOD_ROWS, 6, d)
    tmod = {tm: _tile_mod_ids(n_ctx, n_lat, dec_seq, tm) for tm in (ROW_TILE, FIN_TILE)}
    past = cache_k.shape[2]
    ck = cache_k.reshape(dec_batch, depth, past, NA_WIDTH)
    cv = cache_v.reshape(dec_batch, depth, past, NA_WIDTH)
    lg_f = jax.nn.log_sigmoid(ret_decay_fwd.astype(F32)).reshape(-1)
    lg_b = jax.nn.log_sigmoid(ret_decay_bwd.astype(F32)).reshape(-1)
    bias_all = _neighbourhood_bias(na_rel_bias, dec_seq)
    vec = lambda g: g.reshape(depth, 1, d)
    g_pre_mix, g_post_mix, g_pre_ffn, g_post_ffn = vec(g_pre_mix), vec(g_post_mix), vec(g_pre_ffn), vec(g_post_ffn)
    w_branch_b = w_branch.astype(BF16)
    w_out_b = w_out.astype(BF16)
    w_router_t = jnp.swapaxes(w_router, 1, 2)
    b_router_l = jnp.broadcast_to(b_router[:, :, None], (depth, N_EXPERTS, LANES))
    b_exp_in = b_exp_in.reshape(depth, N_EXPERTS, 1, 2 * D_FF)
    b_exp_out = b_exp_out.reshape(depth, N_EXPERTS, 1, d)

    ks, vs, sfs, sbs = [], [], [], []
    for l in range(depth):
        h = _prenorm(x, g_pre_mix, mod_all, l, tmod[ROW_TILE], ROW_TILE)
        z_qkvu = _project(h, w_in, l, 0, QKVU_W, F32)
        z_ret = _project(h, w_in, l, QKVU_W, RET_W, BF16)
        z_gate = _project(h, w_in, l, QKVU_W + RET_W, GATE_W, BF16)
        ks.append(z_qkvu[:n_ctx, NA_WIDTH:2 * NA_WIDTH].reshape(batch, seq, NA_HEADS, NA_HEAD_DIM))
        vs.append(z_qkvu[:n_ctx, 2 * NA_WIDTH:3 * NA_WIDTH].reshape(batch, seq, NA_HEADS, NA_HEAD_DIM))

        a_pair = (_attention_ctx(z_qkvu, batch, seq),
                  _attention_lat(z_qkvu, ck, cv, bias_all, l, n_ctx, dec_batch, dec_seq))
        f_pair = (_fourier(z_qkvu, 0, batch, seq), _fourier(z_qkvu, n_ctx, dec_batch, dec_seq))
        r_ctx, s_f, s_b = _retention(z_ret, lg_f, lg_b, l, 0, batch, seq, rotary=False, state_out=True)
        (r_lat,) = _retention(z_ret, lg_f, lg_b, l, n_ctx, dec_batch, dec_seq, rotary=True,
                              states=(state_ret_fwd, state_ret_bwd))
        sfs.append(s_f)
        sbs.append(s_b)

        x1, h2, wts, lpos, seg, off = _finish(
            (a_pair, f_pair, (r_ctx, r_lat)), z_gate, x, mod_all, g_post_mix, g_pre_ffn,
            w_branch_b, w_out_b, w_router_t, b_router_l, l, tmod[FIN_TILE])
        x = _moe(h2, wts, lpos, seg, off, x1, mod_all, g_post_ffn,
                 w_exp_in, b_exp_in, w_exp_out, b_exp_out, l, tmod[FIN_TILE],
                 split_rows=n_ctx if l == depth - 1 else None)

    y_prompt = x[0].reshape(batch, seq, d)
    y_sample = x[1].reshape(dec_batch, dec_seq, d)
    return (y_prompt, y_sample, jnp.stack(ks, axis=1), jnp.stack(vs, axis=1),
            jnp.stack(sfs, axis=1), jnp.stack(sbs, axis=1))
```

```python
import functools

import numpy as np
import jax
import jax.numpy as jnp
from jax import lax
from jax.experimental import pallas as pl
from jax.experimental.pallas import tpu as pltpu

F32 = jnp.float32
BF16 = jnp.bfloat16
I32 = jnp.int32

D_MODEL = 1024
GRID_W = 64
NA_HEADS = 8
NA_HEAD_DIM = 64
NA_WIDTH = NA_HEADS * NA_HEAD_DIM
WIN_H = 8
WIN_W = 16
KEY_SLAB_ROWS = 12
FOURIER_GROUPS = 4
FOURIER_GROUP_DIM = 128
RET_HEADS = 4
RET_KEY_DIM = 128
ROPE_BASE = 10000.0
BRANCH_WIDTH = 512
N_EXPERTS = 32
TOP_K = 4
D_FF = 1024
SWIGLU_LIMIT = 7.0
SWIGLU_ALPHA = 1.702
EPS = 1e-6
NEG_INF = -1e30

QKVU_W = 4 * NA_WIDTH
RET_W = 4 * BRANCH_WIDTH
GATE_W = 3 * D_MODEL
PROJ_TILE = 1024
PROJ_ROWS = 2048
ATTN_Q_TILE = 512
RET_DECAY_BYTES = 4 * 1024 * 1024

LANES = 128
MOD_ROWS = 16
ROW_TILE = 1024
FIN_TILE = 512
MOE_BLOCK = 512
EXPERT_ROW_STEP = 128
SEG_ALIGN = 8
SORT_CHUNK = 256
SORT_ROWS = -(-(FIN_TILE * TOP_K + N_EXPERTS * (SEG_ALIGN - 1)) // SORT_CHUNK) * SORT_CHUNK
VMEM_LIMIT = 56 * 1024 * 1024


def _params(n_axes, vmem=VMEM_LIMIT):
    return pltpu.CompilerParams(dimension_semantics=("arbitrary",) * n_axes, vmem_limit_bytes=vmem)


def _rms(x):
    return lax.rsqrt(jnp.mean(x * x, axis=-1, keepdims=True) + EPS)


def _mod_kernel(cv_ref, w_ref, b_ref, o_ref):
    cv = cv_ref[...]
    s = (cv * jax.nn.sigmoid(cv)).astype(BF16)
    o_ref[...] = jnp.dot(s, w_ref[...].astype(BF16), preferred_element_type=F32) + b_ref[...]


def _modulation(cv, w_mod, b_mod):
    depth, d, n = w_mod.shape
    tn = 1536
    return pl.pallas_call(
        _mod_kernel,
        grid=(depth, n // tn),
        in_specs=[pl.BlockSpec((MOD_ROWS, d), lambda l, j: (0, 0)),
                  pl.BlockSpec((None, d, tn), lambda l, j: (l, 0, j)),
                  pl.BlockSpec((None, 1, tn), lambda l, j: (l, 0, j))],
        out_specs=pl.BlockSpec((None, MOD_ROWS, tn), lambda l, j: (l, 0, j)),
        out_shape=jax.ShapeDtypeStruct((depth, MOD_ROWS, n), F32),
        compiler_params=_params(2),
        name="modulation",
    )(cv, w_mod, b_mod.reshape(depth, 1, n))


def _token_rows(x, tm):
    if isinstance(x, tuple):
        d = x[0].shape[1]
        ct = x[0].shape[0] // tm
        specs = [pl.BlockSpec((tm, d), lambda i, *_: (jnp.minimum(i, ct - 1), 0)),
                 pl.BlockSpec((tm, d), lambda i, *_: (jnp.maximum(i - ct, 0), 0))]
        return specs, list(x), ct, x[0].shape[0] + x[1].shape[0]
    return [pl.BlockSpec((tm, x.shape[1]), lambda i, *_: (i, 0))], [x], None, x.shape[0]


def _load_rows(refs, ctx_tiles):
    if len(refs) == 1:
        return refs[0][...]
    return jnp.where(pl.program_id(0) < ctx_tiles, refs[0][...], refs[1][...])


def _prenorm_kernel(tmod_ref, *refs, ctx_tiles):
    del tmod_ref
    g_ref, mod_ref, o_ref = refs[-3:]
    x = _load_rows(refs[:-3], ctx_tiles)
    h = x * _rms(x) * g_ref[...]
    o_ref[...] = (h * (1.0 + mod_ref[1:2, :]) + mod_ref[0:1, :]).astype(o_ref.dtype)


def _prenorm(x, g, mod, layer, tile_mod, tm):
    x_specs, x_args, ctx_tiles, n = _token_rows(x, tm)
    d = x_args[0].shape[1]
    return pl.pallas_call(
        functools.partial(_prenorm_kernel, ctx_tiles=ctx_tiles),
        grid_spec=pltpu.PrefetchScalarGridSpec(
            num_scalar_prefetch=1, grid=(n // tm,),
            in_specs=x_specs + [pl.BlockSpec((None, 1, d), lambda i, t: (layer, 0, 0)),
                                pl.BlockSpec((None, None, 6, d), lambda i, t: (layer, t[i], 0, 0))],
            out_specs=pl.BlockSpec((tm, d), lambda i, t: (i, 0))),
        out_shape=jax.ShapeDtypeStruct((n, d), BF16),
        compiler_params=_params(1),
        name="prenorm",
    )(tile_mod, *x_args, g, mod)


def _proj_kernel(h_ref, w_ref, o_ref, wb_ref):
    @pl.when(pl.program_id(1) == 0)
    def _():
        wb_ref[...] = w_ref[...].astype(BF16)

    o_ref[...] = jnp.dot(h_ref[...], wb_ref[...], preferred_element_type=F32).astype(o_ref.dtype)


def _project(h, w, layer, col0, width, out_dtype):
    n, d = h.shape
    tm = PROJ_ROWS
    tn = PROJ_TILE
    cb = col0 // tn
    return pl.pallas_call(
        _proj_kernel,
        grid=(width // tn, n // tm),
        in_specs=[pl.BlockSpec((tm, d), lambda j, i: (i, 0)),
                  pl.BlockSpec((None, d, tn), lambda j, i: (layer, 0, cb + j))],
        out_specs=pl.BlockSpec((tm, tn), lambda j, i: (i, j)),
        out_shape=jax.ShapeDtypeStruct((n, width), out_dtype),
        scratch_shapes=[pltpu.VMEM((d, tn), BF16)],
        compiler_params=_params(2),
        name="in_proj",
    )(h, w)


def _head_pair_masks():
    lane = lax.broadcasted_iota(I32, (1, LANES), 1)
    first = lane < NA_HEAD_DIM
    return first, jnp.logical_not(first)


def _attn_ctx_kernel(q_ref, k_ref, v_ref, o_ref):
    masks = _head_pair_masks()
    scale = NA_HEAD_DIM ** -0.5
    for p in range(NA_WIDTH // LANES):
        cols = slice(LANES * p, LANES * (p + 1))
        q2 = q_ref[:, cols] * scale
        k2 = k_ref[:, cols].astype(BF16)
        v2 = v_ref[:, cols].astype(BF16)
        outs = []
        for m in masks:
            qa = jnp.where(m, q2, 0.0).astype(BF16)
            s = lax.dot_general(qa, k2, (((1,), (1,)), ((), ())), preferred_element_type=F32)
            e = jnp.exp(s - jnp.max(s, axis=-1, keepdims=True))
            den = jnp.sum(e, axis=-1, keepdims=True)
            outs.append(jnp.dot(e.astype(BF16), v2, preferred_element_type=F32) / den)
        o_ref[:, cols] = jnp.where(masks[0], outs[0], outs[1]).astype(o_ref.dtype)


def _attention_ctx(z_qkv, n_seq, seq):
    return pl.pallas_call(
        _attn_ctx_kernel,
        grid=(n_seq,),
        in_specs=[pl.BlockSpec((seq, NA_WIDTH), lambda b: (b, 0)),
                  pl.BlockSpec((seq, NA_WIDTH), lambda b: (b, 1)),
                  pl.BlockSpec((seq, NA_WIDTH), lambda b: (b, 2))],
        out_specs=pl.BlockSpec((seq, NA_WIDTH), lambda b: (b, 0)),
        out_shape=jax.ShapeDtypeStruct((n_seq * seq, NA_WIDTH), BF16),
        compiler_params=_params(1),
        name="attn_ctx",
    )(z_qkv, z_qkv, z_qkv)


def _attn_lat_kernel(q_ref, k_ref, v_ref, kc_ref, vc_ref, bias_ref, o_ref, kb_ref, vb_ref, *, tq):
    masks = _head_pair_masks()
    scale = NA_HEAD_DIM ** -0.5
    seq = q_ref.shape[0]
    slab = bias_ref.shape[3]
    per_half = seq // 2 // tq
    kb_ref[...] = k_ref[...].astype(BF16)
    vb_ref[...] = v_ref[...].astype(BF16)
    kc = kc_ref[...].astype(BF16)
    vc = vc_ref[...].astype(BF16)
    nt = (((1,), (1,)), ((), ()))

    def q_tile(qi, carry):
        rows = pl.ds(pl.multiple_of(qi * tq, tq), tq)
        half = qi // per_half
        half_rows = pl.ds(pl.multiple_of((qi % per_half) * tq, tq), tq)
        keys = pl.ds(pl.multiple_of(half * (seq - slab), seq - slab), slab)
        k2 = kb_ref[keys, :]
        v2 = vb_ref[keys, :]
        q2 = q_ref[rows, :] * scale
        outs = []
        for hh, m in enumerate(masks):
            qa = jnp.where(m, q2, 0.0).astype(BF16)
            s_lat = lax.dot_general(qa, k2, nt, preferred_element_type=F32) + bias_ref[hh, half, half_rows, :]
            s_ctx = lax.dot_general(qa, kc, nt, preferred_element_type=F32)
            mx = jnp.maximum(jnp.max(s_lat, axis=-1, keepdims=True), jnp.max(s_ctx, axis=-1, keepdims=True))
            e_lat = jnp.exp(s_lat - mx)
            e_ctx = jnp.exp(s_ctx - mx)
            den = jnp.sum(e_lat, axis=-1, keepdims=True) + jnp.sum(e_ctx, axis=-1, keepdims=True)
            o = (jnp.dot(e_lat.astype(BF16), v2, preferred_element_type=F32)
                 + jnp.dot(e_ctx.astype(BF16), vc, preferred_element_type=F32))
            outs.append(o / den)
        o_ref[rows, :] = jnp.where(masks[0], outs[0], outs[1]).astype(o_ref.dtype)
        return carry

    lax.fori_loop(0, q_ref.shape[0] // tq, q_tile, 0)


def _attention_lat(z_qkv, cache_k, cache_v, bias, layer, row0, n_seq, seq):
    past = cache_k.shape[2]
    pairs = NA_WIDTH // LANES
    rb = row0 // seq
    kv_cols = NA_WIDTH // LANES
    slab = bias.shape[-1]
    return pl.pallas_call(
        functools.partial(_attn_lat_kernel, tq=min(seq // 2, ATTN_Q_TILE)),
        grid=(pairs, n_seq),
        in_specs=[pl.BlockSpec((seq, LANES), lambda p, b: (rb + b, p)),
                  pl.BlockSpec((seq, LANES), lambda p, b: (rb + b, kv_cols + p)),
                  pl.BlockSpec((seq, LANES), lambda p, b: (rb + b, 2 * kv_cols + p)),
                  pl.BlockSpec((None, None, past, LANES), lambda p, b: (b, layer, 0, p)),
                  pl.BlockSpec((None, None, past, LANES), lambda p, b: (b, layer, 0, p)),
                  pl.BlockSpec((None, 2, 2, seq // 2, slab), lambda p, b: (layer, p, 0, 0, 0))],
        out_specs=pl.BlockSpec((seq, LANES), lambda p, b: (b, p)),
        out_shape=jax.ShapeDtypeStruct((n_seq * seq, NA_WIDTH), BF16),
        scratch_shapes=[pltpu.VMEM((seq, LANES), BF16), pltpu.VMEM((seq, LANES), BF16)],
        compiler_params=_params(2),
        name="attn_lat",
    )(z_qkv, z_qkv, z_qkv, cache_k, cache_v, bias)


def _neighbourhood_bias(rpb, seq):
    rows = seq // GRID_W
    kh = WIN_H
    assert rows >= WIN_H
    lead = rpb.shape[:-2]
    c = np.arange(GRID_W)
    q_cs = np.clip(c - WIN_W // 2, 0, GRID_W - WIN_W)
    col_ok = (c[None, :] >= q_cs[:, None]) & (c[None, :] < q_cs[:, None] + WIN_W)
    r = np.arange(rows)
    rs = np.clip(r - kh // 2, 0, rows - kh)
    base = np.where(r < rows // 2, 0, rows - KEY_SLAB_ROWS)
    s = np.arange(KEY_SLAB_ROWS)
    rk = base[:, None] + s[None, :]
    row_ok = (rk >= rs[:, None]) & (rk < rs[:, None] + kh)
    assert (rs >= base).all() and (rs + kh <= base + KEY_SLAB_ROWS).all()
    pick_c = (c[None, None, :] - c[None, :, None] + WIN_W - 1
              == np.arange(2 * WIN_W - 1)[:, None, None]).astype(np.float32)
    pick_r = ((rk - r[:, None] + WIN_H - 1)[None] == np.arange(2 * WIN_H - 1)[:, None, None]).astype(np.float32)
    w = jnp.einsum("...ij,jqk->...iqk", rpb, pick_c, precision=lax.Precision.HIGHEST)
    b = jnp.einsum("...iqk,irs->...rqsk", w, pick_r, precision=lax.Precision.HIGHEST)
    mask = row_ok[:, None, :, None] & col_ok[None, :, None, :]
    b = jnp.where(jnp.asarray(mask), b, NEG_INF)
    return b.reshape(lead + (2, seq // 2, KEY_SLAB_ROWS * GRID_W))


def _fourier_kernel(u_ref, ct2_ref, cc_ref, sc_ref, o_ref, pq_ref):
    t = u_ref.shape[0]
    for g in range(FOURIER_GROUPS):
        cols = slice(FOURIER_GROUP_DIM * g, FOURIER_GROUP_DIM * (g + 1))
        ug = u_ref[:, cols].astype(BF16)
        pq_ref[0:t, cols] = jnp.dot(ug, cc_ref[...], preferred_element_type=F32).astype(BF16)
        pq_ref[t:2 * t, cols] = jnp.dot(ug, sc_ref[...], preferred_element_type=F32).astype(BF16)
    o_ref[...] = jnp.dot(ct2_ref[...], pq_ref[...], preferred_element_type=F32).astype(o_ref.dtype)


def _dft_tables(t):
    def cs(n):
        k = np.arange(n, dtype=np.int64)
        ang = 2.0 * np.pi * ((k[:, None] * k[None, :]) % n).astype(np.float64) / n
        return np.cos(ang) / np.sqrt(n), np.sin(ang) / np.sqrt(n)

    ct, st = cs(t)
    cc, sc = cs(FOURIER_GROUP_DIM)
    ct2 = np.concatenate([ct, -st], axis=1).astype(np.float32)
    return (jnp.asarray(ct2).astype(BF16), jnp.asarray(cc.astype(np.float32)).astype(BF16),
            jnp.asarray(sc.astype(np.float32)).astype(BF16))


def _fourier(z_qkvu, row0, n_seq, seq):
    ct2, cc, sc = _dft_tables(seq)
    width = FOURIER_GROUPS * FOURIER_GROUP_DIM
    rb = row0 // seq
    ucol = 3 * NA_WIDTH // width
    return pl.pallas_call(
        _fourier_kernel,
        grid=(n_seq,),
        in_specs=[pl.BlockSpec((seq, width), lambda b: (rb + b, ucol)),
                  pl.BlockSpec((seq, 2 * seq), lambda b: (0, 0)),
                  pl.BlockSpec((FOURIER_GROUP_DIM, FOURIER_GROUP_DIM), lambda b: (0, 0)),
                  pl.BlockSpec((FOURIER_GROUP_DIM, FOURIER_GROUP_DIM), lambda b: (0, 0))],
        out_specs=pl.BlockSpec((seq, width), lambda b: (b, 0)),
        out_shape=jax.ShapeDtypeStruct((n_seq * seq, width), BF16),
        scratch_shapes=[pltpu.VMEM((2 * seq, width), BF16)],
        compiler_params=_params(1),
        name="fourier",
    )(z_qkvu, ct2, cc, sc)


def _rotary_tables(t):
    pos = np.arange(t)
    row = (pos // GRID_W).astype(np.float64)
    col = (pos % GRID_W).astype(np.float64)
    nf = RET_KEY_DIM // 4
    inv_freq = ROPE_BASE ** (-np.arange(nf, dtype=np.float64) / nf)
    ar = row[:, None] * inv_freq[None]
    ac = col[:, None] * inv_freq[None]
    cos = np.concatenate([np.cos(ar), np.cos(ar), np.cos(ac), np.cos(ac)], axis=1)
    sin = np.concatenate([-np.sin(ar), np.sin(ar), -np.sin(ac), np.sin(ac)], axis=1)
    return jnp.asarray(cos.astype(np.float32)), jnp.asarray(sin.astype(np.float32))


def _ret_kernel(lgf_ref, lgb_ref, *refs, t, tq, layer, rotary, state_in, state_out):
    refs = list(refs)
    q_ref, k_ref, v_ref, g_ref = refs[:4]
    refs = refs[4:]
    if rotary:
        cos_ref, sin_ref = refs[:2]
        refs = refs[2:]
    if state_in:
        sf0_ref, sb0_ref = refs[:2]
        refs = refs[2:]
    o_ref = refs[0]
    refs = refs[1:]
    if state_out:
        sf_ref, sb_ref = refs[:2]
        refs = refs[2:]
    dec_ref, kb_ref = refs

    scale = RET_KEY_DIM ** -0.5
    nq = t // tq
    heads_here = dec_ref.shape[0]

    if rotary:
        lane = lax.broadcasted_iota(I32, (1, LANES), 1)
        low = (lane % (RET_KEY_DIM // 2)) < (RET_KEY_DIM // 4)

        def rot(x, rows):
            swapped = jnp.where(low, pltpu.roll(x, LANES - RET_KEY_DIM // 4, 1), pltpu.roll(x, RET_KEY_DIM // 4, 1))
            return x * cos_ref[rows, :] + swapped * sin_ref[rows, :]
    else:
        def rot(x, rows):
            return x

    def one_head(hh):
        h = pl.program_id(0) * heads_here + hh
        cols = slice(LANES * hh, LANES * (hh + 1))
        lgf = lgf_ref[layer * RET_HEADS + h]
        lgb = lgb_ref[layer * RET_HEADS + h]

        @pl.when(pl.program_id(1) == 0)
        def _():
            def fill(ri, c):
                rows = pl.ds(pl.multiple_of(ri * tq, tq), tq)
                i = lax.broadcasted_iota(I32, (tq, t), 0) + ri * tq
                j = lax.broadcasted_iota(I32, (tq, t), 1)
                d = (i - j).astype(F32)
                m = jnp.exp(jnp.abs(d) * jnp.where(d > 0, lgf, lgb))
                dec_ref[hh, rows, :] = jnp.where(d == 0, 2.0, m)
                return c

            lax.fori_loop(0, nq, fill, 0)

        kr = rot(k_ref[:, cols].astype(F32), slice(0, t))
        kb_ref[hh] = kr.astype(BF16)
        vb = v_ref[:, cols]

        if state_out:
            j = lax.broadcasted_iota(I32, (t, 1), 0).astype(F32)
            tn = (((0,), (0,)), ((), ()))
            kf = (kr * (scale * jnp.exp(lgf * (t - 1.0 - j)))).astype(BF16)
            kbw = (kr * (scale * jnp.exp(lgb * j))).astype(BF16)
            sf = lax.dot_general(kf, vb, tn, preferred_element_type=F32)
            sb = lax.dot_general(kbw, vb, tn, preferred_element_type=F32)
            if state_in:
                sf = sf + jnp.exp(lgf * t) * sf0_ref[hh]
                sb = sb + jnp.exp(lgb * t) * sb0_ref[hh]
            sf_ref[hh] = sf
            sb_ref[hh] = sb

        def q_tile(qi, carry):
            r0 = pl.multiple_of(qi * tq, tq)
            rows = pl.ds(r0, tq)
            qr = rot(q_ref[rows, cols].astype(F32), rows)
            s = lax.dot_general((qr * scale).astype(BF16), kb_ref[hh], (((1,), (1,)), ((), ())),
                                preferred_element_type=F32)
            y = jnp.dot((s * dec_ref[hh, rows, :]).astype(BF16), vb, preferred_element_type=F32)
            if state_in:
                pos = (lax.broadcasted_iota(I32, (tq, 1), 0) + r0).astype(F32)
                qf = (qr * jnp.exp(lgf * (pos + 1.0))).astype(BF16)
                qb = (qr * jnp.exp(lgb * (t - pos))).astype(BF16)
                y = (y + jnp.dot(qf, sf0_ref[hh].astype(BF16), preferred_element_type=F32)
                     + jnp.dot(qb, sb0_ref[hh].astype(BF16), preferred_element_type=F32))
            mean = jnp.mean(y, axis=-1, keepdims=True)
            yc = y - mean
            yn = yc * lax.rsqrt(jnp.mean(yc * yc, axis=-1, keepdims=True) + EPS)
            g = g_ref[rows, cols].astype(F32)
            o_ref[rows, cols] = (g * jax.nn.sigmoid(g) * yn).astype(o_ref.dtype)
            return carry

        lax.fori_loop(0, nq, q_tile, 0)

    for hh in range(heads_here):
        one_head(hh)


def _retention(z_ret, lg_f, lg_b, layer, row0, n_seq, seq, *, rotary, states=None, state_out=False):
    rb = row0 // seq
    tq = min(seq, ATTN_Q_TILE)
    state_in = states is not None
    hps = RET_HEADS if seq * seq * RET_HEADS * 4 <= RET_DECAY_BYTES else 1
    cb = BRANCH_WIDTH // (hps * LANES)
    width = hps * LANES
    in_specs = [pl.BlockSpec((seq, width), lambda h, b, *_: (rb + b, 0 * cb + h)),
                pl.BlockSpec((seq, width), lambda h, b, *_: (rb + b, 1 * cb + h)),
                pl.BlockSpec((seq, width), lambda h, b, *_: (rb + b, 2 * cb + h)),
                pl.BlockSpec((seq, width), lambda h, b, *_: (rb + b, 3 * cb + h))]
    args = [z_ret, z_ret, z_ret, z_ret]
    if rotary:
        cos, sin = _rotary_tables(seq)
        in_specs += [pl.BlockSpec((seq, LANES), lambda h, b, *_: (0, 0))] * 2
        args += [cos, sin]
    if state_in:
        st_spec = pl.BlockSpec((None, None, hps, RET_KEY_DIM, RET_KEY_DIM), lambda h, b, *_: (b, layer, h, 0, 0))
        in_specs += [st_spec, st_spec]
        args += list(states)
    out_specs = [pl.BlockSpec((seq, width), lambda h, b, *_: (b, h))]
    out_shape = [jax.ShapeDtypeStruct((n_seq * seq, RET_HEADS * LANES), BF16)]
    if state_out:
        so = pl.BlockSpec((None, hps, RET_KEY_DIM, RET_KEY_DIM), lambda h, b, *_: (b, h, 0, 0))
        out_specs += [so, so]
        out_shape += [jax.ShapeDtypeStruct((n_seq, RET_HEADS, RET_KEY_DIM, RET_KEY_DIM), F32)] * 2
    return pl.pallas_call(
        functools.partial(_ret_kernel, t=seq, tq=tq, layer=layer, rotary=rotary, state_in=state_in,
                          state_out=state_out),
        grid_spec=pltpu.PrefetchScalarGridSpec(
            num_scalar_prefetch=2, grid=(RET_HEADS // hps, n_seq),
            in_specs=in_specs, out_specs=out_specs,
            scratch_shapes=[pltpu.VMEM((hps, seq, seq), F32), pltpu.VMEM((hps, seq, LANES), BF16)]),
        out_shape=out_shape,
        compiler_params=_params(2),
        name="retention",
    )(lg_f, lg_b, *args)


def _split_dot_nt(w, x):
    nt = (((1,), (1,)), ((), ()))
    w_hi = w.astype(BF16)
    w_lo = (w - w_hi.astype(F32)).astype(BF16)
    x_hi = x.astype(BF16)
    x_lo = (x - x_hi.astype(F32)).astype(BF16)
    return (lax.dot_general(w_hi, x_hi, nt, preferred_element_type=F32)
            + lax.dot_general(w_hi, x_lo, nt, preferred_element_type=F32)
            + lax.dot_general(w_lo, x_hi, nt, preferred_element_type=F32))


def _finish_kernel(tmod_ref, *refs, ctx_tiles, n_x):
    del tmod_ref
    branch_refs, refs = refs[:6], refs[6:]
    zg_ref, refs = refs[0], refs[1:]
    x_refs, refs = refs[:n_x], refs[n_x:]
    (mod_ref, gpost_ref, gpre_ref, wb_ref, wo_ref, wrt_ref, br_ref, tri_ref, ltri_ref,
     x1_ref, h2_ref, wts_ref, lpos_ref, seg_ref, off_ref) = refs
    d = D_MODEL

    def branch(j):
        return _load_rows(branch_refs[2 * j:2 * j + 2], ctx_tiles)

    def gate(j):
        return jax.nn.sigmoid(zg_ref[:, d * j:d * (j + 1)].astype(F32))

    merged = (gate(0) * jnp.dot(branch(0), wb_ref[0], preferred_element_type=F32)
              + gate(1) * jnp.dot(branch(1), wb_ref[1], preferred_element_type=F32)
              + gate(2) * jnp.dot(branch(2), wb_ref[2], preferred_element_type=F32))
    y = jnp.dot(merged.astype(BF16), wo_ref[...], preferred_element_type=F32)
    x1 = _load_rows(x_refs, ctx_tiles) + mod_ref[2:3, :] * (y * _rms(y) * gpost_ref[...])
    x1_ref[...] = x1
    h2 = x1 * _rms(x1) * gpre_ref[...] * (1.0 + mod_ref[4:5, :]) + mod_ref[3:4, :]
    h2_ref[...] = h2

    logits = _split_dot_nt(wrt_ref[...], h2) + br_ref[:, 0:1]
    tm = logits.shape[1]
    eidx = lax.broadcasted_iota(I32, (N_EXPERTS, tm), 0)
    cur = logits
    vals, hots = [], []
    for k in range(TOP_K):
        m = jnp.max(cur, axis=0, keepdims=True)
        sel = jnp.min(jnp.where(cur == m, eidx, N_EXPERTS), axis=0, keepdims=True)
        hot = eidx == sel
        vals.append(m)
        hots.append(hot)
        cur = jnp.where(hot, -jnp.inf, cur)
    exps = [jnp.exp(v - vals[0]) for v in vals]
    den = exps[0] + exps[1] + exps[2] + exps[3]
    for k in range(TOP_K):
        wts_ref[k:k + 1, :] = exps[k] / den

    member = jnp.logical_or(jnp.logical_or(hots[0], hots[1]), jnp.logical_or(hots[2], hots[3]))
    member_f = member.astype(F32)
    before = jnp.dot(member_f.astype(BF16), tri_ref[...], preferred_element_type=F32)
    units = jnp.ceil(jnp.sum(member_f, axis=1, keepdims=True) * (1.0 / SEG_ALIGN))
    units = jnp.broadcast_to(units, seg_ref.shape)
    off = jnp.dot(ltri_ref[...], units.astype(BF16), preferred_element_type=F32) * SEG_ALIGN
    seg_ref[...] = units * SEG_ALIGN
    off_ref[...] = off
    place = before + off[:, 0:1]
    for k in range(TOP_K):
        lpos_ref[k:k + 1, :] = jnp.sum(jnp.where(hots[k], place, 0.0), axis=0, keepdims=True).astype(I32)


def _finish(branches, z_gate, x, mod, g_post, g_pre_ffn, w_branch, w_out, w_router_t, b_router, layer, tile_mod):
    tm = FIN_TILE
    x_specs, x_args, _, n = _token_rows(x, tm)
    d = x_args[0].shape[1]
    tri = jnp.asarray(np.triu(np.ones((tm, tm), np.float32), k=1)).astype(BF16)
    ltri = jnp.asarray(np.tril(np.ones((N_EXPERTS, N_EXPERTS), np.float32), k=-1)).astype(BF16)
    row = lambda i, t: (i, 0)
    const2 = lambda i, t: (0, 0)
    lay3 = lambda i, t: (layer, 0, 0)
    col = lambda i, t: (0, i)
    branch_specs, branch_args = [], []
    for pair in branches:
        specs, args, ctx_tiles, _ = _token_rows(pair, tm)
        branch_specs += specs
        branch_args += args
    outs = pl.pallas_call(
        functools.partial(_finish_kernel, ctx_tiles=ctx_tiles, n_x=len(x_args)),
        grid_spec=pltpu.PrefetchScalarGridSpec(
            num_scalar_prefetch=1, grid=(n // tm,),
            in_specs=branch_specs + [pl.BlockSpec((tm, GATE_W), row)] + x_specs + [
                      pl.BlockSpec((None, None, 6, d), lambda i, t: (layer, t[i], 0, 0)),
                      pl.BlockSpec((None, 1, d), lay3),
                      pl.BlockSpec((None, 1, d), lay3),
                      pl.BlockSpec((None, 3, BRANCH_WIDTH, d), lambda i, t: (layer, 0, 0, 0)),
                      pl.BlockSpec((None, d, d), lay3),
                      pl.BlockSpec((None, N_EXPERTS, d), lay3),
                      pl.BlockSpec((None, N_EXPERTS, LANES), lay3),
                      pl.BlockSpec((tm, tm), const2),
                      pl.BlockSpec((N_EXPERTS, N_EXPERTS), const2)],
            out_specs=[pl.BlockSpec((tm, d), row),
                       pl.BlockSpec((tm, d), row),
                       pl.BlockSpec((TOP_K, tm), col),
                       pl.BlockSpec((TOP_K, tm), col),
                       pl.BlockSpec((None, N_EXPERTS, LANES), lambda i, t: (i, 0, 0)),
                       pl.BlockSpec((None, N_EXPERTS, LANES), lambda i, t: (i, 0, 0))]),
        out_shape=[jax.ShapeDtypeStruct((n, d), F32),
                   jax.ShapeDtypeStruct((n, d), F32),
                   jax.ShapeDtypeStruct((TOP_K, n), F32),
                   jax.ShapeDtypeStruct((TOP_K, n), I32),
                   jax.ShapeDtypeStruct((n // tm, N_EXPERTS, LANES), F32),
                   jax.ShapeDtypeStruct((n // tm, N_EXPERTS, LANES), F32)],
        compiler_params=_params(1),
        name="merge_router",
    )(tile_mod, *branch_args, z_gate, *x_args, mod, g_post, g_pre_ffn, w_branch, w_out, w_router_t, b_router,
      tri, ltri)
    return outs


def _segment_chunks(length, src_ref, src0, dst_ref, dst0, sem, max_chunk, fixed_src=False):
    out = []
    chunk = max_chunk
    while chunk >= SEG_ALIGN:
        done = jnp.bitwise_and(length, ~(2 * chunk - 1))
        present = jnp.bitwise_and(length, chunk) != 0
        s = 0 if fixed_src else pl.multiple_of(src0 + done, SEG_ALIGN)
        dd = pl.multiple_of(dst0 + done, SEG_ALIGN)
        out.append((present, pltpu.make_async_copy(src_ref.at[pl.ds(s, chunk)], dst_ref.at[pl.ds(dd, chunk)], sem)))
        chunk //= 2
    return out


def _for_each_chunk(n_segments, chunks_of, action):
    def body(e, c):
        for present, cp in chunks_of(e):
            pl.when(present)(functools.partial(action, cp))
        return c

    lax.fori_loop(0, n_segments, body, 0)


def _start(cp):
    cp.start()


def _wait(cp):
    cp.wait()


def _wait_rows(total, src_ref, dst_ref, sem):
    chunk = pl.next_power_of_2(SORT_ROWS) // 2
    while chunk >= SEG_ALIGN:
        @pl.when(jnp.bitwise_and(total, chunk) != 0)
        def _(chunk=chunk):
            pltpu.make_async_copy(src_ref.at[pl.ds(0, chunk)], dst_ref.at[pl.ds(0, chunk)], sem).wait()
        chunk //= 2


def _scatter_kernel(seg_ref, off_ref, pos_ref, used_ref, tpos_ref, tlen_ref, h_ref, lpos_ref, xe_hbm,
                    buf_ref, zero_ref, sem, *, tm):
    t = pl.program_id(0)
    slot = t % 2
    hb = h_ref[...].astype(BF16)
    lp = [lpos_ref[k:k + 1, :] for k in range(TOP_K)]
    rows = buf_ref.shape[1]
    for c0 in range(0, rows, SORT_CHUNK):
        r = lax.broadcasted_iota(I32, (SORT_CHUNK, tm), 0) + c0
        hit = jnp.logical_or(jnp.logical_or(r == lp[0], r == lp[1]), jnp.logical_or(r == lp[2], r == lp[3]))
        onehot = jnp.where(hit, 1.0, 0.0).astype(BF16)
        buf_ref[slot, c0:c0 + SORT_CHUNK, :] = jnp.dot(onehot, hb, preferred_element_type=F32)

    def segments_of(tile):
        def segment(e):
            j = tile * N_EXPERTS + e
            return _segment_chunks(seg_ref[j], buf_ref.at[tile % 2], off_ref[j], xe_hbm, pos_ref[j],
                                   sem.at[tile % 2], tm)
        return segment

    @pl.when(t > 0)
    def _():
        _wait_rows(used_ref[t - 1], buf_ref.at[1 - slot], xe_hbm, sem.at[1 - slot])

    _for_each_chunk(N_EXPERTS, segments_of(t), _start)

    @pl.when(t == pl.num_programs(0) - 1)
    def _():
        _wait_rows(used_ref[t], buf_ref.at[slot], xe_hbm, sem.at[slot])
        zero_ref[...] = jnp.zeros_like(zero_ref)

        def tail(e):
            return _segment_chunks(tlen_ref[e], zero_ref, 0, xe_hbm, tpos_ref[e], sem.at[0], zero_ref.shape[0],
                                   fixed_src=True)

        _for_each_chunk(N_EXPERTS, tail, _start)
        _for_each_chunk(N_EXPERTS, tail, _wait)


def _scatter_rows(h2, lpos, plan, n_rows):
    n, d = h2.shape
    tm = FIN_TILE
    return pl.pallas_call(
        functools.partial(_scatter_kernel, tm=tm),
        grid_spec=pltpu.PrefetchScalarGridSpec(
            num_scalar_prefetch=6, grid=(n // tm,),
            in_specs=[pl.BlockSpec((tm, d), lambda i, *_: (i, 0)),
                      pl.BlockSpec((TOP_K, tm), lambda i, *_: (0, i))],
            out_specs=pl.BlockSpec(memory_space=pl.ANY),
            scratch_shapes=[pltpu.VMEM((2, SORT_ROWS, d), F32), pltpu.VMEM((MOE_BLOCK // 2, d), F32),
                            pltpu.SemaphoreType.DMA((2,))]),
        out_shape=jax.ShapeDtypeStruct((n_rows, d), F32),
        compiler_params=_params(1),
        name="moe_scatter",
    )(plan["seg"], plan["off"], plan["pos"], plan["used"], plan["tail_pos"], plan["tail_len"], h2, lpos)


def _expert_kernel(be_ref, nxt_ref, valid_ref, nu_ref, xb_ref, w1_hbm, b1_ref, w2_hbm, b2_ref, yb_ref,
                   w1s_ref, w2s_ref, w1b_ref, w2b_ref, sem, *, layer):
    i = pl.program_id(0)
    used = i < nu_ref[0]
    fresh = jnp.logical_or(i == 0, be_ref[i] != be_ref[jnp.maximum(i - 1, 0)])

    def fetch(e):
        return (pltpu.make_async_copy(w1_hbm.at[layer, e], w1s_ref, sem.at[0]),
                pltpu.make_async_copy(w2_hbm.at[layer, e], w2s_ref, sem.at[1]))

    @pl.when(i == 0)
    def _():
        for cp in fetch(be_ref[0]):
            cp.start()

    @pl.when(jnp.logical_and(used, fresh))
    def _():
        for cp in fetch(be_ref[i]):
            cp.wait()
        w1b_ref[...] = w1s_ref[...].astype(BF16)
        w2b_ref[...] = w2s_ref[...].astype(BF16)

        @pl.when(nxt_ref[i] >= 0)
        def _():
            for cp in fetch(nxt_ref[i]):
                cp.start()

    def ffn(m):
        z = jnp.dot(xb_ref[0:m, :].astype(BF16), w1b_ref[...], preferred_element_type=F32) + b1_ref[...]
        glu = jnp.minimum(z[:, :D_FF], SWIGLU_LIMIT)
        lin = jnp.clip(z[:, D_FF:], -SWIGLU_LIMIT, SWIGLU_LIMIT)
        act = glu * jax.nn.sigmoid(SWIGLU_ALPHA * glu) * (lin + 1.0)
        yb_ref[0:m, :] = jnp.dot(act.astype(BF16), w2b_ref[...], preferred_element_type=F32) + b2_ref[...]
        if m < yb_ref.shape[0]:
            yb_ref[m:, :] = jnp.zeros((yb_ref.shape[0] - m, yb_ref.shape[1]), F32)

    valid = valid_ref[i]
    for m in range(EXPERT_ROW_STEP, yb_ref.shape[0] + 1, EXPERT_ROW_STEP):
        pl.when(jnp.logical_and(valid > m - EXPERT_ROW_STEP, valid <= m))(functools.partial(ffn, m))

    @pl.when(valid == 0)
    def _():
        yb_ref[...] = jnp.zeros_like(yb_ref)


def _experts(xb, block_expert, next_expert, valid_rows, n_used, w1, b1, w2, b2, layer):
    n_rows, d = xb.shape
    tm = MOE_BLOCK
    return pl.pallas_call(
        functools.partial(_expert_kernel, layer=layer),
        grid_spec=pltpu.PrefetchScalarGridSpec(
            num_scalar_prefetch=4, grid=(n_rows // tm,),
            in_specs=[pl.BlockSpec((tm, d), lambda i, be, nx, vr, nu: (jnp.minimum(i, nu[0] - 1), 0)),
                      pl.BlockSpec(memory_space=pl.ANY),
                      pl.BlockSpec((None, None, 1, 2 * D_FF), lambda i, be, nx, vr, nu: (layer, be[i], 0, 0)),
                      pl.BlockSpec(memory_space=pl.ANY),
                      pl.BlockSpec((None, None, 1, d), lambda i, be, nx, vr, nu: (layer, be[i], 0, 0))],
            out_specs=pl.BlockSpec((tm, d), lambda i, be, nx, vr, nu: (i, 0)),
            scratch_shapes=[pltpu.VMEM((d, 2 * D_FF), F32), pltpu.VMEM((D_FF, d), F32),
                            pltpu.VMEM((d, 2 * D_FF), BF16), pltpu.VMEM((D_FF, d), BF16),
                            pltpu.SemaphoreType.DMA((2,))]),
        out_shape=jax.ShapeDtypeStruct((n_rows, d), F32),
        compiler_params=_params(1),
        name="moe_experts",
    )(block_expert, next_expert, valid_rows, n_used, xb, w1, b1, w2, b2)


def _gather_kernel(tmod_ref, seg_ref, off_ref, pos_ref, used_ref, ye_hbm, lpos_ref, wts_ref, x1_ref, mod_ref, g_ref,
                   *refs, tm, tile0, prenorm_next):
    del tmod_ref
    if prenorm_next:
        modn_ref, gn_ref, o_ref, hn_ref, buf_ref, sem = refs
    else:
        o_ref, buf_ref, sem = refs
    step = pl.program_id(0)
    t = step + tile0
    slot = step % 2

    def segments_of(tile):
        def segment(e):
            j = tile * N_EXPERTS + e
            half = (tile - tile0) % 2
            return _segment_chunks(seg_ref[j], ye_hbm, pos_ref[j], buf_ref.at[half], off_ref[j], sem.at[half], tm)
        return segment

    @pl.when(step == 0)
    def _():
        buf_ref[...] = jnp.zeros_like(buf_ref)
        _for_each_chunk(N_EXPERTS, segments_of(t), _start)

    @pl.when(step + 1 < pl.num_programs(0))
    def _():
        _for_each_chunk(N_EXPERTS, segments_of(t + 1), _start)

    _wait_rows(used_ref[t], ye_hbm, buf_ref.at[slot], sem.at[slot])

    lp = [lpos_ref[:, k:k + 1] for k in range(TOP_K)]
    wt = [wts_ref[:, k:k + 1] for k in range(TOP_K)]
    y = jnp.zeros(o_ref.shape, F32)
    for c0 in range(0, buf_ref.shape[1], SORT_CHUNK):
        r = lax.broadcasted_iota(I32, (tm, SORT_CHUNK), 1) + c0
        wm = jnp.where(r == lp[0], wt[0], 0.0)
        for k in range(1, TOP_K):
            wm = jnp.where(r == lp[k], wt[k], wm)
        y = y + jnp.dot(wm.astype(BF16), buf_ref[slot, c0:c0 + SORT_CHUNK, :].astype(BF16),
                        preferred_element_type=F32)
    x2 = x1_ref[...] + mod_ref[5:6, :] * (y * _rms(y) * g_ref[...])
    o_ref[...] = x2
    if prenorm_next:
        hn = x2 * _rms(x2) * gn_ref[...]
        hn_ref[...] = (hn * (1.0 + modn_ref[1:2, :]) + modn_ref[0:1, :]).astype(hn_ref.dtype)


def _gather_combine(ye, lpos_t, wts_t, plan, x1, mod, g_post, layer, tile_mod, row0=0, rows=None, g_pre_next=None):
    n, d = x1.shape
    tm = FIN_TILE
    rows = n if rows is None else rows
    tile0 = row0 // tm
    row = lambda i, *_: (tile0 + i, 0)
    out_row = lambda i, *_: (i, 0)
    mod_spec = lambda lay: pl.BlockSpec((None, None, 6, d), lambda i, t, *_: (lay, t[tile0 + i], 0, 0))
    gain_spec = lambda lay: pl.BlockSpec((None, 1, d), lambda i, *_: (lay, 0, 0))
    prenorm_next = g_pre_next is not None
    in_specs = [pl.BlockSpec(memory_space=pl.ANY),
                pl.BlockSpec((tm, TOP_K), row),
                pl.BlockSpec((tm, TOP_K), row),
                pl.BlockSpec((tm, d), row),
                mod_spec(layer), gain_spec(layer)]
    args = [ye, lpos_t, wts_t, x1, mod, g_post]
    out_specs = [pl.BlockSpec((tm, d), out_row)]
    out_shape = [jax.ShapeDtypeStruct((rows, d), F32)]
    if prenorm_next:
        in_specs += [mod_spec(layer + 1), gain_spec(layer + 1)]
        args += [mod, g_pre_next]
        out_specs += [pl.BlockSpec((tm, d), out_row)]
        out_shape += [jax.ShapeDtypeStruct((rows, d), BF16)]
    outs = pl.pallas_call(
        functools.partial(_gather_kernel, tm=tm, tile0=tile0, prenorm_next=prenorm_next),
        grid_spec=pltpu.PrefetchScalarGridSpec(
            num_scalar_prefetch=5, grid=(rows // tm,),
            in_specs=in_specs, out_specs=out_specs,
            scratch_shapes=[pltpu.VMEM((2, SORT_ROWS, d), F32), pltpu.SemaphoreType.DMA((2,))]),
        out_shape=out_shape,
        compiler_params=_params(1),
        name="moe_gather",
    )(tile_mod, plan["seg"], plan["off"], plan["pos"], plan["used"], *args)
    return outs if prenorm_next else outs[0]


def _moe(h2, wts, lpos, seg, off, x1, mod, g_post, w1, b1, w2, b2, layer, tile_mod, split_rows=None,
         g_pre_next=None):
    n, d = h2.shape
    blk = MOE_BLOCK
    tm = FIN_TILE
    tiles = n // tm
    n_rows = -(-(n * TOP_K + tiles * N_EXPERTS * (SEG_ALIGN - 1) + N_EXPERTS * (blk - 1)) // blk) * blk
    n_blocks = n_rows // blk
    seg = seg[:, :, 0].astype(I32)
    off = off[:, :, 0].astype(I32)
    rows_e = jnp.sum(seg, axis=0)
    region = (rows_e + blk - 1) // blk * blk
    pend = jnp.cumsum(region)
    pstart = pend - region
    pos = pstart[None, :] + jnp.cumsum(seg, axis=0) - seg
    plan = {"seg": seg.reshape(-1), "off": off.reshape(-1), "pos": pos.reshape(-1).astype(I32),
            "used": jnp.sum(seg, axis=1).astype(I32),
            "tail_pos": (pstart + rows_e).astype(I32), "tail_len": (region - rows_e).astype(I32)}
    blocks = jnp.arange(n_blocks, dtype=I32) * blk
    block_expert = jnp.minimum(jnp.sum(blocks[:, None] >= pend[None, :], axis=1), N_EXPERTS - 1).astype(I32)
    n_used = (pend[-1:] // blk).astype(I32)
    valid_rows = jnp.clip((pstart + rows_e)[block_expert] - blocks, 0, blk).astype(I32)
    ids = jnp.arange(n_blocks, dtype=I32)
    run_start = jnp.logical_and(jnp.concatenate([jnp.ones((1,), bool), block_expert[1:] != block_expert[:-1]]),
                                ids < n_used[0])
    first_after = lax.cummin(jnp.where(run_start, ids, n_blocks)[::-1])[::-1]
    first_after = jnp.concatenate([first_after[1:], jnp.full((1,), n_blocks, I32)])
    next_expert = jnp.where(first_after < n_blocks, block_expert[jnp.minimum(first_after, n_blocks - 1)], -1)
    xe = _scatter_rows(h2, lpos, plan, n_rows)
    ye = _experts(xe, block_expert, next_expert.astype(I32), valid_rows, n_used, w1, b1, w2, b2, layer)
    combine = functools.partial(_gather_combine, ye, lpos.T, wts.T, plan, x1, mod, g_post, layer, tile_mod)
    if split_rows is None:
        return combine(g_pre_next=g_pre_next)
    return combine(row0=0, rows=split_rows), combine(row0=split_rows, rows=n - split_rows)


def _tile_mod_ids(n_ctx_rows, n_lat_rows, lat_seq, tm):
    ctx = np.zeros((n_ctx_rows // tm,), np.int32)
    lat = 1 + (np.arange(n_lat_rows // tm) * tm) // lat_seq
    return jnp.asarray(np.concatenate([ctx, lat.astype(np.int32)]))


def kernel(x_prompt, x_sample, cache_k, cache_v, state_ret_fwd, state_ret_bwd, c, c_ctx, w_mod, b_mod, g_pre_mix, g_post_mix, g_pre_ffn, g_post_ffn, w_in, na_rel_bias, ret_decay_fwd, ret_decay_bwd, w_branch, w_out, w_router, b_router, w_exp_in, b_exp_in, w_exp_out, b_exp_out):
    batch, seq, d = x_prompt.shape
    dec_batch, dec_seq, _ = x_sample.shape
    depth = w_in.shape[0]
    n_ctx = batch * seq
    n_lat = dec_batch * dec_seq
    assert 1 + dec_batch <= MOD_ROWS

    x = (x_prompt.reshape(n_ctx, d), x_sample.reshape(n_lat, d))
    cvec =jnp.concatenate([c_ctx[None], c, jnp.zeros((MOD_ROWS - 1 - dec_batch, d), F32)], axis=0)
    mod_all = _modulation(cvec, w_mod, b_mod).reshape(depth, MOD_ROWS, 6, d)
    tmod = {tm: _tile_mod_ids(n_ctx, n_lat, dec_seq, tm) for tm in (ROW_TILE, FIN_TILE)}
    past = cache_k.shape[2]
    ck = cache_k.reshape(dec_batch, depth, past, NA_WIDTH)
    cv = cache_v.reshape(dec_batch, depth, past, NA_WIDTH)
    lg_f = jax.nn.log_sigmoid(ret_decay_fwd.astype(F32)).reshape(-1)
    lg_b = jax.nn.log_sigmoid(ret_decay_bwd.astype(F32)).reshape(-1)
    bias_all = _neighbourhood_bias(na_rel_bias, dec_seq)
    vec = lambda g: g.reshape(depth, 1, d)
    g_pre_mix, g_post_mix, g_pre_ffn, g_post_ffn = vec(g_pre_mix), vec(g_post_mix), vec(g_pre_ffn), vec(g_post_ffn)
    w_branch_b = w_branch.astype(BF16)
    w_out_b = w_out.astype(BF16)
    w_router_t = jnp.swapaxes(w_router, 1, 2)
    b_router_l = jnp.broadcast_to(b_router[:, :, None], (depth, N_EXPERTS, LANES))
    b_exp_in = b_exp_in.reshape(depth, N_EXPERTS, 1, 2 * D_FF)
    b_exp_out = b_exp_out.reshape(depth, N_EXPERTS, 1, d)

    ks, vs, sfs, sbs = [], [], [], []
    h = _prenorm(x, g_pre_mix, mod_all, 0, tmod[ROW_TILE], ROW_TILE)
    for l in range(depth):
        last = l == depth - 1
        z_qkvu = _project(h, w_in, l, 0, QKVU_W, F32)
        z_ret = _project(h, w_in, l, QKVU_W, RET_W, BF16)
        z_gate = _project(h, w_in, l, QKVU_W + RET_W, GATE_W, BF16)
        ks.append(z_qkvu[:n_ctx, NA_WIDTH:2 * NA_WIDTH].reshape(batch, seq, NA_HEADS, NA_HEAD_DIM))
        vs.append(z_qkvu[:n_ctx, 2 * NA_WIDTH:3 * NA_WIDTH].reshape(batch, seq, NA_HEADS, NA_HEAD_DIM))

        a_pair = (_attention_ctx(z_qkvu, batch, seq),
                  _attention_lat(z_qkvu, ck, cv, bias_all, l, n_ctx, dec_batch, dec_seq))
        f_pair = (_fourier(z_qkvu, 0, batch, seq), _fourier(z_qkvu, n_ctx, dec_batch, dec_seq))
        r_ctx, s_f, s_b = _retention(z_ret, lg_f, lg_b, l, 0, batch, seq, rotary=False, state_out=True)
        (r_lat,) = _retention(z_ret, lg_f, lg_b, l, n_ctx, dec_batch, dec_seq, rotary=True,
                              states=(state_ret_fwd, state_ret_bwd))
        sfs.append(s_f)
        sbs.append(s_b)

        x1, h2, wts, lpos, seg, off = _finish(
            (a_pair, f_pair, (r_ctx, r_lat)), z_gate, x, mod_all, g_post_mix, g_pre_ffn,
            w_branch_b, w_out_b, w_router_t, b_router_l, l, tmod[FIN_TILE])
        out = _moe(h2, wts, lpos, seg, off, x1, mod_all, g_post_ffn,
                   w_exp_in, b_exp_in, w_exp_out, b_exp_out, l, tmod[FIN_TILE],
                   split_rows=n_ctx if last else None, g_pre_next=None if last else g_pre_mix)
        x, h = (out, None) if last else out

    y_prompt = x[0].reshape(batch, seq, d)
    y_sample = x[1].reshape(dec_batch, dec_seq, d)
    return (y_prompt, y_sample, jnp.stack(ks, axis=1), jnp.stack(vs, axis=1),
            jnp.stack(sfs, axis=1), jnp.stack(sbs, axis=1))
```

```python
import functools

import numpy as np
import jax
import jax.numpy as jnp
from jax import lax
from jax.experimental import pallas as pl
from jax.experimental.pallas import tpu as pltpu

F32 = jnp.float32
BF16 = jnp.bfloat16
I32 = jnp.int32

D_MODEL = 1024
GRID_W = 64
NA_HEADS = 8
NA_HEAD_DIM = 64
NA_WIDTH = NA_HEADS * NA_HEAD_DIM
WIN_H = 8
WIN_W = 16
KEY_SLAB_ROWS = 12
FOURIER_GROUPS = 4
FOURIER_GROUP_DIM = 128
RET_HEADS = 4
RET_KEY_DIM = 128
ROPE_BASE = 10000.0
BRANCH_WIDTH = 512
N_EXPERTS = 32
TOP_K = 4
D_FF = 1024
SWIGLU_LIMIT = 7.0
SWIGLU_ALPHA = 1.702
EPS = 1e-6
NEG_INF = -1e30

QKVU_W = 4 * NA_WIDTH
RET_W = 4 * BRANCH_WIDTH
GATE_W = 3 * D_MODEL
PROJ_TILE = 1024
PROJ_ROWS = 2048
ATTN_Q_TILE = 512
RET_DECAY_BYTES = 4 * 1024 * 1024

LANES = 128
MOD_ROWS = 16
ROW_TILE = 1024
FIN_TILE = 512
MOE_BLOCK = 512
EXPERT_ROW_STEP = 128
SEG_ALIGN = 8
SORT_CHUNK = 256
SORT_ROWS = -(-(FIN_TILE * TOP_K + N_EXPERTS * (SEG_ALIGN - 1)) // SORT_CHUNK) * SORT_CHUNK
VMEM_LIMIT = 56 * 1024 * 1024


def _params(n_axes, vmem=VMEM_LIMIT):
    return pltpu.CompilerParams(dimension_semantics=("arbitrary",) * n_axes, vmem_limit_bytes=vmem)


def _rms(x):
    return lax.rsqrt(jnp.mean(x * x, axis=-1, keepdims=True) + EPS)


def _mod_kernel(cv_ref, w_ref, b_ref, o_ref):
    cv = cv_ref[...]
    s = (cv * jax.nn.sigmoid(cv)).astype(BF16)
    o_ref[...] = jnp.dot(s, w_ref[...].astype(BF16), preferred_element_type=F32) + b_ref[...]


def _modulation(cv, w_mod, b_mod):
    depth, d, n = w_mod.shape
    tn = 1536
    return pl.pallas_call(
        _mod_kernel,
        grid=(depth, n // tn),
        in_specs=[pl.BlockSpec((MOD_ROWS, d), lambda l, j: (0, 0)),
                  pl.BlockSpec((None, d, tn), lambda l, j: (l, 0, j)),
                  pl.BlockSpec((None, 1, tn), lambda l, j: (l, 0, j))],
        out_specs=pl.BlockSpec((None, MOD_ROWS, tn), lambda l, j: (l, 0, j)),
        out_shape=jax.ShapeDtypeStruct((depth, MOD_ROWS, n), F32),
        compiler_params=_params(2),
        name="modulation",
    )(cv, w_mod, b_mod.reshape(depth, 1, n))


def _token_rows(x, tm):
    if isinstance(x, tuple):
        d = x[0].shape[1]
        ct = x[0].shape[0] // tm
        specs = [pl.BlockSpec((tm, d), lambda i, *_: (jnp.minimum(i, ct - 1), 0)),
                 pl.BlockSpec((tm, d), lambda i, *_: (jnp.maximum(i - ct, 0), 0))]
        return specs, list(x), ct, x[0].shape[0] + x[1].shape[0]
    return [pl.BlockSpec((tm, x.shape[1]), lambda i, *_: (i, 0))], [x], None, x.shape[0]


def _load_rows(refs, ctx_tiles):
    if len(refs) == 1:
        return refs[0][...]
    return jnp.where(pl.program_id(0) < ctx_tiles, refs[0][...], refs[1][...])


def _prenorm_kernel(tmod_ref, *refs, ctx_tiles):
    del tmod_ref
    g_ref, mod_ref, o_ref = refs[-3:]
    x = _load_rows(refs[:-3], ctx_tiles)
    h = x * _rms(x) * g_ref[...]
    o_ref[...] = (h * (1.0 + mod_ref[1:2, :]) + mod_ref[0:1, :]).astype(o_ref.dtype)


def _prenorm(x, g, mod, layer, tile_mod, tm):
    x_specs, x_args, ctx_tiles, n = _token_rows(x, tm)
    d = x_args[0].shape[1]
    return pl.pallas_call(
        functools.partial(_prenorm_kernel, ctx_tiles=ctx_tiles),
        grid_spec=pltpu.PrefetchScalarGridSpec(
            num_scalar_prefetch=1, grid=(n // tm,),
            in_specs=x_specs + [pl.BlockSpec((None, 1, d), lambda i, t: (layer, 0, 0)),
                                pl.BlockSpec((None, None, 6, d), lambda i, t: (layer, t[i], 0, 0))],
            out_specs=pl.BlockSpec((tm, d), lambda i, t: (i, 0))),
        out_shape=jax.ShapeDtypeStruct((n, d), BF16),
        compiler_params=_params(1),
        name="prenorm",
    )(tile_mod, *x_args, g, mod)


def _proj_kernel(h_ref, w_ref, o_ref, wb_ref):
    @pl.when(pl.program_id(1) == 0)
    def _():
        wb_ref[...] = w_ref[...].astype(BF16)

    o_ref[...] = jnp.dot(h_ref[...], wb_ref[...], preferred_element_type=F32).astype(o_ref.dtype)


def _project(h, w, layer, col0, width, out_dtype):
    n, d = h.shape
    tm = PROJ_ROWS
    tn = PROJ_TILE
    cb = col0 // tn
    return pl.pallas_call(
        _proj_kernel,
        grid=(width // tn, n // tm),
        in_specs=[pl.BlockSpec((tm, d), lambda j, i: (i, 0)),
                  pl.BlockSpec((None, d, tn), lambda j, i: (layer, 0, cb + j))],
        out_specs=pl.BlockSpec((tm, tn), lambda j, i: (i, j)),
        out_shape=jax.ShapeDtypeStruct((n, width), out_dtype),
        scratch_shapes=[pltpu.VMEM((d, tn), BF16)],
        compiler_params=_params(2),
        name="in_proj",
    )(h, w)


def _head_pair_masks():
    lane = lax.broadcasted_iota(I32, (1, LANES), 1)
    first = lane < NA_HEAD_DIM
    return first, jnp.logical_not(first)


def _attn_ctx_kernel(q_ref, k_ref, v_ref, o_ref):
    masks = _head_pair_masks()
    scale = NA_HEAD_DIM ** -0.5
    for p in range(NA_WIDTH // LANES):
        cols = slice(LANES * p, LANES * (p + 1))
        q2 = q_ref[:, cols] * scale
        k2 = k_ref[:, cols].astype(BF16)
        v2 = v_ref[:, cols].astype(BF16)
        outs = []
        for m in masks:
            qa = jnp.where(m, q2, 0.0).astype(BF16)
            s = lax.dot_general(qa, k2, (((1,), (1,)), ((), ())), preferred_element_type=F32)
            e = jnp.exp(s - jnp.max(s, axis=-1, keepdims=True))
            den = jnp.sum(e, axis=-1, keepdims=True)
            outs.append(jnp.dot(e.astype(BF16), v2, preferred_element_type=F32) / den)
        o_ref[:, cols] = jnp.where(masks[0], outs[0], outs[1]).astype(o_ref.dtype)


def _attention_ctx(z_qkv, n_seq, seq):
    return pl.pallas_call(
        _attn_ctx_kernel,
        grid=(n_seq,),
        in_specs=[pl.BlockSpec((seq, NA_WIDTH), lambda b: (b, 0)),
                  pl.BlockSpec((seq, NA_WIDTH), lambda b: (b, 1)),
                  pl.BlockSpec((seq, NA_WIDTH), lambda b: (b, 2))],
        out_specs=pl.BlockSpec((seq, NA_WIDTH), lambda b: (b, 0)),
        out_shape=jax.ShapeDtypeStruct((n_seq * seq, NA_WIDTH), BF16),
        compiler_params=_params(1),
        name="attn_ctx",
    )(z_qkv, z_qkv, z_qkv)


def _attn_lat_kernel(q_ref, k_ref, v_ref, kc_ref, vc_ref, bias_ref, o_ref, kb_ref, vb_ref, *, tq):
    masks = _head_pair_masks()
    scale = NA_HEAD_DIM ** -0.5
    seq = q_ref.shape[0]
    slab = bias_ref.shape[3]
    per_half = seq // 2 // tq
    kb_ref[...] = k_ref[...].astype(BF16)
    vb_ref[...] = v_ref[...].astype(BF16)
    kc = kc_ref[...].astype(BF16)
    vc = vc_ref[...].astype(BF16)
    nt = (((1,), (1,)), ((), ()))

    def q_tile(qi, carry):
        rows = pl.ds(pl.multiple_of(qi * tq, tq), tq)
        half = qi // per_half
        half_rows = pl.ds(pl.multiple_of((qi % per_half) * tq, tq), tq)
        keys = pl.ds(pl.multiple_of(half * (seq - slab), seq - slab), slab)
        k2 = kb_ref[keys, :]
        v2 = vb_ref[keys, :]
        q2 = q_ref[rows, :] * scale
        outs = []
        for hh, m in enumerate(masks):
            qa = jnp.where(m, q2, 0.0).astype(BF16)
            s_lat = lax.dot_general(qa, k2, nt, preferred_element_type=F32) + bias_ref[hh, half, half_rows, :]
            s_ctx = lax.dot_general(qa, kc, nt, preferred_element_type=F32)
            mx = jnp.maximum(jnp.max(s_lat, axis=-1, keepdims=True), jnp.max(s_ctx, axis=-1, keepdims=True))
            e_lat = jnp.exp(s_lat - mx)
            e_ctx = jnp.exp(s_ctx - mx)
            den = jnp.sum(e_lat, axis=-1, keepdims=True) + jnp.sum(e_ctx, axis=-1, keepdims=True)
            o = (jnp.dot(e_lat.astype(BF16), v2, preferred_element_type=F32)
                 + jnp.dot(e_ctx.astype(BF16), vc, preferred_element_type=F32))
            outs.append(o / den)
        o_ref[rows, :] = jnp.where(masks[0], outs[0], outs[1]).astype(o_ref.dtype)
        return carry

    lax.fori_loop(0, q_ref.shape[0] // tq, q_tile, 0)


def _attention_lat(z_qkv, cache_k, cache_v, bias, layer, row0, n_seq, seq):
    past = cache_k.shape[2]
    pairs = NA_WIDTH // LANES
    rb = row0 // seq
    kv_cols = NA_WIDTH // LANES
    slab = bias.shape[-1]
    return pl.pallas_call(
        functools.partial(_attn_lat_kernel, tq=min(seq // 2, ATTN_Q_TILE)),
        grid=(pairs, n_seq),
        in_specs=[pl.BlockSpec((seq, LANES), lambda p, b: (rb + b, p)),
                  pl.BlockSpec((seq, LANES), lambda p, b: (rb + b, kv_cols + p)),
                  pl.BlockSpec((seq, LANES), lambda p, b: (rb + b, 2 * kv_cols + p)),
                  pl.BlockSpec((None, None, past, LANES), lambda p, b: (b, layer, 0, p)),
                  pl.BlockSpec((None, None, past, LANES), lambda p, b: (b, layer, 0, p)),
                  pl.BlockSpec((None, 2, 2, seq // 2, slab), lambda p, b: (layer, p, 0, 0, 0))],
        out_specs=pl.BlockSpec((seq, LANES), lambda p, b: (b, p)),
        out_shape=jax.ShapeDtypeStruct((n_seq * seq, NA_WIDTH), BF16),
        scratch_shapes=[pltpu.VMEM((seq, LANES), BF16), pltpu.VMEM((seq, LANES), BF16)],
        compiler_params=_params(2),
        name="attn_lat",
    )(z_qkv, z_qkv, z_qkv, cache_k, cache_v, bias)


def _neighbourhood_bias(rpb, seq):
    rows = seq // GRID_W
    kh = WIN_H
    assert rows >= WIN_H
    lead = rpb.shape[:-2]
    c = np.arange(GRID_W)
    q_cs = np.clip(c - WIN_W // 2, 0, GRID_W - WIN_W)
    col_ok = (c[None, :] >= q_cs[:, None]) & (c[None, :] < q_cs[:, None] + WIN_W)
    r = np.arange(rows)
    rs = np.clip(r - kh // 2, 0, rows - kh)
    base = np.where(r < rows // 2, 0, rows - KEY_SLAB_ROWS)
    assert (rs >= base).all() and (rs + kh <= base + KEY_SLAB_ROWS).all()
    pick_c = (c[None, None, :] - c[None, :, None] + WIN_W - 1
              == np.arange(2 * WIN_W - 1)[:, None, None]).astype(np.float32)
    w = jnp.einsum("...ij,jqk->...qik", rpb, pick_c, precision=lax.Precision.HIGHEST)
    w = jnp.where(jnp.asarray(col_ok)[:, None, :], w, NEG_INF)
    blocks = []
    for rq in range(rows):
        lo = int(rs[rq]) - rq + WIN_H - 1
        slab = w[..., lo:lo + kh, :].reshape(lead + (GRID_W, kh * GRID_W))
        left = int(rs[rq] - base[rq])
        pad = ((0, 0),) * (len(lead) + 1) + ((left * GRID_W, (KEY_SLAB_ROWS - kh - left) * GRID_W),)
        blocks.append(jnp.pad(slab, pad, constant_values=NEG_INF))
    return jnp.stack(blocks, axis=-3).reshape(lead + (2, seq // 2, KEY_SLAB_ROWS * GRID_W))


def _fourier_kernel(u_ref, ct2_ref, cc_ref, sc_ref, o_ref, pq_ref):
    t = u_ref.shape[0]
    for g in range(FOURIER_GROUPS):
        cols = slice(FOURIER_GROUP_DIM * g, FOURIER_GROUP_DIM * (g + 1))
        ug = u_ref[:, cols].astype(BF16)
        pq_ref[0:t, cols] = jnp.dot(ug, cc_ref[...], preferred_element_type=F32).astype(BF16)
        pq_ref[t:2 * t, cols] = jnp.dot(ug, sc_ref[...], preferred_element_type=F32).astype(BF16)
    o_ref[...] = jnp.dot(ct2_ref[...], pq_ref[...], preferred_element_type=F32).astype(o_ref.dtype)


def _dft_tables(t):
    def cs(n):
        k = np.arange(n, dtype=np.int64)
        ang = 2.0 * np.pi * ((k[:, None] * k[None, :]) % n).astype(np.float64) / n
        return np.cos(ang) / np.sqrt(n), np.sin(ang) / np.sqrt(n)

    ct, st = cs(t)
    cc, sc = cs(FOURIER_GROUP_DIM)
    ct2 = np.concatenate([ct, -st], axis=1).astype(np.float32)
    return (jnp.asarray(ct2).astype(BF16), jnp.asarray(cc.astype(np.float32)).astype(BF16),
            jnp.asarray(sc.astype(np.float32)).astype(BF16))


def _fourier(z_qkvu, row0, n_seq, seq):
    ct2, cc, sc = _dft_tables(seq)
    width = FOURIER_GROUPS * FOURIER_GROUP_DIM
    rb = row0 // seq
    ucol = 3 * NA_WIDTH // width
    return pl.pallas_call(
        _fourier_kernel,
        grid=(n_seq,),
        in_specs=[pl.BlockSpec((seq, width), lambda b: (rb + b, ucol)),
                  pl.BlockSpec((seq, 2 * seq), lambda b: (0, 0)),
                  pl.BlockSpec((FOURIER_GROUP_DIM, FOURIER_GROUP_DIM), lambda b: (0, 0)),
                  pl.BlockSpec((FOURIER_GROUP_DIM, FOURIER_GROUP_DIM), lambda b: (0, 0))],
        out_specs=pl.BlockSpec((seq, width), lambda b: (b, 0)),
        out_shape=jax.ShapeDtypeStruct((n_seq * seq, width), BF16),
        scratch_shapes=[pltpu.VMEM((2 * seq, width), BF16)],
        compiler_params=_params(1),
        name="fourier",
    )(z_qkvu, ct2, cc, sc)


def _rotary_tables(t):
    pos = np.arange(t)
    row = (pos // GRID_W).astype(np.float64)
    col = (pos % GRID_W).astype(np.float64)
    nf = RET_KEY_DIM // 4
    inv_freq = ROPE_BASE ** (-np.arange(nf, dtype=np.float64) / nf)
    ar = row[:, None] * inv_freq[None]
    ac = col[:, None] * inv_freq[None]
    cos = np.concatenate([np.cos(ar), np.cos(ar), np.cos(ac), np.cos(ac)], axis=1)
    sin = np.concatenate([-np.sin(ar), np.sin(ar), -np.sin(ac), np.sin(ac)], axis=1)
    return jnp.asarray(cos.astype(np.float32)), jnp.asarray(sin.astype(np.float32))


def _ret_kernel(lgf_ref, lgb_ref, *refs, t, tq, layer, rotary, state_in, state_out):
    refs = list(refs)
    q_ref, k_ref, v_ref, g_ref = refs[:4]
    refs = refs[4:]
    if rotary:
        cos_ref, sin_ref = refs[:2]
        refs = refs[2:]
    if state_in:
        sf0_ref, sb0_ref = refs[:2]
        refs = refs[2:]
    o_ref = refs[0]
    refs = refs[1:]
    if state_out:
        sf_ref, sb_ref = refs[:2]
        refs = refs[2:]
    dec_ref, kb_ref = refs

    scale = RET_KEY_DIM ** -0.5
    nq = t // tq
    heads_here = dec_ref.shape[0]

    if rotary:
        lane = lax.broadcasted_iota(I32, (1, LANES), 1)
        low = (lane % (RET_KEY_DIM // 2)) < (RET_KEY_DIM // 4)

        def rot(x, rows):
            swapped = jnp.where(low, pltpu.roll(x, LANES - RET_KEY_DIM // 4, 1), pltpu.roll(x, RET_KEY_DIM // 4, 1))
            return x * cos_ref[rows, :] + swapped * sin_ref[rows, :]
    else:
        def rot(x, rows):
            return x

    def one_head(hh):
        h = pl.program_id(0) * heads_here + hh
        cols = slice(LANES * hh, LANES * (hh + 1))
        lgf = lgf_ref[layer * RET_HEADS + h]
        lgb = lgb_ref[layer * RET_HEADS + h]

        @pl.when(pl.program_id(1) == 0)
        def _():
            def fill(ri, c):
                rows = pl.ds(pl.multiple_of(ri * tq, tq), tq)
                i = lax.broadcasted_iota(I32, (tq, t), 0) + ri * tq
                j = lax.broadcasted_iota(I32, (tq, t), 1)
                d = (i - j).astype(F32)
                m = jnp.exp(jnp.abs(d) * jnp.where(d > 0, lgf, lgb))
                dec_ref[hh, rows, :] = jnp.where(d == 0, 2.0, m)
                return c

            lax.fori_loop(0, nq, fill, 0)

        kr = rot(k_ref[:, cols].astype(F32), slice(0, t))
        kb_ref[hh] = kr.astype(BF16)
        vb = v_ref[:, cols]

        if state_out:
            j = lax.broadcasted_iota(I32, (t, 1), 0).astype(F32)
            tn = (((0,), (0,)), ((), ()))
            kf = (kr * (scale * jnp.exp(lgf * (t - 1.0 - j)))).astype(BF16)
            kbw = (kr * (scale * jnp.exp(lgb * j))).astype(BF16)
            sf = lax.dot_general(kf, vb, tn, preferred_element_type=F32)
            sb = lax.dot_general(kbw, vb, tn, preferred_element_type=F32)
            if state_in:
                sf = sf + jnp.exp(lgf * t) * sf0_ref[hh]
                sb = sb + jnp.exp(lgb * t) * sb0_ref[hh]
            sf_ref[hh] = sf
            sb_ref[hh] = sb

        def q_tile(qi, carry):
            r0 = pl.multiple_of(qi * tq, tq)
            rows = pl.ds(r0, tq)
            qr = rot(q_ref[rows, cols].astype(F32), rows)
            s = lax.dot_general((qr * scale).astype(BF16), kb_ref[hh], (((1,), (1,)), ((), ())),
                                preferred_element_type=F32)
            y = jnp.dot((s * dec_ref[hh, rows, :]).astype(BF16), vb, preferred_element_type=F32)
            if state_in:
                pos = (lax.broadcasted_iota(I32, (tq, 1), 0) + r0).astype(F32)
                qf = (qr * jnp.exp(lgf * (pos + 1.0))).astype(BF16)
                qb = (qr * jnp.exp(lgb * (t - pos))).astype(BF16)
                y = (y + jnp.dot(qf, sf0_ref[hh].astype(BF16), preferred_element_type=F32)
                     + jnp.dot(qb, sb0_ref[hh].astype(BF16), preferred_element_type=F32))
            mean = jnp.mean(y, axis=-1, keepdims=True)
            yc = y - mean
            yn = yc * lax.rsqrt(jnp.mean(yc * yc, axis=-1, keepdims=True) + EPS)
            g = g_ref[rows, cols].astype(F32)
            o_ref[rows, cols] = (g * jax.nn.sigmoid(g) * yn).astype(o_ref.dtype)
            return carry

        lax.fori_loop(0, nq, q_tile, 0)

    for hh in range(heads_here):
        one_head(hh)


def _retention(z_ret, lg_f, lg_b, layer, row0, n_seq, seq, *, rotary, states=None, state_out=False):
    rb = row0 // seq
    tq = min(seq, ATTN_Q_TILE)
    state_in = states is not None
    hps = RET_HEADS if seq * seq * RET_HEADS * 4 <= RET_DECAY_BYTES else 1
    cb = BRANCH_WIDTH // (hps * LANES)
    width = hps * LANES
    in_specs = [pl.BlockSpec((seq, width), lambda h, b, *_: (rb + b, 0 * cb + h)),
                pl.BlockSpec((seq, width), lambda h, b, *_: (rb + b, 1 * cb + h)),
                pl.BlockSpec((seq, width), lambda h, b, *_: (rb + b, 2 * cb + h)),
                pl.BlockSpec((seq, width), lambda h, b, *_: (rb + b, 3 * cb + h))]
    args = [z_ret, z_ret, z_ret, z_ret]
    if rotary:
        cos, sin = _rotary_tables(seq)
        in_specs += [pl.BlockSpec((seq, LANES), lambda h, b, *_: (0, 0))] * 2
        args += [cos, sin]
    if state_in:
        st_spec = pl.BlockSpec((None, None, hps, RET_KEY_DIM, RET_KEY_DIM), lambda h, b, *_: (b, layer, h, 0, 0))
        in_specs += [st_spec, st_spec]
        args += list(states)
    out_specs = [pl.BlockSpec((seq, width), lambda h, b, *_: (b, h))]
    out_shape = [jax.ShapeDtypeStruct((n_seq * seq, RET_HEADS * LANES), BF16)]
    if state_out:
        so = pl.BlockSpec((None, hps, RET_KEY_DIM, RET_KEY_DIM), lambda h, b, *_: (b, h, 0, 0))
        out_specs += [so, so]
        out_shape += [jax.ShapeDtypeStruct((n_seq, RET_HEADS, RET_KEY_DIM, RET_KEY_DIM), F32)] * 2
    return pl.pallas_call(
        functools.partial(_ret_kernel, t=seq, tq=tq, layer=layer, rotary=rotary, state_in=state_in,
                          state_out=state_out),
        grid_spec=pltpu.PrefetchScalarGridSpec(
            num_scalar_prefetch=2, grid=(RET_HEADS // hps, n_seq),
            in_specs=in_specs, out_specs=out_specs,
            scratch_shapes=[pltpu.VMEM((hps, seq, seq), F32), pltpu.VMEM((hps, seq, LANES), BF16)]),
        out_shape=out_shape,
        compiler_params=_params(2),
        name="retention",
    )(lg_f, lg_b, *args)


def _split_dot_nt(w, x):
    nt = (((1,), (1,)), ((), ()))
    w_hi = w.astype(BF16)
    w_lo = (w - w_hi.astype(F32)).astype(BF16)
    x_hi = x.astype(BF16)
    x_lo = (x - x_hi.astype(F32)).astype(BF16)
    return (lax.dot_general(w_hi, x_hi, nt, preferred_element_type=F32)
            + lax.dot_general(w_hi, x_lo, nt, preferred_element_type=F32)
            + lax.dot_general(w_lo, x_hi, nt, preferred_element_type=F32))


def _finish_kernel(tmod_ref, *refs, ctx_tiles, n_x):
    del tmod_ref
    branch_refs, refs = refs[:6], refs[6:]
    zg_ref, refs = refs[0], refs[1:]
    x_refs, refs = refs[:n_x], refs[n_x:]
    (mod_ref, gpost_ref, gpre_ref, wb_ref, wo_ref, wrt_ref, br_ref, tri_ref, ltri_ref,
     x1_ref, h2_ref, wts_ref, lpos_ref, seg_ref, off_ref) = refs
    d = D_MODEL

    def branch(j):
        return _load_rows(branch_refs[2 * j:2 * j + 2], ctx_tiles)

    def gate(j):
        return jax.nn.sigmoid(zg_ref[:, d * j:d * (j + 1)].astype(F32))

    merged = (gate(0) * jnp.dot(branch(0), wb_ref[0], preferred_element_type=F32)
              + gate(1) * jnp.dot(branch(1), wb_ref[1], preferred_element_type=F32)
              + gate(2) * jnp.dot(branch(2), wb_ref[2], preferred_element_type=F32))
    y = jnp.dot(merged.astype(BF16), wo_ref[...], preferred_element_type=F32)
    x1 = _load_rows(x_refs, ctx_tiles) + mod_ref[2:3, :] * (y * _rms(y) * gpost_ref[...])
    x1_ref[...] = x1
    h2 = x1 * _rms(x1) * gpre_ref[...] * (1.0 + mod_ref[4:5, :]) + mod_ref[3:4, :]
    h2_ref[...] = h2

    logits = _split_dot_nt(wrt_ref[...], h2) + br_ref[:, 0:1]
    tm = logits.shape[1]
    eidx = lax.broadcasted_iota(I32, (N_EXPERTS, tm), 0)
    cur = logits
    vals, hots = [], []
    for k in range(TOP_K):
        m = jnp.max(cur, axis=0, keepdims=True)
        sel = jnp.min(jnp.where(cur == m, eidx, N_EXPERTS), axis=0, keepdims=True)
        hot = eidx == sel
        vals.append(m)
        hots.append(hot)
        cur = jnp.where(hot, -jnp.inf, cur)
    exps = [jnp.exp(v - vals[0]) for v in vals]
    den = exps[0] + exps[1] + exps[2] + exps[3]
    for k in range(TOP_K):
        wts_ref[k:k + 1, :] = exps[k] / den

    member = jnp.logical_or(jnp.logical_or(hots[0], hots[1]), jnp.logical_or(hots[2], hots[3]))
    member_f = member.astype(F32)
    before = jnp.dot(member_f.astype(BF16), tri_ref[...], preferred_element_type=F32)
    units = jnp.ceil(jnp.sum(member_f, axis=1, keepdims=True) * (1.0 / SEG_ALIGN))
    units = jnp.broadcast_to(units, seg_ref.shape)
    off = jnp.dot(ltri_ref[...], units.astype(BF16), preferred_element_type=F32) * SEG_ALIGN
    seg_ref[...] = units * SEG_ALIGN
    off_ref[...] = off
    place = before + off[:, 0:1]
    for k in range(TOP_K):
        lpos_ref[k:k + 1, :] = jnp.sum(jnp.where(hots[k], place, 0.0), axis=0, keepdims=True).astype(I32)


def _finish(branches, z_gate, x, mod, g_post, g_pre_ffn, w_branch, w_out, w_router_t, b_router, layer, tile_mod):
    tm = FIN_TILE
    x_specs, x_args, _, n = _token_rows(x, tm)
    d = x_args[0].shape[1]
    tri = jnp.asarray(np.triu(np.ones((tm, tm), np.float32), k=1)).astype(BF16)
    ltri = jnp.asarray(np.tril(np.ones((N_EXPERTS, N_EXPERTS), np.float32), k=-1)).astype(BF16)
    row = lambda i, t: (i, 0)
    const2 = lambda i, t: (0, 0)
    lay3 = lambda i, t: (layer, 0, 0)
    col = lambda i, t: (0, i)
    branch_specs, branch_args = [], []
    for pair in branches:
        specs, args, ctx_tiles, _ = _token_rows(pair, tm)
        branch_specs += specs
        branch_args += args
    outs = pl.pallas_call(
        functools.partial(_finish_kernel, ctx_tiles=ctx_tiles, n_x=len(x_args)),
        grid_spec=pltpu.PrefetchScalarGridSpec(
            num_scalar_prefetch=1, grid=(n // tm,),
            in_specs=branch_specs + [pl.BlockSpec((tm, GATE_W), row)] + x_specs + [
                      pl.BlockSpec((None, None, 6, d), lambda i, t: (layer, t[i], 0, 0)),
                      pl.BlockSpec((None, 1, d), lay3),
                      pl.BlockSpec((None, 1, d), lay3),
                      pl.BlockSpec((None, 3, BRANCH_WIDTH, d), lambda i, t: (layer, 0, 0, 0)),
                      pl.BlockSpec((None, d, d), lay3),
                      pl.BlockSpec((None, N_EXPERTS, d), lay3),
                      pl.BlockSpec((None, N_EXPERTS, LANES), lay3),
                      pl.BlockSpec((tm, tm), const2),
                      pl.BlockSpec((N_EXPERTS, N_EXPERTS), const2)],
            out_specs=[pl.BlockSpec((tm, d), row),
                       pl.BlockSpec((tm, d), row),
                       pl.BlockSpec((TOP_K, tm), col),
                       pl.BlockSpec((TOP_K, tm), col),
                       pl.BlockSpec((None, N_EXPERTS, LANES), lambda i, t: (i, 0, 0)),
                       pl.BlockSpec((None, N_EXPERTS, LANES), lambda i, t: (i, 0, 0))]),
        out_shape=[jax.ShapeDtypeStruct((n, d), F32),
                   jax.ShapeDtypeStruct((n, d), F32),
                   jax.ShapeDtypeStruct((TOP_K, n), F32),
                   jax.ShapeDtypeStruct((TOP_K, n), I32),
                   jax.ShapeDtypeStruct((n // tm, N_EXPERTS, LANES), F32),
                   jax.ShapeDtypeStruct((n // tm, N_EXPERTS, LANES), F32)],
        compiler_params=_params(1),
        name="merge_router",
    )(tile_mod, *branch_args, z_gate, *x_args, mod, g_post, g_pre_ffn, w_branch, w_out, w_router_t, b_router,
      tri, ltri)
    return outs


def _segment_chunks(length, src_ref, src0, dst_ref, dst0, sem, max_chunk, fixed_src=False):
    out = []
    chunk = max_chunk
    while chunk >= SEG_ALIGN:
        done = jnp.bitwise_and(length, ~(2 * chunk - 1))
        present = jnp.bitwise_and(length, chunk) != 0
        s = 0 if fixed_src else pl.multiple_of(src0 + done, SEG_ALIGN)
        dd = pl.multiple_of(dst0 + done, SEG_ALIGN)
        out.append((present, pltpu.make_async_copy(src_ref.at[pl.ds(s, chunk)], dst_ref.at[pl.ds(dd, chunk)], sem)))
        chunk //= 2
    return out


def _for_each_chunk(n_segments, chunks_of, action):
    def body(e, c):
        for present, cp in chunks_of(e):
            pl.when(present)(functools.partial(action, cp))
        return c

    lax.fori_loop(0, n_segments, body, 0)


def _start(cp):
    cp.start()


def _wait(cp):
    cp.wait()


def _wait_rows(total, src_ref, dst_ref, sem):
    chunk = pl.next_power_of_2(SORT_ROWS) // 2
    while chunk >= SEG_ALIGN:
        @pl.when(jnp.bitwise_and(total, chunk) != 0)
        def _(chunk=chunk):
            pltpu.make_async_copy(src_ref.at[pl.ds(0, chunk)], dst_ref.at[pl.ds(0, chunk)], sem).wait()
        chunk //= 2


def _scatter_kernel(seg_ref, off_ref, pos_ref, used_ref, tpos_ref, tlen_ref, h_ref, lpos_ref, xe_hbm,
                    buf_ref, zero_ref, sem, *, tm):
    t = pl.program_id(0)
    slot = t % 2
    hb = h_ref[...].astype(BF16)
    lp = [lpos_ref[k:k + 1, :] for k in range(TOP_K)]
    rows = buf_ref.shape[1]
    for c0 in range(0, rows, SORT_CHUNK):
        r = lax.broadcasted_iota(I32, (SORT_CHUNK, tm), 0) + c0
        hit = jnp.logical_or(jnp.logical_or(r == lp[0], r == lp[1]), jnp.logical_or(r == lp[2], r == lp[3]))
        onehot = jnp.where(hit, 1.0, 0.0).astype(BF16)
        buf_ref[slot, c0:c0 + SORT_CHUNK, :] = jnp.dot(onehot, hb, preferred_element_type=F32)

    def segments_of(tile):
        def segment(e):
            j = tile * N_EXPERTS + e
            return _segment_chunks(seg_ref[j], buf_ref.at[tile % 2], off_ref[j], xe_hbm, pos_ref[j],
                                   sem.at[tile % 2], tm)
        return segment

    @pl.when(t > 0)
    def _():
        _wait_rows(used_ref[t - 1], buf_ref.at[1 - slot], xe_hbm, sem.at[1 - slot])

    _for_each_chunk(N_EXPERTS, segments_of(t), _start)

    @pl.when(t == pl.num_programs(0) - 1)
    def _():
        _wait_rows(used_ref[t], buf_ref.at[slot], xe_hbm, sem.at[slot])
        zero_ref[...] = jnp.zeros_like(zero_ref)

        def tail(e):
            return _segment_chunks(tlen_ref[e], zero_ref, 0, xe_hbm, tpos_ref[e], sem.at[0], zero_ref.shape[0],
                                   fixed_src=True)

        _for_each_chunk(N_EXPERTS, tail, _start)
        _for_each_chunk(N_EXPERTS, tail, _wait)


def _scatter_rows(h2, lpos, plan, n_rows):
    n, d = h2.shape
    tm = FIN_TILE
    return pl.pallas_call(
        functools.partial(_scatter_kernel, tm=tm),
        grid_spec=pltpu.PrefetchScalarGridSpec(
            num_scalar_prefetch=6, grid=(n // tm,),
            in_specs=[pl.BlockSpec((tm, d), lambda i, *_: (i, 0)),
                      pl.BlockSpec((TOP_K, tm), lambda i, *_: (0, i))],
            out_specs=pl.BlockSpec(memory_space=pl.ANY),
            scratch_shapes=[pltpu.VMEM((2, SORT_ROWS, d), F32), pltpu.VMEM((MOE_BLOCK // 2, d), F32),
                            pltpu.SemaphoreType.DMA((2,))]),
        out_shape=jax.ShapeDtypeStruct((n_rows, d), F32),
        compiler_params=_params(1),
        name="moe_scatter",
    )(plan["seg"], plan["off"], plan["pos"], plan["used"], plan["tail_pos"], plan["tail_len"], h2, lpos)


def _expert_kernel(be_ref, nxt_ref, valid_ref, nu_ref, xb_ref, w1_hbm, b1_ref, w2_hbm, b2_ref, yb_ref,
                   w1s_ref, w2s_ref, w1b_ref, w2b_ref, sem, *, layer):
    i = pl.program_id(0)
    used = i < nu_ref[0]
    fresh = jnp.logical_or(i == 0, be_ref[i] != be_ref[jnp.maximum(i - 1, 0)])

    def fetch(e):
        return (pltpu.make_async_copy(w1_hbm.at[layer, e], w1s_ref, sem.at[0]),
                pltpu.make_async_copy(w2_hbm.at[layer, e], w2s_ref, sem.at[1]))

    @pl.when(i == 0)
    def _():
        for cp in fetch(be_ref[0]):
            cp.start()

    @pl.when(jnp.logical_and(used, fresh))
    def _():
        for cp in fetch(be_ref[i]):
            cp.wait()
        w1b_ref[...] = w1s_ref[...].astype(BF16)
        w2b_ref[...] = w2s_ref[...].astype(BF16)

        @pl.when(nxt_ref[i] >= 0)
        def _():
            for cp in fetch(nxt_ref[i]):
                cp.start()

    def ffn(m):
        z = jnp.dot(xb_ref[0:m, :].astype(BF16), w1b_ref[...], preferred_element_type=F32) + b1_ref[...]
        glu = jnp.minimum(z[:, :D_FF], SWIGLU_LIMIT)
        lin = jnp.clip(z[:, D_FF:], -SWIGLU_LIMIT, SWIGLU_LIMIT)
        act = glu * jax.nn.sigmoid(SWIGLU_ALPHA * glu) * (lin + 1.0)
        yb_ref[0:m, :] = jnp.dot(act.astype(BF16), w2b_ref[...], preferred_element_type=F32) + b2_ref[...]
        if m < yb_ref.shape[0]:
            yb_ref[m:, :] = jnp.zeros((yb_ref.shape[0] - m, yb_ref.shape[1]), F32)

    valid = valid_ref[i]
    for m in range(EXPERT_ROW_STEP, yb_ref.shape[0] + 1, EXPERT_ROW_STEP):
        pl.when(jnp.logical_and(valid > m - EXPERT_ROW_STEP, valid <= m))(functools.partial(ffn, m))

    @pl.when(valid == 0)
    def _():
        yb_ref[...] = jnp.zeros_like(yb_ref)


def _experts(xb, block_expert, next_expert, valid_rows, n_used, w1, b1, w2, b2, layer):
    n_rows, d = xb.shape
    tm = MOE_BLOCK
    return pl.pallas_call(
        functools.partial(_expert_kernel, layer=layer),
        grid_spec=pltpu.PrefetchScalarGridSpec(
            num_scalar_prefetch=4, grid=(n_rows // tm,),
            in_specs=[pl.BlockSpec((tm, d), lambda i, be, nx, vr, nu: (jnp.minimum(i, nu[0] - 1), 0)),
                      pl.BlockSpec(memory_space=pl.ANY),
                      pl.BlockSpec((None, None, 1, 2 * D_FF), lambda i, be, nx, vr, nu: (layer, be[i], 0, 0)),
                      pl.BlockSpec(memory_space=pl.ANY),
                      pl.BlockSpec((None, None, 1, d), lambda i, be, nx, vr, nu: (layer, be[i], 0, 0))],
            out_specs=pl.BlockSpec((tm, d), lambda i, be, nx, vr, nu: (i, 0)),
            scratch_shapes=[pltpu.VMEM((d, 2 * D_FF), F32), pltpu.VMEM((D_FF, d), F32),
                            pltpu.VMEM((d, 2 * D_FF), BF16), pltpu.VMEM((D_FF, d), BF16),
                            pltpu.SemaphoreType.DMA((2,))]),
        out_shape=jax.ShapeDtypeStruct((n_rows, d), F32),
        compiler_params=_params(1),
        name="moe_experts",
    )(block_expert, next_expert, valid_rows, n_used, xb, w1, b1, w2, b2)


def _gather_kernel(tmod_ref, seg_ref, off_ref, pos_ref, used_ref, ye_hbm, lpos_ref, wts_ref, x1_ref, mod_ref, g_ref,
                   *refs, tm, tile0, prenorm_next):
    del tmod_ref
    if prenorm_next:
        modn_ref, gn_ref, o_ref, hn_ref, buf_ref, sem = refs
    else:
        o_ref, buf_ref, sem = refs
    step = pl.program_id(0)
    t = step + tile0
    slot = step % 2

    def segments_of(tile):
        def segment(e):
            j = tile * N_EXPERTS + e
            half = (tile - tile0) % 2
            return _segment_chunks(seg_ref[j], ye_hbm, pos_ref[j], buf_ref.at[half], off_ref[j], sem.at[half], tm)
        return segment

    @pl.when(step == 0)
    def _():
        buf_ref[...] = jnp.zeros_like(buf_ref)
        _for_each_chunk(N_EXPERTS, segments_of(t), _start)

    @pl.when(step + 1 < pl.num_programs(0))
    def _():
        _for_each_chunk(N_EXPERTS, segments_of(t + 1), _start)

    _wait_rows(used_ref[t], ye_hbm, buf_ref.at[slot], sem.at[slot])

    lp = [lpos_ref[:, k:k + 1] for k in range(TOP_K)]
    wt = [wts_ref[:, k:k + 1] for k in range(TOP_K)]
    y = jnp.zeros(o_ref.shape, F32)
    for c0 in range(0, buf_ref.shape[1], SORT_CHUNK):
        r = lax.broadcasted_iota(I32, (tm, SORT_CHUNK), 1) + c0
        wm = jnp.where(r == lp[0], wt[0], 0.0)
        for k in range(1, TOP_K):
            wm = jnp.where(r == lp[k], wt[k], wm)
        y = y + jnp.dot(wm.astype(BF16), buf_ref[slot, c0:c0 + SORT_CHUNK, :].astype(BF16),
                        preferred_element_type=F32)
    x2 = x1_ref[...] + mod_ref[5:6, :] * (y * _rms(y) * g_ref[...])
    o_ref[...] = x2
    if prenorm_next:
        hn = x2 * _rms(x2) * gn_ref[...]
        hn_ref[...] = (hn * (1.0 + modn_ref[1:2, :]) + modn_ref[0:1, :]).astype(hn_ref.dtype)


def _gather_combine(ye, lpos_t, wts_t, plan, x1, mod, g_post, layer, tile_mod, row0=0, rows=None, g_pre_next=None):
    n, d = x1.shape
    tm = FIN_TILE
    rows = n if rows is None else rows
    tile0 = row0 // tm
    row = lambda i, *_: (tile0 + i, 0)
    out_row = lambda i, *_: (i, 0)
    mod_spec = lambda lay: pl.BlockSpec((None, None, 6, d), lambda i, t, *_: (lay, t[tile0 + i], 0, 0))
    gain_spec = lambda lay: pl.BlockSpec((None, 1, d), lambda i, *_: (lay, 0, 0))
    prenorm_next = g_pre_next is not None
    in_specs = [pl.BlockSpec(memory_space=pl.ANY),
                pl.BlockSpec((tm, TOP_K), row),
                pl.BlockSpec((tm, TOP_K), row),
                pl.BlockSpec((tm, d), row),
                mod_spec(layer), gain_spec(layer)]
    args = [ye, lpos_t, wts_t, x1, mod, g_post]
    out_specs = [pl.BlockSpec((tm, d), out_row)]
    out_shape = [jax.ShapeDtypeStruct((rows, d), F32)]
    if prenorm_next:
        in_specs += [mod_spec(layer + 1), gain_spec(layer + 1)]
        args += [mod, g_pre_next]
        out_specs += [pl.BlockSpec((tm, d), out_row)]
        out_shape += [jax.ShapeDtypeStruct((rows, d), BF16)]
    outs = pl.pallas_call(
        functools.partial(_gather_kernel, tm=tm, tile0=tile0, prenorm_next=prenorm_next),
        grid_spec=pltpu.PrefetchScalarGridSpec(
            num_scalar_prefetch=5, grid=(rows // tm,),
            in_specs=in_specs, out_specs=out_specs,
            scratch_shapes=[pltpu.VMEM((2, SORT_ROWS, d), F32), pltpu.SemaphoreType.DMA((2,))]),
        out_shape=out_shape,
        compiler_params=_params(1),
        name="moe_gather",
    )(tile_mod, plan["seg"], plan["off"], plan["pos"], plan["used"], *args)
    return outs if prenorm_next else outs[0]


def _moe(h2, wts, lpos, seg, off, x1, mod, g_post, w1, b1, w2, b2, layer, tile_mod, split_rows=None,
         g_pre_next=None):
    n, d = h2.shape
    blk = MOE_BLOCK
    tm = FIN_TILE
    tiles = n // tm
    n_rows = -(-(n * TOP_K + tiles * N_EXPERTS * (SEG_ALIGN - 1) + N_EXPERTS * (blk - 1)) // blk) * blk
    n_blocks = n_rows // blk
    seg = seg[:, :, 0].astype(I32)
    off = off[:, :, 0].astype(I32)
    rows_e = jnp.sum(seg, axis=0)
    region = (rows_e + blk - 1) // blk * blk
    pend = jnp.cumsum(region)
    pstart = pend - region
    pos = pstart[None, :] + jnp.cumsum(seg, axis=0) - seg
    plan = {"seg": seg.reshape(-1), "off": off.reshape(-1), "pos": pos.reshape(-1).astype(I32),
            "used": jnp.sum(seg, axis=1).astype(I32),
            "tail_pos": (pstart + rows_e).astype(I32), "tail_len": (region - rows_e).astype(I32)}
    blocks = jnp.arange(n_blocks, dtype=I32) * blk
    block_expert = jnp.minimum(jnp.sum(blocks[:, None] >= pend[None, :], axis=1), N_EXPERTS - 1).astype(I32)
    n_used = (pend[-1:] // blk).astype(I32)
    valid_rows = jnp.clip((pstart + rows_e)[block_expert] - blocks, 0, blk).astype(I32)
    ids = jnp.arange(n_blocks, dtype=I32)
    run_start = jnp.logical_and(jnp.concatenate([jnp.ones((1,), bool), block_expert[1:] != block_expert[:-1]]),
                                ids < n_used[0])
    first_after = lax.cummin(jnp.where(run_start, ids, n_blocks)[::-1])[::-1]
    first_after = jnp.concatenate([first_after[1:], jnp.full((1,), n_blocks, I32)])
    next_expert = jnp.where(first_after < n_blocks, block_expert[jnp.minimum(first_after, n_blocks - 1)], -1)
    xe = _scatter_rows(h2, lpos, plan, n_rows)
    ye = _experts(xe, block_expert, next_expert.astype(I32), valid_rows, n_used, w1, b1, w2, b2, layer)
    combine = functools.partial(_gather_combine, ye, lpos.T, wts.T, plan, x1, mod, g_post, layer, tile_mod)
    if split_rows is None:
        return combine(g_pre_next=g_pre_next)
    return combine(row0=0, rows=split_rows), combine(row0=split_rows, rows=n - split_rows)


def _tile_mod_ids(n_ctx_rows, n_lat_rows, lat_seq, tm):
    ctx = np.zeros((n_ctx_rows // tm,), np.int32)
    lat = 1 + (np.arange(n_lat_rows // tm) * tm) // lat_seq
    return jnp.asarray(np.concatenate([ctx, lat.astype(np.int32)]))


def kernel(x_prompt, x_sample, cache_k, cache_v, state_ret_fwd, state_ret_bwd, c, c_ctx, w_mod, b_mod, g_pre_mix, g_post_mix, g_pre_ffn, g_post_ffn, w_in, na_rel_bias, ret_decay_fwd, ret_decay_bwd, w_branch, w_out, w_router, b_router, w_exp_in, b_exp_in, w_exp_out, b_exp_out):
    batch, seq, d = x_prompt.shape
    dec_batch, dec_seq, _ = x_sample.shape
    depth = w_in.shape[0]
    n_ctx = batch * seq
    n_lat = dec_batch * dec_seq
    assert 1 + dec_batch <= MOD_ROWS

    x = (x_prompt.reshape(n_ctx, d), x_sample.reshape(n_lat, d))
    cvec =jnp.concatenate([c_ctx[None], c, jnp.zeros((MOD_ROWS - 1 - dec_batch, d), F32)], axis=0)
    mod_all = _modulation(cvec, w_mod, b_mod).reshape(depth, MOD_ROWS, 6, d)
    tmod = {tm: _tile_mod_ids(n_ctx, n_lat, dec_seq, tm) for tm in (ROW_TILE, FIN_TILE)}
    past = cache_k.shape[2]
    ck = cache_k.reshape(dec_batch, depth, past, NA_WIDTH)
    cv = cache_v.reshape(dec_batch, depth, past, NA_WIDTH)
    lg_f = jax.nn.log_sigmoid(ret_decay_fwd.astype(F32)).reshape(-1)
    lg_b = jax.nn.log_sigmoid(ret_decay_bwd.astype(F32)).reshape(-1)
    bias_all = _neighbourhood_bias(na_rel_bias, dec_seq)
    vec = lambda g: g.reshape(depth, 1, d)
    g_pre_mix, g_post_mix, g_pre_ffn, g_post_ffn = vec(g_pre_mix), vec(g_post_mix), vec(g_pre_ffn), vec(g_post_ffn)
    w_branch_b = w_branch.astype(BF16)
    w_out_b = w_out.astype(BF16)
    w_router_t = jnp.swapaxes(w_router, 1, 2)
    b_router_l = jnp.broadcast_to(b_router[:, :, None], (depth, N_EXPERTS, LANES))
    b_exp_in = b_exp_in.reshape(depth, N_EXPERTS, 1, 2 * D_FF)
    b_exp_out = b_exp_out.reshape(depth, N_EXPERTS, 1, d)

    ks, vs, sfs, sbs = [], [], [], []
    h = _prenorm(x, g_pre_mix, mod_all, 0, tmod[ROW_TILE], ROW_TILE)
    for l in range(depth):
        last = l == depth - 1
        z_qkvu = _project(h, w_in, l, 0, QKVU_W, F32)
        z_ret = _project(h, w_in, l, QKVU_W, RET_W, BF16)
        z_gate = _project(h, w_in, l, QKVU_W + RET_W, GATE_W, BF16)
        ks.append(z_qkvu[:n_ctx, NA_WIDTH:2 * NA_WIDTH].reshape(batch, seq, NA_HEADS, NA_HEAD_DIM))
        vs.append(z_qkvu[:n_ctx, 2 * NA_WIDTH:3 * NA_WIDTH].reshape(batch, seq, NA_HEADS, NA_HEAD_DIM))

        a_pair = (_attention_ctx(z_qkvu, batch, seq),
                  _attention_lat(z_qkvu, ck, cv, bias_all, l, n_ctx, dec_batch, dec_seq))
        f_pair = (_fourier(z_qkvu, 0, batch, seq), _fourier(z_qkvu, n_ctx, dec_batch, dec_seq))
        r_ctx, s_f, s_b = _retention(z_ret, lg_f, lg_b, l, 0, batch, seq, rotary=False, state_out=True)
        (r_lat,) = _retention(z_ret, lg_f, lg_b, l, n_ctx, dec_batch, dec_seq, rotary=True,
                              states=(state_ret_fwd, state_ret_bwd))
        sfs.append(s_f)
        sbs.append(s_b)

        x1, h2, wts, lpos, seg, off = _finish(
            (a_pair, f_pair, (r_ctx, r_lat)), z_gate, x, mod_all, g_post_mix, g_pre_ffn,
            w_branch_b, w_out_b, w_router_t, b_router_l, l, tmod[FIN_TILE])
        out = _moe(h2, wts, lpos, seg, off, x1, mod_all, g_post_ffn,
                   w_exp_in, b_exp_in, w_exp_out, b_exp_out, l, tmod[FIN_TILE],
                   split_rows=n_ctx if last else None, g_pre_next=None if last else g_pre_mix)
        x, h = (out, None) if last else out

    y_prompt = x[0].reshape(batch, seq, d)
    y_sample = x[1].reshape(dec_batch, dec_seq, d)
    return (y_prompt, y_sample, jnp.stack(ks, axis=1), jnp.stack(vs, axis=1),
            jnp.stack(sfs, axis=1), jnp.stack(sbs, axis=1))
```

```python
import functools

import numpy as np
import jax
import jax.numpy as jnp
from jax import lax
from jax.experimental import pallas as pl
from jax.experimental.pallas import tpu as pltpu

F32 = jnp.float32
BF16 = jnp.bfloat16
I32 = jnp.int32

D_MODEL = 1024
GRID_W = 64
NA_HEADS = 8
NA_HEAD_DIM = 64
NA_WIDTH = NA_HEADS * NA_HEAD_DIM
WIN_H = 8
WIN_W = 16
KEY_SLAB_ROWS = 12
FOURIER_GROUPS = 4
FOURIER_GROUP_DIM = 128
RET_HEADS = 4
RET_KEY_DIM = 128
ROPE_BASE = 10000.0
BRANCH_WIDTH = 512
N_EXPERTS = 32
TOP_K = 4
D_FF = 1024
SWIGLU_LIMIT = 7.0
SWIGLU_ALPHA = 1.702
EPS = 1e-6
NEG_INF = -1e30

QKVU_W = 4 * NA_WIDTH
RET_W = 4 * BRANCH_WIDTH
GATE_W = 3 * D_MODEL
PROJ_TILE = 1024
PROJ_ROWS = 2048
ATTN_Q_TILE = 512
RET_DECAY_BYTES = 4 * 1024 * 1024

LANES = 128
MOD_ROWS = 16
ROW_TILE = 1024
FIN_TILE = 512
MOE_BLOCK = 512
EXPERT_ROW_STEP = 128
SEG_ALIGN = 8
SORT_CHUNK = 256
SORT_ROWS = -(-(FIN_TILE * TOP_K + N_EXPERTS * (SEG_ALIGN - 1)) // SORT_CHUNK) * SORT_CHUNK
VMEM_LIMIT = 56 * 1024 * 1024


def _params(n_axes, vmem=VMEM_LIMIT):
    return pltpu.CompilerParams(dimension_semantics=("arbitrary",) * n_axes, vmem_limit_bytes=vmem)


def _rms(x):
    return lax.rsqrt(jnp.mean(x * x, axis=-1, keepdims=True) + EPS)


def _mod_kernel(cv_ref, w_ref, b_ref, o_ref):
    cv = cv_ref[...]
    s = (cv * jax.nn.sigmoid(cv)).astype(BF16)
    o_ref[...] = jnp.dot(s, w_ref[...].astype(BF16), preferred_element_type=F32) + b_ref[...]


def _modulation(cv, w_mod, b_mod):
    depth, d, n = w_mod.shape
    tn = 1536
    return pl.pallas_call(
        _mod_kernel,
        grid=(depth, n // tn),
        in_specs=[pl.BlockSpec((MOD_ROWS, d), lambda l, j: (0, 0)),
                  pl.BlockSpec((None, d, tn), lambda l, j: (l, 0, j)),
                  pl.BlockSpec((None, 1, tn), lambda l, j: (l, 0, j))],
        out_specs=pl.BlockSpec((None, MOD_ROWS, tn), lambda l, j: (l, 0, j)),
        out_shape=jax.ShapeDtypeStruct((depth, MOD_ROWS, n), F32),
        compiler_params=_params(2),
        name="modulation",
    )(cv, w_mod, b_mod.reshape(depth, 1, n))


def _token_rows(x, tm):
    if isinstance(x, tuple):
        d = x[0].shape[1]
        ct = x[0].shape[0] // tm
        specs = [pl.BlockSpec((tm, d), lambda i, *_: (jnp.minimum(i, ct - 1), 0)),
                 pl.BlockSpec((tm, d), lambda i, *_: (jnp.maximum(i - ct, 0), 0))]
        return specs, list(x), ct, x[0].shape[0] + x[1].shape[0]
    return [pl.BlockSpec((tm, x.shape[1]), lambda i, *_: (i, 0))], [x], None, x.shape[0]


def _load_rows(refs, ctx_tiles):
    if len(refs) == 1:
        return refs[0][...]
    return jnp.where(pl.program_id(0) < ctx_tiles, refs[0][...], refs[1][...])


def _prenorm_kernel(tmod_ref, *refs, ctx_tiles):
    del tmod_ref
    g_ref, mod_ref, o_ref = refs[-3:]
    x = _load_rows(refs[:-3], ctx_tiles)
    h = x * _rms(x) * g_ref[...]
    o_ref[...] = (h * (1.0 + mod_ref[1:2, :]) + mod_ref[0:1, :]).astype(o_ref.dtype)


def _prenorm(x, g, mod, layer, tile_mod, tm):
    x_specs, x_args, ctx_tiles, n = _token_rows(x, tm)
    d = x_args[0].shape[1]
    return pl.pallas_call(
        functools.partial(_prenorm_kernel, ctx_tiles=ctx_tiles),
        grid_spec=pltpu.PrefetchScalarGridSpec(
            num_scalar_prefetch=1, grid=(n // tm,),
            in_specs=x_specs + [pl.BlockSpec((None, 1, d), lambda i, t: (layer, 0, 0)),
                                pl.BlockSpec((None, None, 6, d), lambda i, t: (layer, t[i], 0, 0))],
            out_specs=pl.BlockSpec((tm, d), lambda i, t: (i, 0))),
        out_shape=jax.ShapeDtypeStruct((n, d), BF16),
        compiler_params=_params(1),
        name="prenorm",
    )(tile_mod, *x_args, g, mod)


def _proj_kernel(h_ref, w_ref, o_ref, wb_ref):
    @pl.when(pl.program_id(1) == 0)
    def _():
        wb_ref[...] = w_ref[...].astype(BF16)

    o_ref[...] = jnp.dot(h_ref[...], wb_ref[...], preferred_element_type=F32).astype(o_ref.dtype)


def _project(h, w, layer, col0, width, out_dtype):
    n, d = h.shape
    tm = PROJ_ROWS
    tn = PROJ_TILE
    cb = col0 // tn
    return pl.pallas_call(
        _proj_kernel,
        grid=(width // tn, n // tm),
        in_specs=[pl.BlockSpec((tm, d), lambda j, i: (i, 0)),
                  pl.BlockSpec((None, d, tn), lambda j, i: (layer, 0, cb + j))],
        out_specs=pl.BlockSpec((tm, tn), lambda j, i: (i, j)),
        out_shape=jax.ShapeDtypeStruct((n, width), out_dtype),
        scratch_shapes=[pltpu.VMEM((d, tn), BF16)],
        compiler_params=_params(2),
        name="in_proj",
    )(h, w)


def _head_pair_masks():
    lane = lax.broadcasted_iota(I32, (1, LANES), 1)
    first = lane < NA_HEAD_DIM
    return first, jnp.logical_not(first)


def _attn_ctx_kernel(q_ref, k_ref, v_ref, o_ref):
    masks = _head_pair_masks()
    scale = NA_HEAD_DIM ** -0.5
    for p in range(NA_WIDTH // LANES):
        cols = slice(LANES * p, LANES * (p + 1))
        q2 = q_ref[:, cols] * scale
        k2 = k_ref[:, cols].astype(BF16)
        v2 = v_ref[:, cols].astype(BF16)
        outs = []
        for m in masks:
            qa = jnp.where(m, q2, 0.0).astype(BF16)
            s = lax.dot_general(qa, k2, (((1,), (1,)), ((), ())), preferred_element_type=F32)
            e = jnp.exp(s - jnp.max(s, axis=-1, keepdims=True))
            den = jnp.sum(e, axis=-1, keepdims=True)
            outs.append(jnp.dot(e.astype(BF16), v2, preferred_element_type=F32) / den)
        o_ref[:, cols] = jnp.where(masks[0], outs[0], outs[1]).astype(o_ref.dtype)


def _attention_ctx(z_qkv, n_seq, seq):
    return pl.pallas_call(
        _attn_ctx_kernel,
        grid=(n_seq,),
        in_specs=[pl.BlockSpec((seq, NA_WIDTH), lambda b: (b, 0)),
                  pl.BlockSpec((seq, NA_WIDTH), lambda b: (b, 1)),
                  pl.BlockSpec((seq, NA_WIDTH), lambda b: (b, 2))],
        out_specs=pl.BlockSpec((seq, NA_WIDTH), lambda b: (b, 0)),
        out_shape=jax.ShapeDtypeStruct((n_seq * seq, NA_WIDTH), BF16),
        compiler_params=_params(1),
        name="attn_ctx",
    )(z_qkv, z_qkv, z_qkv)


def _attn_lat_kernel(q_ref, k_ref, v_ref, kc_ref, vc_ref, bias_ref, o_ref, kb_ref, vb_ref, *, tq):
    masks = _head_pair_masks()
    scale = NA_HEAD_DIM ** -0.5
    seq = q_ref.shape[0]
    slab = bias_ref.shape[3]
    per_half = seq // 2 // tq
    kb_ref[...] = k_ref[...].astype(BF16)
    vb_ref[...] = v_ref[...].astype(BF16)
    kc = kc_ref[...].astype(BF16)
    vc = vc_ref[...].astype(BF16)
    nt = (((1,), (1,)), ((), ()))

    def q_tile(qi, carry):
        rows = pl.ds(pl.multiple_of(qi * tq, tq), tq)
        half = qi // per_half
        half_rows = pl.ds(pl.multiple_of((qi % per_half) * tq, tq), tq)
        keys = pl.ds(pl.multiple_of(half * (seq - slab), seq - slab), slab)
        k2 = kb_ref[keys, :]
        v2 = vb_ref[keys, :]
        q2 = q_ref[rows, :] * scale
        outs = []
        for hh, m in enumerate(masks):
            qa = jnp.where(m, q2, 0.0).astype(BF16)
            s_lat = lax.dot_general(qa, k2, nt, preferred_element_type=F32) + bias_ref[hh, half, half_rows, :]
            s_ctx = lax.dot_general(qa, kc, nt, preferred_element_type=F32)
            mx = jnp.maximum(jnp.max(s_lat, axis=-1, keepdims=True), jnp.max(s_ctx, axis=-1, keepdims=True))
            e_lat = jnp.exp(s_lat - mx)
            e_ctx = jnp.exp(s_ctx - mx)
            den = jnp.sum(e_lat, axis=-1, keepdims=True) + jnp.sum(e_ctx, axis=-1, keepdims=True)
            o = (jnp.dot(e_lat.astype(BF16), v2, preferred_element_type=F32)
                 + jnp.dot(e_ctx.astype(BF16), vc, preferred_element_type=F32))
            outs.append(o / den)
        o_ref[rows, :] = jnp.where(masks[0], outs[0], outs[1]).astype(o_ref.dtype)
        return carry

    lax.fori_loop(0, q_ref.shape[0] // tq, q_tile, 0)


def _attention_lat(z_qkv, cache_k, cache_v, bias, layer, row0, n_seq, seq):
    past = cache_k.shape[2]
    pairs = NA_WIDTH // LANES
    rb = row0 // seq
    kv_cols = NA_WIDTH // LANES
    slab = bias.shape[-1]
    return pl.pallas_call(
        functools.partial(_attn_lat_kernel, tq=min(seq // 2, ATTN_Q_TILE)),
        grid=(pairs, n_seq),
        in_specs=[pl.BlockSpec((seq, LANES), lambda p, b: (rb + b, p)),
                  pl.BlockSpec((seq, LANES), lambda p, b: (rb + b, kv_cols + p)),
                  pl.BlockSpec((seq, LANES), lambda p, b: (rb + b, 2 * kv_cols + p)),
                  pl.BlockSpec((None, None, past, LANES), lambda p, b: (b, layer, 0, p)),
                  pl.BlockSpec((None, None, past, LANES), lambda p, b: (b, layer, 0, p)),
                  pl.BlockSpec((None, 2, 2, seq // 2, slab), lambda p, b: (layer, p, 0, 0, 0))],
        out_specs=pl.BlockSpec((seq, LANES), lambda p, b: (b, p)),
        out_shape=jax.ShapeDtypeStruct((n_seq * seq, NA_WIDTH), BF16),
        scratch_shapes=[pltpu.VMEM((seq, LANES), BF16), pltpu.VMEM((seq, LANES), BF16)],
        compiler_params=_params(2),
        name="attn_lat",
    )(z_qkv, z_qkv, z_qkv, cache_k, cache_v, bias)


def _neighbourhood_bias(rpb, seq):
    rows = seq // GRID_W
    kh = WIN_H
    assert rows >= WIN_H
    lead = rpb.shape[:-2]
    c = np.arange(GRID_W)
    q_cs = np.clip(c - WIN_W // 2, 0, GRID_W - WIN_W)
    col_ok = (c[None, :] >= q_cs[:, None]) & (c[None, :] < q_cs[:, None] + WIN_W)
    r = np.arange(rows)
    rs = np.clip(r - kh // 2, 0, rows - kh)
    base = np.where(r < rows // 2, 0, rows - KEY_SLAB_ROWS)
    assert (rs >= base).all() and (rs + kh <= base + KEY_SLAB_ROWS).all()
    pick_c = (c[None, None, :] - c[None, :, None] + WIN_W - 1
              == np.arange(2 * WIN_W - 1)[:, None, None]).astype(np.float32)
    w = jnp.einsum("...ij,jqk->...qik", rpb, pick_c, precision=lax.Precision.HIGHEST)
    w = jnp.where(jnp.asarray(col_ok)[:, None, :], w, NEG_INF)
    blocks = []
    for rq in range(rows):
        lo = int(rs[rq]) - rq + WIN_H - 1
        slab = w[..., lo:lo + kh, :].reshape(lead + (GRID_W, kh * GRID_W))
        left = int(rs[rq] - base[rq])
        pad = ((0, 0),) * (len(lead) + 1) + ((left * GRID_W, (KEY_SLAB_ROWS - kh - left) * GRID_W),)
        blocks.append(jnp.pad(slab, pad, constant_values=NEG_INF))
    return jnp.stack(blocks, axis=-3).reshape(lead + (2, seq // 2, KEY_SLAB_ROWS * GRID_W))


def _fourier_kernel(u_ref, ct2_ref, cc_ref, sc_ref, o_ref, pq_ref):
    t = u_ref.shape[0]
    for g in range(FOURIER_GROUPS):
        cols = slice(FOURIER_GROUP_DIM * g, FOURIER_GROUP_DIM * (g + 1))
        ug = u_ref[:, cols].astype(BF16)
        pq_ref[0:t, cols] = jnp.dot(ug, cc_ref[...], preferred_element_type=F32).astype(BF16)
        pq_ref[t:2 * t, cols] = jnp.dot(ug, sc_ref[...], preferred_element_type=F32).astype(BF16)
    o_ref[...] = jnp.dot(ct2_ref[...], pq_ref[...], preferred_element_type=F32).astype(o_ref.dtype)


def _dft_tables(t):
    def cs(n):
        k = np.arange(n, dtype=np.int64)
        ang = 2.0 * np.pi * ((k[:, None] * k[None, :]) % n).astype(np.float64) / n
        return np.cos(ang) / np.sqrt(n), np.sin(ang) / np.sqrt(n)

    ct, st = cs(t)
    cc, sc = cs(FOURIER_GROUP_DIM)
    ct2 = np.concatenate([ct, -st], axis=1).astype(np.float32)
    return (jnp.asarray(ct2).astype(BF16), jnp.asarray(cc.astype(np.float32)).astype(BF16),
            jnp.asarray(sc.astype(np.float32)).astype(BF16))


def _fourier(z_qkvu, row0, n_seq, seq):
    ct2, cc, sc = _dft_tables(seq)
    width = FOURIER_GROUPS * FOURIER_GROUP_DIM
    rb = row0 // seq
    ucol = 3 * NA_WIDTH // width
    return pl.pallas_call(
        _fourier_kernel,
        grid=(n_seq,),
        in_specs=[pl.BlockSpec((seq, width), lambda b: (rb + b, ucol)),
                  pl.BlockSpec((seq, 2 * seq), lambda b: (0, 0)),
                  pl.BlockSpec((FOURIER_GROUP_DIM, FOURIER_GROUP_DIM), lambda b: (0, 0)),
                  pl.BlockSpec((FOURIER_GROUP_DIM, FOURIER_GROUP_DIM), lambda b: (0, 0))],
        out_specs=pl.BlockSpec((seq, width), lambda b: (b, 0)),
        out_shape=jax.ShapeDtypeStruct((n_seq * seq, width), BF16),
        scratch_shapes=[pltpu.VMEM((2 * seq, width), BF16)],
        compiler_params=_params(1),
        name="fourier",
    )(z_qkvu, ct2, cc, sc)


def _rotary_tables(t):
    pos = np.arange(t)
    row = (pos // GRID_W).astype(np.float64)
    col = (pos % GRID_W).astype(np.float64)
    nf = RET_KEY_DIM // 4
    inv_freq = ROPE_BASE ** (-np.arange(nf, dtype=np.float64) / nf)
    ar = row[:, None] * inv_freq[None]
    ac = col[:, None] * inv_freq[None]
    cos = np.concatenate([np.cos(ar), np.cos(ar), np.cos(ac), np.cos(ac)], axis=1)
    sin = np.concatenate([-np.sin(ar), np.sin(ar), -np.sin(ac), np.sin(ac)], axis=1)
    return jnp.asarray(cos.astype(np.float32)), jnp.asarray(sin.astype(np.float32))


def _ret_kernel(lgf_ref, lgb_ref, *refs, t, tq, layer, rotary, state_in, state_out):
    refs = list(refs)
    q_ref, k_ref, v_ref, g_ref = refs[:4]
    refs = refs[4:]
    if rotary:
        cos_ref, sin_ref = refs[:2]
        refs = refs[2:]
    if state_in:
        sf0_ref, sb0_ref = refs[:2]
        refs = refs[2:]
    o_ref = refs[0]
    refs = refs[1:]
    if state_out:
        sf_ref, sb_ref = refs[:2]
        refs = refs[2:]
    dec_ref, kb_ref = refs

    scale = RET_KEY_DIM ** -0.5
    nq = t // tq
    heads_here = dec_ref.shape[0]

    if rotary:
        lane = lax.broadcasted_iota(I32, (1, LANES), 1)
        low = (lane % (RET_KEY_DIM // 2)) < (RET_KEY_DIM // 4)

        def rot(x, rows):
            swapped = jnp.where(low, pltpu.roll(x, LANES - RET_KEY_DIM // 4, 1), pltpu.roll(x, RET_KEY_DIM // 4, 1))
            return x * cos_ref[rows, :] + swapped * sin_ref[rows, :]
    else:
        def rot(x, rows):
            return x

    def one_head(hh):
        h = pl.program_id(0) * heads_here + hh
        cols = slice(LANES * hh, LANES * (hh + 1))
        lgf = lgf_ref[layer * RET_HEADS + h]
        lgb = lgb_ref[layer * RET_HEADS + h]

        @pl.when(pl.program_id(1) == 0)
        def _():
            def fill(ri, c):
                rows = pl.ds(pl.multiple_of(ri * tq, tq), tq)
                i = lax.broadcasted_iota(I32, (tq, t), 0) + ri * tq
                j = lax.broadcasted_iota(I32, (tq, t), 1)
                d = (i - j).astype(F32)
                m = jnp.exp(jnp.abs(d) * jnp.where(d > 0, lgf, lgb))
                dec_ref[hh, rows, :] = jnp.where(d == 0, 2.0, m)
                return c

            lax.fori_loop(0, nq, fill, 0)

        kr = rot(k_ref[:, cols].astype(F32), slice(0, t))
        kb_ref[hh] = kr.astype(BF16)
        vb = v_ref[:, cols]

        if state_out:
            j = lax.broadcasted_iota(I32, (t, 1), 0).astype(F32)
            tn = (((0,), (0,)), ((), ()))
            kf = (kr * (scale * jnp.exp(lgf * (t - 1.0 - j)))).astype(BF16)
            kbw = (kr * (scale * jnp.exp(lgb * j))).astype(BF16)
            sf = lax.dot_general(kf, vb, tn, preferred_element_type=F32)
            sb = lax.dot_general(kbw, vb, tn, preferred_element_type=F32)
            if state_in:
                sf = sf + jnp.exp(lgf * t) * sf0_ref[hh]
                sb = sb + jnp.exp(lgb * t) * sb0_ref[hh]
            sf_ref[hh] = sf
            sb_ref[hh] = sb

        def q_tile(qi, carry):
            r0 = pl.multiple_of(qi * tq, tq)
            rows = pl.ds(r0, tq)
            qr = rot(q_ref[rows, cols].astype(F32), rows)
            s = lax.dot_general((qr * scale).astype(BF16), kb_ref[hh], (((1,), (1,)), ((), ())),
                                preferred_element_type=F32)
            y = jnp.dot((s * dec_ref[hh, rows, :]).astype(BF16), vb, preferred_element_type=F32)
            if state_in:
                pos = (lax.broadcasted_iota(I32, (tq, 1), 0) + r0).astype(F32)
                qf = (qr * jnp.exp(lgf * (pos + 1.0))).astype(BF16)
                qb = (qr * jnp.exp(lgb * (t - pos))).astype(BF16)
                y = (y + jnp.dot(qf, sf0_ref[hh].astype(BF16), preferred_element_type=F32)
                     + jnp.dot(qb, sb0_ref[hh].astype(BF16), preferred_element_type=F32))
            mean = jnp.mean(y, axis=-1, keepdims=True)
            yc = y - mean
            yn = yc * lax.rsqrt(jnp.mean(yc * yc, axis=-1, keepdims=True) + EPS)
            g = g_ref[rows, cols].astype(F32)
            o_ref[rows, cols] = (g * jax.nn.sigmoid(g) * yn).astype(o_ref.dtype)
            return carry

        lax.fori_loop(0, nq, q_tile, 0)

    for hh in range(heads_here):
        one_head(hh)


def _retention(z_ret, lg_f, lg_b, layer, row0, n_seq, seq, *, rotary, states=None, state_out=False):
    rb = row0 // seq
    tq = min(seq, ATTN_Q_TILE)
    state_in = states is not None
    hps = RET_HEADS if seq * seq * RET_HEADS * 4 <= RET_DECAY_BYTES else 1
    cb = BRANCH_WIDTH // (hps * LANES)
    width = hps * LANES
    in_specs = [pl.BlockSpec((seq, width), lambda h, b, *_: (rb + b, 0 * cb + h)),
                pl.BlockSpec((seq, width), lambda h, b, *_: (rb + b, 1 * cb + h)),
                pl.BlockSpec((seq, width), lambda h, b, *_: (rb + b, 2 * cb + h)),
                pl.BlockSpec((seq, width), lambda h, b, *_: (rb + b, 3 * cb + h))]
    args = [z_ret, z_ret, z_ret, z_ret]
    if rotary:
        cos, sin = _rotary_tables(seq)
        in_specs += [pl.BlockSpec((seq, LANES), lambda h, b, *_: (0, 0))] * 2
        args += [cos, sin]
    if state_in:
        st_spec = pl.BlockSpec((None, None, hps, RET_KEY_DIM, RET_KEY_DIM), lambda h, b, *_: (b, layer, h, 0, 0))
        in_specs += [st_spec, st_spec]
        args += list(states)
    out_specs = [pl.BlockSpec((seq, width), lambda h, b, *_: (b, h))]
    out_shape = [jax.ShapeDtypeStruct((n_seq * seq, RET_HEADS * LANES), BF16)]
    if state_out:
        so = pl.BlockSpec((None, hps, RET_KEY_DIM, RET_KEY_DIM), lambda h, b, *_: (b, h, 0, 0))
        out_specs += [so, so]
        out_shape += [jax.ShapeDtypeStruct((n_seq, RET_HEADS, RET_KEY_DIM, RET_KEY_DIM), F32)] * 2
    return pl.pallas_call(
        functools.partial(_ret_kernel, t=seq, tq=tq, layer=layer, rotary=rotary, state_in=state_in,
                          state_out=state_out),
        grid_spec=pltpu.PrefetchScalarGridSpec(
            num_scalar_prefetch=2, grid=(RET_HEADS // hps, n_seq),
            in_specs=in_specs, out_specs=out_specs,
            scratch_shapes=[pltpu.VMEM((hps, seq, seq), F32), pltpu.VMEM((hps, seq, LANES), BF16)]),
        out_shape=out_shape,
        compiler_params=_params(2),
        name="retention",
    )(lg_f, lg_b, *args)


def _split_dot_nt(w, x):
    nt = (((1,), (1,)), ((), ()))
    w_hi = w.astype(BF16)
    w_lo = (w - w_hi.astype(F32)).astype(BF16)
    x_hi = x.astype(BF16)
    x_lo = (x - x_hi.astype(F32)).astype(BF16)
    return (lax.dot_general(w_hi, x_hi, nt, preferred_element_type=F32)
            + lax.dot_general(w_hi, x_lo, nt, preferred_element_type=F32)
            + lax.dot_general(w_lo, x_hi, nt, preferred_element_type=F32))


def _finish_kernel(tmod_ref, *refs, ctx_tiles, n_x):
    del tmod_ref
    branch_refs, refs = refs[:6], refs[6:]
    zg_ref, refs = refs[0], refs[1:]
    x_refs, refs = refs[:n_x], refs[n_x:]
    (mod_ref, gpost_ref, gpre_ref, wb_ref, wo_ref, wrt_ref, br_ref, tri_ref, ltri_ref,
     x1_ref, h2_ref, wts_ref, lpos_ref, seg_ref, off_ref) = refs
    d = D_MODEL

    def branch(j):
        return _load_rows(branch_refs[2 * j:2 * j + 2], ctx_tiles)

    def gate(j):
        return jax.nn.sigmoid(zg_ref[:, d * j:d * (j + 1)].astype(F32))

    merged = (gate(0) * jnp.dot(branch(0), wb_ref[0], preferred_element_type=F32)
              + gate(1) * jnp.dot(branch(1), wb_ref[1], preferred_element_type=F32)
              + gate(2) * jnp.dot(branch(2), wb_ref[2], preferred_element_type=F32))
    y = jnp.dot(merged.astype(BF16), wo_ref[...], preferred_element_type=F32)
    x1 = _load_rows(x_refs, ctx_tiles) + mod_ref[2:3, :] * (y * _rms(y) * gpost_ref[...])
    x1_ref[...] = x1
    h2 = x1 * _rms(x1) * gpre_ref[...] * (1.0 + mod_ref[4:5, :]) + mod_ref[3:4, :]
    h2_ref[...] = h2

    logits = _split_dot_nt(wrt_ref[...], h2) + br_ref[:, 0:1]
    tm = logits.shape[1]
    eidx = lax.broadcasted_iota(I32, (N_EXPERTS, tm), 0)
    cur = logits
    vals, hots = [], []
    for k in range(TOP_K):
        m = jnp.max(cur, axis=0, keepdims=True)
        sel = jnp.min(jnp.where(cur == m, eidx, N_EXPERTS), axis=0, keepdims=True)
        hot = eidx == sel
        vals.append(m)
        hots.append(hot)
        cur = jnp.where(hot, -jnp.inf, cur)
    exps = [jnp.exp(v - vals[0]) for v in vals]
    den = exps[0] + exps[1] + exps[2] + exps[3]
    for k in range(TOP_K):
        wts_ref[k:k + 1, :] = exps[k] / den

    member = jnp.logical_or(jnp.logical_or(hots[0], hots[1]), jnp.logical_or(hots[2], hots[3]))
    member_f = member.astype(F32)
    before = jnp.dot(member_f.astype(BF16), tri_ref[...], preferred_element_type=F32)
    units = jnp.ceil(jnp.sum(member_f, axis=1, keepdims=True) * (1.0 / SEG_ALIGN))
    units = jnp.broadcast_to(units, seg_ref.shape)
    off = jnp.dot(ltri_ref[...], units.astype(BF16), preferred_element_type=F32) * SEG_ALIGN
    seg_ref[...] = units * SEG_ALIGN
    off_ref[...] = off
    place = before + off[:, 0:1]
    for k in range(TOP_K):
        lpos_ref[k:k + 1, :] = jnp.sum(jnp.where(hots[k], place, 0.0), axis=0, keepdims=True).astype(I32)


def _finish(branches, z_gate, x, mod, g_post, g_pre_ffn, w_branch, w_out, w_router_t, b_router, layer, tile_mod):
    tm = FIN_TILE
    x_specs, x_args, _, n = _token_rows(x, tm)
    d = x_args[0].shape[1]
    tri = jnp.asarray(np.triu(np.ones((tm, tm), np.float32), k=1)).astype(BF16)
    ltri = jnp.asarray(np.tril(np.ones((N_EXPERTS, N_EXPERTS), np.float32), k=-1)).astype(BF16)
    row = lambda i, t: (i, 0)
    const2 = lambda i, t: (0, 0)
    lay3 = lambda i, t: (layer, 0, 0)
    col = lambda i, t: (0, i)
    branch_specs, branch_args = [], []
    for pair in branches:
        specs, args, ctx_tiles, _ = _token_rows(pair, tm)
        branch_specs += specs
        branch_args += args
    outs = pl.pallas_call(
        functools.partial(_finish_kernel, ctx_tiles=ctx_tiles, n_x=len(x_args)),
        grid_spec=pltpu.PrefetchScalarGridSpec(
            num_scalar_prefetch=1, grid=(n // tm,),
            in_specs=branch_specs + [pl.BlockSpec((tm, GATE_W), row)] + x_specs + [
                      pl.BlockSpec((None, None, 6, d), lambda i, t: (layer, t[i], 0, 0)),
                      pl.BlockSpec((None, 1, d), lay3),
                      pl.BlockSpec((None, 1, d), lay3),
                      pl.BlockSpec((None, 3, BRANCH_WIDTH, d), lambda i, t: (layer, 0, 0, 0)),
                      pl.BlockSpec((None, d, d), lay3),
                      pl.BlockSpec((None, N_EXPERTS, d), lay3),
                      pl.BlockSpec((None, N_EXPERTS, LANES), lay3),
                      pl.BlockSpec((tm, tm), const2),
                      pl.BlockSpec((N_EXPERTS, N_EXPERTS), const2)],
            out_specs=[pl.BlockSpec((tm, d), row),
                       pl.BlockSpec((tm, d), row),
                       pl.BlockSpec((TOP_K, tm), col),
                       pl.BlockSpec((TOP_K, tm), col),
                       pl.BlockSpec((None, N_EXPERTS, LANES), lambda i, t: (i, 0, 0)),
                       pl.BlockSpec((None, N_EXPERTS, LANES), lambda i, t: (i, 0, 0))]),
        out_shape=[jax.ShapeDtypeStruct((n, d), F32),
                   jax.ShapeDtypeStruct((n, d), F32),
                   jax.ShapeDtypeStruct((TOP_K, n), F32),
                   jax.ShapeDtypeStruct((TOP_K, n), I32),
                   jax.ShapeDtypeStruct((n // tm, N_EXPERTS, LANES), F32),
                   jax.ShapeDtypeStruct((n // tm, N_EXPERTS, LANES), F32)],
        compiler_params=_params(1),
        name="merge_router",
    )(tile_mod, *branch_args, z_gate, *x_args, mod, g_post, g_pre_ffn, w_branch, w_out, w_router_t, b_router,
      tri, ltri)
    return outs


def _segment_chunks(length, src_ref, src0, dst_ref, dst0, sem, max_chunk, fixed_src=False):
    out = []
    chunk = max_chunk
    while chunk >= SEG_ALIGN:
        done = jnp.bitwise_and(length, ~(2 * chunk - 1))
        present = jnp.bitwise_and(length, chunk) != 0
        s = 0 if fixed_src else pl.multiple_of(src0 + done, SEG_ALIGN)
        dd = pl.multiple_of(dst0 + done, SEG_ALIGN)
        out.append((present, pltpu.make_async_copy(src_ref.at[pl.ds(s, chunk)], dst_ref.at[pl.ds(dd, chunk)], sem)))
        chunk //= 2
    return out


def _for_each_chunk(n_segments, chunks_of, action):
    def body(e, c):
        for present, cp in chunks_of(e):
            pl.when(present)(functools.partial(action, cp))
        return c

    lax.fori_loop(0, n_segments, body, 0, unroll=4)


def _start(cp):
    cp.start()


def _wait(cp):
    cp.wait()


def _wait_rows(total, src_ref, dst_ref, sem):
    chunk = pl.next_power_of_2(SORT_ROWS) // 2
    while chunk >= SEG_ALIGN:
        @pl.when(jnp.bitwise_and(total, chunk) != 0)
        def _(chunk=chunk):
            pltpu.make_async_copy(src_ref.at[pl.ds(0, chunk)], dst_ref.at[pl.ds(0, chunk)], sem).wait()
        chunk //= 2


def _scatter_kernel(seg_ref, off_ref, pos_ref, used_ref, tpos_ref, tlen_ref, h_ref, lpos_ref, xe_hbm,
                    buf_ref, zero_ref, sem, *, tm):
    t = pl.program_id(0)
    slot = t % 2
    hb = h_ref[...].astype(BF16)
    lp = [lpos_ref[k:k + 1, :] for k in range(TOP_K)]
    rows = buf_ref.shape[1]
    for c0 in range(0, rows, SORT_CHUNK):
        r = lax.broadcasted_iota(I32, (SORT_CHUNK, tm), 0) + c0
        hit = jnp.logical_or(jnp.logical_or(r == lp[0], r == lp[1]), jnp.logical_or(r == lp[2], r == lp[3]))
        onehot = jnp.where(hit, 1.0, 0.0).astype(BF16)
        buf_ref[slot, c0:c0 + SORT_CHUNK, :] = jnp.dot(onehot, hb, preferred_element_type=F32)

    def segments_of(tile):
        def segment(e):
            j = tile * N_EXPERTS + e
            return _segment_chunks(seg_ref[j], buf_ref.at[tile % 2], off_ref[j], xe_hbm, pos_ref[j],
                                   sem.at[tile % 2], tm)
        return segment

    @pl.when(t > 0)
    def _():
        _wait_rows(used_ref[t - 1], buf_ref.at[1 - slot], xe_hbm, sem.at[1 - slot])

    _for_each_chunk(N_EXPERTS, segments_of(t), _start)

    @pl.when(t == pl.num_programs(0) - 1)
    def _():
        _wait_rows(used_ref[t], buf_ref.at[slot], xe_hbm, sem.at[slot])
        zero_ref[...] = jnp.zeros_like(zero_ref)

        def tail(e):
            return _segment_chunks(tlen_ref[e], zero_ref, 0, xe_hbm, tpos_ref[e], sem.at[0], zero_ref.shape[0],
                                   fixed_src=True)

        _for_each_chunk(N_EXPERTS, tail, _start)
        _for_each_chunk(N_EXPERTS, tail, _wait)


def _scatter_rows(h2, lpos, plan, n_rows):
    n, d = h2.shape
    tm = FIN_TILE
    return pl.pallas_call(
        functools.partial(_scatter_kernel, tm=tm),
        grid_spec=pltpu.PrefetchScalarGridSpec(
            num_scalar_prefetch=6, grid=(n // tm,),
            in_specs=[pl.BlockSpec((tm, d), lambda i, *_: (i, 0)),
                      pl.BlockSpec((TOP_K, tm), lambda i, *_: (0, i))],
            out_specs=pl.BlockSpec(memory_space=pl.ANY),
            scratch_shapes=[pltpu.VMEM((2, SORT_ROWS, d), F32), pltpu.VMEM((MOE_BLOCK // 2, d), F32),
                            pltpu.SemaphoreType.DMA((2,))]),
        out_shape=jax.ShapeDtypeStruct((n_rows, d), F32),
        compiler_params=_params(1),
        name="moe_scatter",
    )(plan["seg"], plan["off"], plan["pos"], plan["used"], plan["tail_pos"], plan["tail_len"], h2, lpos)


def _expert_kernel(be_ref, nxt_ref, valid_ref, nu_ref, xb_ref, w1_hbm, b1_ref, w2_hbm, b2_ref, yb_ref,
                   w1s_ref, w2s_ref, w1b_ref, w2b_ref, sem, *, layer):
    i = pl.program_id(0)
    used = i < nu_ref[0]
    fresh = jnp.logical_or(i == 0, be_ref[i] != be_ref[jnp.maximum(i - 1, 0)])

    def fetch(e):
        return (pltpu.make_async_copy(w1_hbm.at[layer, e], w1s_ref, sem.at[0]),
                pltpu.make_async_copy(w2_hbm.at[layer, e], w2s_ref, sem.at[1]))

    @pl.when(i == 0)
    def _():
        for cp in fetch(be_ref[0]):
            cp.start()

    @pl.when(jnp.logical_and(used, fresh))
    def _():
        for cp in fetch(be_ref[i]):
            cp.wait()
        w1b_ref[...] = w1s_ref[...].astype(BF16)
        w2b_ref[...] = w2s_ref[...].astype(BF16)

        @pl.when(nxt_ref[i] >= 0)
        def _():
            for cp in fetch(nxt_ref[i]):
                cp.start()

    def ffn(m):
        e = be_ref[i]
        z = jnp.dot(xb_ref[0:m, :].astype(BF16), w1b_ref[...], preferred_element_type=F32) + b1_ref[e]
        glu = jnp.minimum(z[:, :D_FF], SWIGLU_LIMIT)
        lin = jnp.clip(z[:, D_FF:], -SWIGLU_LIMIT, SWIGLU_LIMIT)
        act = glu * jax.nn.sigmoid(SWIGLU_ALPHA * glu) * (lin + 1.0)
        yb_ref[0:m, :] = jnp.dot(act.astype(BF16), w2b_ref[...], preferred_element_type=F32) + b2_ref[e]
        if m < yb_ref.shape[0]:
            yb_ref[m:, :] = jnp.zeros((yb_ref.shape[0] - m, yb_ref.shape[1]), F32)

    valid = valid_ref[i]
    for m in range(EXPERT_ROW_STEP, yb_ref.shape[0] + 1, EXPERT_ROW_STEP):
        pl.when(jnp.logical_and(valid > m - EXPERT_ROW_STEP, valid <= m))(functools.partial(ffn, m))

    @pl.when(valid == 0)
    def _():
        yb_ref[...] = jnp.zeros_like(yb_ref)


def _experts(xb, block_expert, next_expert, valid_rows, n_used, w1, b1, w2, b2, layer):
    n_rows, d = xb.shape
    tm = MOE_BLOCK
    return pl.pallas_call(
        functools.partial(_expert_kernel, layer=layer),
        grid_spec=pltpu.PrefetchScalarGridSpec(
            num_scalar_prefetch=4, grid=(n_rows // tm,),
            in_specs=[pl.BlockSpec((tm, d), lambda i, be, nx, vr, nu: (jnp.minimum(i, nu[0] - 1), 0)),
                      pl.BlockSpec(memory_space=pl.ANY),
                      pl.BlockSpec((None, N_EXPERTS, 1, 2 * D_FF), lambda i, be, nx, vr, nu: (layer, 0, 0, 0)),
                      pl.BlockSpec(memory_space=pl.ANY),
                      pl.BlockSpec((None, N_EXPERTS, 1, d), lambda i, be, nx, vr, nu: (layer, 0, 0, 0))],
            out_specs=pl.BlockSpec((tm, d), lambda i, be, nx, vr, nu: (i, 0)),
            scratch_shapes=[pltpu.VMEM((d, 2 * D_FF), F32), pltpu.VMEM((D_FF, d), F32),
                            pltpu.VMEM((d, 2 * D_FF), BF16), pltpu.VMEM((D_FF, d), BF16),
                            pltpu.SemaphoreType.DMA((2,))]),
        out_shape=jax.ShapeDtypeStruct((n_rows, d), F32),
        compiler_params=_params(1),
        name="moe_experts",
    )(block_expert, next_expert, valid_rows, n_used, xb, w1, b1, w2, b2)


def _gather_kernel(tmod_ref, seg_ref, off_ref, pos_ref, used_ref, ye_hbm, lpos_ref, wts_ref, x1_ref, mod_ref, g_ref,
                   *refs, tm, tile0, prenorm_next):
    del tmod_ref
    if prenorm_next:
        modn_ref, gn_ref, o_ref, hn_ref, buf_ref, sem = refs
    else:
        o_ref, buf_ref, sem = refs
    step = pl.program_id(0)
    t = step + tile0
    slot = step % 2

    def segments_of(tile):
        def segment(e):
            j = tile * N_EXPERTS + e
            half = (tile - tile0) % 2
            return _segment_chunks(seg_ref[j], ye_hbm, pos_ref[j], buf_ref.at[half], off_ref[j], sem.at[half], tm)
        return segment

    @pl.when(step == 0)
    def _():
        buf_ref[...] = jnp.zeros_like(buf_ref)
        _for_each_chunk(N_EXPERTS, segments_of(t), _start)

    @pl.when(step + 1 < pl.num_programs(0))
    def _():
        _for_each_chunk(N_EXPERTS, segments_of(t + 1), _start)

    _wait_rows(used_ref[t], ye_hbm, buf_ref.at[slot], sem.at[slot])

    lp = [lpos_ref[:, k:k + 1] for k in range(TOP_K)]
    wt = [wts_ref[:, k:k + 1] for k in range(TOP_K)]
    y = jnp.zeros(o_ref.shape, F32)
    for c0 in range(0, buf_ref.shape[1], SORT_CHUNK):
        r = lax.broadcasted_iota(I32, (tm, SORT_CHUNK), 1) + c0
        wm = jnp.where(r == lp[0], wt[0], 0.0)
        for k in range(1, TOP_K):
            wm = jnp.where(r == lp[k], wt[k], wm)
        y = y + jnp.dot(wm.astype(BF16), buf_ref[slot, c0:c0 + SORT_CHUNK, :].astype(BF16),
                        preferred_element_type=F32)
    x2 = x1_ref[...] + mod_ref[5:6, :] * (y * _rms(y) * g_ref[...])
    o_ref[...] = x2
    if prenorm_next:
        hn = x2 * _rms(x2) * gn_ref[...]
        hn_ref[...] = (hn * (1.0 + modn_ref[1:2, :]) + modn_ref[0:1, :]).astype(hn_ref.dtype)


def _gather_combine(ye, lpos_t, wts_t, plan, x1, mod, g_post, layer, tile_mod, row0=0, rows=None, g_pre_next=None):
    n, d = x1.shape
    tm = FIN_TILE
    rows = n if rows is None else rows
    tile0 = row0 // tm
    row = lambda i, *_: (tile0 + i, 0)
    out_row = lambda i, *_: (i, 0)
    mod_spec = lambda lay: pl.BlockSpec((None, None, 6, d), lambda i, t, *_: (lay, t[tile0 + i], 0, 0))
    gain_spec = lambda lay: pl.BlockSpec((None, 1, d), lambda i, *_: (lay, 0, 0))
    prenorm_next = g_pre_next is not None
    in_specs = [pl.BlockSpec(memory_space=pl.ANY),
                pl.BlockSpec((tm, TOP_K), row),
                pl.BlockSpec((tm, TOP_K), row),
                pl.BlockSpec((tm, d), row),
                mod_spec(layer), gain_spec(layer)]
    args = [ye, lpos_t, wts_t, x1, mod, g_post]
    out_specs = [pl.BlockSpec((tm, d), out_row)]
    out_shape = [jax.ShapeDtypeStruct((rows, d), F32)]
    if prenorm_next:
        in_specs += [mod_spec(layer + 1), gain_spec(layer + 1)]
        args += [mod, g_pre_next]
        out_specs += [pl.BlockSpec((tm, d), out_row)]
        out_shape += [jax.ShapeDtypeStruct((rows, d), BF16)]
    outs = pl.pallas_call(
        functools.partial(_gather_kernel, tm=tm, tile0=tile0, prenorm_next=prenorm_next),
        grid_spec=pltpu.PrefetchScalarGridSpec(
            num_scalar_prefetch=5, grid=(rows // tm,),
            in_specs=in_specs, out_specs=out_specs,
            scratch_shapes=[pltpu.VMEM((2, SORT_ROWS, d), F32), pltpu.SemaphoreType.DMA((2,))]),
        out_shape=out_shape,
        compiler_params=_params(1),
        name="moe_gather",
    )(tile_mod, plan["seg"], plan["off"], plan["pos"], plan["used"], *args)
    return outs if prenorm_next else outs[0]


def _moe(h2, wts, lpos, seg, off, x1, mod, g_post, w1, b1, w2, b2, layer, tile_mod, split_rows=None,
         g_pre_next=None):
    n, d = h2.shape
    blk = MOE_BLOCK
    tm = FIN_TILE
    tiles = n // tm
    n_rows = -(-(n * TOP_K + tiles * N_EXPERTS * (SEG_ALIGN - 1) + N_EXPERTS * (blk - 1)) // blk) * blk
    n_blocks = n_rows // blk
    seg = seg[:, :, 0].astype(I32)
    off = off[:, :, 0].astype(I32)
    rows_e = jnp.sum(seg, axis=0)
    region = (rows_e + blk - 1) // blk * blk
    pend = jnp.cumsum(region)
    pstart = pend - region
    pos = pstart[None, :] + jnp.cumsum(seg, axis=0) - seg
    plan = {"seg": seg.reshape(-1), "off": off.reshape(-1), "pos": pos.reshape(-1).astype(I32),
            "used": jnp.sum(seg, axis=1).astype(I32),
            "tail_pos": (pstart + rows_e).astype(I32), "tail_len": (region - rows_e).astype(I32)}
    blocks = jnp.arange(n_blocks, dtype=I32) * blk
    block_expert = jnp.minimum(jnp.sum(blocks[:, None] >= pend[None, :], axis=1), N_EXPERTS - 1).astype(I32)
    n_used = (pend[-1:] // blk).astype(I32)
    valid_rows = jnp.clip((pstart + rows_e)[block_expert] - blocks, 0, blk).astype(I32)
    ids = jnp.arange(n_blocks, dtype=I32)
    run_start = jnp.logical_and(jnp.concatenate([jnp.ones((1,), bool), block_expert[1:] != block_expert[:-1]]),
                                ids < n_used[0])
    first_after = lax.cummin(jnp.where(run_start, ids, n_blocks)[::-1])[::-1]
    first_after = jnp.concatenate([first_after[1:], jnp.full((1,), n_blocks, I32)])
    next_expert = jnp.where(first_after < n_blocks, block_expert[jnp.minimum(first_after, n_blocks - 1)], -1)
    xe = _scatter_rows(h2, lpos, plan, n_rows)
    ye = _experts(xe, block_expert, next_expert.astype(I32), valid_rows, n_used, w1, b1, w2, b2, layer)
    combine = functools.partial(_gather_combine, ye, lpos.T, wts.T, plan, x1, mod, g_post, layer, tile_mod)
    if split_rows is None:
        return combine(g_pre_next=g_pre_next)
    return combine(row0=0, rows=split_rows), combine(row0=split_rows, rows=n - split_rows)


def _tile_mod_ids(n_ctx_rows, n_lat_rows, lat_seq, tm):
    ctx = np.zeros((n_ctx_rows // tm,), np.int32)
    lat = 1 + (np.arange(n_lat_rows // tm) * tm) // lat_seq
    return jnp.asarray(np.concatenate([ctx, lat.astype(np.int32)]))


def kernel(x_prompt, x_sample, cache_k, cache_v, state_ret_fwd, state_ret_bwd, c, c_ctx, w_mod, b_mod, g_pre_mix, g_post_mix, g_pre_ffn, g_post_ffn, w_in, na_rel_bias, ret_decay_fwd, ret_decay_bwd, w_branch, w_out, w_router, b_router, w_exp_in, b_exp_in, w_exp_out, b_exp_out):
    batch, seq, d = x_prompt.shape
    dec_batch, dec_seq, _ = x_sample.shape
    depth = w_in.shape[0]
    n_ctx = batch * seq
    n_lat = dec_batch * dec_seq
    assert 1 + dec_batch <= MOD_ROWS

    x = (x_prompt.reshape(n_ctx, d), x_sample.reshape(n_lat, d))
    cvec =jnp.concatenate([c_ctx[None], c, jnp.zeros((MOD_ROWS - 1 - dec_batch, d), F32)], axis=0)
    mod_all = _modulation(cvec, w_mod, b_mod).reshape(depth, MOD_ROWS, 6, d)
    tmod = {tm: _tile_mod_ids(n_ctx, n_lat, dec_seq, tm) for tm in (ROW_TILE, FIN_TILE)}
    past = cache_k.shape[2]
    ck = cache_k.reshape(dec_batch, depth, past, NA_WIDTH)
    cv = cache_v.reshape(dec_batch, depth, past, NA_WIDTH)
    lg_f = jax.nn.log_sigmoid(ret_decay_fwd.astype(F32)).reshape(-1)
    lg_b = jax.nn.log_sigmoid(ret_decay_bwd.astype(F32)).reshape(-1)
    bias_all = _neighbourhood_bias(na_rel_bias, dec_seq)
    vec = lambda g: g.reshape(depth, 1, d)
    g_pre_mix, g_post_mix, g_pre_ffn, g_post_ffn = vec(g_pre_mix), vec(g_post_mix), vec(g_pre_ffn), vec(g_post_ffn)
    w_branch_b = w_branch.astype(BF16)
    w_out_b = w_out.astype(BF16)
    w_router_t = jnp.swapaxes(w_router, 1, 2)
    b_router_l = jnp.broadcast_to(b_router[:, :, None], (depth, N_EXPERTS, LANES))
    b_exp_in = b_exp_in.reshape(depth, N_EXPERTS, 1, 2 * D_FF)
    b_exp_out = b_exp_out.reshape(depth, N_EXPERTS, 1, d)

    ks, vs, sfs, sbs = [], [], [], []
    h = _prenorm(x, g_pre_mix, mod_all, 0, tmod[ROW_TILE], ROW_TILE)
    for l in range(depth):
        last = l == depth - 1
        z_qkvu = _project(h, w_in, l, 0, QKVU_W, F32)
        z_ret = _project(h, w_in, l, QKVU_W, RET_W, BF16)
        z_gate = _project(h, w_in, l, QKVU_W + RET_W, GATE_W, BF16)
        ks.append(z_qkvu[:n_ctx, NA_WIDTH:2 * NA_WIDTH].reshape(batch, seq, NA_HEADS, NA_HEAD_DIM))
        vs.append(z_qkvu[:n_ctx, 2 * NA_WIDTH:3 * NA_WIDTH].reshape(batch, seq, NA_HEADS, NA_HEAD_DIM))

        a_pair = (_attention_ctx(z_qkvu, batch, seq),
                  _attention_lat(z_qkvu, ck, cv, bias_all, l, n_ctx, dec_batch, dec_seq))
        f_pair = (_fourier(z_qkvu, 0, batch, seq), _fourier(z_qkvu, n_ctx, dec_batch, dec_seq))
        r_ctx, s_f, s_b = _retention(z_ret, lg_f, lg_b, l, 0, batch, seq, rotary=False, state_out=True)
        (r_lat,) = _retention(z_ret, lg_f, lg_b, l, n_ctx, dec_batch, dec_seq, rotary=True,
                              states=(state_ret_fwd, state_ret_bwd))
        sfs.append(s_f)
        sbs.append(s_b)

        x1, h2, wts, lpos, seg, off = _finish(
            (a_pair, f_pair, (r_ctx, r_lat)), z_gate, x, mod_all, g_post_mix, g_pre_ffn,
            w_branch_b, w_out_b, w_router_t, b_router_l, l, tmod[FIN_TILE])
        out = _moe(h2, wts, lpos, seg, off, x1, mod_all, g_post_ffn,
                   w_exp_in, b_exp_in, w_exp_out, b_exp_out, l, tmod[FIN_TILE],
                   split_rows=n_ctx if last else None, g_pre_next=None if last else g_pre_mix)
        x, h = (out, None) if last else out

    y_prompt = x[0].reshape(batch, seq, d)
    y_sample = x[1].reshape(dec_batch, dec_seq, d)
    return (y_prompt, y_sample, jnp.stack(ks, axis=1), jnp.stack(vs, axis=1),
            jnp.stack(sfs, axis=1), jnp.stack(sbs, axis=1))
```

```python
import functools

import numpy as np
import jax
import jax.numpy as jnp
from jax import lax
from jax.experimental import pallas as pl
from jax.experimental.pallas import tpu as pltpu

F32 = jnp.float32
BF16 = jnp.bfloat16
I32 = jnp.int32

D_MODEL = 1024
GRID_W = 64
NA_HEADS = 8
NA_HEAD_DIM = 64
NA_WIDTH = NA_HEADS * NA_HEAD_DIM
WIN_H = 8
WIN_W = 16
KEY_SLAB_ROWS = 12
FOURIER_GROUPS = 4
FOURIER_GROUP_DIM = 128
RET_HEADS = 4
RET_KEY_DIM = 128
ROPE_BASE = 10000.0
BRANCH_WIDTH = 512
N_EXPERTS = 32
TOP_K = 4
D_FF = 1024
SWIGLU_LIMIT = 7.0
SWIGLU_ALPHA = 1.702
EPS = 1e-6
NEG_INF = -1e30

QKVU_W = 4 * NA_WIDTH
RET_W = 4 * BRANCH_WIDTH
GATE_W = 3 * D_MODEL
PROJ_TILE = 1024
PROJ_ROWS = 2048
ATTN_Q_TILE = 512
RET_Q_TILE = 1024
RET_DECAY_BYTES = 4 * 1024 * 1024

LANES = 128
MOD_ROWS = 16
ROW_TILE = 1024
FIN_TILE = 512
MOE_BLOCK = 512
EXPERT_ROW_STEP = 128
SEG_ALIGN = 8
SORT_CHUNK = 256
SORT_ROWS = -(-(FIN_TILE * TOP_K + N_EXPERTS * (SEG_ALIGN - 1)) // SORT_CHUNK) * SORT_CHUNK
VMEM_LIMIT = 56 * 1024 * 1024


def _params(n_axes, vmem=VMEM_LIMIT):
    return pltpu.CompilerParams(dimension_semantics=("arbitrary",) * n_axes, vmem_limit_bytes=vmem)


def _rms(x):
    return lax.rsqrt(jnp.mean(x * x, axis=-1, keepdims=True) + EPS)


def _mod_kernel(cv_ref, w_ref, b_ref, o_ref):
    cv = cv_ref[...]
    s = (cv * jax.nn.sigmoid(cv)).astype(BF16)
    o_ref[...] = jnp.dot(s, w_ref[...].astype(BF16), preferred_element_type=F32) + b_ref[...]


def _modulation(cv, w_mod, b_mod):
    depth, d, n = w_mod.shape
    tn = 1536
    return pl.pallas_call(
        _mod_kernel,
        grid=(depth, n // tn),
        in_specs=[pl.BlockSpec((MOD_ROWS, d), lambda l, j: (0, 0)),
                  pl.BlockSpec((None, d, tn), lambda l, j: (l, 0, j)),
                  pl.BlockSpec((None, 1, tn), lambda l, j: (l, 0, j))],
        out_specs=pl.BlockSpec((None, MOD_ROWS, tn), lambda l, j: (l, 0, j)),
        out_shape=jax.ShapeDtypeStruct((depth, MOD_ROWS, n), F32),
        compiler_params=_params(2),
        name="modulation",
    )(cv, w_mod, b_mod.reshape(depth, 1, n))


def _token_rows(x, tm):
    if isinstance(x, tuple):
        d = x[0].shape[1]
        ct = x[0].shape[0] // tm
        specs = [pl.BlockSpec((tm, d), lambda i, *_: (jnp.minimum(i, ct - 1), 0)),
                 pl.BlockSpec((tm, d), lambda i, *_: (jnp.maximum(i - ct, 0), 0))]
        return specs, list(x), ct, x[0].shape[0] + x[1].shape[0]
    return [pl.BlockSpec((tm, x.shape[1]), lambda i, *_: (i, 0))], [x], None, x.shape[0]


def _load_rows(refs, ctx_tiles):
    if len(refs) == 1:
        return refs[0][...]
    return jnp.where(pl.program_id(0) < ctx_tiles, refs[0][...], refs[1][...])


def _prenorm_kernel(tmod_ref, *refs, ctx_tiles):
    del tmod_ref
    g_ref, mod_ref, o_ref = refs[-3:]
    x = _load_rows(refs[:-3], ctx_tiles)
    h = x * _rms(x) * g_ref[...]
    o_ref[...] = (h * (1.0 + mod_ref[1:2, :]) + mod_ref[0:1, :]).astype(o_ref.dtype)


def _prenorm(x, g, mod, layer, tile_mod, tm):
    x_specs, x_args, ctx_tiles, n = _token_rows(x, tm)
    d = x_args[0].shape[1]
    return pl.pallas_call(
        functools.partial(_prenorm_kernel, ctx_tiles=ctx_tiles),
        grid_spec=pltpu.PrefetchScalarGridSpec(
            num_scalar_prefetch=1, grid=(n // tm,),
            in_specs=x_specs + [pl.BlockSpec((None, 1, d), lambda i, t: (layer, 0, 0)),
                                pl.BlockSpec((None, None, 6, d), lambda i, t: (layer, t[i], 0, 0))],
            out_specs=pl.BlockSpec((tm, d), lambda i, t: (i, 0))),
        out_shape=jax.ShapeDtypeStruct((n, d), BF16),
        compiler_params=_params(1),
        name="prenorm",
    )(tile_mod, *x_args, g, mod)


def _proj_kernel(h_ref, w_ref, o_ref, wb_ref):
    @pl.when(pl.program_id(1) == 0)
    def _():
        wb_ref[...] = w_ref[...].astype(BF16)

    o_ref[...] = jnp.dot(h_ref[...], wb_ref[...], preferred_element_type=F32).astype(o_ref.dtype)


def _project(h, w, layer, col0, width, out_dtype):
    n, d = h.shape
    tm = PROJ_ROWS
    tn = PROJ_TILE
    cb = col0 // tn
    return pl.pallas_call(
        _proj_kernel,
        grid=(width // tn, n // tm),
        in_specs=[pl.BlockSpec((tm, d), lambda j, i: (i, 0)),
                  pl.BlockSpec((None, d, tn), lambda j, i: (layer, 0, cb + j))],
        out_specs=pl.BlockSpec((tm, tn), lambda j, i: (i, j)),
        out_shape=jax.ShapeDtypeStruct((n, width), out_dtype),
        scratch_shapes=[pltpu.VMEM((d, tn), BF16)],
        compiler_params=_params(2),
        name="in_proj",
    )(h, w)


def _head_pair_masks():
    lane = lax.broadcasted_iota(I32, (1, LANES), 1)
    first = lane < NA_HEAD_DIM
    return first, jnp.logical_not(first)


def _attn_ctx_kernel(q_ref, k_ref, v_ref, o_ref):
    masks = _head_pair_masks()
    scale = NA_HEAD_DIM ** -0.5
    for p in range(NA_WIDTH // LANES):
        cols = slice(LANES * p, LANES * (p + 1))
        q2 = q_ref[:, cols] * scale
        k2 = k_ref[:, cols].astype(BF16)
        v2 = v_ref[:, cols].astype(BF16)
        outs = []
        for m in masks:
            qa = jnp.where(m, q2, 0.0).astype(BF16)
            s = lax.dot_general(qa, k2, (((1,), (1,)), ((), ())), preferred_element_type=F32)
            e = jnp.exp(s - jnp.max(s, axis=-1, keepdims=True))
            den = jnp.sum(e, axis=-1, keepdims=True)
            outs.append(jnp.dot(e.astype(BF16), v2, preferred_element_type=F32) / den)
        o_ref[:, cols] = jnp.where(masks[0], outs[0], outs[1]).astype(o_ref.dtype)


def _attention_ctx(z_qkv, n_seq, seq):
    return pl.pallas_call(
        _attn_ctx_kernel,
        grid=(n_seq,),
        in_specs=[pl.BlockSpec((seq, NA_WIDTH), lambda b: (b, 0)),
                  pl.BlockSpec((seq, NA_WIDTH), lambda b: (b, 1)),
                  pl.BlockSpec((seq, NA_WIDTH), lambda b: (b, 2))],
        out_specs=pl.BlockSpec((seq, NA_WIDTH), lambda b: (b, 0)),
        out_shape=jax.ShapeDtypeStruct((n_seq * seq, NA_WIDTH), BF16),
        compiler_params=_params(1),
        name="attn_ctx",
    )(z_qkv, z_qkv, z_qkv)


def _attn_lat_kernel(q_ref, k_ref, v_ref, kc_ref, vc_ref, bias_ref, o_ref, kb_ref, vb_ref, *, tq):
    masks = _head_pair_masks()
    scale = NA_HEAD_DIM ** -0.5
    seq = q_ref.shape[0]
    slab = bias_ref.shape[3]
    per_half = seq // 2 // tq
    kb_ref[...] = k_ref[...].astype(BF16)
    vb_ref[...] = v_ref[...].astype(BF16)
    kc = kc_ref[...].astype(BF16)
    vc = vc_ref[...].astype(BF16)
    nt = (((1,), (1,)), ((), ()))

    def q_tile(qi, carry):
        rows = pl.ds(pl.multiple_of(qi * tq, tq), tq)
        half = qi // per_half
        half_rows = pl.ds(pl.multiple_of((qi % per_half) * tq, tq), tq)
        keys = pl.ds(pl.multiple_of(half * (seq - slab), seq - slab), slab)
        k2 = kb_ref[keys, :]
        v2 = vb_ref[keys, :]
        q2 = q_ref[rows, :] * scale
        outs = []
        for hh, m in enumerate(masks):
            qa = jnp.where(m, q2, 0.0).astype(BF16)
            s_lat = lax.dot_general(qa, k2, nt, preferred_element_type=F32) + bias_ref[hh, half, half_rows, :]
            s_ctx = lax.dot_general(qa, kc, nt, preferred_element_type=F32)
            mx = jnp.maximum(jnp.max(s_lat, axis=-1, keepdims=True), jnp.max(s_ctx, axis=-1, keepdims=True))
            e_lat = jnp.exp(s_lat - mx)
            e_ctx = jnp.exp(s_ctx - mx)
            den = jnp.sum(e_lat, axis=-1, keepdims=True) + jnp.sum(e_ctx, axis=-1, keepdims=True)
            o = (jnp.dot(e_lat.astype(BF16), v2, preferred_element_type=F32)
                 + jnp.dot(e_ctx.astype(BF16), vc, preferred_element_type=F32))
            outs.append(o / den)
        o_ref[rows, :] = jnp.where(masks[0], outs[0], outs[1]).astype(o_ref.dtype)
        return carry

    lax.fori_loop(0, q_ref.shape[0] // tq, q_tile, 0)


def _attention_lat(z_qkv, cache_k, cache_v, bias, layer, row0, n_seq, seq):
    past = cache_k.shape[2]
    pairs = NA_WIDTH // LANES
    rb = row0 // seq
    kv_cols = NA_WIDTH // LANES
    slab = bias.shape[-1]
    return pl.pallas_call(
        functools.partial(_attn_lat_kernel, tq=min(seq // 2, ATTN_Q_TILE)),
        grid=(pairs, n_seq),
        in_specs=[pl.BlockSpec((seq, LANES), lambda p, b: (rb + b, p)),
                  pl.BlockSpec((seq, LANES), lambda p, b: (rb + b, kv_cols + p)),
                  pl.BlockSpec((seq, LANES), lambda p, b: (rb + b, 2 * kv_cols + p)),
                  pl.BlockSpec((None, None, past, LANES), lambda p, b: (b, layer, 0, p)),
                  pl.BlockSpec((None, None, past, LANES), lambda p, b: (b, layer, 0, p)),
                  pl.BlockSpec((None, 2, 2, seq // 2, slab), lambda p, b: (layer, p, 0, 0, 0))],
        out_specs=pl.BlockSpec((seq, LANES), lambda p, b: (b, p)),
        out_shape=jax.ShapeDtypeStruct((n_seq * seq, NA_WIDTH), BF16),
        scratch_shapes=[pltpu.VMEM((seq, LANES), BF16), pltpu.VMEM((seq, LANES), BF16)],
        compiler_params=_params(2),
        name="attn_lat",
    )(z_qkv, z_qkv, z_qkv, cache_k, cache_v, bias)


def _neighbourhood_bias(rpb, seq):
    rows = seq // GRID_W
    kh = WIN_H
    assert rows >= WIN_H
    lead = rpb.shape[:-2]
    c = np.arange(GRID_W)
    q_cs = np.clip(c - WIN_W // 2, 0, GRID_W - WIN_W)
    col_ok = (c[None, :] >= q_cs[:, None]) & (c[None, :] < q_cs[:, None] + WIN_W)
    r = np.arange(rows)
    rs = np.clip(r - kh // 2, 0, rows - kh)
    base = np.where(r < rows // 2, 0, rows - KEY_SLAB_ROWS)
    assert (rs >= base).all() and (rs + kh <= base + KEY_SLAB_ROWS).all()
    pick_c = (c[None, None, :] - c[None, :, None] + WIN_W - 1
              == np.arange(2 * WIN_W - 1)[:, None, None]).astype(np.float32)
    w = jnp.einsum("...ij,jqk->...qik", rpb, pick_c, precision=lax.Precision.HIGHEST)
    w = jnp.where(jnp.asarray(col_ok)[:, None, :], w, NEG_INF)
    blocks = []
    for rq in range(rows):
        lo = int(rs[rq]) - rq + WIN_H - 1
        slab = w[..., lo:lo + kh, :].reshape(lead + (GRID_W, kh * GRID_W))
        left = int(rs[rq] - base[rq])
        pad = ((0, 0),) * (len(lead) + 1) + ((left * GRID_W, (KEY_SLAB_ROWS - kh - left) * GRID_W),)
        blocks.append(jnp.pad(slab, pad, constant_values=NEG_INF))
    return jnp.stack(blocks, axis=-3).reshape(lead + (2, seq // 2, KEY_SLAB_ROWS * GRID_W))


def _fourier_kernel(u_ref, ct2_ref, cc_ref, sc_ref, o_ref, pq_ref):
    t = u_ref.shape[0]
    for g in range(FOURIER_GROUPS):
        cols = slice(FOURIER_GROUP_DIM * g, FOURIER_GROUP_DIM * (g + 1))
        ug = u_ref[:, cols].astype(BF16)
        pq_ref[0:t, cols] = jnp.dot(ug, cc_ref[...], preferred_element_type=F32).astype(BF16)
        pq_ref[t:2 * t, cols] = jnp.dot(ug, sc_ref[...], preferred_element_type=F32).astype(BF16)
    o_ref[...] = jnp.dot(ct2_ref[...], pq_ref[...], preferred_element_type=F32).astype(o_ref.dtype)


def _dft_tables(t):
    def cs(n):
        k = np.arange(n, dtype=np.int64)
        ang = 2.0 * np.pi * ((k[:, None] * k[None, :]) % n).astype(np.float64) / n
        return np.cos(ang) / np.sqrt(n), np.sin(ang) / np.sqrt(n)

    ct, st = cs(t)
    cc, sc = cs(FOURIER_GROUP_DIM)
    ct2 = np.concatenate([ct, -st], axis=1).astype(np.float32)
    return (jnp.asarray(ct2).astype(BF16), jnp.asarray(cc.astype(np.float32)).astype(BF16),
            jnp.asarray(sc.astype(np.float32)).astype(BF16))


def _fourier(z_qkvu, row0, n_seq, seq):
    ct2, cc, sc = _dft_tables(seq)
    width = FOURIER_GROUPS * FOURIER_GROUP_DIM
    rb = row0 // seq
    ucol = 3 * NA_WIDTH // width
    return pl.pallas_call(
        _fourier_kernel,
        grid=(n_seq,),
        in_specs=[pl.BlockSpec((seq, width), lambda b: (rb + b, ucol)),
                  pl.BlockSpec((seq, 2 * seq), lambda b: (0, 0)),
                  pl.BlockSpec((FOURIER_GROUP_DIM, FOURIER_GROUP_DIM), lambda b: (0, 0)),
                  pl.BlockSpec((FOURIER_GROUP_DIM, FOURIER_GROUP_DIM), lambda b: (0, 0))],
        out_specs=pl.BlockSpec((seq, width), lambda b: (b, 0)),
        out_shape=jax.ShapeDtypeStruct((n_seq * seq, width), BF16),
        scratch_shapes=[pltpu.VMEM((2 * seq, width), BF16)],
        compiler_params=_params(1),
        name="fourier",
    )(z_qkvu, ct2, cc, sc)


def _rotary_tables(t):
    pos = np.arange(t)
    row = (pos // GRID_W).astype(np.float64)
    col = (pos % GRID_W).astype(np.float64)
    nf = RET_KEY_DIM // 4
    inv_freq = ROPE_BASE ** (-np.arange(nf, dtype=np.float64) / nf)
    ar = row[:, None] * inv_freq[None]
    ac = col[:, None] * inv_freq[None]
    cos = np.concatenate([np.cos(ar), np.cos(ar), np.cos(ac), np.cos(ac)], axis=1)
    sin = np.concatenate([-np.sin(ar), np.sin(ar), -np.sin(ac), np.sin(ac)], axis=1)
    return jnp.asarray(cos.astype(np.float32)), jnp.asarray(sin.astype(np.float32))


def _ret_kernel(lgf_ref, lgb_ref, *refs, t, tq, layer, rotary, state_in, state_out):
    refs = list(refs)
    q_ref, k_ref, v_ref, g_ref = refs[:4]
    refs = refs[4:]
    if rotary:
        cos_ref, sin_ref = refs[:2]
        refs = refs[2:]
    if state_in:
        sf0_ref, sb0_ref = refs[:2]
        refs = refs[2:]
    o_ref = refs[0]
    refs = refs[1:]
    if state_out:
        sf_ref, sb_ref = refs[:2]
        refs = refs[2:]
    dec_ref, kb_ref = refs

    scale = RET_KEY_DIM ** -0.5
    nq = t // tq
    heads_here = dec_ref.shape[0]

    if rotary:
        lane = lax.broadcasted_iota(I32, (1, LANES), 1)
        low = (lane % (RET_KEY_DIM // 2)) < (RET_KEY_DIM // 4)

        def rot(x, rows):
            swapped = jnp.where(low, pltpu.roll(x, LANES - RET_KEY_DIM // 4, 1), pltpu.roll(x, RET_KEY_DIM // 4, 1))
            return x * cos_ref[rows, :] + swapped * sin_ref[rows, :]
    else:
        def rot(x, rows):
            return x

    def one_head(hh):
        h = pl.program_id(0) * heads_here + hh
        cols = slice(LANES * hh, LANES * (hh + 1))
        lgf = lgf_ref[layer * RET_HEADS + h]
        lgb = lgb_ref[layer * RET_HEADS + h]

        @pl.when(pl.program_id(1) == 0)
        def _():
            def fill(ri, c):
                rows = pl.ds(pl.multiple_of(ri * tq, tq), tq)
                i = lax.broadcasted_iota(I32, (tq, t), 0) + ri * tq
                j = lax.broadcasted_iota(I32, (tq, t), 1)
                d = (i - j).astype(F32)
                m = jnp.exp(jnp.abs(d) * jnp.where(d > 0, lgf, lgb))
                dec_ref[hh, rows, :] = jnp.where(d == 0, 2.0, m)
                return c

            lax.fori_loop(0, nq, fill, 0)

        kr = rot(k_ref[:, cols].astype(F32), slice(0, t))
        kb_ref[hh] = kr.astype(BF16)
        vb = v_ref[:, cols]

        if state_out:
            j = lax.broadcasted_iota(I32, (t, 1), 0).astype(F32)
            tn = (((0,), (0,)), ((), ()))
            kf = (kr * (scale * jnp.exp(lgf * (t - 1.0 - j)))).astype(BF16)
            kbw = (kr * (scale * jnp.exp(lgb * j))).astype(BF16)
            sf = lax.dot_general(kf, vb, tn, preferred_element_type=F32)
            sb = lax.dot_general(kbw, vb, tn, preferred_element_type=F32)
            if state_in:
                sf = sf + jnp.exp(lgf * t) * sf0_ref[hh]
                sb = sb + jnp.exp(lgb * t) * sb0_ref[hh]
            sf_ref[hh] = sf
            sb_ref[hh] = sb

        def q_tile(qi, carry):
            r0 = pl.multiple_of(qi * tq, tq)
            rows = pl.ds(r0, tq)
            qr = rot(q_ref[rows, cols].astype(F32), rows)
            s = lax.dot_general((qr * scale).astype(BF16), kb_ref[hh], (((1,), (1,)), ((), ())),
                                preferred_element_type=F32)
            y = jnp.dot((s * dec_ref[hh, rows, :]).astype(BF16), vb, preferred_element_type=F32)
            if state_in:
                pos = (lax.broadcasted_iota(I32, (tq, 1), 0) + r0).astype(F32)
                qf = (qr * jnp.exp(lgf * (pos + 1.0))).astype(BF16)
                qb = (qr * jnp.exp(lgb * (t - pos))).astype(BF16)
                y = (y + jnp.dot(qf, sf0_ref[hh].astype(BF16), preferred_element_type=F32)
                     + jnp.dot(qb, sb0_ref[hh].astype(BF16), preferred_element_type=F32))
            mean = jnp.mean(y, axis=-1, keepdims=True)
            yc = y - mean
            yn = yc * lax.rsqrt(jnp.mean(yc * yc, axis=-1, keepdims=True) + EPS)
            g = g_ref[rows, cols].astype(F32)
            o_ref[rows, cols] = (g * jax.nn.sigmoid(g) * yn).astype(o_ref.dtype)
            return carry

        lax.fori_loop(0, nq, q_tile, 0)

    for hh in range(heads_here):
        one_head(hh)


def _retention(z_ret, lg_f, lg_b, layer, row0, n_seq, seq, *, rotary, states=None, state_out=False):
    rb = row0 // seq
    tq = min(seq, RET_Q_TILE)
    state_in = states is not None
    hps = RET_HEADS if seq * seq * RET_HEADS * 4 <= RET_DECAY_BYTES else 1
    cb = BRANCH_WIDTH // (hps * LANES)
    width = hps * LANES
    in_specs = [pl.BlockSpec((seq, width), lambda h, b, *_: (rb + b, 0 * cb + h)),
                pl.BlockSpec((seq, width), lambda h, b, *_: (rb + b, 1 * cb + h)),
                pl.BlockSpec((seq, width), lambda h, b, *_: (rb + b, 2 * cb + h)),
                pl.BlockSpec((seq, width), lambda h, b, *_: (rb + b, 3 * cb + h))]
    args = [z_ret, z_ret, z_ret, z_ret]
    if rotary:
        cos, sin = _rotary_tables(seq)
        in_specs += [pl.BlockSpec((seq, LANES), lambda h, b, *_: (0, 0))] * 2
        args += [cos, sin]
    if state_in:
        st_spec = pl.BlockSpec((None, None, hps, RET_KEY_DIM, RET_KEY_DIM), lambda h, b, *_: (b, layer, h, 0, 0))
        in_specs += [st_spec, st_spec]
        args += list(states)
    out_specs = [pl.BlockSpec((seq, width), lambda h, b, *_: (b, h))]
    out_shape = [jax.ShapeDtypeStruct((n_seq * seq, RET_HEADS * LANES), BF16)]
    if state_out:
        so = pl.BlockSpec((None, hps, RET_KEY_DIM, RET_KEY_DIM), lambda h, b, *_: (b, h, 0, 0))
        out_specs += [so, so]
        out_shape += [jax.ShapeDtypeStruct((n_seq, RET_HEADS, RET_KEY_DIM, RET_KEY_DIM), F32)] * 2
    return pl.pallas_call(
        functools.partial(_ret_kernel, t=seq, tq=tq, layer=layer, rotary=rotary, state_in=state_in,
                          state_out=state_out),
        grid_spec=pltpu.PrefetchScalarGridSpec(
            num_scalar_prefetch=2, grid=(RET_HEADS // hps, n_seq),
            in_specs=in_specs, out_specs=out_specs,
            scratch_shapes=[pltpu.VMEM((hps, seq, seq), F32), pltpu.VMEM((hps, seq, LANES), BF16)]),
        out_shape=out_shape,
        compiler_params=_params(2),
        name="retention",
    )(lg_f, lg_b, *args)


def _split_dot_nt(w, x):
    nt = (((1,), (1,)), ((), ()))
    w_hi = w.astype(BF16)
    w_lo = (w - w_hi.astype(F32)).astype(BF16)
    x_hi = x.astype(BF16)
    x_lo = (x - x_hi.astype(F32)).astype(BF16)
    return (lax.dot_general(w_hi, x_hi, nt, preferred_element_type=F32)
            + lax.dot_general(w_hi, x_lo, nt, preferred_element_type=F32)
            + lax.dot_general(w_lo, x_hi, nt, preferred_element_type=F32))


def _finish_kernel(tmod_ref, *refs, ctx_tiles, n_x):
    del tmod_ref
    branch_refs, refs = refs[:6], refs[6:]
    zg_ref, refs = refs[0], refs[1:]
    x_refs, refs = refs[:n_x], refs[n_x:]
    (mod_ref, gpost_ref, gpre_ref, wb_ref, wo_ref, wrt_ref, br_ref, tri_ref, ltri_ref,
     x1_ref, h2_ref, wts_ref, lpos_ref, seg_ref, off_ref) = refs
    d = D_MODEL

    def branch(j):
        return _load_rows(branch_refs[2 * j:2 * j + 2], ctx_tiles)

    def gate(j):
        return jax.nn.sigmoid(zg_ref[:, d * j:d * (j + 1)].astype(F32))

    merged = (gate(0) * jnp.dot(branch(0), wb_ref[0], preferred_element_type=F32)
              + gate(1) * jnp.dot(branch(1), wb_ref[1], preferred_element_type=F32)
              + gate(2) * jnp.dot(branch(2), wb_ref[2], preferred_element_type=F32))
    y = jnp.dot(merged.astype(BF16), wo_ref[...], preferred_element_type=F32)
    x1 = _load_rows(x_refs, ctx_tiles) + mod_ref[2:3, :] * (y * _rms(y) * gpost_ref[...])
    x1_ref[...] = x1
    h2 = x1 * _rms(x1) * gpre_ref[...] * (1.0 + mod_ref[4:5, :]) + mod_ref[3:4, :]
    h2_ref[...] = h2

    logits = _split_dot_nt(wrt_ref[...], h2) + br_ref[:, 0:1]
    tm = logits.shape[1]
    eidx = lax.broadcasted_iota(I32, (N_EXPERTS, tm), 0)
    cur = logits
    vals, hots = [], []
    for k in range(TOP_K):
        m = jnp.max(cur, axis=0, keepdims=True)
        sel = jnp.min(jnp.where(cur == m, eidx, N_EXPERTS), axis=0, keepdims=True)
        hot = eidx == sel
        vals.append(m)
        hots.append(hot)
        cur = jnp.where(hot, -jnp.inf, cur)
    exps = [jnp.exp(v - vals[0]) for v in vals]
    den = exps[0] + exps[1] + exps[2] + exps[3]
    for k in range(TOP_K):
        wts_ref[k:k + 1, :] = exps[k] / den

    member = jnp.logical_or(jnp.logical_or(hots[0], hots[1]), jnp.logical_or(hots[2], hots[3]))
    member_f = member.astype(F32)
    before = jnp.dot(member_f.astype(BF16), tri_ref[...], preferred_element_type=F32)
    units = jnp.ceil(jnp.sum(member_f, axis=1, keepdims=True) * (1.0 / SEG_ALIGN))
    units = jnp.broadcast_to(units, seg_ref.shape)
    off = jnp.dot(ltri_ref[...], units.astype(BF16), preferred_element_type=F32) * SEG_ALIGN
    seg_ref[...] = units * SEG_ALIGN
    off_ref[...] = off
    place = before + off[:, 0:1]
    for k in range(TOP_K):
        lpos_ref[k:k + 1, :] = jnp.sum(jnp.where(hots[k], place, 0.0), axis=0, keepdims=True).astype(I32)


def _finish(branches, z_gate, x, mod, g_post, g_pre_ffn, w_branch, w_out, w_router_t, b_router, layer, tile_mod):
    tm = FIN_TILE
    x_specs, x_args, _, n = _token_rows(x, tm)
    d = x_args[0].shape[1]
    tri = jnp.asarray(np.triu(np.ones((tm, tm), np.float32), k=1)).astype(BF16)
    ltri = jnp.asarray(np.tril(np.ones((N_EXPERTS, N_EXPERTS), np.float32), k=-1)).astype(BF16)
    row = lambda i, t: (i, 0)
    const2 = lambda i, t: (0, 0)
    lay3 = lambda i, t: (layer, 0, 0)
    col = lambda i, t: (0, i)
    branch_specs, branch_args = [], []
    for pair in branches:
        specs, args, ctx_tiles, _ = _token_rows(pair, tm)
        branch_specs += specs
        branch_args += args
    outs = pl.pallas_call(
        functools.partial(_finish_kernel, ctx_tiles=ctx_tiles, n_x=len(x_args)),
        grid_spec=pltpu.PrefetchScalarGridSpec(
            num_scalar_prefetch=1, grid=(n // tm,),
            in_specs=branch_specs + [pl.BlockSpec((tm, GATE_W), row)] + x_specs + [
                      pl.BlockSpec((None, None, 6, d), lambda i, t: (layer, t[i], 0, 0)),
                      pl.BlockSpec((None, 1, d), lay3),
                      pl.BlockSpec((None, 1, d), lay3),
                      pl.BlockSpec((None, 3, BRANCH_WIDTH, d), lambda i, t: (layer, 0, 0, 0)),
                      pl.BlockSpec((None, d, d), lay3),
                      pl.BlockSpec((None, N_EXPERTS, d), lay3),
                      pl.BlockSpec((None, N_EXPERTS, LANES), lay3),
                      pl.BlockSpec((tm, tm), const2),
                      pl.BlockSpec((N_EXPERTS, N_EXPERTS), const2)],
            out_specs=[pl.BlockSpec((tm, d), row),
                       pl.BlockSpec((tm, d), row),
                       pl.BlockSpec((TOP_K, tm), col),
                       pl.BlockSpec((TOP_K, tm), col),
                       pl.BlockSpec((None, N_EXPERTS, LANES), lambda i, t: (i, 0, 0)),
                       pl.BlockSpec((None, N_EXPERTS, LANES), lambda i, t: (i, 0, 0))]),
        out_shape=[jax.ShapeDtypeStruct((n, d), F32),
                   jax.ShapeDtypeStruct((n, d), F32),
                   jax.ShapeDtypeStruct((TOP_K, n), F32),
                   jax.ShapeDtypeStruct((TOP_K, n), I32),
                   jax.ShapeDtypeStruct((n // tm, N_EXPERTS, LANES), F32),
                   jax.ShapeDtypeStruct((n // tm, N_EXPERTS, LANES), F32)],
        compiler_params=_params(1),
        name="merge_router",
    )(tile_mod, *branch_args, z_gate, *x_args, mod, g_post, g_pre_ffn, w_branch, w_out, w_router_t, b_router,
      tri, ltri)
    return outs


def _segment_chunks(length, src_ref, src0, dst_ref, dst0, sem, max_chunk, fixed_src=False):
    out = []
    chunk = max_chunk
    while chunk >= SEG_ALIGN:
        done = jnp.bitwise_and(length, ~(2 * chunk - 1))
        present = jnp.bitwise_and(length, chunk) != 0
        s = 0 if fixed_src else pl.multiple_of(src0 + done, SEG_ALIGN)
        dd = pl.multiple_of(dst0 + done, SEG_ALIGN)
        out.append((present, pltpu.make_async_copy(src_ref.at[pl.ds(s, chunk)], dst_ref.at[pl.ds(dd, chunk)], sem)))
        chunk //= 2
    return out


def _for_each_chunk(n_segments, chunks_of, action):
    def body(e, c):
        for present, cp in chunks_of(e):
            pl.when(present)(functools.partial(action, cp))
        return c

    lax.fori_loop(0, n_segments, body, 0, unroll=4)


def _start(cp):
    cp.start()


def _wait(cp):
    cp.wait()


def _wait_rows(total, src_ref, dst_ref, sem):
    chunk = pl.next_power_of_2(SORT_ROWS) // 2
    while chunk >= SEG_ALIGN:
        @pl.when(jnp.bitwise_and(total, chunk) != 0)
        def _(chunk=chunk):
            pltpu.make_async_copy(src_ref.at[pl.ds(0, chunk)], dst_ref.at[pl.ds(0, chunk)], sem).wait()
        chunk //= 2


def _scatter_kernel(seg_ref, off_ref, pos_ref, used_ref, tpos_ref, tlen_ref, h_ref, lpos_ref, xe_hbm,
                    buf_ref, zero_ref, sem, *, tm):
    t = pl.program_id(0)
    slot = t % 2
    hb = h_ref[...].astype(BF16)
    lp = [lpos_ref[k:k + 1, :] for k in range(TOP_K)]
    rows = buf_ref.shape[1]
    for c0 in range(0, rows, SORT_CHUNK):
        r = lax.broadcasted_iota(I32, (SORT_CHUNK, tm), 0) + c0
        hit = jnp.logical_or(jnp.logical_or(r == lp[0], r == lp[1]), jnp.logical_or(r == lp[2], r == lp[3]))
        onehot = jnp.where(hit, 1.0, 0.0).astype(BF16)
        buf_ref[slot, c0:c0 + SORT_CHUNK, :] = jnp.dot(onehot, hb, preferred_element_type=F32)

    def segments_of(tile):
        def segment(e):
            j = tile * N_EXPERTS + e
            return _segment_chunks(seg_ref[j], buf_ref.at[tile % 2], off_ref[j], xe_hbm, pos_ref[j],
                                   sem.at[tile % 2], tm)
        return segment

    @pl.when(t > 0)
    def _():
        _wait_rows(used_ref[t - 1], buf_ref.at[1 - slot], xe_hbm, sem.at[1 - slot])

    _for_each_chunk(N_EXPERTS, segments_of(t), _start)

    @pl.when(t == pl.num_programs(0) - 1)
    def _():
        _wait_rows(used_ref[t], buf_ref.at[slot], xe_hbm, sem.at[slot])
        zero_ref[...] = jnp.zeros_like(zero_ref)

        def tail(e):
            return _segment_chunks(tlen_ref[e], zero_ref, 0, xe_hbm, tpos_ref[e], sem.at[0], zero_ref.shape[0],
                                   fixed_src=True)

        _for_each_chunk(N_EXPERTS, tail, _start)
        _for_each_chunk(N_EXPERTS, tail, _wait)


def _scatter_rows(h2, lpos, plan, n_rows):
    n, d = h2.shape
    tm = FIN_TILE
    return pl.pallas_call(
        functools.partial(_scatter_kernel, tm=tm),
        grid_spec=pltpu.PrefetchScalarGridSpec(
            num_scalar_prefetch=6, grid=(n // tm,),
            in_specs=[pl.BlockSpec((tm, d), lambda i, *_: (i, 0)),
                      pl.BlockSpec((TOP_K, tm), lambda i, *_: (0, i))],
            out_specs=pl.BlockSpec(memory_space=pl.ANY),
            scratch_shapes=[pltpu.VMEM((2, SORT_ROWS, d), F32), pltpu.VMEM((MOE_BLOCK // 2, d), F32),
                            pltpu.SemaphoreType.DMA((2,))]),
        out_shape=jax.ShapeDtypeStruct((n_rows, d), F32),
        compiler_params=_params(1),
        name="moe_scatter",
    )(plan["seg"], plan["off"], plan["pos"], plan["used"], plan["tail_pos"], plan["tail_len"], h2, lpos)


def _expert_kernel(be_ref, nxt_ref, valid_ref, nu_ref, xb_ref, w1_hbm, b1_ref, w2_hbm, b2_ref, yb_ref,
                   w1s_ref, w2s_ref, w1b_ref, w2b_ref, sem, *, layer):
    i = pl.program_id(0)
    used = i < nu_ref[0]
    fresh = jnp.logical_or(i == 0, be_ref[i] != be_ref[jnp.maximum(i - 1, 0)])

    def fetch(e):
        return (pltpu.make_async_copy(w1_hbm.at[layer, e], w1s_ref, sem.at[0]),
                pltpu.make_async_copy(w2_hbm.at[layer, e], w2s_ref, sem.at[1]))

    @pl.when(i == 0)
    def _():
        for cp in fetch(be_ref[0]):
            cp.start()

    @pl.when(jnp.logical_and(used, fresh))
    def _():
        for cp in fetch(be_ref[i]):
            cp.wait()
        w1b_ref[...] = w1s_ref[...].astype(BF16)
        w2b_ref[...] = w2s_ref[...].astype(BF16)

        @pl.when(nxt_ref[i] >= 0)
        def _():
            for cp in fetch(nxt_ref[i]):
                cp.start()

    def ffn(m):
        e = be_ref[i]
        z = jnp.dot(xb_ref[0:m, :].astype(BF16), w1b_ref[...], preferred_element_type=F32) + b1_ref[e]
        glu = jnp.minimum(z[:, :D_FF], SWIGLU_LIMIT)
        lin = jnp.clip(z[:, D_FF:], -SWIGLU_LIMIT, SWIGLU_LIMIT)
        act = glu * jax.nn.sigmoid(SWIGLU_ALPHA * glu) * (lin + 1.0)
        yb_ref[0:m, :] = jnp.dot(act.astype(BF16), w2b_ref[...], preferred_element_type=F32) + b2_ref[e]
        if m < yb_ref.shape[0]:
            yb_ref[m:, :] = jnp.zeros((yb_ref.shape[0] - m, yb_ref.shape[1]), F32)

    valid = valid_ref[i]
    for m in range(EXPERT_ROW_STEP, yb_ref.shape[0] + 1, EXPERT_ROW_STEP):
        pl.when(jnp.logical_and(valid > m - EXPERT_ROW_STEP, valid <= m))(functools.partial(ffn, m))

    @pl.when(valid == 0)
    def _():
        yb_ref[...] = jnp.zeros_like(yb_ref)


def _experts(xb, block_expert, next_expert, valid_rows, n_used, w1, b1, w2, b2, layer):
    n_rows, d = xb.shape
    tm = MOE_BLOCK
    return pl.pallas_call(
        functools.partial(_expert_kernel, layer=layer),
        grid_spec=pltpu.PrefetchScalarGridSpec(
            num_scalar_prefetch=4, grid=(n_rows // tm,),
            in_specs=[pl.BlockSpec((tm, d), lambda i, be, nx, vr, nu: (jnp.minimum(i, nu[0] - 1), 0)),
                      pl.BlockSpec(memory_space=pl.ANY),
                      pl.BlockSpec((None, N_EXPERTS, 1, 2 * D_FF), lambda i, be, nx, vr, nu: (layer, 0, 0, 0)),
                      pl.BlockSpec(memory_space=pl.ANY),
                      pl.BlockSpec((None, N_EXPERTS, 1, d), lambda i, be, nx, vr, nu: (layer, 0, 0, 0))],
            out_specs=pl.BlockSpec((tm, d), lambda i, be, nx, vr, nu: (i, 0)),
            scratch_shapes=[pltpu.VMEM((d, 2 * D_FF), F32), pltpu.VMEM((D_FF, d), F32),
                            pltpu.VMEM((d, 2 * D_FF), BF16), pltpu.VMEM((D_FF, d), BF16),
                            pltpu.SemaphoreType.DMA((2,))]),
        out_shape=jax.ShapeDtypeStruct((n_rows, d), F32),
        compiler_params=_params(1),
        name="moe_experts",
    )(block_expert, next_expert, valid_rows, n_used, xb, w1, b1, w2, b2)


def _gather_kernel(tmod_ref, seg_ref, off_ref, pos_ref, used_ref, ye_hbm, lpos_ref, wts_ref, x1_ref, mod_ref, g_ref,
                   *refs, tm, tile0, prenorm_next):
    del tmod_ref
    if prenorm_next:
        modn_ref, gn_ref, o_ref, hn_ref, buf_ref, sem = refs
    else:
        o_ref, buf_ref, sem = refs
    step = pl.program_id(0)
    t = step + tile0
    slot = step % 2

    def segments_of(tile):
        def segment(e):
            j = tile * N_EXPERTS + e
            half = (tile - tile0) % 2
            return _segment_chunks(seg_ref[j], ye_hbm, pos_ref[j], buf_ref.at[half], off_ref[j], sem.at[half], tm)
        return segment

    @pl.when(step == 0)
    def _():
        buf_ref[...] = jnp.zeros_like(buf_ref)
        _for_each_chunk(N_EXPERTS, segments_of(t), _start)

    @pl.when(step + 1 < pl.num_programs(0))
    def _():
        _for_each_chunk(N_EXPERTS, segments_of(t + 1), _start)

    _wait_rows(used_ref[t], ye_hbm, buf_ref.at[slot], sem.at[slot])

    lp = [lpos_ref[:, k:k + 1] for k in range(TOP_K)]
    wt = [wts_ref[:, k:k + 1] for k in range(TOP_K)]
    y = jnp.zeros(o_ref.shape, F32)
    for c0 in range(0, buf_ref.shape[1], SORT_CHUNK):
        r = lax.broadcasted_iota(I32, (tm, SORT_CHUNK), 1) + c0
        wm = jnp.where(r == lp[0], wt[0], 0.0)
        for k in range(1, TOP_K):
            wm = jnp.where(r == lp[k], wt[k], wm)
        y = y + jnp.dot(wm.astype(BF16), buf_ref[slot, c0:c0 + SORT_CHUNK, :].astype(BF16),
                        preferred_element_type=F32)
    x2 = x1_ref[...] + mod_ref[5:6, :] * (y * _rms(y) * g_ref[...])
    o_ref[...] = x2
    if prenorm_next:
        hn = x2 * _rms(x2) * gn_ref[...]
        hn_ref[...] = (hn * (1.0 + modn_ref[1:2, :]) + modn_ref[0:1, :]).astype(hn_ref.dtype)


def _gather_combine(ye, lpos_t, wts_t, plan, x1, mod, g_post, layer, tile_mod, row0=0, rows=None, g_pre_next=None):
    n, d = x1.shape
    tm = FIN_TILE
    rows = n if rows is None else rows
    tile0 = row0 // tm
    row = lambda i, *_: (tile0 + i, 0)
    out_row = lambda i, *_: (i, 0)
    mod_spec = lambda lay: pl.BlockSpec((None, None, 6, d), lambda i, t, *_: (lay, t[tile0 + i], 0, 0))
    gain_spec = lambda lay: pl.BlockSpec((None, 1, d), lambda i, *_: (lay, 0, 0))
    prenorm_next = g_pre_next is not None
    in_specs = [pl.BlockSpec(memory_space=pl.ANY),
                pl.BlockSpec((tm, TOP_K), row),
                pl.BlockSpec((tm, TOP_K), row),
                pl.BlockSpec((tm, d), row),
                mod_spec(layer), gain_spec(layer)]
    args = [ye, lpos_t, wts_t, x1, mod, g_post]
    out_specs = [pl.BlockSpec((tm, d), out_row)]
    out_shape = [jax.ShapeDtypeStruct((rows, d), F32)]
    if prenorm_next:
        in_specs += [mod_spec(layer + 1), gain_spec(layer + 1)]
        args += [mod, g_pre_next]
        out_specs += [pl.BlockSpec((tm, d), out_row)]
        out_shape += [jax.ShapeDtypeStruct((rows, d), BF16)]
    outs = pl.pallas_call(
        functools.partial(_gather_kernel, tm=tm, tile0=tile0, prenorm_next=prenorm_next),
        grid_spec=pltpu.PrefetchScalarGridSpec(
            num_scalar_prefetch=5, grid=(rows // tm,),
            in_specs=in_specs, out_specs=out_specs,
            scratch_shapes=[pltpu.VMEM((2, SORT_ROWS, d), F32), pltpu.SemaphoreType.DMA((2,))]),
        out_shape=out_shape,
        compiler_params=_params(1),
        name="moe_gather",
    )(tile_mod, plan["seg"], plan["off"], plan["pos"], plan["used"], *args)
    return outs if prenorm_next else outs[0]


def _moe(h2, wts, lpos, seg, off, x1, mod, g_post, w1, b1, w2, b2, layer, tile_mod, split_rows=None,
         g_pre_next=None):
    n, d = h2.shape
    blk = MOE_BLOCK
    tm = FIN_TILE
    tiles = n // tm
    n_rows = -(-(n * TOP_K + tiles * N_EXPERTS * (SEG_ALIGN - 1) + N_EXPERTS * (blk - 1)) // blk) * blk
    n_blocks = n_rows // blk
    seg = seg[:, :, 0].astype(I32)
    off = off[:, :, 0].astype(I32)
    rows_e = jnp.sum(seg, axis=0)
    region = (rows_e + blk - 1) // blk * blk
    pend = jnp.cumsum(region)
    pstart = pend - region
    pos = pstart[None, :] + jnp.cumsum(seg, axis=0) - seg
    plan = {"seg": seg.reshape(-1), "off": off.reshape(-1), "pos": pos.reshape(-1).astype(I32),
            "used": jnp.sum(seg, axis=1).astype(I32),
            "tail_pos": (pstart + rows_e).astype(I32), "tail_len": (region - rows_e).astype(I32)}
    blocks = jnp.arange(n_blocks, dtype=I32) * blk
    block_expert = jnp.minimum(jnp.sum(blocks[:, None] >= pend[None, :], axis=1), N_EXPERTS - 1).astype(I32)
    n_used = (pend[-1:] // blk).astype(I32)
    valid_rows = jnp.clip((pstart + rows_e)[block_expert] - blocks, 0, blk).astype(I32)
    ids = jnp.arange(n_blocks, dtype=I32)
    run_start = jnp.logical_and(jnp.concatenate([jnp.ones((1,), bool), block_expert[1:] != block_expert[:-1]]),
                                ids < n_used[0])
    first_after = lax.cummin(jnp.where(run_start, ids, n_blocks)[::-1])[::-1]
    first_after = jnp.concatenate([first_after[1:], jnp.full((1,), n_blocks, I32)])
    next_expert = jnp.where(first_after < n_blocks, block_expert[jnp.minimum(first_after, n_blocks - 1)], -1)
    xe = _scatter_rows(h2, lpos, plan, n_rows)
    ye = _experts(xe, block_expert, next_expert.astype(I32), valid_rows, n_used, w1, b1, w2, b2, layer)
    combine = functools.partial(_gather_combine, ye, lpos.T, wts.T, plan, x1, mod, g_post, layer, tile_mod)
    if split_rows is None:
        return combine(g_pre_next=g_pre_next)
    return combine(row0=0, rows=split_rows), combine(row0=split_rows, rows=n - split_rows)


def _tile_mod_ids(n_ctx_rows, n_lat_rows, lat_seq, tm):
    ctx = np.zeros((n_ctx_rows // tm,), np.int32)
    lat = 1 + (np.arange(n_lat_rows // tm) * tm) // lat_seq
    return jnp.asarray(np.concatenate([ctx, lat.astype(np.int32)]))


def kernel(x_prompt, x_sample, cache_k, cache_v, state_ret_fwd, state_ret_bwd, c, c_ctx, w_mod, b_mod, g_pre_mix, g_post_mix, g_pre_ffn, g_post_ffn, w_in, na_rel_bias, ret_decay_fwd, ret_decay_bwd, w_branch, w_out, w_router, b_router, w_exp_in, b_exp_in, w_exp_out, b_exp_out):
    batch, seq, d = x_prompt.shape
    dec_batch, dec_seq, _ = x_sample.shape
    depth = w_in.shape[0]
    n_ctx = batch * seq
    n_lat = dec_batch * dec_seq
    assert 1 + dec_batch <= MOD_ROWS

    x = (x_prompt.reshape(n_ctx, d), x_sample.reshape(n_lat, d))
    cvec =jnp.concatenate([c_ctx[None], c, jnp.zeros((MOD_ROWS - 1 - dec_batch, d), F32)], axis=0)
    mod_all = _modulation(cvec, w_mod, b_mod).reshape(depth, MOD_ROWS, 6, d)
    tmod = {tm: _tile_mod_ids(n_ctx, n_lat, dec_seq, tm) for tm in (ROW_TILE, FIN_TILE)}
    past = cache_k.shape[2]
    ck = cache_k.reshape(dec_batch, depth, past, NA_WIDTH)
    cv = cache_v.reshape(dec_batch, depth, past, NA_WIDTH)
    lg_f = jax.nn.log_sigmoid(ret_decay_fwd.astype(F32)).reshape(-1)
    lg_b = jax.nn.log_sigmoid(ret_decay_bwd.astype(F32)).reshape(-1)
    bias_all = _neighbourhood_bias(na_rel_bias, dec_seq)
    vec = lambda g: g.reshape(depth, 1, d)
    g_pre_mix, g_post_mix, g_pre_ffn, g_post_ffn = vec(g_pre_mix), vec(g_post_mix), vec(g_pre_ffn), vec(g_post_ffn)
    w_branch_b = w_branch.astype(BF16)
    w_out_b = w_out.astype(BF16)
    w_router_t = jnp.swapaxes(w_router, 1, 2)
    b_router_l = jnp.broadcast_to(b_router[:, :, None], (depth, N_EXPERTS, LANES))
    b_exp_in = b_exp_in.reshape(depth, N_EXPERTS, 1, 2 * D_FF)
    b_exp_out = b_exp_out.reshape(depth, N_EXPERTS, 1, d)

    ks, vs, sfs, sbs = [], [], [], []
    h = _prenorm(x, g_pre_mix, mod_all, 0, tmod[ROW_TILE], ROW_TILE)
    for l in range(depth):
        last = l == depth - 1
        z_qkvu = _project(h, w_in, l, 0, QKVU_W, F32)
        z_ret = _project(h, w_in, l, QKVU_W, RET_W, BF16)
        z_gate = _project(h, w_in, l, QKVU_W + RET_W, GATE_W, BF16)
        ks.append(z_qkvu[:n_ctx, NA_WIDTH:2 * NA_WIDTH].reshape(batch, seq, NA_HEADS, NA_HEAD_DIM))
        vs.append(z_qkvu[:n_ctx, 2 * NA_WIDTH:3 * NA_WIDTH].reshape(batch, seq, NA_HEADS, NA_HEAD_DIM))

        a_pair = (_attention_ctx(z_qkvu, batch, seq),
                  _attention_lat(z_qkvu, ck, cv, bias_all, l, n_ctx, dec_batch, dec_seq))
        f_pair = (_fourier(z_qkvu, 0, batch, seq), _fourier(z_qkvu, n_ctx, dec_batch, dec_seq))
        r_ctx, s_f, s_b = _retention(z_ret, lg_f, lg_b, l, 0, batch, seq, rotary=False, state_out=True)
        (r_lat,) = _retention(z_ret, lg_f, lg_b, l, n_ctx, dec_batch, dec_seq, rotary=True,
                              states=(state_ret_fwd, state_ret_bwd))
        sfs.append(s_f)
        sbs.append(s_b)

        x1, h2, wts, lpos, seg, off = _finish(
            (a_pair, f_pair, (r_ctx, r_lat)), z_gate, x, mod_all, g_post_mix, g_pre_ffn,
            w_branch_b, w_out_b, w_router_t, b_router_l, l, tmod[FIN_TILE])
        out = _moe(h2, wts, lpos, seg, off, x1, mod_all, g_post_ffn,
                   w_exp_in, b_exp_in, w_exp_out, b_exp_out, l, tmod[FIN_TILE],
                   split_rows=n_ctx if last else None, g_pre_next=None if last else g_pre_mix)
        x, h = (out, None) if last else out

    y_prompt = x[0].reshape(batch, seq, d)
    y_sample = x[1].reshape(dec_batch, dec_seq, d)
    return (y_prompt, y_sample, jnp.stack(ks, axis=1), jnp.stack(vs, axis=1),
            jnp.stack(sfs, axis=1), jnp.stack(sbs, axis=1))
```

```python
import functools

import numpy as np
import jax
import jax.numpy as jnp
from jax import lax
from jax.experimental import pallas as pl
from jax.experimental.pallas import tpu as pltpu

F32 = jnp.float32
BF16 = jnp.bfloat16
I32 = jnp.int32

D_MODEL = 1024
GRID_W = 64
NA_HEADS = 8
NA_HEAD_DIM = 64
NA_WIDTH = NA_HEADS * NA_HEAD_DIM
WIN_H = 8
WIN_W = 16
KEY_SLAB_ROWS = 12
FOURIER_GROUPS = 4
FOURIER_GROUP_DIM = 128
RET_HEADS = 4
RET_KEY_DIM = 128
ROPE_BASE = 10000.0
BRANCH_WIDTH = 512
N_EXPERTS = 32
TOP_K = 4
D_FF = 1024
SWIGLU_LIMIT = 7.0
SWIGLU_ALPHA = 1.702
EPS = 1e-6
NEG_INF = -1e30

QKVU_W = 4 * NA_WIDTH
RET_W = 4 * BRANCH_WIDTH
GATE_W = 3 * D_MODEL
PROJ_TILE = 1024
PROJ_ROWS = 2048
ATTN_Q_TILE = 512
RET_Q_TILE = 1024
RET_DECAY_BYTES = 4 * 1024 * 1024

LANES = 128
MOD_ROWS = 16
ROW_TILE = 1024
FIN_TILE = 512
MOE_BLOCK = 512
EXPERT_ROW_STEP = 128
SEG_ALIGN = 8
SORT_CHUNK = 256
SORT_ROWS = -(-(FIN_TILE * TOP_K + N_EXPERTS * (SEG_ALIGN - 1)) // SORT_CHUNK) * SORT_CHUNK
VMEM_LIMIT = 56 * 1024 * 1024


def _params(n_axes, vmem=VMEM_LIMIT):
    return pltpu.CompilerParams(dimension_semantics=("arbitrary",) * n_axes, vmem_limit_bytes=vmem)


def _rms(x):
    return lax.rsqrt(jnp.mean(x * x, axis=-1, keepdims=True) + EPS)


def _mod_kernel(cv_ref, w_ref, b_ref, o_ref):
    cv = cv_ref[...]
    s = (cv * jax.nn.sigmoid(cv)).astype(BF16)
    o_ref[...] = jnp.dot(s, w_ref[...].astype(BF16), preferred_element_type=F32) + b_ref[...]


def _modulation(cv, w_mod, b_mod):
    depth, d, n = w_mod.shape
    tn = 1536
    return pl.pallas_call(
        _mod_kernel,
        grid=(depth, n // tn),
        in_specs=[pl.BlockSpec((MOD_ROWS, d), lambda l, j: (0, 0)),
                  pl.BlockSpec((None, d, tn), lambda l, j: (l, 0, j)),
                  pl.BlockSpec((None, 1, tn), lambda l, j: (l, 0, j))],
        out_specs=pl.BlockSpec((None, MOD_ROWS, tn), lambda l, j: (l, 0, j)),
        out_shape=jax.ShapeDtypeStruct((depth, MOD_ROWS, n), F32),
        compiler_params=_params(2),
        name="modulation",
    )(cv, w_mod, b_mod.reshape(depth, 1, n))


def _token_rows(x, tm):
    if isinstance(x, tuple):
        d = x[0].shape[1]
        ct = x[0].shape[0] // tm
        specs = [pl.BlockSpec((tm, d), lambda i, *_: (jnp.minimum(i, ct - 1), 0)),
                 pl.BlockSpec((tm, d), lambda i, *_: (jnp.maximum(i - ct, 0), 0))]
        return specs, list(x), ct, x[0].shape[0] + x[1].shape[0]
    return [pl.BlockSpec((tm, x.shape[1]), lambda i, *_: (i, 0))], [x], None, x.shape[0]


def _load_rows(refs, ctx_tiles):
    if len(refs) == 1:
        return refs[0][...]
    return jnp.where(pl.program_id(0) < ctx_tiles, refs[0][...], refs[1][...])


def _prenorm_kernel(tmod_ref, *refs, ctx_tiles):
    del tmod_ref
    g_ref, mod_ref, o_ref = refs[-3:]
    x = _load_rows(refs[:-3], ctx_tiles)
    h = x * _rms(x) * g_ref[...]
    o_ref[...] = (h * (1.0 + mod_ref[1:2, :]) + mod_ref[0:1, :]).astype(o_ref.dtype)


def _prenorm(x, g, mod, layer, tile_mod, tm):
    x_specs, x_args, ctx_tiles, n = _token_rows(x, tm)
    d = x_args[0].shape[1]
    return pl.pallas_call(
        functools.partial(_prenorm_kernel, ctx_tiles=ctx_tiles),
        grid_spec=pltpu.PrefetchScalarGridSpec(
            num_scalar_prefetch=1, grid=(n // tm,),
            in_specs=x_specs + [pl.BlockSpec((None, 1, d), lambda i, t: (layer, 0, 0)),
                                pl.BlockSpec((None, None, 6, d), lambda i, t: (layer, t[i], 0, 0))],
            out_specs=pl.BlockSpec((tm, d), lambda i, t: (i, 0))),
        out_shape=jax.ShapeDtypeStruct((n, d), BF16),
        compiler_params=_params(1),
        name="prenorm",
    )(tile_mod, *x_args, g, mod)


def _proj_kernel(h_ref, w_ref, o_ref, wb_ref):
    @pl.when(pl.program_id(1) == 0)
    def _():
        wb_ref[...] = w_ref[...].astype(BF16)

    o_ref[...] = jnp.dot(h_ref[...], wb_ref[...], preferred_element_type=F32).astype(o_ref.dtype)


def _project(h, w, layer, col0, width, out_dtype):
    n, d = h.shape
    tm = PROJ_ROWS
    tn = PROJ_TILE
    cb = col0 // tn
    return pl.pallas_call(
        _proj_kernel,
        grid=(width // tn, n // tm),
        in_specs=[pl.BlockSpec((tm, d), lambda j, i: (i, 0)),
                  pl.BlockSpec((None, d, tn), lambda j, i: (layer, 0, cb + j))],
        out_specs=pl.BlockSpec((tm, tn), lambda j, i: (i, j)),
        out_shape=jax.ShapeDtypeStruct((n, width), out_dtype),
        scratch_shapes=[pltpu.VMEM((d, tn), BF16)],
        compiler_params=_params(2),
        name="in_proj",
    )(h, w)


def _head_pair_masks():
    lane = lax.broadcasted_iota(I32, (1, LANES), 1)
    first = lane < NA_HEAD_DIM
    return first, jnp.logical_not(first)


def _attn_ctx_kernel(q_ref, k_ref, v_ref, o_ref):
    masks = _head_pair_masks()
    scale = NA_HEAD_DIM ** -0.5
    for p in range(NA_WIDTH // LANES):
        cols = slice(LANES * p, LANES * (p + 1))
        q2 = q_ref[:, cols] * scale
        k2 = k_ref[:, cols].astype(BF16)
        v2 = v_ref[:, cols].astype(BF16)
        outs = []
        for m in masks:
            qa = jnp.where(m, q2, 0.0).astype(BF16)
            s = lax.dot_general(qa, k2, (((1,), (1,)), ((), ())), preferred_element_type=F32)
            e = jnp.exp(s - jnp.max(s, axis=-1, keepdims=True))
            den = jnp.sum(e, axis=-1, keepdims=True)
            outs.append(jnp.dot(e.astype(BF16), v2, preferred_element_type=F32) / den)
        o_ref[:, cols] = jnp.where(masks[0], outs[0], outs[1]).astype(o_ref.dtype)


def _attention_ctx(z_qkv, n_seq, seq):
    return pl.pallas_call(
        _attn_ctx_kernel,
        grid=(n_seq,),
        in_specs=[pl.BlockSpec((seq, NA_WIDTH), lambda b: (b, 0)),
                  pl.BlockSpec((seq, NA_WIDTH), lambda b: (b, 1)),
                  pl.BlockSpec((seq, NA_WIDTH), lambda b: (b, 2))],
        out_specs=pl.BlockSpec((seq, NA_WIDTH), lambda b: (b, 0)),
        out_shape=jax.ShapeDtypeStruct((n_seq * seq, NA_WIDTH), BF16),
        compiler_params=_params(1),
        name="attn_ctx",
    )(z_qkv, z_qkv, z_qkv)


def _attn_lat_kernel(q_ref, k_ref, v_ref, kc_ref, vc_ref, bias_ref, o_ref, kb_ref, vb_ref, *, tq):
    masks = _head_pair_masks()
    scale = NA_HEAD_DIM ** -0.5
    seq = q_ref.shape[0]
    slab = bias_ref.shape[3]
    per_half = seq // 2 // tq
    kb_ref[...] = k_ref[...].astype(BF16)
    vb_ref[...] = v_ref[...].astype(BF16)
    kc = kc_ref[...].astype(BF16)
    vc = vc_ref[...].astype(BF16)
    nt = (((1,), (1,)), ((), ()))

    def q_tile(qi, carry):
        rows = pl.ds(pl.multiple_of(qi * tq, tq), tq)
        half = qi // per_half
        half_rows = pl.ds(pl.multiple_of((qi % per_half) * tq, tq), tq)
        keys = pl.ds(pl.multiple_of(half * (seq - slab), seq - slab), slab)
        k2 = kb_ref[keys, :]
        v2 = vb_ref[keys, :]
        q2 = q_ref[rows, :] * scale
        outs = []
        for hh, m in enumerate(masks):
            qa = jnp.where(m, q2, 0.0).astype(BF16)
            s_lat = lax.dot_general(qa, k2, nt, preferred_element_type=F32) + bias_ref[hh, half, half_rows, :]
            s_ctx = lax.dot_general(qa, kc, nt, preferred_element_type=F32)
            mx = jnp.maximum(jnp.max(s_lat, axis=-1, keepdims=True), jnp.max(s_ctx, axis=-1, keepdims=True))
            e_lat = jnp.exp(s_lat - mx)
            e_ctx = jnp.exp(s_ctx - mx)
            den = jnp.sum(e_lat, axis=-1, keepdims=True) + jnp.sum(e_ctx, axis=-1, keepdims=True)
            o = (jnp.dot(e_lat.astype(BF16), v2, preferred_element_type=F32)
                 + jnp.dot(e_ctx.astype(BF16), vc, preferred_element_type=F32))
            outs.append(o / den)
        o_ref[rows, :] = jnp.where(masks[0], outs[0], outs[1]).astype(o_ref.dtype)
        return carry

    lax.fori_loop(0, q_ref.shape[0] // tq, q_tile, 0)


def _attention_lat(z_qkv, cache_k, cache_v, bias, layer, row0, n_seq, seq):
    past = cache_k.shape[2]
    pairs = NA_WIDTH // LANES
    rb = row0 // seq
    kv_cols = NA_WIDTH // LANES
    slab = bias.shape[-1]
    return pl.pallas_call(
        functools.partial(_attn_lat_kernel, tq=min(seq // 2, ATTN_Q_TILE)),
        grid=(pairs, n_seq),
        in_specs=[pl.BlockSpec((seq, LANES), lambda p, b: (rb + b, p)),
                  pl.BlockSpec((seq, LANES), lambda p, b: (rb + b, kv_cols + p)),
                  pl.BlockSpec((seq, LANES), lambda p, b: (rb + b, 2 * kv_cols + p)),
                  pl.BlockSpec((None, None, past, LANES), lambda p, b: (b, layer, 0, p)),
                  pl.BlockSpec((None, None, past, LANES), lambda p, b: (b, layer, 0, p)),
                  pl.BlockSpec((None, 2, 2, seq // 2, slab), lambda p, b: (layer, p, 0, 0, 0))],
        out_specs=pl.BlockSpec((seq, LANES), lambda p, b: (b, p)),
        out_shape=jax.ShapeDtypeStruct((n_seq * seq, NA_WIDTH), BF16),
        scratch_shapes=[pltpu.VMEM((seq, LANES), BF16), pltpu.VMEM((seq, LANES), BF16)],
        compiler_params=_params(2),
        name="attn_lat",
    )(z_qkv, z_qkv, z_qkv, cache_k, cache_v, bias)


def _neighbourhood_bias(rpb, seq):
    rows = seq // GRID_W
    kh = WIN_H
    assert rows >= WIN_H
    lead = rpb.shape[:-2]
    c = np.arange(GRID_W)
    q_cs = np.clip(c - WIN_W // 2, 0, GRID_W - WIN_W)
    col_ok = (c[None, :] >= q_cs[:, None]) & (c[None, :] < q_cs[:, None] + WIN_W)
    r = np.arange(rows)
    rs = np.clip(r - kh // 2, 0, rows - kh)
    base = np.where(r < rows // 2, 0, rows - KEY_SLAB_ROWS)
    assert (rs >= base).all() and (rs + kh <= base + KEY_SLAB_ROWS).all()
    pick_c = (c[None, None, :] - c[None, :, None] + WIN_W - 1
              == np.arange(2 * WIN_W - 1)[:, None, None]).astype(np.float32)
    w = jnp.einsum("...ij,jqk->...qik", rpb, pick_c, precision=lax.Precision.HIGHEST)
    w = jnp.where(jnp.asarray(col_ok)[:, None, :], w, NEG_INF)
    blocks = []
    for rq in range(rows):
        lo = int(rs[rq]) - rq + WIN_H - 1
        slab = w[..., lo:lo + kh, :].reshape(lead + (GRID_W, kh * GRID_W))
        left = int(rs[rq] - base[rq])
        pad = ((0, 0),) * (len(lead) + 1) + ((left * GRID_W, (KEY_SLAB_ROWS - kh - left) * GRID_W),)
        blocks.append(jnp.pad(slab, pad, constant_values=NEG_INF))
    return jnp.stack(blocks, axis=-3).reshape(lead + (2, seq // 2, KEY_SLAB_ROWS * GRID_W))


def _fourier_kernel(u_ref, ct2_ref, cc_ref, sc_ref, o_ref, pq_ref):
    t = u_ref.shape[0]
    for g in range(FOURIER_GROUPS):
        cols = slice(FOURIER_GROUP_DIM * g, FOURIER_GROUP_DIM * (g + 1))
        ug = u_ref[:, cols].astype(BF16)
        pq_ref[0:t, cols] = jnp.dot(ug, cc_ref[...], preferred_element_type=F32).astype(BF16)
        pq_ref[t:2 * t, cols] = jnp.dot(ug, sc_ref[...], preferred_element_type=F32).astype(BF16)
    o_ref[...] = jnp.dot(ct2_ref[...], pq_ref[...], preferred_element_type=F32).astype(o_ref.dtype)


def _dft_tables(t):
    def cs(n):
        k = np.arange(n, dtype=np.int64)
        ang = 2.0 * np.pi * ((k[:, None] * k[None, :]) % n).astype(np.float64) / n
        return np.cos(ang) / np.sqrt(n), np.sin(ang) / np.sqrt(n)

    ct, st = cs(t)
    cc, sc = cs(FOURIER_GROUP_DIM)
    ct2 = np.concatenate([ct, -st], axis=1).astype(np.float32)
    return (jnp.asarray(ct2).astype(BF16), jnp.asarray(cc.astype(np.float32)).astype(BF16),
            jnp.asarray(sc.astype(np.float32)).astype(BF16))


def _fourier(z_qkvu, row0, n_seq, seq):
    ct2, cc, sc = _dft_tables(seq)
    width = FOURIER_GROUPS * FOURIER_GROUP_DIM
    rb = row0 // seq
    ucol = 3 * NA_WIDTH // width
    return pl.pallas_call(
        _fourier_kernel,
        grid=(n_seq,),
        in_specs=[pl.BlockSpec((seq, width), lambda b: (rb + b, ucol)),
                  pl.BlockSpec((seq, 2 * seq), lambda b: (0, 0)),
                  pl.BlockSpec((FOURIER_GROUP_DIM, FOURIER_GROUP_DIM), lambda b: (0, 0)),
                  pl.BlockSpec((FOURIER_GROUP_DIM, FOURIER_GROUP_DIM), lambda b: (0, 0))],
        out_specs=pl.BlockSpec((seq, width), lambda b: (b, 0)),
        out_shape=jax.ShapeDtypeStruct((n_seq * seq, width), BF16),
        scratch_shapes=[pltpu.VMEM((2 * seq, width), BF16)],
        compiler_params=_params(1),
        name="fourier",
    )(z_qkvu, ct2, cc, sc)


def _rotary_tables(t):
    pos = np.arange(t)
    row = (pos // GRID_W).astype(np.float64)
    col = (pos % GRID_W).astype(np.float64)
    nf = RET_KEY_DIM // 4
    inv_freq = ROPE_BASE ** (-np.arange(nf, dtype=np.float64) / nf)
    ar = row[:, None] * inv_freq[None]
    ac = col[:, None] * inv_freq[None]
    cos = np.concatenate([np.cos(ar), np.cos(ar), np.cos(ac), np.cos(ac)], axis=1)
    sin = np.concatenate([-np.sin(ar), np.sin(ar), -np.sin(ac), np.sin(ac)], axis=1)
    return jnp.asarray(cos.astype(np.float32)), jnp.asarray(sin.astype(np.float32))


def _ret_kernel(lgf_ref, lgb_ref, *refs, t, tq, layer, rotary, state_in, state_out):
    refs = list(refs)
    q_ref, k_ref, v_ref, g_ref = refs[:4]
    refs = refs[4:]
    if rotary:
        cos_ref, sin_ref = refs[:2]
        refs = refs[2:]
    if state_in:
        sf0_ref, sb0_ref = refs[:2]
        refs = refs[2:]
    o_ref = refs[0]
    refs = refs[1:]
    if state_out:
        sf_ref, sb_ref = refs[:2]
        refs = refs[2:]
    dec_ref, kb_ref = refs

    scale = RET_KEY_DIM ** -0.5
    nq = t // tq
    heads_here = dec_ref.shape[0]

    if rotary:
        lane = lax.broadcasted_iota(I32, (1, LANES), 1)
        low = (lane % (RET_KEY_DIM // 2)) < (RET_KEY_DIM // 4)

        def rot(x, rows):
            swapped = jnp.where(low, pltpu.roll(x, LANES - RET_KEY_DIM // 4, 1), pltpu.roll(x, RET_KEY_DIM // 4, 1))
            return x * cos_ref[rows, :] + swapped * sin_ref[rows, :]
    else:
        def rot(x, rows):
            return x

    def one_head(hh):
        h = pl.program_id(0) * heads_here + hh
        cols = slice(LANES * hh, LANES * (hh + 1))
        lgf = lgf_ref[layer * RET_HEADS + h]
        lgb = lgb_ref[layer * RET_HEADS + h]

        @pl.when(pl.program_id(1) == 0)
        def _():
            def fill(ri, c):
                rows = pl.ds(pl.multiple_of(ri * tq, tq), tq)
                i = lax.broadcasted_iota(I32, (tq, t), 0) + ri * tq
                j = lax.broadcasted_iota(I32, (tq, t), 1)
                d = (i - j).astype(F32)
                m = jnp.exp(jnp.abs(d) * jnp.where(d > 0, lgf, lgb))
                dec_ref[hh, rows, :] = jnp.where(d == 0, 2.0, m)
                return c

            lax.fori_loop(0, nq, fill, 0)

        kr = rot(k_ref[:, cols].astype(F32), slice(0, t))
        kb_ref[hh] = kr.astype(BF16)
        vb = v_ref[:, cols]

        if state_out:
            j = lax.broadcasted_iota(I32, (t, 1), 0).astype(F32)
            tn = (((0,), (0,)), ((), ()))
            kf = (kr * (scale * jnp.exp(lgf * (t - 1.0 - j)))).astype(BF16)
            kbw = (kr * (scale * jnp.exp(lgb * j))).astype(BF16)
            sf = lax.dot_general(kf, vb, tn, preferred_element_type=F32)
            sb = lax.dot_general(kbw, vb, tn, preferred_element_type=F32)
            if state_in:
                sf = sf + jnp.exp(lgf * t) * sf0_ref[hh]
                sb = sb + jnp.exp(lgb * t) * sb0_ref[hh]
            sf_ref[hh] = sf
            sb_ref[hh] = sb

        def q_tile(qi, carry):
            r0 = pl.multiple_of(qi * tq, tq)
            rows = pl.ds(r0, tq)
            qr = rot(q_ref[rows, cols].astype(F32), rows)
            s = lax.dot_general((qr * scale).astype(BF16), kb_ref[hh], (((1,), (1,)), ((), ())),
                                preferred_element_type=F32)
            y = jnp.dot((s * dec_ref[hh, rows, :]).astype(BF16), vb, preferred_element_type=F32)
            if state_in:
                pos = (lax.broadcasted_iota(I32, (tq, 1), 0) + r0).astype(F32)
                qf = (qr * jnp.exp(lgf * (pos + 1.0))).astype(BF16)
                qb = (qr * jnp.exp(lgb * (t - pos))).astype(BF16)
                y = (y + jnp.dot(qf, sf0_ref[hh].astype(BF16), preferred_element_type=F32)
                     + jnp.dot(qb, sb0_ref[hh].astype(BF16), preferred_element_type=F32))
            mean = jnp.mean(y, axis=-1, keepdims=True)
            yc = y - mean
            yn = yc * lax.rsqrt(jnp.mean(yc * yc, axis=-1, keepdims=True) + EPS)
            g = g_ref[rows, cols].astype(F32)
            o_ref[rows, cols] = (g * jax.nn.sigmoid(g) * yn).astype(o_ref.dtype)
            return carry

        lax.fori_loop(0, nq, q_tile, 0)

    for hh in range(heads_here):
        one_head(hh)


def _retention(z_ret, lg_f, lg_b, layer, row0, n_seq, seq, *, rotary, states=None, state_out=False):
    rb = row0 // seq
    tq = min(seq, RET_Q_TILE)
    state_in = states is not None
    hps = RET_HEADS if seq * seq * RET_HEADS * 4 <= RET_DECAY_BYTES else 1
    cb = BRANCH_WIDTH // (hps * LANES)
    width = hps * LANES
    in_specs = [pl.BlockSpec((seq, width), lambda h, b, *_: (rb + b, 0 * cb + h)),
                pl.BlockSpec((seq, width), lambda h, b, *_: (rb + b, 1 * cb + h)),
                pl.BlockSpec((seq, width), lambda h, b, *_: (rb + b, 2 * cb + h)),
                pl.BlockSpec((seq, width), lambda h, b, *_: (rb + b, 3 * cb + h))]
    args = [z_ret, z_ret, z_ret, z_ret]
    if rotary:
        cos, sin = _rotary_tables(seq)
        in_specs += [pl.BlockSpec((seq, LANES), lambda h, b, *_: (0, 0))] * 2
        args += [cos, sin]
    if state_in:
        st_spec = pl.BlockSpec((None, None, hps, RET_KEY_DIM, RET_KEY_DIM), lambda h, b, *_: (b, layer, h, 0, 0))
        in_specs += [st_spec, st_spec]
        args += list(states)
    out_specs = [pl.BlockSpec((seq, width), lambda h, b, *_: (b, h))]
    out_shape = [jax.ShapeDtypeStruct((n_seq * seq, RET_HEADS * LANES), BF16)]
    if state_out:
        so = pl.BlockSpec((None, hps, RET_KEY_DIM, RET_KEY_DIM), lambda h, b, *_: (b, h, 0, 0))
        out_specs += [so, so]
        out_shape += [jax.ShapeDtypeStruct((n_seq, RET_HEADS, RET_KEY_DIM, RET_KEY_DIM), F32)] * 2
    return pl.pallas_call(
        functools.partial(_ret_kernel, t=seq, tq=tq, layer=layer, rotary=rotary, state_in=state_in,
                          state_out=state_out),
        grid_spec=pltpu.PrefetchScalarGridSpec(
            num_scalar_prefetch=2, grid=(RET_HEADS // hps, n_seq),
            in_specs=in_specs, out_specs=out_specs,
            scratch_shapes=[pltpu.VMEM((hps, seq, seq), F32), pltpu.VMEM((hps, seq, LANES), BF16)]),
        out_shape=out_shape,
        compiler_params=_params(2),
        name="retention",
    )(lg_f, lg_b, *args)


def _split_dot_nt(w, x):
    nt = (((1,), (1,)), ((), ()))
    w_hi = w.astype(BF16)
    w_lo = (w - w_hi.astype(F32)).astype(BF16)
    x_hi = x.astype(BF16)
    x_lo = (x - x_hi.astype(F32)).astype(BF16)
    return (lax.dot_general(w_hi, x_hi, nt, preferred_element_type=F32)
            + lax.dot_general(w_hi, x_lo, nt, preferred_element_type=F32)
            + lax.dot_general(w_lo, x_hi, nt, preferred_element_type=F32))


def _finish_kernel(tmod_ref, *refs, ctx_tiles, n_x):
    del tmod_ref
    branch_refs, refs = refs[:6], refs[6:]
    zg_ref, refs = refs[0], refs[1:]
    x_refs, refs = refs[:n_x], refs[n_x:]
    (mod_ref, gpost_ref, gpre_ref, wb_ref, wo_ref, wrt_ref, br_ref, tri_ref, ltri_ref,
     x1_ref, h2_ref, wts_ref, lpos_ref, seg_ref, off_ref) = refs
    d = D_MODEL

    def branch(j):
        return _load_rows(branch_refs[2 * j:2 * j + 2], ctx_tiles)

    def gate(j):
        return jax.nn.sigmoid(zg_ref[:, d * j:d * (j + 1)].astype(F32))

    merged = (gate(0) * jnp.dot(branch(0), wb_ref[0], preferred_element_type=F32)
              + gate(1) * jnp.dot(branch(1), wb_ref[1], preferred_element_type=F32)
              + gate(2) * jnp.dot(branch(2), wb_ref[2], preferred_element_type=F32))
    y = jnp.dot(merged.astype(BF16), wo_ref[...], preferred_element_type=F32)
    x1 = _load_rows(x_refs, ctx_tiles) + mod_ref[2:3, :] * (y * _rms(y) * gpost_ref[...])
    x1_ref[...] = x1
    h2 = x1 * _rms(x1) * gpre_ref[...] * (1.0 + mod_ref[4:5, :]) + mod_ref[3:4, :]
    h2_ref[...] = h2

    logits = _split_dot_nt(wrt_ref[...], h2) + br_ref[:, 0:1]
    tm = logits.shape[1]
    eidx = lax.broadcasted_iota(I32, (N_EXPERTS, tm), 0)
    cur = logits
    vals, hots = [], []
    for k in range(TOP_K):
        m = jnp.max(cur, axis=0, keepdims=True)
        sel = jnp.min(jnp.where(cur == m, eidx, N_EXPERTS), axis=0, keepdims=True)
        hot = eidx == sel
        vals.append(m)
        hots.append(hot)
        cur = jnp.where(hot, -jnp.inf, cur)
    exps = [jnp.exp(v - vals[0]) for v in vals]
    den = exps[0] + exps[1] + exps[2] + exps[3]
    for k in range(TOP_K):
        wts_ref[k:k + 1, :] = exps[k] / den

    member = jnp.logical_or(jnp.logical_or(hots[0], hots[1]), jnp.logical_or(hots[2], hots[3]))
    member_f = member.astype(F32)
    before = jnp.dot(member_f.astype(BF16), tri_ref[...], preferred_element_type=F32)
    units = jnp.ceil(jnp.sum(member_f, axis=1, keepdims=True) * (1.0 / SEG_ALIGN))
    units = jnp.broadcast_to(units, seg_ref.shape)
    off = jnp.dot(ltri_ref[...], units.astype(BF16), preferred_element_type=F32) * SEG_ALIGN
    seg_ref[...] = units * SEG_ALIGN
    off_ref[...] = off
    place = before + off[:, 0:1]
    for k in range(TOP_K):
        lpos_ref[k:k + 1, :] = jnp.sum(jnp.where(hots[k], place, 0.0), axis=0, keepdims=True).astype(I32)


def _finish(branches, z_gate, x, mod, g_post, g_pre_ffn, w_branch, w_out, w_router_t, b_router, layer, tile_mod):
    tm = FIN_TILE
    x_specs, x_args, _, n = _token_rows(x, tm)
    d = x_args[0].shape[1]
    tri = jnp.asarray(np.triu(np.ones((tm, tm), np.float32), k=1)).astype(BF16)
    ltri = jnp.asarray(np.tril(np.ones((N_EXPERTS, N_EXPERTS), np.float32), k=-1)).astype(BF16)
    row = lambda i, t: (i, 0)
    const2 = lambda i, t: (0, 0)
    lay3 = lambda i, t: (layer, 0, 0)
    col = lambda i, t: (0, i)
    branch_specs, branch_args = [], []
    for pair in branches:
        specs, args, ctx_tiles, _ = _token_rows(pair, tm)
        branch_specs += specs
        branch_args += args
    outs = pl.pallas_call(
        functools.partial(_finish_kernel, ctx_tiles=ctx_tiles, n_x=len(x_args)),
        grid_spec=pltpu.PrefetchScalarGridSpec(
            num_scalar_prefetch=1, grid=(n // tm,),
            in_specs=branch_specs + [pl.BlockSpec((tm, GATE_W), row)] + x_specs + [
                      pl.BlockSpec((None, None, 6, d), lambda i, t: (layer, t[i], 0, 0)),
                      pl.BlockSpec((None, 1, d), lay3),
                      pl.BlockSpec((None, 1, d), lay3),
                      pl.BlockSpec((None, 3, BRANCH_WIDTH, d), lambda i, t: (layer, 0, 0, 0)),
                      pl.BlockSpec((None, d, d), lay3),
                      pl.BlockSpec((None, N_EXPERTS, d), lay3),
                      pl.BlockSpec((None, N_EXPERTS, LANES), lay3),
                      pl.BlockSpec((tm, tm), const2),
                      pl.BlockSpec((N_EXPERTS, N_EXPERTS), const2)],
            out_specs=[pl.BlockSpec((tm, d), row),
                       pl.BlockSpec((tm, d), row),
                       pl.BlockSpec((TOP_K, tm), col),
                       pl.BlockSpec((TOP_K, tm), col),
                       pl.BlockSpec((None, N_EXPERTS, LANES), lambda i, t: (i, 0, 0)),
                       pl.BlockSpec((None, N_EXPERTS, LANES), lambda i, t: (i, 0, 0))]),
        out_shape=[jax.ShapeDtypeStruct((n, d), F32),
                   jax.ShapeDtypeStruct((n, d), F32),
                   jax.ShapeDtypeStruct((TOP_K, n), F32),
                   jax.ShapeDtypeStruct((TOP_K, n), I32),
                   jax.ShapeDtypeStruct((n // tm, N_EXPERTS, LANES), F32),
                   jax.ShapeDtypeStruct((n // tm, N_EXPERTS, LANES), F32)],
        compiler_params=_params(1),
        name="merge_router",
    )(tile_mod, *branch_args, z_gate, *x_args, mod, g_post, g_pre_ffn, w_branch, w_out, w_router_t, b_router,
      tri, ltri)
    return outs


def _segment_chunks(length, src_ref, src0, dst_ref, dst0, sem, max_chunk, fixed_src=False):
    out = []
    chunk = max_chunk
    while chunk >= SEG_ALIGN:
        done = jnp.bitwise_and(length, ~(2 * chunk - 1))
        present = jnp.bitwise_and(length, chunk) != 0
        s = 0 if fixed_src else pl.multiple_of(src0 + done, SEG_ALIGN)
        dd = pl.multiple_of(dst0 + done, SEG_ALIGN)
        out.append((present, pltpu.make_async_copy(src_ref.at[pl.ds(s, chunk)], dst_ref.at[pl.ds(dd, chunk)], sem)))
        chunk //= 2
    return out


def _for_each_chunk(n_segments, chunks_of, action):
    def body(e, c):
        for present, cp in chunks_of(e):
            pl.when(present)(functools.partial(action, cp))
        return c

    lax.fori_loop(0, n_segments, body, 0, unroll=4)


def _start(cp):
    cp.start()


def _wait(cp):
    cp.wait()


def _wait_rows(total, src_ref, dst_ref, sem):
    chunk = pl.next_power_of_2(SORT_ROWS) // 2
    while chunk >= SEG_ALIGN:
        @pl.when(jnp.bitwise_and(total, chunk) != 0)
        def _(chunk=chunk):
            pltpu.make_async_copy(src_ref.at[pl.ds(0, chunk)], dst_ref.at[pl.ds(0, chunk)], sem).wait()
        chunk //= 2


def _offset_in_chunk(rows, c0):
    assert SORT_CHUNK == 256
    inside = lax.shift_right_logical(rows, 8) == c0 // SORT_CHUNK
    return jnp.where(inside, jnp.bitwise_and(rows, SORT_CHUNK - 1), -1).astype(BF16)


def _scatter_kernel(seg_ref, off_ref, pos_ref, used_ref, tpos_ref, tlen_ref, h_ref, lpos_ref, xe_hbm,
                    buf_ref, zero_ref, sem, *, tm):
    t = pl.program_id(0)
    slot = t % 2
    hb = h_ref[...].astype(BF16)
    lp = [lpos_ref[k:k + 1, :] for k in range(TOP_K)]
    rows = buf_ref.shape[1]
    r_off = lax.broadcasted_iota(I32, (SORT_CHUNK, tm), 0).astype(BF16)
    one = jnp.ones((SORT_CHUNK, tm), BF16)
    zero = jnp.zeros((SORT_CHUNK, tm), BF16)
    for c0 in range(0, rows, SORT_CHUNK):
        here = [_offset_in_chunk(p, c0) for p in lp]
        hit = jnp.logical_or(jnp.logical_or(r_off == here[0], r_off == here[1]),
                             jnp.logical_or(r_off == here[2], r_off == here[3]))
        onehot = jnp.where(hit, one, zero)
        buf_ref[slot, c0:c0 + SORT_CHUNK, :] = jnp.dot(onehot, hb, preferred_element_type=F32)

    def segments_of(tile):
        def segment(e):
            j = tile * N_EXPERTS + e
            return _segment_chunks(seg_ref[j], buf_ref.at[tile % 2], off_ref[j], xe_hbm, pos_ref[j],
                                   sem.at[tile % 2], tm)
        return segment

    @pl.when(t > 0)
    def _():
        _wait_rows(used_ref[t - 1], buf_ref.at[1 - slot], xe_hbm, sem.at[1 - slot])

    _for_each_chunk(N_EXPERTS, segments_of(t), _start)

    @pl.when(t == pl.num_programs(0) - 1)
    def _():
        _wait_rows(used_ref[t], buf_ref.at[slot], xe_hbm, sem.at[slot])
        zero_ref[...] = jnp.zeros_like(zero_ref)

        def tail(e):
            return _segment_chunks(tlen_ref[e], zero_ref, 0, xe_hbm, tpos_ref[e], sem.at[0], zero_ref.shape[0],
                                   fixed_src=True)

        _for_each_chunk(N_EXPERTS, tail, _start)
        _for_each_chunk(N_EXPERTS, tail, _wait)


def _scatter_rows(h2, lpos, plan, n_rows):
    n, d = h2.shape
    tm = FIN_TILE
    return pl.pallas_call(
        functools.partial(_scatter_kernel, tm=tm),
        grid_spec=pltpu.PrefetchScalarGridSpec(
            num_scalar_prefetch=6, grid=(n // tm,),
            in_specs=[pl.BlockSpec((tm, d), lambda i, *_: (i, 0)),
                      pl.BlockSpec((TOP_K, tm), lambda i, *_: (0, i))],
            out_specs=pl.BlockSpec(memory_space=pl.ANY),
            scratch_shapes=[pltpu.VMEM((2, SORT_ROWS, d), F32), pltpu.VMEM((MOE_BLOCK // 2, d), F32),
                            pltpu.SemaphoreType.DMA((2,))]),
        out_shape=jax.ShapeDtypeStruct((n_rows, d), F32),
        compiler_params=_params(1),
        name="moe_scatter",
    )(plan["seg"], plan["off"], plan["pos"], plan["used"], plan["tail_pos"], plan["tail_len"], h2, lpos)


def _expert_kernel(be_ref, nxt_ref, valid_ref, nu_ref, xb_ref, w1_hbm, b1_ref, w2_hbm, b2_ref, yb_ref,
                   w1s_ref, w2s_ref, w1b_ref, w2b_ref, sem, *, layer):
    i = pl.program_id(0)
    used = i < nu_ref[0]
    fresh = jnp.logical_or(i == 0, be_ref[i] != be_ref[jnp.maximum(i - 1, 0)])

    def fetch(e):
        return (pltpu.make_async_copy(w1_hbm.at[layer, e], w1s_ref, sem.at[0]),
                pltpu.make_async_copy(w2_hbm.at[layer, e], w2s_ref, sem.at[1]))

    @pl.when(i == 0)
    def _():
        for cp in fetch(be_ref[0]):
            cp.start()

    @pl.when(jnp.logical_and(used, fresh))
    def _():
        for cp in fetch(be_ref[i]):
            cp.wait()
        w1b_ref[...] = w1s_ref[...].astype(BF16)
        w2b_ref[...] = w2s_ref[...].astype(BF16)

        @pl.when(nxt_ref[i] >= 0)
        def _():
            for cp in fetch(nxt_ref[i]):
                cp.start()

    def ffn(m):
        e = be_ref[i]
        z = jnp.dot(xb_ref[0:m, :].astype(BF16), w1b_ref[...], preferred_element_type=F32) + b1_ref[e]
        glu = jnp.minimum(z[:, :D_FF], SWIGLU_LIMIT)
        lin = jnp.clip(z[:, D_FF:], -SWIGLU_LIMIT, SWIGLU_LIMIT)
        act = glu * jax.nn.sigmoid(SWIGLU_ALPHA * glu) * (lin + 1.0)
        yb_ref[0:m, :] = jnp.dot(act.astype(BF16), w2b_ref[...], preferred_element_type=F32) + b2_ref[e]
        if m < yb_ref.shape[0]:
            yb_ref[m:, :] = jnp.zeros((yb_ref.shape[0] - m, yb_ref.shape[1]), F32)

    valid = valid_ref[i]
    for m in range(EXPERT_ROW_STEP, yb_ref.shape[0] + 1, EXPERT_ROW_STEP):
        pl.when(jnp.logical_and(valid > m - EXPERT_ROW_STEP, valid <= m))(functools.partial(ffn, m))

    @pl.when(valid == 0)
    def _():
        yb_ref[...] = jnp.zeros_like(yb_ref)


def _experts(xb, block_expert, next_expert, valid_rows, n_used, w1, b1, w2, b2, layer):
    n_rows, d = xb.shape
    tm = MOE_BLOCK
    return pl.pallas_call(
        functools.partial(_expert_kernel, layer=layer),
        grid_spec=pltpu.PrefetchScalarGridSpec(
            num_scalar_prefetch=4, grid=(n_rows // tm,),
            in_specs=[pl.BlockSpec((tm, d), lambda i, be, nx, vr, nu: (jnp.minimum(i, nu[0] - 1), 0)),
                      pl.BlockSpec(memory_space=pl.ANY),
                      pl.BlockSpec((None, N_EXPERTS, 1, 2 * D_FF), lambda i, be, nx, vr, nu: (layer, 0, 0, 0)),
                      pl.BlockSpec(memory_space=pl.ANY),
                      pl.BlockSpec((None, N_EXPERTS, 1, d), lambda i, be, nx, vr, nu: (layer, 0, 0, 0))],
            out_specs=pl.BlockSpec((tm, d), lambda i, be, nx, vr, nu: (i, 0)),
            scratch_shapes=[pltpu.VMEM((d, 2 * D_FF), F32), pltpu.VMEM((D_FF, d), F32),
                            pltpu.VMEM((d, 2 * D_FF), BF16), pltpu.VMEM((D_FF, d), BF16),
                            pltpu.SemaphoreType.DMA((2,))]),
        out_shape=jax.ShapeDtypeStruct((n_rows, d), F32),
        compiler_params=_params(1),
        name="moe_experts",
    )(block_expert, next_expert, valid_rows, n_used, xb, w1, b1, w2, b2)


def _gather_kernel(tmod_ref, seg_ref, off_ref, pos_ref, used_ref, ye_hbm, lpos_ref, wts_ref, x1_ref, mod_ref, g_ref,
                   *refs, tm, tile0, prenorm_next):
    del tmod_ref
    if prenorm_next:
        modn_ref, gn_ref, o_ref, hn_ref, buf_ref, sem = refs
    else:
        o_ref, buf_ref, sem = refs
    step = pl.program_id(0)
    t = step + tile0
    slot = step % 2

    def segments_of(tile):
        def segment(e):
            j = tile * N_EXPERTS + e
            half = (tile - tile0) % 2
            return _segment_chunks(seg_ref[j], ye_hbm, pos_ref[j], buf_ref.at[half], off_ref[j], sem.at[half], tm)
        return segment

    @pl.when(step == 0)
    def _():
        buf_ref[...] = jnp.zeros_like(buf_ref)
        _for_each_chunk(N_EXPERTS, segments_of(t), _start)

    @pl.when(step + 1 < pl.num_programs(0))
    def _():
        _for_each_chunk(N_EXPERTS, segments_of(t + 1), _start)

    _wait_rows(used_ref[t], ye_hbm, buf_ref.at[slot], sem.at[slot])

    lp = [lpos_ref[:, k:k + 1] for k in range(TOP_K)]
    wt = [jnp.broadcast_to(wts_ref[:, k:k + 1].astype(BF16), (tm, SORT_CHUNK)) for k in range(TOP_K)]
    c_off = lax.broadcasted_iota(I32, (tm, SORT_CHUNK), 1).astype(BF16)
    y = jnp.zeros(o_ref.shape, F32)
    for c0 in range(0, buf_ref.shape[1], SORT_CHUNK):
        wm = jnp.zeros((tm, SORT_CHUNK), BF16)
        for k in range(TOP_K):
            wm = jnp.where(c_off == _offset_in_chunk(lp[k], c0), wt[k], wm)
        y = y + jnp.dot(wm, buf_ref[slot, c0:c0 + SORT_CHUNK, :].astype(BF16), preferred_element_type=F32)
    x2 = x1_ref[...] + mod_ref[5:6, :] * (y * _rms(y) * g_ref[...])
    o_ref[...] = x2
    if prenorm_next:
        hn = x2 * _rms(x2) * gn_ref[...]
        hn_ref[...] = (hn * (1.0 + modn_ref[1:2, :]) + modn_ref[0:1, :]).astype(hn_ref.dtype)


def _gather_combine(ye, lpos_t, wts_t, plan, x1, mod, g_post, layer, tile_mod, row0=0, rows=None, g_pre_next=None):
    n, d = x1.shape
    tm = FIN_TILE
    rows = n if rows is None else rows
    tile0 = row0 // tm
    row = lambda i, *_: (tile0 + i, 0)
    out_row = lambda i, *_: (i, 0)
    mod_spec = lambda lay: pl.BlockSpec((None, None, 6, d), lambda i, t, *_: (lay, t[tile0 + i], 0, 0))
    gain_spec = lambda lay: pl.BlockSpec((None, 1, d), lambda i, *_: (lay, 0, 0))
    prenorm_next = g_pre_next is not None
    in_specs = [pl.BlockSpec(memory_space=pl.ANY),
                pl.BlockSpec((tm, TOP_K), row),
                pl.BlockSpec((tm, TOP_K), row),
                pl.BlockSpec((tm, d), row),
                mod_spec(layer), gain_spec(layer)]
    args = [ye, lpos_t, wts_t, x1, mod, g_post]
    out_specs = [pl.BlockSpec((tm, d), out_row)]
    out_shape = [jax.ShapeDtypeStruct((rows, d), F32)]
    if prenorm_next:
        in_specs += [mod_spec(layer + 1), gain_spec(layer + 1)]
        args += [mod, g_pre_next]
        out_specs += [pl.BlockSpec((tm, d), out_row)]
        out_shape += [jax.ShapeDtypeStruct((rows, d), BF16)]
    outs = pl.pallas_call(
        functools.partial(_gather_kernel, tm=tm, tile0=tile0, prenorm_next=prenorm_next),
        grid_spec=pltpu.PrefetchScalarGridSpec(
            num_scalar_prefetch=5, grid=(rows // tm,),
            in_specs=in_specs, out_specs=out_specs,
            scratch_shapes=[pltpu.VMEM((2, SORT_ROWS, d), F32), pltpu.SemaphoreType.DMA((2,))]),
        out_shape=out_shape,
        compiler_params=_params(1),
        name="moe_gather",
    )(tile_mod, plan["seg"], plan["off"], plan["pos"], plan["used"], *args)
    return outs if prenorm_next else outs[0]


def _moe(h2, wts, lpos, seg, off, x1, mod, g_post, w1, b1, w2, b2, layer, tile_mod, split_rows=None,
         g_pre_next=None):
    n, d = h2.shape
    blk = MOE_BLOCK
    tm = FIN_TILE
    tiles = n // tm
    n_rows = -(-(n * TOP_K + tiles * N_EXPERTS * (SEG_ALIGN - 1) + N_EXPERTS * (blk - 1)) // blk) * blk
    n_blocks = n_rows // blk
    seg = seg[:, :, 0].astype(I32)
    off = off[:, :, 0].astype(I32)
    rows_e = jnp.sum(seg, axis=0)
    region = (rows_e + blk - 1) // blk * blk
    pend = jnp.cumsum(region)
    pstart = pend - region
    pos = pstart[None, :] + jnp.cumsum(seg, axis=0) - seg
    plan = {"seg": seg.reshape(-1), "off": off.reshape(-1), "pos": pos.reshape(-1).astype(I32),
            "used": jnp.sum(seg, axis=1).astype(I32),
            "tail_pos": (pstart + rows_e).astype(I32), "tail_len": (region - rows_e).astype(I32)}
    blocks = jnp.arange(n_blocks, dtype=I32) * blk
    block_expert = jnp.minimum(jnp.sum(blocks[:, None] >= pend[None, :], axis=1), N_EXPERTS - 1).astype(I32)
    n_used = (pend[-1:] // blk).astype(I32)
    valid_rows = jnp.clip((pstart + rows_e)[block_expert] - blocks, 0, blk).astype(I32)
    ids = jnp.arange(n_blocks, dtype=I32)
    run_start = jnp.logical_and(jnp.concatenate([jnp.ones((1,), bool), block_expert[1:] != block_expert[:-1]]),
                                ids < n_used[0])
    first_after = lax.cummin(jnp.where(run_start, ids, n_blocks)[::-1])[::-1]
    first_after = jnp.concatenate([first_after[1:], jnp.full((1,), n_blocks, I32)])
    next_expert = jnp.where(first_after < n_blocks, block_expert[jnp.minimum(first_after, n_blocks - 1)], -1)
    xe = _scatter_rows(h2, lpos, plan, n_rows)
    ye = _experts(xe, block_expert, next_expert.astype(I32), valid_rows, n_used, w1, b1, w2, b2, layer)
    combine = functools.partial(_gather_combine, ye, lpos.T, wts.T, plan, x1, mod, g_post, layer, tile_mod)
    if split_rows is None:
        return combine(g_pre_next=g_pre_next)
    return combine(row0=0, rows=split_rows), combine(row0=split_rows, rows=n - split_rows)


def _tile_mod_ids(n_ctx_rows, n_lat_rows, lat_seq, tm):
    ctx = np.zeros((n_ctx_rows // tm,), np.int32)
    lat = 1 + (np.arange(n_lat_rows // tm) * tm) // lat_seq
    return jnp.asarray(np.concatenate([ctx, lat.astype(np.int32)]))


def kernel(x_prompt, x_sample, cache_k, cache_v, state_ret_fwd, state_ret_bwd, c, c_ctx, w_mod, b_mod, g_pre_mix, g_post_mix, g_pre_ffn, g_post_ffn, w_in, na_rel_bias, ret_decay_fwd, ret_decay_bwd, w_branch, w_out, w_router, b_router, w_exp_in, b_exp_in, w_exp_out, b_exp_out):
    batch, seq, d = x_prompt.shape
    dec_batch, dec_seq, _ = x_sample.shape
    depth = w_in.shape[0]
    n_ctx = batch * seq
    n_lat = dec_batch * dec_seq
    assert 1 + dec_batch <= MOD_ROWS

    x = (x_prompt.reshape(n_ctx, d), x_sample.reshape(n_lat, d))
    cvec =jnp.concatenate([c_ctx[None], c, jnp.zeros((MOD_ROWS - 1 - dec_batch, d), F32)], axis=0)
    mod_all = _modulation(cvec, w_mod, b_mod).reshape(depth, MOD_ROWS, 6, d)
    tmod = {tm: _tile_mod_ids(n_ctx, n_lat, dec_seq, tm) for tm in (ROW_TILE, FIN_TILE)}
    past = cache_k.shape[2]
    ck = cache_k.reshape(dec_batch, depth, past, NA_WIDTH)
    cv = cache_v.reshape(dec_batch, depth, past, NA_WIDTH)
    lg_f = jax.nn.log_sigmoid(ret_decay_fwd.astype(F32)).reshape(-1)
    lg_b = jax.nn.log_sigmoid(ret_decay_bwd.astype(F32)).reshape(-1)
    bias_all = _neighbourhood_bias(na_rel_bias, dec_seq)
    vec = lambda g: g.reshape(depth, 1, d)
    g_pre_mix, g_post_mix, g_pre_ffn, g_post_ffn = vec(g_pre_mix), vec(g_post_mix), vec(g_pre_ffn), vec(g_post_ffn)
    w_branch_b = w_branch.astype(BF16)
    w_out_b = w_out.astype(BF16)
    w_router_t = jnp.swapaxes(w_router, 1, 2)
    b_router_l = jnp.broadcast_to(b_router[:, :, None], (depth, N_EXPERTS, LANES))
    b_exp_in = b_exp_in.reshape(depth, N_EXPERTS, 1, 2 * D_FF)
    b_exp_out = b_exp_out.reshape(depth, N_EXPERTS, 1, d)

    ks, vs, sfs, sbs = [], [], [], []
    h = _prenorm(x, g_pre_mix, mod_all, 0, tmod[ROW_TILE], ROW_TILE)
    for l in range(depth):
        last = l == depth - 1
        z_qkvu = _project(h, w_in, l, 0, QKVU_W, F32)
        z_ret = _project(h, w_in, l, QKVU_W, RET_W, BF16)
        z_gate = _project(h, w_in, l, QKVU_W + RET_W, GATE_W, BF16)
        ks.append(z_qkvu[:n_ctx, NA_WIDTH:2 * NA_WIDTH].reshape(batch, seq, NA_HEADS, NA_HEAD_DIM))
        vs.append(z_qkvu[:n_ctx, 2 * NA_WIDTH:3 * NA_WIDTH].reshape(batch, seq, NA_HEADS, NA_HEAD_DIM))

        a_pair = (_attention_ctx(z_qkvu, batch, seq),
                  _attention_lat(z_qkvu, ck, cv, bias_all, l, n_ctx, dec_batch, dec_seq))
        f_pair = (_fourier(z_qkvu, 0, batch, seq), _fourier(z_qkvu, n_ctx, dec_batch, dec_seq))
        r_ctx, s_f, s_b = _retention(z_ret, lg_f, lg_b, l, 0, batch, seq, rotary=False, state_out=True)
        (r_lat,) = _retention(z_ret, lg_f, lg_b, l, n_ctx, dec_batch, dec_seq, rotary=True,
                              states=(state_ret_fwd, state_ret_bwd))
        sfs.append(s_f)
        sbs.append(s_b)

        x1, h2, wts, lpos, seg, off = _finish(
            (a_pair, f_pair, (r_ctx, r_lat)), z_gate, x, mod_all, g_post_mix, g_pre_ffn,
            w_branch_b, w_out_b, w_router_t, b_router_l, l, tmod[FIN_TILE])
        out = _moe(h2, wts, lpos, seg, off, x1, mod_all, g_post_ffn,
                   w_exp_in, b_exp_in, w_exp_out, b_exp_out, l, tmod[FIN_TILE],
                   split_rows=n_ctx if last else None, g_pre_next=None if last else g_pre_mix)
        x, h = (out, None) if last else out

    y_prompt = x[0].reshape(batch, seq, d)
    y_sample = x[1].reshape(dec_batch, dec_seq, d)
    return (y_prompt, y_sample, jnp.stack(ks, axis=1), jnp.stack(vs, axis=1),
            jnp.stack(sfs, axis=1), jnp.stack(sbs, axis=1))
```

```python
import functools

import numpy as np
import jax
import jax.numpy as jnp
from jax import lax
from jax.experimental import pallas as pl
from jax.experimental.pallas import tpu as pltpu

F32 = jnp.float32
BF16 = jnp.bfloat16
I32 = jnp.int32

D_MODEL = 1024
GRID_W = 64
NA_HEADS = 8
NA_HEAD_DIM = 64
NA_WIDTH = NA_HEADS * NA_HEAD_DIM
WIN_H = 8
WIN_W = 16
KEY_SLAB_ROWS = 12
FOURIER_GROUPS = 4
FOURIER_GROUP_DIM = 128
RET_HEADS = 4
RET_KEY_DIM = 128
ROPE_BASE = 10000.0
BRANCH_WIDTH = 512
N_EXPERTS = 32
TOP_K = 4
D_FF = 1024
SWIGLU_LIMIT = 7.0
SWIGLU_ALPHA = 1.702
EPS = 1e-6
NEG_INF = -1e30

QKVU_W = 4 * NA_WIDTH
RET_W = 4 * BRANCH_WIDTH
GATE_W = 3 * D_MODEL
PROJ_TILE = 1024
PROJ_ROWS = 2048
ATTN_Q_TILE = 512
RET_Q_TILE = 1024
RET_DECAY_BYTES = 4 * 1024 * 1024

LANES = 128
MOD_ROWS = 16
ROW_TILE = 1024
FIN_TILE = 512
MOE_BLOCK = 512
EXPERT_ROW_STEP = 128
SEG_ALIGN = 8
SORT_CHUNK = 256
RARE_CHUNK = 128
SORT_ROWS = -(-(FIN_TILE * TOP_K + N_EXPERTS * (SEG_ALIGN - 1)) // SORT_CHUNK) * SORT_CHUNK
VMEM_LIMIT = 56 * 1024 * 1024


def _params(n_axes, vmem=VMEM_LIMIT):
    return pltpu.CompilerParams(dimension_semantics=("arbitrary",) * n_axes, vmem_limit_bytes=vmem)


def _rms(x):
    return lax.rsqrt(jnp.mean(x * x, axis=-1, keepdims=True) + EPS)


def _mod_kernel(cv_ref, w_ref, b_ref, o_ref):
    cv = cv_ref[...]
    s = (cv * jax.nn.sigmoid(cv)).astype(BF16)
    o_ref[...] = jnp.dot(s, w_ref[...].astype(BF16), preferred_element_type=F32) + b_ref[...]


def _modulation(cv, w_mod, b_mod):
    depth, d, n = w_mod.shape
    tn = 1536
    return pl.pallas_call(
        _mod_kernel,
        grid=(depth, n // tn),
        in_specs=[pl.BlockSpec((MOD_ROWS, d), lambda l, j: (0, 0)),
                  pl.BlockSpec((None, d, tn), lambda l, j: (l, 0, j)),
                  pl.BlockSpec((None, 1, tn), lambda l, j: (l, 0, j))],
        out_specs=pl.BlockSpec((None, MOD_ROWS, tn), lambda l, j: (l, 0, j)),
        out_shape=jax.ShapeDtypeStruct((depth, MOD_ROWS, n), F32),
        compiler_params=_params(2),
        name="modulation",
    )(cv, w_mod, b_mod.reshape(depth, 1, n))


def _token_rows(x, tm):
    if isinstance(x, tuple):
        d = x[0].shape[1]
        ct = x[0].shape[0] // tm
        specs = [pl.BlockSpec((tm, d), lambda i, *_: (jnp.minimum(i, ct - 1), 0)),
                 pl.BlockSpec((tm, d), lambda i, *_: (jnp.maximum(i - ct, 0), 0))]
        return specs, list(x), ct, x[0].shape[0] + x[1].shape[0]
    return [pl.BlockSpec((tm, x.shape[1]), lambda i, *_: (i, 0))], [x], None, x.shape[0]


def _load_rows(refs, ctx_tiles):
    if len(refs) == 1:
        return refs[0][...]
    return jnp.where(pl.program_id(0) < ctx_tiles, refs[0][...], refs[1][...])


def _prenorm_kernel(tmod_ref, *refs, ctx_tiles):
    del tmod_ref
    g_ref, mod_ref, o_ref = refs[-3:]
    x = _load_rows(refs[:-3], ctx_tiles)
    h = x * _rms(x) * g_ref[...]
    o_ref[...] = (h * (1.0 + mod_ref[1:2, :]) + mod_ref[0:1, :]).astype(o_ref.dtype)


def _prenorm(x, g, mod, layer, tile_mod, tm):
    x_specs, x_args, ctx_tiles, n = _token_rows(x, tm)
    d = x_args[0].shape[1]
    return pl.pallas_call(
        functools.partial(_prenorm_kernel, ctx_tiles=ctx_tiles),
        grid_spec=pltpu.PrefetchScalarGridSpec(
            num_scalar_prefetch=1, grid=(n // tm,),
            in_specs=x_specs + [pl.BlockSpec((None, 1, d), lambda i, t: (layer, 0, 0)),
                                pl.BlockSpec((None, None, 6, d), lambda i, t: (layer, t[i], 0, 0))],
            out_specs=pl.BlockSpec((tm, d), lambda i, t: (i, 0))),
        out_shape=jax.ShapeDtypeStruct((n, d), BF16),
        compiler_params=_params(1),
        name="prenorm",
    )(tile_mod, *x_args, g, mod)


def _proj_kernel(h_ref, w_ref, o_ref, wb_ref):
    @pl.when(pl.program_id(1) == 0)
    def _():
        wb_ref[...] = w_ref[...].astype(BF16)

    o_ref[...] = jnp.dot(h_ref[...], wb_ref[...], preferred_element_type=F32).astype(o_ref.dtype)


def _project(h, w, layer, col0, width, out_dtype):
    n, d = h.shape
    tm = PROJ_ROWS
    tn = PROJ_TILE
    cb = col0 // tn
    return pl.pallas_call(
        _proj_kernel,
        grid=(width // tn, n // tm),
        in_specs=[pl.BlockSpec((tm, d), lambda j, i: (i, 0)),
                  pl.BlockSpec((None, d, tn), lambda j, i: (layer, 0, cb + j))],
        out_specs=pl.BlockSpec((tm, tn), lambda j, i: (i, j)),
        out_shape=jax.ShapeDtypeStruct((n, width), out_dtype),
        scratch_shapes=[pltpu.VMEM((d, tn), BF16)],
        compiler_params=_params(2),
        name="in_proj",
    )(h, w)


def _head_pair_masks():
    lane = lax.broadcasted_iota(I32, (1, LANES), 1)
    first = lane < NA_HEAD_DIM
    return first, jnp.logical_not(first)


def _attn_ctx_kernel(q_ref, k_ref, v_ref, o_ref):
    masks = _head_pair_masks()
    scale = NA_HEAD_DIM ** -0.5
    for p in range(NA_WIDTH // LANES):
        cols = slice(LANES * p, LANES * (p + 1))
        q2 = q_ref[:, cols] * scale
        k2 = k_ref[:, cols].astype(BF16)
        v2 = v_ref[:, cols].astype(BF16)
        outs = []
        for m in masks:
            qa = jnp.where(m, q2, 0.0).astype(BF16)
            s = lax.dot_general(qa, k2, (((1,), (1,)), ((), ())), preferred_element_type=F32)
            e = jnp.exp(s - jnp.max(s, axis=-1, keepdims=True))
            den = jnp.sum(e, axis=-1, keepdims=True)
            outs.append(jnp.dot(e.astype(BF16), v2, preferred_element_type=F32) / den)
        o_ref[:, cols] = jnp.where(masks[0], outs[0], outs[1]).astype(o_ref.dtype)


def _attention_ctx(z_qkv, n_seq, seq):
    return pl.pallas_call(
        _attn_ctx_kernel,
        grid=(n_seq,),
        in_specs=[pl.BlockSpec((seq, NA_WIDTH), lambda b: (b, 0)),
                  pl.BlockSpec((seq, NA_WIDTH), lambda b: (b, 1)),
                  pl.BlockSpec((seq, NA_WIDTH), lambda b: (b, 2))],
        out_specs=pl.BlockSpec((seq, NA_WIDTH), lambda b: (b, 0)),
        out_shape=jax.ShapeDtypeStruct((n_seq * seq, NA_WIDTH), BF16),
        compiler_params=_params(1),
        name="attn_ctx",
    )(z_qkv, z_qkv, z_qkv)


def _attn_lat_kernel(q_ref, k_ref, v_ref, kc_ref, vc_ref, bias_ref, o_ref, kb_ref, vb_ref, *, tq):
    masks = _head_pair_masks()
    scale = NA_HEAD_DIM ** -0.5
    seq = q_ref.shape[0]
    slab = bias_ref.shape[3]
    per_half = seq // 2 // tq
    kb_ref[...] = k_ref[...].astype(BF16)
    vb_ref[...] = v_ref[...].astype(BF16)
    kc = kc_ref[...].astype(BF16)
    vc = vc_ref[...].astype(BF16)
    nt = (((1,), (1,)), ((), ()))

    def q_tile(qi, carry):
        rows = pl.ds(pl.multiple_of(qi * tq, tq), tq)
        half = qi // per_half
        half_rows = pl.ds(pl.multiple_of((qi % per_half) * tq, tq), tq)
        keys = pl.ds(pl.multiple_of(half * (seq - slab), seq - slab), slab)
        k2 = kb_ref[keys, :]
        v2 = vb_ref[keys, :]
        q2 = q_ref[rows, :] * scale
        outs = []
        for hh, m in enumerate(masks):
            qa = jnp.where(m, q2, 0.0).astype(BF16)
            s_lat = lax.dot_general(qa, k2, nt, preferred_element_type=F32) + bias_ref[hh, half, half_rows, :]
            s_ctx = lax.dot_general(qa, kc, nt, preferred_element_type=F32)
            mx = jnp.maximum(jnp.max(s_lat, axis=-1, keepdims=True), jnp.max(s_ctx, axis=-1, keepdims=True))
            e_lat = jnp.exp(s_lat - mx)
            e_ctx = jnp.exp(s_ctx - mx)
            den = jnp.sum(e_lat, axis=-1, keepdims=True) + jnp.sum(e_ctx, axis=-1, keepdims=True)
            o = (jnp.dot(e_lat.astype(BF16), v2, preferred_element_type=F32)
                 + jnp.dot(e_ctx.astype(BF16), vc, preferred_element_type=F32))
            outs.append(o / den)
        o_ref[rows, :] = jnp.where(masks[0], outs[0], outs[1]).astype(o_ref.dtype)
        return carry

    lax.fori_loop(0, q_ref.shape[0] // tq, q_tile, 0)


def _attention_lat(z_qkv, cache_k, cache_v, bias, layer, row0, n_seq, seq):
    past = cache_k.shape[2]
    pairs = NA_WIDTH // LANES
    rb = row0 // seq
    kv_cols = NA_WIDTH // LANES
    slab = bias.shape[-1]
    return pl.pallas_call(
        functools.partial(_attn_lat_kernel, tq=min(seq // 2, ATTN_Q_TILE)),
        grid=(pairs, n_seq),
        in_specs=[pl.BlockSpec((seq, LANES), lambda p, b: (rb + b, p)),
                  pl.BlockSpec((seq, LANES), lambda p, b: (rb + b, kv_cols + p)),
                  pl.BlockSpec((seq, LANES), lambda p, b: (rb + b, 2 * kv_cols + p)),
                  pl.BlockSpec((None, None, past, LANES), lambda p, b: (b, layer, 0, p)),
                  pl.BlockSpec((None, None, past, LANES), lambda p, b: (b, layer, 0, p)),
                  pl.BlockSpec((None, 2, 2, seq // 2, slab), lambda p, b: (layer, p, 0, 0, 0))],
        out_specs=pl.BlockSpec((seq, LANES), lambda p, b: (b, p)),
        out_shape=jax.ShapeDtypeStruct((n_seq * seq, NA_WIDTH), BF16),
        scratch_shapes=[pltpu.VMEM((seq, LANES), BF16), pltpu.VMEM((seq, LANES), BF16)],
        compiler_params=_params(2),
        name="attn_lat",
    )(z_qkv, z_qkv, z_qkv, cache_k, cache_v, bias)


def _neighbourhood_bias(rpb, seq):
    rows = seq // GRID_W
    kh = WIN_H
    assert rows >= WIN_H
    lead = rpb.shape[:-2]
    c = np.arange(GRID_W)
    q_cs = np.clip(c - WIN_W // 2, 0, GRID_W - WIN_W)
    col_ok = (c[None, :] >= q_cs[:, None]) & (c[None, :] < q_cs[:, None] + WIN_W)
    r = np.arange(rows)
    rs = np.clip(r - kh // 2, 0, rows - kh)
    base = np.where(r < rows // 2, 0, rows - KEY_SLAB_ROWS)
    assert (rs >= base).all() and (rs + kh <= base + KEY_SLAB_ROWS).all()
    pick_c = (c[None, None, :] - c[None, :, None] + WIN_W - 1
              == np.arange(2 * WIN_W - 1)[:, None, None]).astype(np.float32)
    w = jnp.einsum("...ij,jqk->...qik", rpb, pick_c, precision=lax.Precision.HIGHEST)
    w = jnp.where(jnp.asarray(col_ok)[:, None, :], w, NEG_INF)
    blocks = []
    for rq in range(rows):
        lo = int(rs[rq]) - rq + WIN_H - 1
        slab = w[..., lo:lo + kh, :].reshape(lead + (GRID_W, kh * GRID_W))
        left = int(rs[rq] - base[rq])
        pad = ((0, 0),) * (len(lead) + 1) + ((left * GRID_W, (KEY_SLAB_ROWS - kh - left) * GRID_W),)
        blocks.append(jnp.pad(slab, pad, constant_values=NEG_INF))
    return jnp.stack(blocks, axis=-3).reshape(lead + (2, seq // 2, KEY_SLAB_ROWS * GRID_W))


def _fourier_kernel(u_ref, ct2_ref, cc_ref, sc_ref, o_ref, pq_ref):
    t = u_ref.shape[0]
    for g in range(FOURIER_GROUPS):
        cols = slice(FOURIER_GROUP_DIM * g, FOURIER_GROUP_DIM * (g + 1))
        ug = u_ref[:, cols].astype(BF16)
        pq_ref[0:t, cols] = jnp.dot(ug, cc_ref[...], preferred_element_type=F32).astype(BF16)
        pq_ref[t:2 * t, cols] = jnp.dot(ug, sc_ref[...], preferred_element_type=F32).astype(BF16)
    o_ref[...] = jnp.dot(ct2_ref[...], pq_ref[...], preferred_element_type=F32).astype(o_ref.dtype)


def _dft_tables(t):
    def cs(n):
        k = np.arange(n, dtype=np.int64)
        ang = 2.0 * np.pi * ((k[:, None] * k[None, :]) % n).astype(np.float64) / n
        return np.cos(ang) / np.sqrt(n), np.sin(ang) / np.sqrt(n)

    ct, st = cs(t)
    cc, sc = cs(FOURIER_GROUP_DIM)
    ct2 = np.concatenate([ct, -st], axis=1).astype(np.float32)
    return (jnp.asarray(ct2).astype(BF16), jnp.asarray(cc.astype(np.float32)).astype(BF16),
            jnp.asarray(sc.astype(np.float32)).astype(BF16))


def _fourier(z_qkvu, row0, n_seq, seq):
    ct2, cc, sc = _dft_tables(seq)
    width = FOURIER_GROUPS * FOURIER_GROUP_DIM
    rb = row0 // seq
    ucol = 3 * NA_WIDTH // width
    return pl.pallas_call(
        _fourier_kernel,
        grid=(n_seq,),
        in_specs=[pl.BlockSpec((seq, width), lambda b: (rb + b, ucol)),
                  pl.BlockSpec((seq, 2 * seq), lambda b: (0, 0)),
                  pl.BlockSpec((FOURIER_GROUP_DIM, FOURIER_GROUP_DIM), lambda b: (0, 0)),
                  pl.BlockSpec((FOURIER_GROUP_DIM, FOURIER_GROUP_DIM), lambda b: (0, 0))],
        out_specs=pl.BlockSpec((seq, width), lambda b: (b, 0)),
        out_shape=jax.ShapeDtypeStruct((n_seq * seq, width), BF16),
        scratch_shapes=[pltpu.VMEM((2 * seq, width), BF16)],
        compiler_params=_params(1),
        name="fourier",
    )(z_qkvu, ct2, cc, sc)


def _rotary_tables(t):
    pos = np.arange(t)
    row = (pos // GRID_W).astype(np.float64)
    col = (pos % GRID_W).astype(np.float64)
    nf = RET_KEY_DIM // 4
    inv_freq = ROPE_BASE ** (-np.arange(nf, dtype=np.float64) / nf)
    ar = row[:, None] * inv_freq[None]
    ac = col[:, None] * inv_freq[None]
    cos = np.concatenate([np.cos(ar), np.cos(ar), np.cos(ac), np.cos(ac)], axis=1)
    sin = np.concatenate([-np.sin(ar), np.sin(ar), -np.sin(ac), np.sin(ac)], axis=1)
    return jnp.asarray(cos.astype(np.float32)), jnp.asarray(sin.astype(np.float32))


def _ret_kernel(lgf_ref, lgb_ref, *refs, t, tq, layer, rotary, state_in, state_out):
    refs = list(refs)
    q_ref, k_ref, v_ref, g_ref = refs[:4]
    refs = refs[4:]
    if rotary:
        cos_ref, sin_ref = refs[:2]
        refs = refs[2:]
    if state_in:
        sf0_ref, sb0_ref = refs[:2]
        refs = refs[2:]
    o_ref = refs[0]
    refs = refs[1:]
    if state_out:
        sf_ref, sb_ref = refs[:2]
        refs = refs[2:]
    dec_ref, kb_ref = refs

    scale = RET_KEY_DIM ** -0.5
    nq = t // tq
    heads_here = dec_ref.shape[0]

    if rotary:
        lane = lax.broadcasted_iota(I32, (1, LANES), 1)
        low = (lane % (RET_KEY_DIM // 2)) < (RET_KEY_DIM // 4)

        def rot(x, rows):
            swapped = jnp.where(low, pltpu.roll(x, LANES - RET_KEY_DIM // 4, 1), pltpu.roll(x, RET_KEY_DIM // 4, 1))
            return x * cos_ref[rows, :] + swapped * sin_ref[rows, :]
    else:
        def rot(x, rows):
            return x

    def one_head(hh):
        h = pl.program_id(0) * heads_here + hh
        cols = slice(LANES * hh, LANES * (hh + 1))
        lgf = lgf_ref[layer * RET_HEADS + h]
        lgb = lgb_ref[layer * RET_HEADS + h]

        @pl.when(pl.program_id(1) == 0)
        def _():
            def fill(ri, c):
                rows = pl.ds(pl.multiple_of(ri * tq, tq), tq)
                i = lax.broadcasted_iota(I32, (tq, t), 0) + ri * tq
                j = lax.broadcasted_iota(I32, (tq, t), 1)
                d = (i - j).astype(F32)
                m = jnp.exp(jnp.abs(d) * jnp.where(d > 0, lgf, lgb))
                dec_ref[hh, rows, :] = jnp.where(d == 0, 2.0, m)
                return c

            lax.fori_loop(0, nq, fill, 0)

        kr = rot(k_ref[:, cols].astype(F32), slice(0, t))
        kb_ref[hh] = kr.astype(BF16)
        vb = v_ref[:, cols]

        if state_out:
            j = lax.broadcasted_iota(I32, (t, 1), 0).astype(F32)
            tn = (((0,), (0,)), ((), ()))
            kf = (kr * (scale * jnp.exp(lgf * (t - 1.0 - j)))).astype(BF16)
            kbw = (kr * (scale * jnp.exp(lgb * j))).astype(BF16)
            sf = lax.dot_general(kf, vb, tn, preferred_element_type=F32)
            sb = lax.dot_general(kbw, vb, tn, preferred_element_type=F32)
            if state_in:
                sf = sf + jnp.exp(lgf * t) * sf0_ref[hh]
                sb = sb + jnp.exp(lgb * t) * sb0_ref[hh]
            sf_ref[hh] = sf
            sb_ref[hh] = sb

        def q_tile(qi, carry):
            r0 = pl.multiple_of(qi * tq, tq)
            rows = pl.ds(r0, tq)
            qr = rot(q_ref[rows, cols].astype(F32), rows)
            s = lax.dot_general((qr * scale).astype(BF16), kb_ref[hh], (((1,), (1,)), ((), ())),
                                preferred_element_type=F32)
            y = jnp.dot((s * dec_ref[hh, rows, :]).astype(BF16), vb, preferred_element_type=F32)
            if state_in:
                pos = (lax.broadcasted_iota(I32, (tq, 1), 0) + r0).astype(F32)
                qf = (qr * jnp.exp(lgf * (pos + 1.0))).astype(BF16)
                qb = (qr * jnp.exp(lgb * (t - pos))).astype(BF16)
                y = (y + jnp.dot(qf, sf0_ref[hh].astype(BF16), preferred_element_type=F32)
                     + jnp.dot(qb, sb0_ref[hh].astype(BF16), preferred_element_type=F32))
            mean = jnp.mean(y, axis=-1, keepdims=True)
            yc = y - mean
            yn = yc * lax.rsqrt(jnp.mean(yc * yc, axis=-1, keepdims=True) + EPS)
            g = g_ref[rows, cols].astype(F32)
            o_ref[rows, cols] = (g * jax.nn.sigmoid(g) * yn).astype(o_ref.dtype)
            return carry

        lax.fori_loop(0, nq, q_tile, 0)

    for hh in range(heads_here):
        one_head(hh)


def _retention(z_ret, lg_f, lg_b, layer, row0, n_seq, seq, *, rotary, states=None, state_out=False):
    rb = row0 // seq
    tq = min(seq, RET_Q_TILE)
    state_in = states is not None
    hps = RET_HEADS if seq * seq * RET_HEADS * 4 <= RET_DECAY_BYTES else 1
    cb = BRANCH_WIDTH // (hps * LANES)
    width = hps * LANES
    in_specs = [pl.BlockSpec((seq, width), lambda h, b, *_: (rb + b, 0 * cb + h)),
                pl.BlockSpec((seq, width), lambda h, b, *_: (rb + b, 1 * cb + h)),
                pl.BlockSpec((seq, width), lambda h, b, *_: (rb + b, 2 * cb + h)),
                pl.BlockSpec((seq, width), lambda h, b, *_: (rb + b, 3 * cb + h))]
    args = [z_ret, z_ret, z_ret, z_ret]
    if rotary:
        cos, sin = _rotary_tables(seq)
        in_specs += [pl.BlockSpec((seq, LANES), lambda h, b, *_: (0, 0))] * 2
        args += [cos, sin]
    if state_in:
        st_spec = pl.BlockSpec((None, None, hps, RET_KEY_DIM, RET_KEY_DIM), lambda h, b, *_: (b, layer, h, 0, 0))
        in_specs += [st_spec, st_spec]
        args += list(states)
    out_specs = [pl.BlockSpec((seq, width), lambda h, b, *_: (b, h))]
    out_shape = [jax.ShapeDtypeStruct((n_seq * seq, RET_HEADS * LANES), BF16)]
    if state_out:
        so = pl.BlockSpec((None, hps, RET_KEY_DIM, RET_KEY_DIM), lambda h, b, *_: (b, h, 0, 0))
        out_specs += [so, so]
        out_shape += [jax.ShapeDtypeStruct((n_seq, RET_HEADS, RET_KEY_DIM, RET_KEY_DIM), F32)] * 2
    return pl.pallas_call(
        functools.partial(_ret_kernel, t=seq, tq=tq, layer=layer, rotary=rotary, state_in=state_in,
                          state_out=state_out),
        grid_spec=pltpu.PrefetchScalarGridSpec(
            num_scalar_prefetch=2, grid=(RET_HEADS // hps, n_seq),
            in_specs=in_specs, out_specs=out_specs,
            scratch_shapes=[pltpu.VMEM((hps, seq, seq), F32), pltpu.VMEM((hps, seq, LANES), BF16)]),
        out_shape=out_shape,
        compiler_params=_params(2),
        name="retention",
    )(lg_f, lg_b, *args)


def _split_dot_nt(w, x):
    nt = (((1,), (1,)), ((), ()))
    w_hi = w.astype(BF16)
    w_lo = (w - w_hi.astype(F32)).astype(BF16)
    x_hi = x.astype(BF16)
    x_lo = (x - x_hi.astype(F32)).astype(BF16)
    return (lax.dot_general(w_hi, x_hi, nt, preferred_element_type=F32)
            + lax.dot_general(w_hi, x_lo, nt, preferred_element_type=F32)
            + lax.dot_general(w_lo, x_hi, nt, preferred_element_type=F32))


def _finish_kernel(tmod_ref, *refs, ctx_tiles, n_x):
    del tmod_ref
    branch_refs, refs = refs[:6], refs[6:]
    zg_ref, refs = refs[0], refs[1:]
    x_refs, refs = refs[:n_x], refs[n_x:]
    (mod_ref, gpost_ref, gpre_ref, wb_ref, wo_ref, wrt_ref, br_ref, tri_ref, ltri_ref,
     x1_ref, h2_ref, wts_ref, lpos_ref, seg_ref, off_ref) = refs
    d = D_MODEL

    def branch(j):
        return _load_rows(branch_refs[2 * j:2 * j + 2], ctx_tiles)

    def gate(j):
        return jax.nn.sigmoid(zg_ref[:, d * j:d * (j + 1)].astype(F32))

    merged = (gate(0) * jnp.dot(branch(0), wb_ref[0], preferred_element_type=F32)
              + gate(1) * jnp.dot(branch(1), wb_ref[1], preferred_element_type=F32)
              + gate(2) * jnp.dot(branch(2), wb_ref[2], preferred_element_type=F32))
    y = jnp.dot(merged.astype(BF16), wo_ref[...], preferred_element_type=F32)
    x1 = _load_rows(x_refs, ctx_tiles) + mod_ref[2:3, :] * (y * _rms(y) * gpost_ref[...])
    x1_ref[...] = x1
    h2 = x1 * _rms(x1) * gpre_ref[...] * (1.0 + mod_ref[4:5, :]) + mod_ref[3:4, :]
    h2_ref[...] = h2

    logits = _split_dot_nt(wrt_ref[...], h2) + br_ref[:, 0:1]
    tm = logits.shape[1]
    eidx = lax.broadcasted_iota(I32, (N_EXPERTS, tm), 0)
    cur = logits
    vals, hots = [], []
    for k in range(TOP_K):
        m = jnp.max(cur, axis=0, keepdims=True)
        sel = jnp.min(jnp.where(cur == m, eidx, N_EXPERTS), axis=0, keepdims=True)
        hot = eidx == sel
        vals.append(m)
        hots.append(hot)
        cur = jnp.where(hot, -jnp.inf, cur)
    exps = [jnp.exp(v - vals[0]) for v in vals]
    den = exps[0] + exps[1] + exps[2] + exps[3]
    for k in range(TOP_K):
        wts_ref[k:k + 1, :] = exps[k] / den

    member = jnp.logical_or(jnp.logical_or(hots[0], hots[1]), jnp.logical_or(hots[2], hots[3]))
    member_f = member.astype(F32)
    before = jnp.dot(member_f.astype(BF16), tri_ref[...], preferred_element_type=F32)
    units = jnp.ceil(jnp.sum(member_f, axis=1, keepdims=True) * (1.0 / SEG_ALIGN))
    units = jnp.broadcast_to(units, seg_ref.shape)
    off = jnp.dot(ltri_ref[...], units.astype(BF16), preferred_element_type=F32) * SEG_ALIGN
    seg_ref[...] = units * SEG_ALIGN
    off_ref[...] = off
    place = before + off[:, 0:1]
    for k in range(TOP_K):
        lpos_ref[k:k + 1, :] = jnp.sum(jnp.where(hots[k], place, 0.0), axis=0, keepdims=True).astype(I32)


def _finish(branches, z_gate, x, mod, g_post, g_pre_ffn, w_branch, w_out, w_router_t, b_router, layer, tile_mod):
    tm = FIN_TILE
    x_specs, x_args, _, n = _token_rows(x, tm)
    d = x_args[0].shape[1]
    tri = jnp.asarray(np.triu(np.ones((tm, tm), np.float32), k=1)).astype(BF16)
    ltri = jnp.asarray(np.tril(np.ones((N_EXPERTS, N_EXPERTS), np.float32), k=-1)).astype(BF16)
    row = lambda i, t: (i, 0)
    const2 = lambda i, t: (0, 0)
    lay3 = lambda i, t: (layer, 0, 0)
    col = lambda i, t: (0, i)
    branch_specs, branch_args = [], []
    for pair in branches:
        specs, args, ctx_tiles, _ = _token_rows(pair, tm)
        branch_specs += specs
        branch_args += args
    outs = pl.pallas_call(
        functools.partial(_finish_kernel, ctx_tiles=ctx_tiles, n_x=len(x_args)),
        grid_spec=pltpu.PrefetchScalarGridSpec(
            num_scalar_prefetch=1, grid=(n // tm,),
            in_specs=branch_specs + [pl.BlockSpec((tm, GATE_W), row)] + x_specs + [
                      pl.BlockSpec((None, None, 6, d), lambda i, t: (layer, t[i], 0, 0)),
                      pl.BlockSpec((None, 1, d), lay3),
                      pl.BlockSpec((None, 1, d), lay3),
                      pl.BlockSpec((None, 3, BRANCH_WIDTH, d), lambda i, t: (layer, 0, 0, 0)),
                      pl.BlockSpec((None, d, d), lay3),
                      pl.BlockSpec((None, N_EXPERTS, d), lay3),
                      pl.BlockSpec((None, N_EXPERTS, LANES), lay3),
                      pl.BlockSpec((tm, tm), const2),
                      pl.BlockSpec((N_EXPERTS, N_EXPERTS), const2)],
            out_specs=[pl.BlockSpec((tm, d), row),
                       pl.BlockSpec((tm, d), row),
                       pl.BlockSpec((TOP_K, tm), col),
                       pl.BlockSpec((TOP_K, tm), col),
                       pl.BlockSpec((None, N_EXPERTS, LANES), lambda i, t: (i, 0, 0)),
                       pl.BlockSpec((None, N_EXPERTS, LANES), lambda i, t: (i, 0, 0))]),
        out_shape=[jax.ShapeDtypeStruct((n, d), F32),
                   jax.ShapeDtypeStruct((n, d), F32),
                   jax.ShapeDtypeStruct((TOP_K, n), F32),
                   jax.ShapeDtypeStruct((TOP_K, n), I32),
                   jax.ShapeDtypeStruct((n // tm, N_EXPERTS, LANES), F32),
                   jax.ShapeDtypeStruct((n // tm, N_EXPERTS, LANES), F32)],
        compiler_params=_params(1),
        name="merge_router",
    )(tile_mod, *branch_args, z_gate, *x_args, mod, g_post, g_pre_ffn, w_branch, w_out, w_router_t, b_router,
      tri, ltri)
    return outs


def _segment_chunks(length, src_ref, src0, dst_ref, dst0, sem, max_chunk, fixed_src=False):
    out = []
    chunk = max_chunk
    while chunk >= SEG_ALIGN:
        done = jnp.bitwise_and(length, ~(2 * chunk - 1))
        s = 0 if fixed_src else pl.multiple_of(src0 + done, SEG_ALIGN)
        dd = pl.multiple_of(dst0 + done, SEG_ALIGN)
        out.append((chunk, pltpu.make_async_copy(src_ref.at[pl.ds(s, chunk)], dst_ref.at[pl.ds(dd, chunk)], sem)))
        chunk //= 2
    return length, out


def _for_each_chunk(n_segments, chunks_of, action):
    def run(length, chunks):
        for chunk, cp in chunks:
            pl.when(jnp.bitwise_and(length, chunk) != 0)(functools.partial(action, cp))

    def body(e, c):
        length, chunks = chunks_of(e)
        large = [p for p in chunks if p[0] >= RARE_CHUNK]
        small = [p for p in chunks if p[0] < RARE_CHUNK]
        if large:
            pl.when(length >= RARE_CHUNK)(functools.partial(run, length, large))
        run(length, small)
        return c

    lax.fori_loop(0, n_segments, body, 0, unroll=4)


def _start(cp):
    cp.start()


def _wait(cp):
    cp.wait()


def _wait_rows(total, src_ref, dst_ref, sem):
    chunk = pl.next_power_of_2(SORT_ROWS) // 2
    while chunk >= SEG_ALIGN:
        @pl.when(jnp.bitwise_and(total, chunk) != 0)
        def _(chunk=chunk):
            pltpu.make_async_copy(src_ref.at[pl.ds(0, chunk)], dst_ref.at[pl.ds(0, chunk)], sem).wait()
        chunk //= 2


def _offset_in_chunk(rows, c0):
    assert SORT_CHUNK == 256
    inside = lax.shift_right_logical(rows, 8) == c0 // SORT_CHUNK
    return jnp.where(inside, jnp.bitwise_and(rows, SORT_CHUNK - 1), -1).astype(BF16)


def _scatter_kernel(seg_ref, off_ref, pos_ref, used_ref, tpos_ref, tlen_ref, h_ref, lpos_ref, xe_hbm,
                    buf_ref, zero_ref, sem, *, tm):
    t = pl.program_id(0)
    slot = t % 2
    hb = h_ref[...].astype(BF16)
    lp = [lpos_ref[k:k + 1, :] for k in range(TOP_K)]
    rows = buf_ref.shape[1]
    r_off = lax.broadcasted_iota(I32, (SORT_CHUNK, tm), 0).astype(BF16)
    one = jnp.ones((SORT_CHUNK, tm), BF16)
    zero = jnp.zeros((SORT_CHUNK, tm), BF16)
    for c0 in range(0, rows, SORT_CHUNK):
        here = [_offset_in_chunk(p, c0) for p in lp]
        hit = jnp.logical_or(jnp.logical_or(r_off == here[0], r_off == here[1]),
                             jnp.logical_or(r_off == here[2], r_off == here[3]))
        onehot = jnp.where(hit, one, zero)
        buf_ref[slot, c0:c0 + SORT_CHUNK, :] = jnp.dot(onehot, hb, preferred_element_type=F32)

    def segments_of(tile):
        def segment(e):
            j = tile * N_EXPERTS + e
            return _segment_chunks(seg_ref[j], buf_ref.at[tile % 2], off_ref[j], xe_hbm, pos_ref[j],
                                   sem.at[tile % 2], tm)
        return segment

    @pl.when(t > 0)
    def _():
        _wait_rows(used_ref[t - 1], buf_ref.at[1 - slot], xe_hbm, sem.at[1 - slot])

    _for_each_chunk(N_EXPERTS, segments_of(t), _start)

    @pl.when(t == pl.num_programs(0) - 1)
    def _():
        _wait_rows(used_ref[t], buf_ref.at[slot], xe_hbm, sem.at[slot])
        zero_ref[...] = jnp.zeros_like(zero_ref)

        def tail(e):
            return _segment_chunks(tlen_ref[e], zero_ref, 0, xe_hbm, tpos_ref[e], sem.at[0], zero_ref.shape[0],
                                   fixed_src=True)

        _for_each_chunk(N_EXPERTS, tail, _start)
        _for_each_chunk(N_EXPERTS, tail, _wait)


def _scatter_rows(h2, lpos, plan, n_rows):
    n, d = h2.shape
    tm = FIN_TILE
    return pl.pallas_call(
        functools.partial(_scatter_kernel, tm=tm),
        grid_spec=pltpu.PrefetchScalarGridSpec(
            num_scalar_prefetch=6, grid=(n // tm,),
            in_specs=[pl.BlockSpec((tm, d), lambda i, *_: (i, 0)),
                      pl.BlockSpec((TOP_K, tm), lambda i, *_: (0, i))],
            out_specs=pl.BlockSpec(memory_space=pl.ANY),
            scratch_shapes=[pltpu.VMEM((2, SORT_ROWS, d), F32), pltpu.VMEM((MOE_BLOCK // 2, d), F32),
                            pltpu.SemaphoreType.DMA((2,))]),
        out_shape=jax.ShapeDtypeStruct((n_rows, d), F32),
        compiler_params=_params(1),
        name="moe_scatter",
    )(plan["seg"], plan["off"], plan["pos"], plan["used"], plan["tail_pos"], plan["tail_len"], h2, lpos)


def _expert_kernel(be_ref, nxt_ref, valid_ref, nu_ref, xb_ref, w1_hbm, b1_ref, w2_hbm, b2_ref, yb_ref,
                   w1s_ref, w2s_ref, w1b_ref, w2b_ref, sem, *, layer):
    i = pl.program_id(0)
    used = i < nu_ref[0]
    fresh = jnp.logical_or(i == 0, be_ref[i] != be_ref[jnp.maximum(i - 1, 0)])

    def fetch(e):
        return (pltpu.make_async_copy(w1_hbm.at[layer, e], w1s_ref, sem.at[0]),
                pltpu.make_async_copy(w2_hbm.at[layer, e], w2s_ref, sem.at[1]))

    @pl.when(i == 0)
    def _():
        for cp in fetch(be_ref[0]):
            cp.start()

    @pl.when(jnp.logical_and(used, fresh))
    def _():
        for cp in fetch(be_ref[i]):
            cp.wait()
        w1b_ref[...] = w1s_ref[...].astype(BF16)
        w2b_ref[...] = w2s_ref[...].astype(BF16)

        @pl.when(nxt_ref[i] >= 0)
        def _():
            for cp in fetch(nxt_ref[i]):
                cp.start()

    def ffn(m):
        e = be_ref[i]
        z = jnp.dot(xb_ref[0:m, :].astype(BF16), w1b_ref[...], preferred_element_type=F32) + b1_ref[e]
        glu = jnp.minimum(z[:, :D_FF], SWIGLU_LIMIT)
        lin = jnp.clip(z[:, D_FF:], -SWIGLU_LIMIT, SWIGLU_LIMIT)
        act = glu * jax.nn.sigmoid(SWIGLU_ALPHA * glu) * (lin + 1.0)
        yb_ref[0:m, :] = jnp.dot(act.astype(BF16), w2b_ref[...], preferred_element_type=F32) + b2_ref[e]
        if m < yb_ref.shape[0]:
            yb_ref[m:, :] = jnp.zeros((yb_ref.shape[0] - m, yb_ref.shape[1]), F32)

    valid = valid_ref[i]
    for m in range(EXPERT_ROW_STEP, yb_ref.shape[0] + 1, EXPERT_ROW_STEP):
        pl.when(jnp.logical_and(valid > m - EXPERT_ROW_STEP, valid <= m))(functools.partial(ffn, m))

    @pl.when(valid == 0)
    def _():
        yb_ref[...] = jnp.zeros_like(yb_ref)


def _experts(xb, block_expert, next_expert, valid_rows, n_used, w1, b1, w2, b2, layer):
    n_rows, d = xb.shape
    tm = MOE_BLOCK
    return pl.pallas_call(
        functools.partial(_expert_kernel, layer=layer),
        grid_spec=pltpu.PrefetchScalarGridSpec(
            num_scalar_prefetch=4, grid=(n_rows // tm,),
            in_specs=[pl.BlockSpec((tm, d), lambda i, be, nx, vr, nu: (jnp.minimum(i, nu[0] - 1), 0)),
                      pl.BlockSpec(memory_space=pl.ANY),
                      pl.BlockSpec((None, N_EXPERTS, 1, 2 * D_FF), lambda i, be, nx, vr, nu: (layer, 0, 0, 0)),
                      pl.BlockSpec(memory_space=pl.ANY),
                      pl.BlockSpec((None, N_EXPERTS, 1, d), lambda i, be, nx, vr, nu: (layer, 0, 0, 0))],
            out_specs=pl.BlockSpec((tm, d), lambda i, be, nx, vr, nu: (i, 0)),
            scratch_shapes=[pltpu.VMEM((d, 2 * D_FF), F32), pltpu.VMEM((D_FF, d), F32),
                            pltpu.VMEM((d, 2 * D_FF), BF16), pltpu.VMEM((D_FF, d), BF16),
                            pltpu.SemaphoreType.DMA((2,))]),
        out_shape=jax.ShapeDtypeStruct((n_rows, d), F32),
        compiler_params=_params(1),
        name="moe_experts",
    )(block_expert, next_expert, valid_rows, n_used, xb, w1, b1, w2, b2)


def _gather_kernel(tmod_ref, seg_ref, off_ref, pos_ref, used_ref, ye_hbm, lpos_ref, wts_ref, x1_ref, mod_ref, g_ref,
                   *refs, tm, tile0, prenorm_next):
    del tmod_ref
    if prenorm_next:
        modn_ref, gn_ref, o_ref, hn_ref, buf_ref, sem = refs
    else:
        o_ref, buf_ref, sem = refs
    step = pl.program_id(0)
    t = step + tile0
    slot = step % 2

    def segments_of(tile):
        def segment(e):
            j = tile * N_EXPERTS + e
            half = (tile - tile0) % 2
            return _segment_chunks(seg_ref[j], ye_hbm, pos_ref[j], buf_ref.at[half], off_ref[j], sem.at[half], tm)
        return segment

    @pl.when(step == 0)
    def _():
        buf_ref[...] = jnp.zeros_like(buf_ref)
        _for_each_chunk(N_EXPERTS, segments_of(t), _start)

    @pl.when(step + 1 < pl.num_programs(0))
    def _():
        _for_each_chunk(N_EXPERTS, segments_of(t + 1), _start)

    _wait_rows(used_ref[t], ye_hbm, buf_ref.at[slot], sem.at[slot])

    lp = [lpos_ref[:, k:k + 1] for k in range(TOP_K)]
    wt = [jnp.broadcast_to(wts_ref[:, k:k + 1].astype(BF16), (tm, SORT_CHUNK)) for k in range(TOP_K)]
    c_off = lax.broadcasted_iota(I32, (tm, SORT_CHUNK), 1).astype(BF16)
    y = jnp.zeros(o_ref.shape, F32)
    for c0 in range(0, buf_ref.shape[1], SORT_CHUNK):
        wm = jnp.zeros((tm, SORT_CHUNK), BF16)
        for k in range(TOP_K):
            wm = jnp.where(c_off == _offset_in_chunk(lp[k], c0), wt[k], wm)
        y = y + jnp.dot(wm, buf_ref[slot, c0:c0 + SORT_CHUNK, :].astype(BF16), preferred_element_type=F32)
    x2 = x1_ref[...] + mod_ref[5:6, :] * (y * _rms(y) * g_ref[...])
    o_ref[...] = x2
    if prenorm_next:
        hn = x2 * _rms(x2) * gn_ref[...]
        hn_ref[...] = (hn * (1.0 + modn_ref[1:2, :]) + modn_ref[0:1, :]).astype(hn_ref.dtype)


def _gather_combine(ye, lpos_t, wts_t, plan, x1, mod, g_post, layer, tile_mod, row0=0, rows=None, g_pre_next=None):
    n, d = x1.shape
    tm = FIN_TILE
    rows = n if rows is None else rows
    tile0 = row0 // tm
    row = lambda i, *_: (tile0 + i, 0)
    out_row = lambda i, *_: (i, 0)
    mod_spec = lambda lay: pl.BlockSpec((None, None, 6, d), lambda i, t, *_: (lay, t[tile0 + i], 0, 0))
    gain_spec = lambda lay: pl.BlockSpec((None, 1, d), lambda i, *_: (lay, 0, 0))
    prenorm_next = g_pre_next is not None
    in_specs = [pl.BlockSpec(memory_space=pl.ANY),
                pl.BlockSpec((tm, TOP_K), row),
                pl.BlockSpec((tm, TOP_K), row),
                pl.BlockSpec((tm, d), row),
                mod_spec(layer), gain_spec(layer)]
    args = [ye, lpos_t, wts_t, x1, mod, g_post]
    out_specs = [pl.BlockSpec((tm, d), out_row)]
    out_shape = [jax.ShapeDtypeStruct((rows, d), F32)]
    if prenorm_next:
        in_specs += [mod_spec(layer + 1), gain_spec(layer + 1)]
        args += [mod, g_pre_next]
        out_specs += [pl.BlockSpec((tm, d), out_row)]
        out_shape += [jax.ShapeDtypeStruct((rows, d), BF16)]
    outs = pl.pallas_call(
        functools.partial(_gather_kernel, tm=tm, tile0=tile0, prenorm_next=prenorm_next),
        grid_spec=pltpu.PrefetchScalarGridSpec(
            num_scalar_prefetch=5, grid=(rows // tm,),
            in_specs=in_specs, out_specs=out_specs,
            scratch_shapes=[pltpu.VMEM((2, SORT_ROWS, d), F32), pltpu.SemaphoreType.DMA((2,))]),
        out_shape=out_shape,
        compiler_params=_params(1),
        name="moe_gather",
    )(tile_mod, plan["seg"], plan["off"], plan["pos"], plan["used"], *args)
    return outs if prenorm_next else outs[0]


def _moe(h2, wts, lpos, seg, off, x1, mod, g_post, w1, b1, w2, b2, layer, tile_mod, split_rows=None,
         g_pre_next=None):
    n, d = h2.shape
    blk = MOE_BLOCK
    tm = FIN_TILE
    tiles = n // tm
    n_rows = -(-(n * TOP_K + tiles * N_EXPERTS * (SEG_ALIGN - 1) + N_EXPERTS * (blk - 1)) // blk) * blk
    n_blocks = n_rows // blk
    seg = seg[:, :, 0].astype(I32)
    off = off[:, :, 0].astype(I32)
    rows_e = jnp.sum(seg, axis=0)
    region = (rows_e + blk - 1) // blk * blk
    pend = jnp.cumsum(region)
    pstart = pend - region
    pos = pstart[None, :] + jnp.cumsum(seg, axis=0) - seg
    plan = {"seg": seg.reshape(-1), "off": off.reshape(-1), "pos": pos.reshape(-1).astype(I32),
            "used": jnp.sum(seg, axis=1).astype(I32),
            "tail_pos": (pstart + rows_e).astype(I32), "tail_len": (region - rows_e).astype(I32)}
    blocks = jnp.arange(n_blocks, dtype=I32) * blk
    block_expert = jnp.minimum(jnp.sum(blocks[:, None] >= pend[None, :], axis=1), N_EXPERTS - 1).astype(I32)
    n_used = (pend[-1:] // blk).astype(I32)
    valid_rows = jnp.clip((pstart + rows_e)[block_expert] - blocks, 0, blk).astype(I32)
    ids = jnp.arange(n_blocks, dtype=I32)
    run_start = jnp.logical_and(jnp.concatenate([jnp.ones((1,), bool), block_expert[1:] != block_expert[:-1]]),
                                ids < n_used[0])
    first_after = lax.cummin(jnp.where(run_start, ids, n_blocks)[::-1])[::-1]
    first_after = jnp.concatenate([first_after[1:], jnp.full((1,), n_blocks, I32)])
    next_expert = jnp.where(first_after < n_blocks, block_expert[jnp.minimum(first_after, n_blocks - 1)], -1)
    xe = _scatter_rows(h2, lpos, plan, n_rows)
    ye = _experts(xe, block_expert, next_expert.astype(I32), valid_rows, n_used, w1, b1, w2, b2, layer)
    combine = functools.partial(_gather_combine, ye, lpos.T, wts.T, plan, x1, mod, g_post, layer, tile_mod)
    if split_rows is None:
        return combine(g_pre_next=g_pre_next)
    return combine(row0=0, rows=split_rows), combine(row0=split_rows, rows=n - split_rows)


def _tile_mod_ids(n_ctx_rows, n_lat_rows, lat_seq, tm):
    ctx = np.zeros((n_ctx_rows // tm,), np.int32)
    lat = 1 + (np.arange(n_lat_rows // tm) * tm) // lat_seq
    return jnp.asarray(np.concatenate([ctx, lat.astype(np.int32)]))


def kernel(x_prompt, x_sample, cache_k, cache_v, state_ret_fwd, state_ret_bwd, c, c_ctx, w_mod, b_mod, g_pre_mix, g_post_mix, g_pre_ffn, g_post_ffn, w_in, na_rel_bias, ret_decay_fwd, ret_decay_bwd, w_branch, w_out, w_router, b_router, w_exp_in, b_exp_in, w_exp_out, b_exp_out):
    batch, seq, d = x_prompt.shape
    dec_batch, dec_seq, _ = x_sample.shape
    depth = w_in.shape[0]
    n_ctx = batch * seq
    n_lat = dec_batch * dec_seq
    assert 1 + dec_batch <= MOD_ROWS

    x = (x_prompt.reshape(n_ctx, d), x_sample.reshape(n_lat, d))
    cvec =jnp.concatenate([c_ctx[None], c, jnp.zeros((MOD_ROWS - 1 - dec_batch, d), F32)], axis=0)
    mod_all = _modulation(cvec, w_mod, b_mod).reshape(depth, MOD_ROWS, 6, d)
    tmod = {tm: _tile_mod_ids(n_ctx, n_lat, dec_seq, tm) for tm in (ROW_TILE, FIN_TILE)}
    past = cache_k.shape[2]
    ck = cache_k.reshape(dec_batch, depth, past, NA_WIDTH)
    cv = cache_v.reshape(dec_batch, depth, past, NA_WIDTH)
    lg_f = jax.nn.log_sigmoid(ret_decay_fwd.astype(F32)).reshape(-1)
    lg_b = jax.nn.log_sigmoid(ret_decay_bwd.astype(F32)).reshape(-1)
    bias_all = _neighbourhood_bias(na_rel_bias, dec_seq)
    vec = lambda g: g.reshape(depth, 1, d)
    g_pre_mix, g_post_mix, g_pre_ffn, g_post_ffn = vec(g_pre_mix), vec(g_post_mix), vec(g_pre_ffn), vec(g_post_ffn)
    w_branch_b = w_branch.astype(BF16)
    w_out_b = w_out.astype(BF16)
    w_router_t = jnp.swapaxes(w_router, 1, 2)
    b_router_l = jnp.broadcast_to(b_router[:, :, None], (depth, N_EXPERTS, LANES))
    b_exp_in = b_exp_in.reshape(depth, N_EXPERTS, 1, 2 * D_FF)
    b_exp_out = b_exp_out.reshape(depth, N_EXPERTS, 1, d)

    ks, vs, sfs, sbs = [], [], [], []
    h = _prenorm(x, g_pre_mix, mod_all, 0, tmod[ROW_TILE], ROW_TILE)
    for l in range(depth):
        last = l == depth - 1
        z_qkvu = _project(h, w_in, l, 0, QKVU_W, F32)
        z_ret = _project(h, w_in, l, QKVU_W, RET_W, BF16)
        z_gate = _project(h, w_in, l, QKVU_W + RET_W, GATE_W, BF16)
        ks.append(z_qkvu[:n_ctx, NA_WIDTH:2 * NA_WIDTH].reshape(batch, seq, NA_HEADS, NA_HEAD_DIM))
        vs.append(z_qkvu[:n_ctx, 2 * NA_WIDTH:3 * NA_WIDTH].reshape(batch, seq, NA_HEADS, NA_HEAD_DIM))

        a_pair = (_attention_ctx(z_qkvu, batch, seq),
                  _attention_lat(z_qkvu, ck, cv, bias_all, l, n_ctx, dec_batch, dec_seq))
        f_pair = (_fourier(z_qkvu, 0, batch, seq), _fourier(z_qkvu, n_ctx, dec_batch, dec_seq))
        r_ctx, s_f, s_b = _retention(z_ret, lg_f, lg_b, l, 0, batch, seq, rotary=False, state_out=True)
        (r_lat,) = _retention(z_ret, lg_f, lg_b, l, n_ctx, dec_batch, dec_seq, rotary=True,
                              states=(state_ret_fwd, state_ret_bwd))
        sfs.append(s_f)
        sbs.append(s_b)

        x1, h2, wts, lpos, seg, off = _finish(
            (a_pair, f_pair, (r_ctx, r_lat)), z_gate, x, mod_all, g_post_mix, g_pre_ffn,
            w_branch_b, w_out_b, w_router_t, b_router_l, l, tmod[FIN_TILE])
        out = _moe(h2, wts, lpos, seg, off, x1, mod_all, g_post_ffn,
                   w_exp_in, b_exp_in, w_exp_out, b_exp_out, l, tmod[FIN_TILE],
                   split_rows=n_ctx if last else None, g_pre_next=None if last else g_pre_mix)
        x, h = (out, None) if last else out

    y_prompt = x[0].reshape(batch, seq, d)
    y_sample = x[1].reshape(dec_batch, dec_seq, d)
    return (y_prompt, y_sample, jnp.stack(ks, axis=1), jnp.stack(vs, axis=1),
            jnp.stack(sfs, axis=1), jnp.stack(sbs, axis=1))
```

```python
import functools

import numpy as np
import jax
import jax.numpy as jnp
from jax import lax
from jax.experimental import pallas as pl
from jax.experimental.pallas import tpu as pltpu

F32 = jnp.float32
BF16 = jnp.bfloat16
I32 = jnp.int32

D_MODEL = 1024
GRID_W = 64
NA_HEADS = 8
NA_HEAD_DIM = 64
NA_WIDTH = NA_HEADS * NA_HEAD_DIM
WIN_H = 8
WIN_W = 16
KEY_SLAB_ROWS = 12
FOURIER_GROUPS = 4
FOURIER_GROUP_DIM = 128
RET_HEADS = 4
RET_KEY_DIM = 128
ROPE_BASE = 10000.0
BRANCH_WIDTH = 512
N_EXPERTS = 32
TOP_K = 4
D_FF = 1024
SWIGLU_LIMIT = 7.0
SWIGLU_ALPHA = 1.702
EPS = 1e-6
NEG_INF = -1e30

QKVU_W = 4 * NA_WIDTH
RET_W = 4 * BRANCH_WIDTH
GATE_W = 3 * D_MODEL
PROJ_TILE = 1024
PROJ_ROWS = 2048
ATTN_Q_TILE = 512
RET_Q_TILE = 1024
RET_DECAY_BYTES = 4 * 1024 * 1024

LANES = 128
MOD_ROWS = 16
ROW_TILE = 1024
FIN_TILE = 512
MOE_BLOCK = 512
EXPERT_ROW_STEP = 128
SEG_ALIGN = 8
SORT_CHUNK = 256
SORT_ROWS = -(-(FIN_TILE * TOP_K + N_EXPERTS * (SEG_ALIGN - 1)) // SORT_CHUNK) * SORT_CHUNK
VMEM_LIMIT = 56 * 1024 * 1024


def _params(n_axes, vmem=VMEM_LIMIT):
    return pltpu.CompilerParams(dimension_semantics=("arbitrary",) * n_axes, vmem_limit_bytes=vmem)


def _rms(x):
    return lax.rsqrt(jnp.mean(x * x, axis=-1, keepdims=True) + EPS)


def _mod_kernel(cv_ref, w_ref, b_ref, o_ref):
    cv = cv_ref[...]
    s = (cv * jax.nn.sigmoid(cv)).astype(BF16)
    o_ref[...] = jnp.dot(s, w_ref[...].astype(BF16), preferred_element_type=F32) + b_ref[...]


def _modulation(cv, w_mod, b_mod):
    depth, d, n = w_mod.shape
    tn = 1536
    return pl.pallas_call(
        _mod_kernel,
        grid=(depth, n // tn),
        in_specs=[pl.BlockSpec((MOD_ROWS, d), lambda l, j: (0, 0)),
                  pl.BlockSpec((None, d, tn), lambda l, j: (l, 0, j)),
                  pl.BlockSpec((None, 1, tn), lambda l, j: (l, 0, j))],
        out_specs=pl.BlockSpec((None, MOD_ROWS, tn), lambda l, j: (l, 0, j)),
        out_shape=jax.ShapeDtypeStruct((depth, MOD_ROWS, n), F32),
        compiler_params=_params(2),
        name="modulation",
    )(cv, w_mod, b_mod.reshape(depth, 1, n))


def _token_rows(x, tm):
    if isinstance(x, tuple):
        d = x[0].shape[1]
        ct = x[0].shape[0] // tm
        specs = [pl.BlockSpec((tm, d), lambda i, *_: (jnp.minimum(i, ct - 1), 0)),
                 pl.BlockSpec((tm, d), lambda i, *_: (jnp.maximum(i - ct, 0), 0))]
        return specs, list(x), ct, x[0].shape[0] + x[1].shape[0]
    return [pl.BlockSpec((tm, x.shape[1]), lambda i, *_: (i, 0))], [x], None, x.shape[0]


def _load_rows(refs, ctx_tiles):
    if len(refs) == 1:
        return refs[0][...]
    return jnp.where(pl.program_id(0) < ctx_tiles, refs[0][...], refs[1][...])


def _prenorm_kernel(tmod_ref, *refs, ctx_tiles):
    del tmod_ref
    g_ref, mod_ref, o_ref = refs[-3:]
    x = _load_rows(refs[:-3], ctx_tiles)
    h = x * _rms(x) * g_ref[...]
    o_ref[...] = (h * (1.0 + mod_ref[1:2, :]) + mod_ref[0:1, :]).astype(o_ref.dtype)


def _prenorm(x, g, mod, layer, tile_mod, tm):
    x_specs, x_args, ctx_tiles, n = _token_rows(x, tm)
    d = x_args[0].shape[1]
    return pl.pallas_call(
        functools.partial(_prenorm_kernel, ctx_tiles=ctx_tiles),
        grid_spec=pltpu.PrefetchScalarGridSpec(
            num_scalar_prefetch=1, grid=(n // tm,),
            in_specs=x_specs + [pl.BlockSpec((None, 1, d), lambda i, t: (layer, 0, 0)),
                                pl.BlockSpec((None, None, 6, d), lambda i, t: (layer, t[i], 0, 0))],
            out_specs=pl.BlockSpec((tm, d), lambda i, t: (i, 0))),
        out_shape=jax.ShapeDtypeStruct((n, d), BF16),
        compiler_params=_params(1),
        name="prenorm",
    )(tile_mod, *x_args, g, mod)


def _proj_kernel(h_ref, w_ref, o_ref, wb_ref):
    @pl.when(pl.program_id(1) == 0)
    def _():
        wb_ref[...] = w_ref[...].astype(BF16)

    o_ref[...] = jnp.dot(h_ref[...], wb_ref[...], preferred_element_type=F32).astype(o_ref.dtype)


def _project(h, w, layer, col0, width, out_dtype):
    n, d = h.shape
    tm = PROJ_ROWS
    tn = PROJ_TILE
    cb = col0 // tn
    return pl.pallas_call(
        _proj_kernel,
        grid=(width // tn, n // tm),
        in_specs=[pl.BlockSpec((tm, d), lambda j, i: (i, 0)),
                  pl.BlockSpec((None, d, tn), lambda j, i: (layer, 0, cb + j))],
        out_specs=pl.BlockSpec((tm, tn), lambda j, i: (i, j)),
        out_shape=jax.ShapeDtypeStruct((n, width), out_dtype),
        scratch_shapes=[pltpu.VMEM((d, tn), BF16)],
        compiler_params=_params(2),
        name="in_proj",
    )(h, w)


def _head_pair_masks():
    lane = lax.broadcasted_iota(I32, (1, LANES), 1)
    first = lane < NA_HEAD_DIM
    return first, jnp.logical_not(first)


def _attn_ctx_kernel(q_ref, k_ref, v_ref, o_ref):
    masks = _head_pair_masks()
    scale = NA_HEAD_DIM ** -0.5
    for p in range(NA_WIDTH // LANES):
        cols = slice(LANES * p, LANES * (p + 1))
        q2 = q_ref[:, cols] * scale
        k2 = k_ref[:, cols].astype(BF16)
        v2 = v_ref[:, cols].astype(BF16)
        outs = []
        for m in masks:
            qa = jnp.where(m, q2, 0.0).astype(BF16)
            s = lax.dot_general(qa, k2, (((1,), (1,)), ((), ())), preferred_element_type=F32)
            e = jnp.exp(s - jnp.max(s, axis=-1, keepdims=True))
            den = jnp.sum(e, axis=-1, keepdims=True)
            outs.append(jnp.dot(e.astype(BF16), v2, preferred_element_type=F32) / den)
        o_ref[:, cols] = jnp.where(masks[0], outs[0], outs[1]).astype(o_ref.dtype)


def _attention_ctx(z_qkv, n_seq, seq):
    return pl.pallas_call(
        _attn_ctx_kernel,
        grid=(n_seq,),
        in_specs=[pl.BlockSpec((seq, NA_WIDTH), lambda b: (b, 0)),
                  pl.BlockSpec((seq, NA_WIDTH), lambda b: (b, 1)),
                  pl.BlockSpec((seq, NA_WIDTH), lambda b: (b, 2))],
        out_specs=pl.BlockSpec((seq, NA_WIDTH), lambda b: (b, 0)),
        out_shape=jax.ShapeDtypeStruct((n_seq * seq, NA_WIDTH), BF16),
        compiler_params=_params(1),
        name="attn_ctx",
    )(z_qkv, z_qkv, z_qkv)


def _attn_lat_kernel(q_ref, k_ref, v_ref, kc_ref, vc_ref, bias_ref, o_ref, kb_ref, vb_ref, *, tq):
    masks = _head_pair_masks()
    scale = NA_HEAD_DIM ** -0.5
    seq = q_ref.shape[0]
    slab = bias_ref.shape[3]
    per_half = seq // 2 // tq
    kb_ref[...] = k_ref[...].astype(BF16)
    vb_ref[...] = v_ref[...].astype(BF16)
    kc = kc_ref[...].astype(BF16)
    vc = vc_ref[...].astype(BF16)
    nt = (((1,), (1,)), ((), ()))

    def q_tile(qi, carry):
        rows = pl.ds(pl.multiple_of(qi * tq, tq), tq)
        half = qi // per_half
        half_rows = pl.ds(pl.multiple_of((qi % per_half) * tq, tq), tq)
        keys = pl.ds(pl.multiple_of(half * (seq - slab), seq - slab), slab)
        k2 = kb_ref[keys, :]
        v2 = vb_ref[keys, :]
        q2 = q_ref[rows, :] * scale
        outs = []
        for hh, m in enumerate(masks):
            qa = jnp.where(m, q2, 0.0).astype(BF16)
            s_lat = lax.dot_general(qa, k2, nt, preferred_element_type=F32) + bias_ref[hh, half, half_rows, :]
            s_ctx = lax.dot_general(qa, kc, nt, preferred_element_type=F32)
            mx = jnp.maximum(jnp.max(s_lat, axis=-1, keepdims=True), jnp.max(s_ctx, axis=-1, keepdims=True))
            e_lat = jnp.exp(s_lat - mx)
            e_ctx = jnp.exp(s_ctx - mx)
            den = jnp.sum(e_lat, axis=-1, keepdims=True) + jnp.sum(e_ctx, axis=-1, keepdims=True)
            o = (jnp.dot(e_lat.astype(BF16), v2, preferred_element_type=F32)
                 + jnp.dot(e_ctx.astype(BF16), vc, preferred_element_type=F32))
            outs.append(o / den)
        o_ref[rows, :] = jnp.where(masks[0], outs[0], outs[1]).astype(o_ref.dtype)
        return carry

    lax.fori_loop(0, q_ref.shape[0] // tq, q_tile, 0, unroll=True)


def _attention_lat(z_qkv, cache_k, cache_v, bias, layer, row0, n_seq, seq):
    past = cache_k.shape[2]
    pairs = NA_WIDTH // LANES
    rb = row0 // seq
    kv_cols = NA_WIDTH // LANES
    slab = bias.shape[-1]
    return pl.pallas_call(
        functools.partial(_attn_lat_kernel, tq=min(seq // 2, ATTN_Q_TILE)),
        grid=(pairs, n_seq),
        in_specs=[pl.BlockSpec((seq, LANES), lambda p, b: (rb + b, p)),
                  pl.BlockSpec((seq, LANES), lambda p, b: (rb + b, kv_cols + p)),
                  pl.BlockSpec((seq, LANES), lambda p, b: (rb + b, 2 * kv_cols + p)),
                  pl.BlockSpec((None, None, past, LANES), lambda p, b: (b, layer, 0, p)),
                  pl.BlockSpec((None, None, past, LANES), lambda p, b: (b, layer, 0, p)),
                  pl.BlockSpec((None, 2, 2, seq // 2, slab), lambda p, b: (layer, p, 0, 0, 0))],
        out_specs=pl.BlockSpec((seq, LANES), lambda p, b: (b, p)),
        out_shape=jax.ShapeDtypeStruct((n_seq * seq, NA_WIDTH), BF16),
        scratch_shapes=[pltpu.VMEM((seq, LANES), BF16), pltpu.VMEM((seq, LANES), BF16)],
        compiler_params=_params(2),
        name="attn_lat",
    )(z_qkv, z_qkv, z_qkv, cache_k, cache_v, bias)


def _neighbourhood_bias(rpb, seq):
    rows = seq // GRID_W
    kh = WIN_H
    assert rows >= WIN_H
    lead = rpb.shape[:-2]
    c = np.arange(GRID_W)
    q_cs = np.clip(c - WIN_W // 2, 0, GRID_W - WIN_W)
    col_ok = (c[None, :] >= q_cs[:, None]) & (c[None, :] < q_cs[:, None] + WIN_W)
    r = np.arange(rows)
    rs = np.clip(r - kh // 2, 0, rows - kh)
    base = np.where(r < rows // 2, 0, rows - KEY_SLAB_ROWS)
    assert (rs >= base).all() and (rs + kh <= base + KEY_SLAB_ROWS).all()
    pick_c = (c[None, None, :] - c[None, :, None] + WIN_W - 1
              == np.arange(2 * WIN_W - 1)[:, None, None]).astype(np.float32)
    w = jnp.einsum("...ij,jqk->...qik", rpb, pick_c, precision=lax.Precision.HIGHEST)
    w = jnp.where(jnp.asarray(col_ok)[:, None, :], w, NEG_INF)
    blocks = []
    for rq in range(rows):
        lo = int(rs[rq]) - rq + WIN_H - 1
        slab = w[..., lo:lo + kh, :].reshape(lead + (GRID_W, kh * GRID_W))
        left = int(rs[rq] - base[rq])
        pad = ((0, 0),) * (len(lead) + 1) + ((left * GRID_W, (KEY_SLAB_ROWS - kh - left) * GRID_W),)
        blocks.append(jnp.pad(slab, pad, constant_values=NEG_INF))
    return jnp.stack(blocks, axis=-3).reshape(lead + (2, seq // 2, KEY_SLAB_ROWS * GRID_W))


def _fourier_kernel(u_ref, ct2_ref, cc_ref, sc_ref, o_ref, pq_ref):
    t = u_ref.shape[0]
    for g in range(FOURIER_GROUPS):
        cols = slice(FOURIER_GROUP_DIM * g, FOURIER_GROUP_DIM * (g + 1))
        ug = u_ref[:, cols].astype(BF16)
        pq_ref[0:t, cols] = jnp.dot(ug, cc_ref[...], preferred_element_type=F32).astype(BF16)
        pq_ref[t:2 * t, cols] = jnp.dot(ug, sc_ref[...], preferred_element_type=F32).astype(BF16)
    o_ref[...] = jnp.dot(ct2_ref[...], pq_ref[...], preferred_element_type=F32).astype(o_ref.dtype)


def _dft_tables(t):
    def cs(n):
        k = np.arange(n, dtype=np.int64)
        ang = 2.0 * np.pi * ((k[:, None] * k[None, :]) % n).astype(np.float64) / n
        return np.cos(ang) / np.sqrt(n), np.sin(ang) / np.sqrt(n)

    ct, st = cs(t)
    cc, sc = cs(FOURIER_GROUP_DIM)
    ct2 = np.concatenate([ct, -st], axis=1).astype(np.float32)
    return (jnp.asarray(ct2).astype(BF16), jnp.asarray(cc.astype(np.float32)).astype(BF16),
            jnp.asarray(sc.astype(np.float32)).astype(BF16))


def _fourier(z_qkvu, row0, n_seq, seq):
    ct2, cc, sc = _dft_tables(seq)
    width = FOURIER_GROUPS * FOURIER_GROUP_DIM
    rb = row0 // seq
    ucol = 3 * NA_WIDTH // width
    return pl.pallas_call(
        _fourier_kernel,
        grid=(n_seq,),
        in_specs=[pl.BlockSpec((seq, width), lambda b: (rb + b, ucol)),
                  pl.BlockSpec((seq, 2 * seq), lambda b: (0, 0)),
                  pl.BlockSpec((FOURIER_GROUP_DIM, FOURIER_GROUP_DIM), lambda b: (0, 0)),
                  pl.BlockSpec((FOURIER_GROUP_DIM, FOURIER_GROUP_DIM), lambda b: (0, 0))],
        out_specs=pl.BlockSpec((seq, width), lambda b: (b, 0)),
        out_shape=jax.ShapeDtypeStruct((n_seq * seq, width), BF16),
        scratch_shapes=[pltpu.VMEM((2 * seq, width), BF16)],
        compiler_params=_params(1),
        name="fourier",
    )(z_qkvu, ct2, cc, sc)


def _rotary_tables(t):
    pos = np.arange(t)
    row = (pos // GRID_W).astype(np.float64)
    col = (pos % GRID_W).astype(np.float64)
    nf = RET_KEY_DIM // 4
    inv_freq = ROPE_BASE ** (-np.arange(nf, dtype=np.float64) / nf)
    ar = row[:, None] * inv_freq[None]
    ac = col[:, None] * inv_freq[None]
    cos = np.concatenate([np.cos(ar), np.cos(ar), np.cos(ac), np.cos(ac)], axis=1)
    sin = np.concatenate([-np.sin(ar), np.sin(ar), -np.sin(ac), np.sin(ac)], axis=1)
    return jnp.asarray(cos.astype(np.float32)), jnp.asarray(sin.astype(np.float32))


def _ret_kernel(lgf_ref, lgb_ref, *refs, t, tq, layer, rotary, state_in, state_out):
    refs = list(refs)
    q_ref, k_ref, v_ref, g_ref = refs[:4]
    refs = refs[4:]
    if rotary:
        cos_ref, sin_ref = refs[:2]
        refs = refs[2:]
    if state_in:
        sf0_ref, sb0_ref = refs[:2]
        refs = refs[2:]
    o_ref = refs[0]
    refs = refs[1:]
    if state_out:
        sf_ref, sb_ref = refs[:2]
        refs = refs[2:]
    dec_ref, kb_ref = refs

    scale = RET_KEY_DIM ** -0.5
    nq = t // tq
    heads_here = dec_ref.shape[0]

    if rotary:
        lane = lax.broadcasted_iota(I32, (1, LANES), 1)
        low = (lane % (RET_KEY_DIM // 2)) < (RET_KEY_DIM // 4)

        def rot(x, rows):
            swapped = jnp.where(low, pltpu.roll(x, LANES - RET_KEY_DIM // 4, 1), pltpu.roll(x, RET_KEY_DIM // 4, 1))
            return x * cos_ref[rows, :] + swapped * sin_ref[rows, :]
    else:
        def rot(x, rows):
            return x

    def one_head(hh):
        h = pl.program_id(0) * heads_here + hh
        cols = slice(LANES * hh, LANES * (hh + 1))
        lgf = lgf_ref[layer * RET_HEADS + h]
        lgb = lgb_ref[layer * RET_HEADS + h]

        @pl.when(pl.program_id(1) == 0)
        def _():
            def fill(ri, c):
                rows = pl.ds(pl.multiple_of(ri * tq, tq), tq)
                i = lax.broadcasted_iota(I32, (tq, t), 0) + ri * tq
                j = lax.broadcasted_iota(I32, (tq, t), 1)
                d = (i - j).astype(F32)
                m = jnp.exp(jnp.abs(d) * jnp.where(d > 0, lgf, lgb))
                dec_ref[hh, rows, :] = jnp.where(d == 0, 2.0, m)
                return c

            lax.fori_loop(0, nq, fill, 0)

        kr = rot(k_ref[:, cols].astype(F32), slice(0, t))
        kb_ref[hh] = kr.astype(BF16)
        vb = v_ref[:, cols]

        if state_out:
            j = lax.broadcasted_iota(I32, (t, 1), 0).astype(F32)
            tn = (((0,), (0,)), ((), ()))
            kf = (kr * (scale * jnp.exp(lgf * (t - 1.0 - j)))).astype(BF16)
            kbw = (kr * (scale * jnp.exp(lgb * j))).astype(BF16)
            sf = lax.dot_general(kf, vb, tn, preferred_element_type=F32)
            sb = lax.dot_general(kbw, vb, tn, preferred_element_type=F32)
            if state_in:
                sf = sf + jnp.exp(lgf * t) * sf0_ref[hh]
                sb = sb + jnp.exp(lgb * t) * sb0_ref[hh]
            sf_ref[hh] = sf
            sb_ref[hh] = sb

        def q_tile(qi, carry):
            r0 = pl.multiple_of(qi * tq, tq)
            rows = pl.ds(r0, tq)
            qr = rot(q_ref[rows, cols].astype(F32), rows)
            s = lax.dot_general((qr * scale).astype(BF16), kb_ref[hh], (((1,), (1,)), ((), ())),
                                preferred_element_type=F32)
            y = jnp.dot((s * dec_ref[hh, rows, :]).astype(BF16), vb, preferred_element_type=F32)
            if state_in:
                pos = (lax.broadcasted_iota(I32, (tq, 1), 0) + r0).astype(F32)
                qf = (qr * jnp.exp(lgf * (pos + 1.0))).astype(BF16)
                qb = (qr * jnp.exp(lgb * (t - pos))).astype(BF16)
                y = (y + jnp.dot(qf, sf0_ref[hh].astype(BF16), preferred_element_type=F32)
                     + jnp.dot(qb, sb0_ref[hh].astype(BF16), preferred_element_type=F32))
            mean = jnp.mean(y, axis=-1, keepdims=True)
            yc = y - mean
            yn = yc * lax.rsqrt(jnp.mean(yc * yc, axis=-1, keepdims=True) + EPS)
            g = g_ref[rows, cols].astype(F32)
            o_ref[rows, cols] = (g * jax.nn.sigmoid(g) * yn).astype(o_ref.dtype)
            return carry

        lax.fori_loop(0, nq, q_tile, 0)

    for hh in range(heads_here):
        one_head(hh)


def _retention(z_ret, lg_f, lg_b, layer, row0, n_seq, seq, *, rotary, states=None, state_out=False):
    rb = row0 // seq
    tq = min(seq, RET_Q_TILE)
    state_in = states is not None
    hps = RET_HEADS if seq * seq * RET_HEADS * 4 <= RET_DECAY_BYTES else 1
    cb = BRANCH_WIDTH // (hps * LANES)
    width = hps * LANES
    in_specs = [pl.BlockSpec((seq, width), lambda h, b, *_: (rb + b, 0 * cb + h)),
                pl.BlockSpec((seq, width), lambda h, b, *_: (rb + b, 1 * cb + h)),
                pl.BlockSpec((seq, width), lambda h, b, *_: (rb + b, 2 * cb + h)),
                pl.BlockSpec((seq, width), lambda h, b, *_: (rb + b, 3 * cb + h))]
    args = [z_ret, z_ret, z_ret, z_ret]
    if rotary:
        cos, sin = _rotary_tables(seq)
        in_specs += [pl.BlockSpec((seq, LANES), lambda h, b, *_: (0, 0))] * 2
        args += [cos, sin]
    if state_in:
        st_spec = pl.BlockSpec((None, None, hps, RET_KEY_DIM, RET_KEY_DIM), lambda h, b, *_: (b, layer, h, 0, 0))
        in_specs += [st_spec, st_spec]
        args += list(states)
    out_specs = [pl.BlockSpec((seq, width), lambda h, b, *_: (b, h))]
    out_shape = [jax.ShapeDtypeStruct((n_seq * seq, RET_HEADS * LANES), BF16)]
    if state_out:
        so = pl.BlockSpec((None, hps, RET_KEY_DIM, RET_KEY_DIM), lambda h, b, *_: (b, h, 0, 0))
        out_specs += [so, so]
        out_shape += [jax.ShapeDtypeStruct((n_seq, RET_HEADS, RET_KEY_DIM, RET_KEY_DIM), F32)] * 2
    return pl.pallas_call(
        functools.partial(_ret_kernel, t=seq, tq=tq, layer=layer, rotary=rotary, state_in=state_in,
                          state_out=state_out),
        grid_spec=pltpu.PrefetchScalarGridSpec(
            num_scalar_prefetch=2, grid=(RET_HEADS // hps, n_seq),
            in_specs=in_specs, out_specs=out_specs,
            scratch_shapes=[pltpu.VMEM((hps, seq, seq), F32), pltpu.VMEM((hps, seq, LANES), BF16)]),
        out_shape=out_shape,
        compiler_params=_params(2),
        name="retention",
    )(lg_f, lg_b, *args)


def _split_dot_nt(w, x):
    nt = (((1,), (1,)), ((), ()))
    w_hi = w.astype(BF16)
    w_lo = (w - w_hi.astype(F32)).astype(BF16)
    x_hi = x.astype(BF16)
    x_lo = (x - x_hi.astype(F32)).astype(BF16)
    return (lax.dot_general(w_hi, x_hi, nt, preferred_element_type=F32)
            + lax.dot_general(w_hi, x_lo, nt, preferred_element_type=F32)
            + lax.dot_general(w_lo, x_hi, nt, preferred_element_type=F32))


def _finish_kernel(tmod_ref, *refs, ctx_tiles, n_x):
    del tmod_ref
    branch_refs, refs = refs[:6], refs[6:]
    zg_ref, refs = refs[0], refs[1:]
    x_refs, refs = refs[:n_x], refs[n_x:]
    (mod_ref, gpost_ref, gpre_ref, wb_ref, wo_ref, wrt_ref, br_ref, tri_ref, ltri_ref,
     x1_ref, h2_ref, wts_ref, lpos_ref, seg_ref, off_ref) = refs
    d = D_MODEL

    def branch(j):
        return _load_rows(branch_refs[2 * j:2 * j + 2], ctx_tiles)

    def gate(j):
        return jax.nn.sigmoid(zg_ref[:, d * j:d * (j + 1)].astype(F32))

    merged = (gate(0) * jnp.dot(branch(0), wb_ref[0], preferred_element_type=F32)
              + gate(1) * jnp.dot(branch(1), wb_ref[1], preferred_element_type=F32)
              + gate(2) * jnp.dot(branch(2), wb_ref[2], preferred_element_type=F32))
    y = jnp.dot(merged.astype(BF16), wo_ref[...], preferred_element_type=F32)
    x1 = _load_rows(x_refs, ctx_tiles) + mod_ref[2:3, :] * (y * _rms(y) * gpost_ref[...])
    x1_ref[...] = x1
    h2 = x1 * _rms(x1) * gpre_ref[...] * (1.0 + mod_ref[4:5, :]) + mod_ref[3:4, :]
    h2_ref[...] = h2

    logits = _split_dot_nt(wrt_ref[...], h2) + br_ref[:, 0:1]
    tm = logits.shape[1]
    eidx = lax.broadcasted_iota(I32, (N_EXPERTS, tm), 0)
    cur = logits
    vals, hots = [], []
    for k in range(TOP_K):
        m = jnp.max(cur, axis=0, keepdims=True)
        sel = jnp.min(jnp.where(cur == m, eidx, N_EXPERTS), axis=0, keepdims=True)
        hot = eidx == sel
        vals.append(m)
        hots.append(hot)
        cur = jnp.where(hot, -jnp.inf, cur)
    exps = [jnp.exp(v - vals[0]) for v in vals]
    den = exps[0] + exps[1] + exps[2] + exps[3]
    for k in range(TOP_K):
        wts_ref[k:k + 1, :] = exps[k] / den

    member = jnp.logical_or(jnp.logical_or(hots[0], hots[1]), jnp.logical_or(hots[2], hots[3]))
    member_f = member.astype(F32)
    before = jnp.dot(member_f.astype(BF16), tri_ref[...], preferred_element_type=F32)
    units = jnp.ceil(jnp.sum(member_f, axis=1, keepdims=True) * (1.0 / SEG_ALIGN))
    units = jnp.broadcast_to(units, seg_ref.shape)
    off = jnp.dot(ltri_ref[...], units.astype(BF16), preferred_element_type=F32) * SEG_ALIGN
    seg_ref[...] = units * SEG_ALIGN
    off_ref[...] = off
    place = before + off[:, 0:1]
    for k in range(TOP_K):
        lpos_ref[k:k + 1, :] = jnp.sum(jnp.where(hots[k], place, 0.0), axis=0, keepdims=True).astype(I32)


def _finish(branches, z_gate, x, mod, g_post, g_pre_ffn, w_branch, w_out, w_router_t, b_router, layer, tile_mod):
    tm = FIN_TILE
    x_specs, x_args, _, n = _token_rows(x, tm)
    d = x_args[0].shape[1]
    tri = jnp.asarray(np.triu(np.ones((tm, tm), np.float32), k=1)).astype(BF16)
    ltri = jnp.asarray(np.tril(np.ones((N_EXPERTS, N_EXPERTS), np.float32), k=-1)).astype(BF16)
    row = lambda i, t: (i, 0)
    const2 = lambda i, t: (0, 0)
    lay3 = lambda i, t: (layer, 0, 0)
    col = lambda i, t: (0, i)
    branch_specs, branch_args = [], []
    for pair in branches:
        specs, args, ctx_tiles, _ = _token_rows(pair, tm)
        branch_specs += specs
        branch_args += args
    outs = pl.pallas_call(
        functools.partial(_finish_kernel, ctx_tiles=ctx_tiles, n_x=len(x_args)),
        grid_spec=pltpu.PrefetchScalarGridSpec(
            num_scalar_prefetch=1, grid=(n // tm,),
            in_specs=branch_specs + [pl.BlockSpec((tm, GATE_W), row)] + x_specs + [
                      pl.BlockSpec((None, None, 6, d), lambda i, t: (layer, t[i], 0, 0)),
                      pl.BlockSpec((None, 1, d), lay3),
                      pl.BlockSpec((None, 1, d), lay3),
                      pl.BlockSpec((None, 3, BRANCH_WIDTH, d), lambda i, t: (layer, 0, 0, 0)),
                      pl.BlockSpec((None, d, d), lay3),
                      pl.BlockSpec((None, N_EXPERTS, d), lay3),
                      pl.BlockSpec((None, N_EXPERTS, LANES), lay3),
                      pl.BlockSpec((tm, tm), const2),
                      pl.BlockSpec((N_EXPERTS, N_EXPERTS), const2)],
            out_specs=[pl.BlockSpec((tm, d), row),
                       pl.BlockSpec((tm, d), row),
                       pl.BlockSpec((TOP_K, tm), col),
                       pl.BlockSpec((TOP_K, tm), col),
                       pl.BlockSpec((None, N_EXPERTS, LANES), lambda i, t: (i, 0, 0)),
                       pl.BlockSpec((None, N_EXPERTS, LANES), lambda i, t: (i, 0, 0))]),
        out_shape=[jax.ShapeDtypeStruct((n, d), F32),
                   jax.ShapeDtypeStruct((n, d), F32),
                   jax.ShapeDtypeStruct((TOP_K, n), F32),
                   jax.ShapeDtypeStruct((TOP_K, n), I32),
                   jax.ShapeDtypeStruct((n // tm, N_EXPERTS, LANES), F32),
                   jax.ShapeDtypeStruct((n // tm, N_EXPERTS, LANES), F32)],
        compiler_params=_params(1),
        name="merge_router",
    )(tile_mod, *branch_args, z_gate, *x_args, mod, g_post, g_pre_ffn, w_branch, w_out, w_router_t, b_router,
      tri, ltri)
    return outs


def _segment_chunks(length, src_ref, src0, dst_ref, dst0, sem, max_chunk, fixed_src=False):
    out = []
    chunk = max_chunk
    while chunk >= SEG_ALIGN:
        done = jnp.bitwise_and(length, ~(2 * chunk - 1))
        present = jnp.bitwise_and(length, chunk) != 0
        s = 0 if fixed_src else pl.multiple_of(src0 + done, SEG_ALIGN)
        dd = pl.multiple_of(dst0 + done, SEG_ALIGN)
        out.append((present, pltpu.make_async_copy(src_ref.at[pl.ds(s, chunk)], dst_ref.at[pl.ds(dd, chunk)], sem)))
        chunk //= 2
    return out


def _for_each_chunk(n_segments, chunks_of, action):
    def body(e, c):
        for present, cp in chunks_of(e):
            pl.when(present)(functools.partial(action, cp))
        return c

    lax.fori_loop(0, n_segments, body, 0, unroll=4)


def _start(cp):
    cp.start()


def _wait(cp):
    cp.wait()


def _wait_rows(total, src_ref, dst_ref, sem):
    chunk = pl.next_power_of_2(SORT_ROWS) // 2
    while chunk >= SEG_ALIGN:
        @pl.when(jnp.bitwise_and(total, chunk) != 0)
        def _(chunk=chunk):
            pltpu.make_async_copy(src_ref.at[pl.ds(0, chunk)], dst_ref.at[pl.ds(0, chunk)], sem).wait()
        chunk //= 2


def _offset_in_chunk(rows, c0):
    assert SORT_CHUNK == 256
    inside = lax.shift_right_logical(rows, 8) == c0 // SORT_CHUNK
    return jnp.where(inside, jnp.bitwise_and(rows, SORT_CHUNK - 1), -1).astype(BF16)


def _scatter_kernel(seg_ref, off_ref, pos_ref, used_ref, tpos_ref, tlen_ref, h_ref, lpos_ref, xe_hbm,
                    buf_ref, zero_ref, sem, *, tm):
    t = pl.program_id(0)
    slot = t % 2
    hb = h_ref[...].astype(BF16)
    lp = [lpos_ref[k:k + 1, :] for k in range(TOP_K)]
    rows = buf_ref.shape[1]
    r_off = lax.broadcasted_iota(I32, (SORT_CHUNK, tm), 0).astype(BF16)
    one = jnp.ones((SORT_CHUNK, tm), BF16)
    zero = jnp.zeros((SORT_CHUNK, tm), BF16)
    for c0 in range(0, rows, SORT_CHUNK):
        here = [_offset_in_chunk(p, c0) for p in lp]
        hit = jnp.logical_or(jnp.logical_or(r_off == here[0], r_off == here[1]),
                             jnp.logical_or(r_off == here[2], r_off == here[3]))
        onehot = jnp.where(hit, one, zero)
        buf_ref[slot, c0:c0 + SORT_CHUNK, :] = jnp.dot(onehot, hb, preferred_element_type=F32)

    def segments_of(tile):
        def segment(e):
            j = tile * N_EXPERTS + e
            return _segment_chunks(seg_ref[j], buf_ref.at[tile % 2], off_ref[j], xe_hbm, pos_ref[j],
                                   sem.at[tile % 2], tm)
        return segment

    @pl.when(t > 0)
    def _():
        _wait_rows(used_ref[t - 1], buf_ref.at[1 - slot], xe_hbm, sem.at[1 - slot])

    _for_each_chunk(N_EXPERTS, segments_of(t), _start)

    @pl.when(t == pl.num_programs(0) - 1)
    def _():
        _wait_rows(used_ref[t], buf_ref.at[slot], xe_hbm, sem.at[slot])
        zero_ref[...] = jnp.zeros_like(zero_ref)

        def tail(e):
            return _segment_chunks(tlen_ref[e], zero_ref, 0, xe_hbm, tpos_ref[e], sem.at[0], zero_ref.shape[0],
                                   fixed_src=True)

        _for_each_chunk(N_EXPERTS, tail, _start)
        _for_each_chunk(N_EXPERTS, tail, _wait)


def _scatter_rows(h2, lpos, plan, n_rows):
    n, d = h2.shape
    tm = FIN_TILE
    return pl.pallas_call(
        functools.partial(_scatter_kernel, tm=tm),
        grid_spec=pltpu.PrefetchScalarGridSpec(
            num_scalar_prefetch=6, grid=(n // tm,),
            in_specs=[pl.BlockSpec((tm, d), lambda i, *_: (i, 0)),
                      pl.BlockSpec((TOP_K, tm), lambda i, *_: (0, i))],
            out_specs=pl.BlockSpec(memory_space=pl.ANY),
            scratch_shapes=[pltpu.VMEM((2, SORT_ROWS, d), F32), pltpu.VMEM((MOE_BLOCK // 2, d), F32),
                            pltpu.SemaphoreType.DMA((2,))]),
        out_shape=jax.ShapeDtypeStruct((n_rows, d), F32),
        compiler_params=_params(1),
        name="moe_scatter",
    )(plan["seg"], plan["off"], plan["pos"], plan["used"], plan["tail_pos"], plan["tail_len"], h2, lpos)


def _expert_kernel(be_ref, nxt_ref, valid_ref, nu_ref, xb_ref, w1_hbm, b1_ref, w2_hbm, b2_ref, yb_ref,
                   w1s_ref, w2s_ref, w1b_ref, w2b_ref, sem, *, layer):
    i = pl.program_id(0)
    used = i < nu_ref[0]
    fresh = jnp.logical_or(i == 0, be_ref[i] != be_ref[jnp.maximum(i - 1, 0)])

    def fetch(e):
        return (pltpu.make_async_copy(w1_hbm.at[layer, e], w1s_ref, sem.at[0]),
                pltpu.make_async_copy(w2_hbm.at[layer, e], w2s_ref, sem.at[1]))

    @pl.when(i == 0)
    def _():
        for cp in fetch(be_ref[0]):
            cp.start()

    @pl.when(jnp.logical_and(used, fresh))
    def _():
        for cp in fetch(be_ref[i]):
            cp.wait()
        w1b_ref[...] = w1s_ref[...].astype(BF16)
        w2b_ref[...] = w2s_ref[...].astype(BF16)

        @pl.when(nxt_ref[i] >= 0)
        def _():
            for cp in fetch(nxt_ref[i]):
                cp.start()

    def ffn(m):
        e = be_ref[i]
        z = jnp.dot(xb_ref[0:m, :].astype(BF16), w1b_ref[...], preferred_element_type=F32) + b1_ref[e]
        glu = jnp.minimum(z[:, :D_FF], SWIGLU_LIMIT)
        lin = jnp.clip(z[:, D_FF:], -SWIGLU_LIMIT, SWIGLU_LIMIT)
        act = glu * jax.nn.sigmoid(SWIGLU_ALPHA * glu) * (lin + 1.0)
        yb_ref[0:m, :] = jnp.dot(act.astype(BF16), w2b_ref[...], preferred_element_type=F32) + b2_ref[e]
        if m < yb_ref.shape[0]:
            yb_ref[m:, :] = jnp.zeros((yb_ref.shape[0] - m, yb_ref.shape[1]), F32)

    valid = valid_ref[i]
    for m in range(EXPERT_ROW_STEP, yb_ref.shape[0] + 1, EXPERT_ROW_STEP):
        pl.when(jnp.logical_and(valid > m - EXPERT_ROW_STEP, valid <= m))(functools.partial(ffn, m))

    @pl.when(valid == 0)
    def _():
        yb_ref[...] = jnp.zeros_like(yb_ref)


def _experts(xb, block_expert, next_expert, valid_rows, n_used, w1, b1, w2, b2, layer):
    n_rows, d = xb.shape
    tm = MOE_BLOCK
    return pl.pallas_call(
        functools.partial(_expert_kernel, layer=layer),
        grid_spec=pltpu.PrefetchScalarGridSpec(
            num_scalar_prefetch=4, grid=(n_rows // tm,),
            in_specs=[pl.BlockSpec((tm, d), lambda i, be, nx, vr, nu: (jnp.minimum(i, nu[0] - 1), 0)),
                      pl.BlockSpec(memory_space=pl.ANY),
                      pl.BlockSpec((None, N_EXPERTS, 1, 2 * D_FF), lambda i, be, nx, vr, nu: (layer, 0, 0, 0)),
                      pl.BlockSpec(memory_space=pl.ANY),
                      pl.BlockSpec((None, N_EXPERTS, 1, d), lambda i, be, nx, vr, nu: (layer, 0, 0, 0))],
            out_specs=pl.BlockSpec((tm, d), lambda i, be, nx, vr, nu: (i, 0)),
            scratch_shapes=[pltpu.VMEM((d, 2 * D_FF), F32), pltpu.VMEM((D_FF, d), F32),
                            pltpu.VMEM((d, 2 * D_FF), BF16), pltpu.VMEM((D_FF, d), BF16),
                            pltpu.SemaphoreType.DMA((2,))]),
        out_shape=jax.ShapeDtypeStruct((n_rows, d), F32),
        compiler_params=_params(1),
        name="moe_experts",
    )(block_expert, next_expert, valid_rows, n_used, xb, w1, b1, w2, b2)


def _gather_kernel(tmod_ref, seg_ref, off_ref, pos_ref, used_ref, ye_hbm, lpos_ref, wts_ref, x1_ref, mod_ref, g_ref,
                   *refs, tm, tile0, prenorm_next):
    del tmod_ref
    if prenorm_next:
        modn_ref, gn_ref, o_ref, hn_ref, buf_ref, sem = refs
    else:
        o_ref, buf_ref, sem = refs
    step = pl.program_id(0)
    t = step + tile0
    slot = step % 2

    def segments_of(tile):
        def segment(e):
            j = tile * N_EXPERTS + e
            half = (tile - tile0) % 2
            return _segment_chunks(seg_ref[j], ye_hbm, pos_ref[j], buf_ref.at[half], off_ref[j], sem.at[half], tm)
        return segment

    @pl.when(step == 0)
    def _():
        buf_ref[...] = jnp.zeros_like(buf_ref)
        _for_each_chunk(N_EXPERTS, segments_of(t), _start)

    @pl.when(step + 1 < pl.num_programs(0))
    def _():
        _for_each_chunk(N_EXPERTS, segments_of(t + 1), _start)

    _wait_rows(used_ref[t], ye_hbm, buf_ref.at[slot], sem.at[slot])

    lp = [lpos_ref[:, k:k + 1] for k in range(TOP_K)]
    wt = [jnp.broadcast_to(wts_ref[:, k:k + 1].astype(BF16), (tm, SORT_CHUNK)) for k in range(TOP_K)]
    c_off = lax.broadcasted_iota(I32, (tm, SORT_CHUNK), 1).astype(BF16)
    y = jnp.zeros(o_ref.shape, F32)
    for c0 in range(0, buf_ref.shape[1], SORT_CHUNK):
        wm = jnp.zeros((tm, SORT_CHUNK), BF16)
        for k in range(TOP_K):
            wm = jnp.where(c_off == _offset_in_chunk(lp[k], c0), wt[k], wm)
        y = y + jnp.dot(wm, buf_ref[slot, c0:c0 + SORT_CHUNK, :].astype(BF16), preferred_element_type=F32)
    x2 = x1_ref[...] + mod_ref[5:6, :] * (y * _rms(y) * g_ref[...])
    o_ref[...] = x2
    if prenorm_next:
        hn = x2 * _rms(x2) * gn_ref[...]
        hn_ref[...] = (hn * (1.0 + modn_ref[1:2, :]) + modn_ref[0:1, :]).astype(hn_ref.dtype)


def _gather_combine(ye, lpos_t, wts_t, plan, x1, mod, g_post, layer, tile_mod, row0=0, rows=None, g_pre_next=None):
    n, d = x1.shape
    tm = FIN_TILE
    rows = n if rows is None else rows
    tile0 = row0 // tm
    row = lambda i, *_: (tile0 + i, 0)
    out_row = lambda i, *_: (i, 0)
    mod_spec = lambda lay: pl.BlockSpec((None, None, 6, d), lambda i, t, *_: (lay, t[tile0 + i], 0, 0))
    gain_spec = lambda lay: pl.BlockSpec((None, 1, d), lambda i, *_: (lay, 0, 0))
    prenorm_next = g_pre_next is not None
    in_specs = [pl.BlockSpec(memory_space=pl.ANY),
                pl.BlockSpec((tm, TOP_K), row),
                pl.BlockSpec((tm, TOP_K), row),
                pl.BlockSpec((tm, d), row),
                mod_spec(layer), gain_spec(layer)]
    args = [ye, lpos_t, wts_t, x1, mod, g_post]
    out_specs = [pl.BlockSpec((tm, d), out_row)]
    out_shape = [jax.ShapeDtypeStruct((rows, d), F32)]
    if prenorm_next:
        in_specs += [mod_spec(layer + 1), gain_spec(layer + 1)]
        args += [mod, g_pre_next]
        out_specs += [pl.BlockSpec((tm, d), out_row)]
        out_shape += [jax.ShapeDtypeStruct((rows, d), BF16)]
    outs = pl.pallas_call(
        functools.partial(_gather_kernel, tm=tm, tile0=tile0, prenorm_next=prenorm_next),
        grid_spec=pltpu.PrefetchScalarGridSpec(
            num_scalar_prefetch=5, grid=(rows // tm,),
            in_specs=in_specs, out_specs=out_specs,
            scratch_shapes=[pltpu.VMEM((2, SORT_ROWS, d), F32), pltpu.SemaphoreType.DMA((2,))]),
        out_shape=out_shape,
        compiler_params=_params(1),
        name="moe_gather",
    )(tile_mod, plan["seg"], plan["off"], plan["pos"], plan["used"], *args)
    return outs if prenorm_next else outs[0]


def _moe(h2, wts, lpos, seg, off, x1, mod, g_post, w1, b1, w2, b2, layer, tile_mod, split_rows=None,
         g_pre_next=None):
    n, d = h2.shape
    blk = MOE_BLOCK
    tm = FIN_TILE
    tiles = n // tm
    n_rows = -(-(n * TOP_K + tiles * N_EXPERTS * (SEG_ALIGN - 1) + N_EXPERTS * (blk - 1)) // blk) * blk
    n_blocks = n_rows // blk
    seg = seg[:, :, 0].astype(I32)
    off = off[:, :, 0].astype(I32)
    rows_e = jnp.sum(seg, axis=0)
    region = (rows_e + blk - 1) // blk * blk
    pend = jnp.cumsum(region)
    pstart = pend - region
    pos = pstart[None, :] + jnp.cumsum(seg, axis=0) - seg
    plan = {"seg": seg.reshape(-1), "off": off.reshape(-1), "pos": pos.reshape(-1).astype(I32),
            "used": jnp.sum(seg, axis=1).astype(I32),
            "tail_pos": (pstart + rows_e).astype(I32), "tail_len": (region - rows_e).astype(I32)}
    blocks = jnp.arange(n_blocks, dtype=I32) * blk
    block_expert = jnp.minimum(jnp.sum(blocks[:, None] >= pend[None, :], axis=1), N_EXPERTS - 1).astype(I32)
    n_used = (pend[-1:] // blk).astype(I32)
    valid_rows = jnp.clip((pstart + rows_e)[block_expert] - blocks, 0, blk).astype(I32)
    ids = jnp.arange(n_blocks, dtype=I32)
    run_start = jnp.logical_and(jnp.concatenate([jnp.ones((1,), bool), block_expert[1:] != block_expert[:-1]]),
                                ids < n_used[0])
    first_after = lax.cummin(jnp.where(run_start, ids, n_blocks)[::-1])[::-1]
    first_after = jnp.concatenate([first_after[1:], jnp.full((1,), n_blocks, I32)])
    next_expert = jnp.where(first_after < n_blocks, block_expert[jnp.minimum(first_after, n_blocks - 1)], -1)
    xe = _scatter_rows(h2, lpos, plan, n_rows)
    ye = _experts(xe, block_expert, next_expert.astype(I32), valid_rows, n_used, w1, b1, w2, b2, layer)
    combine = functools.partial(_gather_combine, ye, lpos.T, wts.T, plan, x1, mod, g_post, layer, tile_mod)
    if split_rows is None:
        return combine(g_pre_next=g_pre_next)
    return combine(row0=0, rows=split_rows), combine(row0=split_rows, rows=n - split_rows)


def _tile_mod_ids(n_ctx_rows, n_lat_rows, lat_seq, tm):
    ctx = np.zeros((n_ctx_rows // tm,), np.int32)
    lat = 1 + (np.arange(n_lat_rows // tm) * tm) // lat_seq
    return jnp.asarray(np.concatenate([ctx, lat.astype(np.int32)]))


def kernel(x_prompt, x_sample, cache_k, cache_v, state_ret_fwd, state_ret_bwd, c, c_ctx, w_mod, b_mod, g_pre_mix, g_post_mix, g_pre_ffn, g_post_ffn, w_in, na_rel_bias, ret_decay_fwd, ret_decay_bwd, w_branch, w_out, w_router, b_router, w_exp_in, b_exp_in, w_exp_out, b_exp_out):
    batch, seq, d = x_prompt.shape
    dec_batch, dec_seq, _ = x_sample.shape
    depth = w_in.shape[0]
    n_ctx = batch * seq
    n_lat = dec_batch * dec_seq
    assert 1 + dec_batch <= MOD_ROWS

    x = (x_prompt.reshape(n_ctx, d), x_sample.reshape(n_lat, d))
    cvec =jnp.concatenate([c_ctx[None], c, jnp.zeros((MOD_ROWS - 1 - dec_batch, d), F32)], axis=0)
    mod_all = _modulation(cvec, w_mod, b_mod).reshape(depth, MOD_ROWS, 6, d)
    tmod = {tm: _tile_mod_ids(n_ctx, n_lat, dec_seq, tm) for tm in (ROW_TILE, FIN_TILE)}
    past = cache_k.shape[2]
    ck = cache_k.reshape(dec_batch, depth, past, NA_WIDTH)
    cv = cache_v.reshape(dec_batch, depth, past, NA_WIDTH)
    lg_f = jax.nn.log_sigmoid(ret_decay_fwd.astype(F32)).reshape(-1)
    lg_b = jax.nn.log_sigmoid(ret_decay_bwd.astype(F32)).reshape(-1)
    bias_all = _neighbourhood_bias(na_rel_bias, dec_seq)
    vec = lambda g: g.reshape(depth, 1, d)
    g_pre_mix, g_post_mix, g_pre_ffn, g_post_ffn = vec(g_pre_mix), vec(g_post_mix), vec(g_pre_ffn), vec(g_post_ffn)
    w_branch_b = w_branch.astype(BF16)
    w_out_b = w_out.astype(BF16)
    w_router_t = jnp.swapaxes(w_router, 1, 2)
    b_router_l = jnp.broadcast_to(b_router[:, :, None], (depth, N_EXPERTS, LANES))
    b_exp_in = b_exp_in.reshape(depth, N_EXPERTS, 1, 2 * D_FF)
    b_exp_out = b_exp_out.reshape(depth, N_EXPERTS, 1, d)

    ks, vs, sfs, sbs = [], [], [], []
    h = _prenorm(x, g_pre_mix, mod_all, 0, tmod[ROW_TILE], ROW_TILE)
    for l in range(depth):
        last = l == depth - 1
        z_qkvu = _project(h, w_in, l, 0, QKVU_W, F32)
        z_ret = _project(h, w_in, l, QKVU_W, RET_W, BF16)
        z_gate = _project(h, w_in, l, QKVU_W + RET_W, GATE_W, BF16)
        ks.append(z_qkvu[:n_ctx, NA_WIDTH:2 * NA_WIDTH].reshape(batch, seq, NA_HEADS, NA_HEAD_DIM))
        vs.append(z_qkvu[:n_ctx, 2 * NA_WIDTH:3 * NA_WIDTH].reshape(batch, seq, NA_HEADS, NA_HEAD_DIM))

        a_pair = (_attention_ctx(z_qkvu, batch, seq),
                  _attention_lat(z_qkvu, ck, cv, bias_all, l, n_ctx, dec_batch, dec_seq))
        f_pair = (_fourier(z_qkvu, 0, batch, seq), _fourier(z_qkvu, n_ctx, dec_batch, dec_seq))
        r_ctx, s_f, s_b = _retention(z_ret, lg_f, lg_b, l, 0, batch, seq, rotary=False, state_out=True)
        (r_lat,) = _retention(z_ret, lg_f, lg_b, l, n_ctx, dec_batch, dec_seq, rotary=True,
                              states=(state_ret_fwd, state_ret_bwd))
        sfs.append(s_f)
        sbs.append(s_b)

        x1, h2, wts, lpos, seg, off = _finish(
            (a_pair, f_pair, (r_ctx, r_lat)), z_gate, x, mod_all, g_post_mix, g_pre_ffn,
            w_branch_b, w_out_b, w_router_t, b_router_l, l, tmod[FIN_TILE])
        out = _moe(h2, wts, lpos, seg, off, x1, mod_all, g_post_ffn,
                   w_exp_in, b_exp_in, w_exp_out, b_exp_out, l, tmod[FIN_TILE],
                   split_rows=n_ctx if last else None, g_pre_next=None if last else g_pre_mix)
        x, h = (out, None) if last else out

    y_prompt = x[0].reshape(batch, seq, d)
    y_sample = x[1].reshape(dec_batch, dec_seq, d)
    return (y_prompt, y_sample, jnp.stack(ks, axis=1), jnp.stack(vs, axis=1),
            jnp.stack(sfs, axis=1), jnp.stack(sbs, axis=1))
```

```python
import functools

import numpy as np
import jax
import jax.numpy as jnp
from jax import lax
from jax.experimental import pallas as pl
from jax.experimental.pallas import tpu as pltpu

F32 = jnp.float32
BF16 = jnp.bfloat16
I32 = jnp.int32

D_MODEL = 1024
GRID_W = 64
NA_HEADS = 8
NA_HEAD_DIM = 64
NA_WIDTH = NA_HEADS * NA_HEAD_DIM
WIN_H = 8
WIN_W = 16
KEY_SLAB_ROWS = 12
FOURIER_GROUPS = 4
FOURIER_GROUP_DIM = 128
RET_HEADS = 4
RET_KEY_DIM = 128
ROPE_BASE = 10000.0
BRANCH_WIDTH = 512
N_EXPERTS = 32
TOP_K = 4
D_FF = 1024
SWIGLU_LIMIT = 7.0
SWIGLU_ALPHA = 1.702
EPS = 1e-6
NEG_INF = -1e30

QKVU_W = 4 * NA_WIDTH
RET_W = 4 * BRANCH_WIDTH
GATE_W = 3 * D_MODEL
PROJ_TILE = 1024
PROJ_ROWS = 2048
ATTN_Q_TILE = 512
ATTN_PAIRS_PER_STEP = 2
RET_Q_TILE = 1024
RET_DECAY_BYTES = 4 * 1024 * 1024

LANES = 128
MOD_ROWS = 16
ROW_TILE = 1024
FIN_TILE = 512
MOE_BLOCK = 512
EXPERT_ROW_STEP = 128
SEG_ALIGN = 8
SORT_CHUNK = 256
SORT_ROWS = -(-(FIN_TILE * TOP_K + N_EXPERTS * (SEG_ALIGN - 1)) // SORT_CHUNK) * SORT_CHUNK
VMEM_LIMIT = 56 * 1024 * 1024


def _params(n_axes, vmem=VMEM_LIMIT):
    return pltpu.CompilerParams(dimension_semantics=("arbitrary",) * n_axes, vmem_limit_bytes=vmem)


def _rms(x):
    return lax.rsqrt(jnp.mean(x * x, axis=-1, keepdims=True) + EPS)


def _mod_kernel(cv_ref, w_ref, b_ref, o_ref):
    cv = cv_ref[...]
    s = (cv * jax.nn.sigmoid(cv)).astype(BF16)
    o_ref[...] = jnp.dot(s, w_ref[...].astype(BF16), preferred_element_type=F32) + b_ref[...]


def _modulation(cv, w_mod, b_mod):
    depth, d, n = w_mod.shape
    tn = 1536
    return pl.pallas_call(
        _mod_kernel,
        grid=(depth, n // tn),
        in_specs=[pl.BlockSpec((MOD_ROWS, d), lambda l, j: (0, 0)),
                  pl.BlockSpec((None, d, tn), lambda l, j: (l, 0, j)),
                  pl.BlockSpec((None, 1, tn), lambda l, j: (l, 0, j))],
        out_specs=pl.BlockSpec((None, MOD_ROWS, tn), lambda l, j: (l, 0, j)),
        out_shape=jax.ShapeDtypeStruct((depth, MOD_ROWS, n), F32),
        compiler_params=_params(2),
        name="modulation",
    )(cv, w_mod, b_mod.reshape(depth, 1, n))


def _token_rows(x, tm):
    if isinstance(x, tuple):
        d = x[0].shape[1]
        ct = x[0].shape[0] // tm
        specs = [pl.BlockSpec((tm, d), lambda i, *_: (jnp.minimum(i, ct - 1), 0)),
                 pl.BlockSpec((tm, d), lambda i, *_: (jnp.maximum(i - ct, 0), 0))]
        return specs, list(x), ct, x[0].shape[0] + x[1].shape[0]
    return [pl.BlockSpec((tm, x.shape[1]), lambda i, *_: (i, 0))], [x], None, x.shape[0]


def _load_rows(refs, ctx_tiles):
    if len(refs) == 1:
        return refs[0][...]
    return jnp.where(pl.program_id(0) < ctx_tiles, refs[0][...], refs[1][...])


def _prenorm_kernel(tmod_ref, *refs, ctx_tiles):
    del tmod_ref
    g_ref, mod_ref, o_ref = refs[-3:]
    x = _load_rows(refs[:-3], ctx_tiles)
    h = x * _rms(x) * g_ref[...]
    o_ref[...] = (h * (1.0 + mod_ref[1:2, :]) + mod_ref[0:1, :]).astype(o_ref.dtype)


def _prenorm(x, g, mod, layer, tile_mod, tm):
    x_specs, x_args, ctx_tiles, n = _token_rows(x, tm)
    d = x_args[0].shape[1]
    return pl.pallas_call(
        functools.partial(_prenorm_kernel, ctx_tiles=ctx_tiles),
        grid_spec=pltpu.PrefetchScalarGridSpec(
            num_scalar_prefetch=1, grid=(n // tm,),
            in_specs=x_specs + [pl.BlockSpec((None, 1, d), lambda i, t: (layer, 0, 0)),
                                pl.BlockSpec((None, None, 6, d), lambda i, t: (layer, t[i], 0, 0))],
            out_specs=pl.BlockSpec((tm, d), lambda i, t: (i, 0))),
        out_shape=jax.ShapeDtypeStruct((n, d), BF16),
        compiler_params=_params(1),
        name="prenorm",
    )(tile_mod, *x_args, g, mod)


def _proj_kernel(h_ref, w_ref, o_ref, wb_ref):
    @pl.when(pl.program_id(1) == 0)
    def _():
        wb_ref[...] = w_ref[...].astype(BF16)

    o_ref[...] = jnp.dot(h_ref[...], wb_ref[...], preferred_element_type=F32).astype(o_ref.dtype)


def _project(h, w, layer, col0, width, out_dtype):
    n, d = h.shape
    tm = PROJ_ROWS
    tn = PROJ_TILE
    cb = col0 // tn
    return pl.pallas_call(
        _proj_kernel,
        grid=(width // tn, n // tm),
        in_specs=[pl.BlockSpec((tm, d), lambda j, i: (i, 0)),
                  pl.BlockSpec((None, d, tn), lambda j, i: (layer, 0, cb + j))],
        out_specs=pl.BlockSpec((tm, tn), lambda j, i: (i, j)),
        out_shape=jax.ShapeDtypeStruct((n, width), out_dtype),
        scratch_shapes=[pltpu.VMEM((d, tn), BF16)],
        compiler_params=_params(2),
        name="in_proj",
    )(h, w)


def _head_pair_masks():
    lane = lax.broadcasted_iota(I32, (1, LANES), 1)
    first = lane < NA_HEAD_DIM
    return first, jnp.logical_not(first)


def _attn_ctx_kernel(q_ref, k_ref, v_ref, o_ref):
    masks = _head_pair_masks()
    scale = NA_HEAD_DIM ** -0.5
    for p in range(NA_WIDTH // LANES):
        cols = slice(LANES * p, LANES * (p + 1))
        q2 = q_ref[:, cols] * scale
        k2 = k_ref[:, cols].astype(BF16)
        v2 = v_ref[:, cols].astype(BF16)
        outs = []
        for m in masks:
            qa = jnp.where(m, q2, 0.0).astype(BF16)
            s = lax.dot_general(qa, k2, (((1,), (1,)), ((), ())), preferred_element_type=F32)
            e = jnp.exp(s - jnp.max(s, axis=-1, keepdims=True))
            den = jnp.sum(e, axis=-1, keepdims=True)
            outs.append(jnp.dot(e.astype(BF16), v2, preferred_element_type=F32) / den)
        o_ref[:, cols] = jnp.where(masks[0], outs[0], outs[1]).astype(o_ref.dtype)


def _attention_ctx(z_qkv, n_seq, seq):
    return pl.pallas_call(
        _attn_ctx_kernel,
        grid=(n_seq,),
        in_specs=[pl.BlockSpec((seq, NA_WIDTH), lambda b: (b, 0)),
                  pl.BlockSpec((seq, NA_WIDTH), lambda b: (b, 1)),
                  pl.BlockSpec((seq, NA_WIDTH), lambda b: (b, 2))],
        out_specs=pl.BlockSpec((seq, NA_WIDTH), lambda b: (b, 0)),
        out_shape=jax.ShapeDtypeStruct((n_seq * seq, NA_WIDTH), BF16),
        compiler_params=_params(1),
        name="attn_ctx",
    )(z_qkv, z_qkv, z_qkv)


def _attn_lat_kernel(q_ref, k_ref, v_ref, kc_ref, vc_ref, bias_ref, o_ref, kb_ref, vb_ref, *, tq):
    masks = _head_pair_masks()
    scale = NA_HEAD_DIM ** -0.5
    seq = q_ref.shape[0]
    slab = bias_ref.shape[3]
    per_half = seq // 2 // tq
    kb_ref[...] = k_ref[...].astype(BF16)
    vb_ref[...] = v_ref[...].astype(BF16)
    kc = kc_ref[...].astype(BF16)
    vc = vc_ref[...].astype(BF16)
    nt = (((1,), (1,)), ((), ()))

    def q_tile(qi, carry):
        rows = pl.ds(pl.multiple_of(qi * tq, tq), tq)
        half = qi // per_half
        half_rows = pl.ds(pl.multiple_of((qi % per_half) * tq, tq), tq)
        keys = pl.ds(pl.multiple_of(half * (seq - slab), seq - slab), slab)
        for pp in range(q_ref.shape[1] // LANES):
            cols = slice(LANES * pp, LANES * (pp + 1))
            k2 = kb_ref[keys, cols]
            v2 = vb_ref[keys, cols]
            q2 = q_ref[rows, cols] * scale
            outs = []
            for hh, m in enumerate(masks):
                qa = jnp.where(m, q2, 0.0).astype(BF16)
                s_lat = (lax.dot_general(qa, k2, nt, preferred_element_type=F32)
                         + bias_ref[2 * pp + hh, half, half_rows, :])
                s_ctx = lax.dot_general(qa, kc[:, cols], nt, preferred_element_type=F32)
                mx = jnp.maximum(jnp.max(s_lat, axis=-1, keepdims=True), jnp.max(s_ctx, axis=-1, keepdims=True))
                e_lat = jnp.exp(s_lat - mx)
                e_ctx = jnp.exp(s_ctx - mx)
                den = jnp.sum(e_lat, axis=-1, keepdims=True) + jnp.sum(e_ctx, axis=-1, keepdims=True)
                o = (jnp.dot(e_lat.astype(BF16), v2, preferred_element_type=F32)
                     + jnp.dot(e_ctx.astype(BF16), vc[:, cols], preferred_element_type=F32))
                outs.append(o / den)
            o_ref[rows, cols] = jnp.where(masks[0], outs[0], outs[1]).astype(o_ref.dtype)
        return carry

    lax.fori_loop(0, q_ref.shape[0] // tq, q_tile, 0, unroll=True)


def _attention_lat(z_qkv, cache_k, cache_v, bias, layer, row0, n_seq, seq):
    past = cache_k.shape[2]
    width = ATTN_PAIRS_PER_STEP * LANES
    steps = NA_WIDTH // width
    rb = row0 // seq
    slab = bias.shape[-1]
    return pl.pallas_call(
        functools.partial(_attn_lat_kernel, tq=min(seq // 2, ATTN_Q_TILE)),
        grid=(steps, n_seq),
        in_specs=[pl.BlockSpec((seq, width), lambda p, b: (rb + b, p)),
                  pl.BlockSpec((seq, width), lambda p, b: (rb + b, steps + p)),
                  pl.BlockSpec((seq, width), lambda p, b: (rb + b, 2 * steps + p)),
                  pl.BlockSpec((None, None, past, width), lambda p, b: (b, layer, 0, p)),
                  pl.BlockSpec((None, None, past, width), lambda p, b: (b, layer, 0, p)),
                  pl.BlockSpec((None, 2 * ATTN_PAIRS_PER_STEP, 2, seq // 2, slab),
                               lambda p, b: (layer, p, 0, 0, 0))],
        out_specs=pl.BlockSpec((seq, width), lambda p, b: (b, p)),
        out_shape=jax.ShapeDtypeStruct((n_seq * seq, NA_WIDTH), BF16),
        scratch_shapes=[pltpu.VMEM((seq, width), BF16), pltpu.VMEM((seq, width), BF16)],
        compiler_params=_params(2),
        name="attn_lat",
    )(z_qkv, z_qkv, z_qkv, cache_k, cache_v, bias)


def _neighbourhood_bias(rpb, seq):
    rows = seq // GRID_W
    kh = WIN_H
    assert rows >= WIN_H
    lead = rpb.shape[:-2]
    c = np.arange(GRID_W)
    q_cs = np.clip(c - WIN_W // 2, 0, GRID_W - WIN_W)
    col_ok = (c[None, :] >= q_cs[:, None]) & (c[None, :] < q_cs[:, None] + WIN_W)
    r = np.arange(rows)
    rs = np.clip(r - kh // 2, 0, rows - kh)
    base = np.where(r < rows // 2, 0, rows - KEY_SLAB_ROWS)
    assert (rs >= base).all() and (rs + kh <= base + KEY_SLAB_ROWS).all()
    pick_c = (c[None, None, :] - c[None, :, None] + WIN_W - 1
              == np.arange(2 * WIN_W - 1)[:, None, None]).astype(np.float32)
    w = jnp.einsum("...ij,jqk->...qik", rpb, pick_c, precision=lax.Precision.HIGHEST)
    w = jnp.where(jnp.asarray(col_ok)[:, None, :], w, NEG_INF)
    blocks = []
    for rq in range(rows):
        lo = int(rs[rq]) - rq + WIN_H - 1
        slab = w[..., lo:lo + kh, :].reshape(lead + (GRID_W, kh * GRID_W))
        left = int(rs[rq] - base[rq])
        pad = ((0, 0),) * (len(lead) + 1) + ((left * GRID_W, (KEY_SLAB_ROWS - kh - left) * GRID_W),)
        blocks.append(jnp.pad(slab, pad, constant_values=NEG_INF))
    return jnp.stack(blocks, axis=-3).reshape(lead + (2, seq // 2, KEY_SLAB_ROWS * GRID_W))


def _fourier_kernel(u_ref, ct2_ref, cc_ref, sc_ref, o_ref, pq_ref):
    t = u_ref.shape[0]
    for g in range(FOURIER_GROUPS):
        cols = slice(FOURIER_GROUP_DIM * g, FOURIER_GROUP_DIM * (g + 1))
        ug = u_ref[:, cols].astype(BF16)
        pq_ref[0:t, cols] = jnp.dot(ug, cc_ref[...], preferred_element_type=F32).astype(BF16)
        pq_ref[t:2 * t, cols] = jnp.dot(ug, sc_ref[...], preferred_element_type=F32).astype(BF16)
    o_ref[...] = jnp.dot(ct2_ref[...], pq_ref[...], preferred_element_type=F32).astype(o_ref.dtype)


def _dft_tables(t):
    def cs(n):
        k = np.arange(n, dtype=np.int64)
        ang = 2.0 * np.pi * ((k[:, None] * k[None, :]) % n).astype(np.float64) / n
        return np.cos(ang) / np.sqrt(n), np.sin(ang) / np.sqrt(n)

    ct, st = cs(t)
    cc, sc = cs(FOURIER_GROUP_DIM)
    ct2 = np.concatenate([ct, -st], axis=1).astype(np.float32)
    return (jnp.asarray(ct2).astype(BF16), jnp.asarray(cc.astype(np.float32)).astype(BF16),
            jnp.asarray(sc.astype(np.float32)).astype(BF16))


def _fourier(z_qkvu, row0, n_seq, seq):
    ct2, cc, sc = _dft_tables(seq)
    width = FOURIER_GROUPS * FOURIER_GROUP_DIM
    rb = row0 // seq
    ucol = 3 * NA_WIDTH // width
    return pl.pallas_call(
        _fourier_kernel,
        grid=(n_seq,),
        in_specs=[pl.BlockSpec((seq, width), lambda b: (rb + b, ucol)),
                  pl.BlockSpec((seq, 2 * seq), lambda b: (0, 0)),
                  pl.BlockSpec((FOURIER_GROUP_DIM, FOURIER_GROUP_DIM), lambda b: (0, 0)),
                  pl.BlockSpec((FOURIER_GROUP_DIM, FOURIER_GROUP_DIM), lambda b: (0, 0))],
        out_specs=pl.BlockSpec((seq, width), lambda b: (b, 0)),
        out_shape=jax.ShapeDtypeStruct((n_seq * seq, width), BF16),
        scratch_shapes=[pltpu.VMEM((2 * seq, width), BF16)],
        compiler_params=_params(1),
        name="fourier",
    )(z_qkvu, ct2, cc, sc)


def _rotary_tables(t):
    pos = np.arange(t)
    row = (pos // GRID_W).astype(np.float64)
    col = (pos % GRID_W).astype(np.float64)
    nf = RET_KEY_DIM // 4
    inv_freq = ROPE_BASE ** (-np.arange(nf, dtype=np.float64) / nf)
    ar = row[:, None] * inv_freq[None]
    ac = col[:, None] * inv_freq[None]
    cos = np.concatenate([np.cos(ar), np.cos(ar), np.cos(ac), np.cos(ac)], axis=1)
    sin = np.concatenate([-np.sin(ar), np.sin(ar), -np.sin(ac), np.sin(ac)], axis=1)
    return jnp.asarray(cos.astype(np.float32)), jnp.asarray(sin.astype(np.float32))


def _ret_kernel(lgf_ref, lgb_ref, *refs, t, tq, layer, rotary, state_in, state_out):
    refs = list(refs)
    q_ref, k_ref, v_ref, g_ref = refs[:4]
    refs = refs[4:]
    if rotary:
        cos_ref, sin_ref = refs[:2]
        refs = refs[2:]
    if state_in:
        sf0_ref, sb0_ref = refs[:2]
        refs = refs[2:]
    o_ref = refs[0]
    refs = refs[1:]
    if state_out:
        sf_ref, sb_ref = refs[:2]
        refs = refs[2:]
    dec_ref, kb_ref = refs

    scale = RET_KEY_DIM ** -0.5
    nq = t // tq
    heads_here = dec_ref.shape[0]

    if rotary:
        lane = lax.broadcasted_iota(I32, (1, LANES), 1)
        low = (lane % (RET_KEY_DIM // 2)) < (RET_KEY_DIM // 4)

        def rot(x, rows):
            swapped = jnp.where(low, pltpu.roll(x, LANES - RET_KEY_DIM // 4, 1), pltpu.roll(x, RET_KEY_DIM // 4, 1))
            return x * cos_ref[rows, :] + swapped * sin_ref[rows, :]
    else:
        def rot(x, rows):
            return x

    def one_head(hh):
        h = pl.program_id(0) * heads_here + hh
        cols = slice(LANES * hh, LANES * (hh + 1))
        lgf = lgf_ref[layer * RET_HEADS + h]
        lgb = lgb_ref[layer * RET_HEADS + h]

        @pl.when(pl.program_id(1) == 0)
        def _():
            def fill(ri, c):
                rows = pl.ds(pl.multiple_of(ri * tq, tq), tq)
                i = lax.broadcasted_iota(I32, (tq, t), 0) + ri * tq
                j = lax.broadcasted_iota(I32, (tq, t), 1)
                d = (i - j).astype(F32)
                m = jnp.exp(jnp.abs(d) * jnp.where(d > 0, lgf, lgb))
                dec_ref[hh, rows, :] = jnp.where(d == 0, 2.0, m)
                return c

            lax.fori_loop(0, nq, fill, 0)

        kr = rot(k_ref[:, cols].astype(F32), slice(0, t))
        kb_ref[hh] = kr.astype(BF16)
        vb = v_ref[:, cols]

        if state_out:
            j = lax.broadcasted_iota(I32, (t, 1), 0).astype(F32)
            tn = (((0,), (0,)), ((), ()))
            kf = (kr * (scale * jnp.exp(lgf * (t - 1.0 - j)))).astype(BF16)
            kbw = (kr * (scale * jnp.exp(lgb * j))).astype(BF16)
            sf = lax.dot_general(kf, vb, tn, preferred_element_type=F32)
            sb = lax.dot_general(kbw, vb, tn, preferred_element_type=F32)
            if state_in:
                sf = sf + jnp.exp(lgf * t) * sf0_ref[hh]
                sb = sb + jnp.exp(lgb * t) * sb0_ref[hh]
            sf_ref[hh] = sf
            sb_ref[hh] = sb

        def q_tile(qi, carry):
            r0 = pl.multiple_of(qi * tq, tq)
            rows = pl.ds(r0, tq)
            qr = rot(q_ref[rows, cols].astype(F32), rows)
            s = lax.dot_general((qr * scale).astype(BF16), kb_ref[hh], (((1,), (1,)), ((), ())),
                                preferred_element_type=F32)
            y = jnp.dot((s * dec_ref[hh, rows, :]).astype(BF16), vb, preferred_element_type=F32)
            if state_in:
                pos = (lax.broadcasted_iota(I32, (tq, 1), 0) + r0).astype(F32)
                qf = (qr * jnp.exp(lgf * (pos + 1.0))).astype(BF16)
                qb = (qr * jnp.exp(lgb * (t - pos))).astype(BF16)
                y = (y + jnp.dot(qf, sf0_ref[hh].astype(BF16), preferred_element_type=F32)
                     + jnp.dot(qb, sb0_ref[hh].astype(BF16), preferred_element_type=F32))
            mean = jnp.mean(y, axis=-1, keepdims=True)
            yc = y - mean
            yn = yc * lax.rsqrt(jnp.mean(yc * yc, axis=-1, keepdims=True) + EPS)
            g = g_ref[rows, cols].astype(F32)
            o_ref[rows, cols] = (g * jax.nn.sigmoid(g) * yn).astype(o_ref.dtype)
            return carry

        lax.fori_loop(0, nq, q_tile, 0)

    for hh in range(heads_here):
        one_head(hh)


def _retention(z_ret, lg_f, lg_b, layer, row0, n_seq, seq, *, rotary, states=None, state_out=False):
    rb = row0 // seq
    tq = min(seq, RET_Q_TILE)
    state_in = states is not None
    hps = RET_HEADS if seq * seq * RET_HEADS * 4 <= RET_DECAY_BYTES else 1
    cb = BRANCH_WIDTH // (hps * LANES)
    width = hps * LANES
    in_specs = [pl.BlockSpec((seq, width), lambda h, b, *_: (rb + b, 0 * cb + h)),
                pl.BlockSpec((seq, width), lambda h, b, *_: (rb + b, 1 * cb + h)),
                pl.BlockSpec((seq, width), lambda h, b, *_: (rb + b, 2 * cb + h)),
                pl.BlockSpec((seq, width), lambda h, b, *_: (rb + b, 3 * cb + h))]
    args = [z_ret, z_ret, z_ret, z_ret]
    if rotary:
        cos, sin = _rotary_tables(seq)
        in_specs += [pl.BlockSpec((seq, LANES), lambda h, b, *_: (0, 0))] * 2
        args += [cos, sin]
    if state_in:
        st_spec = pl.BlockSpec((None, None, hps, RET_KEY_DIM, RET_KEY_DIM), lambda h, b, *_: (b, layer, h, 0, 0))
        in_specs += [st_spec, st_spec]
        args += list(states)
    out_specs = [pl.BlockSpec((seq, width), lambda h, b, *_: (b, h))]
    out_shape = [jax.ShapeDtypeStruct((n_seq * seq, RET_HEADS * LANES), BF16)]
    if state_out:
        so = pl.BlockSpec((None, hps, RET_KEY_DIM, RET_KEY_DIM), lambda h, b, *_: (b, h, 0, 0))
        out_specs += [so, so]
        out_shape += [jax.ShapeDtypeStruct((n_seq, RET_HEADS, RET_KEY_DIM, RET_KEY_DIM), F32)] * 2
    return pl.pallas_call(
        functools.partial(_ret_kernel, t=seq, tq=tq, layer=layer, rotary=rotary, state_in=state_in,
                          state_out=state_out),
        grid_spec=pltpu.PrefetchScalarGridSpec(
            num_scalar_prefetch=2, grid=(RET_HEADS // hps, n_seq),
            in_specs=in_specs, out_specs=out_specs,
            scratch_shapes=[pltpu.VMEM((hps, seq, seq), F32), pltpu.VMEM((hps, seq, LANES), BF16)]),
        out_shape=out_shape,
        compiler_params=_params(2),
        name="retention",
    )(lg_f, lg_b, *args)


def _split_dot_nt(w, x):
    nt = (((1,), (1,)), ((), ()))
    w_hi = w.astype(BF16)
    w_lo = (w - w_hi.astype(F32)).astype(BF16)
    x_hi = x.astype(BF16)
    x_lo = (x - x_hi.astype(F32)).astype(BF16)
    return (lax.dot_general(w_hi, x_hi, nt, preferred_element_type=F32)
            + lax.dot_general(w_hi, x_lo, nt, preferred_element_type=F32)
            + lax.dot_general(w_lo, x_hi, nt, preferred_element_type=F32))


def _finish_kernel(tmod_ref, *refs, ctx_tiles, n_x):
    del tmod_ref
    branch_refs, refs = refs[:6], refs[6:]
    zg_ref, refs = refs[0], refs[1:]
    x_refs, refs = refs[:n_x], refs[n_x:]
    (mod_ref, gpost_ref, gpre_ref, wb_ref, wo_ref, wrt_ref, br_ref, tri_ref, ltri_ref,
     x1_ref, h2_ref, wts_ref, lpos_ref, seg_ref, off_ref) = refs
    d = D_MODEL

    def branch(j):
        return _load_rows(branch_refs[2 * j:2 * j + 2], ctx_tiles)

    def gate(j):
        return jax.nn.sigmoid(zg_ref[:, d * j:d * (j + 1)].astype(F32))

    merged = (gate(0) * jnp.dot(branch(0), wb_ref[0], preferred_element_type=F32)
              + gate(1) * jnp.dot(branch(1), wb_ref[1], preferred_element_type=F32)
              + gate(2) * jnp.dot(branch(2), wb_ref[2], preferred_element_type=F32))
    y = jnp.dot(merged.astype(BF16), wo_ref[...], preferred_element_type=F32)
    x1 = _load_rows(x_refs, ctx_tiles) + mod_ref[2:3, :] * (y * _rms(y) * gpost_ref[...])
    x1_ref[...] = x1
    h2 = x1 * _rms(x1) * gpre_ref[...] * (1.0 + mod_ref[4:5, :]) + mod_ref[3:4, :]
    h2_ref[...] = h2

    logits = _split_dot_nt(wrt_ref[...], h2) + br_ref[:, 0:1]
    tm = logits.shape[1]
    eidx = lax.broadcasted_iota(I32, (N_EXPERTS, tm), 0)
    cur = logits
    vals, hots = [], []
    for k in range(TOP_K):
        m = jnp.max(cur, axis=0, keepdims=True)
        sel = jnp.min(jnp.where(cur == m, eidx, N_EXPERTS), axis=0, keepdims=True)
        hot = eidx == sel
        vals.append(m)
        hots.append(hot)
        cur = jnp.where(hot, -jnp.inf, cur)
    exps = [jnp.exp(v - vals[0]) for v in vals]
    den = exps[0] + exps[1] + exps[2] + exps[3]
    for k in range(TOP_K):
        wts_ref[k:k + 1, :] = exps[k] / den

    member = jnp.logical_or(jnp.logical_or(hots[0], hots[1]), jnp.logical_or(hots[2], hots[3]))
    member_f = member.astype(F32)
    before = jnp.dot(member_f.astype(BF16), tri_ref[...], preferred_element_type=F32)
    units = jnp.ceil(jnp.sum(member_f, axis=1, keepdims=True) * (1.0 / SEG_ALIGN))
    units = jnp.broadcast_to(units, seg_ref.shape)
    off = jnp.dot(ltri_ref[...], units.astype(BF16), preferred_element_type=F32) * SEG_ALIGN
    seg_ref[...] = units * SEG_ALIGN
    off_ref[...] = off
    place = before + off[:, 0:1]
    for k in range(TOP_K):
        lpos_ref[k:k + 1, :] = jnp.sum(jnp.where(hots[k], place, 0.0), axis=0, keepdims=True).astype(I32)


def _finish(branches, z_gate, x, mod, g_post, g_pre_ffn, w_branch, w_out, w_router_t, b_router, layer, tile_mod):
    tm = FIN_TILE
    x_specs, x_args, _, n = _token_rows(x, tm)
    d = x_args[0].shape[1]
    tri = jnp.asarray(np.triu(np.ones((tm, tm), np.float32), k=1)).astype(BF16)
    ltri = jnp.asarray(np.tril(np.ones((N_EXPERTS, N_EXPERTS), np.float32), k=-1)).astype(BF16)
    row = lambda i, t: (i, 0)
    const2 = lambda i, t: (0, 0)
    lay3 = lambda i, t: (layer, 0, 0)
    col = lambda i, t: (0, i)
    branch_specs, branch_args = [], []
    for pair in branches:
        specs, args, ctx_tiles, _ = _token_rows(pair, tm)
        branch_specs += specs
        branch_args += args
    outs = pl.pallas_call(
        functools.partial(_finish_kernel, ctx_tiles=ctx_tiles, n_x=len(x_args)),
        grid_spec=pltpu.PrefetchScalarGridSpec(
            num_scalar_prefetch=1, grid=(n // tm,),
            in_specs=branch_specs + [pl.BlockSpec((tm, GATE_W), row)] + x_specs + [
                      pl.BlockSpec((None, None, 6, d), lambda i, t: (layer, t[i], 0, 0)),
                      pl.BlockSpec((None, 1, d), lay3),
                      pl.BlockSpec((None, 1, d), lay3),
                      pl.BlockSpec((None, 3, BRANCH_WIDTH, d), lambda i, t: (layer, 0, 0, 0)),
                      pl.BlockSpec((None, d, d), lay3),
                      pl.BlockSpec((None, N_EXPERTS, d), lay3),
                      pl.BlockSpec((None, N_EXPERTS, LANES), lay3),
                      pl.BlockSpec((tm, tm), const2),
                      pl.BlockSpec((N_EXPERTS, N_EXPERTS), const2)],
            out_specs=[pl.BlockSpec((tm, d), row),
                       pl.BlockSpec((tm, d), row),
                       pl.BlockSpec((TOP_K, tm), col),
                       pl.BlockSpec((TOP_K, tm), col),
                       pl.BlockSpec((None, N_EXPERTS, LANES), lambda i, t: (i, 0, 0)),
                       pl.BlockSpec((None, N_EXPERTS, LANES), lambda i, t: (i, 0, 0))]),
        out_shape=[jax.ShapeDtypeStruct((n, d), F32),
                   jax.ShapeDtypeStruct((n, d), F32),
                   jax.ShapeDtypeStruct((TOP_K, n), F32),
                   jax.ShapeDtypeStruct((TOP_K, n), I32),
                   jax.ShapeDtypeStruct((n // tm, N_EXPERTS, LANES), F32),
                   jax.ShapeDtypeStruct((n // tm, N_EXPERTS, LANES), F32)],
        compiler_params=_params(1),
        name="merge_router",
    )(tile_mod, *branch_args, z_gate, *x_args, mod, g_post, g_pre_ffn, w_branch, w_out, w_router_t, b_router,
      tri, ltri)
    return outs


def _segment_chunks(length, src_ref, src0, dst_ref, dst0, sem, max_chunk, fixed_src=False):
    out = []
    chunk = max_chunk
    while chunk >= SEG_ALIGN:
        done = jnp.bitwise_and(length, ~(2 * chunk - 1))
        present = jnp.bitwise_and(length, chunk) != 0
        s = 0 if fixed_src else pl.multiple_of(src0 + done, SEG_ALIGN)
        dd = pl.multiple_of(dst0 + done, SEG_ALIGN)
        out.append((present, pltpu.make_async_copy(src_ref.at[pl.ds(s, chunk)], dst_ref.at[pl.ds(dd, chunk)], sem)))
        chunk //= 2
    return out


def _for_each_chunk(n_segments, chunks_of, action):
    def body(e, c):
        for present, cp in chunks_of(e):
            pl.when(present)(functools.partial(action, cp))
        return c

    lax.fori_loop(0, n_segments, body, 0, unroll=4)


def _start(cp):
    cp.start()


def _wait(cp):
    cp.wait()


def _wait_rows(total, src_ref, dst_ref, sem):
    chunk = pl.next_power_of_2(SORT_ROWS) // 2
    while chunk >= SEG_ALIGN:
        @pl.when(jnp.bitwise_and(total, chunk) != 0)
        def _(chunk=chunk):
            pltpu.make_async_copy(src_ref.at[pl.ds(0, chunk)], dst_ref.at[pl.ds(0, chunk)], sem).wait()
        chunk //= 2


def _offset_in_chunk(rows, c0):
    assert SORT_CHUNK == 256
    inside = lax.shift_right_logical(rows, 8) == c0 // SORT_CHUNK
    return jnp.where(inside, jnp.bitwise_and(rows, SORT_CHUNK - 1), -1).astype(BF16)


def _scatter_kernel(seg_ref, off_ref, pos_ref, used_ref, tpos_ref, tlen_ref, h_ref, lpos_ref, xe_hbm,
                    buf_ref, zero_ref, sem, *, tm):
    t = pl.program_id(0)
    slot = t % 2
    hb = h_ref[...].astype(BF16)
    lp = [lpos_ref[k:k + 1, :] for k in range(TOP_K)]
    rows = buf_ref.shape[1]
    r_off = lax.broadcasted_iota(I32, (SORT_CHUNK, tm), 0).astype(BF16)
    one = jnp.ones((SORT_CHUNK, tm), BF16)
    zero = jnp.zeros((SORT_CHUNK, tm), BF16)
    for c0 in range(0, rows, SORT_CHUNK):
        here = [_offset_in_chunk(p, c0) for p in lp]
        hit = jnp.logical_or(jnp.logical_or(r_off == here[0], r_off == here[1]),
                             jnp.logical_or(r_off == here[2], r_off == here[3]))
        onehot = jnp.where(hit, one, zero)
        buf_ref[slot, c0:c0 + SORT_CHUNK, :] = jnp.dot(onehot, hb, preferred_element_type=F32)

    def segments_of(tile):
        def segment(e):
            j = tile * N_EXPERTS + e
            return _segment_chunks(seg_ref[j], buf_ref.at[tile % 2], off_ref[j], xe_hbm, pos_ref[j],
                                   sem.at[tile % 2], tm)
        return segment

    @pl.when(t > 0)
    def _():
        _wait_rows(used_ref[t - 1], buf_ref.at[1 - slot], xe_hbm, sem.at[1 - slot])

    _for_each_chunk(N_EXPERTS, segments_of(t), _start)

    @pl.when(t == pl.num_programs(0) - 1)
    def _():
        _wait_rows(used_ref[t], buf_ref.at[slot], xe_hbm, sem.at[slot])
        zero_ref[...] = jnp.zeros_like(zero_ref)

        def tail(e):
            return _segment_chunks(tlen_ref[e], zero_ref, 0, xe_hbm, tpos_ref[e], sem.at[0], zero_ref.shape[0],
                                   fixed_src=True)

        _for_each_chunk(N_EXPERTS, tail, _start)
        _for_each_chunk(N_EXPERTS, tail, _wait)


def _scatter_rows(h2, lpos, plan, n_rows):
    n, d = h2.shape
    tm = FIN_TILE
    return pl.pallas_call(
        functools.partial(_scatter_kernel, tm=tm),
        grid_spec=pltpu.PrefetchScalarGridSpec(
            num_scalar_prefetch=6, grid=(n // tm,),
            in_specs=[pl.BlockSpec((tm, d), lambda i, *_: (i, 0)),
                      pl.BlockSpec((TOP_K, tm), lambda i, *_: (0, i))],
            out_specs=pl.BlockSpec(memory_space=pl.ANY),
            scratch_shapes=[pltpu.VMEM((2, SORT_ROWS, d), F32), pltpu.VMEM((MOE_BLOCK // 2, d), F32),
                            pltpu.SemaphoreType.DMA((2,))]),
        out_shape=jax.ShapeDtypeStruct((n_rows, d), F32),
        compiler_params=_params(1),
        name="moe_scatter",
    )(plan["seg"], plan["off"], plan["pos"], plan["used"], plan["tail_pos"], plan["tail_len"], h2, lpos)


def _expert_kernel(be_ref, nxt_ref, valid_ref, nu_ref, xb_ref, w1_hbm, b1_ref, w2_hbm, b2_ref, yb_ref,
                   w1s_ref, w2s_ref, w1b_ref, w2b_ref, sem, *, layer):
    i = pl.program_id(0)
    used = i < nu_ref[0]
    fresh = jnp.logical_or(i == 0, be_ref[i] != be_ref[jnp.maximum(i - 1, 0)])

    def fetch(e):
        return (pltpu.make_async_copy(w1_hbm.at[layer, e], w1s_ref, sem.at[0]),
                pltpu.make_async_copy(w2_hbm.at[layer, e], w2s_ref, sem.at[1]))

    @pl.when(i == 0)
    def _():
        for cp in fetch(be_ref[0]):
            cp.start()

    @pl.when(jnp.logical_and(used, fresh))
    def _():
        for cp in fetch(be_ref[i]):
            cp.wait()
        w1b_ref[...] = w1s_ref[...].astype(BF16)
        w2b_ref[...] = w2s_ref[...].astype(BF16)

        @pl.when(nxt_ref[i] >= 0)
        def _():
            for cp in fetch(nxt_ref[i]):
                cp.start()

    def ffn(m):
        e = be_ref[i]
        z = jnp.dot(xb_ref[0:m, :].astype(BF16), w1b_ref[...], preferred_element_type=F32) + b1_ref[e]
        glu = jnp.minimum(z[:, :D_FF], SWIGLU_LIMIT)
        lin = jnp.clip(z[:, D_FF:], -SWIGLU_LIMIT, SWIGLU_LIMIT)
        act = glu * jax.nn.sigmoid(SWIGLU_ALPHA * glu) * (lin + 1.0)
        yb_ref[0:m, :] = jnp.dot(act.astype(BF16), w2b_ref[...], preferred_element_type=F32) + b2_ref[e]
        if m < yb_ref.shape[0]:
            yb_ref[m:, :] = jnp.zeros((yb_ref.shape[0] - m, yb_ref.shape[1]), F32)

    valid = valid_ref[i]
    for m in range(EXPERT_ROW_STEP, yb_ref.shape[0] + 1, EXPERT_ROW_STEP):
        pl.when(jnp.logical_and(valid > m - EXPERT_ROW_STEP, valid <= m))(functools.partial(ffn, m))

    @pl.when(valid == 0)
    def _():
        yb_ref[...] = jnp.zeros_like(yb_ref)


def _experts(xb, block_expert, next_expert, valid_rows, n_used, w1, b1, w2, b2, layer):
    n_rows, d = xb.shape
    tm = MOE_BLOCK
    return pl.pallas_call(
        functools.partial(_expert_kernel, layer=layer),
        grid_spec=pltpu.PrefetchScalarGridSpec(
            num_scalar_prefetch=4, grid=(n_rows // tm,),
            in_specs=[pl.BlockSpec((tm, d), lambda i, be, nx, vr, nu: (jnp.minimum(i, nu[0] - 1), 0)),
                      pl.BlockSpec(memory_space=pl.ANY),
                      pl.BlockSpec((None, N_EXPERTS, 1, 2 * D_FF), lambda i, be, nx, vr, nu: (layer, 0, 0, 0)),
                      pl.BlockSpec(memory_space=pl.ANY),
                      pl.BlockSpec((None, N_EXPERTS, 1, d), lambda i, be, nx, vr, nu: (layer, 0, 0, 0))],
            out_specs=pl.BlockSpec((tm, d), lambda i, be, nx, vr, nu: (i, 0)),
            scratch_shapes=[pltpu.VMEM((d, 2 * D_FF), F32), pltpu.VMEM((D_FF, d), F32),
                            pltpu.VMEM((d, 2 * D_FF), BF16), pltpu.VMEM((D_FF, d), BF16),
                            pltpu.SemaphoreType.DMA((2,))]),
        out_shape=jax.ShapeDtypeStruct((n_rows, d), F32),
        compiler_params=_params(1),
        name="moe_experts",
    )(block_expert, next_expert, valid_rows, n_used, xb, w1, b1, w2, b2)


def _gather_kernel(tmod_ref, seg_ref, off_ref, pos_ref, used_ref, ye_hbm, lpos_ref, wts_ref, x1_ref, mod_ref, g_ref,
                   *refs, tm, tile0, prenorm_next):
    del tmod_ref
    if prenorm_next:
        modn_ref, gn_ref, o_ref, hn_ref, buf_ref, sem = refs
    else:
        o_ref, buf_ref, sem = refs
    step = pl.program_id(0)
    t = step + tile0
    slot = step % 2

    def segments_of(tile):
        def segment(e):
            j = tile * N_EXPERTS + e
            half = (tile - tile0) % 2
            return _segment_chunks(seg_ref[j], ye_hbm, pos_ref[j], buf_ref.at[half], off_ref[j], sem.at[half], tm)
        return segment

    @pl.when(step == 0)
    def _():
        buf_ref[...] = jnp.zeros_like(buf_ref)
        _for_each_chunk(N_EXPERTS, segments_of(t), _start)

    @pl.when(step + 1 < pl.num_programs(0))
    def _():
        _for_each_chunk(N_EXPERTS, segments_of(t + 1), _start)

    _wait_rows(used_ref[t], ye_hbm, buf_ref.at[slot], sem.at[slot])

    lp = [lpos_ref[:, k:k + 1] for k in range(TOP_K)]
    wt = [jnp.broadcast_to(wts_ref[:, k:k + 1].astype(BF16), (tm, SORT_CHUNK)) for k in range(TOP_K)]
    c_off = lax.broadcasted_iota(I32, (tm, SORT_CHUNK), 1).astype(BF16)
    y = jnp.zeros(o_ref.shape, F32)
    for c0 in range(0, buf_ref.shape[1], SORT_CHUNK):
        wm = jnp.zeros((tm, SORT_CHUNK), BF16)
        for k in range(TOP_K):
            wm = jnp.where(c_off == _offset_in_chunk(lp[k], c0), wt[k], wm)
        y = y + jnp.dot(wm, buf_ref[slot, c0:c0 + SORT_CHUNK, :].astype(BF16), preferred_element_type=F32)
    x2 = x1_ref[...] + mod_ref[5:6, :] * (y * _rms(y) * g_ref[...])
    o_ref[...] = x2
    if prenorm_next:
        hn = x2 * _rms(x2) * gn_ref[...]
        hn_ref[...] = (hn * (1.0 + modn_ref[1:2, :]) + modn_ref[0:1, :]).astype(hn_ref.dtype)


def _gather_combine(ye, lpos_t, wts_t, plan, x1, mod, g_post, layer, tile_mod, row0=0, rows=None, g_pre_next=None):
    n, d = x1.shape
    tm = FIN_TILE
    rows = n if rows is None else rows
    tile0 = row0 // tm
    row = lambda i, *_: (tile0 + i, 0)
    out_row = lambda i, *_: (i, 0)
    mod_spec = lambda lay: pl.BlockSpec((None, None, 6, d), lambda i, t, *_: (lay, t[tile0 + i], 0, 0))
    gain_spec = lambda lay: pl.BlockSpec((None, 1, d), lambda i, *_: (lay, 0, 0))
    prenorm_next = g_pre_next is not None
    in_specs = [pl.BlockSpec(memory_space=pl.ANY),
                pl.BlockSpec((tm, TOP_K), row),
                pl.BlockSpec((tm, TOP_K), row),
                pl.BlockSpec((tm, d), row),
                mod_spec(layer), gain_spec(layer)]
    args = [ye, lpos_t, wts_t, x1, mod, g_post]
    out_specs = [pl.BlockSpec((tm, d), out_row)]
    out_shape = [jax.ShapeDtypeStruct((rows, d), F32)]
    if prenorm_next:
        in_specs += [mod_spec(layer + 1), gain_spec(layer + 1)]
        args += [mod, g_pre_next]
        out_specs += [pl.BlockSpec((tm, d), out_row)]
        out_shape += [jax.ShapeDtypeStruct((rows, d), BF16)]
    outs = pl.pallas_call(
        functools.partial(_gather_kernel, tm=tm, tile0=tile0, prenorm_next=prenorm_next),
        grid_spec=pltpu.PrefetchScalarGridSpec(
            num_scalar_prefetch=5, grid=(rows // tm,),
            in_specs=in_specs, out_specs=out_specs,
            scratch_shapes=[pltpu.VMEM((2, SORT_ROWS, d), F32), pltpu.SemaphoreType.DMA((2,))]),
        out_shape=out_shape,
        compiler_params=_params(1),
        name="moe_gather",
    )(tile_mod, plan["seg"], plan["off"], plan["pos"], plan["used"], *args)
    return outs if prenorm_next else outs[0]


def _moe(h2, wts, lpos, seg, off, x1, mod, g_post, w1, b1, w2, b2, layer, tile_mod, split_rows=None,
         g_pre_next=None):
    n, d = h2.shape
    blk = MOE_BLOCK
    tm = FIN_TILE
    tiles = n // tm
    n_rows = -(-(n * TOP_K + tiles * N_EXPERTS * (SEG_ALIGN - 1) + N_EXPERTS * (blk - 1)) // blk) * blk
    n_blocks = n_rows // blk
    seg = seg[:, :, 0].astype(I32)
    off = off[:, :, 0].astype(I32)
    rows_e = jnp.sum(seg, axis=0)
    region = (rows_e + blk - 1) // blk * blk
    pend = jnp.cumsum(region)
    pstart = pend - region
    pos = pstart[None, :] + jnp.cumsum(seg, axis=0) - seg
    plan = {"seg": seg.reshape(-1), "off": off.reshape(-1), "pos": pos.reshape(-1).astype(I32),
            "used": jnp.sum(seg, axis=1).astype(I32),
            "tail_pos": (pstart + rows_e).astype(I32), "tail_len": (region - rows_e).astype(I32)}
    blocks = jnp.arange(n_blocks, dtype=I32) * blk
    block_expert = jnp.minimum(jnp.sum(blocks[:, None] >= pend[None, :], axis=1), N_EXPERTS - 1).astype(I32)
    n_used = (pend[-1:] // blk).astype(I32)
    valid_rows = jnp.clip((pstart + rows_e)[block_expert] - blocks, 0, blk).astype(I32)
    ids = jnp.arange(n_blocks, dtype=I32)
    run_start = jnp.logical_and(jnp.concatenate([jnp.ones((1,), bool), block_expert[1:] != block_expert[:-1]]),
                                ids < n_used[0])
    first_after = lax.cummin(jnp.where(run_start, ids, n_blocks)[::-1])[::-1]
    first_after = jnp.concatenate([first_after[1:], jnp.full((1,), n_blocks, I32)])
    next_expert = jnp.where(first_after < n_blocks, block_expert[jnp.minimum(first_after, n_blocks - 1)], -1)
    xe = _scatter_rows(h2, lpos, plan, n_rows)
    ye = _experts(xe, block_expert, next_expert.astype(I32), valid_rows, n_used, w1, b1, w2, b2, layer)
    combine = functools.partial(_gather_combine, ye, lpos.T, wts.T, plan, x1, mod, g_post, layer, tile_mod)
    if split_rows is None:
        return combine(g_pre_next=g_pre_next)
    return combine(row0=0, rows=split_rows), combine(row0=split_rows, rows=n - split_rows)


def _tile_mod_ids(n_ctx_rows, n_lat_rows, lat_seq, tm):
    ctx = np.zeros((n_ctx_rows // tm,), np.int32)
    lat = 1 + (np.arange(n_lat_rows // tm) * tm) // lat_seq
    return jnp.asarray(np.concatenate([ctx, lat.astype(np.int32)]))


def kernel(x_prompt, x_sample, cache_k, cache_v, state_ret_fwd, state_ret_bwd, c, c_ctx, w_mod, b_mod, g_pre_mix, g_post_mix, g_pre_ffn, g_post_ffn, w_in, na_rel_bias, ret_decay_fwd, ret_decay_bwd, w_branch, w_out, w_router, b_router, w_exp_in, b_exp_in, w_exp_out, b_exp_out):
    batch, seq, d = x_prompt.shape
    dec_batch, dec_seq, _ = x_sample.shape
    depth = w_in.shape[0]
    n_ctx = batch * seq
    n_lat = dec_batch * dec_seq
    assert 1 + dec_batch <= MOD_ROWS

    x = (x_prompt.reshape(n_ctx, d), x_sample.reshape(n_lat, d))
    cvec =jnp.concatenate([c_ctx[None], c, jnp.zeros((MOD_ROWS - 1 - dec_batch, d), F32)], axis=0)
    mod_all = _modulation(cvec, w_mod, b_mod).reshape(depth, MOD_ROWS, 6, d)
    tmod = {tm: _tile_mod_ids(n_ctx, n_lat, dec_seq, tm) for tm in (ROW_TILE, FIN_TILE)}
    past = cache_k.shape[2]
    ck = cache_k.reshape(dec_batch, depth, past, NA_WIDTH)
    cv = cache_v.reshape(dec_batch, depth, past, NA_WIDTH)
    lg_f = jax.nn.log_sigmoid(ret_decay_fwd.astype(F32)).reshape(-1)
    lg_b = jax.nn.log_sigmoid(ret_decay_bwd.astype(F32)).reshape(-1)
    bias_all = _neighbourhood_bias(na_rel_bias, dec_seq)
    vec = lambda g: g.reshape(depth, 1, d)
    g_pre_mix, g_post_mix, g_pre_ffn, g_post_ffn = vec(g_pre_mix), vec(g_post_mix), vec(g_pre_ffn), vec(g_post_ffn)
    w_branch_b = w_branch.astype(BF16)
    w_out_b = w_out.astype(BF16)
    w_router_t = jnp.swapaxes(w_router, 1, 2)
    b_router_l = jnp.broadcast_to(b_router[:, :, None], (depth, N_EXPERTS, LANES))
    b_exp_in = b_exp_in.reshape(depth, N_EXPERTS, 1, 2 * D_FF)
    b_exp_out = b_exp_out.reshape(depth, N_EXPERTS, 1, d)

    ks, vs, sfs, sbs = [], [], [], []
    h = _prenorm(x, g_pre_mix, mod_all, 0, tmod[ROW_TILE], ROW_TILE)
    for l in range(depth):
        last = l == depth - 1
        z_qkvu = _project(h, w_in, l, 0, QKVU_W, F32)
        z_ret = _project(h, w_in, l, QKVU_W, RET_W, BF16)
        z_gate = _project(h, w_in, l, QKVU_W + RET_W, GATE_W, BF16)
        ks.append(z_qkvu[:n_ctx, NA_WIDTH:2 * NA_WIDTH].reshape(batch, seq, NA_HEADS, NA_HEAD_DIM))
        vs.append(z_qkvu[:n_ctx, 2 * NA_WIDTH:3 * NA_WIDTH].reshape(batch, seq, NA_HEADS, NA_HEAD_DIM))

        a_pair = (_attention_ctx(z_qkvu, batch, seq),
                  _attention_lat(z_qkvu, ck, cv, bias_all, l, n_ctx, dec_batch, dec_seq))
        f_pair = (_fourier(z_qkvu, 0, batch, seq), _fourier(z_qkvu, n_ctx, dec_batch, dec_seq))
        r_ctx, s_f, s_b = _retention(z_ret, lg_f, lg_b, l, 0, batch, seq, rotary=False, state_out=True)
        (r_lat,) = _retention(z_ret, lg_f, lg_b, l, n_ctx, dec_batch, dec_seq, rotary=True,
                              states=(state_ret_fwd, state_ret_bwd))
        sfs.append(s_f)
        sbs.append(s_b)

        x1, h2, wts, lpos, seg, off = _finish(
            (a_pair, f_pair, (r_ctx, r_lat)), z_gate, x, mod_all, g_post_mix, g_pre_ffn,
            w_branch_b, w_out_b, w_router_t, b_router_l, l, tmod[FIN_TILE])
        out = _moe(h2, wts, lpos, seg, off, x1, mod_all, g_post_ffn,
                   w_exp_in, b_exp_in, w_exp_out, b_exp_out, l, tmod[FIN_TILE],
                   split_rows=n_ctx if last else None, g_pre_next=None if last else g_pre_mix)
        x, h = (out, None) if last else out

    y_prompt = x[0].reshape(batch, seq, d)
    y_sample = x[1].reshape(dec_batch, dec_seq, d)
    return (y_prompt, y_sample, jnp.stack(ks, axis=1), jnp.stack(vs, axis=1),
            jnp.stack(sfs, axis=1), jnp.stack(sbs, axis=1))
```

```python
import functools

import numpy as np
import jax
import jax.numpy as jnp
from jax import lax
from jax.experimental import pallas as pl
from jax.experimental.pallas import tpu as pltpu

F32 = jnp.float32
BF16 = jnp.bfloat16
I32 = jnp.int32

D_MODEL = 1024
GRID_W = 64
NA_HEADS = 8
NA_HEAD_DIM = 64
NA_WIDTH = NA_HEADS * NA_HEAD_DIM
WIN_H = 8
WIN_W = 16
KEY_SLAB_ROWS = 12
FOURIER_GROUPS = 4
FOURIER_GROUP_DIM = 128
RET_HEADS = 4
RET_KEY_DIM = 128
ROPE_BASE = 10000.0
BRANCH_WIDTH = 512
N_EXPERTS = 32
TOP_K = 4
D_FF = 1024
SWIGLU_LIMIT = 7.0
SWIGLU_ALPHA = 1.702
EPS = 1e-6
NEG_INF = -1e30

QKVU_W = 4 * NA_WIDTH
RET_W = 4 * BRANCH_WIDTH
GATE_W = 3 * D_MODEL
PROJ_TILE = 1024
PROJ_ROWS = 2048
ATTN_Q_TILE = 512
ATTN_PAIRS_PER_STEP = 2
RET_Q_TILE = 1024
RET_DECAY_BYTES = 4 * 1024 * 1024

LANES = 128
MOD_ROWS = 16
ROW_TILE = 1024
FIN_TILE = 512
MOE_BLOCK = 512
EXPERT_ROW_STEP = 128
SEG_ALIGN = 8
SORT_CHUNK = 256
SORT_ROWS = -(-(FIN_TILE * TOP_K + N_EXPERTS * (SEG_ALIGN - 1)) // SORT_CHUNK) * SORT_CHUNK
VMEM_LIMIT = 56 * 1024 * 1024


def _params(n_axes, vmem=VMEM_LIMIT):
    return pltpu.CompilerParams(dimension_semantics=("arbitrary",) * n_axes, vmem_limit_bytes=vmem)


def _rms(x):
    return lax.rsqrt(jnp.mean(x * x, axis=-1, keepdims=True) + EPS)


def _mod_kernel(cv_ref, w_ref, b_ref, o_ref):
    cv = cv_ref[...]
    s = (cv * jax.nn.sigmoid(cv)).astype(BF16)
    o_ref[...] = jnp.dot(s, w_ref[...].astype(BF16), preferred_element_type=F32) + b_ref[...]


def _modulation(cv, w_mod, b_mod):
    depth, d, n = w_mod.shape
    tn = 1536
    return pl.pallas_call(
        _mod_kernel,
        grid=(depth, n // tn),
        in_specs=[pl.BlockSpec((MOD_ROWS, d), lambda l, j: (0, 0)),
                  pl.BlockSpec((None, d, tn), lambda l, j: (l, 0, j)),
                  pl.BlockSpec((None, 1, tn), lambda l, j: (l, 0, j))],
        out_specs=pl.BlockSpec((None, MOD_ROWS, tn), lambda l, j: (l, 0, j)),
        out_shape=jax.ShapeDtypeStruct((depth, MOD_ROWS, n), F32),
        compiler_params=_params(2),
        name="modulation",
    )(cv, w_mod, b_mod.reshape(depth, 1, n))


def _token_rows(x, tm):
    if isinstance(x, tuple):
        d = x[0].shape[1]
        ct = x[0].shape[0] // tm
        specs = [pl.BlockSpec((tm, d), lambda i, *_: (jnp.minimum(i, ct - 1), 0)),
                 pl.BlockSpec((tm, d), lambda i, *_: (jnp.maximum(i - ct, 0), 0))]
        return specs, list(x), ct, x[0].shape[0] + x[1].shape[0]
    return [pl.BlockSpec((tm, x.shape[1]), lambda i, *_: (i, 0))], [x], None, x.shape[0]


def _load_rows(refs, ctx_tiles):
    if len(refs) == 1:
        return refs[0][...]
    return jnp.where(pl.program_id(0) < ctx_tiles, refs[0][...], refs[1][...])


def _prenorm_kernel(tmod_ref, *refs, ctx_tiles):
    del tmod_ref
    g_ref, mod_ref, o_ref = refs[-3:]
    x = _load_rows(refs[:-3], ctx_tiles)
    h = x * _rms(x) * g_ref[...]
    o_ref[...] = (h * (1.0 + mod_ref[1:2, :]) + mod_ref[0:1, :]).astype(o_ref.dtype)


def _prenorm(x, g, mod, layer, tile_mod, tm):
    x_specs, x_args, ctx_tiles, n = _token_rows(x, tm)
    d = x_args[0].shape[1]
    return pl.pallas_call(
        functools.partial(_prenorm_kernel, ctx_tiles=ctx_tiles),
        grid_spec=pltpu.PrefetchScalarGridSpec(
            num_scalar_prefetch=1, grid=(n // tm,),
            in_specs=x_specs + [pl.BlockSpec((None, 1, d), lambda i, t: (layer, 0, 0)),
                                pl.BlockSpec((None, None, 6, d), lambda i, t: (layer, t[i], 0, 0))],
            out_specs=pl.BlockSpec((tm, d), lambda i, t: (i, 0))),
        out_shape=jax.ShapeDtypeStruct((n, d), BF16),
        compiler_params=_params(1),
        name="prenorm",
    )(tile_mod, *x_args, g, mod)


def _proj_kernel(h_ref, w_ref, o_ref, wb_ref):
    @pl.when(pl.program_id(1) == 0)
    def _():
        wb_ref[...] = w_ref[...].astype(BF16)

    o_ref[...] = jnp.dot(h_ref[...], wb_ref[...], preferred_element_type=F32).astype(o_ref.dtype)


def _project(h, w, layer, col0, width, out_dtype):
    n, d = h.shape
    tm = PROJ_ROWS
    tn = PROJ_TILE
    cb = col0 // tn
    return pl.pallas_call(
        _proj_kernel,
        grid=(width // tn, n // tm),
        in_specs=[pl.BlockSpec((tm, d), lambda j, i: (i, 0)),
                  pl.BlockSpec((None, d, tn), lambda j, i: (layer, 0, cb + j))],
        out_specs=pl.BlockSpec((tm, tn), lambda j, i: (i, j)),
        out_shape=jax.ShapeDtypeStruct((n, width), out_dtype),
        scratch_shapes=[pltpu.VMEM((d, tn), BF16)],
        compiler_params=_params(2),
        name="in_proj",
    )(h, w)


def _head_pair_masks():
    lane = lax.broadcasted_iota(I32, (1, LANES), 1)
    first = lane < NA_HEAD_DIM
    return first, jnp.logical_not(first)


def _attn_ctx_kernel(q_ref, k_ref, v_ref, o_ref):
    masks = _head_pair_masks()
    scale = NA_HEAD_DIM ** -0.5
    for p in range(NA_WIDTH // LANES):
        cols = slice(LANES * p, LANES * (p + 1))
        q2 = q_ref[:, cols] * scale
        k2 = k_ref[:, cols].astype(BF16)
        v2 = v_ref[:, cols].astype(BF16)
        outs = []
        for m in masks:
            qa = jnp.where(m, q2, 0.0).astype(BF16)
            s = lax.dot_general(qa, k2, (((1,), (1,)), ((), ())), preferred_element_type=F32)
            e = jnp.exp(s - jnp.max(s, axis=-1, keepdims=True))
            den = jnp.sum(e, axis=-1, keepdims=True)
            outs.append(jnp.dot(e.astype(BF16), v2, preferred_element_type=F32) / den)
        o_ref[:, cols] = jnp.where(masks[0], outs[0], outs[1]).astype(o_ref.dtype)


def _attention_ctx(z_qkv, n_seq, seq):
    return pl.pallas_call(
        _attn_ctx_kernel,
        grid=(n_seq,),
        in_specs=[pl.BlockSpec((seq, NA_WIDTH), lambda b: (b, 0)),
                  pl.BlockSpec((seq, NA_WIDTH), lambda b: (b, 1)),
                  pl.BlockSpec((seq, NA_WIDTH), lambda b: (b, 2))],
        out_specs=pl.BlockSpec((seq, NA_WIDTH), lambda b: (b, 0)),
        out_shape=jax.ShapeDtypeStruct((n_seq * seq, NA_WIDTH), BF16),
        compiler_params=_params(1),
        name="attn_ctx",
    )(z_qkv, z_qkv, z_qkv)


def _attn_lat_kernel(q_ref, k_ref, v_ref, kc_ref, vc_ref, bias_ref, o_ref, kb_ref, vb_ref, *, tq):
    masks = _head_pair_masks()
    scale = NA_HEAD_DIM ** -0.5
    seq = q_ref.shape[0]
    slab = bias_ref.shape[3]
    per_half = seq // 2 // tq
    kb_ref[...] = k_ref[...].astype(BF16)
    vb_ref[...] = v_ref[...].astype(BF16)
    kc = kc_ref[...].astype(BF16)
    vc = vc_ref[...].astype(BF16)
    nt = (((1,), (1,)), ((), ()))

    def q_tile(qi, carry):
        rows = pl.ds(pl.multiple_of(qi * tq, tq), tq)
        half = qi // per_half
        half_rows = pl.ds(pl.multiple_of((qi % per_half) * tq, tq), tq)
        keys = pl.ds(pl.multiple_of(half * (seq - slab), seq - slab), slab)
        for pp in range(q_ref.shape[1] // LANES):
            cols = slice(LANES * pp, LANES * (pp + 1))
            k2 = kb_ref[keys, cols]
            v2 = vb_ref[keys, cols]
            q2 = q_ref[rows, cols] * scale
            outs = []
            for hh, m in enumerate(masks):
                qa = jnp.where(m, q2, 0.0).astype(BF16)
                s_lat = (lax.dot_general(qa, k2, nt, preferred_element_type=F32)
                         + bias_ref[2 * pp + hh, half, half_rows, :])
                s_ctx = lax.dot_general(qa, kc[:, cols], nt, preferred_element_type=F32)
                mx = jnp.maximum(jnp.max(s_lat, axis=-1, keepdims=True), jnp.max(s_ctx, axis=-1, keepdims=True))
                e_lat = jnp.exp(s_lat - mx)
                e_ctx = jnp.exp(s_ctx - mx)
                den = jnp.sum(e_lat, axis=-1, keepdims=True) + jnp.sum(e_ctx, axis=-1, keepdims=True)
                o = (jnp.dot(e_lat.astype(BF16), v2, preferred_element_type=F32)
                     + jnp.dot(e_ctx.astype(BF16), vc[:, cols], preferred_element_type=F32))
                outs.append(o / den)
            o_ref[rows, cols] = jnp.where(masks[0], outs[0], outs[1]).astype(o_ref.dtype)
        return carry

    lax.fori_loop(0, q_ref.shape[0] // tq, q_tile, 0, unroll=True)


def _attention_lat(z_qkv, cache_k, cache_v, bias, layer, row0, n_seq, seq):
    past = cache_k.shape[2]
    width = ATTN_PAIRS_PER_STEP * LANES
    steps = NA_WIDTH // width
    rb = row0 // seq
    slab = bias.shape[-1]
    return pl.pallas_call(
        functools.partial(_attn_lat_kernel, tq=min(seq // 2, ATTN_Q_TILE)),
        grid=(steps, n_seq),
        in_specs=[pl.BlockSpec((seq, width), lambda p, b: (rb + b, p)),
                  pl.BlockSpec((seq, width), lambda p, b: (rb + b, steps + p)),
                  pl.BlockSpec((seq, width), lambda p, b: (rb + b, 2 * steps + p)),
                  pl.BlockSpec((None, None, past, width), lambda p, b: (b, layer, 0, p)),
                  pl.BlockSpec((None, None, past, width), lambda p, b: (b, layer, 0, p)),
                  pl.BlockSpec((None, 2 * ATTN_PAIRS_PER_STEP, 2, seq // 2, slab),
                               lambda p, b: (layer, p, 0, 0, 0))],
        out_specs=pl.BlockSpec((seq, width), lambda p, b: (b, p)),
        out_shape=jax.ShapeDtypeStruct((n_seq * seq, NA_WIDTH), BF16),
        scratch_shapes=[pltpu.VMEM((seq, width), BF16), pltpu.VMEM((seq, width), BF16)],
        compiler_params=_params(2),
        name="attn_lat",
    )(z_qkv, z_qkv, z_qkv, cache_k, cache_v, bias)


def _neighbourhood_bias(rpb, seq):
    rows = seq // GRID_W
    kh = WIN_H
    assert rows >= WIN_H
    lead = rpb.shape[:-2]
    c = np.arange(GRID_W)
    q_cs = np.clip(c - WIN_W // 2, 0, GRID_W - WIN_W)
    col_ok = (c[None, :] >= q_cs[:, None]) & (c[None, :] < q_cs[:, None] + WIN_W)
    r = np.arange(rows)
    rs = np.clip(r - kh // 2, 0, rows - kh)
    base = np.where(r < rows // 2, 0, rows - KEY_SLAB_ROWS)
    assert (rs >= base).all() and (rs + kh <= base + KEY_SLAB_ROWS).all()
    pick_c = (c[None, None, :] - c[None, :, None] + WIN_W - 1
              == np.arange(2 * WIN_W - 1)[:, None, None]).astype(np.float32)
    w = jnp.einsum("...ij,jqk->...iqk", rpb, pick_c, precision=lax.Precision.HIGHEST)
    w = jnp.where(jnp.asarray(col_ok), w, NEG_INF)
    none = jnp.full(lead + (1, GRID_W, GRID_W), NEG_INF, F32)
    first = jnp.concatenate([none, w], axis=-3)
    second = jnp.concatenate([w, none], axis=-3)
    masked = jnp.full_like(first, NEG_INF)
    pairs = jnp.stack([jnp.concatenate([first, second], axis=-1),
                       jnp.concatenate([masked, second], axis=-1),
                       jnp.concatenate([first, masked], axis=-1)], axis=-4)
    plan = []
    for rq in range(rows):
        lo = int(rs[rq]) - rq + WIN_H - 1
        row_plan = []
        for m in range(KEY_SLAB_ROWS // 2):
            i0 = int(base[rq]) + 2 * m - rq + WIN_H - 1
            in0, in1 = lo <= i0 < lo + kh, lo <= i0 + 1 < lo + kh
            variant = 0 if in0 and in1 else 1 if in1 else 2 if in0 else None
            row_plan.append((variant, i0 + 1))
        plan.append(row_plan)
    n_tables = int(np.prod(lead))
    out = pl.pallas_call(
        functools.partial(_bias_kernel, plan=plan, rows_per_half=rows // 2),
        grid=(n_tables,),
        in_specs=[pl.BlockSpec((None, 3, 2 * WIN_H, GRID_W, 2 * GRID_W), lambda i: (i, 0, 0, 0, 0))],
        out_specs=pl.BlockSpec((None, 2, seq // 2, KEY_SLAB_ROWS * GRID_W), lambda i: (i, 0, 0, 0)),
        out_shape=jax.ShapeDtypeStruct((n_tables, 2, seq // 2, KEY_SLAB_ROWS * GRID_W), F32),
        compiler_params=_params(1),
        name="attn_bias",
    )(pairs.reshape((n_tables, 3, 2 * WIN_H, GRID_W, 2 * GRID_W)))
    return out.reshape(lead + (2, seq // 2, KEY_SLAB_ROWS * GRID_W))


def _bias_kernel(pairs_ref, o_ref, *, plan, rows_per_half):
    block = 2 * GRID_W
    for rq, row_plan in enumerate(plan):
        hf, r0 = rq // rows_per_half, (rq % rows_per_half) * GRID_W
        for m, (variant, idx) in enumerate(row_plan):
            if variant is None:
                val = jnp.full((GRID_W, block), NEG_INF, F32)
            else:
                val = pairs_ref[variant, idx]
            o_ref[hf, r0:r0 + GRID_W, m * block:(m + 1) * block] = val


def _fourier_kernel(u_ref, ct2_ref, cc_ref, sc_ref, o_ref, pq_ref):
    t = u_ref.shape[0]
    for g in range(FOURIER_GROUPS):
        cols = slice(FOURIER_GROUP_DIM * g, FOURIER_GROUP_DIM * (g + 1))
        ug = u_ref[:, cols].astype(BF16)
        pq_ref[0:t, cols] = jnp.dot(ug, cc_ref[...], preferred_element_type=F32).astype(BF16)
        pq_ref[t:2 * t, cols] = jnp.dot(ug, sc_ref[...], preferred_element_type=F32).astype(BF16)
    o_ref[...] = jnp.dot(ct2_ref[...], pq_ref[...], preferred_element_type=F32).astype(o_ref.dtype)


def _dft_tables(t):
    def cs(n):
        k = np.arange(n, dtype=np.int64)
        ang = 2.0 * np.pi * ((k[:, None] * k[None, :]) % n).astype(np.float64) / n
        return np.cos(ang) / np.sqrt(n), np.sin(ang) / np.sqrt(n)

    ct, st = cs(t)
    cc, sc = cs(FOURIER_GROUP_DIM)
    ct2 = np.concatenate([ct, -st], axis=1).astype(np.float32)
    return (jnp.asarray(ct2).astype(BF16), jnp.asarray(cc.astype(np.float32)).astype(BF16),
            jnp.asarray(sc.astype(np.float32)).astype(BF16))


def _fourier(z_qkvu, row0, n_seq, seq):
    ct2, cc, sc = _dft_tables(seq)
    width = FOURIER_GROUPS * FOURIER_GROUP_DIM
    rb = row0 // seq
    ucol = 3 * NA_WIDTH // width
    return pl.pallas_call(
        _fourier_kernel,
        grid=(n_seq,),
        in_specs=[pl.BlockSpec((seq, width), lambda b: (rb + b, ucol)),
                  pl.BlockSpec((seq, 2 * seq), lambda b: (0, 0)),
                  pl.BlockSpec((FOURIER_GROUP_DIM, FOURIER_GROUP_DIM), lambda b: (0, 0)),
                  pl.BlockSpec((FOURIER_GROUP_DIM, FOURIER_GROUP_DIM), lambda b: (0, 0))],
        out_specs=pl.BlockSpec((seq, width), lambda b: (b, 0)),
        out_shape=jax.ShapeDtypeStruct((n_seq * seq, width), BF16),
        scratch_shapes=[pltpu.VMEM((2 * seq, width), BF16)],
        compiler_params=_params(1),
        name="fourier",
    )(z_qkvu, ct2, cc, sc)


def _rotary_tables(t):
    pos = np.arange(t)
    row = (pos // GRID_W).astype(np.float64)
    col = (pos % GRID_W).astype(np.float64)
    nf = RET_KEY_DIM // 4
    inv_freq = ROPE_BASE ** (-np.arange(nf, dtype=np.float64) / nf)
    ar = row[:, None] * inv_freq[None]
    ac = col[:, None] * inv_freq[None]
    cos = np.concatenate([np.cos(ar), np.cos(ar), np.cos(ac), np.cos(ac)], axis=1)
    sin = np.concatenate([-np.sin(ar), np.sin(ar), -np.sin(ac), np.sin(ac)], axis=1)
    return jnp.asarray(cos.astype(np.float32)), jnp.asarray(sin.astype(np.float32))


def _ret_kernel(lgf_ref, lgb_ref, *refs, t, tq, layer, rotary, state_in, state_out):
    refs = list(refs)
    q_ref, k_ref, v_ref, g_ref = refs[:4]
    refs = refs[4:]
    if rotary:
        cos_ref, sin_ref = refs[:2]
        refs = refs[2:]
    if state_in:
        sf0_ref, sb0_ref = refs[:2]
        refs = refs[2:]
    o_ref = refs[0]
    refs = refs[1:]
    if state_out:
        sf_ref, sb_ref = refs[:2]
        refs = refs[2:]
    dec_ref, kb_ref = refs

    scale = RET_KEY_DIM ** -0.5
    nq = t // tq
    heads_here = dec_ref.shape[0]

    if rotary:
        lane = lax.broadcasted_iota(I32, (1, LANES), 1)
        low = (lane % (RET_KEY_DIM // 2)) < (RET_KEY_DIM // 4)

        def rot(x, rows):
            swapped = jnp.where(low, pltpu.roll(x, LANES - RET_KEY_DIM // 4, 1), pltpu.roll(x, RET_KEY_DIM // 4, 1))
            return x * cos_ref[rows, :] + swapped * sin_ref[rows, :]
    else:
        def rot(x, rows):
            return x

    def one_head(hh):
        h = pl.program_id(0) * heads_here + hh
        cols = slice(LANES * hh, LANES * (hh + 1))
        lgf = lgf_ref[layer * RET_HEADS + h]
        lgb = lgb_ref[layer * RET_HEADS + h]

        @pl.when(pl.program_id(1) == 0)
        def _():
            def fill(ri, c):
                rows = pl.ds(pl.multiple_of(ri * tq, tq), tq)
                i = lax.broadcasted_iota(I32, (tq, t), 0) + ri * tq
                j = lax.broadcasted_iota(I32, (tq, t), 1)
                d = (i - j).astype(F32)
                m = jnp.exp(jnp.abs(d) * jnp.where(d > 0, lgf, lgb))
                dec_ref[hh, rows, :] = jnp.where(d == 0, 2.0, m)
                return c

            lax.fori_loop(0, nq, fill, 0)

        kr = rot(k_ref[:, cols].astype(F32), slice(0, t))
        kb_ref[hh] = kr.astype(BF16)
        vb = v_ref[:, cols]

        if state_out:
            j = lax.broadcasted_iota(I32, (t, 1), 0).astype(F32)
            tn = (((0,), (0,)), ((), ()))
            kf = (kr * (scale * jnp.exp(lgf * (t - 1.0 - j)))).astype(BF16)
            kbw = (kr * (scale * jnp.exp(lgb * j))).astype(BF16)
            sf = lax.dot_general(kf, vb, tn, preferred_element_type=F32)
            sb = lax.dot_general(kbw, vb, tn, preferred_element_type=F32)
            if state_in:
                sf = sf + jnp.exp(lgf * t) * sf0_ref[hh]
                sb = sb + jnp.exp(lgb * t) * sb0_ref[hh]
            sf_ref[hh] = sf
            sb_ref[hh] = sb

        def q_tile(qi, carry):
            r0 = pl.multiple_of(qi * tq, tq)
            rows = pl.ds(r0, tq)
            qr = rot(q_ref[rows, cols].astype(F32), rows)
            s = lax.dot_general((qr * scale).astype(BF16), kb_ref[hh], (((1,), (1,)), ((), ())),
                                preferred_element_type=F32)
            y = jnp.dot((s * dec_ref[hh, rows, :]).astype(BF16), vb, preferred_element_type=F32)
            if state_in:
                pos = (lax.broadcasted_iota(I32, (tq, 1), 0) + r0).astype(F32)
                qf = (qr * jnp.exp(lgf * (pos + 1.0))).astype(BF16)
                qb = (qr * jnp.exp(lgb * (t - pos))).astype(BF16)
                y = (y + jnp.dot(qf, sf0_ref[hh].astype(BF16), preferred_element_type=F32)
                     + jnp.dot(qb, sb0_ref[hh].astype(BF16), preferred_element_type=F32))
            mean = jnp.mean(y, axis=-1, keepdims=True)
            yc = y - mean
            yn = yc * lax.rsqrt(jnp.mean(yc * yc, axis=-1, keepdims=True) + EPS)
            g = g_ref[rows, cols].astype(F32)
            o_ref[rows, cols] = (g * jax.nn.sigmoid(g) * yn).astype(o_ref.dtype)
            return carry

        lax.fori_loop(0, nq, q_tile, 0)

    for hh in range(heads_here):
        one_head(hh)


def _retention(z_ret, lg_f, lg_b, layer, row0, n_seq, seq, *, rotary, states=None, state_out=False):
    rb = row0 // seq
    tq = min(seq, RET_Q_TILE)
    state_in = states is not None
    hps = RET_HEADS if seq * seq * RET_HEADS * 4 <= RET_DECAY_BYTES else 1
    cb = BRANCH_WIDTH // (hps * LANES)
    width = hps * LANES
    in_specs = [pl.BlockSpec((seq, width), lambda h, b, *_: (rb + b, 0 * cb + h)),
                pl.BlockSpec((seq, width), lambda h, b, *_: (rb + b, 1 * cb + h)),
                pl.BlockSpec((seq, width), lambda h, b, *_: (rb + b, 2 * cb + h)),
                pl.BlockSpec((seq, width), lambda h, b, *_: (rb + b, 3 * cb + h))]
    args = [z_ret, z_ret, z_ret, z_ret]
    if rotary:
        cos, sin = _rotary_tables(seq)
        in_specs += [pl.BlockSpec((seq, LANES), lambda h, b, *_: (0, 0))] * 2
        args += [cos, sin]
    if state_in:
        st_spec = pl.BlockSpec((None, None, hps, RET_KEY_DIM, RET_KEY_DIM), lambda h, b, *_: (b, layer, h, 0, 0))
        in_specs += [st_spec, st_spec]
        args += list(states)
    out_specs = [pl.BlockSpec((seq, width), lambda h, b, *_: (b, h))]
    out_shape = [jax.ShapeDtypeStruct((n_seq * seq, RET_HEADS * LANES), BF16)]
    if state_out:
        so = pl.BlockSpec((None, hps, RET_KEY_DIM, RET_KEY_DIM), lambda h, b, *_: (b, h, 0, 0))
        out_specs += [so, so]
        out_shape += [jax.ShapeDtypeStruct((n_seq, RET_HEADS, RET_KEY_DIM, RET_KEY_DIM), F32)] * 2
    return pl.pallas_call(
        functools.partial(_ret_kernel, t=seq, tq=tq, layer=layer, rotary=rotary, state_in=state_in,
                          state_out=state_out),
        grid_spec=pltpu.PrefetchScalarGridSpec(
            num_scalar_prefetch=2, grid=(RET_HEADS // hps, n_seq),
            in_specs=in_specs, out_specs=out_specs,
            scratch_shapes=[pltpu.VMEM((hps, seq, seq), F32), pltpu.VMEM((hps, seq, LANES), BF16)]),
        out_shape=out_shape,
        compiler_params=_params(2),
        name="retention",
    )(lg_f, lg_b, *args)


def _split_dot_nt(w, x):
    nt = (((1,), (1,)), ((), ()))
    w_hi = w.astype(BF16)
    w_lo = (w - w_hi.astype(F32)).astype(BF16)
    x_hi = x.astype(BF16)
    x_lo = (x - x_hi.astype(F32)).astype(BF16)
    return (lax.dot_general(w_hi, x_hi, nt, preferred_element_type=F32)
            + lax.dot_general(w_hi, x_lo, nt, preferred_element_type=F32)
            + lax.dot_general(w_lo, x_hi, nt, preferred_element_type=F32))


def _finish_kernel(tmod_ref, *refs, ctx_tiles, n_x):
    del tmod_ref
    branch_refs, refs = refs[:6], refs[6:]
    zg_ref, refs = refs[0], refs[1:]
    x_refs, refs = refs[:n_x], refs[n_x:]
    (mod_ref, gpost_ref, gpre_ref, wb_ref, wo_ref, wrt_ref, br_ref, tri_ref, ltri_ref,
     x1_ref, h2_ref, wts_ref, lpos_ref, seg_ref, off_ref) = refs
    d = D_MODEL

    def branch(j):
        return _load_rows(branch_refs[2 * j:2 * j + 2], ctx_tiles)

    def gate(j):
        return jax.nn.sigmoid(zg_ref[:, d * j:d * (j + 1)].astype(F32))

    merged = (gate(0) * jnp.dot(branch(0), wb_ref[0], preferred_element_type=F32)
              + gate(1) * jnp.dot(branch(1), wb_ref[1], preferred_element_type=F32)
              + gate(2) * jnp.dot(branch(2), wb_ref[2], preferred_element_type=F32))
    y = jnp.dot(merged.astype(BF16), wo_ref[...], preferred_element_type=F32)
    x1 = _load_rows(x_refs, ctx_tiles) + mod_ref[2:3, :] * (y * _rms(y) * gpost_ref[...])
    x1_ref[...] = x1
    h2 = x1 * _rms(x1) * gpre_ref[...] * (1.0 + mod_ref[4:5, :]) + mod_ref[3:4, :]
    h2_ref[...] = h2

    logits = _split_dot_nt(wrt_ref[...], h2) + br_ref[:, 0:1]
    tm = logits.shape[1]
    eidx = lax.broadcasted_iota(I32, (N_EXPERTS, tm), 0)
    cur = logits
    vals, hots = [], []
    for k in range(TOP_K):
        m = jnp.max(cur, axis=0, keepdims=True)
        sel = jnp.min(jnp.where(cur == m, eidx, N_EXPERTS), axis=0, keepdims=True)
        hot = eidx == sel
        vals.append(m)
        hots.append(hot)
        cur = jnp.where(hot, -jnp.inf, cur)
    exps = [jnp.exp(v - vals[0]) for v in vals]
    den = exps[0] + exps[1] + exps[2] + exps[3]
    for k in range(TOP_K):
        wts_ref[k:k + 1, :] = exps[k] / den

    member = jnp.logical_or(jnp.logical_or(hots[0], hots[1]), jnp.logical_or(hots[2], hots[3]))
    member_f = member.astype(F32)
    before = jnp.dot(member_f.astype(BF16), tri_ref[...], preferred_element_type=F32)
    units = jnp.ceil(jnp.sum(member_f, axis=1, keepdims=True) * (1.0 / SEG_ALIGN))
    units = jnp.broadcast_to(units, seg_ref.shape)
    off = jnp.dot(ltri_ref[...], units.astype(BF16), preferred_element_type=F32) * SEG_ALIGN
    seg_ref[...] = units * SEG_ALIGN
    off_ref[...] = off
    place = before + off[:, 0:1]
    for k in range(TOP_K):
        lpos_ref[k:k + 1, :] = jnp.sum(jnp.where(hots[k], place, 0.0), axis=0, keepdims=True).astype(I32)


def _finish(branches, z_gate, x, mod, g_post, g_pre_ffn, w_branch, w_out, w_router_t, b_router, layer, tile_mod):
    tm = FIN_TILE
    x_specs, x_args, _, n = _token_rows(x, tm)
    d = x_args[0].shape[1]
    tri = jnp.asarray(np.triu(np.ones((tm, tm), np.float32), k=1)).astype(BF16)
    ltri = jnp.asarray(np.tril(np.ones((N_EXPERTS, N_EXPERTS), np.float32), k=-1)).astype(BF16)
    row = lambda i, t: (i, 0)
    const2 = lambda i, t: (0, 0)
    lay3 = lambda i, t: (layer, 0, 0)
    col = lambda i, t: (0, i)
    branch_specs, branch_args = [], []
    for pair in branches:
        specs, args, ctx_tiles, _ = _token_rows(pair, tm)
        branch_specs += specs
        branch_args += args
    outs = pl.pallas_call(
        functools.partial(_finish_kernel, ctx_tiles=ctx_tiles, n_x=len(x_args)),
        grid_spec=pltpu.PrefetchScalarGridSpec(
            num_scalar_prefetch=1, grid=(n // tm,),
            in_specs=branch_specs + [pl.BlockSpec((tm, GATE_W), row)] + x_specs + [
                      pl.BlockSpec((None, None, 6, d), lambda i, t: (layer, t[i], 0, 0)),
                      pl.BlockSpec((None, 1, d), lay3),
                      pl.BlockSpec((None, 1, d), lay3),
                      pl.BlockSpec((None, 3, BRANCH_WIDTH, d), lambda i, t: (layer, 0, 0, 0)),
                      pl.BlockSpec((None, d, d), lay3),
                      pl.BlockSpec((None, N_EXPERTS, d), lay3),
                      pl.BlockSpec((None, N_EXPERTS, LANES), lay3),
                      pl.BlockSpec((tm, tm), const2),
                      pl.BlockSpec((N_EXPERTS, N_EXPERTS), const2)],
            out_specs=[pl.BlockSpec((tm, d), row),
                       pl.BlockSpec((tm, d), row),
                       pl.BlockSpec((TOP_K, tm), col),
                       pl.BlockSpec((TOP_K, tm), col),
                       pl.BlockSpec((None, N_EXPERTS, LANES), lambda i, t: (i, 0, 0)),
                       pl.BlockSpec((None, N_EXPERTS, LANES), lambda i, t: (i, 0, 0))]),
        out_shape=[jax.ShapeDtypeStruct((n, d), F32),
                   jax.ShapeDtypeStruct((n, d), F32),
                   jax.ShapeDtypeStruct((TOP_K, n), F32),
                   jax.ShapeDtypeStruct((TOP_K, n), I32),
                   jax.ShapeDtypeStruct((n // tm, N_EXPERTS, LANES), F32),
                   jax.ShapeDtypeStruct((n // tm, N_EXPERTS, LANES), F32)],
        compiler_params=_params(1),
        name="merge_router",
    )(tile_mod, *branch_args, z_gate, *x_args, mod, g_post, g_pre_ffn, w_branch, w_out, w_router_t, b_router,
      tri, ltri)
    return outs


def _segment_chunks(length, src_ref, src0, dst_ref, dst0, sem, max_chunk, fixed_src=False):
    out = []
    chunk = max_chunk
    while chunk >= SEG_ALIGN:
        done = jnp.bitwise_and(length, ~(2 * chunk - 1))
        present = jnp.bitwise_and(length, chunk) != 0
        s = 0 if fixed_src else pl.multiple_of(src0 + done, SEG_ALIGN)
        dd = pl.multiple_of(dst0 + done, SEG_ALIGN)
        out.append((present, pltpu.make_async_copy(src_ref.at[pl.ds(s, chunk)], dst_ref.at[pl.ds(dd, chunk)], sem)))
        chunk //= 2
    return out


def _for_each_chunk(n_segments, chunks_of, action):
    def body(e, c):
        for present, cp in chunks_of(e):
            pl.when(present)(functools.partial(action, cp))
        return c

    lax.fori_loop(0, n_segments, body, 0, unroll=4)


def _start(cp):
    cp.start()


def _wait(cp):
    cp.wait()


def _wait_rows(total, src_ref, dst_ref, sem):
    chunk = pl.next_power_of_2(SORT_ROWS) // 2
    while chunk >= SEG_ALIGN:
        @pl.when(jnp.bitwise_and(total, chunk) != 0)
        def _(chunk=chunk):
            pltpu.make_async_copy(src_ref.at[pl.ds(0, chunk)], dst_ref.at[pl.ds(0, chunk)], sem).wait()
        chunk //= 2


def _offset_in_chunk(rows, c0):
    assert SORT_CHUNK == 256
    inside = lax.shift_right_logical(rows, 8) == c0 // SORT_CHUNK
    return jnp.where(inside, jnp.bitwise_and(rows, SORT_CHUNK - 1), -1).astype(BF16)


def _scatter_kernel(seg_ref, off_ref, pos_ref, used_ref, tpos_ref, tlen_ref, h_ref, lpos_ref, xe_hbm,
                    buf_ref, zero_ref, sem, *, tm):
    t = pl.program_id(0)
    slot = t % 2
    hb = h_ref[...].astype(BF16)
    lp = [lpos_ref[k:k + 1, :] for k in range(TOP_K)]
    rows = buf_ref.shape[1]
    r_off = lax.broadcasted_iota(I32, (SORT_CHUNK, tm), 0).astype(BF16)
    one = jnp.ones((SORT_CHUNK, tm), BF16)
    zero = jnp.zeros((SORT_CHUNK, tm), BF16)
    for c0 in range(0, rows, SORT_CHUNK):
        here = [_offset_in_chunk(p, c0) for p in lp]
        hit = jnp.logical_or(jnp.logical_or(r_off == here[0], r_off == here[1]),
                             jnp.logical_or(r_off == here[2], r_off == here[3]))
        onehot = jnp.where(hit, one, zero)
        buf_ref[slot, c0:c0 + SORT_CHUNK, :] = jnp.dot(onehot, hb, preferred_element_type=F32)

    def segments_of(tile):
        def segment(e):
            j = tile * N_EXPERTS + e
            return _segment_chunks(seg_ref[j], buf_ref.at[tile % 2], off_ref[j], xe_hbm, pos_ref[j],
                                   sem.at[tile % 2], tm)
        return segment

    @pl.when(t > 0)
    def _():
        _wait_rows(used_ref[t - 1], buf_ref.at[1 - slot], xe_hbm, sem.at[1 - slot])

    _for_each_chunk(N_EXPERTS, segments_of(t), _start)

    @pl.when(t == pl.num_programs(0) - 1)
    def _():
        _wait_rows(used_ref[t], buf_ref.at[slot], xe_hbm, sem.at[slot])
        zero_ref[...] = jnp.zeros_like(zero_ref)

        def tail(e):
            return _segment_chunks(tlen_ref[e], zero_ref, 0, xe_hbm, tpos_ref[e], sem.at[0], zero_ref.shape[0],
                                   fixed_src=True)

        _for_each_chunk(N_EXPERTS, tail, _start)
        _for_each_chunk(N_EXPERTS, tail, _wait)


def _scatter_rows(h2, lpos, plan, n_rows):
    n, d = h2.shape
    tm = FIN_TILE
    return pl.pallas_call(
        functools.partial(_scatter_kernel, tm=tm),
        grid_spec=pltpu.PrefetchScalarGridSpec(
            num_scalar_prefetch=6, grid=(n // tm,),
            in_specs=[pl.BlockSpec((tm, d), lambda i, *_: (i, 0)),
                      pl.BlockSpec((TOP_K, tm), lambda i, *_: (0, i))],
            out_specs=pl.BlockSpec(memory_space=pl.ANY),
            scratch_shapes=[pltpu.VMEM((2, SORT_ROWS, d), F32), pltpu.VMEM((MOE_BLOCK // 2, d), F32),
                            pltpu.SemaphoreType.DMA((2,))]),
        out_shape=jax.ShapeDtypeStruct((n_rows, d), F32),
        compiler_params=_params(1),
        name="moe_scatter",
    )(plan["seg"], plan["off"], plan["pos"], plan["used"], plan["tail_pos"], plan["tail_len"], h2, lpos)


def _expert_kernel(be_ref, nxt_ref, valid_ref, nu_ref, xb_ref, w1_hbm, b1_ref, w2_hbm, b2_ref, yb_ref,
                   w1s_ref, w2s_ref, w1b_ref, w2b_ref, sem, *, layer):
    i = pl.program_id(0)
    used = i < nu_ref[0]
    fresh = jnp.logical_or(i == 0, be_ref[i] != be_ref[jnp.maximum(i - 1, 0)])

    def fetch(e):
        return (pltpu.make_async_copy(w1_hbm.at[layer, e], w1s_ref, sem.at[0]),
                pltpu.make_async_copy(w2_hbm.at[layer, e], w2s_ref, sem.at[1]))

    @pl.when(i == 0)
    def _():
        for cp in fetch(be_ref[0]):
            cp.start()

    @pl.when(jnp.logical_and(used, fresh))
    def _():
        for cp in fetch(be_ref[i]):
            cp.wait()
        w1b_ref[...] = w1s_ref[...].astype(BF16)
        w2b_ref[...] = w2s_ref[...].astype(BF16)

        @pl.when(nxt_ref[i] >= 0)
        def _():
            for cp in fetch(nxt_ref[i]):
                cp.start()

    def ffn(m):
        e = be_ref[i]
        z = jnp.dot(xb_ref[0:m, :].astype(BF16), w1b_ref[...], preferred_element_type=F32) + b1_ref[e]
        glu = jnp.minimum(z[:, :D_FF], SWIGLU_LIMIT)
        lin = jnp.clip(z[:, D_FF:], -SWIGLU_LIMIT, SWIGLU_LIMIT)
        act = glu * jax.nn.sigmoid(SWIGLU_ALPHA * glu) * (lin + 1.0)
        yb_ref[0:m, :] = jnp.dot(act.astype(BF16), w2b_ref[...], preferred_element_type=F32) + b2_ref[e]
        if m < yb_ref.shape[0]:
            yb_ref[m:, :] = jnp.zeros((yb_ref.shape[0] - m, yb_ref.shape[1]), F32)

    valid = valid_ref[i]
    for m in range(EXPERT_ROW_STEP, yb_ref.shape[0] + 1, EXPERT_ROW_STEP):
        pl.when(jnp.logical_and(valid > m - EXPERT_ROW_STEP, valid <= m))(functools.partial(ffn, m))

    @pl.when(valid == 0)
    def _():
        yb_ref[...] = jnp.zeros_like(yb_ref)


def _experts(xb, block_expert, next_expert, valid_rows, n_used, w1, b1, w2, b2, layer):
    n_rows, d = xb.shape
    tm = MOE_BLOCK
    return pl.pallas_call(
        functools.partial(_expert_kernel, layer=layer),
        grid_spec=pltpu.PrefetchScalarGridSpec(
            num_scalar_prefetch=4, grid=(n_rows // tm,),
            in_specs=[pl.BlockSpec((tm, d), lambda i, be, nx, vr, nu: (jnp.minimum(i, nu[0] - 1), 0)),
                      pl.BlockSpec(memory_space=pl.ANY),
                      pl.BlockSpec((None, N_EXPERTS, 1, 2 * D_FF), lambda i, be, nx, vr, nu: (layer, 0, 0, 0)),
                      pl.BlockSpec(memory_space=pl.ANY),
                      pl.BlockSpec((None, N_EXPERTS, 1, d), lambda i, be, nx, vr, nu: (layer, 0, 0, 0))],
            out_specs=pl.BlockSpec((tm, d), lambda i, be, nx, vr, nu: (i, 0)),
            scratch_shapes=[pltpu.VMEM((d, 2 * D_FF), F32), pltpu.VMEM((D_FF, d), F32),
                            pltpu.VMEM((d, 2 * D_FF), BF16), pltpu.VMEM((D_FF, d), BF16),
                            pltpu.SemaphoreType.DMA((2,))]),
        out_shape=jax.ShapeDtypeStruct((n_rows, d), F32),
        compiler_params=_params(1),
        name="moe_experts",
    )(block_expert, next_expert, valid_rows, n_used, xb, w1, b1, w2, b2)


def _gather_kernel(tmod_ref, seg_ref, off_ref, pos_ref, used_ref, ye_hbm, lpos_ref, wts_ref, x1_ref, mod_ref, g_ref,
                   *refs, tm, tile0, prenorm_next):
    del tmod_ref
    if prenorm_next:
        modn_ref, gn_ref, o_ref, hn_ref, buf_ref, sem = refs
    else:
        o_ref, buf_ref, sem = refs
    step = pl.program_id(0)
    t = step + tile0
    slot = step % 2

    def segments_of(tile):
        def segment(e):
            j = tile * N_EXPERTS + e
            half = (tile - tile0) % 2
            return _segment_chunks(seg_ref[j], ye_hbm, pos_ref[j], buf_ref.at[half], off_ref[j], sem.at[half], tm)
        return segment

    @pl.when(step == 0)
    def _():
        buf_ref[...] = jnp.zeros_like(buf_ref)
        _for_each_chunk(N_EXPERTS, segments_of(t), _start)

    @pl.when(step + 1 < pl.num_programs(0))
    def _():
        _for_each_chunk(N_EXPERTS, segments_of(t + 1), _start)

    _wait_rows(used_ref[t], ye_hbm, buf_ref.at[slot], sem.at[slot])

    lp = [lpos_ref[:, k:k + 1] for k in range(TOP_K)]
    wt = [jnp.broadcast_to(wts_ref[:, k:k + 1].astype(BF16), (tm, SORT_CHUNK)) for k in range(TOP_K)]
    c_off = lax.broadcasted_iota(I32, (tm, SORT_CHUNK), 1).astype(BF16)
    y = jnp.zeros(o_ref.shape, F32)
    for c0 in range(0, buf_ref.shape[1], SORT_CHUNK):
        wm = jnp.zeros((tm, SORT_CHUNK), BF16)
        for k in range(TOP_K):
            wm = jnp.where(c_off == _offset_in_chunk(lp[k], c0), wt[k], wm)
        y = y + jnp.dot(wm, buf_ref[slot, c0:c0 + SORT_CHUNK, :].astype(BF16), preferred_element_type=F32)
    x2 = x1_ref[...] + mod_ref[5:6, :] * (y * _rms(y) * g_ref[...])
    o_ref[...] = x2
    if prenorm_next:
        hn = x2 * _rms(x2) * gn_ref[...]
        hn_ref[...] = (hn * (1.0 + modn_ref[1:2, :]) + modn_ref[0:1, :]).astype(hn_ref.dtype)


def _gather_combine(ye, lpos_t, wts_t, plan, x1, mod, g_post, layer, tile_mod, row0=0, rows=None, g_pre_next=None):
    n, d = x1.shape
    tm = FIN_TILE
    rows = n if rows is None else rows
    tile0 = row0 // tm
    row = lambda i, *_: (tile0 + i, 0)
    out_row = lambda i, *_: (i, 0)
    mod_spec = lambda lay: pl.BlockSpec((None, None, 6, d), lambda i, t, *_: (lay, t[tile0 + i], 0, 0))
    gain_spec = lambda lay: pl.BlockSpec((None, 1, d), lambda i, *_: (lay, 0, 0))
    prenorm_next = g_pre_next is not None
    in_specs = [pl.BlockSpec(memory_space=pl.ANY),
                pl.BlockSpec((tm, TOP_K), row),
                pl.BlockSpec((tm, TOP_K), row),
                pl.BlockSpec((tm, d), row),
                mod_spec(layer), gain_spec(layer)]
    args = [ye, lpos_t, wts_t, x1, mod, g_post]
    out_specs = [pl.BlockSpec((tm, d), out_row)]
    out_shape = [jax.ShapeDtypeStruct((rows, d), F32)]
    if prenorm_next:
        in_specs += [mod_spec(layer + 1), gain_spec(layer + 1)]
        args += [mod, g_pre_next]
        out_specs += [pl.BlockSpec((tm, d), out_row)]
        out_shape += [jax.ShapeDtypeStruct((rows, d), BF16)]
    outs = pl.pallas_call(
        functools.partial(_gather_kernel, tm=tm, tile0=tile0, prenorm_next=prenorm_next),
        grid_spec=pltpu.PrefetchScalarGridSpec(
            num_scalar_prefetch=5, grid=(rows // tm,),
            in_specs=in_specs, out_specs=out_specs,
            scratch_shapes=[pltpu.VMEM((2, SORT_ROWS, d), F32), pltpu.SemaphoreType.DMA((2,))]),
        out_shape=out_shape,
        compiler_params=_params(1),
        name="moe_gather",
    )(tile_mod, plan["seg"], plan["off"], plan["pos"], plan["used"], *args)
    return outs if prenorm_next else outs[0]


def _moe(h2, wts, lpos, seg, off, x1, mod, g_post, w1, b1, w2, b2, layer, tile_mod, split_rows=None,
         g_pre_next=None):
    n, d = h2.shape
    blk = MOE_BLOCK
    tm = FIN_TILE
    tiles = n // tm
    n_rows = -(-(n * TOP_K + tiles * N_EXPERTS * (SEG_ALIGN - 1) + N_EXPERTS * (blk - 1)) // blk) * blk
    n_blocks = n_rows // blk
    seg = seg[:, :, 0].astype(I32)
    off = off[:, :, 0].astype(I32)
    rows_e = jnp.sum(seg, axis=0)
    region = (rows_e + blk - 1) // blk * blk
    pend = jnp.cumsum(region)
    pstart = pend - region
    pos = pstart[None, :] + jnp.cumsum(seg, axis=0) - seg
    plan = {"seg": seg.reshape(-1), "off": off.reshape(-1), "pos": pos.reshape(-1).astype(I32),
            "used": jnp.sum(seg, axis=1).astype(I32),
            "tail_pos": (pstart + rows_e).astype(I32), "tail_len": (region - rows_e).astype(I32)}
    blocks = jnp.arange(n_blocks, dtype=I32) * blk
    block_expert = jnp.minimum(jnp.sum(blocks[:, None] >= pend[None, :], axis=1), N_EXPERTS - 1).astype(I32)
    n_used = (pend[-1:] // blk).astype(I32)
    valid_rows = jnp.clip((pstart + rows_e)[block_expert] - blocks, 0, blk).astype(I32)
    ids = jnp.arange(n_blocks, dtype=I32)
    run_start = jnp.logical_and(jnp.concatenate([jnp.ones((1,), bool), block_expert[1:] != block_expert[:-1]]),
                                ids < n_used[0])
    first_after = lax.cummin(jnp.where(run_start, ids, n_blocks)[::-1])[::-1]
    first_after = jnp.concatenate([first_after[1:], jnp.full((1,), n_blocks, I32)])
    next_expert = jnp.where(first_after < n_blocks, block_expert[jnp.minimum(first_after, n_blocks - 1)], -1)
    xe = _scatter_rows(h2, lpos, plan, n_rows)
    ye = _experts(xe, block_expert, next_expert.astype(I32), valid_rows, n_used, w1, b1, w2, b2, layer)
    combine = functools.partial(_gather_combine, ye, lpos.T, wts.T, plan, x1, mod, g_post, layer, tile_mod)
    if split_rows is None:
        return combine(g_pre_next=g_pre_next)
    return combine(row0=0, rows=split_rows), combine(row0=split_rows, rows=n - split_rows)


def _tile_mod_ids(n_ctx_rows, n_lat_rows, lat_seq, tm):
    ctx = np.zeros((n_ctx_rows // tm,), np.int32)
    lat = 1 + (np.arange(n_lat_rows // tm) * tm) // lat_seq
    return jnp.asarray(np.concatenate([ctx, lat.astype(np.int32)]))


def kernel(x_prompt, x_sample, cache_k, cache_v, state_ret_fwd, state_ret_bwd, c, c_ctx, w_mod, b_mod, g_pre_mix, g_post_mix, g_pre_ffn, g_post_ffn, w_in, na_rel_bias, ret_decay_fwd, ret_decay_bwd, w_branch, w_out, w_router, b_router, w_exp_in, b_exp_in, w_exp_out, b_exp_out):
    batch, seq, d = x_prompt.shape
    dec_batch, dec_seq, _ = x_sample.shape
    depth = w_in.shape[0]
    n_ctx = batch * seq
    n_lat = dec_batch * dec_seq
    assert 1 + dec_batch <= MOD_ROWS

    x = (x_prompt.reshape(n_ctx, d), x_sample.reshape(n_lat, d))
    cvec =jnp.concatenate([c_ctx[None], c, jnp.zeros((MOD_ROWS - 1 - dec_batch, d), F32)], axis=0)
    mod_all = _modulation(cvec, w_mod, b_mod).reshape(depth, MOD_ROWS, 6, d)
    tmod = {tm: _tile_mod_ids(n_ctx, n_lat, dec_seq, tm) for tm in (ROW_TILE, FIN_TILE)}
    past = cache_k.shape[2]
    ck = cache_k.reshape(dec_batch, depth, past, NA_WIDTH)
    cv = cache_v.reshape(dec_batch, depth, past, NA_WIDTH)
    lg_f = jax.nn.log_sigmoid(ret_decay_fwd.astype(F32)).reshape(-1)
    lg_b = jax.nn.log_sigmoid(ret_decay_bwd.astype(F32)).reshape(-1)
    bias_all = _neighbourhood_bias(na_rel_bias, dec_seq)
    vec = lambda g: g.reshape(depth, 1, d)
    g_pre_mix, g_post_mix, g_pre_ffn, g_post_ffn = vec(g_pre_mix), vec(g_post_mix), vec(g_pre_ffn), vec(g_post_ffn)
    w_branch_b = w_branch.astype(BF16)
    w_out_b = w_out.astype(BF16)
    w_router_t = jnp.swapaxes(w_router, 1, 2)
    b_router_l = jnp.broadcast_to(b_router[:, :, None], (depth, N_EXPERTS, LANES))
    b_exp_in = b_exp_in.reshape(depth, N_EXPERTS, 1, 2 * D_FF)
    b_exp_out = b_exp_out.reshape(depth, N_EXPERTS, 1, d)

    ks, vs, sfs, sbs = [], [], [], []
    h = _prenorm(x, g_pre_mix, mod_all, 0, tmod[ROW_TILE], ROW_TILE)
    for l in range(depth):
        last = l == depth - 1
        z_qkvu = _project(h, w_in, l, 0, QKVU_W, F32)
        z_ret = _project(h, w_in, l, QKVU_W, RET_W, BF16)
        z_gate = _project(h, w_in, l, QKVU_W + RET_W, GATE_W, BF16)
        ks.append(z_qkvu[:n_ctx, NA_WIDTH:2 * NA_WIDTH].reshape(batch, seq, NA_HEADS, NA_HEAD_DIM))
        vs.append(z_qkvu[:n_ctx, 2 * NA_WIDTH:3 * NA_WIDTH].reshape(batch, seq, NA_HEADS, NA_HEAD_DIM))

        a_pair = (_attention_ctx(z_qkvu, batch, seq),
                  _attention_lat(z_qkvu, ck, cv, bias_all, l, n_ctx, dec_batch, dec_seq))
        f_pair = (_fourier(z_qkvu, 0, batch, seq), _fourier(z_qkvu, n_ctx, dec_batch, dec_seq))
        r_ctx, s_f, s_b = _retention(z_ret, lg_f, lg_b, l, 0, batch, seq, rotary=False, state_out=True)
        (r_lat,) = _retention(z_ret, lg_f, lg_b, l, n_ctx, dec_batch, dec_seq, rotary=True,
                              states=(state_ret_fwd, state_ret_bwd))
        sfs.append(s_f)
        sbs.append(s_b)

        x1, h2, wts, lpos, seg, off = _finish(
            (a_pair, f_pair, (r_ctx, r_lat)), z_gate, x, mod_all, g_post_mix, g_pre_ffn,
            w_branch_b, w_out_b, w_router_t, b_router_l, l, tmod[FIN_TILE])
        out = _moe(h2, wts, lpos, seg, off, x1, mod_all, g_post_ffn,
                   w_exp_in, b_exp_in, w_exp_out, b_exp_out, l, tmod[FIN_TILE],
                   split_rows=n_ctx if last else None, g_pre_next=None if last else g_pre_mix)
        x, h = (out, None) if last else out

    y_prompt = x[0].reshape(batch, seq, d)
    y_sample = x[1].reshape(dec_batch, dec_seq, d)
    return (y_prompt, y_sample, jnp.stack(ks, axis=1), jnp.stack(vs, axis=1),
            jnp.stack(sfs, axis=1), jnp.stack(sbs, axis=1))
```

```python
import functools

import numpy as np
import jax
import jax.numpy as jnp
from jax import lax
from jax.experimental import pallas as pl
from jax.experimental.pallas import tpu as pltpu

F32 = jnp.float32
BF16 = jnp.bfloat16
I32 = jnp.int32

D_MODEL = 1024
GRID_W = 64
NA_HEADS = 8
NA_HEAD_DIM = 64
NA_WIDTH = NA_HEADS * NA_HEAD_DIM
WIN_H = 8
WIN_W = 16
KEY_SLAB_ROWS = 12
FOURIER_GROUPS = 4
FOURIER_GROUP_DIM = 128
RET_HEADS = 4
RET_KEY_DIM = 128
ROPE_BASE = 10000.0
BRANCH_WIDTH = 512
N_EXPERTS = 32
TOP_K = 4
D_FF = 1024
SWIGLU_LIMIT = 7.0
SWIGLU_ALPHA = 1.702
EPS = 1e-6
NEG_INF = -1e30

QKVU_W = 4 * NA_WIDTH
RET_W = 4 * BRANCH_WIDTH
GATE_W = 3 * D_MODEL
PROJ_TILE = 1024
PROJ_ROWS = 2048
ATTN_Q_TILE = 512
ATTN_PAIRS_PER_STEP = 2
RET_Q_TILE = 1024
RET_DECAY_BYTES = 4 * 1024 * 1024

LANES = 128
MOD_ROWS = 16
ROW_TILE = 1024
FIN_TILE = 512
MOE_BLOCK = 512
EXPERT_ROW_STEP = 128
SEG_ALIGN = 8
SORT_CHUNK = 256
SORT_ROWS = -(-(FIN_TILE * TOP_K + N_EXPERTS * (SEG_ALIGN - 1)) // SORT_CHUNK) * SORT_CHUNK
VMEM_LIMIT = 56 * 1024 * 1024


def _params(n_axes, vmem=VMEM_LIMIT):
    return pltpu.CompilerParams(dimension_semantics=("arbitrary",) * n_axes, vmem_limit_bytes=vmem)


def _rms(x):
    return lax.rsqrt(jnp.mean(x * x, axis=-1, keepdims=True) + EPS)


def _mod_kernel(cv_ref, w_ref, b_ref, o_ref):
    cv = cv_ref[...]
    s = (cv * jax.nn.sigmoid(cv)).astype(BF16)
    o_ref[...] = jnp.dot(s, w_ref[...].astype(BF16), preferred_element_type=F32) + b_ref[...]


def _modulation(cv, w_mod, b_mod):
    depth, d, n = w_mod.shape
    tn = 1536
    return pl.pallas_call(
        _mod_kernel,
        grid=(depth, n // tn),
        in_specs=[pl.BlockSpec((MOD_ROWS, d), lambda l, j: (0, 0)),
                  pl.BlockSpec((None, d, tn), lambda l, j: (l, 0, j)),
                  pl.BlockSpec((None, 1, tn), lambda l, j: (l, 0, j))],
        out_specs=pl.BlockSpec((None, MOD_ROWS, tn), lambda l, j: (l, 0, j)),
        out_shape=jax.ShapeDtypeStruct((depth, MOD_ROWS, n), F32),
        compiler_params=_params(2),
        name="modulation",
    )(cv, w_mod, b_mod.reshape(depth, 1, n))


def _token_rows(x, tm):
    if isinstance(x, tuple):
        d = x[0].shape[1]
        ct = x[0].shape[0] // tm
        specs = [pl.BlockSpec((tm, d), lambda i, *_: (jnp.minimum(i, ct - 1), 0)),
                 pl.BlockSpec((tm, d), lambda i, *_: (jnp.maximum(i - ct, 0), 0))]
        return specs, list(x), ct, x[0].shape[0] + x[1].shape[0]
    return [pl.BlockSpec((tm, x.shape[1]), lambda i, *_: (i, 0))], [x], None, x.shape[0]


def _load_rows(refs, ctx_tiles):
    if len(refs) == 1:
        return refs[0][...]
    return jnp.where(pl.program_id(0) < ctx_tiles, refs[0][...], refs[1][...])


def _prenorm_kernel(tmod_ref, *refs, ctx_tiles):
    del tmod_ref
    g_ref, mod_ref, o_ref = refs[-3:]
    x = _load_rows(refs[:-3], ctx_tiles)
    h = x * _rms(x) * g_ref[...]
    o_ref[...] = (h * (1.0 + mod_ref[1:2, :]) + mod_ref[0:1, :]).astype(o_ref.dtype)


def _prenorm(x, g, mod, layer, tile_mod, tm):
    x_specs, x_args, ctx_tiles, n = _token_rows(x, tm)
    d = x_args[0].shape[1]
    return pl.pallas_call(
        functools.partial(_prenorm_kernel, ctx_tiles=ctx_tiles),
        grid_spec=pltpu.PrefetchScalarGridSpec(
            num_scalar_prefetch=1, grid=(n // tm,),
            in_specs=x_specs + [pl.BlockSpec((None, 1, d), lambda i, t: (layer, 0, 0)),
                                pl.BlockSpec((None, None, 6, d), lambda i, t: (layer, t[i], 0, 0))],
            out_specs=pl.BlockSpec((tm, d), lambda i, t: (i, 0))),
        out_shape=jax.ShapeDtypeStruct((n, d), BF16),
        compiler_params=_params(1),
        name="prenorm",
    )(tile_mod, *x_args, g, mod)


def _proj_kernel(h_ref, w_ref, o_ref, wb_ref):
    @pl.when(pl.program_id(1) == 0)
    def _():
        wb_ref[...] = w_ref[...].astype(BF16)

    o_ref[...] = jnp.dot(h_ref[...], wb_ref[...], preferred_element_type=F32).astype(o_ref.dtype)


def _project(h, w, layer, col0, width, out_dtype):
    n, d = h.shape
    tm = PROJ_ROWS
    tn = PROJ_TILE
    cb = col0 // tn
    return pl.pallas_call(
        _proj_kernel,
        grid=(width // tn, n // tm),
        in_specs=[pl.BlockSpec((tm, d), lambda j, i: (i, 0)),
                  pl.BlockSpec((None, d, tn), lambda j, i: (layer, 0, cb + j))],
        out_specs=pl.BlockSpec((tm, tn), lambda j, i: (i, j)),
        out_shape=jax.ShapeDtypeStruct((n, width), out_dtype),
        scratch_shapes=[pltpu.VMEM((d, tn), BF16)],
        compiler_params=_params(2),
        name="in_proj",
    )(h, w)


def _head_pair_masks():
    lane = lax.broadcasted_iota(I32, (1, LANES), 1)
    first = lane < NA_HEAD_DIM
    return first, jnp.logical_not(first)


def _attn_ctx_kernel(q_ref, k_ref, v_ref, o_ref):
    masks = _head_pair_masks()
    scale = NA_HEAD_DIM ** -0.5
    for p in range(NA_WIDTH // LANES):
        cols = slice(LANES * p, LANES * (p + 1))
        q2 = q_ref[:, cols] * scale
        k2 = k_ref[:, cols].astype(BF16)
        v2 = v_ref[:, cols].astype(BF16)
        outs = []
        for m in masks:
            qa = jnp.where(m, q2, 0.0).astype(BF16)
            s = lax.dot_general(qa, k2, (((1,), (1,)), ((), ())), preferred_element_type=F32)
            e = jnp.exp(s - jnp.max(s, axis=-1, keepdims=True))
            den = jnp.sum(e, axis=-1, keepdims=True)
            outs.append(jnp.dot(e.astype(BF16), v2, preferred_element_type=F32) / den)
        o_ref[:, cols] = jnp.where(masks[0], outs[0], outs[1]).astype(o_ref.dtype)


def _attention_ctx(z_qkv, n_seq, seq):
    return pl.pallas_call(
        _attn_ctx_kernel,
        grid=(n_seq,),
        in_specs=[pl.BlockSpec((seq, NA_WIDTH), lambda b: (b, 0)),
                  pl.BlockSpec((seq, NA_WIDTH), lambda b: (b, 1)),
                  pl.BlockSpec((seq, NA_WIDTH), lambda b: (b, 2))],
        out_specs=pl.BlockSpec((seq, NA_WIDTH), lambda b: (b, 0)),
        out_shape=jax.ShapeDtypeStruct((n_seq * seq, NA_WIDTH), BF16),
        compiler_params=_params(1),
        name="attn_ctx",
    )(z_qkv, z_qkv, z_qkv)


def _attn_lat_kernel(q_ref, k_ref, v_ref, kc_ref, vc_ref, bias_ref, o_ref, kb_ref, vb_ref, *, tq):
    masks = _head_pair_masks()
    scale = NA_HEAD_DIM ** -0.5
    seq = q_ref.shape[0]
    slab = bias_ref.shape[3]
    per_half = seq // 2 // tq
    kb_ref[...] = k_ref[...].astype(BF16)
    vb_ref[...] = v_ref[...].astype(BF16)
    kc = kc_ref[...].astype(BF16)
    vc = vc_ref[...].astype(BF16)
    nt = (((1,), (1,)), ((), ()))

    def q_tile(qi, carry):
        rows = pl.ds(pl.multiple_of(qi * tq, tq), tq)
        half = qi // per_half
        half_rows = pl.ds(pl.multiple_of((qi % per_half) * tq, tq), tq)
        keys = pl.ds(pl.multiple_of(half * (seq - slab), seq - slab), slab)
        for pp in range(q_ref.shape[1] // LANES):
            cols = slice(LANES * pp, LANES * (pp + 1))
            k2 = kb_ref[keys, cols]
            v2 = vb_ref[keys, cols]
            q2 = q_ref[rows, cols] * scale
            outs = []
            for hh, m in enumerate(masks):
                qa = jnp.where(m, q2, 0.0).astype(BF16)
                s_lat = (lax.dot_general(qa, k2, nt, preferred_element_type=F32)
                         + bias_ref[2 * pp + hh, half, half_rows, :])
                s_ctx = lax.dot_general(qa, kc[:, cols], nt, preferred_element_type=F32)
                mx = jnp.maximum(jnp.max(s_lat, axis=-1, keepdims=True), jnp.max(s_ctx, axis=-1, keepdims=True))
                e_lat = jnp.exp(s_lat - mx)
                e_ctx = jnp.exp(s_ctx - mx)
                den = jnp.sum(e_lat, axis=-1, keepdims=True) + jnp.sum(e_ctx, axis=-1, keepdims=True)
                o = (jnp.dot(e_lat.astype(BF16), v2, preferred_element_type=F32)
                     + jnp.dot(e_ctx.astype(BF16), vc[:, cols], preferred_element_type=F32))
                outs.append(o / den)
            o_ref[rows, cols] = jnp.where(masks[0], outs[0], outs[1]).astype(o_ref.dtype)
        return carry

    lax.fori_loop(0, q_ref.shape[0] // tq, q_tile, 0, unroll=True)


def _attention_lat(z_qkv, cache_k, cache_v, bias, layer, row0, n_seq, seq):
    past = cache_k.shape[2]
    width = ATTN_PAIRS_PER_STEP * LANES
    steps = NA_WIDTH // width
    rb = row0 // seq
    slab = bias.shape[-1]
    return pl.pallas_call(
        functools.partial(_attn_lat_kernel, tq=min(seq // 2, ATTN_Q_TILE)),
        grid=(steps, n_seq),
        in_specs=[pl.BlockSpec((seq, width), lambda p, b: (rb + b, p)),
                  pl.BlockSpec((seq, width), lambda p, b: (rb + b, steps + p)),
                  pl.BlockSpec((seq, width), lambda p, b: (rb + b, 2 * steps + p)),
                  pl.BlockSpec((None, None, past, width), lambda p, b: (b, layer, 0, p)),
                  pl.BlockSpec((None, None, past, width), lambda p, b: (b, layer, 0, p)),
                  pl.BlockSpec((None, 2 * ATTN_PAIRS_PER_STEP, 2, seq // 2, slab),
                               lambda p, b: (layer, p, 0, 0, 0))],
        out_specs=pl.BlockSpec((seq, width), lambda p, b: (b, p)),
        out_shape=jax.ShapeDtypeStruct((n_seq * seq, NA_WIDTH), BF16),
        scratch_shapes=[pltpu.VMEM((seq, width), BF16), pltpu.VMEM((seq, width), BF16)],
        compiler_params=_params(2),
        name="attn_lat",
    )(z_qkv, z_qkv, z_qkv, cache_k, cache_v, bias)


def _neighbourhood_bias(rpb, seq):
    rows = seq // GRID_W
    kh = WIN_H
    assert rows >= WIN_H
    lead = rpb.shape[:-2]
    c = np.arange(GRID_W)
    q_cs = np.clip(c - WIN_W // 2, 0, GRID_W - WIN_W)
    col_ok = (c[None, :] >= q_cs[:, None]) & (c[None, :] < q_cs[:, None] + WIN_W)
    r = np.arange(rows)
    rs = np.clip(r - kh // 2, 0, rows - kh)
    base = np.where(r < rows // 2, 0, rows - KEY_SLAB_ROWS)
    assert (rs >= base).all() and (rs + kh <= base + KEY_SLAB_ROWS).all()
    pick_c = (c[None, None, :] - c[None, :, None] + WIN_W - 1
              == np.arange(2 * WIN_W - 1)[:, None, None]).astype(np.float32)
    w = jnp.einsum("...ij,jqk->...qik", rpb, pick_c, precision=lax.Precision.HIGHEST)
    w = jnp.where(jnp.asarray(col_ok)[:, None, :], w, NEG_INF)
    blocks = []
    for rq in range(rows):
        lo = int(rs[rq]) - rq + WIN_H - 1
        slab = w[..., lo:lo + kh, :].reshape(lead + (GRID_W, kh * GRID_W))
        left = int(rs[rq] - base[rq])
        pad = ((0, 0),) * (len(lead) + 1) + ((left * GRID_W, (KEY_SLAB_ROWS - kh - left) * GRID_W),)
        blocks.append(jnp.pad(slab, pad, constant_values=NEG_INF))
    return jnp.stack(blocks, axis=-3).reshape(lead + (2, seq // 2, KEY_SLAB_ROWS * GRID_W))


def _fourier_kernel(u_ref, ct2_ref, cc_ref, sc_ref, o_ref, pq_ref):
    t = u_ref.shape[0]
    for g in range(FOURIER_GROUPS):
        cols = slice(FOURIER_GROUP_DIM * g, FOURIER_GROUP_DIM * (g + 1))
        ug = u_ref[:, cols].astype(BF16)
        pq_ref[0:t, cols] = jnp.dot(ug, cc_ref[...], preferred_element_type=F32).astype(BF16)
        pq_ref[t:2 * t, cols] = jnp.dot(ug, sc_ref[...], preferred_element_type=F32).astype(BF16)
    o_ref[...] = jnp.dot(ct2_ref[...], pq_ref[...], preferred_element_type=F32).astype(o_ref.dtype)


def _dft_tables(t):
    def cs(n):
        k = np.arange(n, dtype=np.int64)
        ang = 2.0 * np.pi * ((k[:, None] * k[None, :]) % n).astype(np.float64) / n
        return np.cos(ang) / np.sqrt(n), np.sin(ang) / np.sqrt(n)

    ct, st = cs(t)
    cc, sc = cs(FOURIER_GROUP_DIM)
    ct2 = np.concatenate([ct, -st], axis=1).astype(np.float32)
    return (jnp.asarray(ct2).astype(BF16), jnp.asarray(cc.astype(np.float32)).astype(BF16),
            jnp.asarray(sc.astype(np.float32)).astype(BF16))


def _fourier(z_qkvu, row0, n_seq, seq):
    ct2, cc, sc = _dft_tables(seq)
    width = FOURIER_GROUPS * FOURIER_GROUP_DIM
    rb = row0 // seq
    ucol = 3 * NA_WIDTH // width
    return pl.pallas_call(
        _fourier_kernel,
        grid=(n_seq,),
        in_specs=[pl.BlockSpec((seq, width), lambda b: (rb + b, ucol)),
                  pl.BlockSpec((seq, 2 * seq), lambda b: (0, 0)),
                  pl.BlockSpec((FOURIER_GROUP_DIM, FOURIER_GROUP_DIM), lambda b: (0, 0)),
                  pl.BlockSpec((FOURIER_GROUP_DIM, FOURIER_GROUP_DIM), lambda b: (0, 0))],
        out_specs=pl.BlockSpec((seq, width), lambda b: (b, 0)),
        out_shape=jax.ShapeDtypeStruct((n_seq * seq, width), BF16),
        scratch_shapes=[pltpu.VMEM((2 * seq, width), BF16)],
        compiler_params=_params(1),
        name="fourier",
    )(z_qkvu, ct2, cc, sc)


def _rotary_tables(t):
    pos = np.arange(t)
    row = (pos // GRID_W).astype(np.float64)
    col = (pos % GRID_W).astype(np.float64)
    nf = RET_KEY_DIM // 4
    inv_freq = ROPE_BASE ** (-np.arange(nf, dtype=np.float64) / nf)
    ar = row[:, None] * inv_freq[None]
    ac = col[:, None] * inv_freq[None]
    cos = np.concatenate([np.cos(ar), np.cos(ar), np.cos(ac), np.cos(ac)], axis=1)
    sin = np.concatenate([-np.sin(ar), np.sin(ar), -np.sin(ac), np.sin(ac)], axis=1)
    return jnp.asarray(cos.astype(np.float32)), jnp.asarray(sin.astype(np.float32))


def _ret_kernel(lgf_ref, lgb_ref, *refs, t, tq, layer, rotary, state_in, state_out):
    refs = list(refs)
    q_ref, k_ref, v_ref, g_ref = refs[:4]
    refs = refs[4:]
    if rotary:
        cos_ref, sin_ref = refs[:2]
        refs = refs[2:]
    if state_in:
        sf0_ref, sb0_ref = refs[:2]
        refs = refs[2:]
    o_ref = refs[0]
    refs = refs[1:]
    if state_out:
        sf_ref, sb_ref = refs[:2]
        refs = refs[2:]
    dec_ref, kb_ref = refs

    scale = RET_KEY_DIM ** -0.5
    nq = t // tq
    heads_here = dec_ref.shape[0]

    if rotary:
        lane = lax.broadcasted_iota(I32, (1, LANES), 1)
        low = (lane % (RET_KEY_DIM // 2)) < (RET_KEY_DIM // 4)

        def rot(x, rows):
            swapped = jnp.where(low, pltpu.roll(x, LANES - RET_KEY_DIM // 4, 1), pltpu.roll(x, RET_KEY_DIM // 4, 1))
            return x * cos_ref[rows, :] + swapped * sin_ref[rows, :]
    else:
        def rot(x, rows):
            return x

    def one_head(hh):
        h = pl.program_id(0) * heads_here + hh
        cols = slice(LANES * hh, LANES * (hh + 1))
        lgf = lgf_ref[layer * RET_HEADS + h]
        lgb = lgb_ref[layer * RET_HEADS + h]

        @pl.when(pl.program_id(1) == 0)
        def _():
            def fill(ri, c):
                rows = pl.ds(pl.multiple_of(ri * tq, tq), tq)
                i = lax.broadcasted_iota(I32, (tq, t), 0) + ri * tq
                j = lax.broadcasted_iota(I32, (tq, t), 1)
                d = (i - j).astype(F32)
                m = jnp.exp(jnp.abs(d) * jnp.where(d > 0, lgf, lgb))
                dec_ref[hh, rows, :] = jnp.where(d == 0, 2.0, m)
                return c

            lax.fori_loop(0, nq, fill, 0)

        kr = rot(k_ref[:, cols].astype(F32), slice(0, t))
        kb_ref[hh] = kr.astype(BF16)
        vb = v_ref[:, cols]

        if state_out:
            j = lax.broadcasted_iota(I32, (t, 1), 0).astype(F32)
            tn = (((0,), (0,)), ((), ()))
            kf = (kr * (scale * jnp.exp(lgf * (t - 1.0 - j)))).astype(BF16)
            kbw = (kr * (scale * jnp.exp(lgb * j))).astype(BF16)
            sf = lax.dot_general(kf, vb, tn, preferred_element_type=F32)
            sb = lax.dot_general(kbw, vb, tn, preferred_element_type=F32)
            if state_in:
                sf = sf + jnp.exp(lgf * t) * sf0_ref[hh]
                sb = sb + jnp.exp(lgb * t) * sb0_ref[hh]
            sf_ref[hh] = sf
            sb_ref[hh] = sb

        def q_tile(qi, carry):
            r0 = pl.multiple_of(qi * tq, tq)
            rows = pl.ds(r0, tq)
            qr = rot(q_ref[rows, cols].astype(F32), rows)
            s = lax.dot_general((qr * scale).astype(BF16), kb_ref[hh], (((1,), (1,)), ((), ())),
                                preferred_element_type=F32)
            y = jnp.dot((s * dec_ref[hh, rows, :]).astype(BF16), vb, preferred_element_type=F32)
            if state_in:
                pos = (lax.broadcasted_iota(I32, (tq, 1), 0) + r0).astype(F32)
                qf = (qr * jnp.exp(lgf * (pos + 1.0))).astype(BF16)
                qb = (qr * jnp.exp(lgb * (t - pos))).astype(BF16)
                y = (y + jnp.dot(qf, sf0_ref[hh].astype(BF16), preferred_element_type=F32)
                     + jnp.dot(qb, sb0_ref[hh].astype(BF16), preferred_element_type=F32))
            mean = jnp.mean(y, axis=-1, keepdims=True)
            yc = y - mean
            yn = yc * lax.rsqrt(jnp.mean(yc * yc, axis=-1, keepdims=True) + EPS)
            g = g_ref[rows, cols].astype(F32)
            o_ref[rows, cols] = (g * jax.nn.sigmoid(g) * yn).astype(o_ref.dtype)
            return carry

        lax.fori_loop(0, nq, q_tile, 0)

    for hh in range(heads_here):
        one_head(hh)


def _retention(z_ret, lg_f, lg_b, layer, row0, n_seq, seq, *, rotary, states=None, state_out=False):
    rb = row0 // seq
    tq = min(seq, RET_Q_TILE)
    state_in = states is not None
    hps = RET_HEADS if seq * seq * RET_HEADS * 4 <= RET_DECAY_BYTES else 1
    cb = BRANCH_WIDTH // (hps * LANES)
    width = hps * LANES
    in_specs = [pl.BlockSpec((seq, width), lambda h, b, *_: (rb + b, 0 * cb + h)),
                pl.BlockSpec((seq, width), lambda h, b, *_: (rb + b, 1 * cb + h)),
                pl.BlockSpec((seq, width), lambda h, b, *_: (rb + b, 2 * cb + h)),
                pl.BlockSpec((seq, width), lambda h, b, *_: (rb + b, 3 * cb + h))]
    args = [z_ret, z_ret, z_ret, z_ret]
    if rotary:
        cos, sin = _rotary_tables(seq)
        in_specs += [pl.BlockSpec((seq, LANES), lambda h, b, *_: (0, 0))] * 2
        args += [cos, sin]
    if state_in:
        st_spec = pl.BlockSpec((None, None, hps, RET_KEY_DIM, RET_KEY_DIM), lambda h, b, *_: (b, layer, h, 0, 0))
        in_specs += [st_spec, st_spec]
        args += list(states)
    out_specs = [pl.BlockSpec((seq, width), lambda h, b, *_: (b, h))]
    out_shape = [jax.ShapeDtypeStruct((n_seq * seq, RET_HEADS * LANES), BF16)]
    if state_out:
        so = pl.BlockSpec((None, hps, RET_KEY_DIM, RET_KEY_DIM), lambda h, b, *_: (b, h, 0, 0))
        out_specs += [so, so]
        out_shape += [jax.ShapeDtypeStruct((n_seq, RET_HEADS, RET_KEY_DIM, RET_KEY_DIM), F32)] * 2
    return pl.pallas_call(
        functools.partial(_ret_kernel, t=seq, tq=tq, layer=layer, rotary=rotary, state_in=state_in,
                          state_out=state_out),
        grid_spec=pltpu.PrefetchScalarGridSpec(
            num_scalar_prefetch=2, grid=(RET_HEADS // hps, n_seq),
            in_specs=in_specs, out_specs=out_specs,
            scratch_shapes=[pltpu.VMEM((hps, seq, seq), F32), pltpu.VMEM((hps, seq, LANES), BF16)]),
        out_shape=out_shape,
        compiler_params=_params(2),
        name="retention",
    )(lg_f, lg_b, *args)


def _split_dot_nt(w, x):
    nt = (((1,), (1,)), ((), ()))
    w_hi = w.astype(BF16)
    w_lo = (w - w_hi.astype(F32)).astype(BF16)
    x_hi = x.astype(BF16)
    x_lo = (x - x_hi.astype(F32)).astype(BF16)
    return (lax.dot_general(w_hi, x_hi, nt, preferred_element_type=F32)
            + lax.dot_general(w_hi, x_lo, nt, preferred_element_type=F32)
            + lax.dot_general(w_lo, x_hi, nt, preferred_element_type=F32))


def _finish_kernel(tmod_ref, *refs, ctx_tiles, n_x):
    del tmod_ref
    branch_refs, refs = refs[:6], refs[6:]
    zg_ref, refs = refs[0], refs[1:]
    x_refs, refs = refs[:n_x], refs[n_x:]
    (mod_ref, gpost_ref, gpre_ref, wb_ref, wo_ref, wrt_ref, br_ref, tri_ref, ltri_ref,
     x1_ref, h2_ref, wts_ref, lpos_ref, seg_ref, off_ref) = refs
    d = D_MODEL

    def branch(j):
        return _load_rows(branch_refs[2 * j:2 * j + 2], ctx_tiles)

    def gate(j):
        return jax.nn.sigmoid(zg_ref[:, d * j:d * (j + 1)].astype(F32))

    merged = (gate(0) * jnp.dot(branch(0), wb_ref[0], preferred_element_type=F32)
              + gate(1) * jnp.dot(branch(1), wb_ref[1], preferred_element_type=F32)
              + gate(2) * jnp.dot(branch(2), wb_ref[2], preferred_element_type=F32))
    y = jnp.dot(merged.astype(BF16), wo_ref[...], preferred_element_type=F32)
    x1 = _load_rows(x_refs, ctx_tiles) + mod_ref[2:3, :] * (y * _rms(y) * gpost_ref[...])
    x1_ref[...] = x1
    h2 = x1 * _rms(x1) * gpre_ref[...] * (1.0 + mod_ref[4:5, :]) + mod_ref[3:4, :]
    h2_ref[...] = h2.astype(h2_ref.dtype)

    logits = _split_dot_nt(wrt_ref[...], h2) + br_ref[:, 0:1]
    tm = logits.shape[1]
    eidx = lax.broadcasted_iota(I32, (N_EXPERTS, tm), 0)
    cur = logits
    vals, hots = [], []
    for k in range(TOP_K):
        m = jnp.max(cur, axis=0, keepdims=True)
        sel = jnp.min(jnp.where(cur == m, eidx, N_EXPERTS), axis=0, keepdims=True)
        hot = eidx == sel
        vals.append(m)
        hots.append(hot)
        cur = jnp.where(hot, -jnp.inf, cur)
    exps = [jnp.exp(v - vals[0]) for v in vals]
    den = exps[0] + exps[1] + exps[2] + exps[3]
    for k in range(TOP_K):
        wts_ref[k:k + 1, :] = exps[k] / den

    member = jnp.logical_or(jnp.logical_or(hots[0], hots[1]), jnp.logical_or(hots[2], hots[3]))
    member_f = member.astype(F32)
    before = jnp.dot(member_f.astype(BF16), tri_ref[...], preferred_element_type=F32)
    units = jnp.ceil(jnp.sum(member_f, axis=1, keepdims=True) * (1.0 / SEG_ALIGN))
    units = jnp.broadcast_to(units, seg_ref.shape)
    off = jnp.dot(ltri_ref[...], units.astype(BF16), preferred_element_type=F32) * SEG_ALIGN
    seg_ref[...] = units * SEG_ALIGN
    off_ref[...] = off
    place = before + off[:, 0:1]
    for k in range(TOP_K):
        lpos_ref[k:k + 1, :] = jnp.sum(jnp.where(hots[k], place, 0.0), axis=0, keepdims=True).astype(I32)


def _finish(branches, z_gate, x, mod, g_post, g_pre_ffn, w_branch, w_out, w_router_t, b_router, layer, tile_mod):
    tm = FIN_TILE
    x_specs, x_args, _, n = _token_rows(x, tm)
    d = x_args[0].shape[1]
    tri = jnp.asarray(np.triu(np.ones((tm, tm), np.float32), k=1)).astype(BF16)
    ltri = jnp.asarray(np.tril(np.ones((N_EXPERTS, N_EXPERTS), np.float32), k=-1)).astype(BF16)
    row = lambda i, t: (i, 0)
    const2 = lambda i, t: (0, 0)
    lay3 = lambda i, t: (layer, 0, 0)
    col = lambda i, t: (0, i)
    branch_specs, branch_args = [], []
    for pair in branches:
        specs, args, ctx_tiles, _ = _token_rows(pair, tm)
        branch_specs += specs
        branch_args += args
    outs = pl.pallas_call(
        functools.partial(_finish_kernel, ctx_tiles=ctx_tiles, n_x=len(x_args)),
        grid_spec=pltpu.PrefetchScalarGridSpec(
            num_scalar_prefetch=1, grid=(n // tm,),
            in_specs=branch_specs + [pl.BlockSpec((tm, GATE_W), row)] + x_specs + [
                      pl.BlockSpec((None, None, 6, d), lambda i, t: (layer, t[i], 0, 0)),
                      pl.BlockSpec((None, 1, d), lay3),
                      pl.BlockSpec((None, 1, d), lay3),
                      pl.BlockSpec((None, 3, BRANCH_WIDTH, d), lambda i, t: (layer, 0, 0, 0)),
                      pl.BlockSpec((None, d, d), lay3),
                      pl.BlockSpec((None, N_EXPERTS, d), lay3),
                      pl.BlockSpec((None, N_EXPERTS, LANES), lay3),
                      pl.BlockSpec((tm, tm), const2),
                      pl.BlockSpec((N_EXPERTS, N_EXPERTS), const2)],
            out_specs=[pl.BlockSpec((tm, d), row),
                       pl.BlockSpec((tm, d), row),
                       pl.BlockSpec((TOP_K, tm), col),
                       pl.BlockSpec((TOP_K, tm), col),
                       pl.BlockSpec((None, N_EXPERTS, LANES), lambda i, t: (i, 0, 0)),
                       pl.BlockSpec((None, N_EXPERTS, LANES), lambda i, t: (i, 0, 0))]),
        out_shape=[jax.ShapeDtypeStruct((n, d), F32),
                   jax.ShapeDtypeStruct((n, d), BF16),
                   jax.ShapeDtypeStruct((TOP_K, n), F32),
                   jax.ShapeDtypeStruct((TOP_K, n), I32),
                   jax.ShapeDtypeStruct((n // tm, N_EXPERTS, LANES), F32),
                   jax.ShapeDtypeStruct((n // tm, N_EXPERTS, LANES), F32)],
        compiler_params=_params(1),
        name="merge_router",
    )(tile_mod, *branch_args, z_gate, *x_args, mod, g_post, g_pre_ffn, w_branch, w_out, w_router_t, b_router,
      tri, ltri)
    return outs


def _segment_chunks(length, src_ref, src0, dst_ref, dst0, sem, max_chunk, fixed_src=False):
    out = []
    chunk = max_chunk
    while chunk >= SEG_ALIGN:
        done = jnp.bitwise_and(length, ~(2 * chunk - 1))
        present = jnp.bitwise_and(length, chunk) != 0
        s = 0 if fixed_src else pl.multiple_of(src0 + done, SEG_ALIGN)
        dd = pl.multiple_of(dst0 + done, SEG_ALIGN)
        out.append((present, pltpu.make_async_copy(src_ref.at[pl.ds(s, chunk)], dst_ref.at[pl.ds(dd, chunk)], sem)))
        chunk //= 2
    return out


def _for_each_chunk(n_segments, chunks_of, action):
    def body(e, c):
        for present, cp in chunks_of(e):
            pl.when(present)(functools.partial(action, cp))
        return c

    lax.fori_loop(0, n_segments, body, 0, unroll=4)


def _start(cp):
    cp.start()


def _wait(cp):
    cp.wait()


def _wait_rows(total, src_ref, dst_ref, sem):
    chunk = pl.next_power_of_2(SORT_ROWS) // 2
    while chunk >= SEG_ALIGN:
        @pl.when(jnp.bitwise_and(total, chunk) != 0)
        def _(chunk=chunk):
            pltpu.make_async_copy(src_ref.at[pl.ds(0, chunk)], dst_ref.at[pl.ds(0, chunk)], sem).wait()
        chunk //= 2


def _offset_in_chunk(rows, c0):
    assert SORT_CHUNK == 256
    inside = lax.shift_right_logical(rows, 8) == c0 // SORT_CHUNK
    return jnp.where(inside, jnp.bitwise_and(rows, SORT_CHUNK - 1), -1).astype(BF16)


def _scatter_kernel(seg_ref, off_ref, pos_ref, used_ref, tpos_ref, tlen_ref, h_ref, lpos_ref, xe_hbm,
                    buf_ref, zero_ref, sem, *, tm):
    t = pl.program_id(0)
    slot = t % 2
    hb = h_ref[...].astype(BF16)
    lp = [lpos_ref[k:k + 1, :] for k in range(TOP_K)]
    rows = buf_ref.shape[1]
    r_off = lax.broadcasted_iota(I32, (SORT_CHUNK, tm), 0).astype(BF16)
    one = jnp.ones((SORT_CHUNK, tm), BF16)
    zero = jnp.zeros((SORT_CHUNK, tm), BF16)
    for c0 in range(0, rows, SORT_CHUNK):
        here = [_offset_in_chunk(p, c0) for p in lp]
        hit = jnp.logical_or(jnp.logical_or(r_off == here[0], r_off == here[1]),
                             jnp.logical_or(r_off == here[2], r_off == here[3]))
        onehot = jnp.where(hit, one, zero)
        buf_ref[slot, c0:c0 + SORT_CHUNK, :] = jnp.dot(onehot, hb, preferred_element_type=F32)

    def segments_of(tile):
        def segment(e):
            j = tile * N_EXPERTS + e
            return _segment_chunks(seg_ref[j], buf_ref.at[tile % 2], off_ref[j], xe_hbm, pos_ref[j],
                                   sem.at[tile % 2], tm)
        return segment

    @pl.when(t > 0)
    def _():
        _wait_rows(used_ref[t - 1], buf_ref.at[1 - slot], xe_hbm, sem.at[1 - slot])

    _for_each_chunk(N_EXPERTS, segments_of(t), _start)

    @pl.when(t == pl.num_programs(0) - 1)
    def _():
        _wait_rows(used_ref[t], buf_ref.at[slot], xe_hbm, sem.at[slot])
        zero_ref[...] = jnp.zeros_like(zero_ref)

        def tail(e):
            return _segment_chunks(tlen_ref[e], zero_ref, 0, xe_hbm, tpos_ref[e], sem.at[0], zero_ref.shape[0],
                                   fixed_src=True)

        _for_each_chunk(N_EXPERTS, tail, _start)
        _for_each_chunk(N_EXPERTS, tail, _wait)


def _scatter_rows(h2, lpos, plan, n_rows):
    n, d = h2.shape
    tm = FIN_TILE
    return pl.pallas_call(
        functools.partial(_scatter_kernel, tm=tm),
        grid_spec=pltpu.PrefetchScalarGridSpec(
            num_scalar_prefetch=6, grid=(n // tm,),
            in_specs=[pl.BlockSpec((tm, d), lambda i, *_: (i, 0)),
                      pl.BlockSpec((TOP_K, tm), lambda i, *_: (0, i))],
            out_specs=pl.BlockSpec(memory_space=pl.ANY),
            scratch_shapes=[pltpu.VMEM((2, SORT_ROWS, d), F32), pltpu.VMEM((MOE_BLOCK // 2, d), F32),
                            pltpu.SemaphoreType.DMA((2,))]),
        out_shape=jax.ShapeDtypeStruct((n_rows, d), F32),
        compiler_params=_params(1),
        name="moe_scatter",
    )(plan["seg"], plan["off"], plan["pos"], plan["used"], plan["tail_pos"], plan["tail_len"], h2, lpos)


def _expert_kernel(be_ref, nxt_ref, valid_ref, nu_ref, xb_ref, w1_hbm, b1_ref, w2_hbm, b2_ref, yb_ref,
                   w1s_ref, w2s_ref, w1b_ref, w2b_ref, sem, *, layer):
    i = pl.program_id(0)
    used = i < nu_ref[0]
    fresh = jnp.logical_or(i == 0, be_ref[i] != be_ref[jnp.maximum(i - 1, 0)])

    def fetch(e):
        return (pltpu.make_async_copy(w1_hbm.at[layer, e], w1s_ref, sem.at[0]),
                pltpu.make_async_copy(w2_hbm.at[layer, e], w2s_ref, sem.at[1]))

    @pl.when(i == 0)
    def _():
        for cp in fetch(be_ref[0]):
            cp.start()

    @pl.when(jnp.logical_and(used, fresh))
    def _():
        for cp in fetch(be_ref[i]):
            cp.wait()
        w1b_ref[...] = w1s_ref[...].astype(BF16)
        w2b_ref[...] = w2s_ref[...].astype(BF16)

        @pl.when(nxt_ref[i] >= 0)
        def _():
            for cp in fetch(nxt_ref[i]):
                cp.start()

    def ffn(m):
        e = be_ref[i]
        z = jnp.dot(xb_ref[0:m, :].astype(BF16), w1b_ref[...], preferred_element_type=F32) + b1_ref[e]
        glu = jnp.minimum(z[:, :D_FF], SWIGLU_LIMIT)
        lin = jnp.clip(z[:, D_FF:], -SWIGLU_LIMIT, SWIGLU_LIMIT)
        act = glu * jax.nn.sigmoid(SWIGLU_ALPHA * glu) * (lin + 1.0)
        yb_ref[0:m, :] = jnp.dot(act.astype(BF16), w2b_ref[...], preferred_element_type=F32) + b2_ref[e]
        if m < yb_ref.shape[0]:
            yb_ref[m:, :] = jnp.zeros((yb_ref.shape[0] - m, yb_ref.shape[1]), F32)

    valid = valid_ref[i]
    for m in range(EXPERT_ROW_STEP, yb_ref.shape[0] + 1, EXPERT_ROW_STEP):
        pl.when(jnp.logical_and(valid > m - EXPERT_ROW_STEP, valid <= m))(functools.partial(ffn, m))

    @pl.when(valid == 0)
    def _():
        yb_ref[...] = jnp.zeros_like(yb_ref)


def _experts(xb, block_expert, next_expert, valid_rows, n_used, w1, b1, w2, b2, layer):
    n_rows, d = xb.shape
    tm = MOE_BLOCK
    return pl.pallas_call(
        functools.partial(_expert_kernel, layer=layer),
        grid_spec=pltpu.PrefetchScalarGridSpec(
            num_scalar_prefetch=4, grid=(n_rows // tm,),
            in_specs=[pl.BlockSpec((tm, d), lambda i, be, nx, vr, nu: (jnp.minimum(i, nu[0] - 1), 0)),
                      pl.BlockSpec(memory_space=pl.ANY),
                      pl.BlockSpec((None, N_EXPERTS, 1, 2 * D_FF), lambda i, be, nx, vr, nu: (layer, 0, 0, 0)),
                      pl.BlockSpec(memory_space=pl.ANY),
                      pl.BlockSpec((None, N_EXPERTS, 1, d), lambda i, be, nx, vr, nu: (layer, 0, 0, 0))],
            out_specs=pl.BlockSpec((tm, d), lambda i, be, nx, vr, nu: (i, 0)),
            scratch_shapes=[pltpu.VMEM((d, 2 * D_FF), F32), pltpu.VMEM((D_FF, d), F32),
                            pltpu.VMEM((d, 2 * D_FF), BF16), pltpu.VMEM((D_FF, d), BF16),
                            pltpu.SemaphoreType.DMA((2,))]),
        out_shape=jax.ShapeDtypeStruct((n_rows, d), F32),
        compiler_params=_params(1),
        name="moe_experts",
    )(block_expert, next_expert, valid_rows, n_used, xb, w1, b1, w2, b2)


def _gather_kernel(tmod_ref, seg_ref, off_ref, pos_ref, used_ref, ye_hbm, lpos_ref, wts_ref, x1_ref, mod_ref, g_ref,
                   *refs, tm, tile0, prenorm_next):
    del tmod_ref
    if prenorm_next:
        modn_ref, gn_ref, o_ref, hn_ref, buf_ref, sem = refs
    else:
        o_ref, buf_ref, sem = refs
    step = pl.program_id(0)
    t = step + tile0
    slot = step % 2

    def segments_of(tile):
        def segment(e):
            j = tile * N_EXPERTS + e
            half = (tile - tile0) % 2
            return _segment_chunks(seg_ref[j], ye_hbm, pos_ref[j], buf_ref.at[half], off_ref[j], sem.at[half], tm)
        return segment

    @pl.when(step == 0)
    def _():
        buf_ref[...] = jnp.zeros_like(buf_ref)
        _for_each_chunk(N_EXPERTS, segments_of(t), _start)

    @pl.when(step + 1 < pl.num_programs(0))
    def _():
        _for_each_chunk(N_EXPERTS, segments_of(t + 1), _start)

    _wait_rows(used_ref[t], ye_hbm, buf_ref.at[slot], sem.at[slot])

    lp = [lpos_ref[:, k:k + 1] for k in range(TOP_K)]
    wt = [jnp.broadcast_to(wts_ref[:, k:k + 1].astype(BF16), (tm, SORT_CHUNK)) for k in range(TOP_K)]
    c_off = lax.broadcasted_iota(I32, (tm, SORT_CHUNK), 1).astype(BF16)
    y = jnp.zeros(o_ref.shape, F32)
    for c0 in range(0, buf_ref.shape[1], SORT_CHUNK):
        wm = jnp.zeros((tm, SORT_CHUNK), BF16)
        for k in range(TOP_K):
            wm = jnp.where(c_off == _offset_in_chunk(lp[k], c0), wt[k], wm)
        y = y + jnp.dot(wm, buf_ref[slot, c0:c0 + SORT_CHUNK, :].astype(BF16), preferred_element_type=F32)
    x2 = x1_ref[...] + mod_ref[5:6, :] * (y * _rms(y) * g_ref[...])
    o_ref[...] = x2
    if prenorm_next:
        hn = x2 * _rms(x2) * gn_ref[...]
        hn_ref[...] = (hn * (1.0 + modn_ref[1:2, :]) + modn_ref[0:1, :]).astype(hn_ref.dtype)


def _gather_combine(ye, lpos_t, wts_t, plan, x1, mod, g_post, layer, tile_mod, row0=0, rows=None, g_pre_next=None):
    n, d = x1.shape
    tm = FIN_TILE
    rows = n if rows is None else rows
    tile0 = row0 // tm
    row = lambda i, *_: (tile0 + i, 0)
    out_row = lambda i, *_: (i, 0)
    mod_spec = lambda lay: pl.BlockSpec((None, None, 6, d), lambda i, t, *_: (lay, t[tile0 + i], 0, 0))
    gain_spec = lambda lay: pl.BlockSpec((None, 1, d), lambda i, *_: (lay, 0, 0))
    prenorm_next = g_pre_next is not None
    in_specs = [pl.BlockSpec(memory_space=pl.ANY),
                pl.BlockSpec((tm, TOP_K), row),
                pl.BlockSpec((tm, TOP_K), row),
                pl.BlockSpec((tm, d), row),
                mod_spec(layer), gain_spec(layer)]
    args = [ye, lpos_t, wts_t, x1, mod, g_post]
    out_specs = [pl.BlockSpec((tm, d), out_row)]
    out_shape = [jax.ShapeDtypeStruct((rows, d), F32)]
    if prenorm_next:
        in_specs += [mod_spec(layer + 1), gain_spec(layer + 1)]
        args += [mod, g_pre_next]
        out_specs += [pl.BlockSpec((tm, d), out_row)]
        out_shape += [jax.ShapeDtypeStruct((rows, d), BF16)]
    outs = pl.pallas_call(
        functools.partial(_gather_kernel, tm=tm, tile0=tile0, prenorm_next=prenorm_next),
        grid_spec=pltpu.PrefetchScalarGridSpec(
            num_scalar_prefetch=5, grid=(rows // tm,),
            in_specs=in_specs, out_specs=out_specs,
            scratch_shapes=[pltpu.VMEM((2, SORT_ROWS, d), F32), pltpu.SemaphoreType.DMA((2,))]),
        out_shape=out_shape,
        compiler_params=_params(1),
        name="moe_gather",
    )(tile_mod, plan["seg"], plan["off"], plan["pos"], plan["used"], *args)
    return outs if prenorm_next else outs[0]


def _moe(h2, wts, lpos, seg, off, x1, mod, g_post, w1, b1, w2, b2, layer, tile_mod, split_rows=None,
         g_pre_next=None):
    n, d = h2.shape
    blk = MOE_BLOCK
    tm = FIN_TILE
    tiles = n // tm
    n_rows = -(-(n * TOP_K + tiles * N_EXPERTS * (SEG_ALIGN - 1) + N_EXPERTS * (blk - 1)) // blk) * blk
    n_blocks = n_rows // blk
    seg = seg[:, :, 0].astype(I32)
    off = off[:, :, 0].astype(I32)
    rows_e = jnp.sum(seg, axis=0)
    region = (rows_e + blk - 1) // blk * blk
    pend = jnp.cumsum(region)
    pstart = pend - region
    pos = pstart[None, :] + jnp.cumsum(seg, axis=0) - seg
    plan = {"seg": seg.reshape(-1), "off": off.reshape(-1), "pos": pos.reshape(-1).astype(I32),
            "used": jnp.sum(seg, axis=1).astype(I32),
            "tail_pos": (pstart + rows_e).astype(I32), "tail_len": (region - rows_e).astype(I32)}
    blocks = jnp.arange(n_blocks, dtype=I32) * blk
    block_expert = jnp.minimum(jnp.sum(blocks[:, None] >= pend[None, :], axis=1), N_EXPERTS - 1).astype(I32)
    n_used = (pend[-1:] // blk).astype(I32)
    valid_rows = jnp.clip((pstart + rows_e)[block_expert] - blocks, 0, blk).astype(I32)
    ids = jnp.arange(n_blocks, dtype=I32)
    run_start = jnp.logical_and(jnp.concatenate([jnp.ones((1,), bool), block_expert[1:] != block_expert[:-1]]),
                                ids < n_used[0])
    first_after = lax.cummin(jnp.where(run_start, ids, n_blocks)[::-1])[::-1]
    first_after = jnp.concatenate([first_after[1:], jnp.full((1,), n_blocks, I32)])
    next_expert = jnp.where(first_after < n_blocks, block_expert[jnp.minimum(first_after, n_blocks - 1)], -1)
    xe = _scatter_rows(h2, lpos, plan, n_rows)
    ye = _experts(xe, block_expert, next_expert.astype(I32), valid_rows, n_used, w1, b1, w2, b2, layer)
    combine = functools.partial(_gather_combine, ye, lpos.T, wts.T, plan, x1, mod, g_post, layer, tile_mod)
    if split_rows is None:
        return combine(g_pre_next=g_pre_next)
    return combine(row0=0, rows=split_rows), combine(row0=split_rows, rows=n - split_rows)


def _tile_mod_ids(n_ctx_rows, n_lat_rows, lat_seq, tm):
    ctx = np.zeros((n_ctx_rows // tm,), np.int32)
    lat = 1 + (np.arange(n_lat_rows // tm) * tm) // lat_seq
    return jnp.asarray(np.concatenate([ctx, lat.astype(np.int32)]))


def kernel(x_prompt, x_sample, cache_k, cache_v, state_ret_fwd, state_ret_bwd, c, c_ctx, w_mod, b_mod, g_pre_mix, g_post_mix, g_pre_ffn, g_post_ffn, w_in, na_rel_bias, ret_decay_fwd, ret_decay_bwd, w_branch, w_out, w_router, b_router, w_exp_in, b_exp_in, w_exp_out, b_exp_out):
    batch, seq, d = x_prompt.shape
    dec_batch, dec_seq, _ = x_sample.shape
    depth = w_in.shape[0]
    n_ctx = batch * seq
    n_lat = dec_batch * dec_seq
    assert 1 + dec_batch <= MOD_ROWS

    x = (x_prompt.reshape(n_ctx, d), x_sample.reshape(n_lat, d))
    cvec =jnp.concatenate([c_ctx[None], c, jnp.zeros((MOD_ROWS - 1 - dec_batch, d), F32)], axis=0)
    mod_all = _modulation(cvec, w_mod, b_mod).reshape(depth, MOD_ROWS, 6, d)
    tmod = {tm: _tile_mod_ids(n_ctx, n_lat, dec_seq, tm) for tm in (ROW_TILE, FIN_TILE)}
    past = cache_k.shape[2]
    ck = cache_k.reshape(dec_batch, depth, past, NA_WIDTH)
    cv = cache_v.reshape(dec_batch, depth, past, NA_WIDTH)
    lg_f = jax.nn.log_sigmoid(ret_decay_fwd.astype(F32)).reshape(-1)
    lg_b = jax.nn.log_sigmoid(ret_decay_bwd.astype(F32)).reshape(-1)
    bias_all = _neighbourhood_bias(na_rel_bias, dec_seq)
    vec = lambda g: g.reshape(depth, 1, d)
    g_pre_mix, g_post_mix, g_pre_ffn, g_post_ffn = vec(g_pre_mix), vec(g_post_mix), vec(g_pre_ffn), vec(g_post_ffn)
    w_branch_b = w_branch.astype(BF16)
    w_out_b = w_out.astype(BF16)
    w_router_t = jnp.swapaxes(w_router, 1, 2)
    b_router_l = jnp.broadcast_to(b_router[:, :, None], (depth, N_EXPERTS, LANES))
    b_exp_in = b_exp_in.reshape(depth, N_EXPERTS, 1, 2 * D_FF)
    b_exp_out = b_exp_out.reshape(depth, N_EXPERTS, 1, d)

    ks, vs, sfs, sbs = [], [], [], []
    h = _prenorm(x, g_pre_mix, mod_all, 0, tmod[ROW_TILE], ROW_TILE)
    for l in range(depth):
        last = l == depth - 1
        z_qkvu = _project(h, w_in, l, 0, QKVU_W, F32)
        z_ret = _project(h, w_in, l, QKVU_W, RET_W, BF16)
        z_gate = _project(h, w_in, l, QKVU_W + RET_W, GATE_W, BF16)
        ks.append(z_qkvu[:n_ctx, NA_WIDTH:2 * NA_WIDTH].reshape(batch, seq, NA_HEADS, NA_HEAD_DIM))
        vs.append(z_qkvu[:n_ctx, 2 * NA_WIDTH:3 * NA_WIDTH].reshape(batch, seq, NA_HEADS, NA_HEAD_DIM))

        a_pair = (_attention_ctx(z_qkvu, batch, seq),
                  _attention_lat(z_qkvu, ck, cv, bias_all, l, n_ctx, dec_batch, dec_seq))
        f_pair = (_fourier(z_qkvu, 0, batch, seq), _fourier(z_qkvu, n_ctx, dec_batch, dec_seq))
        r_ctx, s_f, s_b = _retention(z_ret, lg_f, lg_b, l, 0, batch, seq, rotary=False, state_out=True)
        (r_lat,) = _retention(z_ret, lg_f, lg_b, l, n_ctx, dec_batch, dec_seq, rotary=True,
                              states=(state_ret_fwd, state_ret_bwd))
        sfs.append(s_f)
        sbs.append(s_b)

        x1, h2, wts, lpos, seg, off = _finish(
            (a_pair, f_pair, (r_ctx, r_lat)), z_gate, x, mod_all, g_post_mix, g_pre_ffn,
            w_branch_b, w_out_b, w_router_t, b_router_l, l, tmod[FIN_TILE])
        out = _moe(h2, wts, lpos, seg, off, x1, mod_all, g_post_ffn,
                   w_exp_in, b_exp_in, w_exp_out, b_exp_out, l, tmod[FIN_TILE],
                   split_rows=n_ctx if last else None, g_pre_next=None if last else g_pre_mix)
        x, h = (out, None) if last else out

    y_prompt = x[0].reshape(batch, seq, d)
    y_sample = x[1].reshape(dec_batch, dec_seq, d)
    return (y_prompt, y_sample, jnp.stack(ks, axis=1), jnp.stack(vs, axis=1),
            jnp.stack(sfs, axis=1), jnp.stack(sbs, axis=1))
```

```python
import functools

import numpy as np
import jax
import jax.numpy as jnp
from jax import lax
from jax.experimental import pallas as pl
from jax.experimental.pallas import tpu as pltpu

F32 = jnp.float32
BF16 = jnp.bfloat16
I32 = jnp.int32

D_MODEL = 1024
GRID_W = 64
NA_HEADS = 8
NA_HEAD_DIM = 64
NA_WIDTH = NA_HEADS * NA_HEAD_DIM
WIN_H = 8
WIN_W = 16
KEY_SLAB_ROWS = 12
FOURIER_GROUPS = 4
FOURIER_GROUP_DIM = 128
RET_HEADS = 4
RET_KEY_DIM = 128
ROPE_BASE = 10000.0
BRANCH_WIDTH = 512
N_EXPERTS = 32
TOP_K = 4
D_FF = 1024
SWIGLU_LIMIT = 7.0
SWIGLU_ALPHA = 1.702
EPS = 1e-6
NEG_INF = -1e30

QKVU_W = 4 * NA_WIDTH
RET_W = 4 * BRANCH_WIDTH
GATE_W = 3 * D_MODEL
PROJ_TILE = 1024
PROJ_ROWS = 2048
ATTN_Q_TILE = 512
ATTN_PAIRS_PER_STEP = 2
RET_Q_TILE = 1024
RET_DECAY_BYTES = 4 * 1024 * 1024

LANES = 128
MOD_ROWS = 16
ROW_TILE = 1024
FIN_TILE = 512
MOE_BLOCK = 512
EXPERT_ROW_STEP = 128
SEG_ALIGN = 8
SORT_CHUNK = 256
SORT_ROWS = -(-(FIN_TILE * TOP_K + N_EXPERTS * (SEG_ALIGN - 1)) // SORT_CHUNK) * SORT_CHUNK
VMEM_LIMIT = 56 * 1024 * 1024


def _params(n_axes, vmem=VMEM_LIMIT):
    return pltpu.CompilerParams(dimension_semantics=("arbitrary",) * n_axes, vmem_limit_bytes=vmem)


def _rms(x):
    return lax.rsqrt(jnp.mean(x * x, axis=-1, keepdims=True) + EPS)


def _mod_kernel(cv_ref, w_ref, b_ref, o_ref):
    cv = cv_ref[...]
    s = (cv * jax.nn.sigmoid(cv)).astype(BF16)
    o_ref[...] = jnp.dot(s, w_ref[...].astype(BF16), preferred_element_type=F32) + b_ref[...]


def _modulation(cv, w_mod, b_mod):
    depth, d, n = w_mod.shape
    tn = 1536
    return pl.pallas_call(
        _mod_kernel,
        grid=(depth, n // tn),
        in_specs=[pl.BlockSpec((MOD_ROWS, d), lambda l, j: (0, 0)),
                  pl.BlockSpec((None, d, tn), lambda l, j: (l, 0, j)),
                  pl.BlockSpec((None, 1, tn), lambda l, j: (l, 0, j))],
        out_specs=pl.BlockSpec((None, MOD_ROWS, tn), lambda l, j: (l, 0, j)),
        out_shape=jax.ShapeDtypeStruct((depth, MOD_ROWS, n), F32),
        compiler_params=_params(2),
        name="modulation",
    )(cv, w_mod, b_mod.reshape(depth, 1, n))


def _token_rows(x, tm):
    if isinstance(x, tuple):
        d = x[0].shape[1]
        ct = x[0].shape[0] // tm
        specs = [pl.BlockSpec((tm, d), lambda i, *_: (jnp.minimum(i, ct - 1), 0)),
                 pl.BlockSpec((tm, d), lambda i, *_: (jnp.maximum(i - ct, 0), 0))]
        return specs, list(x), ct, x[0].shape[0] + x[1].shape[0]
    return [pl.BlockSpec((tm, x.shape[1]), lambda i, *_: (i, 0))], [x], None, x.shape[0]


def _load_rows(refs, ctx_tiles):
    if len(refs) == 1:
        return refs[0][...]
    return jnp.where(pl.program_id(0) < ctx_tiles, refs[0][...], refs[1][...])


def _prenorm_kernel(tmod_ref, *refs, ctx_tiles):
    del tmod_ref
    g_ref, mod_ref, o_ref = refs[-3:]
    x = _load_rows(refs[:-3], ctx_tiles)
    h = x * _rms(x) * g_ref[...]
    o_ref[...] = (h * (1.0 + mod_ref[1:2, :]) + mod_ref[0:1, :]).astype(o_ref.dtype)


def _prenorm(x, g, mod, layer, tile_mod, tm):
    x_specs, x_args, ctx_tiles, n = _token_rows(x, tm)
    d = x_args[0].shape[1]
    return pl.pallas_call(
        functools.partial(_prenorm_kernel, ctx_tiles=ctx_tiles),
        grid_spec=pltpu.PrefetchScalarGridSpec(
            num_scalar_prefetch=1, grid=(n // tm,),
            in_specs=x_specs + [pl.BlockSpec((None, 1, d), lambda i, t: (layer, 0, 0)),
                                pl.BlockSpec((None, None, 6, d), lambda i, t: (layer, t[i], 0, 0))],
            out_specs=pl.BlockSpec((tm, d), lambda i, t: (i, 0))),
        out_shape=jax.ShapeDtypeStruct((n, d), BF16),
        compiler_params=_params(1),
        name="prenorm",
    )(tile_mod, *x_args, g, mod)


def _proj_kernel(h_ref, w_ref, o_ref, wb_ref):
    @pl.when(pl.program_id(1) == 0)
    def _():
        wb_ref[...] = w_ref[...].astype(BF16)

    o_ref[...] = jnp.dot(h_ref[...], wb_ref[...], preferred_element_type=F32).astype(o_ref.dtype)


def _project(h, w, layer, col0, width, out_dtype):
    n, d = h.shape
    tm = PROJ_ROWS
    tn = PROJ_TILE
    cb = col0 // tn
    return pl.pallas_call(
        _proj_kernel,
        grid=(width // tn, n // tm),
        in_specs=[pl.BlockSpec((tm, d), lambda j, i: (i, 0)),
                  pl.BlockSpec((None, d, tn), lambda j, i: (layer, 0, cb + j))],
        out_specs=pl.BlockSpec((tm, tn), lambda j, i: (i, j)),
        out_shape=jax.ShapeDtypeStruct((n, width), out_dtype),
        scratch_shapes=[pltpu.VMEM((d, tn), BF16)],
        compiler_params=_params(2),
        name="in_proj",
    )(h, w)


def _head_pair_masks():
    lane = lax.broadcasted_iota(I32, (1, LANES), 1)
    first = lane < NA_HEAD_DIM
    return first, jnp.logical_not(first)


def _attn_ctx_kernel(q_ref, k_ref, v_ref, o_ref):
    masks = _head_pair_masks()
    scale = NA_HEAD_DIM ** -0.5
    for p in range(NA_WIDTH // LANES):
        cols = slice(LANES * p, LANES * (p + 1))
        q2 = q_ref[:, cols] * scale
        k2 = k_ref[:, cols].astype(BF16)
        v2 = v_ref[:, cols].astype(BF16)
        outs = []
        for m in masks:
            qa = jnp.where(m, q2, 0.0).astype(BF16)
            s = lax.dot_general(qa, k2, (((1,), (1,)), ((), ())), preferred_element_type=F32)
            e = jnp.exp(s - jnp.max(s, axis=-1, keepdims=True))
            den = jnp.sum(e, axis=-1, keepdims=True)
            outs.append(jnp.dot(e.astype(BF16), v2, preferred_element_type=F32) / den)
        o_ref[:, cols] = jnp.where(masks[0], outs[0], outs[1]).astype(o_ref.dtype)


def _attention_ctx(z_qkv, n_seq, seq):
    return pl.pallas_call(
        _attn_ctx_kernel,
        grid=(n_seq,),
        in_specs=[pl.BlockSpec((seq, NA_WIDTH), lambda b: (b, 0)),
                  pl.BlockSpec((seq, NA_WIDTH), lambda b: (b, 1)),
                  pl.BlockSpec((seq, NA_WIDTH), lambda b: (b, 2))],
        out_specs=pl.BlockSpec((seq, NA_WIDTH), lambda b: (b, 0)),
        out_shape=jax.ShapeDtypeStruct((n_seq * seq, NA_WIDTH), BF16),
        compiler_params=_params(1),
        name="attn_ctx",
    )(z_qkv, z_qkv, z_qkv)


def _attn_lat_kernel(q_ref, k_ref, v_ref, kc_ref, vc_ref, bias_ref, o_ref, kb_ref, vb_ref, *, tq):
    masks = _head_pair_masks()
    scale = NA_HEAD_DIM ** -0.5
    seq = q_ref.shape[0]
    slab = bias_ref.shape[3]
    per_half = seq // 2 // tq
    kb_ref[...] = k_ref[...].astype(BF16)
    vb_ref[...] = v_ref[...].astype(BF16)
    kc = kc_ref[...].astype(BF16)
    vc = vc_ref[...].astype(BF16)
    nt = (((1,), (1,)), ((), ()))

    def q_tile(qi, carry):
        rows = pl.ds(pl.multiple_of(qi * tq, tq), tq)
        half = qi // per_half
        half_rows = pl.ds(pl.multiple_of((qi % per_half) * tq, tq), tq)
        keys = pl.ds(pl.multiple_of(half * (seq - slab), seq - slab), slab)
        for pp in range(q_ref.shape[1] // LANES):
            cols = slice(LANES * pp, LANES * (pp + 1))
            k2 = kb_ref[keys, cols]
            v2 = vb_ref[keys, cols]
            q2 = q_ref[rows, cols] * scale
            outs = []
            for hh, m in enumerate(masks):
                qa = jnp.where(m, q2, 0.0).astype(BF16)
                s_lat = (lax.dot_general(qa, k2, nt, preferred_element_type=F32)
                         + bias_ref[2 * pp + hh, half, half_rows, :])
                s_ctx = lax.dot_general(qa, kc[:, cols], nt, preferred_element_type=F32)
                mx = jnp.maximum(jnp.max(s_lat, axis=-1, keepdims=True), jnp.max(s_ctx, axis=-1, keepdims=True))
                e_lat = jnp.exp(s_lat - mx)
                e_ctx = jnp.exp(s_ctx - mx)
                den = jnp.sum(e_lat, axis=-1, keepdims=True) + jnp.sum(e_ctx, axis=-1, keepdims=True)
                o = (jnp.dot(e_lat.astype(BF16), v2, preferred_element_type=F32)
                     + jnp.dot(e_ctx.astype(BF16), vc[:, cols], preferred_element_type=F32))
                outs.append(o / den)
            o_ref[rows, cols] = jnp.where(masks[0], outs[0], outs[1]).astype(o_ref.dtype)
        return carry

    lax.fori_loop(0, q_ref.shape[0] // tq, q_tile, 0, unroll=True)


def _attention_lat(z_qkv, cache_k, cache_v, bias, layer, row0, n_seq, seq):
    past = cache_k.shape[2]
    width = ATTN_PAIRS_PER_STEP * LANES
    steps = NA_WIDTH // width
    rb = row0 // seq
    slab = bias.shape[-1]
    return pl.pallas_call(
        functools.partial(_attn_lat_kernel, tq=min(seq // 2, ATTN_Q_TILE)),
        grid=(steps, n_seq),
        in_specs=[pl.BlockSpec((seq, width), lambda p, b: (rb + b, p)),
                  pl.BlockSpec((seq, width), lambda p, b: (rb + b, steps + p)),
                  pl.BlockSpec((seq, width), lambda p, b: (rb + b, 2 * steps + p)),
                  pl.BlockSpec((None, None, past, width), lambda p, b: (b, layer, 0, p)),
                  pl.BlockSpec((None, None, past, width), lambda p, b: (b, layer, 0, p)),
                  pl.BlockSpec((None, 2 * ATTN_PAIRS_PER_STEP, 2, seq // 2, slab),
                               lambda p, b: (layer, p, 0, 0, 0))],
        out_specs=pl.BlockSpec((seq, width), lambda p, b: (b, p)),
        out_shape=jax.ShapeDtypeStruct((n_seq * seq, NA_WIDTH), BF16),
        scratch_shapes=[pltpu.VMEM((seq, width), BF16), pltpu.VMEM((seq, width), BF16)],
        compiler_params=_params(2),
        name="attn_lat",
    )(z_qkv, z_qkv, z_qkv, cache_k, cache_v, bias)


def _neighbourhood_bias(rpb, seq):
    rows = seq // GRID_W
    kh = WIN_H
    assert rows >= WIN_H
    lead = rpb.shape[:-2]
    c = np.arange(GRID_W)
    q_cs = np.clip(c - WIN_W // 2, 0, GRID_W - WIN_W)
    col_ok = (c[None, :] >= q_cs[:, None]) & (c[None, :] < q_cs[:, None] + WIN_W)
    r = np.arange(rows)
    rs = np.clip(r - kh // 2, 0, rows - kh)
    base = np.where(r < rows // 2, 0, rows - KEY_SLAB_ROWS)
    assert (rs >= base).all() and (rs + kh <= base + KEY_SLAB_ROWS).all()
    pick_c = (c[None, None, :] - c[None, :, None] + WIN_W - 1
              == np.arange(2 * WIN_W - 1)[:, None, None]).astype(np.float32)
    w = jnp.einsum("...ij,jqk->...qik", rpb, pick_c, precision=lax.Precision.HIGHEST)
    w = jnp.where(jnp.asarray(col_ok)[:, None, :], w, NEG_INF)
    blocks = []
    for rq in range(rows):
        lo = int(rs[rq]) - rq + WIN_H - 1
        slab = w[..., lo:lo + kh, :].reshape(lead + (GRID_W, kh * GRID_W))
        left = int(rs[rq] - base[rq])
        pad = ((0, 0),) * (len(lead) + 1) + ((left * GRID_W, (KEY_SLAB_ROWS - kh - left) * GRID_W),)
        blocks.append(jnp.pad(slab, pad, constant_values=NEG_INF))
    return jnp.stack(blocks, axis=-3).reshape(lead + (2, seq // 2, KEY_SLAB_ROWS * GRID_W))


def _fourier_kernel(u_ref, ct2_ref, cc_ref, sc_ref, o_ref, pq_ref):
    t = u_ref.shape[0]
    for g in range(FOURIER_GROUPS):
        cols = slice(FOURIER_GROUP_DIM * g, FOURIER_GROUP_DIM * (g + 1))
        ug = u_ref[:, cols].astype(BF16)
        pq_ref[0:t, cols] = jnp.dot(ug, cc_ref[...], preferred_element_type=F32).astype(BF16)
        pq_ref[t:2 * t, cols] = jnp.dot(ug, sc_ref[...], preferred_element_type=F32).astype(BF16)
    o_ref[...] = jnp.dot(ct2_ref[...], pq_ref[...], preferred_element_type=F32).astype(o_ref.dtype)


def _dft_tables(t):
    def cs(n):
        k = np.arange(n, dtype=np.int64)
        ang = 2.0 * np.pi * ((k[:, None] * k[None, :]) % n).astype(np.float64) / n
        return np.cos(ang) / np.sqrt(n), np.sin(ang) / np.sqrt(n)

    ct, st = cs(t)
    cc, sc = cs(FOURIER_GROUP_DIM)
    ct2 = np.concatenate([ct, -st], axis=1).astype(np.float32)
    return (jnp.asarray(ct2).astype(BF16), jnp.asarray(cc.astype(np.float32)).astype(BF16),
            jnp.asarray(sc.astype(np.float32)).astype(BF16))


def _fourier(z_qkvu, row0, n_seq, seq):
    ct2, cc, sc = _dft_tables(seq)
    width = FOURIER_GROUPS * FOURIER_GROUP_DIM
    rb = row0 // seq
    ucol = 3 * NA_WIDTH // width
    return pl.pallas_call(
        _fourier_kernel,
        grid=(n_seq,),
        in_specs=[pl.BlockSpec((seq, width), lambda b: (rb + b, ucol)),
                  pl.BlockSpec((seq, 2 * seq), lambda b: (0, 0)),
                  pl.BlockSpec((FOURIER_GROUP_DIM, FOURIER_GROUP_DIM), lambda b: (0, 0)),
                  pl.BlockSpec((FOURIER_GROUP_DIM, FOURIER_GROUP_DIM), lambda b: (0, 0))],
        out_specs=pl.BlockSpec((seq, width), lambda b: (b, 0)),
        out_shape=jax.ShapeDtypeStruct((n_seq * seq, width), BF16),
        scratch_shapes=[pltpu.VMEM((2 * seq, width), BF16)],
        compiler_params=_params(1),
        name="fourier",
    )(z_qkvu, ct2, cc, sc)


def _rotary_tables(t):
    pos = np.arange(t)
    row = (pos // GRID_W).astype(np.float64)
    col = (pos % GRID_W).astype(np.float64)
    nf = RET_KEY_DIM // 4
    inv_freq = ROPE_BASE ** (-np.arange(nf, dtype=np.float64) / nf)
    ar = row[:, None] * inv_freq[None]
    ac = col[:, None] * inv_freq[None]
    cos = np.concatenate([np.cos(ar), np.cos(ar), np.cos(ac), np.cos(ac)], axis=1)
    sin = np.concatenate([-np.sin(ar), np.sin(ar), -np.sin(ac), np.sin(ac)], axis=1)
    return jnp.asarray(cos.astype(np.float32)), jnp.asarray(sin.astype(np.float32))


def _ret_kernel(lgf_ref, lgb_ref, *refs, t, tq, layer, rotary, state_in, state_out):
    refs = list(refs)
    q_ref, k_ref, v_ref, g_ref = refs[:4]
    refs = refs[4:]
    if rotary:
        cos_ref, sin_ref = refs[:2]
        refs = refs[2:]
    if state_in:
        sf0_ref, sb0_ref = refs[:2]
        refs = refs[2:]
    o_ref = refs[0]
    refs = refs[1:]
    if state_out:
        sf_ref, sb_ref = refs[:2]
        refs = refs[2:]
    dec_ref, kb_ref = refs

    scale = RET_KEY_DIM ** -0.5
    nq = t // tq
    heads_here = dec_ref.shape[0]

    if rotary:
        lane = lax.broadcasted_iota(I32, (1, LANES), 1)
        low = (lane % (RET_KEY_DIM // 2)) < (RET_KEY_DIM // 4)

        def rot(x, rows):
            swapped = jnp.where(low, pltpu.roll(x, LANES - RET_KEY_DIM // 4, 1), pltpu.roll(x, RET_KEY_DIM // 4, 1))
            return x * cos_ref[rows, :] + swapped * sin_ref[rows, :]
    else:
        def rot(x, rows):
            return x

    def one_head(hh):
        h = pl.program_id(0) * heads_here + hh
        cols = slice(LANES * hh, LANES * (hh + 1))
        lgf = lgf_ref[layer * RET_HEADS + h]
        lgb = lgb_ref[layer * RET_HEADS + h]

        @pl.when(pl.program_id(1) == 0)
        def _():
            def fill(ri, c):
                rows = pl.ds(pl.multiple_of(ri * tq, tq), tq)
                i = lax.broadcasted_iota(I32, (tq, t), 0) + ri * tq
                j = lax.broadcasted_iota(I32, (tq, t), 1)
                d = (i - j).astype(F32)
                m = jnp.exp(jnp.abs(d) * jnp.where(d > 0, lgf, lgb))
                dec_ref[hh, rows, :] = jnp.where(d == 0, 2.0, m)
                return c

            lax.fori_loop(0, nq, fill, 0)

        kr = rot(k_ref[:, cols].astype(F32), slice(0, t))
        kb_ref[hh] = kr.astype(BF16)
        vb = v_ref[:, cols]

        if state_out:
            j = lax.broadcasted_iota(I32, (t, 1), 0).astype(F32)
            tn = (((0,), (0,)), ((), ()))
            kf = (kr * (scale * jnp.exp(lgf * (t - 1.0 - j)))).astype(BF16)
            kbw = (kr * (scale * jnp.exp(lgb * j))).astype(BF16)
            sf = lax.dot_general(kf, vb, tn, preferred_element_type=F32)
            sb = lax.dot_general(kbw, vb, tn, preferred_element_type=F32)
            if state_in:
                sf = sf + jnp.exp(lgf * t) * sf0_ref[hh]
                sb = sb + jnp.exp(lgb * t) * sb0_ref[hh]
            sf_ref[hh] = sf
            sb_ref[hh] = sb

        def q_tile(qi, carry):
            r0 = pl.multiple_of(qi * tq, tq)
            rows = pl.ds(r0, tq)
            qr = rot(q_ref[rows, cols].astype(F32), rows)
            s = lax.dot_general((qr * scale).astype(BF16), kb_ref[hh], (((1,), (1,)), ((), ())),
                                preferred_element_type=F32)
            y = jnp.dot((s * dec_ref[hh, rows, :]).astype(BF16), vb, preferred_element_type=F32)
            if state_in:
                pos = (lax.broadcasted_iota(I32, (tq, 1), 0) + r0).astype(F32)
                qf = (qr * jnp.exp(lgf * (pos + 1.0))).astype(BF16)
                qb = (qr * jnp.exp(lgb * (t - pos))).astype(BF16)
                y = (y + jnp.dot(qf, sf0_ref[hh].astype(BF16), preferred_element_type=F32)
                     + jnp.dot(qb, sb0_ref[hh].astype(BF16), preferred_element_type=F32))
            mean = jnp.mean(y, axis=-1, keepdims=True)
            yc = y - mean
            yn = yc * lax.rsqrt(jnp.mean(yc * yc, axis=-1, keepdims=True) + EPS)
            g = g_ref[rows, cols].astype(F32)
            o_ref[rows, cols] = (g * jax.nn.sigmoid(g) * yn).astype(o_ref.dtype)
            return carry

        lax.fori_loop(0, nq, q_tile, 0)

    for hh in range(heads_here):
        one_head(hh)


def _retention(z_ret, lg_f, lg_b, layer, row0, n_seq, seq, *, rotary, states=None, state_out=False):
    rb = row0 // seq
    tq = min(seq, RET_Q_TILE)
    state_in = states is not None
    hps = RET_HEADS if seq * seq * RET_HEADS * 4 <= RET_DECAY_BYTES else 1
    cb = BRANCH_WIDTH // (hps * LANES)
    width = hps * LANES
    in_specs = [pl.BlockSpec((seq, width), lambda h, b, *_: (rb + b, 0 * cb + h)),
                pl.BlockSpec((seq, width), lambda h, b, *_: (rb + b, 1 * cb + h)),
                pl.BlockSpec((seq, width), lambda h, b, *_: (rb + b, 2 * cb + h)),
                pl.BlockSpec((seq, width), lambda h, b, *_: (rb + b, 3 * cb + h))]
    args = [z_ret, z_ret, z_ret, z_ret]
    if rotary:
        cos, sin = _rotary_tables(seq)
        in_specs += [pl.BlockSpec((seq, LANES), lambda h, b, *_: (0, 0))] * 2
        args += [cos, sin]
    if state_in:
        st_spec = pl.BlockSpec((None, None, hps, RET_KEY_DIM, RET_KEY_DIM), lambda h, b, *_: (b, layer, h, 0, 0))
        in_specs += [st_spec, st_spec]
        args += list(states)
    out_specs = [pl.BlockSpec((seq, width), lambda h, b, *_: (b, h))]
    out_shape = [jax.ShapeDtypeStruct((n_seq * seq, RET_HEADS * LANES), BF16)]
    if state_out:
        so = pl.BlockSpec((None, hps, RET_KEY_DIM, RET_KEY_DIM), lambda h, b, *_: (b, h, 0, 0))
        out_specs += [so, so]
        out_shape += [jax.ShapeDtypeStruct((n_seq, RET_HEADS, RET_KEY_DIM, RET_KEY_DIM), F32)] * 2
    return pl.pallas_call(
        functools.partial(_ret_kernel, t=seq, tq=tq, layer=layer, rotary=rotary, state_in=state_in,
                          state_out=state_out),
        grid_spec=pltpu.PrefetchScalarGridSpec(
            num_scalar_prefetch=2, grid=(RET_HEADS // hps, n_seq),
            in_specs=in_specs, out_specs=out_specs,
            scratch_shapes=[pltpu.VMEM((hps, seq, seq), F32), pltpu.VMEM((hps, seq, LANES), BF16)]),
        out_shape=out_shape,
        compiler_params=_params(2),
        name="retention",
    )(lg_f, lg_b, *args)


def _split_dot_nt(w, x):
    nt = (((1,), (1,)), ((), ()))
    w_hi = w.astype(BF16)
    w_lo = (w - w_hi.astype(F32)).astype(BF16)
    x_hi = x.astype(BF16)
    x_lo = (x - x_hi.astype(F32)).astype(BF16)
    return (lax.dot_general(w_hi, x_hi, nt, preferred_element_type=F32)
            + lax.dot_general(w_hi, x_lo, nt, preferred_element_type=F32)
            + lax.dot_general(w_lo, x_hi, nt, preferred_element_type=F32))


def _finish_kernel(tmod_ref, *refs, ctx_tiles, n_x):
    del tmod_ref
    branch_refs, refs = refs[:6], refs[6:]
    zg_ref, refs = refs[0], refs[1:]
    x_refs, refs = refs[:n_x], refs[n_x:]
    (mod_ref, gpost_ref, gpre_ref, wb_ref, wo_ref, wrt_ref, br_ref, tri_ref, ltri_ref,
     x1_ref, h2_ref, wts_ref, lpos_ref, seg_ref, off_ref) = refs
    d = D_MODEL

    def branch(j):
        return _load_rows(branch_refs[2 * j:2 * j + 2], ctx_tiles)

    def gate(j):
        return jax.nn.sigmoid(zg_ref[:, d * j:d * (j + 1)].astype(F32))

    merged = (gate(0) * jnp.dot(branch(0), wb_ref[0], preferred_element_type=F32)
              + gate(1) * jnp.dot(branch(1), wb_ref[1], preferred_element_type=F32)
              + gate(2) * jnp.dot(branch(2), wb_ref[2], preferred_element_type=F32))
    y = jnp.dot(merged.astype(BF16), wo_ref[...], preferred_element_type=F32)
    x1 = _load_rows(x_refs, ctx_tiles) + mod_ref[2:3, :] * (y * _rms(y) * gpost_ref[...])
    x1_ref[...] = x1
    h2 = x1 * _rms(x1) * gpre_ref[...] * (1.0 + mod_ref[4:5, :]) + mod_ref[3:4, :]
    h2_ref[...] = h2

    logits = _split_dot_nt(wrt_ref[...], h2) + br_ref[:, 0:1]
    tm = logits.shape[1]
    eidx = lax.broadcasted_iota(I32, (N_EXPERTS, tm), 0)
    cur = logits
    vals, hots = [], []
    for k in range(TOP_K):
        m = jnp.max(cur, axis=0, keepdims=True)
        sel = jnp.min(jnp.where(cur == m, eidx, N_EXPERTS), axis=0, keepdims=True)
        hot = eidx == sel
        vals.append(m)
        hots.append(hot)
        cur = jnp.where(hot, -jnp.inf, cur)
    exps = [jnp.exp(v - vals[0]) for v in vals]
    den = exps[0] + exps[1] + exps[2] + exps[3]
    for k in range(TOP_K):
        wts_ref[k:k + 1, :] = exps[k] / den

    member = jnp.logical_or(jnp.logical_or(hots[0], hots[1]), jnp.logical_or(hots[2], hots[3]))
    member_f = member.astype(F32)
    before = jnp.dot(member_f.astype(BF16), tri_ref[...], preferred_element_type=F32)
    units = jnp.ceil(jnp.sum(member_f, axis=1, keepdims=True) * (1.0 / SEG_ALIGN))
    units = jnp.broadcast_to(units, seg_ref.shape)
    off = jnp.dot(ltri_ref[...], units.astype(BF16), preferred_element_type=F32) * SEG_ALIGN
    seg_ref[...] = units * SEG_ALIGN
    off_ref[...] = off
    place = before + off[:, 0:1]
    for k in range(TOP_K):
        lpos_ref[k:k + 1, :] = jnp.sum(jnp.where(hots[k], place, 0.0), axis=0, keepdims=True).astype(I32)


def _finish(branches, z_gate, x, mod, g_post, g_pre_ffn, w_branch, w_out, w_router_t, b_router, layer, tile_mod):
    tm = FIN_TILE
    x_specs, x_args, _, n = _token_rows(x, tm)
    d = x_args[0].shape[1]
    tri = jnp.asarray(np.triu(np.ones((tm, tm), np.float32), k=1)).astype(BF16)
    ltri = jnp.asarray(np.tril(np.ones((N_EXPERTS, N_EXPERTS), np.float32), k=-1)).astype(BF16)
    row = lambda i, t: (i, 0)
    const2 = lambda i, t: (0, 0)
    lay3 = lambda i, t: (layer, 0, 0)
    col = lambda i, t: (0, i)
    branch_specs, branch_args = [], []
    for pair in branches:
        specs, args, ctx_tiles, _ = _token_rows(pair, tm)
        branch_specs += specs
        branch_args += args
    outs = pl.pallas_call(
        functools.partial(_finish_kernel, ctx_tiles=ctx_tiles, n_x=len(x_args)),
        grid_spec=pltpu.PrefetchScalarGridSpec(
            num_scalar_prefetch=1, grid=(n // tm,),
            in_specs=branch_specs + [pl.BlockSpec((tm, GATE_W), row)] + x_specs + [
                      pl.BlockSpec((None, None, 6, d), lambda i, t: (layer, t[i], 0, 0)),
                      pl.BlockSpec((None, 1, d), lay3),
                      pl.BlockSpec((None, 1, d), lay3),
                      pl.BlockSpec((None, 3, BRANCH_WIDTH, d), lambda i, t: (layer, 0, 0, 0)),
                      pl.BlockSpec((None, d, d), lay3),
                      pl.BlockSpec((None, N_EXPERTS, d), lay3),
                      pl.BlockSpec((None, N_EXPERTS, LANES), lay3),
                      pl.BlockSpec((tm, tm), const2),
                      pl.BlockSpec((N_EXPERTS, N_EXPERTS), const2)],
            out_specs=[pl.BlockSpec((tm, d), row),
                       pl.BlockSpec((tm, d), row),
                       pl.BlockSpec((TOP_K, tm), col),
                       pl.BlockSpec((TOP_K, tm), col),
                       pl.BlockSpec((None, N_EXPERTS, LANES), lambda i, t: (i, 0, 0)),
                       pl.BlockSpec((None, N_EXPERTS, LANES), lambda i, t: (i, 0, 0))]),
        out_shape=[jax.ShapeDtypeStruct((n, d), F32),
                   jax.ShapeDtypeStruct((n, d), F32),
                   jax.ShapeDtypeStruct((TOP_K, n), F32),
                   jax.ShapeDtypeStruct((TOP_K, n), I32),
                   jax.ShapeDtypeStruct((n // tm, N_EXPERTS, LANES), F32),
                   jax.ShapeDtypeStruct((n // tm, N_EXPERTS, LANES), F32)],
        compiler_params=_params(1),
        name="merge_router",
    )(tile_mod, *branch_args, z_gate, *x_args, mod, g_post, g_pre_ffn, w_branch, w_out, w_router_t, b_router,
      tri, ltri)
    return outs


def _segment_chunks(length, src_ref, src0, dst_ref, dst0, sem, max_chunk, fixed_src=False):
    out = []
    chunk = max_chunk
    while chunk >= SEG_ALIGN:
        done = jnp.bitwise_and(length, ~(2 * chunk - 1))
        present = jnp.bitwise_and(length, chunk) != 0
        s = 0 if fixed_src else pl.multiple_of(src0 + done, SEG_ALIGN)
        dd = pl.multiple_of(dst0 + done, SEG_ALIGN)
        out.append((present, pltpu.make_async_copy(src_ref.at[pl.ds(s, chunk)], dst_ref.at[pl.ds(dd, chunk)], sem)))
        chunk //= 2
    return out


def _for_each_chunk(n_segments, chunks_of, action):
    def body(e, c):
        for present, cp in chunks_of(e):
            pl.when(present)(functools.partial(action, cp))
        return c

    lax.fori_loop(0, n_segments, body, 0, unroll=4)


def _start(cp):
    cp.start()


def _wait(cp):
    cp.wait()


def _wait_rows(total, src_ref, dst_ref, sem):
    chunk = pl.next_power_of_2(SORT_ROWS) // 2
    while chunk >= SEG_ALIGN:
        @pl.when(jnp.bitwise_and(total, chunk) != 0)
        def _(chunk=chunk):
            pltpu.make_async_copy(src_ref.at[pl.ds(0, chunk)], dst_ref.at[pl.ds(0, chunk)], sem).wait()
        chunk //= 2


def _offset_in_chunk(rows, c0):
    assert SORT_CHUNK == 256
    inside = lax.shift_right_logical(rows, 8) == c0 // SORT_CHUNK
    return jnp.where(inside, jnp.bitwise_and(rows, SORT_CHUNK - 1), -1).astype(BF16)


def _scatter_kernel(seg_ref, off_ref, pos_ref, used_ref, tpos_ref, tlen_ref, end_ref, h_ref, lpos_ref, xe_hbm,
                    buf_ref, zero_ref, sem, *, tm):
    t = pl.program_id(0)
    slot = t % 2
    hb = h_ref[...].astype(BF16)
    lp = [lpos_ref[k:k + 1, :] for k in range(TOP_K)]
    rows = buf_ref.shape[1]
    r_off = lax.broadcasted_iota(I32, (SORT_CHUNK, tm), 0).astype(BF16)
    one = jnp.ones((SORT_CHUNK, tm), BF16)
    zero = jnp.zeros((SORT_CHUNK, tm), BF16)
    for c0 in range(0, rows, SORT_CHUNK):
        here = [_offset_in_chunk(p, c0) for p in lp]
        hit = jnp.logical_or(jnp.logical_or(r_off == here[0], r_off == here[1]),
                             jnp.logical_or(r_off == here[2], r_off == here[3]))
        onehot = jnp.where(hit, one, zero)
        buf_ref[slot, c0:c0 + SORT_CHUNK, :] = jnp.dot(onehot, hb, preferred_element_type=F32)

    def segments_of(tile):
        def segment(e):
            j = tile * N_EXPERTS + e
            return _segment_chunks(seg_ref[j], buf_ref.at[tile % 2], off_ref[j], xe_hbm, pos_ref[j],
                                   sem.at[tile % 2], tm)
        return segment

    @pl.when(t > 0)
    def _():
        _wait_rows(used_ref[t - 1], buf_ref.at[1 - slot], xe_hbm, sem.at[1 - slot])

    _for_each_chunk(N_EXPERTS, segments_of(t), _start)

    @pl.when(t == pl.num_programs(0) - 1)
    def _():
        _wait_rows(used_ref[t], buf_ref.at[slot], xe_hbm, sem.at[slot])
        zero_ref[...] = jnp.zeros_like(zero_ref)

        def tail(e):
            return _segment_chunks(tlen_ref[e], zero_ref, 0, xe_hbm, tpos_ref[e], sem.at[0], zero_ref.shape[0],
                                   fixed_src=True)

        _for_each_chunk(N_EXPERTS, tail, _start)
        _for_each_chunk(N_EXPERTS, tail, _wait)

        zero_rows = zero_ref.shape[0]
        first_unused = end_ref[0]
        n_fill = lax.div(xe_hbm.shape[0] - first_unused, zero_rows)

        def fill_copy(j):
            dst = pl.multiple_of(first_unused + j * zero_rows, zero_rows)
            return pltpu.make_async_copy(zero_ref, xe_hbm.at[pl.ds(dst, zero_rows)], sem.at[0])

        def fill_start(j, c):
            fill_copy(j).start()
            return c

        def fill_wait(j, c):
            fill_copy(j).wait()
            return c

        lax.fori_loop(0, n_fill, fill_start, 0)
        lax.fori_loop(0, n_fill, fill_wait, 0)


def _scatter_rows(h2, lpos, plan, n_rows):
    n, d = h2.shape
    tm = FIN_TILE
    return pl.pallas_call(
        functools.partial(_scatter_kernel, tm=tm),
        grid_spec=pltpu.PrefetchScalarGridSpec(
            num_scalar_prefetch=7, grid=(n // tm,),
            in_specs=[pl.BlockSpec((tm, d), lambda i, *_: (i, 0)),
                      pl.BlockSpec((TOP_K, tm), lambda i, *_: (0, i))],
            out_specs=pl.BlockSpec(memory_space=pl.ANY),
            scratch_shapes=[pltpu.VMEM((2, SORT_ROWS, d), F32), pltpu.VMEM((MOE_BLOCK // 2, d), F32),
                            pltpu.SemaphoreType.DMA((2,))]),
        out_shape=jax.ShapeDtypeStruct((n_rows, d), F32),
        compiler_params=_params(1),
        name="moe_scatter",
    )(plan["seg"], plan["off"], plan["pos"], plan["used"], plan["tail_pos"], plan["tail_len"], plan["end"], h2, lpos)


def _expert_kernel(be_ref, nxt_ref, valid_ref, nu_ref, xb_ref, w1_hbm, b1_ref, w2_hbm, b2_ref, yb_ref,
                   w1s_ref, w2s_ref, w1b_ref, w2b_ref, sem, *, layer):
    i = pl.program_id(0)
    used = i < nu_ref[0]
    fresh = jnp.logical_or(i == 0, be_ref[i] != be_ref[jnp.maximum(i - 1, 0)])

    def fetch(e):
        return (pltpu.make_async_copy(w1_hbm.at[layer, e], w1s_ref, sem.at[0]),
                pltpu.make_async_copy(w2_hbm.at[layer, e], w2s_ref, sem.at[1]))

    @pl.when(i == 0)
    def _():
        for cp in fetch(be_ref[0]):
            cp.start()

    @pl.when(jnp.logical_and(used, fresh))
    def _():
        for cp in fetch(be_ref[i]):
            cp.wait()
        w1b_ref[...] = w1s_ref[...].astype(BF16)
        w2b_ref[...] = w2s_ref[...].astype(BF16)

        @pl.when(nxt_ref[i] >= 0)
        def _():
            for cp in fetch(nxt_ref[i]):
                cp.start()

    def ffn(m):
        e = be_ref[i]
        z = jnp.dot(xb_ref[0:m, :].astype(BF16), w1b_ref[...], preferred_element_type=F32) + b1_ref[e]
        glu = jnp.minimum(z[:, :D_FF], SWIGLU_LIMIT)
        lin = jnp.clip(z[:, D_FF:], -SWIGLU_LIMIT, SWIGLU_LIMIT)
        act = glu * jax.nn.sigmoid(SWIGLU_ALPHA * glu) * (lin + 1.0)
        yb_ref[0:m, :] = jnp.dot(act.astype(BF16), w2b_ref[...], preferred_element_type=F32) + b2_ref[e]
        if m < yb_ref.shape[0]:
            yb_ref[m:, :] = jnp.zeros((yb_ref.shape[0] - m, yb_ref.shape[1]), F32)

    valid = valid_ref[i]
    for m in range(EXPERT_ROW_STEP, yb_ref.shape[0] + 1, EXPERT_ROW_STEP):
        pl.when(jnp.logical_and(valid > m - EXPERT_ROW_STEP, valid <= m))(functools.partial(ffn, m))

    @pl.when(valid == 0)
    def _():
        yb_ref[...] = jnp.zeros_like(yb_ref)


def _experts(xb, block_expert, next_expert, valid_rows, n_used, w1, b1, w2, b2, layer):
    n_rows, d = xb.shape
    tm = MOE_BLOCK
    return pl.pallas_call(
        functools.partial(_expert_kernel, layer=layer),
        grid_spec=pltpu.PrefetchScalarGridSpec(
            num_scalar_prefetch=4, grid=(n_rows // tm,),
            in_specs=[pl.BlockSpec((tm, d), lambda i, be, nx, vr, nu: (jnp.minimum(i, nu[0] - 1), 0)),
                      pl.BlockSpec(memory_space=pl.ANY),
                      pl.BlockSpec((None, N_EXPERTS, 1, 2 * D_FF), lambda i, be, nx, vr, nu: (layer, 0, 0, 0)),
                      pl.BlockSpec(memory_space=pl.ANY),
                      pl.BlockSpec((None, N_EXPERTS, 1, d), lambda i, be, nx, vr, nu: (layer, 0, 0, 0))],
            out_specs=pl.BlockSpec((tm, d), lambda i, be, nx, vr, nu: (i, 0)),
            scratch_shapes=[pltpu.VMEM((d, 2 * D_FF), F32), pltpu.VMEM((D_FF, d), F32),
                            pltpu.VMEM((d, 2 * D_FF), BF16), pltpu.VMEM((D_FF, d), BF16),
                            pltpu.SemaphoreType.DMA((2,))]),
        out_shape=jax.ShapeDtypeStruct((n_rows, d), F32),
        compiler_params=_params(1),
        name="moe_experts",
    )(block_expert, next_expert, valid_rows, n_used, xb, w1, b1, w2, b2)


def _gather_kernel(tmod_ref, seg_ref, off_ref, pos_ref, used_ref, ye_hbm, lpos_ref, wts_ref, x1_ref, mod_ref, g_ref,
                   *refs, tm, tile0, prenorm_next):
    del tmod_ref
    if prenorm_next:
        modn_ref, gn_ref, o_ref, hn_ref, buf_ref, sem = refs
    else:
        o_ref, buf_ref, sem = refs
    step = pl.program_id(0)
    t = step + tile0
    slot = step % 2

    def segments_of(tile):
        def segment(e):
            j = tile * N_EXPERTS + e
            half = (tile - tile0) % 2
            return _segment_chunks(seg_ref[j], ye_hbm, pos_ref[j], buf_ref.at[half], off_ref[j], sem.at[half], tm)
        return segment

    @pl.when(step == 0)
    def _():
        buf_ref[...] = jnp.zeros_like(buf_ref)
        _for_each_chunk(N_EXPERTS, segments_of(t), _start)

    @pl.when(step + 1 < pl.num_programs(0))
    def _():
        _for_each_chunk(N_EXPERTS, segments_of(t + 1), _start)

    _wait_rows(used_ref[t], ye_hbm, buf_ref.at[slot], sem.at[slot])

    lp = [lpos_ref[:, k:k + 1] for k in range(TOP_K)]
    wt = [jnp.broadcast_to(wts_ref[:, k:k + 1].astype(BF16), (tm, SORT_CHUNK)) for k in range(TOP_K)]
    c_off = lax.broadcasted_iota(I32, (tm, SORT_CHUNK), 1).astype(BF16)
    y = jnp.zeros(o_ref.shape, F32)
    for c0 in range(0, buf_ref.shape[1], SORT_CHUNK):
        wm = jnp.zeros((tm, SORT_CHUNK), BF16)
        for k in range(TOP_K):
            wm = jnp.where(c_off == _offset_in_chunk(lp[k], c0), wt[k], wm)
        y = y + jnp.dot(wm, buf_ref[slot, c0:c0 + SORT_CHUNK, :].astype(BF16), preferred_element_type=F32)
    x2 = x1_ref[...] + mod_ref[5:6, :] * (y * _rms(y) * g_ref[...])
    o_ref[...] = x2
    if prenorm_next:
        hn = x2 * _rms(x2) * gn_ref[...]
        hn_ref[...] = (hn * (1.0 + modn_ref[1:2, :]) + modn_ref[0:1, :]).astype(hn_ref.dtype)


def _gather_combine(ye, lpos_t, wts_t, plan, x1, mod, g_post, layer, tile_mod, row0=0, rows=None, g_pre_next=None):
    n, d = x1.shape
    tm = FIN_TILE
    rows = n if rows is None else rows
    tile0 = row0 // tm
    row = lambda i, *_: (tile0 + i, 0)
    out_row = lambda i, *_: (i, 0)
    mod_spec = lambda lay: pl.BlockSpec((None, None, 6, d), lambda i, t, *_: (lay, t[tile0 + i], 0, 0))
    gain_spec = lambda lay: pl.BlockSpec((None, 1, d), lambda i, *_: (lay, 0, 0))
    prenorm_next = g_pre_next is not None
    in_specs = [pl.BlockSpec(memory_space=pl.ANY),
                pl.BlockSpec((tm, TOP_K), row),
                pl.BlockSpec((tm, TOP_K), row),
                pl.BlockSpec((tm, d), row),
                mod_spec(layer), gain_spec(layer)]
    args = [ye, lpos_t, wts_t, x1, mod, g_post]
    out_specs = [pl.BlockSpec((tm, d), out_row)]
    out_shape = [jax.ShapeDtypeStruct((rows, d), F32)]
    if prenorm_next:
        in_specs += [mod_spec(layer + 1), gain_spec(layer + 1)]
        args += [mod, g_pre_next]
        out_specs += [pl.BlockSpec((tm, d), out_row)]
        out_shape += [jax.ShapeDtypeStruct((rows, d), BF16)]
    outs = pl.pallas_call(
        functools.partial(_gather_kernel, tm=tm, tile0=tile0, prenorm_next=prenorm_next),
        grid_spec=pltpu.PrefetchScalarGridSpec(
            num_scalar_prefetch=5, grid=(rows // tm,),
            in_specs=in_specs, out_specs=out_specs,
            scratch_shapes=[pltpu.VMEM((2, SORT_ROWS, d), F32), pltpu.SemaphoreType.DMA((2,))]),
        out_shape=out_shape,
        compiler_params=_params(1),
        name="moe_gather",
    )(tile_mod, plan["seg"], plan["off"], plan["pos"], plan["used"], *args)
    return outs if prenorm_next else outs[0]


def _moe(h2, wts, lpos, seg, off, x1, mod, g_post, w1, b1, w2, b2, layer, tile_mod, split_rows=None,
         g_pre_next=None):
    n, d = h2.shape
    blk = MOE_BLOCK
    tm = FIN_TILE
    tiles = n // tm
    n_rows = -(-(n * TOP_K + tiles * N_EXPERTS * (SEG_ALIGN - 1) + N_EXPERTS * (blk - 1)) // blk) * blk
    n_blocks = n_rows // blk
    seg = seg[:, :, 0].astype(I32)
    off = off[:, :, 0].astype(I32)
    rows_e = jnp.sum(seg, axis=0)
    region = (rows_e + blk - 1) // blk * blk
    pend = jnp.cumsum(region)
    pstart = pend - region
    pos = pstart[None, :] + jnp.cumsum(seg, axis=0) - seg
    plan = {"seg": seg.reshape(-1), "off": off.reshape(-1), "pos": pos.reshape(-1).astype(I32),
            "used": jnp.sum(seg, axis=1).astype(I32),
            "tail_pos": (pstart + rows_e).astype(I32), "tail_len": (region - rows_e).astype(I32),
            "end": pend[-1:].astype(I32)}
    blocks = jnp.arange(n_blocks, dtype=I32) * blk
    block_expert = jnp.minimum(jnp.sum(blocks[:, None] >= pend[None, :], axis=1), N_EXPERTS - 1).astype(I32)
    n_used = (pend[-1:] // blk).astype(I32)
    valid_rows = jnp.clip((pstart + rows_e)[block_expert] - blocks, 0, blk).astype(I32)
    ids = jnp.arange(n_blocks, dtype=I32)
    run_start = jnp.logical_and(jnp.concatenate([jnp.ones((1,), bool), block_expert[1:] != block_expert[:-1]]),
                                ids < n_used[0])
    first_after = lax.cummin(jnp.where(run_start, ids, n_blocks)[::-1])[::-1]
    first_after = jnp.concatenate([first_after[1:], jnp.full((1,), n_blocks, I32)])
    next_expert = jnp.where(first_after < n_blocks, block_expert[jnp.minimum(first_after, n_blocks - 1)], -1)
    xe = _scatter_rows(h2, lpos, plan, n_rows)
    ye = _experts(xe, block_expert, next_expert.astype(I32), valid_rows, n_used, w1, b1, w2, b2, layer)
    combine = functools.partial(_gather_combine, ye, lpos.T, wts.T, plan, x1, mod, g_post, layer, tile_mod)
    if split_rows is None:
        return combine(g_pre_next=g_pre_next)
    return combine(row0=0, rows=split_rows), combine(row0=split_rows, rows=n - split_rows)


def _tile_mod_ids(n_ctx_rows, n_lat_rows, lat_seq, tm):
    ctx = np.zeros((n_ctx_rows // tm,), np.int32)
    lat = 1 + (np.arange(n_lat_rows // tm) * tm) // lat_seq
    return jnp.asarray(np.concatenate([ctx, lat.astype(np.int32)]))


def kernel(x_prompt, x_sample, cache_k, cache_v, state_ret_fwd, state_ret_bwd, c, c_ctx, w_mod, b_mod, g_pre_mix, g_post_mix, g_pre_ffn, g_post_ffn, w_in, na_rel_bias, ret_decay_fwd, ret_decay_bwd, w_branch, w_out, w_router, b_router, w_exp_in, b_exp_in, w_exp_out, b_exp_out):
    batch, seq, d = x_prompt.shape
    dec_batch, dec_seq, _ = x_sample.shape
    depth = w_in.shape[0]
    n_ctx = batch * seq
    n_lat = dec_batch * dec_seq
    assert 1 + dec_batch <= MOD_ROWS

    x = (x_prompt.reshape(n_ctx, d), x_sample.reshape(n_lat, d))
    cvec =jnp.concatenate([c_ctx[None], c, jnp.zeros((MOD_ROWS - 1 - dec_batch, d), F32)], axis=0)
    mod_all = _modulation(cvec, w_mod, b_mod).reshape(depth, MOD_ROWS, 6, d)
    tmod = {tm: _tile_mod_ids(n_ctx, n_lat, dec_seq, tm) for tm in (ROW_TILE, FIN_TILE)}
    past = cache_k.shape[2]
    ck = cache_k.reshape(dec_batch, depth, past, NA_WIDTH)
    cv = cache_v.reshape(dec_batch, depth, past, NA_WIDTH)
    lg_f = jax.nn.log_sigmoid(ret_decay_fwd.astype(F32)).reshape(-1)
    lg_b = jax.nn.log_sigmoid(ret_decay_bwd.astype(F32)).reshape(-1)
    bias_all = _neighbourhood_bias(na_rel_bias, dec_seq)
    vec = lambda g: g.reshape(depth, 1, d)
    g_pre_mix, g_post_mix, g_pre_ffn, g_post_ffn = vec(g_pre_mix), vec(g_post_mix), vec(g_pre_ffn), vec(g_post_ffn)
    w_branch_b = w_branch.astype(BF16)
    w_out_b = w_out.astype(BF16)
    w_router_t = jnp.swapaxes(w_router, 1, 2)
    b_router_l = jnp.broadcast_to(b_router[:, :, None], (depth, N_EXPERTS, LANES))
    b_exp_in = b_exp_in.reshape(depth, N_EXPERTS, 1, 2 * D_FF)
    b_exp_out = b_exp_out.reshape(depth, N_EXPERTS, 1, d)

    ks, vs, sfs, sbs = [], [], [], []
    h = _prenorm(x, g_pre_mix, mod_all, 0, tmod[ROW_TILE], ROW_TILE)
    for l in range(depth):
        last = l == depth - 1
        z_qkvu = _project(h, w_in, l, 0, QKVU_W, F32)
        z_ret = _project(h, w_in, l, QKVU_W, RET_W, BF16)
        z_gate = _project(h, w_in, l, QKVU_W + RET_W, GATE_W, BF16)
        ks.append(z_qkvu[:n_ctx, NA_WIDTH:2 * NA_WIDTH].reshape(batch, seq, NA_HEADS, NA_HEAD_DIM))
        vs.append(z_qkvu[:n_ctx, 2 * NA_WIDTH:3 * NA_WIDTH].reshape(batch, seq, NA_HEADS, NA_HEAD_DIM))

        a_pair = (_attention_ctx(z_qkvu, batch, seq),
                  _attention_lat(z_qkvu, ck, cv, bias_all, l, n_ctx, dec_batch, dec_seq))
        f_pair = (_fourier(z_qkvu, 0, batch, seq), _fourier(z_qkvu, n_ctx, dec_batch, dec_seq))
        r_ctx, s_f, s_b = _retention(z_ret, lg_f, lg_b, l, 0, batch, seq, rotary=False, state_out=True)
        (r_lat,) = _retention(z_ret, lg_f, lg_b, l, n_ctx, dec_batch, dec_seq, rotary=True,
                              states=(state_ret_fwd, state_ret_bwd))
        sfs.append(s_f)
        sbs.append(s_b)

        x1, h2, wts, lpos, seg, off = _finish(
            (a_pair, f_pair, (r_ctx, r_lat)), z_gate, x, mod_all, g_post_mix, g_pre_ffn,
            w_branch_b, w_out_b, w_router_t, b_router_l, l, tmod[FIN_TILE])
        out = _moe(h2, wts, lpos, seg, off, x1, mod_all, g_post_ffn,
                   w_exp_in, b_exp_in, w_exp_out, b_exp_out, l, tmod[FIN_TILE],
                   split_rows=n_ctx if last else None, g_pre_next=None if last else g_pre_mix)
        x, h = (out, None) if last else out

    y_prompt = x[0].reshape(batch, seq, d)
    y_sample = x[1].reshape(dec_batch, dec_seq, d)
    return (y_prompt, y_sample, jnp.stack(ks, axis=1), jnp.stack(vs, axis=1),
            jnp.stack(sfs, axis=1), jnp.stack(sbs, axis=1))
```

```python
import functools

import numpy as np
import jax
import jax.numpy as jnp
from jax import lax
from jax.experimental import pallas as pl
from jax.experimental.pallas import tpu as pltpu

F32 = jnp.float32
BF16 = jnp.bfloat16
I32 = jnp.int32

D_MODEL = 1024
GRID_W = 64
NA_HEADS = 8
NA_HEAD_DIM = 64
NA_WIDTH = NA_HEADS * NA_HEAD_DIM
WIN_H = 8
WIN_W = 16
KEY_SLAB_ROWS = 12
FOURIER_GROUPS = 4
FOURIER_GROUP_DIM = 128
RET_HEADS = 4
RET_KEY_DIM = 128
ROPE_BASE = 10000.0
BRANCH_WIDTH = 512
N_EXPERTS = 32
TOP_K = 4
D_FF = 1024
SWIGLU_LIMIT = 7.0
SWIGLU_ALPHA = 1.702
EPS = 1e-6
NEG_INF = -1e30

QKVU_W = 4 * NA_WIDTH
RET_W = 4 * BRANCH_WIDTH
GATE_W = 3 * D_MODEL
PROJ_TILE = 1024
PROJ_ROWS = 2048
ATTN_Q_TILE = 512
ATTN_PAIRS_PER_STEP = 2
RET_Q_TILE = 1024
RET_DECAY_BYTES = 4 * 1024 * 1024

LANES = 128
MOD_ROWS = 16
ROW_TILE = 1024
FIN_TILE = 512
MOE_BLOCK = 512
EXPERT_ROW_STEP = 128
SEG_ALIGN = 8
SORT_CHUNK = 256
SORT_ROWS = -(-(FIN_TILE * TOP_K + N_EXPERTS * (SEG_ALIGN - 1)) // SORT_CHUNK) * SORT_CHUNK
VMEM_LIMIT = 56 * 1024 * 1024


def _params(n_axes, vmem=VMEM_LIMIT):
    return pltpu.CompilerParams(dimension_semantics=("arbitrary",) * n_axes, vmem_limit_bytes=vmem)


def _rms(x):
    return lax.rsqrt(jnp.mean(x * x, axis=-1, keepdims=True) + EPS)


def _mod_kernel(cv_ref, w_ref, b_ref, o_ref):
    cv = cv_ref[...]
    s = (cv * jax.nn.sigmoid(cv)).astype(BF16)
    o_ref[...] = jnp.dot(s, w_ref[...].astype(BF16), preferred_element_type=F32) + b_ref[...]


def _modulation(cv, w_mod, b_mod):
    depth, d, n = w_mod.shape
    tn = 1536
    return pl.pallas_call(
        _mod_kernel,
        grid=(depth, n // tn),
        in_specs=[pl.BlockSpec((MOD_ROWS, d), lambda l, j: (0, 0)),
                  pl.BlockSpec((None, d, tn), lambda l, j: (l, 0, j)),
                  pl.BlockSpec((None, 1, tn), lambda l, j: (l, 0, j))],
        out_specs=pl.BlockSpec((None, MOD_ROWS, tn), lambda l, j: (l, 0, j)),
        out_shape=jax.ShapeDtypeStruct((depth, MOD_ROWS, n), F32),
        compiler_params=_params(2),
        name="modulation",
    )(cv, w_mod, b_mod.reshape(depth, 1, n))


def _token_rows(x, tm):
    if isinstance(x, tuple):
        d = x[0].shape[1]
        ct = x[0].shape[0] // tm
        specs = [pl.BlockSpec((tm, d), lambda i, *_: (jnp.minimum(i, ct - 1), 0)),
                 pl.BlockSpec((tm, d), lambda i, *_: (jnp.maximum(i - ct, 0), 0))]
        return specs, list(x), ct, x[0].shape[0] + x[1].shape[0]
    return [pl.BlockSpec((tm, x.shape[1]), lambda i, *_: (i, 0))], [x], None, x.shape[0]


def _load_rows(refs, ctx_tiles):
    if len(refs) == 1:
        return refs[0][...]
    return jnp.where(pl.program_id(0) < ctx_tiles, refs[0][...], refs[1][...])


def _prenorm_kernel(tmod_ref, *refs, ctx_tiles):
    del tmod_ref
    g_ref, mod_ref, o_ref = refs[-3:]
    x = _load_rows(refs[:-3], ctx_tiles)
    h = x * _rms(x) * g_ref[...]
    o_ref[...] = (h * (1.0 + mod_ref[1:2, :]) + mod_ref[0:1, :]).astype(o_ref.dtype)


def _prenorm(x, g, mod, layer, tile_mod, tm):
    x_specs, x_args, ctx_tiles, n = _token_rows(x, tm)
    d = x_args[0].shape[1]
    return pl.pallas_call(
        functools.partial(_prenorm_kernel, ctx_tiles=ctx_tiles),
        grid_spec=pltpu.PrefetchScalarGridSpec(
            num_scalar_prefetch=1, grid=(n // tm,),
            in_specs=x_specs + [pl.BlockSpec((None, 1, d), lambda i, t: (layer, 0, 0)),
                                pl.BlockSpec((None, None, 6, d), lambda i, t: (layer, t[i], 0, 0))],
            out_specs=pl.BlockSpec((tm, d), lambda i, t: (i, 0))),
        out_shape=jax.ShapeDtypeStruct((n, d), BF16),
        compiler_params=_params(1),
        name="prenorm",
    )(tile_mod, *x_args, g, mod)


def _proj_kernel(h_ref, w_ref, o_ref, wb_ref):
    @pl.when(pl.program_id(1) == 0)
    def _():
        wb_ref[...] = w_ref[...].astype(BF16)

    o_ref[...] = jnp.dot(h_ref[...], wb_ref[...], preferred_element_type=F32).astype(o_ref.dtype)


def _project(h, w, layer, col0, width, out_dtype):
    n, d = h.shape
    tm = PROJ_ROWS
    tn = PROJ_TILE
    cb = col0 // tn
    return pl.pallas_call(
        _proj_kernel,
        grid=(width // tn, n // tm),
        in_specs=[pl.BlockSpec((tm, d), lambda j, i: (i, 0)),
                  pl.BlockSpec((None, d, tn), lambda j, i: (layer, 0, cb + j))],
        out_specs=pl.BlockSpec((tm, tn), lambda j, i: (i, j)),
        out_shape=jax.ShapeDtypeStruct((n, width), out_dtype),
        scratch_shapes=[pltpu.VMEM((d, tn), BF16)],
        compiler_params=_params(2),
        name="in_proj",
    )(h, w)


def _head_pair_masks():
    lane = lax.broadcasted_iota(I32, (1, LANES), 1)
    first = lane < NA_HEAD_DIM
    return first, jnp.logical_not(first)


def _attn_ctx_kernel(q_ref, k_ref, v_ref, o_ref):
    masks = _head_pair_masks()
    scale = NA_HEAD_DIM ** -0.5
    for p in range(NA_WIDTH // LANES):
        cols = slice(LANES * p, LANES * (p + 1))
        q2 = q_ref[:, cols] * scale
        k2 = k_ref[:, cols].astype(BF16)
        v2 = v_ref[:, cols].astype(BF16)
        outs = []
        for m in masks:
            qa = jnp.where(m, q2, 0.0).astype(BF16)
            s = lax.dot_general(qa, k2, (((1,), (1,)), ((), ())), preferred_element_type=F32)
            e = jnp.exp(s - jnp.max(s, axis=-1, keepdims=True))
            den = jnp.sum(e, axis=-1, keepdims=True)
            outs.append(jnp.dot(e.astype(BF16), v2, preferred_element_type=F32) / den)
        o_ref[:, cols] = jnp.where(masks[0], outs[0], outs[1]).astype(o_ref.dtype)


def _attention_ctx(z_qkv, n_seq, seq):
    return pl.pallas_call(
        _attn_ctx_kernel,
        grid=(n_seq,),
        in_specs=[pl.BlockSpec((seq, NA_WIDTH), lambda b: (b, 0)),
                  pl.BlockSpec((seq, NA_WIDTH), lambda b: (b, 1)),
                  pl.BlockSpec((seq, NA_WIDTH), lambda b: (b, 2))],
        out_specs=pl.BlockSpec((seq, NA_WIDTH), lambda b: (b, 0)),
        out_shape=jax.ShapeDtypeStruct((n_seq * seq, NA_WIDTH), BF16),
        compiler_params=_params(1),
        name="attn_ctx",
    )(z_qkv, z_qkv, z_qkv)


def _attn_lat_kernel(q_ref, k_ref, v_ref, kc_ref, vc_ref, bias_ref, o_ref, kb_ref, vb_ref, *, tq):
    masks = _head_pair_masks()
    scale = NA_HEAD_DIM ** -0.5
    seq = q_ref.shape[0]
    slab = bias_ref.shape[3]
    per_half = seq // 2 // tq
    kb_ref[...] = k_ref[...].astype(BF16)
    vb_ref[...] = v_ref[...].astype(BF16)
    kc = kc_ref[...].astype(BF16)
    vc = vc_ref[...].astype(BF16)
    nt = (((1,), (1,)), ((), ()))

    def q_tile(qi, carry):
        rows = pl.ds(pl.multiple_of(qi * tq, tq), tq)
        half = qi // per_half
        half_rows = pl.ds(pl.multiple_of((qi % per_half) * tq, tq), tq)
        keys = pl.ds(pl.multiple_of(half * (seq - slab), seq - slab), slab)
        for pp in range(q_ref.shape[1] // LANES):
            cols = slice(LANES * pp, LANES * (pp + 1))
            k2 = kb_ref[keys, cols]
            v2 = vb_ref[keys, cols]
            q2 = q_ref[rows, cols] * scale
            outs = []
            for hh, m in enumerate(masks):
                qa = jnp.where(m, q2, 0.0).astype(BF16)
                s_lat = (lax.dot_general(qa, k2, nt, preferred_element_type=F32)
                         + bias_ref[2 * pp + hh, half, half_rows, :])
                s_ctx = lax.dot_general(qa, kc[:, cols], nt, preferred_element_type=F32)
                mx = jnp.maximum(jnp.max(s_lat, axis=-1, keepdims=True), jnp.max(s_ctx, axis=-1, keepdims=True))
                e_lat = jnp.exp(s_lat - mx)
                e_ctx = jnp.exp(s_ctx - mx)
                den = jnp.sum(e_lat, axis=-1, keepdims=True) + jnp.sum(e_ctx, axis=-1, keepdims=True)
                o = (jnp.dot(e_lat.astype(BF16), v2, preferred_element_type=F32)
                     + jnp.dot(e_ctx.astype(BF16), vc[:, cols], preferred_element_type=F32))
                outs.append(o / den)
            o_ref[rows, cols] = jnp.where(masks[0], outs[0], outs[1]).astype(o_ref.dtype)
        return carry

    lax.fori_loop(0, q_ref.shape[0] // tq, q_tile, 0, unroll=True)


def _attention_lat(z_qkv, cache_k, cache_v, bias, layer, row0, n_seq, seq):
    past = cache_k.shape[2]
    width = ATTN_PAIRS_PER_STEP * LANES
    steps = NA_WIDTH // width
    rb = row0 // seq
    slab = bias.shape[-1]
    return pl.pallas_call(
        functools.partial(_attn_lat_kernel, tq=min(seq // 2, ATTN_Q_TILE)),
        grid=(steps, n_seq),
        in_specs=[pl.BlockSpec((seq, width), lambda p, b: (rb + b, p)),
                  pl.BlockSpec((seq, width), lambda p, b: (rb + b, steps + p)),
                  pl.BlockSpec((seq, width), lambda p, b: (rb + b, 2 * steps + p)),
                  pl.BlockSpec((None, None, past, width), lambda p, b: (b, layer, 0, p)),
                  pl.BlockSpec((None, None, past, width), lambda p, b: (b, layer, 0, p)),
                  pl.BlockSpec((None, 2 * ATTN_PAIRS_PER_STEP, 2, seq // 2, slab),
                               lambda p, b: (layer, p, 0, 0, 0))],
        out_specs=pl.BlockSpec((seq, width), lambda p, b: (b, p)),
        out_shape=jax.ShapeDtypeStruct((n_seq * seq, NA_WIDTH), BF16),
        scratch_shapes=[pltpu.VMEM((seq, width), BF16), pltpu.VMEM((seq, width), BF16)],
        compiler_params=_params(2),
        name="attn_lat",
    )(z_qkv, z_qkv, z_qkv, cache_k, cache_v, bias)


def _neighbourhood_bias(rpb, seq):
    rows = seq // GRID_W
    kh = WIN_H
    assert rows >= WIN_H
    lead = rpb.shape[:-2]
    c = np.arange(GRID_W)
    q_cs = np.clip(c - WIN_W // 2, 0, GRID_W - WIN_W)
    col_ok = (c[None, :] >= q_cs[:, None]) & (c[None, :] < q_cs[:, None] + WIN_W)
    r = np.arange(rows)
    rs = np.clip(r - kh // 2, 0, rows - kh)
    base = np.where(r < rows // 2, 0, rows - KEY_SLAB_ROWS)
    assert (rs >= base).all() and (rs + kh <= base + KEY_SLAB_ROWS).all()
    pick_c = (c[None, None, :] - c[None, :, None] + WIN_W - 1
              == np.arange(2 * WIN_W - 1)[:, None, None]).astype(np.float32)
    w = jnp.einsum("...ij,jqk->...qik", rpb, pick_c, precision=lax.Precision.HIGHEST)
    w = jnp.where(jnp.asarray(col_ok)[:, None, :], w, NEG_INF)
    blocks = []
    for rq in range(rows):
        lo = int(rs[rq]) - rq + WIN_H - 1
        slab = w[..., lo:lo + kh, :].reshape(lead + (GRID_W, kh * GRID_W))
        left = int(rs[rq] - base[rq])
        pad = ((0, 0),) * (len(lead) + 1) + ((left * GRID_W, (KEY_SLAB_ROWS - kh - left) * GRID_W),)
        blocks.append(jnp.pad(slab, pad, constant_values=NEG_INF))
    return jnp.stack(blocks, axis=-3).reshape(lead + (2, seq // 2, KEY_SLAB_ROWS * GRID_W))


def _fourier_kernel(u_ref, ct2_ref, cc_ref, sc_ref, o_ref, pq_ref):
    t = u_ref.shape[0]
    for g in range(FOURIER_GROUPS):
        cols = slice(FOURIER_GROUP_DIM * g, FOURIER_GROUP_DIM * (g + 1))
        ug = u_ref[:, cols].astype(BF16)
        pq_ref[0:t, cols] = jnp.dot(ug, cc_ref[...], preferred_element_type=F32).astype(BF16)
        pq_ref[t:2 * t, cols] = jnp.dot(ug, sc_ref[...], preferred_element_type=F32).astype(BF16)
    o_ref[...] = jnp.dot(ct2_ref[...], pq_ref[...], preferred_element_type=F32).astype(o_ref.dtype)


def _dft_tables(t):
    def cs(n):
        k = np.arange(n, dtype=np.int64)
        ang = 2.0 * np.pi * ((k[:, None] * k[None, :]) % n).astype(np.float64) / n
        return np.cos(ang) / np.sqrt(n), np.sin(ang) / np.sqrt(n)

    ct, st = cs(t)
    cc, sc = cs(FOURIER_GROUP_DIM)
    ct2 = np.concatenate([ct, -st], axis=1).astype(np.float32)
    return (jnp.asarray(ct2).astype(BF16), jnp.asarray(cc.astype(np.float32)).astype(BF16),
            jnp.asarray(sc.astype(np.float32)).astype(BF16))


def _fourier(z_qkvu, row0, n_seq, seq):
    ct2, cc, sc = _dft_tables(seq)
    width = FOURIER_GROUPS * FOURIER_GROUP_DIM
    rb = row0 // seq
    ucol = 3 * NA_WIDTH // width
    return pl.pallas_call(
        _fourier_kernel,
        grid=(n_seq,),
        in_specs=[pl.BlockSpec((seq, width), lambda b: (rb + b, ucol)),
                  pl.BlockSpec((seq, 2 * seq), lambda b: (0, 0)),
                  pl.BlockSpec((FOURIER_GROUP_DIM, FOURIER_GROUP_DIM), lambda b: (0, 0)),
                  pl.BlockSpec((FOURIER_GROUP_DIM, FOURIER_GROUP_DIM), lambda b: (0, 0))],
        out_specs=pl.BlockSpec((seq, width), lambda b: (b, 0)),
        out_shape=jax.ShapeDtypeStruct((n_seq * seq, width), BF16),
        scratch_shapes=[pltpu.VMEM((2 * seq, width), BF16)],
        compiler_params=_params(1),
        name="fourier",
    )(z_qkvu, ct2, cc, sc)


def _rotary_tables(t):
    pos = np.arange(t)
    row = (pos // GRID_W).astype(np.float64)
    col = (pos % GRID_W).astype(np.float64)
    nf = RET_KEY_DIM // 4
    inv_freq = ROPE_BASE ** (-np.arange(nf, dtype=np.float64) / nf)
    ar = row[:, None] * inv_freq[None]
    ac = col[:, None] * inv_freq[None]
    cos = np.concatenate([np.cos(ar), np.cos(ar), np.cos(ac), np.cos(ac)], axis=1)
    sin = np.concatenate([-np.sin(ar), np.sin(ar), -np.sin(ac), np.sin(ac)], axis=1)
    return jnp.asarray(cos.astype(np.float32)), jnp.asarray(sin.astype(np.float32))


def _ret_kernel(lgf_ref, lgb_ref, *refs, t, tq, layer, rotary, state_in, state_out):
    refs = list(refs)
    q_ref, k_ref, v_ref, g_ref = refs[:4]
    refs = refs[4:]
    if rotary:
        cos_ref, sin_ref = refs[:2]
        refs = refs[2:]
    if state_in:
        sf0_ref, sb0_ref = refs[:2]
        refs = refs[2:]
    o_ref = refs[0]
    refs = refs[1:]
    if state_out:
        sf_ref, sb_ref = refs[:2]
        refs = refs[2:]
    dec_ref, kb_ref = refs

    scale = RET_KEY_DIM ** -0.5
    nq = t // tq
    heads_here = dec_ref.shape[0]

    if rotary:
        lane = lax.broadcasted_iota(I32, (1, LANES), 1)
        low = (lane % (RET_KEY_DIM // 2)) < (RET_KEY_DIM // 4)

        def rot(x, rows):
            swapped = jnp.where(low, pltpu.roll(x, LANES - RET_KEY_DIM // 4, 1), pltpu.roll(x, RET_KEY_DIM // 4, 1))
            return x * cos_ref[rows, :] + swapped * sin_ref[rows, :]
    else:
        def rot(x, rows):
            return x

    def one_head(hh):
        h = pl.program_id(0) * heads_here + hh
        cols = slice(LANES * hh, LANES * (hh + 1))
        lgf = lgf_ref[layer * RET_HEADS + h]
        lgb = lgb_ref[layer * RET_HEADS + h]

        @pl.when(pl.program_id(1) == 0)
        def _():
            def fill(ri, c):
                rows = pl.ds(pl.multiple_of(ri * tq, tq), tq)
                i = lax.broadcasted_iota(I32, (tq, t), 0) + ri * tq
                j = lax.broadcasted_iota(I32, (tq, t), 1)
                d = (i - j).astype(F32)
                m = jnp.exp(jnp.abs(d) * jnp.where(d > 0, lgf, lgb))
                dec_ref[hh, rows, :] = jnp.where(d == 0, 2.0, m)
                return c

            lax.fori_loop(0, nq, fill, 0)

        kr = rot(k_ref[:, cols].astype(F32), slice(0, t))
        kb_ref[hh] = kr.astype(BF16)
        vb = v_ref[:, cols]

        if state_out:
            j = lax.broadcasted_iota(I32, (t, 1), 0).astype(F32)
            tn = (((0,), (0,)), ((), ()))
            kf = (kr * (scale * jnp.exp(lgf * (t - 1.0 - j)))).astype(BF16)
            kbw = (kr * (scale * jnp.exp(lgb * j))).astype(BF16)
            sf = lax.dot_general(kf, vb, tn, preferred_element_type=F32)
            sb = lax.dot_general(kbw, vb, tn, preferred_element_type=F32)
            if state_in:
                sf = sf + jnp.exp(lgf * t) * sf0_ref[hh]
                sb = sb + jnp.exp(lgb * t) * sb0_ref[hh]
            sf_ref[hh] = sf
            sb_ref[hh] = sb

        def q_tile(qi, carry):
            r0 = pl.multiple_of(qi * tq, tq)
            rows = pl.ds(r0, tq)
            qr = rot(q_ref[rows, cols].astype(F32), rows)
            s = lax.dot_general((qr * scale).astype(BF16), kb_ref[hh], (((1,), (1,)), ((), ())),
                                preferred_element_type=F32)
            y = jnp.dot((s * dec_ref[hh, rows, :]).astype(BF16), vb, preferred_element_type=F32)
            if state_in:
                pos = (lax.broadcasted_iota(I32, (tq, 1), 0) + r0).astype(F32)
                qf = (qr * jnp.exp(lgf * (pos + 1.0))).astype(BF16)
                qb = (qr * jnp.exp(lgb * (t - pos))).astype(BF16)
                y = (y + jnp.dot(qf, sf0_ref[hh].astype(BF16), preferred_element_type=F32)
                     + jnp.dot(qb, sb0_ref[hh].astype(BF16), preferred_element_type=F32))
            mean = jnp.mean(y, axis=-1, keepdims=True)
            yc = y - mean
            yn = yc * lax.rsqrt(jnp.mean(yc * yc, axis=-1, keepdims=True) + EPS)
            g = g_ref[rows, cols].astype(F32)
            o_ref[rows, cols] = (g * jax.nn.sigmoid(g) * yn).astype(o_ref.dtype)
            return carry

        lax.fori_loop(0, nq, q_tile, 0)

    for hh in range(heads_here):
        one_head(hh)


def _retention(z_ret, lg_f, lg_b, layer, row0, n_seq, seq, *, rotary, states=None, state_out=False):
    rb = row0 // seq
    tq = min(seq, RET_Q_TILE)
    state_in = states is not None
    hps = RET_HEADS if seq * seq * RET_HEADS * 4 <= RET_DECAY_BYTES else 1
    cb = BRANCH_WIDTH // (hps * LANES)
    width = hps * LANES
    in_specs = [pl.BlockSpec((seq, width), lambda h, b, *_: (rb + b, 0 * cb + h)),
                pl.BlockSpec((seq, width), lambda h, b, *_: (rb + b, 1 * cb + h)),
                pl.BlockSpec((seq, width), lambda h, b, *_: (rb + b, 2 * cb + h)),
                pl.BlockSpec((seq, width), lambda h, b, *_: (rb + b, 3 * cb + h))]
    args = [z_ret, z_ret, z_ret, z_ret]
    if rotary:
        cos, sin = _rotary_tables(seq)
        in_specs += [pl.BlockSpec((seq, LANES), lambda h, b, *_: (0, 0))] * 2
        args += [cos, sin]
    if state_in:
        st_spec = pl.BlockSpec((None, None, hps, RET_KEY_DIM, RET_KEY_DIM), lambda h, b, *_: (b, layer, h, 0, 0))
        in_specs += [st_spec, st_spec]
        args += list(states)
    out_specs = [pl.BlockSpec((seq, width), lambda h, b, *_: (b, h))]
    out_shape = [jax.ShapeDtypeStruct((n_seq * seq, RET_HEADS * LANES), BF16)]
    if state_out:
        so = pl.BlockSpec((None, hps, RET_KEY_DIM, RET_KEY_DIM), lambda h, b, *_: (b, h, 0, 0))
        out_specs += [so, so]
        out_shape += [jax.ShapeDtypeStruct((n_seq, RET_HEADS, RET_KEY_DIM, RET_KEY_DIM), F32)] * 2
    return pl.pallas_call(
        functools.partial(_ret_kernel, t=seq, tq=tq, layer=layer, rotary=rotary, state_in=state_in,
                          state_out=state_out),
        grid_spec=pltpu.PrefetchScalarGridSpec(
            num_scalar_prefetch=2, grid=(RET_HEADS // hps, n_seq),
            in_specs=in_specs, out_specs=out_specs,
            scratch_shapes=[pltpu.VMEM((hps, seq, seq), F32), pltpu.VMEM((hps, seq, LANES), BF16)]),
        out_shape=out_shape,
        compiler_params=_params(2),
        name="retention",
    )(lg_f, lg_b, *args)


def _split_dot_nt(w, x):
    nt = (((1,), (1,)), ((), ()))
    w_hi = w.astype(BF16)
    w_lo = (w - w_hi.astype(F32)).astype(BF16)
    x_hi = x.astype(BF16)
    x_lo = (x - x_hi.astype(F32)).astype(BF16)
    return (lax.dot_general(w_hi, x_hi, nt, preferred_element_type=F32)
            + lax.dot_general(w_hi, x_lo, nt, preferred_element_type=F32)
            + lax.dot_general(w_lo, x_hi, nt, preferred_element_type=F32))


def _finish_kernel(tmod_ref, *refs, ctx_tiles, n_x):
    del tmod_ref
    branch_refs, refs = refs[:6], refs[6:]
    zg_ref, refs = refs[0], refs[1:]
    x_refs, refs = refs[:n_x], refs[n_x:]
    (mod_ref, gpost_ref, gpre_ref, wb_ref, wo_ref, wrt_ref, br_ref, tri_ref, ltri_ref,
     x1_ref, h2_ref, wts_ref, lpos_ref, seg_ref, off_ref) = refs
    d = D_MODEL

    def branch(j):
        return _load_rows(branch_refs[2 * j:2 * j + 2], ctx_tiles)

    def gate(j):
        return jax.nn.sigmoid(zg_ref[:, d * j:d * (j + 1)].astype(F32))

    merged = (gate(0) * jnp.dot(branch(0), wb_ref[0], preferred_element_type=F32)
              + gate(1) * jnp.dot(branch(1), wb_ref[1], preferred_element_type=F32)
              + gate(2) * jnp.dot(branch(2), wb_ref[2], preferred_element_type=F32))
    y = jnp.dot(merged.astype(BF16), wo_ref[...], preferred_element_type=F32)
    x1 = _load_rows(x_refs, ctx_tiles) + mod_ref[2:3, :] * (y * _rms(y) * gpost_ref[...])
    x1_ref[...] = x1
    h2 = x1 * _rms(x1) * gpre_ref[...] * (1.0 + mod_ref[4:5, :]) + mod_ref[3:4, :]
    h2_ref[...] = h2

    logits = _split_dot_nt(wrt_ref[...], h2) + br_ref[:, 0:1]
    tm = logits.shape[1]
    eidx = lax.broadcasted_iota(I32, (N_EXPERTS, tm), 0)
    cur = logits
    vals, hots = [], []
    for k in range(TOP_K):
        m = jnp.max(cur, axis=0, keepdims=True)
        sel = jnp.min(jnp.where(cur == m, eidx, N_EXPERTS), axis=0, keepdims=True)
        hot = eidx == sel
        vals.append(m)
        hots.append(hot)
        cur = jnp.where(hot, -jnp.inf, cur)
    exps = [jnp.exp(v - vals[0]) for v in vals]
    den = exps[0] + exps[1] + exps[2] + exps[3]
    for k in range(TOP_K):
        wts_ref[k:k + 1, :] = exps[k] / den

    member = jnp.logical_or(jnp.logical_or(hots[0], hots[1]), jnp.logical_or(hots[2], hots[3]))
    member_f = member.astype(F32)
    before = jnp.dot(member_f.astype(BF16), tri_ref[...], preferred_element_type=F32)
    units = jnp.ceil(jnp.sum(member_f, axis=1, keepdims=True) * (1.0 / SEG_ALIGN))
    units = jnp.broadcast_to(units, seg_ref.shape)
    off = jnp.dot(ltri_ref[...], units.astype(BF16), preferred_element_type=F32) * SEG_ALIGN
    seg_ref[...] = units * SEG_ALIGN
    off_ref[...] = off
    place = before + off[:, 0:1]
    for k in range(TOP_K):
        lpos_ref[k:k + 1, :] = jnp.sum(jnp.where(hots[k], place, 0.0), axis=0, keepdims=True).astype(I32)


def _finish(branches, z_gate, x, mod, g_post, g_pre_ffn, w_branch, w_out, w_router_t, b_router, layer, tile_mod):
    tm = FIN_TILE
    x_specs, x_args, _, n = _token_rows(x, tm)
    d = x_args[0].shape[1]
    tri = jnp.asarray(np.triu(np.ones((tm, tm), np.float32), k=1)).astype(BF16)
    ltri = jnp.asarray(np.tril(np.ones((N_EXPERTS, N_EXPERTS), np.float32), k=-1)).astype(BF16)
    row = lambda i, t: (i, 0)
    const2 = lambda i, t: (0, 0)
    lay3 = lambda i, t: (layer, 0, 0)
    col = lambda i, t: (0, i)
    branch_specs, branch_args = [], []
    for pair in branches:
        specs, args, ctx_tiles, _ = _token_rows(pair, tm)
        branch_specs += specs
        branch_args += args
    outs = pl.pallas_call(
        functools.partial(_finish_kernel, ctx_tiles=ctx_tiles, n_x=len(x_args)),
        grid_spec=pltpu.PrefetchScalarGridSpec(
            num_scalar_prefetch=1, grid=(n // tm,),
            in_specs=branch_specs + [pl.BlockSpec((tm, GATE_W), row)] + x_specs + [
                      pl.BlockSpec((None, None, 6, d), lambda i, t: (layer, t[i], 0, 0)),
                      pl.BlockSpec((None, 1, d), lay3),
                      pl.BlockSpec((None, 1, d), lay3),
                      pl.BlockSpec((None, 3, BRANCH_WIDTH, d), lambda i, t: (layer, 0, 0, 0)),
                      pl.BlockSpec((None, d, d), lay3),
                      pl.BlockSpec((None, N_EXPERTS, d), lay3),
                      pl.BlockSpec((None, N_EXPERTS, LANES), lay3),
                      pl.BlockSpec((tm, tm), const2),
                      pl.BlockSpec((N_EXPERTS, N_EXPERTS), const2)],
            out_specs=[pl.BlockSpec((tm, d), row),
                       pl.BlockSpec((tm, d), row),
                       pl.BlockSpec((TOP_K, tm), col),
                       pl.BlockSpec((TOP_K, tm), col),
                       pl.BlockSpec((None, N_EXPERTS, LANES), lambda i, t: (i, 0, 0)),
                       pl.BlockSpec((None, N_EXPERTS, LANES), lambda i, t: (i, 0, 0))]),
        out_shape=[jax.ShapeDtypeStruct((n, d), F32),
                   jax.ShapeDtypeStruct((n, d), F32),
                   jax.ShapeDtypeStruct((TOP_K, n), F32),
                   jax.ShapeDtypeStruct((TOP_K, n), I32),
                   jax.ShapeDtypeStruct((n // tm, N_EXPERTS, LANES), F32),
                   jax.ShapeDtypeStruct((n // tm, N_EXPERTS, LANES), F32)],
        compiler_params=_params(1),
        name="merge_router",
    )(tile_mod, *branch_args, z_gate, *x_args, mod, g_post, g_pre_ffn, w_branch, w_out, w_router_t, b_router,
      tri, ltri)
    return outs


def _segment_chunks(length, src_ref, src0, dst_ref, dst0, sem, max_chunk, fixed_src=False):
    out = []
    chunk = max_chunk
    while chunk >= SEG_ALIGN:
        done = jnp.bitwise_and(length, ~(2 * chunk - 1))
        present = jnp.bitwise_and(length, chunk) != 0
        s = 0 if fixed_src else pl.multiple_of(src0 + done, SEG_ALIGN)
        dd = pl.multiple_of(dst0 + done, SEG_ALIGN)
        out.append((present, pltpu.make_async_copy(src_ref.at[pl.ds(s, chunk)], dst_ref.at[pl.ds(dd, chunk)], sem)))
        chunk //= 2
    return out


def _for_each_chunk(n_segments, chunks_of, action):
    def body(e, c):
        for present, cp in chunks_of(e):
            pl.when(present)(functools.partial(action, cp))
        return c

    lax.fori_loop(0, n_segments, body, 0, unroll=4)


def _start(cp):
    cp.start()


def _wait(cp):
    cp.wait()


def _wait_rows(total, src_ref, dst_ref, sem):
    chunk = pl.next_power_of_2(SORT_ROWS) // 2
    while chunk >= SEG_ALIGN:
        @pl.when(jnp.bitwise_and(total, chunk) != 0)
        def _(chunk=chunk):
            pltpu.make_async_copy(src_ref.at[pl.ds(0, chunk)], dst_ref.at[pl.ds(0, chunk)], sem).wait()
        chunk //= 2


def _offset_in_chunk(rows, c0):
    assert SORT_CHUNK == 256
    inside = lax.shift_right_logical(rows, 8) == c0 // SORT_CHUNK
    return jnp.where(inside, jnp.bitwise_and(rows, SORT_CHUNK - 1), -1).astype(BF16)


def _scatter_kernel(seg_ref, off_ref, pos_ref, used_ref, tpos_ref, tlen_ref, end_ref, h_ref, lpos_ref, xe_hbm,
                    buf_ref, zero_ref, sem, *, tm):
    t = pl.program_id(0)
    slot = t % 2

    zero_rows = zero_ref.shape[0]
    first_unused = end_ref[0]
    n_fill = lax.div(xe_hbm.shape[0] - first_unused, zero_rows)

    def fill_copy(j):
        dst = pl.multiple_of(first_unused + j * zero_rows, zero_rows)
        return pltpu.make_async_copy(zero_ref, xe_hbm.at[pl.ds(dst, zero_rows)], sem.at[2])

    def fill_start(j, c):
        fill_copy(j).start()
        return c

    def fill_wait(j, c):
        fill_copy(j).wait()
        return c

    @pl.when(t == 0)
    def _():
        zero_ref[...] = jnp.zeros_like(zero_ref)
        lax.fori_loop(0, n_fill, fill_start, 0)

    hb = h_ref[...].astype(BF16)
    lp = [lpos_ref[k:k + 1, :] for k in range(TOP_K)]
    rows = buf_ref.shape[1]
    r_off = lax.broadcasted_iota(I32, (SORT_CHUNK, tm), 0).astype(BF16)
    one = jnp.ones((SORT_CHUNK, tm), BF16)
    zero = jnp.zeros((SORT_CHUNK, tm), BF16)
    for c0 in range(0, rows, SORT_CHUNK):
        here = [_offset_in_chunk(p, c0) for p in lp]
        hit = jnp.logical_or(jnp.logical_or(r_off == here[0], r_off == here[1]),
                             jnp.logical_or(r_off == here[2], r_off == here[3]))
        onehot = jnp.where(hit, one, zero)
        buf_ref[slot, c0:c0 + SORT_CHUNK, :] = jnp.dot(onehot, hb, preferred_element_type=F32)

    def segments_of(tile):
        def segment(e):
            j = tile * N_EXPERTS + e
            return _segment_chunks(seg_ref[j], buf_ref.at[tile % 2], off_ref[j], xe_hbm, pos_ref[j],
                                   sem.at[tile % 2], tm)
        return segment

    @pl.when(t > 0)
    def _():
        _wait_rows(used_ref[t - 1], buf_ref.at[1 - slot], xe_hbm, sem.at[1 - slot])

    _for_each_chunk(N_EXPERTS, segments_of(t), _start)

    @pl.when(t == pl.num_programs(0) - 1)
    def _():
        _wait_rows(used_ref[t], buf_ref.at[slot], xe_hbm, sem.at[slot])

        def tail(e):
            return _segment_chunks(tlen_ref[e], zero_ref, 0, xe_hbm, tpos_ref[e], sem.at[0], zero_ref.shape[0],
                                   fixed_src=True)

        _for_each_chunk(N_EXPERTS, tail, _start)
        _for_each_chunk(N_EXPERTS, tail, _wait)
        lax.fori_loop(0, n_fill, fill_wait, 0)


def _scatter_rows(h2, lpos, plan, n_rows):
    n, d = h2.shape
    tm = FIN_TILE
    return pl.pallas_call(
        functools.partial(_scatter_kernel, tm=tm),
        grid_spec=pltpu.PrefetchScalarGridSpec(
            num_scalar_prefetch=7, grid=(n // tm,),
            in_specs=[pl.BlockSpec((tm, d), lambda i, *_: (i, 0)),
                      pl.BlockSpec((TOP_K, tm), lambda i, *_: (0, i))],
            out_specs=pl.BlockSpec(memory_space=pl.ANY),
            scratch_shapes=[pltpu.VMEM((2, SORT_ROWS, d), F32), pltpu.VMEM((MOE_BLOCK // 2, d), F32),
                            pltpu.SemaphoreType.DMA((3,))]),
        out_shape=jax.ShapeDtypeStruct((n_rows, d), F32),
        compiler_params=_params(1),
        name="moe_scatter",
    )(plan["seg"], plan["off"], plan["pos"], plan["used"], plan["tail_pos"], plan["tail_len"], plan["end"], h2, lpos)


def _expert_kernel(be_ref, nxt_ref, valid_ref, nu_ref, xb_ref, w1_hbm, b1_ref, w2_hbm, b2_ref, yb_ref,
                   w1s_ref, w2s_ref, w1b_ref, w2b_ref, sem, *, layer):
    i = pl.program_id(0)
    used = i < nu_ref[0]
    fresh = jnp.logical_or(i == 0, be_ref[i] != be_ref[jnp.maximum(i - 1, 0)])

    def fetch(e):
        return (pltpu.make_async_copy(w1_hbm.at[layer, e], w1s_ref, sem.at[0]),
                pltpu.make_async_copy(w2_hbm.at[layer, e], w2s_ref, sem.at[1]))

    @pl.when(i == 0)
    def _():
        for cp in fetch(be_ref[0]):
            cp.start()

    @pl.when(jnp.logical_and(used, fresh))
    def _():
        for cp in fetch(be_ref[i]):
            cp.wait()
        w1b_ref[...] = w1s_ref[...].astype(BF16)
        w2b_ref[...] = w2s_ref[...].astype(BF16)

        @pl.when(nxt_ref[i] >= 0)
        def _():
            for cp in fetch(nxt_ref[i]):
                cp.start()

    def ffn(m):
        e = be_ref[i]
        z = jnp.dot(xb_ref[0:m, :].astype(BF16), w1b_ref[...], preferred_element_type=F32) + b1_ref[e]
        glu = jnp.minimum(z[:, :D_FF], SWIGLU_LIMIT)
        lin = jnp.clip(z[:, D_FF:], -SWIGLU_LIMIT, SWIGLU_LIMIT)
        act = glu * jax.nn.sigmoid(SWIGLU_ALPHA * glu) * (lin + 1.0)
        yb_ref[0:m, :] = jnp.dot(act.astype(BF16), w2b_ref[...], preferred_element_type=F32) + b2_ref[e]
        if m < yb_ref.shape[0]:
            yb_ref[m:, :] = jnp.zeros((yb_ref.shape[0] - m, yb_ref.shape[1]), F32)

    valid = valid_ref[i]
    for m in range(EXPERT_ROW_STEP, yb_ref.shape[0] + 1, EXPERT_ROW_STEP):
        pl.when(jnp.logical_and(valid > m - EXPERT_ROW_STEP, valid <= m))(functools.partial(ffn, m))

    @pl.when(valid == 0)
    def _():
        yb_ref[...] = jnp.zeros_like(yb_ref)


def _experts(xb, block_expert, next_expert, valid_rows, n_used, w1, b1, w2, b2, layer):
    n_rows, d = xb.shape
    tm = MOE_BLOCK
    return pl.pallas_call(
        functools.partial(_expert_kernel, layer=layer),
        grid_spec=pltpu.PrefetchScalarGridSpec(
            num_scalar_prefetch=4, grid=(n_rows // tm,),
            in_specs=[pl.BlockSpec((tm, d), lambda i, be, nx, vr, nu: (jnp.minimum(i, nu[0] - 1), 0)),
                      pl.BlockSpec(memory_space=pl.ANY),
                      pl.BlockSpec((None, N_EXPERTS, 1, 2 * D_FF), lambda i, be, nx, vr, nu: (layer, 0, 0, 0)),
                      pl.BlockSpec(memory_space=pl.ANY),
                      pl.BlockSpec((None, N_EXPERTS, 1, d), lambda i, be, nx, vr, nu: (layer, 0, 0, 0))],
            out_specs=pl.BlockSpec((tm, d), lambda i, be, nx, vr, nu: (i, 0)),
            scratch_shapes=[pltpu.VMEM((d, 2 * D_FF), F32), pltpu.VMEM((D_FF, d), F32),
                            pltpu.VMEM((d, 2 * D_FF), BF16), pltpu.VMEM((D_FF, d), BF16),
                            pltpu.SemaphoreType.DMA((2,))]),
        out_shape=jax.ShapeDtypeStruct((n_rows, d), F32),
        compiler_params=_params(1),
        name="moe_experts",
    )(block_expert, next_expert, valid_rows, n_used, xb, w1, b1, w2, b2)


def _gather_kernel(tmod_ref, seg_ref, off_ref, pos_ref, used_ref, ye_hbm, lpos_ref, wts_ref, x1_ref, mod_ref, g_ref,
                   *refs, tm, tile0, prenorm_next):
    del tmod_ref
    if prenorm_next:
        modn_ref, gn_ref, o_ref, hn_ref, buf_ref, sem = refs
    else:
        o_ref, buf_ref, sem = refs
    step = pl.program_id(0)
    t = step + tile0
    slot = step % 2

    def segments_of(tile):
        def segment(e):
            j = tile * N_EXPERTS + e
            half = (tile - tile0) % 2
            return _segment_chunks(seg_ref[j], ye_hbm, pos_ref[j], buf_ref.at[half], off_ref[j], sem.at[half], tm)
        return segment

    @pl.when(step == 0)
    def _():
        buf_ref[...] = jnp.zeros_like(buf_ref)
        _for_each_chunk(N_EXPERTS, segments_of(t), _start)

    @pl.when(step + 1 < pl.num_programs(0))
    def _():
        _for_each_chunk(N_EXPERTS, segments_of(t + 1), _start)

    _wait_rows(used_ref[t], ye_hbm, buf_ref.at[slot], sem.at[slot])

    lp = [lpos_ref[:, k:k + 1] for k in range(TOP_K)]
    wt = [jnp.broadcast_to(wts_ref[:, k:k + 1].astype(BF16), (tm, SORT_CHUNK)) for k in range(TOP_K)]
    c_off = lax.broadcasted_iota(I32, (tm, SORT_CHUNK), 1).astype(BF16)
    y = jnp.zeros(o_ref.shape, F32)
    for c0 in range(0, buf_ref.shape[1], SORT_CHUNK):
        wm = jnp.zeros((tm, SORT_CHUNK), BF16)
        for k in range(TOP_K):
            wm = jnp.where(c_off == _offset_in_chunk(lp[k], c0), wt[k], wm)
        y = y + jnp.dot(wm, buf_ref[slot, c0:c0 + SORT_CHUNK, :].astype(BF16), preferred_element_type=F32)
    x2 = x1_ref[...] + mod_ref[5:6, :] * (y * _rms(y) * g_ref[...])
    o_ref[...] = x2
    if prenorm_next:
        hn = x2 * _rms(x2) * gn_ref[...]
        hn_ref[...] = (hn * (1.0 + modn_ref[1:2, :]) + modn_ref[0:1, :]).astype(hn_ref.dtype)


def _gather_combine(ye, lpos_t, wts_t, plan, x1, mod, g_post, layer, tile_mod, row0=0, rows=None, g_pre_next=None):
    n, d = x1.shape
    tm = FIN_TILE
    rows = n if rows is None else rows
    tile0 = row0 // tm
    row = lambda i, *_: (tile0 + i, 0)
    out_row = lambda i, *_: (i, 0)
    mod_spec = lambda lay: pl.BlockSpec((None, None, 6, d), lambda i, t, *_: (lay, t[tile0 + i], 0, 0))
    gain_spec = lambda lay: pl.BlockSpec((None, 1, d), lambda i, *_: (lay, 0, 0))
    prenorm_next = g_pre_next is not None
    in_specs = [pl.BlockSpec(memory_space=pl.ANY),
                pl.BlockSpec((tm, TOP_K), row),
                pl.BlockSpec((tm, TOP_K), row),
                pl.BlockSpec((tm, d), row),
                mod_spec(layer), gain_spec(layer)]
    args = [ye, lpos_t, wts_t, x1, mod, g_post]
    out_specs = [pl.BlockSpec((tm, d), out_row)]
    out_shape = [jax.ShapeDtypeStruct((rows, d), F32)]
    if prenorm_next:
        in_specs += [mod_spec(layer + 1), gain_spec(layer + 1)]
        args += [mod, g_pre_next]
        out_specs += [pl.BlockSpec((tm, d), out_row)]
        out_shape += [jax.ShapeDtypeStruct((rows, d), BF16)]
    outs = pl.pallas_call(
        functools.partial(_gather_kernel, tm=tm, tile0=tile0, prenorm_next=prenorm_next),
        grid_spec=pltpu.PrefetchScalarGridSpec(
            num_scalar_prefetch=5, grid=(rows // tm,),
            in_specs=in_specs, out_specs=out_specs,
            scratch_shapes=[pltpu.VMEM((2, SORT_ROWS, d), F32), pltpu.SemaphoreType.DMA((2,))]),
        out_shape=out_shape,
        compiler_params=_params(1),
        name="moe_gather",
    )(tile_mod, plan["seg"], plan["off"], plan["pos"], plan["used"], *args)
    return outs if prenorm_next else outs[0]


def _moe(h2, wts, lpos, seg, off, x1, mod, g_post, w1, b1, w2, b2, layer, tile_mod, split_rows=None,
         g_pre_next=None):
    n, d = h2.shape
    blk = MOE_BLOCK
    tm = FIN_TILE
    tiles = n // tm
    n_rows = -(-(n * TOP_K + tiles * N_EXPERTS * (SEG_ALIGN - 1) + N_EXPERTS * (blk - 1)) // blk) * blk
    n_blocks = n_rows // blk
    seg = seg[:, :, 0].astype(I32)
    off = off[:, :, 0].astype(I32)
    rows_e = jnp.sum(seg, axis=0)
    region = (rows_e + blk - 1) // blk * blk
    pend = jnp.cumsum(region)
    pstart = pend - region
    pos = pstart[None, :] + jnp.cumsum(seg, axis=0) - seg
    plan = {"seg": seg.reshape(-1), "off": off.reshape(-1), "pos": pos.reshape(-1).astype(I32),
            "used": jnp.sum(seg, axis=1).astype(I32),
            "tail_pos": (pstart + rows_e).astype(I32), "tail_len": (region - rows_e).astype(I32),
            "end": pend[-1:].astype(I32)}
    blocks = jnp.arange(n_blocks, dtype=I32) * blk
    block_expert = jnp.minimum(jnp.sum(blocks[:, None] >= pend[None, :], axis=1), N_EXPERTS - 1).astype(I32)
    n_used = (pend[-1:] // blk).astype(I32)
    valid_rows = jnp.clip((pstart + rows_e)[block_expert] - blocks, 0, blk).astype(I32)
    ids = jnp.arange(n_blocks, dtype=I32)
    run_start = jnp.logical_and(jnp.concatenate([jnp.ones((1,), bool), block_expert[1:] != block_expert[:-1]]),
                                ids < n_used[0])
    first_after = lax.cummin(jnp.where(run_start, ids, n_blocks)[::-1])[::-1]
    first_after = jnp.concatenate([first_after[1:], jnp.full((1,), n_blocks, I32)])
    next_expert = jnp.where(first_after < n_blocks, block_expert[jnp.minimum(first_after, n_blocks - 1)], -1)
    xe = _scatter_rows(h2, lpos, plan, n_rows)
    ye = _experts(xe, block_expert, next_expert.astype(I32), valid_rows, n_used, w1, b1, w2, b2, layer)
    combine = functools.partial(_gather_combine, ye, lpos.T, wts.T, plan, x1, mod, g_post, layer, tile_mod)
    if split_rows is None:
        return combine(g_pre_next=g_pre_next)
    return combine(row0=0, rows=split_rows), combine(row0=split_rows, rows=n - split_rows)


def _tile_mod_ids(n_ctx_rows, n_lat_rows, lat_seq, tm):
    ctx = np.zeros((n_ctx_rows // tm,), np.int32)
    lat = 1 + (np.arange(n_lat_rows // tm) * tm) // lat_seq
    return jnp.asarray(np.concatenate([ctx, lat.astype(np.int32)]))


def kernel(x_prompt, x_sample, cache_k, cache_v, state_ret_fwd, state_ret_bwd, c, c_ctx, w_mod, b_mod, g_pre_mix, g_post_mix, g_pre_ffn, g_post_ffn, w_in, na_rel_bias, ret_decay_fwd, ret_decay_bwd, w_branch, w_out, w_router, b_router, w_exp_in, b_exp_in, w_exp_out, b_exp_out):
    batch, seq, d = x_prompt.shape
    dec_batch, dec_seq, _ = x_sample.shape
    depth = w_in.shape[0]
    n_ctx = batch * seq
    n_lat = dec_batch * dec_seq
    assert 1 + dec_batch <= MOD_ROWS

    x = (x_prompt.reshape(n_ctx, d), x_sample.reshape(n_lat, d))
    cvec =jnp.concatenate([c_ctx[None], c, jnp.zeros((MOD_ROWS - 1 - dec_batch, d), F32)], axis=0)
    mod_all = _modulation(cvec, w_mod, b_mod).reshape(depth, MOD_ROWS, 6, d)
    tmod = {tm: _tile_mod_ids(n_ctx, n_lat, dec_seq, tm) for tm in (ROW_TILE, FIN_TILE)}
    past = cache_k.shape[2]
    ck = cache_k.reshape(dec_batch, depth, past, NA_WIDTH)
    cv = cache_v.reshape(dec_batch, depth, past, NA_WIDTH)
    lg_f = jax.nn.log_sigmoid(ret_decay_fwd.astype(F32)).reshape(-1)
    lg_b = jax.nn.log_sigmoid(ret_decay_bwd.astype(F32)).reshape(-1)
    bias_all = _neighbourhood_bias(na_rel_bias, dec_seq)
    vec = lambda g: g.reshape(depth, 1, d)
    g_pre_mix, g_post_mix, g_pre_ffn, g_post_ffn = vec(g_pre_mix), vec(g_post_mix), vec(g_pre_ffn), vec(g_post_ffn)
    w_branch_b = w_branch.astype(BF16)
    w_out_b = w_out.astype(BF16)
    w_router_t = jnp.swapaxes(w_router, 1, 2)
    b_router_l = jnp.broadcast_to(b_router[:, :, None], (depth, N_EXPERTS, LANES))
    b_exp_in = b_exp_in.reshape(depth, N_EXPERTS, 1, 2 * D_FF)
    b_exp_out = b_exp_out.reshape(depth, N_EXPERTS, 1, d)

    ks, vs, sfs, sbs = [], [], [], []
    h = _prenorm(x, g_pre_mix, mod_all, 0, tmod[ROW_TILE], ROW_TILE)
    for l in range(depth):
        last = l == depth - 1
        z_qkvu = _project(h, w_in, l, 0, QKVU_W, F32)
        z_ret = _project(h, w_in, l, QKVU_W, RET_W, BF16)
        z_gate = _project(h, w_in, l, QKVU_W + RET_W, GATE_W, BF16)
        ks.append(z_qkvu[:n_ctx, NA_WIDTH:2 * NA_WIDTH].reshape(batch, seq, NA_HEADS, NA_HEAD_DIM))
        vs.append(z_qkvu[:n_ctx, 2 * NA_WIDTH:3 * NA_WIDTH].reshape(batch, seq, NA_HEADS, NA_HEAD_DIM))

        a_pair = (_attention_ctx(z_qkvu, batch, seq),
                  _attention_lat(z_qkvu, ck, cv, bias_all, l, n_ctx, dec_batch, dec_seq))
        f_pair = (_fourier(z_qkvu, 0, batch, seq), _fourier(z_qkvu, n_ctx, dec_batch, dec_seq))
        r_ctx, s_f, s_b = _retention(z_ret, lg_f, lg_b, l, 0, batch, seq, rotary=False, state_out=True)
        (r_lat,) = _retention(z_ret, lg_f, lg_b, l, n_ctx, dec_batch, dec_seq, rotary=True,
                              states=(state_ret_fwd, state_ret_bwd))
        sfs.append(s_f)
        sbs.append(s_b)

        x1, h2, wts, lpos, seg, off = _finish(
            (a_pair, f_pair, (r_ctx, r_lat)), z_gate, x, mod_all, g_post_mix, g_pre_ffn,
            w_branch_b, w_out_b, w_router_t, b_router_l, l, tmod[FIN_TILE])
        out = _moe(h2, wts, lpos, seg, off, x1, mod_all, g_post_ffn,
                   w_exp_in, b_exp_in, w_exp_out, b_exp_out, l, tmod[FIN_TILE],
                   split_rows=n_ctx if last else None, g_pre_next=None if last else g_pre_mix)
        x, h = (out, None) if last else out

    y_prompt = x[0].reshape(batch, seq, d)
    y_sample = x[1].reshape(dec_batch, dec_seq, d)
    return (y_prompt, y_sample, jnp.stack(ks, axis=1), jnp.stack(vs, axis=1),
            jnp.stack(sfs, axis=1), jnp.stack(sbs, axis=1))
```
